```python
import math
import jax, jax.numpy as jnp
from jax import lax
import numpy as np

D_MODEL = 1024
BATCH = 8
SEQ = 16384
DEPTH = 1

MEM_LEN = 256
EPS = 1e-6

SSD_HEADS = 16
SSD_HEAD_DIM = 64
SSD_DIM = SSD_HEADS * SSD_HEAD_DIM
SSD_GROUPS = 2
SSD_HEADS_PER_GROUP = SSD_HEADS // SSD_GROUPS
SSD_STATE = 128
SSD_CONV = 4
SSD_CHUNK = 128
SSD_CONV_CH = SSD_DIM + 2 * SSD_GROUPS * SSD_STATE

HG_HEADS = 8
HG_K = 128
HG_V = 128
HG_KDIM = HG_HEADS * HG_K
HG_VDIM = HG_HEADS * HG_V
HG_CHUNK = 64

D_MIX = SSD_DIM + HG_VDIM
N_IN = SSD_DIM + SSD_CONV_CH + SSD_HEADS + 2 * HG_KDIM + 2 * HG_VDIM

XA_HEADS = 4
XA_HEAD_DIM = D_MODEL // XA_HEADS

FFN_DIM = -(-8 * D_MODEL // (3 * 256)) * 256

kernel_name = "hybrid_ssd_hgrn2_xattn_block"


def rmsnorm(x, w):
    xf = x.astype(jnp.float32)
    y = xf * lax.rsqrt(jnp.mean(xf * xf, axis=-1, keepdims=True) + EPS)
    return (y * w.astype(jnp.float32)).astype(x.dtype)


def causal_depthwise_conv(u, w, b):
    ch = u.shape[-1]
    out = lax.conv_general_dilated(
        u, w[:, None, :].astype(u.dtype), window_strides=(1,),
        padding=[(w.shape[0] - 1, 0)], dimension_numbers=("NWC", "WIO", "NWC"),
        feature_group_count=ch)
    return out + b.astype(u.dtype)


def ssd_mixer(xs, bm, cm, dt, a_log, d_skip):
    bsz, seqlen = xs.shape[0], xs.shape[1]
    nc, q = seqlen // SSD_CHUNK, SSD_CHUNK
    g, e, p, n = SSD_GROUPS, SSD_HEADS_PER_GROUP, SSD_HEAD_DIM, SSD_STATE
    a = -jnp.exp(a_log.astype(jnp.float32)).reshape(g, e)
    x_c = xs.reshape(bsz, nc, q, g, e, p)
    b_c = bm.reshape(bsz, nc, q, g, n)
    c_c = cm.reshape(bsz, nc, q, g, n)
    dt_c = dt.reshape(bsz, nc, q, g, e)
    acum = jnp.cumsum(jnp.moveaxis(dt_c * a, 2, -1), axis=-1)
    causal = jnp.tril(jnp.ones((q, q), dtype=bool))
    seg = acum[..., :, None] - acum[..., None, :]
    l_dec = jnp.exp(jnp.where(causal, seg, -jnp.inf))
    xdt = x_c * dt_c[..., None]
    cb = jnp.einsum("bclgn,bcsgn->bcgls", c_c, b_c)
    y_diag = jnp.einsum("bcgels,bcsgep->bclgep", cb[:, :, :, None] * l_dec, xdt)
    dec_to_end = jnp.moveaxis(jnp.exp(acum[..., -1:] - acum), -1, 2)
    states = jnp.einsum("bcsgn,bcsgep->bcgepn", b_c, xdt * dec_to_end[..., None])
    chunk_decay = jnp.exp(acum[..., -1])

    def step(s, inp):
        st, dec = inp
        return s * dec[..., None, None] + st, s

    s0 = jnp.zeros((bsz, g, e, p, n), jnp.float32)
    _, s_in = lax.scan(step, s0, (jnp.moveaxis(states, 1, 0).astype(jnp.float32),
                                  jnp.moveaxis(chunk_decay, 1, 0)))
    s_in = jnp.moveaxis(s_in, 0, 1)
    dec_from_start = jnp.moveaxis(jnp.exp(acum), -1, 2)
    y_off = jnp.einsum("bclgn,bcgepn->bclgep", c_c, s_in) * dec_from_start[..., None]
    y = (y_diag + y_off).reshape(bsz, seqlen, SSD_HEADS, p)
    return y + d_skip.astype(jnp.float32)[:, None] * xs


def hgrn2_mixer(q_raw, f_raw, i_val, lb):
    bsz, seqlen = q_raw.shape[0], q_raw.shape[1]
    nc, c = seqlen // HG_CHUNK, HG_CHUNK
    qf = jax.nn.silu(q_raw)
    fg = lb + (1.0 - lb) * jax.nn.sigmoid(f_raw.astype(jnp.float32))
    kf = 1.0 - fg
    gl = jnp.log(fg)

    def to_chunks(t):
        return t.reshape(bsz, nc, c, t.shape[2], t.shape[3]).transpose(1, 0, 3, 2, 4)

    causal = jnp.tril(jnp.ones((c, c), dtype=bool))[:, :, None]

    def step(s, inp):
        qc, kc, vc, gc = inp
        bcum = jnp.cumsum(gc, axis=2)
        o_inter = jnp.einsum("bhqk,bhkv->bhqv", qc * jnp.exp(bcum), s)
        seg = bcum[:, :, :, None, :] - bcum[:, :, None, :, :]
        dec = jnp.exp(jnp.where(causal, seg, -jnp.inf))
        att = jnp.einsum("bhik,bhijk->bhij", qc, dec * kc[:, :, None, :, :])
        o_intra = jnp.einsum("bhij,bhjv->bhiv", att, vc)
        b_last = bcum[:, :, -1:, :]
        s_new = s * jnp.exp(b_last[:, :, 0, :])[..., None] + jnp.einsum(
            "bhjk,bhjv->bhkv", kc * jnp.exp(b_last - bcum), vc)
        return s_new, (o_inter + o_intra).astype(jnp.float32)

    s0 = jnp.zeros((bsz, HG_HEADS, HG_K, HG_V), jnp.float32)
    _, o = lax.scan(step, s0, (to_chunks(qf), to_chunks(kf), to_chunks(i_val), to_chunks(gl)))
    return o.transpose(1, 0, 3, 2, 4).reshape(bsz, seqlen, HG_HEADS, HG_V)


def _fwd_setup_inputs(seed: int = 0) -> dict:
    key = jax.random.key(seed)
    ks = jax.random.split(key, 24)
    f32 = jnp.float32

    def nrm(k, shape, scale):
        return jax.random.normal(k, shape, f32) * scale

    def gain(k, shape):
        return 1.0 + 0.02 * jax.random.normal(k, shape, f32)

    dt = jnp.exp(jax.random.uniform(ks[5], (DEPTH, SSD_HEADS), f32)
                 * (math.log(0.1) - math.log(0.001)) + math.log(0.001))
    return {
        "x": nrm(ks[0], (BATCH, SEQ, D_MODEL), 1.0),
        "mem": nrm(ks[1], (BATCH, MEM_LEN, D_MODEL), 1.0),
        "norm_mix_w": gain(ks[2], (DEPTH, D_MODEL)),
        "w_in": nrm(ks[3], (DEPTH, D_MODEL, N_IN), D_MODEL ** -0.5),
        "conv_w": nrm(ks[4], (DEPTH, SSD_CONV, SSD_CONV_CH), SSD_CONV ** -0.5),
        "conv_b": nrm(ks[6], (DEPTH, SSD_CONV_CH), 0.02),
        "dt_bias": dt + jnp.log(-jnp.expm1(-dt)),
        "a_log": jnp.log(jax.random.uniform(ks[7], (DEPTH, SSD_HEADS), f32, 1.0, 16.0)),
        "d_skip": 1.0 + 0.1 * jax.random.normal(ks[8], (DEPTH, SSD_HEADS), f32),
        "ssd_norm_w": gain(ks[9], (DEPTH, SSD_DIM)),
        "hg_lower_bounds": nrm(ks[10], (DEPTH + 1, HG_KDIM), 0.5),
        "hg_norm_w": gain(ks[11], (DEPTH, HG_V)),
        "w_out": nrm(ks[12], (DEPTH, D_MIX, D_MODEL), D_MIX ** -0.5),
        "norm_xa_w": gain(ks[13], (DEPTH, D_MODEL)),
        "norm_mem_w": gain(ks[14], (DEPTH, D_MODEL)),
        "xa_wq": nrm(ks[15], (DEPTH, D_MODEL, D_MODEL), D_MODEL ** -0.5),
        "xa_wkv": nrm(ks[16], (DEPTH, D_MODEL, 2 * D_MODEL), D_MODEL ** -0.5),
        "xa_wo": nrm(ks[17], (DEPTH, D_MODEL, D_MODEL), D_MODEL ** -0.5),
        "norm_ffn_w": gain(ks[18], (DEPTH, D_MODEL)),
        "ffn_w_gate": nrm(ks[19], (DEPTH, D_MODEL, FFN_DIM), D_MODEL ** -0.5),
        "ffn_w_up": nrm(ks[20], (DEPTH, D_MODEL, FFN_DIM), D_MODEL ** -0.5),
        "ffn_w_down": nrm(ks[21], (DEPTH, FFN_DIM, D_MODEL), FFN_DIM ** -0.5),
        "norm_final_w": gain(ks[22], (D_MODEL,)),
    }


def _fwd_reference(x, mem, norm_mix_w, w_in, conv_w, conv_b, dt_bias, a_log, d_skip, ssd_norm_w,
              hg_lower_bounds, hg_norm_w, w_out, norm_xa_w, norm_mem_w, xa_wq, xa_wkv, xa_wo,
              norm_ffn_w, ffn_w_gate, ffn_w_up, ffn_w_down, norm_final_w):
    bsz, seqlen, _ = x.shape
    lb_all = jnp.cumsum(jax.nn.softmax(hg_lower_bounds.astype(jnp.float32), axis=0), axis=0)
    s1 = SSD_DIM
    s2 = s1 + SSD_CONV_CH
    s3 = s2 + SSD_HEADS
    s4 = s3 + HG_KDIM
    s5 = s4 + HG_KDIM
    s6 = s5 + HG_VDIM
    for l in range(DEPTH):
        h = rmsnorm(x, norm_mix_w[l])
        proj = h @ w_in[l]
        z, xbc, dt_raw, hq, hf, hi, hgate = jnp.split(proj, [s1, s2, s3, s4, s5, s6], axis=-1)
        xbc = jax.nn.silu(causal_depthwise_conv(xbc, conv_w[l], conv_b[l]))
        xs, bm, cm = jnp.split(xbc, [SSD_DIM, SSD_DIM + SSD_GROUPS * SSD_STATE], axis=-1)
        dt = jax.nn.softplus((dt_raw + dt_bias[l]).astype(jnp.float32))
        y_a = ssd_mixer(xs.reshape(bsz, seqlen, SSD_HEADS, SSD_HEAD_DIM),
                        bm.reshape(bsz, seqlen, SSD_GROUPS, SSD_STATE),
                        cm.reshape(bsz, seqlen, SSD_GROUPS, SSD_STATE),
                        dt, a_log[l], d_skip[l])
        yz = (y_a.reshape(bsz, seqlen, SSD_DIM) * jax.nn.silu(z)).reshape(
            bsz, seqlen, SSD_GROUPS, SSD_DIM // SSD_GROUPS)
        y_a = rmsnorm(yz, ssd_norm_w[l].reshape(SSD_GROUPS, -1)).reshape(bsz, seqlen, SSD_DIM)
        o_b = hgrn2_mixer(hq.reshape(bsz, seqlen, HG_HEADS, HG_K),
                          hf.reshape(bsz, seqlen, HG_HEADS, HG_K),
                          hi.reshape(bsz, seqlen, HG_HEADS, HG_V),
                          lb_all[l].reshape(HG_HEADS, HG_K))
        o_b = rmsnorm(o_b, hg_norm_w[l]) * jax.nn.silu(hgate.reshape(bsz, seqlen, HG_HEADS, HG_V))
        mixed = jnp.concatenate([y_a, o_b.reshape(bsz, seqlen, HG_VDIM)], axis=-1).astype(x.dtype)
        x = x + mixed @ w_out[l]
        h = rmsnorm(x, norm_xa_w[l])
        m = rmsnorm(mem, norm_mem_w[l])
        qx = (h @ xa_wq[l]).reshape(bsz, seqlen, XA_HEADS, XA_HEAD_DIM)
        km, vm = jnp.split(m @ xa_wkv[l], 2, axis=-1)
        km = km.reshape(bsz, MEM_LEN, XA_HEADS, XA_HEAD_DIM)
        vm = vm.reshape(bsz, MEM_LEN, XA_HEADS, XA_HEAD_DIM)
        sc = jnp.einsum("bqhd,bkhd->bhqk", qx, km, preferred_element_type=jnp.float32)
        pr = jax.nn.softmax(sc * (XA_HEAD_DIM ** -0.5), axis=-1).astype(vm.dtype)
        ox = jnp.einsum("bhqk,bkhd->bqhd", pr, vm).reshape(bsz, seqlen, D_MODEL)
        x = x + ox @ xa_wo[l]
        h = rmsnorm(x, norm_ffn_w[l])
        x = x + (jax.nn.silu(h @ ffn_w_gate[l]) * (h @ ffn_w_up[l])) @ ffn_w_down[l]
    return rmsnorm(x, norm_final_w)


import jax as _jax
import jax.numpy as _jnp

TWIN_FORMAT = 'train_step'
FWD_PARAMS = ['x', 'mem', 'norm_mix_w', 'w_in', 'conv_w', 'conv_b', 'dt_bias', 'a_log', 'd_skip', 'ssd_norm_w', 'hg_lower_bounds', 'hg_norm_w', 'w_out', 'norm_xa_w', 'norm_mem_w', 'xa_wq', 'xa_wkv', 'xa_wo', 'norm_ffn_w', 'ffn_w_gate', 'ffn_w_up', 'ffn_w_down', 'norm_final_w']
TWIN_WEIGHTS = ['norm_mix_w', 'w_in', 'conv_w', 'conv_b', 'dt_bias', 'a_log', 'd_skip', 'ssd_norm_w', 'hg_lower_bounds', 'hg_norm_w', 'w_out', 'norm_xa_w', 'norm_mem_w', 'xa_wq', 'xa_wkv', 'xa_wo', 'norm_ffn_w', 'ffn_w_gate', 'ffn_w_up', 'ffn_w_down', 'norm_final_w']
TWIN_DIFF_INPUT = 'x'
TWIN_INPUTS = ['x', 'mem', 'norm_mix_w', 'w_in', 'conv_w', 'conv_b', 'dt_bias', 'a_log', 'd_skip', 'ssd_norm_w', 'hg_lower_bounds', 'hg_norm_w', 'w_out', 'norm_xa_w', 'norm_mem_w', 'xa_wq', 'xa_wkv', 'xa_wo', 'norm_ffn_w', 'ffn_w_gate', 'ffn_w_up', 'ffn_w_down', 'norm_final_w', 'loss_target', 'm_norm_mix_w', 'm_w_in', 'm_conv_w', 'm_conv_b', 'm_dt_bias', 'm_a_log', 'm_d_skip', 'm_ssd_norm_w', 'm_hg_lower_bounds', 'm_hg_norm_w', 'm_w_out', 'm_norm_xa_w', 'm_norm_mem_w', 'm_xa_wq', 'm_xa_wkv', 'm_xa_wo', 'm_norm_ffn_w', 'm_ffn_w_gate', 'm_ffn_w_up', 'm_ffn_w_down', 'm_norm_final_w', 'v_norm_mix_w', 'v_w_in', 'v_conv_w', 'v_conv_b', 'v_dt_bias', 'v_a_log', 'v_d_skip', 'v_ssd_norm_w', 'v_hg_lower_bounds', 'v_hg_norm_w', 'v_w_out', 'v_norm_xa_w', 'v_norm_mem_w', 'v_xa_wq', 'v_xa_wkv', 'v_xa_wo', 'v_norm_ffn_w', 'v_ffn_w_gate', 'v_ffn_w_up', 'v_ffn_w_down', 'v_norm_final_w']
TWIN_OUTPUTS = ['loss', 'grad_x', 'grad_norm_mix_w', 'grad_w_in', 'grad_conv_w', 'grad_conv_b', 'grad_dt_bias', 'grad_a_log', 'grad_d_skip', 'grad_ssd_norm_w', 'grad_hg_lower_bounds', 'grad_hg_norm_w', 'grad_w_out', 'grad_norm_xa_w', 'grad_norm_mem_w', 'grad_xa_wq', 'grad_xa_wkv', 'grad_xa_wo', 'grad_norm_ffn_w', 'grad_ffn_w_gate', 'grad_ffn_w_up', 'grad_ffn_w_down', 'grad_norm_final_w', 'delta_norm_mix_w', 'delta_w_in', 'delta_conv_w', 'delta_conv_b', 'delta_dt_bias', 'delta_a_log', 'delta_d_skip', 'delta_ssd_norm_w', 'delta_hg_lower_bounds', 'delta_hg_norm_w', 'delta_w_out', 'delta_norm_xa_w', 'delta_norm_mem_w', 'delta_xa_wq', 'delta_xa_wkv', 'delta_xa_wo', 'delta_norm_ffn_w', 'delta_ffn_w_gate', 'delta_ffn_w_up', 'delta_ffn_w_down', 'delta_norm_final_w', 'new_m_norm_mix_w', 'new_m_w_in', 'new_m_conv_w', 'new_m_conv_b', 'new_m_dt_bias', 'new_m_a_log', 'new_m_d_skip', 'new_m_ssd_norm_w', 'new_m_hg_lower_bounds', 'new_m_hg_norm_w', 'new_m_w_out', 'new_m_norm_xa_w', 'new_m_norm_mem_w', 'new_m_xa_wq', 'new_m_xa_wkv', 'new_m_xa_wo', 'new_m_norm_ffn_w', 'new_m_ffn_w_gate', 'new_m_ffn_w_up', 'new_m_ffn_w_down', 'new_m_norm_final_w', 'new_v_norm_mix_w', 'new_v_w_in', 'new_v_conv_w', 'new_v_conv_b', 'new_v_dt_bias', 'new_v_a_log', 'new_v_d_skip', 'new_v_ssd_norm_w', 'new_v_hg_lower_bounds', 'new_v_hg_norm_w', 'new_v_w_out', 'new_v_norm_xa_w', 'new_v_norm_mem_w', 'new_v_xa_wq', 'new_v_xa_wkv', 'new_v_xa_wo', 'new_v_norm_ffn_w', 'new_v_ffn_w_gate', 'new_v_ffn_w_up', 'new_v_ffn_w_down', 'new_v_norm_final_w']
TWIN_LEAF_KINDS = {'loss': 'loss', 'grad_x': 'grad_x', 'grad_norm_mix_w': 'grad_w', 'grad_w_in': 'grad_w', 'grad_conv_w': 'grad_w', 'grad_conv_b': 'grad_w', 'grad_dt_bias': 'grad_w', 'grad_a_log': 'grad_w', 'grad_d_skip': 'grad_w', 'grad_ssd_norm_w': 'grad_w', 'grad_hg_lower_bounds': 'grad_w', 'grad_hg_norm_w': 'grad_w', 'grad_w_out': 'grad_w', 'grad_norm_xa_w': 'grad_w', 'grad_norm_mem_w': 'grad_w', 'grad_xa_wq': 'grad_w', 'grad_xa_wkv': 'grad_w', 'grad_xa_wo': 'grad_w', 'grad_norm_ffn_w': 'grad_w', 'grad_ffn_w_gate': 'grad_w', 'grad_ffn_w_up': 'grad_w', 'grad_ffn_w_down': 'grad_w', 'grad_norm_final_w': 'grad_w', 'delta_norm_mix_w': 'delta_w', 'delta_w_in': 'delta_w', 'delta_conv_w': 'delta_w', 'delta_conv_b': 'delta_w', 'delta_dt_bias': 'delta_w', 'delta_a_log': 'delta_w', 'delta_d_skip': 'delta_w', 'delta_ssd_norm_w': 'delta_w', 'delta_hg_lower_bounds': 'delta_w', 'delta_hg_norm_w': 'delta_w', 'delta_w_out': 'delta_w', 'delta_norm_xa_w': 'delta_w', 'delta_norm_mem_w': 'delta_w', 'delta_xa_wq': 'delta_w', 'delta_xa_wkv': 'delta_w', 'delta_xa_wo': 'delta_w', 'delta_norm_ffn_w': 'delta_w', 'delta_ffn_w_gate': 'delta_w', 'delta_ffn_w_up': 'delta_w', 'delta_ffn_w_down': 'delta_w', 'delta_norm_final_w': 'delta_w', 'new_m_norm_mix_w': 'new_m', 'new_m_w_in': 'new_m', 'new_m_conv_w': 'new_m', 'new_m_conv_b': 'new_m', 'new_m_dt_bias': 'new_m', 'new_m_a_log': 'new_m', 'new_m_d_skip': 'new_m', 'new_m_ssd_norm_w': 'new_m', 'new_m_hg_lower_bounds': 'new_m', 'new_m_hg_norm_w': 'new_m', 'new_m_w_out': 'new_m', 'new_m_norm_xa_w': 'new_m', 'new_m_norm_mem_w': 'new_m', 'new_m_xa_wq': 'new_m', 'new_m_xa_wkv': 'new_m', 'new_m_xa_wo': 'new_m', 'new_m_norm_ffn_w': 'new_m', 'new_m_ffn_w_gate': 'new_m', 'new_m_ffn_w_up': 'new_m', 'new_m_ffn_w_down': 'new_m', 'new_m_norm_final_w': 'new_m', 'new_v_norm_mix_w': 'new_v', 'new_v_w_in': 'new_v', 'new_v_conv_w': 'new_v', 'new_v_conv_b': 'new_v', 'new_v_dt_bias': 'new_v', 'new_v_a_log': 'new_v', 'new_v_d_skip': 'new_v', 'new_v_ssd_norm_w': 'new_v', 'new_v_hg_lower_bounds': 'new_v', 'new_v_hg_norm_w': 'new_v', 'new_v_w_out': 'new_v', 'new_v_norm_xa_w': 'new_v', 'new_v_norm_mem_w': 'new_v', 'new_v_xa_wq': 'new_v', 'new_v_xa_wkv': 'new_v', 'new_v_xa_wo': 'new_v', 'new_v_norm_ffn_w': 'new_v', 'new_v_ffn_w_gate': 'new_v', 'new_v_ffn_w_up': 'new_v', 'new_v_ffn_w_down': 'new_v', 'new_v_norm_final_w': 'new_v'}


def _forward(args):
    return _fwd_reference(*[args[k] for k in FWD_PARAMS])


def _output_shape():
    def fwd():
        inp = _fwd_setup_inputs(0)
        return _fwd_reference(*[inp[k] for k in FWD_PARAMS])
    out = _jax.eval_shape(fwd)
    return out.shape, out.dtype

N_MICROBATCH = 1
ADAM_LR = 0.001
ADAM_B1 = 0.9
ADAM_B2 = 0.999
ADAM_EPS = 1e-08
ADAM_WD = 0.01
ADAM_STEP = 10
PER_EXAMPLE_BATCH_AXIS = {'x': 0, 'mem': 0, 'loss_target': 0}
SHARED_INPUTS = []
_WEIGHT_DTYPES = {'norm_mix_w': _jnp.float32, 'w_in': _jnp.float32, 'conv_w': _jnp.float32, 'conv_b': _jnp.float32, 'dt_bias': _jnp.float32, 'a_log': _jnp.float32, 'd_skip': _jnp.float32, 'ssd_norm_w': _jnp.float32, 'hg_lower_bounds': _jnp.float32, 'hg_norm_w': _jnp.float32, 'w_out': _jnp.float32, 'norm_xa_w': _jnp.float32, 'norm_mem_w': _jnp.float32, 'xa_wq': _jnp.float32, 'xa_wkv': _jnp.float32, 'xa_wo': _jnp.float32, 'norm_ffn_w': _jnp.float32, 'ffn_w_gate': _jnp.float32, 'ffn_w_up': _jnp.float32, 'ffn_w_down': _jnp.float32, 'norm_final_w': _jnp.float32}
MOMENT_SCALE = {'norm_mix_w': 3.747473e-01, 'w_in': 1.442760e-01, 'conv_w': 1.831785e-01, 'conv_b': 2.554288e-01, 'dt_bias': 4.177599e-01, 'a_log': 6.582565e-01, 'd_skip': 7.824279e-01, 'ssd_norm_w': 2.070649e-01, 'hg_lower_bounds': 1.138008e-02, 'hg_norm_w': 3.763105e-01, 'w_out': 2.488825e-01, 'norm_xa_w': 3.128051e-02, 'norm_mem_w': 4.737590e-02, 'xa_wq': 3.146441e-02, 'xa_wkv': 3.161526e-02, 'xa_wo': 3.170552e-02, 'norm_ffn_w': 2.199703e-01, 'ffn_w_gate': 9.237687e-02, 'ffn_w_up': 8.936499e-02, 'ffn_w_down': 1.485475e-01, 'norm_final_w': 1.280847e+02}


def _to_microbatches(a, axis):
    t = _jnp.moveaxis(a, axis, 0)
    t = t.reshape((N_MICROBATCH, t.shape[0] // N_MICROBATCH) + t.shape[1:])
    return _jnp.moveaxis(t, 1, axis + 1)


def setup_inputs(seed: int = 0) -> dict:
    inp = _fwd_setup_inputs(seed)
    key = _jax.random.fold_in(_jax.random.key(seed), 7919)
    shape, _ = _output_shape()
    out = dict(inp)
    out["loss_target"] = _jax.random.normal(_jax.random.fold_in(key, 0), shape, _jnp.float32)
    for i, name in enumerate(TWIN_WEIGHTS):
        w = inp[name].astype(_jnp.float32)
        if MOMENT_SCALE is None:
            s = _jnp.sqrt(_jnp.mean(_jnp.square(w)) + 1e-30)
        else:
            s = MOMENT_SCALE[name]
        km, kv = _jax.random.split(_jax.random.fold_in(key, i + 1))
        out[name] = w
        out["m_" + name] = s * _jax.random.normal(km, w.shape, _jnp.float32)
        out["v_" + name] = (s * s) * _jax.random.uniform(kv, w.shape, _jnp.float32, 0.5, 1.5)
    if N_MICROBATCH > 1:
        for name, axis in PER_EXAMPLE_BATCH_AXIS.items():
            out[name] = _to_microbatches(out[name], axis)
    return {'x': out['x'], 'mem': out['mem'], 'norm_mix_w': out['norm_mix_w'], 'w_in': out['w_in'], 'conv_w': out['conv_w'], 'conv_b': out['conv_b'], 'dt_bias': out['dt_bias'], 'a_log': out['a_log'], 'd_skip': out['d_skip'], 'ssd_norm_w': out['ssd_norm_w'], 'hg_lower_bounds': out['hg_lower_bounds'], 'hg_norm_w': out['hg_norm_w'], 'w_out': out['w_out'], 'norm_xa_w': out['norm_xa_w'], 'norm_mem_w': out['norm_mem_w'], 'xa_wq': out['xa_wq'], 'xa_wkv': out['xa_wkv'], 'xa_wo': out['xa_wo'], 'norm_ffn_w': out['norm_ffn_w'], 'ffn_w_gate': out['ffn_w_gate'], 'ffn_w_up': out['ffn_w_up'], 'ffn_w_down': out['ffn_w_down'], 'norm_final_w': out['norm_final_w'], 'loss_target': out['loss_target'], 'm_norm_mix_w': out['m_norm_mix_w'], 'm_w_in': out['m_w_in'], 'm_conv_w': out['m_conv_w'], 'm_conv_b': out['m_conv_b'], 'm_dt_bias': out['m_dt_bias'], 'm_a_log': out['m_a_log'], 'm_d_skip': out['m_d_skip'], 'm_ssd_norm_w': out['m_ssd_norm_w'], 'm_hg_lower_bounds': out['m_hg_lower_bounds'], 'm_hg_norm_w': out['m_hg_norm_w'], 'm_w_out': out['m_w_out'], 'm_norm_xa_w': out['m_norm_xa_w'], 'm_norm_mem_w': out['m_norm_mem_w'], 'm_xa_wq': out['m_xa_wq'], 'm_xa_wkv': out['m_xa_wkv'], 'm_xa_wo': out['m_xa_wo'], 'm_norm_ffn_w': out['m_norm_ffn_w'], 'm_ffn_w_gate': out['m_ffn_w_gate'], 'm_ffn_w_up': out['m_ffn_w_up'], 'm_ffn_w_down': out['m_ffn_w_down'], 'm_norm_final_w': out['m_norm_final_w'], 'v_norm_mix_w': out['v_norm_mix_w'], 'v_w_in': out['v_w_in'], 'v_conv_w': out['v_conv_w'], 'v_conv_b': out['v_conv_b'], 'v_dt_bias': out['v_dt_bias'], 'v_a_log': out['v_a_log'], 'v_d_skip': out['v_d_skip'], 'v_ssd_norm_w': out['v_ssd_norm_w'], 'v_hg_lower_bounds': out['v_hg_lower_bounds'], 'v_hg_norm_w': out['v_hg_norm_w'], 'v_w_out': out['v_w_out'], 'v_norm_xa_w': out['v_norm_xa_w'], 'v_norm_mem_w': out['v_norm_mem_w'], 'v_xa_wq': out['v_xa_wq'], 'v_xa_wkv': out['v_xa_wkv'], 'v_xa_wo': out['v_xa_wo'], 'v_norm_ffn_w': out['v_norm_ffn_w'], 'v_ffn_w_gate': out['v_ffn_w_gate'], 'v_ffn_w_up': out['v_ffn_w_up'], 'v_ffn_w_down': out['v_ffn_w_down'], 'v_norm_final_w': out['v_norm_final_w']}


def _loss(weights, diff, rest, loss_target):
    with _jax.named_scope("forward"):
        args = {**rest, TWIN_DIFF_INPUT: diff, **{k: w.astype(_WEIGHT_DTYPES[k]) for k, w in weights.items()}}
        y = _forward(args)
    with _jax.named_scope("loss_head"):
        err = _jnp.square(y.astype(_jnp.float32) - loss_target)
        return 0.5 * _jnp.sum(_jnp.mean(err, axis=-1)) if err.ndim else 0.5 * err


def _adamw(w, g, m, v):
    m = ADAM_B1 * m + (1.0 - ADAM_B1) * g
    v = ADAM_B2 * v + (1.0 - ADAM_B2) * _jnp.square(g)
    m_hat = m / (1.0 - ADAM_B1 ** ADAM_STEP)
    v_hat = v / (1.0 - ADAM_B2 ** ADAM_STEP)
    delta = -ADAM_LR * (m_hat / (_jnp.sqrt(v_hat) + ADAM_EPS) + ADAM_WD * w)
    return delta, m, v


def reference(x, mem, norm_mix_w, w_in, conv_w, conv_b, dt_bias, a_log, d_skip, ssd_norm_w, hg_lower_bounds, hg_norm_w, w_out, norm_xa_w, norm_mem_w, xa_wq, xa_wkv, xa_wo, norm_ffn_w, ffn_w_gate, ffn_w_up, ffn_w_down, norm_final_w, loss_target, m_norm_mix_w, m_w_in, m_conv_w, m_conv_b, m_dt_bias, m_a_log, m_d_skip, m_ssd_norm_w, m_hg_lower_bounds, m_hg_norm_w, m_w_out, m_norm_xa_w, m_norm_mem_w, m_xa_wq, m_xa_wkv, m_xa_wo, m_norm_ffn_w, m_ffn_w_gate, m_ffn_w_up, m_ffn_w_down, m_norm_final_w, v_norm_mix_w, v_w_in, v_conv_w, v_conv_b, v_dt_bias, v_a_log, v_d_skip, v_ssd_norm_w, v_hg_lower_bounds, v_hg_norm_w, v_w_out, v_norm_xa_w, v_norm_mem_w, v_xa_wq, v_xa_wkv, v_xa_wo, v_norm_ffn_w, v_ffn_w_gate, v_ffn_w_up, v_ffn_w_down, v_norm_final_w):
    given = dict(x=x, mem=mem, norm_mix_w=norm_mix_w, w_in=w_in, conv_w=conv_w, conv_b=conv_b, dt_bias=dt_bias, a_log=a_log, d_skip=d_skip, ssd_norm_w=ssd_norm_w, hg_lower_bounds=hg_lower_bounds, hg_norm_w=hg_norm_w, w_out=w_out, norm_xa_w=norm_xa_w, norm_mem_w=norm_mem_w, xa_wq=xa_wq, xa_wkv=xa_wkv, xa_wo=xa_wo, norm_ffn_w=norm_ffn_w, ffn_w_gate=ffn_w_gate, ffn_w_up=ffn_w_up, ffn_w_down=ffn_w_down, norm_final_w=norm_final_w, loss_target=loss_target, m_norm_mix_w=m_norm_mix_w, m_w_in=m_w_in, m_conv_w=m_conv_w, m_conv_b=m_conv_b, m_dt_bias=m_dt_bias, m_a_log=m_a_log, m_d_skip=m_d_skip, m_ssd_norm_w=m_ssd_norm_w, m_hg_lower_bounds=m_hg_lower_bounds, m_hg_norm_w=m_hg_norm_w, m_w_out=m_w_out, m_norm_xa_w=m_norm_xa_w, m_norm_mem_w=m_norm_mem_w, m_xa_wq=m_xa_wq, m_xa_wkv=m_xa_wkv, m_xa_wo=m_xa_wo, m_norm_ffn_w=m_norm_ffn_w, m_ffn_w_gate=m_ffn_w_gate, m_ffn_w_up=m_ffn_w_up, m_ffn_w_down=m_ffn_w_down, m_norm_final_w=m_norm_final_w, v_norm_mix_w=v_norm_mix_w, v_w_in=v_w_in, v_conv_w=v_conv_w, v_conv_b=v_conv_b, v_dt_bias=v_dt_bias, v_a_log=v_a_log, v_d_skip=v_d_skip, v_ssd_norm_w=v_ssd_norm_w, v_hg_lower_bounds=v_hg_lower_bounds, v_hg_norm_w=v_hg_norm_w, v_w_out=v_w_out, v_norm_xa_w=v_norm_xa_w, v_norm_mem_w=v_norm_mem_w, v_xa_wq=v_xa_wq, v_xa_wkv=v_xa_wkv, v_xa_wo=v_xa_wo, v_norm_ffn_w=v_norm_ffn_w, v_ffn_w_gate=v_ffn_w_gate, v_ffn_w_up=v_ffn_w_up, v_ffn_w_down=v_ffn_w_down, v_norm_final_w=v_norm_final_w)
    weights = {n: given[n] for n in TWIN_WEIGHTS}
    shared = {n: given[n] for n in SHARED_INPUTS}
    per_example = {n: given[n] for n in ['x', 'mem']}
    grad_fn = _jax.value_and_grad(_loss, argnums=(0, 1))

    def one_microbatch(ex, loss_target):
        ex = dict(ex)
        diff = ex.pop(TWIN_DIFF_INPUT)
        return grad_fn(weights, diff, {**shared, **ex}, loss_target)

    if N_MICROBATCH == 1:
        loss, (grad_w, grad_x) = one_microbatch(per_example, given["loss_target"])
    else:
        def body(carry, xs):
            loss_sum, grad_sum = carry
            l_k, (gw_k, gx_k) = one_microbatch(xs[0], xs[1])
            with _jax.named_scope("update"):
                return (loss_sum + l_k, _jax.tree.map(_jnp.add, grad_sum, gw_k)), gx_k

        init = (_jnp.zeros((), _jnp.float32), _jax.tree.map(_jnp.zeros_like, weights))
        (loss, grad_w), grad_x = _jax.lax.scan(body, init, (per_example, given["loss_target"]))
    with _jax.named_scope("update"):
        delta_w, new_m, new_v = {}, {}, {}
        for n in TWIN_WEIGHTS:
            delta_w[n], new_m[n], new_v[n] = _adamw(weights[n], grad_w[n], given["m_" + n], given["v_" + n])
    return (loss, grad_x, *[grad_w[n] for n in TWIN_WEIGHTS], *[delta_w[n] for n in TWIN_WEIGHTS],
            *[new_m[n] for n in TWIN_WEIGHTS], *[new_v[n] for n in TWIN_WEIGHTS])
```

```python
import jax
import jax.numpy as jnp
from jax import lax
from jax.experimental import pallas as pl
from jax.experimental.pallas import tpu as pltpu

F32 = jnp.float32
BF = jnp.bfloat16
HI = lax.Precision.HIGHEST
MESH = pl.DeviceIdType.MESH
SDS = jax.ShapeDtypeStruct
ANY = pl.BlockSpec(memory_space=pl.ANY)

D = 1024
EPS = 1e-6
NH_SSD = 16
SSD_P = 64
NH_HG = 8
Q = 128
SUB = 16
NSUB = Q // SUB
XA_HEADS = 4
XA_HD = 256
MEM_LEN = 256
FFN = 2816
TL = 256
VMEM_LIMIT = 56 << 20

Z0, XBC0, HQ0, HF0, HI0, HG0, DT0, NINP = 0, 1024, 2560, 3584, 4608, 5632, 6656, 6784
N_IN = 6672

ADAM_LR, ADAM_B1, ADAM_B2, ADAM_EPS, ADAM_WD, ADAM_STEP = 0.001, 0.9, 0.999, 1e-08, 0.01, 10

BIG = (
    ("w_in", (1024, 6672), 1), ("w_out", (2048, 1024), 0), ("xa_wq", (1024, 1024), 0),
    ("xa_wkv", (1024, 2048), 1), ("xa_wo", (1024, 1024), 0), ("ffn_w_gate", (1024, 2816), 1),
    ("ffn_w_up", (1024, 2816), 1), ("ffn_w_down", (2816, 1024), 0))
SMALL = ("norm_mix_w", "conv_w", "conv_b", "dt_bias", "a_log", "d_skip", "ssd_norm_w", "hg_lower_bounds",
         "hg_norm_w", "norm_xa_w", "norm_mem_w", "norm_ffn_w", "norm_final_w")
WEIGHTS = ("norm_mix_w", "w_in", "conv_w", "conv_b", "dt_bias", "a_log", "d_skip", "ssd_norm_w", "hg_lower_bounds",
           "hg_norm_w", "w_out", "norm_xa_w", "norm_mem_w", "xa_wq", "xa_wkv", "xa_wo", "norm_ffn_w", "ffn_w_gate",
           "ffn_w_up", "ffn_w_down", "norm_final_w")
SLAB_ROWS = 21264


def _cparams():
    return pltpu.CompilerParams(dimension_semantics=("arbitrary",), vmem_limit_bytes=VMEM_LIMIT)


def _const(shape):
    return pl.BlockSpec(shape, lambda i: (0,) * len(shape))


def _rows(tl, n):
    return pl.BlockSpec((tl, n), lambda i: (i, 0))


def _dot(a, b):
    return jnp.dot(a.astype(BF), b.astype(BF), preferred_element_type=F32)


def _dot_nt(a, b):
    return lax.dot_general(a.astype(BF), b.astype(BF), (((1,), (1,)), ((), ())), preferred_element_type=F32)


def _dot_tn(a, b):
    return lax.dot_general(a.astype(BF), b.astype(BF), (((0,), (0,)), ((), ())), preferred_element_type=F32)


def _dot_hi(a, b):
    return jnp.dot(a, b, precision=HI, preferred_element_type=F32)


def _iota(shape, dim):
    return lax.broadcasted_iota(jnp.int32, shape, dim)


def _sigmoid(v):
    return 1.0 / (1.0 + jnp.exp(-v))


def _rms(v, w):
    r = lax.rsqrt(jnp.mean(v * v, axis=-1, keepdims=True) + EPS)
    n = v * r
    return n * w, n, r


def _rms_bwd(dy, n, r, w):
    dn = dy * w
    return r * (dn - n * jnp.mean(dn * n, axis=-1, keepdims=True)), dy * n


def _colsum(v):
    return jnp.sum(v, axis=0, keepdims=True)


def _zero_first(*refs):
    @pl.when(pl.program_id(0) == 0)
    def _():
        for r in refs:
            r[...] = jnp.zeros_like(r)


def _in_proj(x, nw, w_p):
    L = x.shape[0]
    tl = min(TL, L)

    def body(x_ref, nw_ref, w_ref, h0_ref, z_ref, xbc_ref, hq_ref, hf_ref, hi_ref, hg_ref, dt_ref):
        h, _, _ = _rms(x_ref[...], nw_ref[...])
        hb = h.astype(BF)
        h0_ref[...] = hb

        def proj(a, b):
            return jnp.dot(hb, w_ref[:, a:b], preferred_element_type=F32)

        z_ref[...] = proj(Z0, XBC0).astype(BF)
        xbc_ref[...] = proj(XBC0, HQ0).astype(BF)
        hq_ref[...] = proj(HQ0, HF0).astype(BF)
        hf_ref[...] = proj(HF0, HI0)
        hi_ref[...] = proj(HI0, HG0).astype(BF)
        hg_ref[...] = proj(HG0, DT0).astype(BF)
        dt_ref[...] = proj(DT0, NINP)

    outs = [SDS((L, D), BF), SDS((L, D), BF), SDS((L, 1536), BF), SDS((L, D), BF), SDS((L, D), F32),
            SDS((L, D), BF), SDS((L, D), BF), SDS((L, 128), F32)]
    return pl.pallas_call(
        body, grid=(L // tl,), name="in_proj",
        in_specs=[_rows(tl, D), _const((1, D)), _const((D, NINP))],
        out_specs=[_rows(tl, o.shape[1]) for o in outs], out_shape=outs,
        compiler_params=_cparams())(x, nw, w_p)


def _mem_kv(mem, nw, wkv):
    def body(m_ref, nw_ref, w_ref, mb_ref, k_ref, v_ref):
        m, _, _ = _rms(m_ref[...], nw_ref[...])
        mb = m.astype(BF)
        mb_ref[...] = mb
        k_ref[...] = jnp.dot(mb, w_ref[:, :D], preferred_element_type=F32).astype(BF)
        v_ref[...] = jnp.dot(mb, w_ref[:, D:], preferred_element_type=F32).astype(BF)

    outs = [SDS((MEM_LEN, D), BF)] * 3
    return pl.pallas_call(
        body, grid=(1,), name="mem_kv",
        in_specs=[_const((MEM_LEN, D)), _const((1, D)), _const((D, 2 * D))],
        out_specs=[_const((MEM_LEN, D))] * 3, out_shape=outs, compiler_params=_cparams())(mem, nw, wkv)


def _mem_kv_bwd(mem, nw, wkv, dk, dv):
    def body(m_ref, nw_ref, w_ref, dk_ref, dv_ref, gnw_ref, dkv_ref):
        _, n, _ = _rms(m_ref[...], nw_ref[...])
        dkb, dvb = dk_ref[...].astype(BF), dv_ref[...].astype(BF)
        dkv_ref[:, :D] = dkb
        dkv_ref[:, D:] = dvb
        dm = _dot_nt(dkb, w_ref[:, :D]) + _dot_nt(dvb, w_ref[:, D:])
        gnw_ref[...] = _colsum(dm * n)

    return pl.pallas_call(
        body, grid=(1,), name="mem_kv_bwd",
        in_specs=[_const((MEM_LEN, D)), _const((1, D)), _const((D, 2 * D)), _const((MEM_LEN, D)), _const((MEM_LEN, D))],
        out_specs=[_const((1, D)), _const((MEM_LEN, 2 * D))],
        out_shape=[SDS((1, D), F32), SDS((MEM_LEN, 2 * D), BF)], compiler_params=_cparams())(mem, nw, wkv, dk, dv)


def _softmax_rows(sc):
    e = jnp.exp(sc - jnp.max(sc, axis=-1, keepdims=True))
    return e / jnp.sum(e, axis=-1, keepdims=True)


def _attn_fwd(x, ya, ob, w_out, nxa, wq, k, v, wo):
    L = x.shape[0]
    tl = min(TL, L)
    scale = XA_HD ** -0.5

    def body(x_ref, ya_ref, ob_ref, wout_ref, nxa_ref, wq_ref, k_ref, v_ref, wo_ref,
             x1_ref, x2_ref, hxa_ref, q_ref, ox_ref):
        x1 = x_ref[...] + jnp.dot(ya_ref[...], wout_ref[:D, :], preferred_element_type=F32) \
            + jnp.dot(ob_ref[...], wout_ref[D:, :], preferred_element_type=F32)
        x1_ref[...] = x1
        h, _, _ = _rms(x1, nxa_ref[...])
        hb = h.astype(BF)
        hxa_ref[...] = hb
        qb = jnp.dot(hb, wq_ref[...], preferred_element_type=F32).astype(BF)
        q_ref[...] = qb
        oxs = []
        for hd in range(XA_HEADS):
            sl = slice(hd * XA_HD, (hd + 1) * XA_HD)
            p = _softmax_rows(_dot_nt(qb[:, sl], k_ref[:, sl]) * scale)
            oxs.append(_dot(p, v_ref[:, sl]))
        oxb = jnp.concatenate(oxs, axis=1).astype(BF)
        ox_ref[...] = oxb
        x2_ref[...] = x1 + jnp.dot(oxb, wo_ref[...], preferred_element_type=F32)

    outs = [SDS((L, D), F32), SDS((L, D), F32), SDS((L, D), BF), SDS((L, D), BF), SDS((L, D), BF)]
    return pl.pallas_call(
        body, grid=(L // tl,), name="attn_fwd",
        in_specs=[_rows(tl, D), _rows(tl, D), _rows(tl, D), _const((2 * D, D)), _const((1, D)), _const((D, D)),
                  _const((MEM_LEN, D)), _const((MEM_LEN, D)), _const((D, D))],
        out_specs=[_rows(tl, D)] * 5, out_shape=outs, compiler_params=_cparams())(x, ya, ob, w_out, nxa, wq, k, v, wo)


def _ffn_loss(x2, tgt, nffn, nfin, wg, wu, wd):
    L = x2.shape[0]
    tl = min(TL, L)

    def body(x2_ref, t_ref, nffn_ref, nfin_ref, wg_ref, wu_ref, wd_ref,
             dx2_ref, h_ref, a_ref, dx3_ref, dg_ref, du_ref, acc_ref):
        _zero_first(acc_ref)
        x2v = x2_ref[...]
        h, n2, r2 = _rms(x2v, nffn_ref[...])
        hb = h.astype(BF)
        h_ref[...] = hb
        g = jnp.dot(hb, wg_ref[...], preferred_element_type=F32)
        u = jnp.dot(hb, wu_ref[...], preferred_element_type=F32)
        sg = _sigmoid(g)
        ab = (g * sg * u).astype(BF)
        a_ref[...] = ab
        x3 = x2v + jnp.dot(ab, wd_ref[...], preferred_element_type=F32)
        y, n3, r3 = _rms(x3, nfin_ref[...])
        err = y - t_ref[...]
        acc_ref[0:1, :] += _colsum(err * err)
        dy = err * (1.0 / D)
        dx3, dwf = _rms_bwd(dy, n3, r3, nfin_ref[...])
        acc_ref[1:2, :] += _colsum(dwf)
        dx3b = dx3.astype(BF)
        dx3_ref[...] = dx3b
        da = _dot_nt(dx3b, wd_ref[...])
        dgb = (da * u * sg * (1.0 + g * (1.0 - sg))).astype(BF)
        dub = (da * g * sg).astype(BF)
        dg_ref[...] = dgb
        du_ref[...] = dub
        dh = _dot_nt(dgb, wg_ref[...]) + _dot_nt(dub, wu_ref[...])
        dn, dwn = _rms_bwd(dh, n2, r2, nffn_ref[...])
        acc_ref[2:3, :] += _colsum(dwn)
        dx2_ref[...] = dx3 + dn

    outs = [SDS((L, D), F32), SDS((L, D), BF), SDS((L, FFN), BF), SDS((L, D), BF), SDS((L, FFN), BF),
            SDS((L, FFN), BF), SDS((8, D), F32)]
    wspec = pl.BlockSpec((D, FFN), lambda i: (0, 0), pipeline_mode=pl.Buffered(1))
    wdspec = pl.BlockSpec((FFN, D), lambda i: (0, 0), pipeline_mode=pl.Buffered(1))
    return pl.pallas_call(
        body, grid=(L // tl,), name="ffn_loss",
        in_specs=[_rows(tl, D), _rows(tl, D), _const((1, D)), _const((1, D)), wspec, wspec, wdspec],
        out_specs=[_rows(tl, D), _rows(tl, D), _rows(tl, FFN), _rows(tl, D), _rows(tl, FFN), _rows(tl, FFN),
                   _const((8, D))],
        out_shape=outs, compiler_params=_cparams())(x2, tgt, nffn, nfin, wg, wu, wd)


def _attn_bwd(dx2, x1, q, k, v, nxa, wq, wo, w_out):
    L = dx2.shape[0]
    tl = min(TL, L)
    scale = XA_HD ** -0.5

    def body(dx2_ref, x1_ref, q_ref, k_ref, v_ref, nxa_ref, wq_ref, wo_ref, wout_ref,
             dx1_ref, dya_ref, dob_ref, dq_ref, dk_ref, dv_ref, acc_ref):
        _zero_first(dk_ref, dv_ref, acc_ref)
        dx2v = dx2_ref[...]
        dox = _dot_nt(dx2v, wo_ref[...]).astype(BF)
        qb = q_ref[...]
        dqs = []
        for hd in range(XA_HEADS):
            sl = slice(hd * XA_HD, (hd + 1) * XA_HD)
            kh, vh, qh, doh = k_ref[:, sl], v_ref[:, sl], qb[:, sl], dox[:, sl]
            p = _softmax_rows(_dot_nt(qh, kh) * scale)
            dp = _dot_nt(doh, vh)
            dv_ref[:, sl] += _dot_tn(p, doh)
            ds = p * (dp - jnp.sum(dp * p, axis=-1, keepdims=True)) * scale
            dqs.append(_dot(ds, kh))
            dk_ref[:, sl] += _dot_tn(ds, qh)
        dqb = jnp.concatenate(dqs, axis=1).astype(BF)
        dq_ref[...] = dqb
        dh = _dot_nt(dqb, wq_ref[...])
        _, n1, r1 = _rms(x1_ref[...], nxa_ref[...])
        dn, dwn = _rms_bwd(dh, n1, r1, nxa_ref[...])
        acc_ref[0:1, :] += _colsum(dwn)
        dx1 = dx2v + dn
        dx1_ref[...] = dx1
        dx1b = dx1.astype(BF)
        dya_ref[...] = _dot_nt(dx1b, wout_ref[:D, :]).astype(BF)
        dob_ref[...] = _dot_nt(dx1b, wout_ref[D:, :]).astype(BF)

    outs = [SDS((L, D), F32), SDS((L, D), BF), SDS((L, D), BF), SDS((L, D), BF), SDS((MEM_LEN, D), F32),
            SDS((MEM_LEN, D), F32), SDS((8, D), F32)]
    return pl.pallas_call(
        body, grid=(L // tl,), name="attn_bwd",
        in_specs=[_rows(tl, D), _rows(tl, D), _rows(tl, D), _const((MEM_LEN, D)), _const((MEM_LEN, D)), _const((1, D)),
                  _const((D, D)), _const((D, D)), _const((2 * D, D))],
        out_specs=[_rows(tl, D)] * 4 + [_const((MEM_LEN, D)), _const((MEM_LEN, D)), _const((8, D))],
        out_shape=outs, compiler_params=_cparams())(dx2, x1, q, k, v, nxa, wq, wo, w_out)


def _in_proj_bwd(x, dx1, dz, dxbc, dhq, dhf, dhi, dhg, ddt, nw, w_p):
    L = x.shape[0]
    tl = min(TL, L)

    def body(x_ref, dx1_ref, dz_ref, dxbc_ref, dhq_ref, dhf_ref, dhi_ref, dhg_ref, ddt_ref, nw_ref, w_ref,
             gx_ref, acc_ref):
        _zero_first(acc_ref)
        dh = _dot_nt(dz_ref[...], w_ref[:, Z0:XBC0]) + _dot_nt(dxbc_ref[...], w_ref[:, XBC0:HQ0]) \
            + _dot_nt(dhq_ref[...], w_ref[:, HQ0:HF0]) + _dot_nt(dhf_ref[...], w_ref[:, HF0:HI0]) \
            + _dot_nt(dhi_ref[...], w_ref[:, HI0:HG0]) + _dot_nt(dhg_ref[...], w_ref[:, HG0:DT0]) \
            + _dot_nt(ddt_ref[...], w_ref[:, DT0:NINP])
        _, n, r = _rms(x_ref[...], nw_ref[...])
        dn, dwn = _rms_bwd(dh, n, r, nw_ref[...])
        acc_ref[0:1, :] += _colsum(dwn)
        gx_ref[...] = dx1_ref[...] + dn

    return pl.pallas_call(
        body, grid=(L // tl,), name="in_proj_bwd",
        in_specs=[_rows(tl, D), _rows(tl, D), _rows(tl, D), _rows(tl, 1536), _rows(tl, D), _rows(tl, D), _rows(tl, D),
                  _rows(tl, D), _rows(tl, 128), _const((1, D)), _const((D, NINP))],
        out_specs=[_rows(tl, D), _const((8, D))], out_shape=[SDS((L, D), F32), SDS((8, D), F32)],
        compiler_params=_cparams())(x, dx1, dz, dxbc, dhq, dhf, dhi, dhg, ddt, nw, w_p)


def _matmul_tn(a, b, name):
    L, M = a.shape
    N = b.shape[1]
    tl = min(512, L)

    def body(a_ref, b_ref, o_ref):
        _zero_first(o_ref)
        o_ref[...] += _dot_tn(a_ref[...], b_ref[...])

    return pl.pallas_call(
        body, grid=(L // tl,), name=name, in_specs=[_rows(tl, M), _rows(tl, N)], out_specs=_const((M, N)),
        out_shape=SDS((M, N), F32), compiler_params=_cparams())(a, b)


def _head_expand():
    e = (jnp.right_shift(_iota((128, D), 1), 6) == _iota((128, D), 0)).astype(F32)
    et = (jnp.right_shift(_iota((D, 128), 0), 6) == _iota((D, 128), 1)).astype(F32)
    return e, et


def _conv_shifts(cur, other, up):
    rows = _iota((Q, 1), 0)
    out = []
    for s in (1, 2, 3):
        if up:
            out.append(jnp.where(rows >= Q - s, pltpu.roll(other, Q - s, 0), pltpu.roll(cur, Q - s, 0)))
        else:
            out.append(jnp.where(rows < s, pltpu.roll(other, s, 0), pltpu.roll(cur, s, 0)))
    return out


def _ssd_pre(u, dtr, dtb, alog):
    e, et = _head_expand()
    sgu = _sigmoid(u)
    xc = u * sgu
    lane = _iota((1, 128), 1)
    hmask = (lane < NH_SSD).astype(F32)
    pre = dtr + dtb
    dt = (jnp.maximum(pre, 0.0) + jnp.log(1.0 + jnp.exp(-jnp.abs(pre)))) * hmask
    a_row = -jnp.exp(alog)
    causal = _iota((Q, Q), 1) <= _iota((Q, Q), 0)
    tri = causal.astype(F32)
    acum = _dot_hi(tri, dt * a_row)
    acum_full = _dot_hi(acum, e)
    alast_full = acum_full[Q - 1:Q, :]
    dt_full = _dot_hi(dt, e)
    xs = xc[:, :D]
    return dict(e=e, et=et, sgu=sgu, xs=xs, bm=xc[:, D:D + 256], cm=xc[:, D + 256:], hmask=hmask, pre=pre, dt=dt,
                a_row=a_row, causal=causal, tri=tri, acum=acum, acum_t=acum.T, eA_full=jnp.exp(acum_full),
                dte_full=jnp.exp(alast_full - acum_full), dt_full=dt_full, xdt=xs * dt_full)


def _ssd_decay(pre, hh, cb):
    seg = pre["acum"][:, hh:hh + 1] - pre["acum_t"][hh:hh + 1, :]
    lm = jnp.where(pre["causal"], jnp.exp(jnp.minimum(seg, 0.0)), 0.0)
    return lm, cb * lm


def _ssd_fwd(xbc, dtr, z, conv_w, conv_b, dtb, alog, dskip_full, nw):
    L = xbc.shape[0]
    nc = L // Q

    def body(xbc_ref, dtr_ref, z_ref, cw_ref, cb_ref, dtb_ref, alog_ref, dsk_ref, nw_ref,
             ya_ref, y_ref, u_ref, st_ref, prev_ref, s_ref):
        @pl.when(pl.program_id(0) == 0)
        def _():
            prev_ref[...] = jnp.zeros_like(prev_ref)
            s_ref[...] = jnp.zeros_like(s_ref)

        xr = xbc_ref[...].astype(F32)
        sh = _conv_shifts(xr, prev_ref[...], up=False)
        u = cb_ref[...] + cw_ref[3:4, :] * xr + cw_ref[2:3, :] * sh[0] + cw_ref[1:2, :] * sh[1] + cw_ref[0:1, :] * sh[2]
        prev_ref[...] = xr
        ub = u.astype(BF)
        u_ref[...] = ub
        pre = _ssd_pre(ub.astype(F32), dtr_ref[...], dtb_ref[...], alog_ref[...])
        lo = _iota((1, 128), 1) < SSD_P
        s_old = s_ref[...]
        st_ref[0] = s_old
        ys = []
        for g in range(2):
            bg, cg = pre["bm"][:, 128 * g:128 * g + 128], pre["cm"][:, 128 * g:128 * g + 128]
            cb = _dot_nt(cg, bg)
            gs = slice(512 * g, 512 * g + 512)
            yd = []
            for j in range(4 * g, 4 * g + 4):
                xp = pre["xdt"][:, 128 * j:128 * j + 128].astype(BF)
                _, m0 = _ssd_decay(pre, 2 * j, cb)
                _, m1 = _ssd_decay(pre, 2 * j + 1, cb)
                yd.append(jnp.where(lo, _dot(m0, xp), _dot(m1, xp)))
            yoff = _dot_nt(cg, s_old[gs, :]) * pre["eA_full"][:, gs]
            ys.append(jnp.concatenate(yd, axis=1) + yoff)
            st = _dot_tn((pre["xdt"] * pre["dte_full"])[:, gs], bg)
            cdcol = jnp.exp(_dot_hi(pre["et"][gs, :], pre["acum_t"])[:, Q - 1:Q])
            s_ref[gs, :] = s_old[gs, :] * cdcol + st
        y = jnp.concatenate(ys, axis=1) + dsk_ref[...] * pre["xs"]
        yb = y.astype(BF)
        y_ref[...] = yb
        zf = z_ref[...].astype(F32)
        yz = yb.astype(F32) * zf * _sigmoid(zf)
        outs = []
        for g in range(2):
            gs = slice(512 * g, 512 * g + 512)
            o, _, _ = _rms(yz[:, gs], nw_ref[:, gs])
            outs.append(o)
        ya_ref[...] = jnp.concatenate(outs, axis=1).astype(BF)

    outs = [SDS((L, D), BF), SDS((L, D), BF), SDS((L, 1536), BF), SDS((nc, D, 128), F32)]
    return pl.pallas_call(
        body, grid=(nc,), name="ssd_fwd",
        in_specs=[_rows(Q, 1536), _rows(Q, 128), _rows(Q, D), _const((4, 1536)), _const((1, 1536)), _const((1, 128)),
                  _const((1, 128)), _const((1, D)), _const((1, D))],
        out_specs=[_rows(Q, D), _rows(Q, D), _rows(Q, 1536), pl.BlockSpec((1, D, 128), lambda i: (i, 0, 0))],
        out_shape=outs, scratch_shapes=[pltpu.VMEM((Q, 1536), F32), pltpu.VMEM((D, 128), F32)],
        compiler_params=_cparams())(xbc, dtr, z, conv_w, conv_b, dtb, alog, dskip_full, nw)


def _ssd_bwd(dya, y, z, u, xbc, dtr, states, conv_w, dtb, alog, dskip_full, nw):
    L = dya.shape[0]
    nc = L // Q

    def body(dya_ref, y_ref, z_ref, u_ref, xc_ref, xp_ref, dtr_ref, st_ref, cw_ref, dtb_ref, alog_ref, dsk_ref, nw_ref,
             dz_ref, dxbc_ref, ddt_ref, gconv_ref, ghead_ref, glane_ref, gs_ref, ndu_ref):
        step = pl.program_id(0)

        @pl.when(step == 0)
        def _():
            for r in (gconv_ref, ghead_ref, glane_ref, gs_ref, ndu_ref):
                r[...] = jnp.zeros_like(r)

        uf = u_ref[...].astype(F32)
        pre = _ssd_pre(uf, dtr_ref[...], dtb_ref[...], alog_ref[...])
        e, et, xs, xdt = pre["e"], pre["et"], pre["xs"], pre["xdt"]
        lane = _iota((1, 128), 1)
        lo = lane < SSD_P
        sub = _iota((128, 1), 0)
        zf = z_ref[...].astype(F32)
        sgz = _sigmoid(zf)
        sz = zf * sgz
        yv = y_ref[...].astype(F32)
        yz = yv * sz
        dyav = dya_ref[...].astype(F32)
        dyz, dnw = [], []
        for g in range(2):
            gs = slice(512 * g, 512 * g + 512)
            _, n, r = _rms(yz[:, gs], nw_ref[:, gs])
            dv, dw = _rms_bwd(dyav[:, gs], n, r, nw_ref[:, gs])
            dyz.append(dv)
            dnw.append(dw)
        dyz = jnp.concatenate(dyz, axis=1)
        glane_ref[1:2, :] += _colsum(jnp.concatenate(dnw, axis=1))
        dy = dyz * sz
        dz_ref[...] = (dyz * yv * sgz * (1.0 + zf * (1.0 - sgz))).astype(BF)
        glane_ref[0:1, :] += _colsum(dy * xs)
        dxs = dsk_ref[...] * dy

        s_in = st_ref[0]
        gst = gs_ref[...]
        gy = dy * pre["eA_full"]
        xdte = xdt * pre["dte_full"]
        dacum = jnp.zeros((Q, 128), F32)
        dacum_t = jnp.zeros((128, Q), F32)
        dxdt, dacum_full, ddte_full, dbs, dcs = [], [], [], [], []
        for g in range(2):
            gs = slice(512 * g, 512 * g + 512)
            bg, cg = pre["bm"][:, 128 * g:128 * g + 128], pre["cm"][:, 128 * g:128 * g + 128]
            sg_, dg_ = s_in[gs, :], gst[gs, :]
            yoff = _dot_nt(cg, sg_) * pre["eA_full"][:, gs]
            dc = _dot(gy[:, gs], sg_)
            dsin = _dot_tn(gy[:, gs], cg)
            dacum_full.append(dy[:, gs] * yoff)
            tg = _dot_nt(bg, dg_)
            ddte_full.append(tg * xdt[:, gs])
            db = _dot(xdte[:, gs], dg_)
            cb = _dot_nt(cg, bg)
            dcb = jnp.zeros((Q, Q), F32)
            dxg = []
            for j in range(4 * g, 4 * g + 4):
                xp = xdt[:, 128 * j:128 * j + 128].astype(BF)
                dyp = dy[:, 128 * j:128 * j + 128]
                dxp = jnp.zeros((Q, 128), F32)
                for idx in range(2):
                    hh = 2 * j + idx
                    lm, m = _ssd_decay(pre, hh, cb)
                    dym = jnp.where(lo if idx == 0 else jnp.logical_not(lo), dyp, 0.0).astype(BF)
                    dm = jnp.where(pre["causal"], _dot_nt(dym, xp), 0.0)
                    w = dm * m
                    dacum = dacum + jnp.where(lane == hh, jnp.sum(w, axis=1, keepdims=True), 0.0)
                    dacum_t = dacum_t + jnp.where(sub == hh, jnp.sum(w, axis=0, keepdims=True), 0.0)
                    dcb = dcb + dm * lm
                    dxp = dxp + _dot_tn(m, dym)
                dxg.append(dxp)
            dxdt.append(jnp.concatenate(dxg, axis=1) + tg * pre["dte_full"][:, gs])
            dcs.append(dc + _dot(dcb, bg))
            dbs.append(db + _dot_tn(dcb, cg))
            cdcol = jnp.exp(_dot_hi(et[gs, :], pre["acum_t"])[:, Q - 1:Q])
            gs_ref[gs, :] = dsin + dg_ * cdcol
        dxdt = jnp.concatenate(dxdt, axis=1)
        dacum = dacum + _dot_hi(jnp.concatenate(dacum_full, axis=1), et) - dacum_t.T
        alast = pre["acum"][Q - 1:Q, :]
        dte = jnp.exp(alast - pre["acum"])
        ddte = _dot_hi(jnp.concatenate(ddte_full, axis=1), et) * dte
        dacum = dacum - ddte
        dcd_col = jnp.sum(_dot_hi(e, gst * s_in), axis=1, keepdims=True)
        dcd_row = jnp.broadcast_to(dcd_col, (128, 128)).T[0:1, :]
        dalast = _colsum(ddte) + dcd_row * jnp.exp(alast)
        dacum = dacum + jnp.where(_iota((Q, 1), 0) == Q - 1, dalast, 0.0)
        ddt = _dot_hi(dxdt * xs, et)
        dxs = dxs + dxdt * pre["dt_full"]
        dda = _dot_hi(pre["tri"].T, dacum)
        ddt = ddt + dda * pre["a_row"]
        ghead_ref[1:2, :] += _colsum(dda * pre["dt"])
        ddtr = ddt * _sigmoid(pre["pre"]) * pre["hmask"]
        ghead_ref[0:1, :] += _colsum(ddtr)
        ddt_ref[...] = ddtr

        dxc = jnp.concatenate([dxs] + dbs + dcs, axis=1)
        sgu = pre["sgu"]
        du = dxc * sgu * (1.0 + uf * (1.0 - sgu))
        shu = _conv_shifts(du, ndu_ref[...], up=True)
        dxr = cw_ref[3:4, :] * du + cw_ref[2:3, :] * shu[0] + cw_ref[1:2, :] * shu[1] + cw_ref[0:1, :] * shu[2]
        ndu_ref[...] = du
        dxbc_ref[...] = dxr.astype(BF)
        xr = xc_ref[...].astype(F32)
        xprev = jnp.where(step == nc - 1, 0.0, xp_ref[...].astype(F32))
        shx = _conv_shifts(xr, xprev, up=False)
        gconv_ref[3:4, :] += _colsum(du * xr)
        gconv_ref[2:3, :] += _colsum(du * shx[0])
        gconv_ref[1:2, :] += _colsum(du * shx[1])
        gconv_ref[0:1, :] += _colsum(du * shx[2])
        gconv_ref[4:5, :] += _colsum(du)

        @pl.when(step == nc - 1)
        def _():
            ghead_ref[2:3, :] = ghead_ref[1:2, :] * pre["a_row"]
            ghead_ref[3:4, :] = _dot_hi(glane_ref[...], et)[0:1, :]

    rev = lambda i: (nc - 1 - i, 0)
    outs = [SDS((L, D), BF), SDS((L, 1536), BF), SDS((L, 128), F32), SDS((8, 1536), F32), SDS((8, 128), F32),
            SDS((8, D), F32)]
    return pl.pallas_call(
        body, grid=(nc,), name="ssd_bwd",
        in_specs=[pl.BlockSpec((Q, D), rev), pl.BlockSpec((Q, D), rev), pl.BlockSpec((Q, D), rev),
                  pl.BlockSpec((Q, 1536), rev), pl.BlockSpec((Q, 1536), rev),
                  pl.BlockSpec((Q, 1536), lambda i: (jnp.maximum(nc - 2 - i, 0), 0)),
                  pl.BlockSpec((Q, 128), rev), pl.BlockSpec((1, D, 128), lambda i: (nc - 1 - i, 0, 0)),
                  _const((4, 1536)), _const((1, 128)), _const((1, 128)), _const((1, D)), _const((1, D))],
        out_specs=[pl.BlockSpec((Q, D), rev), pl.BlockSpec((Q, 1536), rev), pl.BlockSpec((Q, 128), rev),
                   _const((8, 1536)), _const((8, 128)), _const((8, D))],
        out_shape=outs, scratch_shapes=[pltpu.VMEM((D, 128), F32), pltpu.VMEM((Q, 1536), F32)],
        compiler_params=_cparams())(dya, y, z, u, xbc, xbc, dtr, states, conv_w, dtb, alog, dskip_full, nw)


def _hg_gates(hq, hf, hlb):
    h0, h1 = hlb[0:1, :], hlb[1:2, :]
    mx = jnp.maximum(h0, h1)
    e0, e1 = jnp.exp(h0 - mx), jnp.exp(h1 - mx)
    lb = e0 / (e0 + e1)
    sg = _sigmoid(hf)
    fg = lb + (1.0 - lb) * sg
    tri = (_iota((Q, Q), 1) <= _iota((Q, Q), 0)).astype(F32)
    return hq * _sigmoid(hq), 1.0 - fg, fg, sg, lb, e1 / (e0 + e1), _dot_hi(tri, jnp.log(fg))


def _hg_intra(b, q, k):
    rowblk = jnp.right_shift(_iota((Q, 1), 0), 4)
    refs = [b[SUB * i + SUB // 2:SUB * i + SUB // 2 + 1, :] for i in range(NSUB)]
    rfull = jnp.concatenate([jnp.broadcast_to(r, (SUB, 128)) for r in refs], axis=0)
    eq = jnp.exp(b - rfull)
    qt_rows = q * eq
    qt = jnp.concatenate([jnp.where(rowblk == i, qt_rows, 0.0) for i in range(NSUB)], axis=1).astype(BF)
    eks = [jnp.exp(jnp.where(rowblk <= i, refs[i] - b, -1e30)) for i in range(NSUB)]
    kt = jnp.concatenate([k * ek for ek in eks], axis=1).astype(BF)
    causal = _iota((Q, Q), 1) <= _iota((Q, Q), 0)
    att = jnp.where(causal, _dot_nt(qt, kt), 0.0)
    return att, qt, kt, eq, eks, rowblk, causal


def _hg_fwd(hq, hf, hi, hg, hlb, nw):
    L = hq.shape[0]
    nc = L // Q

    def body(hq_ref, hf_ref, hi_ref, hg_ref, hlb_ref, nw_ref, ob_ref, o_ref, st_ref, s_ref):
        @pl.when(pl.program_id(0) == 0)
        def _():
            s_ref[...] = jnp.zeros_like(s_ref)

        qf, kf, _, _, _, _, bcum = _hg_gates(hq_ref[...].astype(F32), hf_ref[...], hlb_ref[...])
        gate = hg_ref[...].astype(F32)
        for h in range(NH_HG):
            sl = slice(128 * h, 128 * h + 128)
            b, q, k, v = bcum[:, sl], qf[:, sl], kf[:, sl], hi_ref[:, sl]
            att = _hg_intra(b, q, k)[0]
            s = s_ref[sl, :]
            st_ref[0, sl, :] = s
            o = _dot(att, v) + _dot(q * jnp.exp(b), s)
            blast = b[Q - 1:Q, :]
            s_ref[sl, :] = s * jnp.exp(b.T[:, Q - 1:Q]) + _dot_tn(k * jnp.exp(blast - b), v)
            ob = o.astype(BF)
            o_ref[:, sl] = ob
            on, _, _ = _rms(ob.astype(F32), nw_ref[...])
            gt = gate[:, sl]
            ob_ref[:, sl] = (on * gt * _sigmoid(gt)).astype(BF)

    outs = [SDS((L, D), BF), SDS((L, D), BF), SDS((nc, D, 128), F32)]
    return pl.pallas_call(
        body, grid=(nc,), name="hg_fwd",
        in_specs=[_rows(Q, D), _rows(Q, D), _rows(Q, D), _rows(Q, D), _const((2, D)), _const((1, 128))],
        out_specs=[_rows(Q, D), _rows(Q, D), pl.BlockSpec((1, D, 128), lambda i: (i, 0, 0))],
        out_shape=outs, scratch_shapes=[pltpu.VMEM((D, 128), F32)],
        compiler_params=_cparams())(hq, hf, hi, hg, hlb, nw)


def _hg_bwd(dob, o, hq, hf, hi, hg, states, hlb, nw):
    L = dob.shape[0]
    nc = L // Q

    def body(dob_ref, o_ref, hq_ref, hf_ref, hi_ref, hg_ref, st_ref, hlb_ref, nw_ref,
             dhq_ref, dhf_ref, dhi_ref, dhg_ref, acc_ref, gs_ref):
        step = pl.program_id(0)

        @pl.when(step == 0)
        def _():
            acc_ref[...] = jnp.zeros_like(acc_ref)
            gs_ref[...] = jnp.zeros_like(gs_ref)

        hqv = hq_ref[...].astype(F32)
        qf, kf, fg, sg, lb, sm1, bcum = _hg_gates(hqv, hf_ref[...], hlb_ref[...])
        gate = hg_ref[...].astype(F32)
        sgg = _sigmoid(gate)
        nwv = nw_ref[...]
        tri_t = (_iota((Q, Q), 1) >= _iota((Q, Q), 0)).astype(F32)
        ones8 = jnp.ones((8, 128), F32)
        dqs, dks, dgls, dnws = [], [], [], []
        for h in range(NH_HG):
            sl = slice(128 * h, 128 * h + 128)
            b, q, k, v = bcum[:, sl], qf[:, sl], kf[:, sl], hi_ref[:, sl]
            gt, sgt = gate[:, sl], sgg[:, sl]
            ov = o_ref[:, sl].astype(F32)
            _, n, r = _rms(ov, nwv)
            dobv = dob_ref[:, sl].astype(F32)
            dhg_ref[:, sl] = (dobv * n * nwv * sgt * (1.0 + gt * (1.0 - sgt))).astype(BF)
            do, dw = _rms_bwd(dobv * gt * sgt, n, r, nwv)
            dnws.append(_colsum(dw))
            dob_h = do.astype(BF)
            att, qt, kt, eq, eks, rowblk, causal = _hg_intra(b, q, k)
            s, gst = st_ref[0, sl, :], gs_ref[sl, :]
            eb = jnp.exp(b)
            blast = b[Q - 1:Q, :]
            eblast_col = jnp.exp(b.T[:, Q - 1:Q])
            ekl = jnp.exp(blast - b)
            qhat, khat = q * eb, k * ekl
            dqhat = _dot_nt(dob_h, s)
            da = jnp.where(causal, _dot_nt(dob_h, v), 0.0).astype(BF)
            dhi_ref[:, sl] = (_dot_tn(att, dob_h) + _dot(khat, gst)).astype(BF)
            dkhat = _dot_nt(v, gst)
            dqt = jnp.dot(da, kt, preferred_element_type=F32)
            dkt = lax.dot_general(da, qt, (((0,), (0,)), ((), ())), preferred_element_type=F32)
            dq = dqhat * eb
            dk = dkhat * ekl
            qhat_r, khat_r = qhat.astype(BF).astype(F32), khat.astype(BF).astype(F32)
            db = qhat_r * dqhat - khat_r * dkhat
            for i in range(NSUB):
                bl = slice(128 * i, 128 * i + 128)
                dq = dq + jnp.where(rowblk == i, dqt[:, bl], 0.0) * eq
                dk = dk + dkt[:, bl] * eks[i]
                db = db + qt[:, bl].astype(F32) * dqt[:, bl] - kt[:, bl].astype(F32) * dkt[:, bl]
            dblast = _colsum(dkhat * khat_r) + lax.dot_general(
                ones8, gst * s, (((1,), (1,)), ((), ())), precision=HI, preferred_element_type=F32)[0:1, :] * jnp.exp(blast)
            db = db + jnp.where(_iota((Q, 1), 0) == Q - 1, dblast, 0.0)
            dgls.append(_dot_hi(tri_t, db))
            gs_ref[sl, :] = _dot_tn(qhat, dob_h) + gst * eblast_col
            dqs.append(dq)
            dks.append(dk)
        dq, dk, dgl = (jnp.concatenate(t, axis=1) for t in (dqs, dks, dgls))
        sgq = _sigmoid(hqv)
        dhq_ref[...] = (dq * sgq * (1.0 + hqv * (1.0 - sgq))).astype(BF)
        dfg = dgl / fg - dk
        dhf_ref[...] = (dfg * (1.0 - lb) * sg * (1.0 - sg)).astype(BF)
        acc_ref[0:1, :] += _colsum(dfg * (1.0 - sg))
        acc_ref[1:2, :] += jnp.concatenate(dnws, axis=1)

        @pl.when(step == nc - 1)
        def _():
            dlb = acc_ref[0:1, :] * lb * sm1
            acc_ref[2:3, :] = dlb
            acc_ref[3:4, :] = -dlb
            tot = acc_ref[1:2, 0:128]
            for h in range(1, NH_HG):
                tot = tot + acc_ref[1:2, 128 * h:128 * h + 128]
            acc_ref[4:5, 0:128] = tot

    rev = lambda i: (nc - 1 - i, 0)
    outs = [SDS((L, D), BF)] * 4 + [SDS((8, D), F32)]
    return pl.pallas_call(
        body, grid=(nc,), name="hg_bwd",
        in_specs=[pl.BlockSpec((Q, D), rev)] * 6 + [pl.BlockSpec((1, D, 128), lambda i: (nc - 1 - i, 0, 0)),
                                                    _const((2, D)), _const((1, 128))],
        out_specs=[pl.BlockSpec((Q, D), rev)] * 4 + [_const((8, D))],
        out_shape=outs, scratch_shapes=[pltpu.VMEM((D, 128), F32)],
        compiler_params=_cparams())(dob, o, hq, hf, hi, hg, states, hlb, nw)


def _pad_lanes(v, n=128):
    return jnp.pad(v, ((0, 0), (0, n - v.shape[1])))


def _permute_w_in(w):
    return jnp.concatenate([w[:, :2560], w[:, 2576:], w[:, 2560:2576], jnp.zeros((D, NINP - N_IN), w.dtype)], axis=1)


def _local_step(x, mem, tgt, wb, ws):
    w_in_p = _permute_w_in(wb["w_in"])
    dtb, alog = _pad_lanes(ws["dt_bias"]), _pad_lanes(ws["a_log"])
    dskip_full = jnp.repeat(ws["d_skip"], SSD_P, axis=1)
    conv_w, conv_b = ws["conv_w"][0], ws["conv_b"]

    h0, z, xbc, hq, hf, hi, hg, dtr = _in_proj(x, ws["norm_mix_w"], w_in_p)
    ya, yssd, u, st_ssd = _ssd_fwd(xbc, dtr, z, conv_w, conv_b, dtb, alog, dskip_full, ws["ssd_norm_w"])
    ob, ohg, st_hg = _hg_fwd(hq, hf, hi, hg, ws["hg_lower_bounds"], ws["hg_norm_w"])
    mb, kmem, vmem = _mem_kv(mem, ws["norm_mem_w"], wb["xa_wkv"])
    x1, x2, hxa, q, ox = _attn_fwd(x, ya, ob, wb["w_out"], ws["norm_xa_w"], wb["xa_wq"], kmem, vmem, wb["xa_wo"])
    nfin = ws["norm_final_w"].reshape(1, D)
    dx2, hffn, act, dx3, dg, du, acc_f = _ffn_loss(x2, tgt, ws["norm_ffn_w"], nfin, wb["ffn_w_gate"], wb["ffn_w_up"],
                                                   wb["ffn_w_down"])
    dx1, dya, dob, dq, dk, dv, acc_a = _attn_bwd(dx2, x1, q, kmem, vmem, ws["norm_xa_w"], wb["xa_wq"], wb["xa_wo"],
                                                 wb["w_out"])
    g_nmem, dkv = _mem_kv_bwd(mem, ws["norm_mem_w"], wb["xa_wkv"], dk, dv)
    dhq, dhf, dhi, dhg, acc_h = _hg_bwd(dob, ohg, hq, hf, hi, hg, st_hg, ws["hg_lower_bounds"], ws["hg_norm_w"])
    dz, dxbc, ddt, gconv, ghead, glane = _ssd_bwd(dya, yssd, z, u, xbc, dtr, st_ssd, conv_w, dtb, alog, dskip_full,
                                                  ws["ssd_norm_w"])
    gx, acc_i = _in_proj_bwd(x, dx1, dz, dxbc, dhq, dhf, dhi, dhg, ddt, ws["norm_mix_w"], w_in_p)

    parts = [_matmul_tn(h0, d_, "gw_in_" + n_) for d_, n_ in
             ((dz, "z"), (dxbc, "xbc"), (ddt, "dt"), (dhq, "hq"), (dhf, "hf"), (dhi, "hi"), (dhg, "hg"))]
    parts[2] = parts[2][:, :NH_SSD]
    mixed = jnp.concatenate([ya, ob], axis=1)
    gb = {
        "w_in": jnp.concatenate(parts, axis=1),
        "w_out": _matmul_tn(mixed, dx1, "gw_out"),
        "xa_wq": _matmul_tn(hxa, dq, "gw_q"),
        "xa_wkv": _matmul_tn(mb, dkv, "gw_kv"),
        "xa_wo": _matmul_tn(ox, dx2, "gw_o"),
        "ffn_w_gate": _matmul_tn(hffn, dg, "gw_gate"),
        "ffn_w_up": _matmul_tn(hffn, du, "gw_up"),
        "ffn_w_down": _matmul_tn(act, dx3, "gw_down"),
    }
    gs = {
        "norm_mix_w": acc_i[0:1], "conv_w": gconv[0:4][None], "conv_b": gconv[4:5],
        "dt_bias": ghead[0:1, :NH_SSD], "a_log": ghead[2:3, :NH_SSD], "d_skip": ghead[3:4, :NH_SSD],
        "ssd_norm_w": glane[1:2], "hg_lower_bounds": acc_h[2:4], "hg_norm_w": acc_h[4:5, :128],
        "norm_xa_w": acc_a[0:1], "norm_mem_w": g_nmem, "norm_ffn_w": acc_f[2:3], "norm_final_w": acc_f[1],
    }
    loss = (0.5 / D) * jnp.sum(acc_f[0])
    return loss, gx, gb, gs


def _place():
    return lax.axis_index("x"), lax.axis_index("y"), lax.axis_index("c")


def _allgather8(blocks, name):
    n = len(blocks)

    def body(*refs):
        ins, outs = refs[:n], refs[n:2 * n]
        send_sems, recv_sems, local_sems = refs[2 * n:]
        x, y, c = _place()
        me, sibling = (x, y, c), (x, y, 1 - c)
        chips = [(1 - x, y), (x, 1 - y), (1 - x, 1 - y)]
        waits_recv, waits_send, locals_ = [], [], []
        for a in range(n):
            out = outs[a]

            def slot(p, out=out):
                return out.at[4 * p[0] + 2 * p[1] + p[2]]

            def copy(k, block, to, src=None, a=a, slot=slot):
                return pltpu.make_async_remote_copy(
                    src_ref=slot(block) if src is None else src, dst_ref=slot(block),
                    send_sem=send_sems.at[7 * a + k], recv_sem=recv_sems.at[7 * a + k],
                    device_id=to, device_id_type=MESH)

            mine = pltpu.make_async_copy(ins[a], slot(me), local_sems.at[a])
            mine.start()
            locals_.append(mine)
            first = [copy(0, me, sibling, src=ins[a])]
            first += [copy(1 + j, me, (*chip, c), src=ins[a]) for j, chip in enumerate(chips)]
            for cp in first:
                cp.start()
            passed = [copy(4 + j, (*chip, c), sibling) for j, chip in enumerate(chips)]
            for j, chip in enumerate(chips):
                copy(1 + j, (*chip, c), me).wait_recv()
                passed[j].start()
            waits_recv.append(copy(0, sibling, me))
            waits_recv += [copy(4 + j, (*chip, 1 - c), me) for j, chip in enumerate(chips)]
            waits_send += first + passed
        for cp in waits_recv:
            cp.wait_recv()
        for cp in waits_send:
            cp.wait_send()
        for cp in locals_:
            cp.wait()

    return pl.pallas_call(
        body, name=name, in_specs=[ANY] * n, out_specs=[ANY] * n,
        out_shape=[SDS((8,) + b.shape, b.dtype) for b in blocks],
        scratch_shapes=[pltpu.SemaphoreType.DMA((7 * n,)), pltpu.SemaphoreType.DMA((7 * n,)),
                        pltpu.SemaphoreType.DMA((n,))])(*blocks)


def _exchange_sibling(src, name):
    def body(src_ref, dst_ref, send_sem, recv_sem):
        x, y, c = _place()
        cp = pltpu.make_async_remote_copy(src_ref=src_ref, dst_ref=dst_ref, send_sem=send_sem, recv_sem=recv_sem,
                                          device_id=(x, y, 1 - c), device_id_type=MESH)
        cp.start()
        cp.wait()

    return pl.pallas_call(
        body, name=name, in_specs=[ANY], out_specs=ANY, out_shape=SDS(src.shape, src.dtype),
        scratch_shapes=[pltpu.SemaphoreType.DMA, pltpu.SemaphoreType.DMA])(src)


def _exchange_chips(src, name):
    def body(src_ref, dst_ref, send_sems, recv_sems):
        x, y, c = _place()
        chips = [(1 - x, y), (x, 1 - y), (1 - x, 1 - y)]
        cps = [pltpu.make_async_remote_copy(
            src_ref=src_ref.at[2 * px + py], dst_ref=dst_ref.at[k], send_sem=send_sems.at[k], recv_sem=recv_sems.at[k],
            device_id=(px, py, c), device_id_type=MESH) for k, (px, py) in enumerate(chips)]
        for cp in cps:
            cp.start()
        for cp in cps:
            cp.wait()

    return pl.pallas_call(
        body, name=name, in_specs=[ANY], out_specs=ANY, out_shape=SDS((3,) + src.shape[1:], src.dtype),
        scratch_shapes=[pltpu.SemaphoreType.DMA((3,)), pltpu.SemaphoreType.DMA((3,))])(src)


def _row_tile(rows, cols, nbuf):
    cap = max(8, (VMEM_LIMIT // 3) // (2 * nbuf * cols * 4))
    best = rows if rows % 8 else 8
    for t in range(8, min(rows, cap) + 1, 8):
        if rows % t == 0:
            best = t
    return best


def _add_pair(a, b, name):
    R = a.shape[0]
    tr = _row_tile(R, 128, 4)

    def body(a_ref, b_ref, o_ref, ob_ref):
        s = a_ref[...] + b_ref[...]
        o_ref[...] = s
        ob_ref[...] = s.astype(BF)

    return pl.pallas_call(
        body, grid=(R // tr,), name=name, in_specs=[_rows(tr, 128)] * 2, out_specs=[_rows(tr, 128)] * 2,
        out_shape=[SDS((R, 128), F32), SDS((R, 128), BF)], compiler_params=_cparams())(a, b)


def _sum_parts(own, parts, name):
    R = own.shape[0]
    P = parts.shape[0]
    tr = _row_tile(R, 128, 2 + P)

    def body(own_ref, p_ref, o_ref):
        s = own_ref[...]
        for k in range(P):
            s = s + p_ref[k].astype(F32)
        o_ref[...] = s

    return pl.pallas_call(
        body, grid=(R // tr,), name=name,
        in_specs=[_rows(tr, 128), pl.BlockSpec((P, tr, 128), lambda i: (0, i, 0))], out_specs=_rows(tr, 128),
        out_shape=SDS((R, 128), F32), compiler_params=_cparams())(own, parts)


def _sum8(parts, name):
    R = parts.shape[1]

    def body(p_ref, o_ref):
        s = p_ref[0]
        for k in range(1, 8):
            s = s + p_ref[k]
        o_ref[...] = s

    return pl.pallas_call(
        body, grid=(1,), name=name, in_specs=[_const((8, R, 128))], out_specs=_const((R, 128)),
        out_shape=SDS((R, 128), F32), compiler_params=_cparams())(parts)


def _adamw(w, g, m, v, name):
    R, C = w.shape
    tr = _row_tile(R, C, 7)
    c1 = 1.0 / (1.0 - ADAM_B1 ** ADAM_STEP)
    c2 = 1.0 / (1.0 - ADAM_B2 ** ADAM_STEP)

    def body(w_ref, g_ref, m_ref, v_ref, d_ref, nm_ref, nv_ref):
        gv = g_ref[...]
        nm = ADAM_B1 * m_ref[...] + (1.0 - ADAM_B1) * gv
        nv = ADAM_B2 * v_ref[...] + (1.0 - ADAM_B2) * gv * gv
        nm_ref[...] = nm
        nv_ref[...] = nv
        d_ref[...] = -ADAM_LR * ((nm * c1) / (jnp.sqrt(nv * c2) + ADAM_EPS) + ADAM_WD * w_ref[...])

    return pl.pallas_call(
        body, grid=(R // tr,), name=name, in_specs=[_rows(tr, C)] * 4, out_specs=[_rows(tr, C)] * 3,
        out_shape=[SDS((R, C), F32)] * 3, compiler_params=_cparams())(w, g, m, v)


def _half_shape(shape, axis):
    r, c = shape
    return (r // 2, c // 4) if axis == 1 else (r // 8, c)


def _pack_halves(shards, c):
    flat = []
    for name, shape, axis in BIG:
        hr = _half_shape(shape, axis)[0]
        flat.append(lax.dynamic_slice_in_dim(shards[name], c * hr, hr, 0).reshape(-1))
    return jnp.concatenate(flat).reshape(SLAB_ROWS, 128)


def _unpack_full(slabs):
    flat = slabs.reshape(4, 2, -1)
    out, off = {}, 0
    for name, shape, axis in BIG:
        hr, hc = _half_shape(shape, axis)
        piece = flat[:, :, off:off + hr * hc].reshape(4, 2 * hr, hc)
        off += hr * hc
        out[name] = piece.transpose(1, 0, 2).reshape(shape) if axis == 1 else piece.reshape(shape)
    return out


def _pack_grads(gb):
    flat = []
    for name, shape, axis in BIG:
        r, c = shape
        g = gb[name]
        if axis == 1:
            flat.append(g.reshape(2, r // 2, 4, c // 4).transpose(0, 2, 1, 3).reshape(2, 4, -1))
        else:
            flat.append(g.reshape(4, 2, -1).transpose(1, 0, 2))
    return jnp.concatenate(flat, axis=2)


def _unpack_shards(own, sib, c):
    own, sib = own.reshape(-1), sib.reshape(-1)
    out, off = {}, 0
    for name, shape, axis in BIG:
        hr, hc = _half_shape(shape, axis)
        a, b = own[off:off + hr * hc].reshape(hr, hc), sib[off:off + hr * hc].reshape(hr, hc)
        off += hr * hc
        out[name] = jnp.concatenate([jnp.where(c == 0, a, b), jnp.where(c == 0, b, a)], axis=0)
    return out


def _pack_small(parts):
    rows = []
    for p in parts:
        p = p.reshape(-1)
        rows.append(jnp.pad(p, (0, (-p.shape[0]) % 128)).reshape(-1, 128))
    out = jnp.concatenate(rows, axis=0)
    return jnp.pad(out, ((0, (-out.shape[0]) % 8), (0, 0)))


def _unpack_small(packed, shapes):
    out, row = [], 0
    for shp in shapes:
        n = 1
        for s in shp:
            n *= s
        nr = -(-n // 128)
        out.append(packed[row:row + nr].reshape(-1)[:n].reshape(shp))
        row += nr
    return out


def kernel(x, mem, norm_mix_w, w_in, conv_w, conv_b, dt_bias, a_log, d_skip, ssd_norm_w, hg_lower_bounds, hg_norm_w, w_out, norm_xa_w, norm_mem_w, xa_wq, xa_wkv, xa_wo, norm_ffn_w, ffn_w_gate, ffn_w_up, ffn_w_down, norm_final_w, loss_target, m_norm_mix_w, m_w_in, m_conv_w, m_conv_b, m_dt_bias, m_a_log, m_d_skip, m_ssd_norm_w, m_hg_lower_bounds, m_hg_norm_w, m_w_out, m_norm_xa_w, m_norm_mem_w, m_xa_wq, m_xa_wkv, m_xa_wo, m_norm_ffn_w, m_ffn_w_gate, m_ffn_w_up, m_ffn_w_down, m_norm_final_w, v_norm_mix_w, v_w_in, v_conv_w, v_conv_b, v_dt_bias, v_a_log, v_d_skip, v_ssd_norm_w, v_hg_lower_bounds, v_hg_norm_w, v_w_out, v_norm_xa_w, v_norm_mem_w, v_xa_wq, v_xa_wkv, v_xa_wo, v_norm_ffn_w, v_ffn_w_gate, v_ffn_w_up, v_ffn_w_down, v_norm_final_w):
    w = dict(norm_mix_w=norm_mix_w, w_in=w_in, conv_w=conv_w, conv_b=conv_b, dt_bias=dt_bias, a_log=a_log, d_skip=d_skip,
             ssd_norm_w=ssd_norm_w, hg_lower_bounds=hg_lower_bounds, hg_norm_w=hg_norm_w, w_out=w_out,
             norm_xa_w=norm_xa_w, norm_mem_w=norm_mem_w, xa_wq=xa_wq, xa_wkv=xa_wkv, xa_wo=xa_wo, norm_ffn_w=norm_ffn_w,
             ffn_w_gate=ffn_w_gate, ffn_w_up=ffn_w_up, ffn_w_down=ffn_w_down, norm_final_w=norm_final_w)
    m = dict(norm_mix_w=m_norm_mix_w, w_in=m_w_in, conv_w=m_conv_w, conv_b=m_conv_b, dt_bias=m_dt_bias, a_log=m_a_log,
             d_skip=m_d_skip, ssd_norm_w=m_ssd_norm_w, hg_lower_bounds=m_hg_lower_bounds, hg_norm_w=m_hg_norm_w,
             w_out=m_w_out, norm_xa_w=m_norm_xa_w, norm_mem_w=m_norm_mem_w, xa_wq=m_xa_wq, xa_wkv=m_xa_wkv,
             xa_wo=m_xa_wo, norm_ffn_w=m_norm_ffn_w, ffn_w_gate=m_ffn_w_gate, ffn_w_up=m_ffn_w_up,
             ffn_w_down=m_ffn_w_down, norm_final_w=m_norm_final_w)
    v = dict(norm_mix_w=v_norm_mix_w, w_in=v_w_in, conv_w=v_conv_w, conv_b=v_conv_b, dt_bias=v_dt_bias, a_log=v_a_log,
             d_skip=v_d_skip, ssd_norm_w=v_ssd_norm_w, hg_lower_bounds=v_hg_lower_bounds, hg_norm_w=v_hg_norm_w,
             w_out=v_w_out, norm_xa_w=v_norm_xa_w, norm_mem_w=v_norm_mem_w, xa_wq=v_xa_wq, xa_wkv=v_xa_wkv,
             xa_wo=v_xa_wo, norm_ffn_w=v_norm_ffn_w, ffn_w_gate=v_ffn_w_gate, ffn_w_up=v_ffn_w_up,
             ffn_w_down=v_ffn_w_down, norm_final_w=v_norm_final_w)
    xi, yi, ci = _place()
    chip = 2 * xi + yi

    shards = {name: w[name][0] for name, _, _ in BIG}
    slabs, conv_all = _allgather8([_pack_halves(shards, ci).astype(BF), conv_w[0]], "gather_weights")
    wb = _unpack_full(slabs)
    ws = {name: w[name] for name in SMALL}
    ws["conv_w"] = conv_all[0::2].transpose(1, 0, 2).reshape(1, 4, 1536)

    loss, gx, gb, gs = _local_step(x[0], mem[0], loss_target[0], wb, ws)

    packed = _pack_grads(gb)
    own = lax.dynamic_index_in_dim(packed, ci, 0, keepdims=False).reshape(4 * SLAB_ROWS, 128)
    sib = lax.dynamic_index_in_dim(packed, 1 - ci, 0, keepdims=False).reshape(4 * SLAB_ROWS, 128)
    from_sib = _exchange_sibling(sib, "grads_to_sibling")
    chip_f32, chip_bf = _add_pair(own, from_sib, "grads_chip_sum")
    others = _exchange_chips(chip_bf.reshape(4, SLAB_ROWS, 128), "grads_to_chips")
    mine = lax.dynamic_index_in_dim(chip_f32.reshape(4, SLAB_ROWS, 128), chip, 0, keepdims=False)
    reduced = _sum_parts(mine, others, "grads_total")
    reduced_sib = _exchange_sibling(reduced, "grads_half_to_sibling")
    g_big = _unpack_shards(reduced, reduced_sib, ci)

    small_parts = [gs[name] for name in SMALL] + [loss.reshape(1)]
    small_shapes = [gs[name].shape for name in SMALL] + [(1,)]
    gathered = _allgather8([_pack_small(small_parts)], "gather_small")[0]
    small = _unpack_small(_sum8(gathered, "small_total"), small_shapes)
    g_small = dict(zip(SMALL, small[:-1]))
    loss_all = small[-1][0]
    g_small["conv_w"] = lax.dynamic_slice_in_dim(g_small["conv_w"], chip * 384, 384, 2)

    grads, delta, new_m, new_v = {}, {}, {}, {}
    for name, _, _ in BIG:
        grads[name] = g_big[name][None]
        d_, nm_, nv_ = _adamw(w[name][0], g_big[name], m[name][0], v[name][0], "adamw_" + name)
        delta[name], new_m[name], new_v[name] = d_[None], nm_[None], nv_[None]
    shapes = [w[name].shape for name in SMALL]
    packs = [_pack_small([t[name] for name in SMALL]) for t in (w, g_small, m, v)]
    outs = _adamw(*packs, "adamw_small")
    for name, g_, d_, nm_, nv_ in zip(SMALL, [g_small[n] for n in SMALL], *[_unpack_small(o, shapes) for o in outs]):
        grads[name] = g_.reshape(w[name].shape)
        delta[name], new_m[name], new_v[name] = d_, nm_, nv_

    return (loss_all, gx[None], *[grads[n] for n in WEIGHTS], *[delta[n] for n in WEIGHTS],
            *[new_m[n] for n in WEIGHTS], *[new_v[n] for n in WEIGHTS])
```

```python
import jax
import jax.numpy as jnp
from jax import lax
from jax.experimental import pallas as pl
from jax.experimental.pallas import tpu as pltpu

F32 = jnp.float32
BF = jnp.bfloat16
HI = lax.Precision.HIGHEST
MESH = pl.DeviceIdType.MESH
SDS = jax.ShapeDtypeStruct
ANY = pl.BlockSpec(memory_space=pl.ANY)

D = 1024
EPS = 1e-6
NH_SSD = 16
SSD_P = 64
NH_HG = 8
Q = 128
SUB = 16
NSUB = Q // SUB
XA_HEADS = 4
XA_HD = 256
MEM_LEN = 256
FFN = 2816
FC = FFN // 4
TL = 256
VMEM_LIMIT = 56 << 20

Z0, XBC0, HQ0, HF0, HI0, HG0, DT0, NINP = 0, 1024, 2560, 3584, 4608, 5632, 6656, 6784
N_IN = 6672

ADAM_LR, ADAM_B1, ADAM_B2, ADAM_EPS, ADAM_WD, ADAM_STEP = 0.001, 0.9, 0.999, 1e-08, 0.01, 10

BIG = ("w_in", "w_out", "xa_wq", "xa_wkv", "xa_wo", "ffn_w_gate", "ffn_w_up", "ffn_w_down")
SMALL = ("norm_mix_w", "conv_w", "conv_b", "dt_bias", "a_log", "d_skip", "ssd_norm_w", "hg_lower_bounds",
         "hg_norm_w", "norm_xa_w", "norm_mem_w", "norm_ffn_w", "norm_final_w")
WEIGHTS = ("norm_mix_w", "w_in", "conv_w", "conv_b", "dt_bias", "a_log", "d_skip", "ssd_norm_w", "hg_lower_bounds",
           "hg_norm_w", "w_out", "norm_xa_w", "norm_mem_w", "xa_wq", "xa_wkv", "xa_wo", "norm_ffn_w", "ffn_w_gate",
           "ffn_w_up", "ffn_w_down", "norm_final_w")


def _cparams():
    return pltpu.CompilerParams(dimension_semantics=("arbitrary",), vmem_limit_bytes=VMEM_LIMIT)


def _const(shape):
    return pl.BlockSpec(shape, lambda i: (0,) * len(shape))


def _rows(tl, n):
    return pl.BlockSpec((tl, n), lambda i: (i, 0))


def _dot(a, b):
    return jnp.dot(a.astype(BF), b.astype(BF), preferred_element_type=F32)


def _dot_nt(a, b):
    return lax.dot_general(a.astype(BF), b.astype(BF), (((1,), (1,)), ((), ())), preferred_element_type=F32)


def _dot_tn(a, b):
    return lax.dot_general(a.astype(BF), b.astype(BF), (((0,), (0,)), ((), ())), preferred_element_type=F32)


def _dot_hi(a, b):
    return jnp.dot(a, b, precision=HI, preferred_element_type=F32)


def _split(v, passes):
    parts, rest = [], v
    for p in range(passes):
        hi = rest.astype(BF)
        parts.append(hi)
        if p + 1 < passes:
            rest = rest - hi.astype(F32)
    return parts


def _sel_dot(a, sel, passes=3):
    sb = sel.astype(BF)
    out = None
    for part in _split(a, passes):
        t = jnp.dot(part, sb, preferred_element_type=F32)
        out = t if out is None else out + t
    return out


def _dot_sel(sel, b, passes=3):
    sb = sel.astype(BF)
    out = None
    for part in _split(b, passes):
        t = jnp.dot(sb, part, preferred_element_type=F32)
        out = t if out is None else out + t
    return out


def _iota(shape, dim):
    return lax.broadcasted_iota(jnp.int32, shape, dim)


def _sigmoid(v):
    return 1.0 / (1.0 + jnp.exp(-v))


def _rms(v, w):
    r = lax.rsqrt(jnp.mean(v * v, axis=-1, keepdims=True) + EPS)
    n = v * r
    return n * w, n, r


def _rms_bwd(dy, n, r, w):
    dn = dy * w
    return r * (dn - n * jnp.mean(dn * n, axis=-1, keepdims=True)), dy * n


def _colsum(v):
    return jnp.sum(v, axis=0, keepdims=True)


def _zero_first(*refs):
    @pl.when(pl.program_id(0) == 0)
    def _():
        for r in refs:
            r[...] = jnp.zeros_like(r)


def _in_proj(x, nw, w_p):
    L = x.shape[0]
    tl = min(TL, L)

    def body(x_ref, nw_ref, w_ref, h0_ref, z_ref, xbc_ref, hq_ref, hf_ref, hi_ref, hg_ref, dt_ref):
        h, _, _ = _rms(x_ref[...], nw_ref[...])
        hb = h.astype(BF)
        h0_ref[...] = hb

        def proj(a, b):
            return jnp.dot(hb, w_ref[:, a:b], preferred_element_type=F32)

        z_ref[...] = proj(Z0, XBC0).astype(BF)
        xbc_ref[...] = proj(XBC0, HQ0).astype(BF)
        hq_ref[...] = proj(HQ0, HF0).astype(BF)
        hf_ref[...] = proj(HF0, HI0)
        hi_ref[...] = proj(HI0, HG0).astype(BF)
        hg_ref[...] = proj(HG0, DT0).astype(BF)
        dt_ref[...] = proj(DT0, NINP)

    outs = [SDS((L, D), BF), SDS((L, D), BF), SDS((L, 1536), BF), SDS((L, D), BF), SDS((L, D), F32),
            SDS((L, D), BF), SDS((L, D), BF), SDS((L, 128), F32)]
    return pl.pallas_call(
        body, grid=(L // tl,), name="in_proj",
        in_specs=[_rows(tl, D), _const((1, D)), _const((D, NINP))],
        out_specs=[_rows(tl, o.shape[1]) for o in outs], out_shape=outs,
        compiler_params=_cparams())(x, nw, w_p)


def _mem_kv(mem, nw, wkv4):
    def body(m_ref, nw_ref, w_ref, k_ref, v_ref):
        m, _, _ = _rms(m_ref[...], nw_ref[...])
        mb = m.astype(BF)
        for i in range(2):
            sl = slice(512 * i, 512 * i + 512)
            k_ref[:, sl] = jnp.dot(mb, w_ref[i], preferred_element_type=F32).astype(BF)
            v_ref[:, sl] = jnp.dot(mb, w_ref[2 + i], preferred_element_type=F32).astype(BF)

    outs = [SDS((MEM_LEN, D), BF)] * 2
    return pl.pallas_call(
        body, grid=(1,), name="mem_kv",
        in_specs=[_const((MEM_LEN, D)), _const((1, D)), _const((4, D, 512))],
        out_specs=[_const((MEM_LEN, D))] * 2, out_shape=outs, compiler_params=_cparams())(mem, nw, wkv4)


def _mem_kv_bwd(mem, nw, wkv4, dk, dv):
    def body(m_ref, nw_ref, w_ref, dk_ref, dv_ref, gnw_ref, gw_ref):
        m, n, _ = _rms(m_ref[...], nw_ref[...])
        mb = m.astype(BF)
        dm = jnp.zeros((MEM_LEN, D), F32)
        for i in range(4):
            src = dk_ref if i < 2 else dv_ref
            d = src[:, 512 * (i % 2):512 * (i % 2) + 512].astype(BF)
            gw_ref[i] = _dot_tn(mb, d)
            dm = dm + _dot_nt(d, w_ref[i])
        gnw_ref[...] = _colsum(dm * n)

    return pl.pallas_call(
        body, grid=(1,), name="mem_kv_bwd",
        in_specs=[_const((MEM_LEN, D)), _const((1, D)), _const((4, D, 512)), _const((MEM_LEN, D)), _const((MEM_LEN, D))],
        out_specs=[_const((1, D)), _const((4, D, 512))],
        out_shape=[SDS((1, D), F32), SDS((4, D, 512), F32)], compiler_params=_cparams())(mem, nw, wkv4, dk, dv)


def _softmax_rows(sc):
    e = jnp.exp(sc - jnp.max(sc, axis=-1, keepdims=True))
    return e / jnp.sum(e, axis=-1, keepdims=True)


def _attn_fwd(x, ya, ob, w_out, nxa, wq, k, v, wo):
    L = x.shape[0]
    tl = min(TL, L)
    scale = XA_HD ** -0.5

    def body(x_ref, ya_ref, ob_ref, wout_ref, nxa_ref, wq_ref, k_ref, v_ref, wo_ref,
             x1_ref, x2_ref, hxa_ref, q_ref, ox_ref):
        x1 = x_ref[...] + jnp.dot(ya_ref[...], wout_ref[:D, :], preferred_element_type=F32) \
            + jnp.dot(ob_ref[...], wout_ref[D:, :], preferred_element_type=F32)
        x1_ref[...] = x1
        h, _, _ = _rms(x1, nxa_ref[...])
        hb = h.astype(BF)
        hxa_ref[...] = hb
        qb = jnp.dot(hb, wq_ref[...], preferred_element_type=F32).astype(BF)
        q_ref[...] = qb
        oxs = []
        for hd in range(XA_HEADS):
            sl = slice(hd * XA_HD, (hd + 1) * XA_HD)
            p = _softmax_rows(_dot_nt(qb[:, sl], k_ref[:, sl]) * scale)
            oxs.append(_dot(p, v_ref[:, sl]))
        oxb = jnp.concatenate(oxs, axis=1).astype(BF)
        ox_ref[...] = oxb
        x2_ref[...] = x1 + jnp.dot(oxb, wo_ref[...], preferred_element_type=F32)

    outs = [SDS((L, D), F32), SDS((L, D), F32), SDS((L, D), BF), SDS((L, D), BF), SDS((L, D), BF)]
    return pl.pallas_call(
        body, grid=(L // tl,), name="attn_fwd",
        in_specs=[_rows(tl, D), _rows(tl, D), _rows(tl, D), _const((2 * D, D)), _const((1, D)), _const((D, D)),
                  _const((MEM_LEN, D)), _const((MEM_LEN, D)), _const((D, D))],
        out_specs=[_rows(tl, D)] * 5, out_shape=outs, compiler_params=_cparams())(x, ya, ob, w_out, nxa, wq, k, v, wo)


def _ffn_loss(x2, tgt, nffn, nfin, wg4, wu4, wd4):
    L = x2.shape[0]
    tl = min(TL, L)

    def body(x2_ref, t_ref, nffn_ref, nfin_ref, wg_ref, wu_ref, wd_ref,
             dx2_ref, h_ref, a_ref, dx3_ref, dg_ref, du_ref, acc_ref):
        _zero_first(acc_ref)
        x2v = x2_ref[...]
        h, n2, r2 = _rms(x2v, nffn_ref[...])
        hb = h.astype(BF)
        h_ref[...] = hb
        x3 = x2v
        gs, us, sgs = [], [], []
        for i in range(4):
            g = jnp.dot(hb, wg_ref[i], preferred_element_type=F32)
            u = jnp.dot(hb, wu_ref[i], preferred_element_type=F32)
            sg = _sigmoid(g)
            ab = (g * sg * u).astype(BF)
            a_ref[i] = ab
            x3 = x3 + jnp.dot(ab, wd_ref[i], preferred_element_type=F32)
            gs.append(g)
            us.append(u)
            sgs.append(sg)
        y, n3, r3 = _rms(x3, nfin_ref[...])
        err = y - t_ref[...]
        acc_ref[0:1, :] += _colsum(err * err)
        dx3, dwf = _rms_bwd(err * (1.0 / D), n3, r3, nfin_ref[...])
        acc_ref[1:2, :] += _colsum(dwf)
        dx3b = dx3.astype(BF)
        dx3_ref[...] = dx3b
        dh = jnp.zeros((tl, D), F32)
        for i in range(4):
            g, u, sg = gs[i], us[i], sgs[i]
            da = _dot_nt(dx3b, wd_ref[i])
            dgb = (da * u * sg * (1.0 + g * (1.0 - sg))).astype(BF)
            dub = (da * g * sg).astype(BF)
            dg_ref[i] = dgb
            du_ref[i] = dub
            dh = dh + _dot_nt(dgb, wg_ref[i]) + _dot_nt(dub, wu_ref[i])
        dn, dwn = _rms_bwd(dh, n2, r2, nffn_ref[...])
        acc_ref[2:3, :] += _colsum(dwn)
        dx2_ref[...] = dx3 + dn

    blk = pl.BlockSpec((4, tl, FC), lambda i: (0, i, 0))
    outs = [SDS((L, D), F32), SDS((L, D), BF), SDS((4, L, FC), BF), SDS((L, D), BF), SDS((4, L, FC), BF),
            SDS((4, L, FC), BF), SDS((8, D), F32)]
    wspec = pl.BlockSpec((4, D, FC), lambda i: (0, 0, 0), pipeline_mode=pl.Buffered(1))
    wdspec = pl.BlockSpec((4, FC, D), lambda i: (0, 0, 0), pipeline_mode=pl.Buffered(1))
    return pl.pallas_call(
        body, grid=(L // tl,), name="ffn_loss",
        in_specs=[_rows(tl, D), _rows(tl, D), _const((1, D)), _const((1, D)), wspec, wspec, wdspec],
        out_specs=[_rows(tl, D), _rows(tl, D), blk, _rows(tl, D), blk, blk, _const((8, D))],
        out_shape=outs, compiler_params=_cparams())(x2, tgt, nffn, nfin, wg4, wu4, wd4)


def _attn_bwd(dx2, x1, q, k, v, nxa, wq, wo, w_out):
    L = dx2.shape[0]
    tl = min(TL, L)
    scale = XA_HD ** -0.5

    def body(dx2_ref, x1_ref, q_ref, k_ref, v_ref, nxa_ref, wq_ref, wo_ref, wout_ref,
             dx1_ref, dya_ref, dob_ref, dq_ref, dk_ref, dv_ref, acc_ref):
        _zero_first(dk_ref, dv_ref, acc_ref)
        dx2v = dx2_ref[...]
        dox = _dot_nt(dx2v, wo_ref[...]).astype(BF)
        qb = q_ref[...]
        dqs = []
        for hd in range(XA_HEADS):
            sl = slice(hd * XA_HD, (hd + 1) * XA_HD)
            kh, vh, qh, doh = k_ref[:, sl], v_ref[:, sl], qb[:, sl], dox[:, sl]
            p = _softmax_rows(_dot_nt(qh, kh) * scale)
            dp = _dot_nt(doh, vh)
            dv_ref[:, sl] += _dot_tn(p, doh)
            ds = p * (dp - jnp.sum(dp * p, axis=-1, keepdims=True)) * scale
            dqs.append(_dot(ds, kh))
            dk_ref[:, sl] += _dot_tn(ds, qh)
        dqb = jnp.concatenate(dqs, axis=1).astype(BF)
        dq_ref[...] = dqb
        dh = _dot_nt(dqb, wq_ref[...])
        _, n1, r1 = _rms(x1_ref[...], nxa_ref[...])
        dn, dwn = _rms_bwd(dh, n1, r1, nxa_ref[...])
        acc_ref[0:1, :] += _colsum(dwn)
        dx1 = dx2v + dn
        dx1_ref[...] = dx1
        dx1b = dx1.astype(BF)
        dya_ref[...] = _dot_nt(dx1b, wout_ref[:D, :]).astype(BF)
        dob_ref[...] = _dot_nt(dx1b, wout_ref[D:, :]).astype(BF)

    outs = [SDS((L, D), F32), SDS((L, D), BF), SDS((L, D), BF), SDS((L, D), BF), SDS((MEM_LEN, D), F32),
            SDS((MEM_LEN, D), F32), SDS((8, D), F32)]
    return pl.pallas_call(
        body, grid=(L // tl,), name="attn_bwd",
        in_specs=[_rows(tl, D), _rows(tl, D), _rows(tl, D), _const((MEM_LEN, D)), _const((MEM_LEN, D)), _const((1, D)),
                  _const((D, D)), _const((D, D)), _const((2 * D, D))],
        out_specs=[_rows(tl, D)] * 4 + [_const((MEM_LEN, D)), _const((MEM_LEN, D)), _const((8, D))],
        out_shape=outs, compiler_params=_cparams())(dx2, x1, q, k, v, nxa, wq, wo, w_out)


def _in_proj_bwd(x, dx1, dz, dxbc, dhq, dhf, dhi, dhg, ddt, nw, w_p):
    L = x.shape[0]
    tl = min(TL, L)

    def body(x_ref, dx1_ref, dz_ref, dxbc_ref, dhq_ref, dhf_ref, dhi_ref, dhg_ref, ddt_ref, nw_ref, w_ref,
             gx_ref, acc_ref):
        _zero_first(acc_ref)
        dh = _dot_nt(dz_ref[...], w_ref[:, Z0:XBC0]) + _dot_nt(dxbc_ref[...], w_ref[:, XBC0:HQ0]) \
            + _dot_nt(dhq_ref[...], w_ref[:, HQ0:HF0]) + _dot_nt(dhf_ref[...], w_ref[:, HF0:HI0]) \
            + _dot_nt(dhi_ref[...], w_ref[:, HI0:HG0]) + _dot_nt(dhg_ref[...], w_ref[:, HG0:DT0]) \
            + _dot_nt(ddt_ref[...], w_ref[:, DT0:NINP])
        _, n, r = _rms(x_ref[...], nw_ref[...])
        dn, dwn = _rms_bwd(dh, n, r, nw_ref[...])
        acc_ref[0:1, :] += _colsum(dwn)
        gx_ref[...] = dx1_ref[...] + dn

    return pl.pallas_call(
        body, grid=(L // tl,), name="in_proj_bwd",
        in_specs=[_rows(tl, D), _rows(tl, D), _rows(tl, D), _rows(tl, 1536), _rows(tl, D), _rows(tl, D), _rows(tl, D),
                  _rows(tl, D), _rows(tl, 128), _const((1, D)), _const((D, NINP))],
        out_specs=[_rows(tl, D), _const((8, D))], out_shape=[SDS((L, D), F32), SDS((8, D), F32)],
        compiler_params=_cparams())(x, dx1, dz, dxbc, dhq, dhf, dhi, dhg, ddt, nw, w_p)


def _matmul_tn(a, b, name):
    L, M = a.shape
    N = b.shape[1]
    tl = min(512, L)

    def body(a_ref, b_ref, o_ref):
        _zero_first(o_ref)
        o_ref[...] += _dot_tn(a_ref[...], b_ref[...])

    return pl.pallas_call(
        body, grid=(L // tl,), name=name, in_specs=[_rows(tl, M), _rows(tl, N)], out_specs=_const((M, N)),
        out_shape=SDS((M, N), F32), compiler_params=_cparams())(a, b)


def _matmul_tn_pair(a0, a1, b, name):
    L, M = a0.shape
    N = b.shape[1]
    tl = min(512, L)

    def body(a0_ref, a1_ref, b_ref, o_ref):
        _zero_first(o_ref)
        bv = b_ref[...].astype(BF)
        o_ref[:M, :] += _dot_tn(a0_ref[...], bv)
        o_ref[M:, :] += _dot_tn(a1_ref[...], bv)

    return pl.pallas_call(
        body, grid=(L // tl,), name=name, in_specs=[_rows(tl, M), _rows(tl, M), _rows(tl, N)],
        out_specs=_const((2 * M, N)), out_shape=SDS((2 * M, N), F32), compiler_params=_cparams())(a0, a1, b)


def _matmul_tn_blocks(a, b, name):
    a_blocked = a.ndim == 3
    L = a.shape[-2]
    M, N = a.shape[-1], b.shape[-1]
    tl = min(512, L)

    def body(a_ref, b_ref, o_ref):
        _zero_first(o_ref)
        for i in range(4):
            o_ref[i] += _dot_tn(a_ref[i] if a_blocked else a_ref[...], b_ref[...] if a_blocked else b_ref[i])

    blk = lambda n: pl.BlockSpec((4, tl, n), lambda i: (0, i, 0))
    return pl.pallas_call(
        body, grid=(L // tl,), name=name,
        in_specs=[blk(M) if a_blocked else _rows(tl, M), _rows(tl, N) if a_blocked else blk(N)],
        out_specs=_const((4, M, N)), out_shape=SDS((4, M, N), F32), compiler_params=_cparams())(a, b)


def _head_expand():
    e = (jnp.right_shift(_iota((128, D), 1), 6) == _iota((128, D), 0)).astype(F32)
    et = (jnp.right_shift(_iota((D, 128), 0), 6) == _iota((D, 128), 1)).astype(F32)
    return e, et


def _conv_shifts(cur, other, up):
    rows = _iota((Q, 1), 0)
    out = []
    for s in (1, 2, 3):
        if up:
            out.append(jnp.where(rows >= Q - s, pltpu.roll(other, Q - s, 0), pltpu.roll(cur, Q - s, 0)))
        else:
            out.append(jnp.where(rows < s, pltpu.roll(other, s, 0), pltpu.roll(cur, s, 0)))
    return out


def _ssd_pre(u, dtr, dtb, alog):
    e, et = _head_expand()
    sgu = _sigmoid(u)
    xc = u * sgu
    lane = _iota((1, 128), 1)
    hmask = (lane < NH_SSD).astype(F32)
    pre = dtr + dtb
    dt = (jnp.maximum(pre, 0.0) + jnp.log(1.0 + jnp.exp(-jnp.abs(pre)))) * hmask
    a_row = -jnp.exp(alog)
    causal = _iota((Q, Q), 1) <= _iota((Q, Q), 0)
    tri = causal.astype(F32)
    acum = _dot_sel(tri, dt * a_row)
    acum_full = _sel_dot(acum, e)
    alast_full = acum_full[Q - 1:Q, :]
    dt_full = _sel_dot(dt, e)
    xs = xc[:, :D]
    return dict(e=e, et=et, sgu=sgu, xs=xs, bm=xc[:, D:D + 256], cm=xc[:, D + 256:], hmask=hmask, pre=pre, dt=dt,
                a_row=a_row, causal=causal, tri=tri, acum=acum, acum_t=acum.T, eA_full=jnp.exp(acum_full),
                dte_full=jnp.exp(alast_full - acum_full), dt_full=dt_full, xdt=xs * dt_full)


def _ssd_decay(pre, hh, cb):
    seg = pre["acum"][:, hh:hh + 1] - pre["acum_t"][hh:hh + 1, :]
    lm = jnp.where(pre["causal"], jnp.exp(jnp.minimum(seg, 0.0)), 0.0)
    return lm, cb * lm


def _ssd_fwd(xbc, dtr, z, conv_w, conv_b, dtb, alog, dskip_full, nw):
    L = xbc.shape[0]
    nc = L // Q

    def body(xbc_ref, dtr_ref, z_ref, cw_ref, cb_ref, dtb_ref, alog_ref, dsk_ref, nw_ref,
             ya_ref, y_ref, u_ref, st_ref, prev_ref, s_ref):
        @pl.when(pl.program_id(0) == 0)
        def _():
            prev_ref[...] = jnp.zeros_like(prev_ref)
            s_ref[...] = jnp.zeros_like(s_ref)

        xr = xbc_ref[...].astype(F32)
        sh = _conv_shifts(xr, prev_ref[...], up=False)
        u = cb_ref[...] + cw_ref[3:4, :] * xr + cw_ref[2:3, :] * sh[0] + cw_ref[1:2, :] * sh[1] + cw_ref[0:1, :] * sh[2]
        prev_ref[...] = xr
        ub = u.astype(BF)
        u_ref[...] = ub
        pre = _ssd_pre(ub.astype(F32), dtr_ref[...], dtb_ref[...], alog_ref[...])
        lo = _iota((1, 128), 1) < SSD_P
        s_old = s_ref[...]
        st_ref[0] = s_old
        ys = []
        for g in range(2):
            bg, cg = pre["bm"][:, 128 * g:128 * g + 128], pre["cm"][:, 128 * g:128 * g + 128]
            cb = _dot_nt(cg, bg)
            gs = slice(512 * g, 512 * g + 512)
            yd = []
            for j in range(4 * g, 4 * g + 4):
                xp = pre["xdt"][:, 128 * j:128 * j + 128].astype(BF)
                _, m0 = _ssd_decay(pre, 2 * j, cb)
                _, m1 = _ssd_decay(pre, 2 * j + 1, cb)
                yd.append(jnp.where(lo, _dot(m0, xp), _dot(m1, xp)))
            yoff = _dot_nt(cg, s_old[gs, :]) * pre["eA_full"][:, gs]
            ys.append(jnp.concatenate(yd, axis=1) + yoff)
            st = _dot_tn((pre["xdt"] * pre["dte_full"])[:, gs], bg)
            cdcol = jnp.exp(_dot_sel(pre["et"][gs, :], pre["acum_t"])[:, Q - 1:Q])
            s_ref[gs, :] = s_old[gs, :] * cdcol + st
        y = jnp.concatenate(ys, axis=1) + dsk_ref[...] * pre["xs"]
        yb = y.astype(BF)
        y_ref[...] = yb
        zf = z_ref[...].astype(F32)
        yz = yb.astype(F32) * zf * _sigmoid(zf)
        outs = []
        for g in range(2):
            gs = slice(512 * g, 512 * g + 512)
            o, _, _ = _rms(yz[:, gs], nw_ref[:, gs])
            outs.append(o)
        ya_ref[...] = jnp.concatenate(outs, axis=1).astype(BF)

    outs = [SDS((L, D), BF), SDS((L, D), BF), SDS((L, 1536), BF), SDS((nc, D, 128), F32)]
    return pl.pallas_call(
        body, grid=(nc,), name="ssd_fwd",
        in_specs=[_rows(Q, 1536), _rows(Q, 128), _rows(Q, D), _const((4, 1536)), _const((1, 1536)), _const((1, 128)),
                  _const((1, 128)), _const((1, D)), _const((1, D))],
        out_specs=[_rows(Q, D), _rows(Q, D), _rows(Q, 1536), pl.BlockSpec((1, D, 128), lambda i: (i, 0, 0))],
        out_shape=outs, scratch_shapes=[pltpu.VMEM((Q, 1536), F32), pltpu.VMEM((D, 128), F32)],
        compiler_params=_cparams())(xbc, dtr, z, conv_w, conv_b, dtb, alog, dskip_full, nw)


def _ssd_bwd(dya, y, z, u, xbc, dtr, states, conv_w, dtb, alog, dskip_full, nw):
    L = dya.shape[0]
    nc = L // Q

    def body(dya_ref, y_ref, z_ref, u_ref, xc_ref, xp_ref, dtr_ref, st_ref, cw_ref, dtb_ref, alog_ref, dsk_ref, nw_ref,
             dz_ref, dxbc_ref, ddt_ref, gconv_ref, ghead_ref, glane_ref, gs_ref, ndu_ref):
        step = pl.program_id(0)

        @pl.when(step == 0)
        def _():
            for r in (gconv_ref, ghead_ref, glane_ref, gs_ref, ndu_ref):
                r[...] = jnp.zeros_like(r)

        uf = u_ref[...].astype(F32)
        pre = _ssd_pre(uf, dtr_ref[...], dtb_ref[...], alog_ref[...])
        e, et, xs, xdt = pre["e"], pre["et"], pre["xs"], pre["xdt"]
        lane = _iota((1, 128), 1)
        lo = lane < SSD_P
        sub = _iota((128, 1), 0)
        zf = z_ref[...].astype(F32)
        sgz = _sigmoid(zf)
        sz = zf * sgz
        yv = y_ref[...].astype(F32)
        yz = yv * sz
        dyav = dya_ref[...].astype(F32)
        dyz, dnw = [], []
        for g in range(2):
            gs = slice(512 * g, 512 * g + 512)
            _, n, r = _rms(yz[:, gs], nw_ref[:, gs])
            dv, dw = _rms_bwd(dyav[:, gs], n, r, nw_ref[:, gs])
            dyz.append(dv)
            dnw.append(dw)
        dyz = jnp.concatenate(dyz, axis=1)
        glane_ref[1:2, :] += _colsum(jnp.concatenate(dnw, axis=1))
        dy = dyz * sz
        dz_ref[...] = (dyz * yv * sgz * (1.0 + zf * (1.0 - sgz))).astype(BF)
        glane_ref[0:1, :] += _colsum(dy * xs)
        dxs = dsk_ref[...] * dy

        s_in = st_ref[0]
        gst = gs_ref[...]
        gy = dy * pre["eA_full"]
        xdte = xdt * pre["dte_full"]
        dacum = jnp.zeros((Q, 128), F32)
        dacum_t = jnp.zeros((128, Q), F32)
        dxdt, dacum_full, ddte_full, dbs, dcs = [], [], [], [], []
        for g in range(2):
            gs = slice(512 * g, 512 * g + 512)
            bg, cg = pre["bm"][:, 128 * g:128 * g + 128], pre["cm"][:, 128 * g:128 * g + 128]
            sg_, dg_ = s_in[gs, :], gst[gs, :]
            yoff = _dot_nt(cg, sg_) * pre["eA_full"][:, gs]
            dc = _dot(gy[:, gs], sg_)
            dsin = _dot_tn(gy[:, gs], cg)
            dacum_full.append(dy[:, gs] * yoff)
            tg = _dot_nt(bg, dg_)
            ddte_full.append(tg * xdt[:, gs])
            db = _dot(xdte[:, gs], dg_)
            cb = _dot_nt(cg, bg)
            dcb = jnp.zeros((Q, Q), F32)
            dxg = []
            for j in range(4 * g, 4 * g + 4):
                xp = xdt[:, 128 * j:128 * j + 128].astype(BF)
                dyp = dy[:, 128 * j:128 * j + 128]
                dxp = jnp.zeros((Q, 128), F32)
                for idx in range(2):
                    hh = 2 * j + idx
                    lm, m = _ssd_decay(pre, hh, cb)
                    dym = jnp.where(lo if idx == 0 else jnp.logical_not(lo), dyp, 0.0).astype(BF)
                    dm = jnp.where(pre["causal"], _dot_nt(dym, xp), 0.0)
                    w = dm * m
                    dacum = dacum + jnp.where(lane == hh, jnp.sum(w, axis=1, keepdims=True), 0.0)
                    dacum_t = dacum_t + jnp.where(sub == hh, jnp.sum(w, axis=0, keepdims=True), 0.0)
                    dcb = dcb + dm * lm
                    dxp = dxp + _dot_tn(m, dym)
                dxg.append(dxp)
            dxdt.append(jnp.concatenate(dxg, axis=1) + tg * pre["dte_full"][:, gs])
            dcs.append(dc + _dot(dcb, bg))
            dbs.append(db + _dot_tn(dcb, cg))
            cdcol = jnp.exp(_dot_sel(et[gs, :], pre["acum_t"])[:, Q - 1:Q])
            gs_ref[gs, :] = dsin + dg_ * cdcol
        dxdt = jnp.concatenate(dxdt, axis=1)
        dacum = dacum + _sel_dot(jnp.concatenate(dacum_full, axis=1), et, 2) - dacum_t.T
        alast = pre["acum"][Q - 1:Q, :]
        dte = jnp.exp(alast - pre["acum"])
        ddte = _sel_dot(jnp.concatenate(ddte_full, axis=1), et, 2) * dte
        dacum = dacum - ddte
        dcd_col = jnp.sum(_dot_sel(e, gst * s_in, 2), axis=1, keepdims=True)
        dcd_row = jnp.broadcast_to(dcd_col, (128, 128)).T[0:1, :]
        dalast = _colsum(ddte) + dcd_row * jnp.exp(alast)
        dacum = dacum + jnp.where(_iota((Q, 1), 0) == Q - 1, dalast, 0.0)
        ddt = _sel_dot(dxdt * xs, et, 2)
        dxs = dxs + dxdt * pre["dt_full"]
        dda = _dot_sel(pre["tri"].T, dacum)
        ddt = ddt + dda * pre["a_row"]
        ghead_ref[1:2, :] += _colsum(dda * pre["dt"])
        ddtr = ddt * _sigmoid(pre["pre"]) * pre["hmask"]
        ghead_ref[0:1, :] += _colsum(ddtr)
        ddt_ref[...] = ddtr

        dxc = jnp.concatenate([dxs] + dbs + dcs, axis=1)
        sgu = pre["sgu"]
        du = dxc * sgu * (1.0 + uf * (1.0 - sgu))
        shu = _conv_shifts(du, ndu_ref[...], up=True)
        dxr = cw_ref[3:4, :] * du + cw_ref[2:3, :] * shu[0] + cw_ref[1:2, :] * shu[1] + cw_ref[0:1, :] * shu[2]
        ndu_ref[...] = du
        dxbc_ref[...] = dxr.astype(BF)
        xr = xc_ref[...].astype(F32)
        xprev = jnp.where(step == nc - 1, 0.0, xp_ref[...].astype(F32))
        shx = _conv_shifts(xr, xprev, up=False)
        gconv_ref[3:4, :] += _colsum(du * xr)
        gconv_ref[2:3, :] += _colsum(du * shx[0])
        gconv_ref[1:2, :] += _colsum(du * shx[1])
        gconv_ref[0:1, :] += _colsum(du * shx[2])
        gconv_ref[4:5, :] += _colsum(du)

        @pl.when(step == nc - 1)
        def _():
            ghead_ref[2:3, :] = ghead_ref[1:2, :] * pre["a_row"]
            ghead_ref[3:4, :] = _dot_hi(glane_ref[...], et)[0:1, :]

    rev = lambda i: (nc - 1 - i, 0)
    outs = [SDS((L, D), BF), SDS((L, 1536), BF), SDS((L, 128), F32), SDS((8, 1536), F32), SDS((8, 128), F32),
            SDS((8, D), F32)]
    return pl.pallas_call(
        body, grid=(nc,), name="ssd_bwd",
        in_specs=[pl.BlockSpec((Q, D), rev), pl.BlockSpec((Q, D), rev), pl.BlockSpec((Q, D), rev),
                  pl.BlockSpec((Q, 1536), rev), pl.BlockSpec((Q, 1536), rev),
                  pl.BlockSpec((Q, 1536), lambda i: (jnp.maximum(nc - 2 - i, 0), 0)),
                  pl.BlockSpec((Q, 128), rev), pl.BlockSpec((1, D, 128), lambda i: (nc - 1 - i, 0, 0)),
                  _const((4, 1536)), _const((1, 128)), _const((1, 128)), _const((1, D)), _const((1, D))],
        out_specs=[pl.BlockSpec((Q, D), rev), pl.BlockSpec((Q, 1536), rev), pl.BlockSpec((Q, 128), rev),
                   _const((8, 1536)), _const((8, 128)), _const((8, D))],
        out_shape=outs, scratch_shapes=[pltpu.VMEM((D, 128), F32), pltpu.VMEM((Q, 1536), F32)],
        compiler_params=_cparams())(dya, y, z, u, xbc, xbc, dtr, states, conv_w, dtb, alog, dskip_full, nw)


def _hg_gates(hq, hf, hlb):
    h0, h1 = hlb[0:1, :], hlb[1:2, :]
    mx = jnp.maximum(h0, h1)
    e0, e1 = jnp.exp(h0 - mx), jnp.exp(h1 - mx)
    lb = e0 / (e0 + e1)
    sg = _sigmoid(hf)
    fg = lb + (1.0 - lb) * sg
    tri = (_iota((Q, Q), 1) <= _iota((Q, Q), 0)).astype(F32)
    return hq * _sigmoid(hq), 1.0 - fg, fg, sg, lb, e1 / (e0 + e1), _dot_sel(tri, jnp.log(fg))


def _hg_intra(b, q, k):
    rowblk = jnp.right_shift(_iota((Q, 1), 0), 4)
    refs = [b[SUB * i + SUB // 2:SUB * i + SUB // 2 + 1, :] for i in range(NSUB)]
    rfull = jnp.concatenate([jnp.broadcast_to(r, (SUB, 128)) for r in refs], axis=0)
    eq = jnp.exp(b - rfull)
    qt_rows = q * eq
    qt = jnp.concatenate([jnp.where(rowblk == i, qt_rows, 0.0) for i in range(NSUB)], axis=1).astype(BF)
    eks = [jnp.exp(jnp.where(rowblk <= i, refs[i] - b, -1e30)) for i in range(NSUB)]
    kt = jnp.concatenate([k * ek for ek in eks], axis=1).astype(BF)
    causal = _iota((Q, Q), 1) <= _iota((Q, Q), 0)
    att = jnp.where(causal, _dot_nt(qt, kt), 0.0)
    return att, qt, kt, eq, eks, rowblk, causal


def _hg_fwd(hq, hf, hi, hg, hlb, nw):
    L = hq.shape[0]
    nc = L // Q

    def body(hq_ref, hf_ref, hi_ref, hg_ref, hlb_ref, nw_ref, ob_ref, o_ref, st_ref, s_ref):
        @pl.when(pl.program_id(0) == 0)
        def _():
            s_ref[...] = jnp.zeros_like(s_ref)

        qf, kf, _, _, _, _, bcum = _hg_gates(hq_ref[...].astype(F32), hf_ref[...], hlb_ref[...])
        gate = hg_ref[...].astype(F32)
        for h in range(NH_HG):
            sl = slice(128 * h, 128 * h + 128)
            b, q, k, v = bcum[:, sl], qf[:, sl], kf[:, sl], hi_ref[:, sl]
            att = _hg_intra(b, q, k)[0]
            s = s_ref[sl, :]
            st_ref[0, sl, :] = s
            o = _dot(att, v) + _dot(q * jnp.exp(b), s)
            blast = b[Q - 1:Q, :]
            s_ref[sl, :] = s * jnp.exp(b.T[:, Q - 1:Q]) + _dot_tn(k * jnp.exp(blast - b), v)
            ob = o.astype(BF)
            o_ref[:, sl] = ob
            on, _, _ = _rms(ob.astype(F32), nw_ref[...])
            gt = gate[:, sl]
            ob_ref[:, sl] = (on * gt * _sigmoid(gt)).astype(BF)

    outs = [SDS((L, D), BF), SDS((L, D), BF), SDS((nc, D, 128), F32)]
    return pl.pallas_call(
        body, grid=(nc,), name="hg_fwd",
        in_specs=[_rows(Q, D), _rows(Q, D), _rows(Q, D), _rows(Q, D), _const((2, D)), _const((1, 128))],
        out_specs=[_rows(Q, D), _rows(Q, D), pl.BlockSpec((1, D, 128), lambda i: (i, 0, 0))],
        out_shape=outs, scratch_shapes=[pltpu.VMEM((D, 128), F32)],
        compiler_params=_cparams())(hq, hf, hi, hg, hlb, nw)


def _hg_bwd(dob, o, hq, hf, hi, hg, states, hlb, nw):
    L = dob.shape[0]
    nc = L // Q

    def body(dob_ref, o_ref, hq_ref, hf_ref, hi_ref, hg_ref, st_ref, hlb_ref, nw_ref,
             dhq_ref, dhf_ref, dhi_ref, dhg_ref, acc_ref, gs_ref):
        step = pl.program_id(0)

        @pl.when(step == 0)
        def _():
            acc_ref[...] = jnp.zeros_like(acc_ref)
            gs_ref[...] = jnp.zeros_like(gs_ref)

        hqv = hq_ref[...].astype(F32)
        qf, kf, fg, sg, lb, sm1, bcum = _hg_gates(hqv, hf_ref[...], hlb_ref[...])
        gate = hg_ref[...].astype(F32)
        sgg = _sigmoid(gate)
        nwv = nw_ref[...]
        tri_t = (_iota((Q, Q), 1) >= _iota((Q, Q), 0)).astype(F32)
        ones8 = jnp.ones((8, 128), F32)
        dqs, dks, dgls, dnws = [], [], [], []
        for h in range(NH_HG):
            sl = slice(128 * h, 128 * h + 128)
            b, q, k, v = bcum[:, sl], qf[:, sl], kf[:, sl], hi_ref[:, sl]
            gt, sgt = gate[:, sl], sgg[:, sl]
            ov = o_ref[:, sl].astype(F32)
            _, n, r = _rms(ov, nwv)
            dobv = dob_ref[:, sl].astype(F32)
            dhg_ref[:, sl] = (dobv * n * nwv * sgt * (1.0 + gt * (1.0 - sgt))).astype(BF)
            do, dw = _rms_bwd(dobv * gt * sgt, n, r, nwv)
            dnws.append(_colsum(dw))
            dob_h = do.astype(BF)
            att, qt, kt, eq, eks, rowblk, causal = _hg_intra(b, q, k)
            s, gst = st_ref[0, sl, :], gs_ref[sl, :]
            eb = jnp.exp(b)
            blast = b[Q - 1:Q, :]
            eblast_col = jnp.exp(b.T[:, Q - 1:Q])
            ekl = jnp.exp(blast - b)
            qhat, khat = q * eb, k * ekl
            dqhat = _dot_nt(dob_h, s)
            da = jnp.where(causal, _dot_nt(dob_h, v), 0.0).astype(BF)
            dhi_ref[:, sl] = (_dot_tn(att, dob_h) + _dot(khat, gst)).astype(BF)
            dkhat = _dot_nt(v, gst)
            dqt = jnp.dot(da, kt, preferred_element_type=F32)
            dkt = lax.dot_general(da, qt, (((0,), (0,)), ((), ())), preferred_element_type=F32)
            dq = dqhat * eb
            dk = dkhat * ekl
            qhat_r, khat_r = qhat.astype(BF).astype(F32), khat.astype(BF).astype(F32)
            db = qhat_r * dqhat - khat_r * dkhat
            for i in range(NSUB):
                bl = slice(128 * i, 128 * i + 128)
                dq = dq + jnp.where(rowblk == i, dqt[:, bl], 0.0) * eq
                dk = dk + dkt[:, bl] * eks[i]
                db = db + qt[:, bl].astype(F32) * dqt[:, bl] - kt[:, bl].astype(F32) * dkt[:, bl]
            dblast = _colsum(dkhat * khat_r) + lax.dot_general(
                ones8, gst * s, (((1,), (1,)), ((), ())), precision=HI, preferred_element_type=F32)[0:1, :] * jnp.exp(blast)
            db = db + jnp.where(_iota((Q, 1), 0) == Q - 1, dblast, 0.0)
            dgls.append(_dot_sel(tri_t, db, 2))
            gs_ref[sl, :] = _dot_tn(qhat, dob_h) + gst * eblast_col
            dqs.append(dq)
            dks.append(dk)
        dq, dk, dgl = (jnp.concatenate(t, axis=1) for t in (dqs, dks, dgls))
        sgq = _sigmoid(hqv)
        dhq_ref[...] = (dq * sgq * (1.0 + hqv * (1.0 - sgq))).astype(BF)
        dfg = dgl / fg - dk
        dhf_ref[...] = (dfg * (1.0 - lb) * sg * (1.0 - sg)).astype(BF)
        acc_ref[0:1, :] += _colsum(dfg * (1.0 - sg))
        acc_ref[1:2, :] += jnp.concatenate(dnws, axis=1)

        @pl.when(step == nc - 1)
        def _():
            dlb = acc_ref[0:1, :] * lb * sm1
            acc_ref[2:3, :] = dlb
            acc_ref[3:4, :] = -dlb
            tot = acc_ref[1:2, 0:128]
            for h in range(1, NH_HG):
                tot = tot + acc_ref[1:2, 128 * h:128 * h + 128]
            acc_ref[4:5, 0:128] = tot

    rev = lambda i: (nc - 1 - i, 0)
    outs = [SDS((L, D), BF)] * 4 + [SDS((8, D), F32)]
    return pl.pallas_call(
        body, grid=(nc,), name="hg_bwd",
        in_specs=[pl.BlockSpec((Q, D), rev)] * 6 + [pl.BlockSpec((1, D, 128), lambda i: (nc - 1 - i, 0, 0)),
                                                    _const((2, D)), _const((1, 128))],
        out_specs=[pl.BlockSpec((Q, D), rev)] * 4 + [_const((8, D))],
        out_shape=outs, scratch_shapes=[pltpu.VMEM((D, 128), F32)],
        compiler_params=_cparams())(dob, o, hq, hf, hi, hg, states, hlb, nw)


def _pad_lanes(v, n=128):
    return jnp.pad(v, ((0, 0), (0, n - v.shape[1])))


def _permute_w_in(w4):
    w = w4.transpose(1, 0, 2).reshape(D, N_IN)
    return jnp.concatenate([w[:, :2560], w[:, 2576:], w[:, 2560:2576], jnp.zeros((D, NINP - N_IN), w.dtype)], axis=1)


def _local_step(x, mem, tgt, wg, ws):
    w_in_p = _permute_w_in(wg["w_in"])
    w_out = wg["w_out"].reshape(2 * D, D)
    wq, wo = wg["xa_wq"].reshape(D, D), wg["xa_wo"].reshape(D, D)
    dtb, alog = _pad_lanes(ws["dt_bias"]), _pad_lanes(ws["a_log"])
    dskip_full = jnp.repeat(ws["d_skip"], SSD_P, axis=1)
    conv_w, conv_b = ws["conv_w"][0], ws["conv_b"]

    h0, z, xbc, hq, hf, hi, hg, dtr = _in_proj(x, ws["norm_mix_w"], w_in_p)
    ya, yssd, u, st_ssd = _ssd_fwd(xbc, dtr, z, conv_w, conv_b, dtb, alog, dskip_full, ws["ssd_norm_w"])
    ob, ohg, st_hg = _hg_fwd(hq, hf, hi, hg, ws["hg_lower_bounds"], ws["hg_norm_w"])
    kmem, vmem = _mem_kv(mem, ws["norm_mem_w"], wg["xa_wkv"])
    x1, x2, hxa, q, ox = _attn_fwd(x, ya, ob, w_out, ws["norm_xa_w"], wq, kmem, vmem, wo)
    nfin = ws["norm_final_w"].reshape(1, D)
    dx2, hffn, act, dx3, dg, du, acc_f = _ffn_loss(x2, tgt, ws["norm_ffn_w"], nfin, wg["ffn_w_gate"], wg["ffn_w_up"],
                                                   wg["ffn_w_down"])
    dx1, dya, dob, dq, dk, dv, acc_a = _attn_bwd(dx2, x1, q, kmem, vmem, ws["norm_xa_w"], wq, wo, w_out)
    g_nmem, g_wkv = _mem_kv_bwd(mem, ws["norm_mem_w"], wg["xa_wkv"], dk, dv)
    dhq, dhf, dhi, dhg, acc_h = _hg_bwd(dob, ohg, hq, hf, hi, hg, st_hg, ws["hg_lower_bounds"], ws["hg_norm_w"])
    dz, dxbc, ddt, gconv, ghead, glane = _ssd_bwd(dya, yssd, z, u, xbc, dtr, st_ssd, conv_w, dtb, alog, dskip_full,
                                                  ws["ssd_norm_w"])
    gx, acc_i = _in_proj_bwd(x, dx1, dz, dxbc, dhq, dhf, dhi, dhg, ddt, ws["norm_mix_w"], w_in_p)

    parts = [_matmul_tn(h0, d_, "gw_in_" + n_) for d_, n_ in
             ((dz, "z"), (dxbc, "xbc"), (ddt, "dt"), (dhq, "hq"), (dhf, "hf"), (dhi, "hi"), (dhg, "hg"))]
    parts[2] = parts[2][:, :NH_SSD]
    gb = {
        "w_in": jnp.concatenate(parts, axis=1).reshape(D, 4, N_IN // 4).transpose(1, 0, 2),
        "w_out": _matmul_tn_pair(ya, ob, dx1, "gw_out").reshape(4, D // 2, D),
        "xa_wq": _matmul_tn(hxa, dq, "gw_q").reshape(4, D // 4, D),
        "xa_wkv": g_wkv,
        "xa_wo": _matmul_tn(ox, dx2, "gw_o").reshape(4, D // 4, D),
        "ffn_w_gate": _matmul_tn_blocks(hffn, dg, "gw_gate"),
        "ffn_w_up": _matmul_tn_blocks(hffn, du, "gw_up"),
        "ffn_w_down": _matmul_tn_blocks(act, dx3, "gw_down"),
    }
    gs = {
        "norm_mix_w": acc_i[0:1], "conv_w": gconv[0:4][None], "conv_b": gconv[4:5],
        "dt_bias": ghead[0:1, :NH_SSD], "a_log": ghead[2:3, :NH_SSD], "d_skip": ghead[3:4, :NH_SSD],
        "ssd_norm_w": glane[1:2], "hg_lower_bounds": acc_h[2:4], "hg_norm_w": acc_h[4:5, :128],
        "norm_xa_w": acc_a[0:1], "norm_mem_w": g_nmem, "norm_ffn_w": acc_f[2:3], "norm_final_w": acc_f[1],
    }
    loss = (0.5 / D) * jnp.sum(acc_f[0])
    return loss, gx, gb, gs


def _place():
    return lax.axis_index("x"), lax.axis_index("y"), lax.axis_index("c")


def _allgather(arrays, halves, name):
    n = len(arrays)

    def body(*refs):
        ins, outs = refs[:n], refs[n:2 * n]
        send_sems, recv_sems, local_sems = refs[2 * n:]
        x, y, c = _place()
        me, sibling = (x, y, c), (x, y, 1 - c)
        chips = [(1 - x, y), (x, 1 - y), (1 - x, 1 - y)]
        waits_recv, waits_send, locals_ = [], [], []
        for a in range(n):
            hr = halves[a]

            def slot(p, ref=outs[a], hr=hr):
                if hr is None:
                    return ref.at[4 * p[0] + 2 * p[1] + p[2]]
                return ref.at[2 * p[0] + p[1], pl.ds(p[2] * hr, hr), :]

            own = ins[a] if hr is None else ins[a].at[pl.ds(c * hr, hr), :]

            def copy(k, piece, to, src=None, a=a, slot=slot):
                return pltpu.make_async_remote_copy(
                    src_ref=slot(piece) if src is None else src, dst_ref=slot(piece),
                    send_sem=send_sems.at[7 * a + k], recv_sem=recv_sems.at[7 * a + k],
                    device_id=to, device_id_type=MESH)

            mine = pltpu.make_async_copy(own, slot(me), local_sems.at[a])
            mine.start()
            locals_.append(mine)
            first = [copy(0, me, sibling, src=own)]
            first += [copy(1 + j, me, (*chip, c), src=own) for j, chip in enumerate(chips)]
            for cp in first:
                cp.start()
            passed = [copy(4 + j, (*chip, c), sibling) for j, chip in enumerate(chips)]
            for j, chip in enumerate(chips):
                copy(1 + j, (*chip, c), me).wait_recv()
                passed[j].start()
            waits_recv.append(copy(0, sibling, me))
            waits_recv += [copy(4 + j, (*chip, 1 - c), me) for j, chip in enumerate(chips)]
            waits_send += first + passed
        for cp in waits_recv:
            cp.wait_recv()
        for cp in waits_send:
            cp.wait_send()
        for cp in locals_:
            cp.wait()

    out_shape = [SDS((8,) + a.shape if hr is None else (4,) + a.shape, a.dtype) for a, hr in zip(arrays, halves)]
    return pl.pallas_call(
        body, name=name, in_specs=[ANY] * n, out_specs=[ANY] * n, out_shape=out_shape,
        scratch_shapes=[pltpu.SemaphoreType.DMA((7 * n,)), pltpu.SemaphoreType.DMA((7 * n,)),
                        pltpu.SemaphoreType.DMA((n,))])(*arrays)


def _send_sibling_halves(grads, halves, name):
    n = len(grads)

    def body(*refs):
        ins, outs = refs[:n], refs[n:2 * n]
        send_sems, recv_sems = refs[2 * n:]
        x, y, c = _place()
        cps = [pltpu.make_async_remote_copy(
            src_ref=ins[a].at[:, pl.ds((1 - c) * halves[a], halves[a]), :], dst_ref=outs[a],
            send_sem=send_sems.at[a], recv_sem=recv_sems.at[a], device_id=(x, y, 1 - c), device_id_type=MESH)
            for a in range(n)]
        for cp in cps:
            cp.start()
        for cp in cps:
            cp.wait()

    return pl.pallas_call(
        body, name=name, in_specs=[ANY] * n, out_specs=[ANY] * n,
        out_shape=[SDS((4, hr, g.shape[2]), g.dtype) for g, hr in zip(grads, halves)],
        scratch_shapes=[pltpu.SemaphoreType.DMA((n,)), pltpu.SemaphoreType.DMA((n,))])(*grads)


def _send_chips(parts, name):
    n = len(parts)

    def body(*refs):
        ins, outs = refs[:n], refs[n:2 * n]
        send_sems, recv_sems = refs[2 * n:]
        x, y, c = _place()
        chips = [(1 - x, y), (x, 1 - y), (1 - x, 1 - y)]
        cps = [pltpu.make_async_remote_copy(
            src_ref=ins[a].at[2 * px + py], dst_ref=outs[a].at[k], send_sem=send_sems.at[3 * a + k],
            recv_sem=recv_sems.at[3 * a + k], device_id=(px, py, c), device_id_type=MESH)
            for a in range(n) for k, (px, py) in enumerate(chips)]
        for cp in cps:
            cp.start()
        for cp in cps:
            cp.wait()

    return pl.pallas_call(
        body, name=name, in_specs=[ANY] * n, out_specs=[ANY] * n,
        out_shape=[SDS((3,) + p.shape[1:], p.dtype) for p in parts],
        scratch_shapes=[pltpu.SemaphoreType.DMA((3 * n,)), pltpu.SemaphoreType.DMA((3 * n,))])(*parts)


def _join_halves(halves_own, name):
    n = len(halves_own)

    def body(*refs):
        ins, outs = refs[:n], refs[n:2 * n]
        send_sems, recv_sems, local_sems = refs[2 * n:]
        x, y, c = _place()
        cps, locs = [], []
        for a in range(n):
            hr = halves_own[a].shape[0]
            rows = outs[a].at[pl.ds(c * hr, hr), :]
            locs.append(pltpu.make_async_copy(ins[a], rows, local_sems.at[a]))
            cps.append(pltpu.make_async_remote_copy(
                src_ref=ins[a], dst_ref=rows, send_sem=send_sems.at[a], recv_sem=recv_sems.at[a],
                device_id=(x, y, 1 - c), device_id_type=MESH))
        for cp in locs + cps:
            cp.start()
        for a in range(n):
            hr = halves_own[a].shape[0]
            other = outs[a].at[pl.ds((1 - c) * hr, hr), :]
            pltpu.make_async_remote_copy(
                src_ref=ins[a], dst_ref=other, send_sem=send_sems.at[a], recv_sem=recv_sems.at[a],
                device_id=(x, y, 1 - c), device_id_type=MESH).wait_recv()
        for cp in cps:
            cp.wait_send()
        for cp in locs:
            cp.wait()

    return pl.pallas_call(
        body, name=name, in_specs=[ANY] * n, out_specs=[ANY] * n,
        out_shape=[SDS((2 * h.shape[0], h.shape[1]), h.dtype) for h in halves_own],
        scratch_shapes=[pltpu.SemaphoreType.DMA((n,)), pltpu.SemaphoreType.DMA((n,)), pltpu.SemaphoreType.DMA((n,))]
    )(*halves_own)


def _row_tile(rows, cols, nbuf):
    cap = max(8, (VMEM_LIMIT // 3) // (2 * nbuf * cols * 4))
    best = None
    for step in (16, 8):
        for t in range(step, min(rows, cap) + 1, step):
            if rows % t == 0:
                best = t
        if best is not None:
            return best
    return rows


def _chip_sum(g, from_sib, place, name):
    _, hr, cols = from_sib.shape
    tr = _row_tile(hr, cols, 4)
    nj = hr // tr

    def body(p_ref, g_ref, s_ref, hb_ref, own_ref):
        s = g_ref[...] + s_ref[...]
        hb_ref[...] = s.astype(BF)

        @pl.when(pl.program_id(1) == p_ref[1])
        def _():
            own_ref[...] = s

    grid_spec = pltpu.PrefetchScalarGridSpec(
        num_scalar_prefetch=1, grid=(nj, 4),
        in_specs=[pl.BlockSpec((None, tr, cols), lambda j, i, p: (i, p[0] * nj + j, 0)),
                  pl.BlockSpec((None, tr, cols), lambda j, i, p: (i, j, 0))],
        out_specs=[pl.BlockSpec((None, tr, cols), lambda j, i, p: (i, j, 0)),
                   pl.BlockSpec((tr, cols), lambda j, i, p: (j, 0))])
    return pl.pallas_call(
        body, grid_spec=grid_spec, name=name, out_shape=[SDS((4, hr, cols), BF), SDS((hr, cols), F32)],
        compiler_params=pltpu.CompilerParams(dimension_semantics=("arbitrary", "arbitrary"),
                                             vmem_limit_bytes=VMEM_LIMIT))(place, g, from_sib)


def _total(own, parts, name):
    hr, cols = own.shape
    tr = _row_tile(hr, cols, 5)

    def body(own_ref, p_ref, o_ref):
        s = own_ref[...]
        for k in range(3):
            s = s + p_ref[k].astype(F32)
        o_ref[...] = s

    return pl.pallas_call(
        body, grid=(hr // tr,), name=name,
        in_specs=[_rows(tr, cols), pl.BlockSpec((3, tr, cols), lambda i: (0, i, 0))], out_specs=_rows(tr, cols),
        out_shape=SDS((hr, cols), F32), compiler_params=_cparams())(own, parts)


def _sum8(parts, name):
    R = parts.shape[1]

    def body(p_ref, o_ref):
        s = p_ref[0]
        for k in range(1, 8):
            s = s + p_ref[k]
        o_ref[...] = s

    return pl.pallas_call(
        body, grid=(1,), name=name, in_specs=[_const((8, R, 128))], out_specs=_const((R, 128)),
        out_shape=SDS((R, 128), F32), compiler_params=_cparams())(parts)


def _adamw(w, g, m, v, name):
    _, R, C = w.shape
    tr = _row_tile(R, C, 7)
    c1 = 1.0 / (1.0 - ADAM_B1 ** ADAM_STEP)
    c2 = 1.0 / (1.0 - ADAM_B2 ** ADAM_STEP)

    def body(w_ref, g_ref, m_ref, v_ref, d_ref, nm_ref, nv_ref):
        gv = g_ref[...]
        nm = ADAM_B1 * m_ref[...] + (1.0 - ADAM_B1) * gv
        nv = ADAM_B2 * v_ref[...] + (1.0 - ADAM_B2) * gv * gv
        nm_ref[...] = nm
        nv_ref[...] = nv
        d_ref[...] = -ADAM_LR * ((nm * c1) / (jnp.sqrt(nv * c2) + ADAM_EPS) + ADAM_WD * w_ref[...])

    blk3 = pl.BlockSpec((None, tr, C), lambda i: (0, i, 0))
    gspec = blk3 if g.ndim == 3 else _rows(tr, C)
    return pl.pallas_call(
        body, grid=(R // tr,), name=name, in_specs=[blk3, gspec, blk3, blk3], out_specs=[blk3] * 3,
        out_shape=[SDS((1, R, C), F32)] * 3, compiler_params=_cparams())(w, g, m, v)


def _pack_small(parts):
    rows = []
    for p in parts:
        p = p.reshape(-1)
        rows.append(jnp.pad(p, (0, (-p.shape[0]) % 128)).reshape(-1, 128))
    out = jnp.concatenate(rows, axis=0)
    return jnp.pad(out, ((0, (-out.shape[0]) % 8), (0, 0)))


def _unpack_small(packed, shapes):
    out, row = [], 0
    for shp in shapes:
        n = 1
        for s in shp:
            n *= s
        nr = -(-n // 128)
        out.append(packed[row:row + nr].reshape(-1)[:n].reshape(shp))
        row += nr
    return out


def kernel(x, mem, norm_mix_w, w_in, conv_w, conv_b, dt_bias, a_log, d_skip, ssd_norm_w, hg_lower_bounds, hg_norm_w, w_out, norm_xa_w, norm_mem_w, xa_wq, xa_wkv, xa_wo, norm_ffn_w, ffn_w_gate, ffn_w_up, ffn_w_down, norm_final_w, loss_target, m_norm_mix_w, m_w_in, m_conv_w, m_conv_b, m_dt_bias, m_a_log, m_d_skip, m_ssd_norm_w, m_hg_lower_bounds, m_hg_norm_w, m_w_out, m_norm_xa_w, m_norm_mem_w, m_xa_wq, m_xa_wkv, m_xa_wo, m_norm_ffn_w, m_ffn_w_gate, m_ffn_w_up, m_ffn_w_down, m_norm_final_w, v_norm_mix_w, v_w_in, v_conv_w, v_conv_b, v_dt_bias, v_a_log, v_d_skip, v_ssd_norm_w, v_hg_lower_bounds, v_hg_norm_w, v_w_out, v_norm_xa_w, v_norm_mem_w, v_xa_wq, v_xa_wkv, v_xa_wo, v_norm_ffn_w, v_ffn_w_gate, v_ffn_w_up, v_ffn_w_down, v_norm_final_w):
    w = dict(norm_mix_w=norm_mix_w, w_in=w_in, conv_w=conv_w, conv_b=conv_b, dt_bias=dt_bias, a_log=a_log, d_skip=d_skip,
             ssd_norm_w=ssd_norm_w, hg_lower_bounds=hg_lower_bounds, hg_norm_w=hg_norm_w, w_out=w_out,
             norm_xa_w=norm_xa_w, norm_mem_w=norm_mem_w, xa_wq=xa_wq, xa_wkv=xa_wkv, xa_wo=xa_wo, norm_ffn_w=norm_ffn_w,
             ffn_w_gate=ffn_w_gate, ffn_w_up=ffn_w_up, ffn_w_down=ffn_w_down, norm_final_w=norm_final_w)
    m = dict(norm_mix_w=m_norm_mix_w, w_in=m_w_in, conv_w=m_conv_w, conv_b=m_conv_b, dt_bias=m_dt_bias, a_log=m_a_log,
             d_skip=m_d_skip, ssd_norm_w=m_ssd_norm_w, hg_lower_bounds=m_hg_lower_bounds, hg_norm_w=m_hg_norm_w,
             w_out=m_w_out, norm_xa_w=m_norm_xa_w, norm_mem_w=m_norm_mem_w, xa_wq=m_xa_wq, xa_wkv=m_xa_wkv,
             xa_wo=m_xa_wo, norm_ffn_w=m_norm_ffn_w, ffn_w_gate=m_ffn_w_gate, ffn_w_up=m_ffn_w_up,
             ffn_w_down=m_ffn_w_down, norm_final_w=m_norm_final_w)
    v = dict(norm_mix_w=v_norm_mix_w, w_in=v_w_in, conv_w=v_conv_w, conv_b=v_conv_b, dt_bias=v_dt_bias, a_log=v_a_log,
             d_skip=v_d_skip, ssd_norm_w=v_ssd_norm_w, hg_lower_bounds=v_hg_lower_bounds, hg_norm_w=v_hg_norm_w,
             w_out=v_w_out, norm_xa_w=v_norm_xa_w, norm_mem_w=v_norm_mem_w, xa_wq=v_xa_wq, xa_wkv=v_xa_wkv,
             xa_wo=v_xa_wo, norm_ffn_w=v_norm_ffn_w, ffn_w_gate=v_ffn_w_gate, ffn_w_up=v_ffn_w_up,
             ffn_w_down=v_ffn_w_down, norm_final_w=v_norm_final_w)
    xi, yi, ci = _place()
    chip = 2 * xi + yi
    place = jnp.stack([ci, chip]).astype(jnp.int32)
    halves = [w[name].shape[1] // 2 for name in BIG]

    gathered = _allgather([w[name][0].astype(BF) for name in BIG] + [conv_w[0]], halves + [None], "gather_weights")
    wg = dict(zip(BIG, gathered[:-1]))
    ws = {name: w[name] for name in SMALL}
    ws["conv_w"] = gathered[-1][0::2].transpose(1, 0, 2).reshape(1, 4, 1536)

    loss, gx, gb, gs = _local_step(x[0], mem[0], loss_target[0], wg, ws)

    glist = [gb[name] for name in BIG]
    from_sib = _send_sibling_halves(glist, halves, "grads_to_sibling")
    sums = [_chip_sum(g, s, place, "grads_chip_sum_" + name) for g, s, name in zip(glist, from_sib, BIG)]
    others = _send_chips([hb for hb, _ in sums], "grads_to_chips")
    reduced = [_total(own, o, "grads_total_" + name) for (_, own), o, name in zip(sums, others, BIG)]
    g_big = dict(zip(BIG, _join_halves(reduced, "grads_join_halves")))

    small_parts = [gs[name] for name in SMALL] + [loss.reshape(1)]
    small_shapes = [gs[name].shape for name in SMALL] + [(1,)]
    packed = _allgather([_pack_small(small_parts)], [None], "gather_small")[0]
    small = _unpack_small(_sum8(packed, "small_total"), small_shapes)
    g_small = dict(zip(SMALL, small[:-1]))
    loss_all = small[-1][0]
    g_small["conv_w"] = lax.dynamic_slice_in_dim(g_small["conv_w"], chip * 384, 384, 2)

    grads, delta, new_m, new_v = {}, {}, {}, {}
    for name in BIG:
        grads[name] = g_big[name][None]
        delta[name], new_m[name], new_v[name] = _adamw(w[name], g_big[name], m[name], v[name], "adamw_" + name)
    shapes = [w[name].shape for name in SMALL]
    packs = [_pack_small([t[name] for name in SMALL])[None] for t in (w, g_small, m, v)]
    outs = _adamw(*packs, "adamw_small")
    for name, g_, d_, nm_, nv_ in zip(SMALL, [g_small[n] for n in SMALL], *[_unpack_small(o[0], shapes) for o in outs]):
        grads[name] = g_.reshape(w[name].shape)
        delta[name], new_m[name], new_v[name] = d_, nm_, nv_

    return (loss_all, gx[None], *[grads[n] for n in WEIGHTS], *[delta[n] for n in WEIGHTS],
            *[new_m[n] for n in WEIGHTS], *[new_v[n] for n in WEIGHTS])
```

```python
import jax
import jax.numpy as jnp
from jax import lax
from jax.experimental import pallas as pl
from jax.experimental.pallas import tpu as pltpu

F32 = jnp.float32
BF = jnp.bfloat16
HI = lax.Precision.HIGHEST
MESH = pl.DeviceIdType.MESH
SDS = jax.ShapeDtypeStruct
ANY = pl.BlockSpec(memory_space=pl.ANY)

D = 1024
EPS = 1e-6
NH_SSD = 16
SSD_P = 64
NH_HG = 8
Q = 128
SUB = 16
NSUB = Q // SUB
XA_HEADS = 4
XA_HD = 256
MEM_LEN = 256
FFN = 2816
TL = 256
VMEM_LIMIT = 56 << 20

N_IN = 6672
Z0, XBC0, DT0, HQ0, HF0, HI0, HG0 = 0, 1024, 2560, 2576, 3600, 4624, 5648

ADAM_LR, ADAM_B1, ADAM_B2, ADAM_EPS, ADAM_WD, ADAM_STEP = 0.001, 0.9, 0.999, 1e-08, 0.01, 10

BIG = ("w_in", "w_out", "xa_wq", "xa_wkv", "xa_wo", "ffn_w_gate", "ffn_w_up", "ffn_w_down")
TRANSPOSED = ("w_in", "ffn_w_gate", "ffn_w_up")
SMALL = ("norm_mix_w", "conv_w", "conv_b", "dt_bias", "a_log", "d_skip", "ssd_norm_w", "hg_lower_bounds",
         "hg_norm_w", "norm_xa_w", "norm_mem_w", "norm_ffn_w", "norm_final_w")
WEIGHTS = ("norm_mix_w", "w_in", "conv_w", "conv_b", "dt_bias", "a_log", "d_skip", "ssd_norm_w", "hg_lower_bounds",
           "hg_norm_w", "w_out", "norm_xa_w", "norm_mem_w", "xa_wq", "xa_wkv", "xa_wo", "norm_ffn_w", "ffn_w_gate",
           "ffn_w_up", "ffn_w_down", "norm_final_w")


def _cparams():
    return pltpu.CompilerParams(dimension_semantics=("arbitrary",), vmem_limit_bytes=VMEM_LIMIT)


def _const(shape):
    return pl.BlockSpec(shape, lambda i: (0,) * len(shape))


def _rows(tl, n):
    return pl.BlockSpec((tl, n), lambda i: (i, 0))


def _dot(a, b):
    return jnp.dot(a.astype(BF), b.astype(BF), preferred_element_type=F32)


def _dot_nt(a, b):
    return lax.dot_general(a.astype(BF), b.astype(BF), (((1,), (1,)), ((), ())), preferred_element_type=F32)


def _dot_tn(a, b):
    return lax.dot_general(a.astype(BF), b.astype(BF), (((0,), (0,)), ((), ())), preferred_element_type=F32)


def _dot_hi(a, b):
    return jnp.dot(a, b, precision=HI, preferred_element_type=F32)


def _split(v, passes):
    parts, rest = [], v
    for p in range(passes):
        hi = rest.astype(BF)
        parts.append(hi)
        if p + 1 < passes:
            rest = rest - hi.astype(F32)
    return parts


def _sel_dot(a, sel, passes=3):
    sb = sel.astype(BF)
    out = None
    for part in _split(a, passes):
        t = jnp.dot(part, sb, preferred_element_type=F32)
        out = t if out is None else out + t
    return out


def _dot_sel(sel, b, passes=3):
    sb = sel.astype(BF)
    out = None
    for part in _split(b, passes):
        t = jnp.dot(sb, part, preferred_element_type=F32)
        out = t if out is None else out + t
    return out


def _iota(shape, dim):
    return lax.broadcasted_iota(jnp.int32, shape, dim)


def _sigmoid(v):
    return 1.0 / (1.0 + jnp.exp(-v))


def _rms(v, w):
    r = lax.rsqrt(jnp.mean(v * v, axis=-1, keepdims=True) + EPS)
    n = v * r
    return n * w, n, r


def _rms_bwd(dy, n, r, w):
    dn = dy * w
    return r * (dn - n * jnp.mean(dn * n, axis=-1, keepdims=True)), dy * n


def _colsum(v):
    return jnp.sum(v, axis=0, keepdims=True)


def _zero_first(*refs):
    @pl.when(pl.program_id(0) == 0)
    def _():
        for r in refs:
            r[...] = jnp.zeros_like(r)


def _in_proj(x, nw, wt):
    L = x.shape[0]
    tl = min(TL, L)

    def body(x_ref, nw_ref, w_ref, h0_ref, z_ref, xbc_ref, hq_ref, hf_ref, hi_ref, hg_ref, dt_ref):
        h, _, _ = _rms(x_ref[...], nw_ref[...])
        hb = h.astype(BF)
        h0_ref[...] = hb

        def proj(a, b):
            return _dot_nt(hb, w_ref[a:b, :])

        z_ref[...] = proj(Z0, XBC0).astype(BF)
        xbc_ref[...] = proj(XBC0, DT0).astype(BF)
        dt_ref[...] = proj(DT0, DT0 + 128)
        hq_ref[...] = proj(HQ0, HF0).astype(BF)
        hf_ref[...] = proj(HF0, HI0)
        hi_ref[...] = proj(HI0, HG0).astype(BF)
        hg_ref[...] = proj(HG0, N_IN).astype(BF)

    outs = [SDS((L, D), BF), SDS((L, D), BF), SDS((L, 1536), BF), SDS((L, D), BF), SDS((L, D), F32),
            SDS((L, D), BF), SDS((L, D), BF), SDS((L, 128), F32)]
    return pl.pallas_call(
        body, grid=(L // tl,), name="in_proj",
        in_specs=[_rows(tl, D), _const((1, D)), _const((N_IN, D))],
        out_specs=[_rows(tl, o.shape[1]) for o in outs], out_shape=outs,
        compiler_params=_cparams())(x, nw, wt)


def _mem_kv(mem, nw, wkv4):
    def body(m_ref, nw_ref, w_ref, k_ref, v_ref):
        m, _, _ = _rms(m_ref[...], nw_ref[...])
        mb = m.astype(BF)
        for i in range(2):
            sl = slice(512 * i, 512 * i + 512)
            k_ref[:, sl] = jnp.dot(mb, w_ref[i], preferred_element_type=F32).astype(BF)
            v_ref[:, sl] = jnp.dot(mb, w_ref[2 + i], preferred_element_type=F32).astype(BF)

    outs = [SDS((MEM_LEN, D), BF)] * 2
    return pl.pallas_call(
        body, grid=(1,), name="mem_kv",
        in_specs=[_const((MEM_LEN, D)), _const((1, D)), _const((4, D, 512))],
        out_specs=[_const((MEM_LEN, D))] * 2, out_shape=outs, compiler_params=_cparams())(mem, nw, wkv4)


def _mem_kv_bwd(mem, nw, wkv4, dk, dv):
    def body(m_ref, nw_ref, w_ref, dk_ref, dv_ref, gnw_ref, gw_ref):
        m, n, _ = _rms(m_ref[...], nw_ref[...])
        mb = m.astype(BF)
        dm = jnp.zeros((MEM_LEN, D), F32)
        for i in range(4):
            src = dk_ref if i < 2 else dv_ref
            d = src[:, 512 * (i % 2):512 * (i % 2) + 512].astype(BF)
            gw_ref[i] = _dot_tn(mb, d)
            dm = dm + _dot_nt(d, w_ref[i])
        gnw_ref[...] = _colsum(dm * n)

    return pl.pallas_call(
        body, grid=(1,), name="mem_kv_bwd",
        in_specs=[_const((MEM_LEN, D)), _const((1, D)), _const((4, D, 512)), _const((MEM_LEN, D)), _const((MEM_LEN, D))],
        out_specs=[_const((1, D)), _const((4, D, 512))],
        out_shape=[SDS((1, D), F32), SDS((4, D, 512), F32)], compiler_params=_cparams())(mem, nw, wkv4, dk, dv)


def _softmax_rows(sc):
    e = jnp.exp(sc - jnp.max(sc, axis=-1, keepdims=True))
    return e / jnp.sum(e, axis=-1, keepdims=True)


def _attn_fwd(x, ya, ob, w_out, nxa, wq, k, v, wo):
    L = x.shape[0]
    tl = min(TL, L)
    scale = XA_HD ** -0.5

    def body(x_ref, ya_ref, ob_ref, wout_ref, nxa_ref, wq_ref, k_ref, v_ref, wo_ref,
             x1_ref, x2_ref, hxa_ref, q_ref, ox_ref):
        x1 = x_ref[...] + jnp.dot(ya_ref[...], wout_ref[:D, :], preferred_element_type=F32) \
            + jnp.dot(ob_ref[...], wout_ref[D:, :], preferred_element_type=F32)
        x1_ref[...] = x1
        h, _, _ = _rms(x1, nxa_ref[...])
        hb = h.astype(BF)
        hxa_ref[...] = hb
        qb = jnp.dot(hb, wq_ref[...], preferred_element_type=F32).astype(BF)
        q_ref[...] = qb
        oxs = []
        for hd in range(XA_HEADS):
            sl = slice(hd * XA_HD, (hd + 1) * XA_HD)
            p = _softmax_rows(_dot_nt(qb[:, sl], k_ref[:, sl]) * scale)
            oxs.append(_dot(p, v_ref[:, sl]))
        oxb = jnp.concatenate(oxs, axis=1).astype(BF)
        ox_ref[...] = oxb
        x2_ref[...] = x1 + jnp.dot(oxb, wo_ref[...], preferred_element_type=F32)

    outs = [SDS((L, D), F32), SDS((L, D), F32), SDS((L, D), BF), SDS((L, D), BF), SDS((L, D), BF)]
    return pl.pallas_call(
        body, grid=(L // tl,), name="attn_fwd",
        in_specs=[_rows(tl, D), _rows(tl, D), _rows(tl, D), _const((2 * D, D)), _const((1, D)), _const((D, D)),
                  _const((MEM_LEN, D)), _const((MEM_LEN, D)), _const((D, D))],
        out_specs=[_rows(tl, D)] * 5, out_shape=outs, compiler_params=_cparams())(x, ya, ob, w_out, nxa, wq, k, v, wo)


def _ffn_loss(x2, tgt, nffn, nfin, wgt, wut, wd):
    L = x2.shape[0]
    tl = min(TL, L)

    def body(x2_ref, t_ref, nffn_ref, nfin_ref, wg_ref, wu_ref, wd_ref,
             dx2_ref, h_ref, a_ref, dx3_ref, dg_ref, du_ref, acc_ref):
        _zero_first(acc_ref)
        x2v = x2_ref[...]
        h, n2, r2 = _rms(x2v, nffn_ref[...])
        hb = h.astype(BF)
        h_ref[...] = hb
        g = _dot_nt(hb, wg_ref[...])
        u = _dot_nt(hb, wu_ref[...])
        sg = _sigmoid(g)
        ab = (g * sg * u).astype(BF)
        a_ref[...] = ab
        x3 = x2v + jnp.dot(ab, wd_ref[...], preferred_element_type=F32)
        y, n3, r3 = _rms(x3, nfin_ref[...])
        err = y - t_ref[...]
        acc_ref[0:1, :] += _colsum(err * err)
        dx3, dwf = _rms_bwd(err * (1.0 / D), n3, r3, nfin_ref[...])
        acc_ref[1:2, :] += _colsum(dwf)
        dx3b = dx3.astype(BF)
        dx3_ref[...] = dx3b
        da = _dot_nt(dx3b, wd_ref[...])
        dgb = (da * u * sg * (1.0 + g * (1.0 - sg))).astype(BF)
        dub = (da * g * sg).astype(BF)
        dg_ref[...] = dgb
        du_ref[...] = dub
        dh = jnp.dot(dgb, wg_ref[...], preferred_element_type=F32) + jnp.dot(dub, wu_ref[...], preferred_element_type=F32)
        dn, dwn = _rms_bwd(dh, n2, r2, nffn_ref[...])
        acc_ref[2:3, :] += _colsum(dwn)
        dx2_ref[...] = dx3 + dn

    outs = [SDS((L, D), F32), SDS((L, D), BF), SDS((L, FFN), BF), SDS((L, D), BF), SDS((L, FFN), BF),
            SDS((L, FFN), BF), SDS((8, D), F32)]
    wspec = pl.BlockSpec((FFN, D), lambda i: (0, 0), pipeline_mode=pl.Buffered(1))
    return pl.pallas_call(
        body, grid=(L // tl,), name="ffn_loss",
        in_specs=[_rows(tl, D), _rows(tl, D), _const((1, D)), _const((1, D)), wspec, wspec, wspec],
        out_specs=[_rows(tl, D), _rows(tl, D), _rows(tl, FFN), _rows(tl, D), _rows(tl, FFN), _rows(tl, FFN),
                   _const((8, D))],
        out_shape=outs, compiler_params=_cparams())(x2, tgt, nffn, nfin, wgt, wut, wd)


def _attn_bwd(dx2, x1, q, k, v, nxa, wq, wo, w_out):
    L = dx2.shape[0]
    tl = min(TL, L)
    scale = XA_HD ** -0.5

    def body(dx2_ref, x1_ref, q_ref, k_ref, v_ref, nxa_ref, wq_ref, wo_ref, wout_ref,
             dx1_ref, dya_ref, dob_ref, dq_ref, dk_ref, dv_ref, acc_ref):
        _zero_first(dk_ref, dv_ref, acc_ref)
        dx2v = dx2_ref[...]
        dox = _dot_nt(dx2v, wo_ref[...]).astype(BF)
        qb = q_ref[...]
        dqs = []
        for hd in range(XA_HEADS):
            sl = slice(hd * XA_HD, (hd + 1) * XA_HD)
            kh, vh, qh, doh = k_ref[:, sl], v_ref[:, sl], qb[:, sl], dox[:, sl]
            p = _softmax_rows(_dot_nt(qh, kh) * scale)
            dp = _dot_nt(doh, vh)
            dv_ref[:, sl] += _dot_tn(p, doh)
            ds = p * (dp - jnp.sum(dp * p, axis=-1, keepdims=True)) * scale
            dqs.append(_dot(ds, kh))
            dk_ref[:, sl] += _dot_tn(ds, qh)
        dqb = jnp.concatenate(dqs, axis=1).astype(BF)
        dq_ref[...] = dqb
        dh = _dot_nt(dqb, wq_ref[...])
        _, n1, r1 = _rms(x1_ref[...], nxa_ref[...])
        dn, dwn = _rms_bwd(dh, n1, r1, nxa_ref[...])
        acc_ref[0:1, :] += _colsum(dwn)
        dx1 = dx2v + dn
        dx1_ref[...] = dx1
        dx1b = dx1.astype(BF)
        dya_ref[...] = _dot_nt(dx1b, wout_ref[:D, :]).astype(BF)
        dob_ref[...] = _dot_nt(dx1b, wout_ref[D:, :]).astype(BF)

    outs = [SDS((L, D), F32), SDS((L, D), BF), SDS((L, D), BF), SDS((L, D), BF), SDS((MEM_LEN, D), F32),
            SDS((MEM_LEN, D), F32), SDS((8, D), F32)]
    return pl.pallas_call(
        body, grid=(L // tl,), name="attn_bwd",
        in_specs=[_rows(tl, D), _rows(tl, D), _rows(tl, D), _const((MEM_LEN, D)), _const((MEM_LEN, D)), _const((1, D)),
                  _const((D, D)), _const((D, D)), _const((2 * D, D))],
        out_specs=[_rows(tl, D)] * 4 + [_const((MEM_LEN, D)), _const((MEM_LEN, D)), _const((8, D))],
        out_shape=outs, compiler_params=_cparams())(dx2, x1, q, k, v, nxa, wq, wo, w_out)


def _in_proj_bwd(x, dx1, dz, dxbc, dhq, dhf, dhi, dhg, ddt, nw, wt):
    L = x.shape[0]
    tl = min(TL, L)

    def body(x_ref, dx1_ref, dz_ref, dxbc_ref, dhq_ref, dhf_ref, dhi_ref, dhg_ref, ddt_ref, nw_ref, w_ref,
             gx_ref, acc_ref):
        _zero_first(acc_ref)
        dh = _dot(dz_ref[...], w_ref[Z0:XBC0, :]) + _dot(dxbc_ref[...], w_ref[XBC0:DT0, :]) \
            + _dot(ddt_ref[...], w_ref[DT0:DT0 + 128, :]) + _dot(dhq_ref[...], w_ref[HQ0:HF0, :]) \
            + _dot(dhf_ref[...], w_ref[HF0:HI0, :]) + _dot(dhi_ref[...], w_ref[HI0:HG0, :]) \
            + _dot(dhg_ref[...], w_ref[HG0:N_IN, :])
        _, n, r = _rms(x_ref[...], nw_ref[...])
        dn, dwn = _rms_bwd(dh, n, r, nw_ref[...])
        acc_ref[0:1, :] += _colsum(dwn)
        gx_ref[...] = dx1_ref[...] + dn

    return pl.pallas_call(
        body, grid=(L // tl,), name="in_proj_bwd",
        in_specs=[_rows(tl, D), _rows(tl, D), _rows(tl, D), _rows(tl, 1536), _rows(tl, D), _rows(tl, D), _rows(tl, D),
                  _rows(tl, D), _rows(tl, 128), _const((1, D)), _const((N_IN, D))],
        out_specs=[_rows(tl, D), _const((8, D))], out_shape=[SDS((L, D), F32), SDS((8, D), F32)],
        compiler_params=_cparams())(x, dx1, dz, dxbc, dhq, dhf, dhi, dhg, ddt, nw, wt)


def _gw_in(h0, dz, dxbc, ddt, dhq, dhf, dhi, dhg):
    L = h0.shape[0]
    tl = min(512, L)

    def body(h_ref, dz_ref, dxbc_ref, ddt_ref, dhq_ref, dhf_ref, dhi_ref, dhg_ref, o_ref):
        _zero_first(o_ref)
        hb = h_ref[...]
        o_ref[Z0:XBC0, :] += _dot_tn(dz_ref[...], hb)
        o_ref[XBC0:DT0, :] += _dot_tn(dxbc_ref[...], hb)
        o_ref[DT0:HQ0, :] += _dot_tn(ddt_ref[...], hb)[0:NH_SSD, :]
        o_ref[HQ0:HF0, :] += _dot_tn(dhq_ref[...], hb)
        o_ref[HF0:HI0, :] += _dot_tn(dhf_ref[...], hb)
        o_ref[HI0:HG0, :] += _dot_tn(dhi_ref[...], hb)
        o_ref[HG0:N_IN, :] += _dot_tn(dhg_ref[...], hb)

    return pl.pallas_call(
        body, grid=(L // tl,), name="gw_in",
        in_specs=[_rows(tl, D), _rows(tl, D), _rows(tl, 1536), _rows(tl, 128), _rows(tl, D), _rows(tl, D),
                  _rows(tl, D), _rows(tl, D)],
        out_specs=_const((N_IN, D)), out_shape=SDS((N_IN, D), F32), compiler_params=_cparams())(
            h0, dz, dxbc, ddt, dhq, dhf, dhi, dhg)


def _matmul_tn(a, b, name):
    L, M = a.shape
    N = b.shape[1]
    tl = min(512, L)

    def body(a_ref, b_ref, o_ref):
        _zero_first(o_ref)
        o_ref[...] += _dot_tn(a_ref[...], b_ref[...])

    return pl.pallas_call(
        body, grid=(L // tl,), name=name, in_specs=[_rows(tl, M), _rows(tl, N)], out_specs=_const((M, N)),
        out_shape=SDS((M, N), F32), compiler_params=_cparams())(a, b)


def _matmul_tn_pair(a0, a1, b, name):
    L, M = a0.shape
    N = b.shape[1]
    tl = min(512, L)

    def body(a0_ref, a1_ref, b_ref, o_ref):
        _zero_first(o_ref)
        bv = b_ref[...].astype(BF)
        o_ref[:M, :] += _dot_tn(a0_ref[...], bv)
        o_ref[M:, :] += _dot_tn(a1_ref[...], bv)

    return pl.pallas_call(
        body, grid=(L // tl,), name=name, in_specs=[_rows(tl, M), _rows(tl, M), _rows(tl, N)],
        out_specs=_const((2 * M, N)), out_shape=SDS((2 * M, N), F32), compiler_params=_cparams())(a0, a1, b)


def _head_expand():
    e = (jnp.right_shift(_iota((128, D), 1), 6) == _iota((128, D), 0)).astype(F32)
    et = (jnp.right_shift(_iota((D, 128), 0), 6) == _iota((D, 128), 1)).astype(F32)
    return e, et


def _conv_shifts(cur, other, up):
    rows = _iota((Q, 1), 0)
    out = []
    for s in (1, 2, 3):
        if up:
            out.append(jnp.where(rows >= Q - s, pltpu.roll(other, Q - s, 0), pltpu.roll(cur, Q - s, 0)))
        else:
            out.append(jnp.where(rows < s, pltpu.roll(other, s, 0), pltpu.roll(cur, s, 0)))
    return out


def _ssd_pre(u, dtr, dtb, alog):
    e, et = _head_expand()
    sgu = _sigmoid(u)
    xc = u * sgu
    lane = _iota((1, 128), 1)
    hmask = (lane < NH_SSD).astype(F32)
    pre = dtr + dtb
    dt = (jnp.maximum(pre, 0.0) + jnp.log(1.0 + jnp.exp(-jnp.abs(pre)))) * hmask
    a_row = -jnp.exp(alog)
    causal = _iota((Q, Q), 1) <= _iota((Q, Q), 0)
    tri = causal.astype(F32)
    acum = _dot_sel(tri, dt * a_row)
    acum_full = _sel_dot(acum, e)
    alast_full = acum_full[Q - 1:Q, :]
    dt_full = _sel_dot(dt, e)
    xs = xc[:, :D]
    return dict(e=e, et=et, sgu=sgu, xs=xs, bm=xc[:, D:D + 256], cm=xc[:, D + 256:], hmask=hmask, pre=pre, dt=dt,
                a_row=a_row, causal=causal, tri=tri, acum=acum, acum_t=acum.T, eA_full=jnp.exp(acum_full),
                dte_full=jnp.exp(alast_full - acum_full), dt_full=dt_full, xdt=xs * dt_full)


def _ssd_decay(pre, hh, cb):
    seg = pre["acum"][:, hh:hh + 1] - pre["acum_t"][hh:hh + 1, :]
    lm = jnp.where(pre["causal"], jnp.exp(jnp.minimum(seg, 0.0)), 0.0)
    return lm, cb * lm


def _ssd_fwd(xbc, dtr, z, conv_w, conv_b, dtb, alog, dskip_full, nw):
    L = xbc.shape[0]
    nc = L // Q

    def body(xbc_ref, dtr_ref, z_ref, cw_ref, cb_ref, dtb_ref, alog_ref, dsk_ref, nw_ref,
             ya_ref, y_ref, u_ref, st_ref, prev_ref, s_ref):
        @pl.when(pl.program_id(0) == 0)
        def _():
            prev_ref[...] = jnp.zeros_like(prev_ref)
            s_ref[...] = jnp.zeros_like(s_ref)

        xr = xbc_ref[...].astype(F32)
        sh = _conv_shifts(xr, prev_ref[...], up=False)
        u = cb_ref[...] + cw_ref[3:4, :] * xr + cw_ref[2:3, :] * sh[0] + cw_ref[1:2, :] * sh[1] + cw_ref[0:1, :] * sh[2]
        prev_ref[...] = xr
        ub = u.astype(BF)
        u_ref[...] = ub
        pre = _ssd_pre(ub.astype(F32), dtr_ref[...], dtb_ref[...], alog_ref[...])
        lo = _iota((1, 128), 1) < SSD_P
        s_old = s_ref[...]
        st_ref[0] = s_old
        ys = []
        for g in range(2):
            bg, cg = pre["bm"][:, 128 * g:128 * g + 128], pre["cm"][:, 128 * g:128 * g + 128]
            cb = _dot_nt(cg, bg)
            gs = slice(512 * g, 512 * g + 512)
            yd = []
            for j in range(4 * g, 4 * g + 4):
                xp = pre["xdt"][:, 128 * j:128 * j + 128].astype(BF)
                _, m0 = _ssd_decay(pre, 2 * j, cb)
                _, m1 = _ssd_decay(pre, 2 * j + 1, cb)
                yd.append(jnp.where(lo, _dot(m0, xp), _dot(m1, xp)))
            yoff = _dot_nt(cg, s_old[gs, :]) * pre["eA_full"][:, gs]
            ys.append(jnp.concatenate(yd, axis=1) + yoff)
            st = _dot_tn((pre["xdt"] * pre["dte_full"])[:, gs], bg)
            cdcol = jnp.exp(_dot_sel(pre["et"][gs, :], pre["acum_t"])[:, Q - 1:Q])
            s_ref[gs, :] = s_old[gs, :] * cdcol + st
        y = jnp.concatenate(ys, axis=1) + dsk_ref[...] * pre["xs"]
        yb = y.astype(BF)
        y_ref[...] = yb
        zf = z_ref[...].astype(F32)
        yz = yb.astype(F32) * zf * _sigmoid(zf)
        outs = []
        for g in range(2):
            gs = slice(512 * g, 512 * g + 512)
            o, _, _ = _rms(yz[:, gs], nw_ref[:, gs])
            outs.append(o)
        ya_ref[...] = jnp.concatenate(outs, axis=1).astype(BF)

    outs = [SDS((L, D), BF), SDS((L, D), BF), SDS((L, 1536), BF), SDS((nc, D, 128), F32)]
    return pl.pallas_call(
        body, grid=(nc,), name="ssd_fwd",
        in_specs=[_rows(Q, 1536), _rows(Q, 128), _rows(Q, D), _const((4, 1536)), _const((1, 1536)), _const((1, 128)),
                  _const((1, 128)), _const((1, D)), _const((1, D))],
        out_specs=[_rows(Q, D), _rows(Q, D), _rows(Q, 1536), pl.BlockSpec((1, D, 128), lambda i: (i, 0, 0))],
        out_shape=outs, scratch_shapes=[pltpu.VMEM((Q, 1536), F32), pltpu.VMEM((D, 128), F32)],
        compiler_params=_cparams())(xbc, dtr, z, conv_w, conv_b, dtb, alog, dskip_full, nw)


def _ssd_bwd(dya, y, z, u, xbc, dtr, states, conv_w, dtb, alog, dskip_full, nw):
    L = dya.shape[0]
    nc = L // Q

    def body(dya_ref, y_ref, z_ref, u_ref, xc_ref, xp_ref, dtr_ref, st_ref, cw_ref, dtb_ref, alog_ref, dsk_ref, nw_ref,
             dz_ref, dxbc_ref, ddt_ref, gconv_ref, ghead_ref, glane_ref, gs_ref, ndu_ref):
        step = pl.program_id(0)

        @pl.when(step == 0)
        def _():
            for r in (gconv_ref, ghead_ref, glane_ref, gs_ref, ndu_ref):
                r[...] = jnp.zeros_like(r)

        uf = u_ref[...].astype(F32)
        pre = _ssd_pre(uf, dtr_ref[...], dtb_ref[...], alog_ref[...])
        e, et, xs, xdt = pre["e"], pre["et"], pre["xs"], pre["xdt"]
        lane = _iota((1, 128), 1)
        lo = lane < SSD_P
        sub = _iota((128, 1), 0)
        zf = z_ref[...].astype(F32)
        sgz = _sigmoid(zf)
        sz = zf * sgz
        yv = y_ref[...].astype(F32)
        yz = yv * sz
        dyav = dya_ref[...].astype(F32)
        dyz, dnw = [], []
        for g in range(2):
            gs = slice(512 * g, 512 * g + 512)
            _, n, r = _rms(yz[:, gs], nw_ref[:, gs])
            dv, dw = _rms_bwd(dyav[:, gs], n, r, nw_ref[:, gs])
            dyz.append(dv)
            dnw.append(dw)
        dyz = jnp.concatenate(dyz, axis=1)
        glane_ref[1:2, :] += _colsum(jnp.concatenate(dnw, axis=1))
        dy = dyz * sz
        dz_ref[...] = (dyz * yv * sgz * (1.0 + zf * (1.0 - sgz))).astype(BF)
        glane_ref[0:1, :] += _colsum(dy * xs)
        dxs = dsk_ref[...] * dy

        s_in = st_ref[0]
        gst = gs_ref[...]
        gy = dy * pre["eA_full"]
        xdte = xdt * pre["dte_full"]
        dacum = jnp.zeros((Q, 128), F32)
        dacum_t = jnp.zeros((128, Q), F32)
        dxdt, dacum_full, ddte_full, dbs, dcs = [], [], [], [], []
        for g in range(2):
            gs = slice(512 * g, 512 * g + 512)
            bg, cg = pre["bm"][:, 128 * g:128 * g + 128], pre["cm"][:, 128 * g:128 * g + 128]
            sg_, dg_ = s_in[gs, :], gst[gs, :]
            yoff = _dot_nt(cg, sg_) * pre["eA_full"][:, gs]
            dc = _dot(gy[:, gs], sg_)
            dsin = _dot_tn(gy[:, gs], cg)
            dacum_full.append(dy[:, gs] * yoff)
            tg = _dot_nt(bg, dg_)
            ddte_full.append(tg * xdt[:, gs])
            db = _dot(xdte[:, gs], dg_)
            cb = _dot_nt(cg, bg)
            dcb = jnp.zeros((Q, Q), F32)
            dxg = []
            for j in range(4 * g, 4 * g + 4):
                xp = xdt[:, 128 * j:128 * j + 128].astype(BF)
                dyp = dy[:, 128 * j:128 * j + 128]
                dxp = jnp.zeros((Q, 128), F32)
                for idx in range(2):
                    hh = 2 * j + idx
                    lm, m = _ssd_decay(pre, hh, cb)
                    dym = jnp.where(lo if idx == 0 else jnp.logical_not(lo), dyp, 0.0).astype(BF)
                    dm = jnp.where(pre["causal"], _dot_nt(dym, xp), 0.0)
                    w = dm * m
                    dacum = dacum + jnp.where(lane == hh, jnp.sum(w, axis=1, keepdims=True), 0.0)
                    dacum_t = dacum_t + jnp.where(sub == hh, jnp.sum(w, axis=0, keepdims=True), 0.0)
                    dcb = dcb + dm * lm
                    dxp = dxp + _dot_tn(m, dym)
                dxg.append(dxp)
            dxdt.append(jnp.concatenate(dxg, axis=1) + tg * pre["dte_full"][:, gs])
            dcs.append(dc + _dot(dcb, bg))
            dbs.append(db + _dot_tn(dcb, cg))
            cdcol = jnp.exp(_dot_sel(et[gs, :], pre["acum_t"])[:, Q - 1:Q])
            gs_ref[gs, :] = dsin + dg_ * cdcol
        dxdt = jnp.concatenate(dxdt, axis=1)
        dacum = dacum + _sel_dot(jnp.concatenate(dacum_full, axis=1), et, 2) - dacum_t.T
        alast = pre["acum"][Q - 1:Q, :]
        dte = jnp.exp(alast - pre["acum"])
        ddte = _sel_dot(jnp.concatenate(ddte_full, axis=1), et, 2) * dte
        dacum = dacum - ddte
        dcd_col = jnp.sum(_dot_sel(e, gst * s_in, 2), axis=1, keepdims=True)
        dcd_row = jnp.broadcast_to(dcd_col, (128, 128)).T[0:1, :]
        dalast = _colsum(ddte) + dcd_row * jnp.exp(alast)
        dacum = dacum + jnp.where(_iota((Q, 1), 0) == Q - 1, dalast, 0.0)
        ddt = _sel_dot(dxdt * xs, et, 2)
        dxs = dxs + dxdt * pre["dt_full"]
        dda = _dot_sel(pre["tri"].T, dacum)
        ddt = ddt + dda * pre["a_row"]
        ghead_ref[1:2, :] += _colsum(dda * pre["dt"])
        ddtr = ddt * _sigmoid(pre["pre"]) * pre["hmask"]
        ghead_ref[0:1, :] += _colsum(ddtr)
        ddt_ref[...] = ddtr

        dxc = jnp.concatenate([dxs] + dbs + dcs, axis=1)
        sgu = pre["sgu"]
        du = dxc * sgu * (1.0 + uf * (1.0 - sgu))
        shu = _conv_shifts(du, ndu_ref[...], up=True)
        dxr = cw_ref[3:4, :] * du + cw_ref[2:3, :] * shu[0] + cw_ref[1:2, :] * shu[1] + cw_ref[0:1, :] * shu[2]
        ndu_ref[...] = du
        dxbc_ref[...] = dxr.astype(BF)
        xr = xc_ref[...].astype(F32)
        xprev = jnp.where(step == nc - 1, 0.0, xp_ref[...].astype(F32))
        shx = _conv_shifts(xr, xprev, up=False)
        gconv_ref[3:4, :] += _colsum(du * xr)
        gconv_ref[2:3, :] += _colsum(du * shx[0])
        gconv_ref[1:2, :] += _colsum(du * shx[1])
        gconv_ref[0:1, :] += _colsum(du * shx[2])
        gconv_ref[4:5, :] += _colsum(du)

        @pl.when(step == nc - 1)
        def _():
            ghead_ref[2:3, :] = ghead_ref[1:2, :] * pre["a_row"]
            ghead_ref[3:4, :] = _dot_hi(glane_ref[...], et)[0:1, :]

    rev = lambda i: (nc - 1 - i, 0)
    outs = [SDS((L, D), BF), SDS((L, 1536), BF), SDS((L, 128), F32), SDS((8, 1536), F32), SDS((8, 128), F32),
            SDS((8, D), F32)]
    return pl.pallas_call(
        body, grid=(nc,), name="ssd_bwd",
        in_specs=[pl.BlockSpec((Q, D), rev), pl.BlockSpec((Q, D), rev), pl.BlockSpec((Q, D), rev),
                  pl.BlockSpec((Q, 1536), rev), pl.BlockSpec((Q, 1536), rev),
                  pl.BlockSpec((Q, 1536), lambda i: (jnp.maximum(nc - 2 - i, 0), 0)),
                  pl.BlockSpec((Q, 128), rev), pl.BlockSpec((1, D, 128), lambda i: (nc - 1 - i, 0, 0)),
                  _const((4, 1536)), _const((1, 128)), _const((1, 128)), _const((1, D)), _const((1, D))],
        out_specs=[pl.BlockSpec((Q, D), rev), pl.BlockSpec((Q, 1536), rev), pl.BlockSpec((Q, 128), rev),
                   _const((8, 1536)), _const((8, 128)), _const((8, D))],
        out_shape=outs, scratch_shapes=[pltpu.VMEM((D, 128), F32), pltpu.VMEM((Q, 1536), F32)],
        compiler_params=_cparams())(dya, y, z, u, xbc, xbc, dtr, states, conv_w, dtb, alog, dskip_full, nw)


def _hg_gates(hq, hf, hlb):
    h0, h1 = hlb[0:1, :], hlb[1:2, :]
    mx = jnp.maximum(h0, h1)
    e0, e1 = jnp.exp(h0 - mx), jnp.exp(h1 - mx)
    lb = e0 / (e0 + e1)
    sg = _sigmoid(hf)
    fg = lb + (1.0 - lb) * sg
    tri = (_iota((Q, Q), 1) <= _iota((Q, Q), 0)).astype(F32)
    return hq * _sigmoid(hq), 1.0 - fg, fg, sg, lb, e1 / (e0 + e1), _dot_sel(tri, jnp.log(fg))


def _hg_intra(b, q, k):
    rowblk = jnp.right_shift(_iota((Q, 1), 0), 4)
    mids = [b[SUB * i + SUB // 2:SUB * i + SUB // 2 + 1, :] for i in range(NSUB)]
    prevs = [mids[0]] + [b[SUB * i - 1:SUB * i, :] for i in range(1, NSUB)]
    mfull = jnp.concatenate([jnp.broadcast_to(r, (SUB, 128)) for r in mids], axis=0)
    rfull = jnp.concatenate([jnp.broadcast_to(r, (SUB, 128)) for r in prevs], axis=0)
    eqd, ek, eqo = jnp.exp(b - mfull), jnp.exp(mfull - b), jnp.exp(b - rfull)
    qd, qo, khat = q * eqd, q * eqo, k * ek
    rtab = jnp.concatenate(prevs, axis=0)
    djs = [jnp.exp(rtab - mids[j]) for j in range(NSUB)]
    zero = jnp.zeros((SUB, 128), F32)
    cols = []
    for j in range(NSUB):
        pieces = []
        for i in range(NSUB):
            rs = slice(SUB * i, SUB * i + SUB)
            pieces.append(zero if i < j else qd[rs] if i == j else qo[rs] * djs[j][i:i + 1, :])
        cols.append(jnp.concatenate(pieces, axis=0))
    qt = jnp.concatenate(cols, axis=1).astype(BF)
    kt = jnp.concatenate([jnp.where(rowblk == j, khat, 0.0) for j in range(NSUB)], axis=1).astype(BF)
    causal = _iota((Q, Q), 1) <= _iota((Q, Q), 0)
    att = jnp.where(causal, _dot_nt(qt, kt), 0.0)
    return att, qt, kt, (eqd, ek, eqo, djs), causal


def _hg_intra_bwd(dqt, dkt, qt, kt, factors):
    eqd, ek, eqo, djs = factors
    dqd, dqo, dkh, db = [], [], [], []
    for i in range(NSUB):
        rs = slice(SUB * i, SUB * i + SUB)
        diag = slice(128 * i, 128 * i + 128)
        dqd.append(dqt[rs, diag])
        dkh.append(dkt[rs, diag])
        dbi = qt[rs, diag].astype(F32) * dqt[rs, diag] - kt[rs, diag].astype(F32) * dkt[rs, diag]
        acc = jnp.zeros((SUB, 128), F32)
        for j in range(i):
            bl = slice(128 * j, 128 * j + 128)
            acc = acc + dqt[rs, bl] * djs[j][i:i + 1, :]
            dbi = dbi + qt[rs, bl].astype(F32) * dqt[rs, bl]
        dqo.append(acc)
        db.append(dbi)
    cat = lambda t: jnp.concatenate(t, axis=0)
    return cat(dqd) * eqd + cat(dqo) * eqo, cat(dkh) * ek, cat(db)


def _hg_fwd(hq, hf, hi, hg, hlb, nw):
    L = hq.shape[0]
    nc = L // Q

    def body(hq_ref, hf_ref, hi_ref, hg_ref, hlb_ref, nw_ref, ob_ref, o_ref, st_ref, s_ref):
        @pl.when(pl.program_id(0) == 0)
        def _():
            s_ref[...] = jnp.zeros_like(s_ref)

        qf, kf, _, _, _, _, bcum = _hg_gates(hq_ref[...].astype(F32), hf_ref[...], hlb_ref[...])
        gate = hg_ref[...].astype(F32)
        heads = [slice(128 * h, 128 * h + 128) for h in range(NH_HG)]
        atts = [_hg_intra(bcum[:, sl], qf[:, sl], kf[:, sl])[0].astype(BF) for sl in heads]
        olds = [s_ref[sl, :] for sl in heads]
        outs_ = [_dot(att, hi_ref[:, sl]) + _dot(qf[:, sl] * jnp.exp(bcum[:, sl]), s)
                 for att, sl, s in zip(atts, heads, olds)]
        for sl, s, o in zip(heads, olds, outs_):
            b, k = bcum[:, sl], kf[:, sl]
            st_ref[0, sl, :] = s
            blast = b[Q - 1:Q, :]
            s_ref[sl, :] = s * jnp.exp(b.T[:, Q - 1:Q]) + _dot_tn(k * jnp.exp(blast - b), hi_ref[:, sl])
            ob = o.astype(BF)
            o_ref[:, sl] = ob
            on, _, _ = _rms(ob.astype(F32), nw_ref[...])
            gt = gate[:, sl]
            ob_ref[:, sl] = (on * gt * _sigmoid(gt)).astype(BF)

    outs = [SDS((L, D), BF), SDS((L, D), BF), SDS((nc, D, 128), F32)]
    return pl.pallas_call(
        body, grid=(nc,), name="hg_fwd",
        in_specs=[_rows(Q, D), _rows(Q, D), _rows(Q, D), _rows(Q, D), _const((2, D)), _const((1, 128))],
        out_specs=[_rows(Q, D), _rows(Q, D), pl.BlockSpec((1, D, 128), lambda i: (i, 0, 0))],
        out_shape=outs, scratch_shapes=[pltpu.VMEM((D, 128), F32)],
        compiler_params=_cparams())(hq, hf, hi, hg, hlb, nw)


def _hg_bwd(dob, o, hq, hf, hi, hg, states, hlb, nw):
    L = dob.shape[0]
    nc = L // Q

    def body(dob_ref, o_ref, hq_ref, hf_ref, hi_ref, hg_ref, st_ref, hlb_ref, nw_ref,
             dhq_ref, dhf_ref, dhi_ref, dhg_ref, acc_ref, gs_ref):
        step = pl.program_id(0)

        @pl.when(step == 0)
        def _():
            acc_ref[...] = jnp.zeros_like(acc_ref)
            gs_ref[...] = jnp.zeros_like(gs_ref)

        hqv = hq_ref[...].astype(F32)
        qf, kf, fg, sg, lb, sm1, bcum = _hg_gates(hqv, hf_ref[...], hlb_ref[...])
        gate = hg_ref[...].astype(F32)
        sgg = _sigmoid(gate)
        nwv = nw_ref[...]
        tri_t = (_iota((Q, Q), 1) >= _iota((Q, Q), 0)).astype(F32)
        ones8 = jnp.ones((8, 128), F32)
        heads = [slice(128 * h, 128 * h + 128) for h in range(NH_HG)]
        row_last = _iota((Q, 1), 0) == Q - 1
        dobs, dnws = [], []
        for sl in heads:
            gt, sgt = gate[:, sl], sgg[:, sl]
            _, n, r = _rms(o_ref[:, sl].astype(F32), nwv)
            dobv = dob_ref[:, sl].astype(F32)
            dhg_ref[:, sl] = (dobv * n * nwv * sgt * (1.0 + gt * (1.0 - sgt))).astype(BF)
            do, dw = _rms_bwd(dobv * gt * sgt, n, r, nwv)
            dnws.append(_colsum(dw))
            dobs.append(do.astype(BF))
        intra = [_hg_intra(bcum[:, sl], qf[:, sl], kf[:, sl]) for sl in heads]
        states = [(st_ref[0, sl, :], gs_ref[sl, :]) for sl in heads]
        das = [jnp.where(it[4], _dot_nt(dob_h, hi_ref[:, sl]), 0.0).astype(BF)
               for it, dob_h, sl in zip(intra, dobs, heads)]
        dqhats = [_dot_nt(dob_h, s) for dob_h, (s, _) in zip(dobs, states)]
        dkhats = [_dot_nt(hi_ref[:, sl], gst) for sl, (_, gst) in zip(heads, states)]
        dqts = [jnp.dot(da, it[2], preferred_element_type=F32) for da, it in zip(das, intra)]
        dkts = [lax.dot_general(da, it[1], (((0,), (0,)), ((), ())), preferred_element_type=F32)
                for da, it in zip(das, intra)]
        dqs, dks, dgls = [], [], []
        for h, sl in enumerate(heads):
            b, q, k = bcum[:, sl], qf[:, sl], kf[:, sl]
            att, qt, kt, factors, _ = intra[h]
            s, gst = states[h]
            dob_h, dqhat, dkhat = dobs[h], dqhats[h], dkhats[h]
            eb = jnp.exp(b)
            blast = b[Q - 1:Q, :]
            ekl = jnp.exp(blast - b)
            qhat, khat = q * eb, k * ekl
            dhi_ref[:, sl] = (_dot_tn(att, dob_h) + _dot(khat, gst)).astype(BF)
            dq_i, dk_i, db = _hg_intra_bwd(dqts[h], dkts[h], qt, kt, factors)
            dqs.append(dq_i + dqhat * eb)
            dks.append(dk_i + dkhat * ekl)
            qhat_r, khat_r = qhat.astype(BF).astype(F32), khat.astype(BF).astype(F32)
            dblast = _colsum(dkhat * khat_r) + lax.dot_general(
                ones8, gst * s, (((1,), (1,)), ((), ())), precision=HI, preferred_element_type=F32)[0:1, :] * jnp.exp(blast)
            db = db + qhat_r * dqhat - khat_r * dkhat + jnp.where(row_last, dblast, 0.0)
            dgls.append(_dot_sel(tri_t, db, 2))
            gs_ref[sl, :] = _dot_tn(qhat, dob_h) + gst * jnp.exp(b.T[:, Q - 1:Q])
        dq, dk, dgl = (jnp.concatenate(t, axis=1) for t in (dqs, dks, dgls))
        sgq = _sigmoid(hqv)
        dhq_ref[...] = (dq * sgq * (1.0 + hqv * (1.0 - sgq))).astype(BF)
        dfg = dgl / fg - dk
        dhf_ref[...] = (dfg * (1.0 - lb) * sg * (1.0 - sg)).astype(BF)
        acc_ref[0:1, :] += _colsum(dfg * (1.0 - sg))
        acc_ref[1:2, :] += jnp.concatenate(dnws, axis=1)

        @pl.when(step == nc - 1)
        def _():
            dlb = acc_ref[0:1, :] * lb * sm1
            acc_ref[2:3, :] = dlb
            acc_ref[3:4, :] = -dlb
            tot = acc_ref[1:2, 0:128]
            for h in range(1, NH_HG):
                tot = tot + acc_ref[1:2, 128 * h:128 * h + 128]
            acc_ref[4:5, 0:128] = tot

    rev = lambda i: (nc - 1 - i, 0)
    outs = [SDS((L, D), BF)] * 4 + [SDS((8, D), F32)]
    return pl.pallas_call(
        body, grid=(nc,), name="hg_bwd",
        in_specs=[pl.BlockSpec((Q, D), rev)] * 6 + [pl.BlockSpec((1, D, 128), lambda i: (nc - 1 - i, 0, 0)),
                                                    _const((2, D)), _const((1, 128))],
        out_specs=[pl.BlockSpec((Q, D), rev)] * 4 + [_const((8, D))],
        out_shape=outs, scratch_shapes=[pltpu.VMEM((D, 128), F32)],
        compiler_params=_cparams())(dob, o, hq, hf, hi, hg, states, hlb, nw)


def _pad_lanes(v, n=128):
    return jnp.pad(v, ((0, 0), (0, n - v.shape[1])))


def _local_step(x, mem, tgt, wg, ws):
    w_in_t = wg["w_in"].reshape(N_IN, D)
    wg_t, wu_t = wg["ffn_w_gate"].reshape(FFN, D), wg["ffn_w_up"].reshape(FFN, D)
    wd = wg["ffn_w_down"].reshape(FFN, D)
    w_out = wg["w_out"].reshape(2 * D, D)
    wq, wo = wg["xa_wq"].reshape(D, D), wg["xa_wo"].reshape(D, D)
    dtb, alog = _pad_lanes(ws["dt_bias"]), _pad_lanes(ws["a_log"])
    dskip_full = jnp.repeat(ws["d_skip"], SSD_P, axis=1)
    conv_w, conv_b = ws["conv_w"][0], ws["conv_b"]

    h0, z, xbc, hq, hf, hi, hg, dtr = _in_proj(x, ws["norm_mix_w"], w_in_t)
    ya, yssd, u, st_ssd = _ssd_fwd(xbc, dtr, z, conv_w, conv_b, dtb, alog, dskip_full, ws["ssd_norm_w"])
    ob, ohg, st_hg = _hg_fwd(hq, hf, hi, hg, ws["hg_lower_bounds"], ws["hg_norm_w"])
    kmem, vmem = _mem_kv(mem, ws["norm_mem_w"], wg["xa_wkv"])
    x1, x2, hxa, q, ox = _attn_fwd(x, ya, ob, w_out, ws["norm_xa_w"], wq, kmem, vmem, wo)
    nfin = ws["norm_final_w"].reshape(1, D)
    dx2, hffn, act, dx3, dg, du, acc_f = _ffn_loss(x2, tgt, ws["norm_ffn_w"], nfin, wg_t, wu_t, wd)
    dx1, dya, dob, dq, dk, dv, acc_a = _attn_bwd(dx2, x1, q, kmem, vmem, ws["norm_xa_w"], wq, wo, w_out)
    g_nmem, g_wkv = _mem_kv_bwd(mem, ws["norm_mem_w"], wg["xa_wkv"], dk, dv)
    dhq, dhf, dhi, dhg, acc_h = _hg_bwd(dob, ohg, hq, hf, hi, hg, st_hg, ws["hg_lower_bounds"], ws["hg_norm_w"])
    dz, dxbc, ddt, gconv, ghead, glane = _ssd_bwd(dya, yssd, z, u, xbc, dtr, st_ssd, conv_w, dtb, alog, dskip_full,
                                                  ws["ssd_norm_w"])
    gx, acc_i = _in_proj_bwd(x, dx1, dz, dxbc, dhq, dhf, dhi, dhg, ddt, ws["norm_mix_w"], w_in_t)

    gb = {
        "w_in": _gw_in(h0, dz, dxbc, ddt, dhq, dhf, dhi, dhg).reshape(4, N_IN // 4, D),
        "w_out": _matmul_tn_pair(ya, ob, dx1, "gw_out").reshape(4, D // 2, D),
        "xa_wq": _matmul_tn(hxa, dq, "gw_q").reshape(4, D // 4, D),
        "xa_wkv": g_wkv,
        "xa_wo": _matmul_tn(ox, dx2, "gw_o").reshape(4, D // 4, D),
        "ffn_w_gate": _matmul_tn(dg, hffn, "gw_gate").reshape(4, FFN // 4, D),
        "ffn_w_up": _matmul_tn(du, hffn, "gw_up").reshape(4, FFN // 4, D),
        "ffn_w_down": _matmul_tn(act, dx3, "gw_down").reshape(4, FFN // 4, D),
    }
    gs = {
        "norm_mix_w": acc_i[0:1], "conv_w": gconv[0:4][None], "conv_b": gconv[4:5],
        "dt_bias": ghead[0:1, :NH_SSD], "a_log": ghead[2:3, :NH_SSD], "d_skip": ghead[3:4, :NH_SSD],
        "ssd_norm_w": glane[1:2], "hg_lower_bounds": acc_h[2:4], "hg_norm_w": acc_h[4:5, :128],
        "norm_xa_w": acc_a[0:1], "norm_mem_w": g_nmem, "norm_ffn_w": acc_f[2:3], "norm_final_w": acc_f[1],
    }
    loss = (0.5 / D) * jnp.sum(acc_f[0])
    return loss, gx, gb, gs


def _place():
    return lax.axis_index("x"), lax.axis_index("y"), lax.axis_index("c")


def _allgather(arrays, halves, name):
    n = len(arrays)

    def body(*refs):
        ins, outs = refs[:n], refs[n:2 * n]
        send_sems, recv_sems, local_sems = refs[2 * n:]
        x, y, c = _place()
        me, sibling = (x, y, c), (x, y, 1 - c)
        chips = [(1 - x, y), (x, 1 - y), (1 - x, 1 - y)]
        waits_recv, waits_send, locals_ = [], [], []
        for a in range(n):
            hc = halves[a]

            def slot(p, ref=outs[a], hc=hc):
                if hc is None:
                    return ref.at[4 * p[0] + 2 * p[1] + p[2]]
                return ref.at[2 * p[0] + p[1], :, pl.ds(p[2] * hc, hc)]

            own = ins[a] if hc is None else ins[a].at[:, pl.ds(c * hc, hc)]

            def copy(k, piece, to, src=None, a=a, slot=slot):
                return pltpu.make_async_remote_copy(
                    src_ref=slot(piece) if src is None else src, dst_ref=slot(piece),
                    send_sem=send_sems.at[7 * a + k], recv_sem=recv_sems.at[7 * a + k],
                    device_id=to, device_id_type=MESH)

            mine = pltpu.make_async_copy(own, slot(me), local_sems.at[a])
            mine.start()
            locals_.append(mine)
            first = [copy(0, me, sibling, src=own)]
            first += [copy(1 + j, me, (*chip, c), src=own) for j, chip in enumerate(chips)]
            for cp in first:
                cp.start()
            passed = [copy(4 + j, (*chip, c), sibling) for j, chip in enumerate(chips)]
            for j, chip in enumerate(chips):
                copy(1 + j, (*chip, c), me).wait_recv()
                passed[j].start()
            waits_recv.append(copy(0, sibling, me))
            waits_recv += [copy(4 + j, (*chip, 1 - c), me) for j, chip in enumerate(chips)]
            waits_send += first + passed
        for cp in waits_recv:
            cp.wait_recv()
        for cp in waits_send:
            cp.wait_send()
        for cp in locals_:
            cp.wait()

    out_shape = [SDS((8,) + a.shape if hc is None else (4,) + a.shape, a.dtype) for a, hc in zip(arrays, halves)]
    return pl.pallas_call(
        body, name=name, in_specs=[ANY] * n, out_specs=[ANY] * n, out_shape=out_shape,
        scratch_shapes=[pltpu.SemaphoreType.DMA((7 * n,)), pltpu.SemaphoreType.DMA((7 * n,)),
                        pltpu.SemaphoreType.DMA((n,))])(*arrays)


def _send_sibling_halves(grads, halves, name):
    n = len(grads)

    def body(*refs):
        ins, outs = refs[:n], refs[n:2 * n]
        send_sems, recv_sems = refs[2 * n:]
        x, y, c = _place()
        cps = [pltpu.make_async_remote_copy(
            src_ref=ins[a].at[:, :, pl.ds((1 - c) * halves[a], halves[a])], dst_ref=outs[a],
            send_sem=send_sems.at[a], recv_sem=recv_sems.at[a], device_id=(x, y, 1 - c), device_id_type=MESH)
            for a in range(n)]
        for cp in cps:
            cp.start()
        for cp in cps:
            cp.wait()

    return pl.pallas_call(
        body, name=name, in_specs=[ANY] * n, out_specs=[ANY] * n,
        out_shape=[SDS((4, g.shape[1], hc), g.dtype) for g, hc in zip(grads, halves)],
        scratch_shapes=[pltpu.SemaphoreType.DMA((n,)), pltpu.SemaphoreType.DMA((n,))])(*grads)


def _send_chips(parts, name):
    n = len(parts)

    def body(*refs):
        ins, outs = refs[:n], refs[n:2 * n]
        send_sems, recv_sems = refs[2 * n:]
        x, y, c = _place()
        chips = [(1 - x, y), (x, 1 - y), (1 - x, 1 - y)]
        cps = [pltpu.make_async_remote_copy(
            src_ref=ins[a].at[2 * px + py], dst_ref=outs[a].at[k], send_sem=send_sems.at[3 * a + k],
            recv_sem=recv_sems.at[3 * a + k], device_id=(px, py, c), device_id_type=MESH)
            for a in range(n) for k, (px, py) in enumerate(chips)]
        for cp in cps:
            cp.start()
        for cp in cps:
            cp.wait()

    return pl.pallas_call(
        body, name=name, in_specs=[ANY] * n, out_specs=[ANY] * n,
        out_shape=[SDS((3,) + p.shape[1:], p.dtype) for p in parts],
        scratch_shapes=[pltpu.SemaphoreType.DMA((3 * n,)), pltpu.SemaphoreType.DMA((3 * n,))])(*parts)


def _swap_halves(bufs, halves, name):
    n = len(bufs)

    def body(*refs):
        outs = refs[n:2 * n]
        send_sems, recv_sems = refs[2 * n:]
        x, y, c = _place()

        def cols(a, which):
            return outs[a].at[:, pl.ds(which * halves[a], halves[a])]

        cps = [pltpu.make_async_remote_copy(
            src_ref=cols(a, c), dst_ref=cols(a, c), send_sem=send_sems.at[a], recv_sem=recv_sems.at[a],
            device_id=(x, y, 1 - c), device_id_type=MESH) for a in range(n)]
        for cp in cps:
            cp.start()
        for a in range(n):
            pltpu.make_async_remote_copy(
                src_ref=cols(a, c), dst_ref=cols(a, 1 - c), send_sem=send_sems.at[a], recv_sem=recv_sems.at[a],
                device_id=(x, y, 1 - c), device_id_type=MESH).wait_recv()
        for cp in cps:
            cp.wait_send()

    return pl.pallas_call(
        body, name=name, in_specs=[ANY] * n, out_specs=[ANY] * n,
        out_shape=[SDS(b.shape, b.dtype) for b in bufs], input_output_aliases={a: a for a in range(n)},
        scratch_shapes=[pltpu.SemaphoreType.DMA((n,)), pltpu.SemaphoreType.DMA((n,))])(*bufs)


def _tile(rows, cols, nbuf):
    budget = (VMEM_LIMIT // 3) // (2 * nbuf * 4)
    if rows % 8 == 0:
        cands = [t for t in range(8, rows + 1, 8) if rows % t == 0 and t * cols <= budget]
        pref = [t for t in cands if t % 16 == 0]
        return (max(pref) if pref else max(cands) if cands else 8), cols
    cands = [t for t in range(128, cols + 1, 128) if cols % t == 0 and rows * t <= budget]
    return rows, (max(cands) if cands else 128)


def _chip_sum(g, from_sib, place, name):
    _, rows, hc = from_sib.shape
    tr, tc = _tile(rows, hc, 4)
    ni, nj = rows // tr, hc // tc

    def body(p_ref, g_ref, s_ref, hb_ref, own_ref):
        s = g_ref[...] + s_ref[...]
        hb_ref[...] = s.astype(BF)

        @pl.when(pl.program_id(2) == p_ref[1])
        def _():
            own_ref[...] = s

    grid_spec = pltpu.PrefetchScalarGridSpec(
        num_scalar_prefetch=1, grid=(ni, nj, 4),
        in_specs=[pl.BlockSpec((None, tr, tc), lambda i, j, k, p: (k, i, p[0] * nj + j)),
                  pl.BlockSpec((None, tr, tc), lambda i, j, k, p: (k, i, j))],
        out_specs=[pl.BlockSpec((None, tr, tc), lambda i, j, k, p: (k, i, j)),
                   pl.BlockSpec((tr, tc), lambda i, j, k, p: (i, j))])
    return pl.pallas_call(
        body, grid_spec=grid_spec, name=name, out_shape=[SDS((4, rows, hc), BF), SDS((rows, hc), F32)],
        compiler_params=pltpu.CompilerParams(dimension_semantics=("arbitrary",) * 3,
                                             vmem_limit_bytes=VMEM_LIMIT))(place, g, from_sib)


def _total(own, parts, place, name):
    rows, hc = own.shape
    tr, tc = _tile(rows, hc, 5)
    ni, nj = rows // tr, hc // tc

    def body(p_ref, own_ref, parts_ref, o_ref):
        s = own_ref[...]
        for k in range(3):
            s = s + parts_ref[k].astype(F32)
        o_ref[...] = s

    grid_spec = pltpu.PrefetchScalarGridSpec(
        num_scalar_prefetch=1, grid=(ni, nj),
        in_specs=[pl.BlockSpec((tr, tc), lambda i, j, p: (i, j)),
                  pl.BlockSpec((3, tr, tc), lambda i, j, p: (0, i, j))],
        out_specs=pl.BlockSpec((tr, tc), lambda i, j, p: (i, p[0] * nj + j)))
    return pl.pallas_call(
        body, grid_spec=grid_spec, name=name, out_shape=SDS((rows, 2 * hc), F32),
        compiler_params=pltpu.CompilerParams(dimension_semantics=("arbitrary",) * 2,
                                             vmem_limit_bytes=VMEM_LIMIT))(place, own, parts)


def _sum8(parts, name):
    R = parts.shape[1]

    def body(p_ref, o_ref):
        s = p_ref[0]
        for k in range(1, 8):
            s = s + p_ref[k]
        o_ref[...] = s

    return pl.pallas_call(
        body, grid=(1,), name=name, in_specs=[_const((8, R, 128))], out_specs=_const((R, 128)),
        out_shape=SDS((R, 128), F32), compiler_params=_cparams())(parts)


def _adamw(w, g, m, v, name):
    _, R, C = w.shape
    tr, tc = _tile(R, C, 7)
    c1 = 1.0 / (1.0 - ADAM_B1 ** ADAM_STEP)
    c2 = 1.0 / (1.0 - ADAM_B2 ** ADAM_STEP)

    def body(w_ref, g_ref, m_ref, v_ref, d_ref, nm_ref, nv_ref):
        gv = g_ref[...]
        nm = ADAM_B1 * m_ref[...] + (1.0 - ADAM_B1) * gv
        nv = ADAM_B2 * v_ref[...] + (1.0 - ADAM_B2) * gv * gv
        nm_ref[...] = nm
        nv_ref[...] = nv
        d_ref[...] = -ADAM_LR * ((nm * c1) / (jnp.sqrt(nv * c2) + ADAM_EPS) + ADAM_WD * w_ref[...])

    blk3 = pl.BlockSpec((None, tr, tc), lambda i, j: (0, i, j))
    return pl.pallas_call(
        body, grid=(R // tr, C // tc), name=name,
        in_specs=[blk3, pl.BlockSpec((tr, tc), lambda i, j: (i, j)), blk3, blk3], out_specs=[blk3] * 3,
        out_shape=[SDS((1, R, C), F32)] * 3,
        compiler_params=pltpu.CompilerParams(dimension_semantics=("arbitrary",) * 2,
                                             vmem_limit_bytes=VMEM_LIMIT))(w, g, m, v)


def _pack_small(parts):
    rows = []
    for p in parts:
        p = p.reshape(-1)
        rows.append(jnp.pad(p, (0, (-p.shape[0]) % 128)).reshape(-1, 128))
    out = jnp.concatenate(rows, axis=0)
    return jnp.pad(out, ((0, (-out.shape[0]) % 8), (0, 0)))


def _unpack_small(packed, shapes):
    out, row = [], 0
    for shp in shapes:
        n = 1
        for s in shp:
            n *= s
        nr = -(-n // 128)
        out.append(packed[row:row + nr].reshape(-1)[:n].reshape(shp))
        row += nr
    return out


def kernel(x, mem, norm_mix_w, w_in, conv_w, conv_b, dt_bias, a_log, d_skip, ssd_norm_w, hg_lower_bounds, hg_norm_w, w_out, norm_xa_w, norm_mem_w, xa_wq, xa_wkv, xa_wo, norm_ffn_w, ffn_w_gate, ffn_w_up, ffn_w_down, norm_final_w, loss_target, m_norm_mix_w, m_w_in, m_conv_w, m_conv_b, m_dt_bias, m_a_log, m_d_skip, m_ssd_norm_w, m_hg_lower_bounds, m_hg_norm_w, m_w_out, m_norm_xa_w, m_norm_mem_w, m_xa_wq, m_xa_wkv, m_xa_wo, m_norm_ffn_w, m_ffn_w_gate, m_ffn_w_up, m_ffn_w_down, m_norm_final_w, v_norm_mix_w, v_w_in, v_conv_w, v_conv_b, v_dt_bias, v_a_log, v_d_skip, v_ssd_norm_w, v_hg_lower_bounds, v_hg_norm_w, v_w_out, v_norm_xa_w, v_norm_mem_w, v_xa_wq, v_xa_wkv, v_xa_wo, v_norm_ffn_w, v_ffn_w_gate, v_ffn_w_up, v_ffn_w_down, v_norm_final_w):
    w = dict(norm_mix_w=norm_mix_w, w_in=w_in, conv_w=conv_w, conv_b=conv_b, dt_bias=dt_bias, a_log=a_log, d_skip=d_skip,
             ssd_norm_w=ssd_norm_w, hg_lower_bounds=hg_lower_bounds, hg_norm_w=hg_norm_w, w_out=w_out,
             norm_xa_w=norm_xa_w, norm_mem_w=norm_mem_w, xa_wq=xa_wq, xa_wkv=xa_wkv, xa_wo=xa_wo, norm_ffn_w=norm_ffn_w,
             ffn_w_gate=ffn_w_gate, ffn_w_up=ffn_w_up, ffn_w_down=ffn_w_down, norm_final_w=norm_final_w)
    m = dict(norm_mix_w=m_norm_mix_w, w_in=m_w_in, conv_w=m_conv_w, conv_b=m_conv_b, dt_bias=m_dt_bias, a_log=m_a_log,
             d_skip=m_d_skip, ssd_norm_w=m_ssd_norm_w, hg_lower_bounds=m_hg_lower_bounds, hg_norm_w=m_hg_norm_w,
             w_out=m_w_out, norm_xa_w=m_norm_xa_w, norm_mem_w=m_norm_mem_w, xa_wq=m_xa_wq, xa_wkv=m_xa_wkv,
             xa_wo=m_xa_wo, norm_ffn_w=m_norm_ffn_w, ffn_w_gate=m_ffn_w_gate, ffn_w_up=m_ffn_w_up,
             ffn_w_down=m_ffn_w_down, norm_final_w=m_norm_final_w)
    v = dict(norm_mix_w=v_norm_mix_w, w_in=v_w_in, conv_w=v_conv_w, conv_b=v_conv_b, dt_bias=v_dt_bias, a_log=v_a_log,
             d_skip=v_d_skip, ssd_norm_w=v_ssd_norm_w, hg_lower_bounds=v_hg_lower_bounds, hg_norm_w=v_hg_norm_w,
             w_out=v_w_out, norm_xa_w=v_norm_xa_w, norm_mem_w=v_norm_mem_w, xa_wq=v_xa_wq, xa_wkv=v_xa_wkv,
             xa_wo=v_xa_wo, norm_ffn_w=v_norm_ffn_w, ffn_w_gate=v_ffn_w_gate, ffn_w_up=v_ffn_w_up,
             ffn_w_down=v_ffn_w_down, norm_final_w=v_norm_final_w)
    xi, yi, ci = _place()
    chip = 2 * xi + yi
    place = jnp.stack([ci, chip]).astype(jnp.int32)

    def shard(t, name):
        return jnp.swapaxes(t[name], 1, 2) if name in TRANSPOSED else t[name]

    wsh = {name: shard(w, name) for name in BIG}
    halves = [wsh[name].shape[2] // 2 for name in BIG]

    gathered = _allgather([wsh[name][0].astype(BF) for name in BIG] + [conv_w[0]], halves + [None], "gather_weights")
    wg = dict(zip(BIG, gathered[:-1]))
    ws = {name: w[name] for name in SMALL}
    ws["conv_w"] = gathered[-1][0::2].transpose(1, 0, 2).reshape(1, 4, 1536)

    loss, gx, gb, gs = _local_step(x[0], mem[0], loss_target[0], wg, ws)

    glist = [gb[name] for name in BIG]
    from_sib = _send_sibling_halves(glist, halves, "grads_to_sibling")
    sums = [_chip_sum(g, s, place, "grads_chip_sum_" + name) for g, s, name in zip(glist, from_sib, BIG)]
    others = _send_chips([hb for hb, _ in sums], "grads_to_chips")
    reduced = [_total(own, o, place, "grads_total_" + name) for (_, own), o, name in zip(sums, others, BIG)]
    g_big = dict(zip(BIG, _swap_halves(reduced, halves, "grads_swap_halves")))

    small_parts = [gs[name] for name in SMALL] + [loss.reshape(1)]
    small_shapes = [gs[name].shape for name in SMALL] + [(1,)]
    packed = _allgather([_pack_small(small_parts)], [None], "gather_small")[0]
    small = _unpack_small(_sum8(packed, "small_total"), small_shapes)
    g_small = dict(zip(SMALL, small[:-1]))
    loss_all = small[-1][0]
    g_small["conv_w"] = lax.dynamic_slice_in_dim(g_small["conv_w"], chip * 384, 384, 2)

    grads, delta, new_m, new_v = {}, {}, {}, {}
    for name in BIG:
        outs = (g_big[name][None],) + tuple(_adamw(wsh[name], g_big[name], shard(m, name), shard(v, name),
                                                   "adamw_" + name))
        if name in TRANSPOSED:
            outs = tuple(jnp.swapaxes(o, 1, 2) for o in outs)
        grads[name], delta[name], new_m[name], new_v[name] = outs
    shapes = [w[name].shape for name in SMALL]
    packs = [_pack_small([t[name] for name in SMALL]) for t in (w, g_small, m, v)]
    outs = _adamw(packs[0][None], packs[1], packs[2][None], packs[3][None], "adamw_small")
    for name, g_, d_, nm_, nv_ in zip(SMALL, [g_small[n] for n in SMALL], *[_unpack_small(o[0], shapes) for o in outs]):
        grads[name] = g_.reshape(w[name].shape)
        delta[name], new_m[name], new_v[name] = d_, nm_, nv_

    return (loss_all, gx[None], *[grads[n] for n in WEIGHTS], *[delta[n] for n in WEIGHTS],
            *[new_m[n] for n in WEIGHTS], *[new_v[n] for n in WEIGHTS])
```

```python
import jax
import jax.numpy as jnp
from jax import lax
from jax.experimental import pallas as pl
from jax.experimental.pallas import tpu as pltpu

F32 = jnp.float32
BF = jnp.bfloat16
HI = lax.Precision.HIGHEST
MESH = pl.DeviceIdType.MESH
SDS = jax.ShapeDtypeStruct
ANY = pl.BlockSpec(memory_space=pl.ANY)

D = 1024
EPS = 1e-6
NH_SSD = 16
SSD_P = 64
NH_HG = 8
Q = 128
SUB = 16
NSUB = Q // SUB
HG_LB_FLOOR = 1e-4
XA_HEADS = 4
XA_HD = 256
MEM_LEN = 256
FFN = 2816
TL = 512
TL_FFN = 256
VMEM_LIMIT = 56 << 20

N_IN = 6672
Z0, XBC0, DT0, HQ0, HF0, HI0, HG0 = 0, 1024, 2560, 2576, 3600, 4624, 5648

ADAM_LR, ADAM_B1, ADAM_B2, ADAM_EPS, ADAM_WD, ADAM_STEP = 0.001, 0.9, 0.999, 1e-08, 0.01, 10

BIG = ("w_in", "w_out", "xa_wq", "xa_wkv", "xa_wo", "ffn_w_gate", "ffn_w_up", "ffn_w_down")
TRANSPOSED = ("w_in", "ffn_w_gate", "ffn_w_up")
SMALL = ("norm_mix_w", "conv_w", "conv_b", "dt_bias", "a_log", "d_skip", "ssd_norm_w", "hg_lower_bounds",
         "hg_norm_w", "norm_xa_w", "norm_mem_w", "norm_ffn_w", "norm_final_w")
WEIGHTS = ("norm_mix_w", "w_in", "conv_w", "conv_b", "dt_bias", "a_log", "d_skip", "ssd_norm_w", "hg_lower_bounds",
           "hg_norm_w", "w_out", "norm_xa_w", "norm_mem_w", "xa_wq", "xa_wkv", "xa_wo", "norm_ffn_w", "ffn_w_gate",
           "ffn_w_up", "ffn_w_down", "norm_final_w")


def _cparams():
    return pltpu.CompilerParams(dimension_semantics=("arbitrary",), vmem_limit_bytes=VMEM_LIMIT)


def _const(shape):
    return pl.BlockSpec(shape, lambda i: (0,) * len(shape))


def _resident(shape):
    return pl.BlockSpec(shape, lambda i: (0,) * len(shape), pipeline_mode=pl.Buffered(1))


def _rows(tl, n):
    return pl.BlockSpec((tl, n), lambda i: (i, 0))


def _dot(a, b):
    return jnp.dot(a.astype(BF), b.astype(BF), preferred_element_type=F32)


def _dot_nt(a, b):
    return lax.dot_general(a.astype(BF), b.astype(BF), (((1,), (1,)), ((), ())), preferred_element_type=F32)


def _dot_tn(a, b):
    return lax.dot_general(a.astype(BF), b.astype(BF), (((0,), (0,)), ((), ())), preferred_element_type=F32)


def _dot_hi(a, b):
    return jnp.dot(a, b, precision=HI, preferred_element_type=F32)


def _split(v, passes):
    parts, rest = [], v
    for p in range(passes):
        hi = rest.astype(BF)
        parts.append(hi)
        if p + 1 < passes:
            rest = rest - hi.astype(F32)
    return parts


def _sel_dot(a, sel, passes=3):
    sb = sel.astype(BF)
    out = None
    for part in _split(a, passes):
        t = jnp.dot(part, sb, preferred_element_type=F32)
        out = t if out is None else out + t
    return out


def _dot_sel(sel, b, passes=3):
    sb = sel.astype(BF)
    out = None
    for part in _split(b, passes):
        t = jnp.dot(sb, part, preferred_element_type=F32)
        out = t if out is None else out + t
    return out


def _iota(shape, dim):
    return lax.broadcasted_iota(jnp.int32, shape, dim)


def _sigmoid(v):
    return 0.5 * jnp.tanh(0.5 * v) + 0.5


def _rms(v, w):
    r = lax.rsqrt(jnp.mean(v * v, axis=-1, keepdims=True) + EPS)
    n = v * r
    return n * w, n, r


def _rms_bwd(dy, n, r, w):
    dn = dy * w
    return r * (dn - n * jnp.mean(dn * n, axis=-1, keepdims=True)), dy * n


def _colsum(v):
    return jnp.sum(v, axis=0, keepdims=True)


def _zero_first(*refs):
    @pl.when(pl.program_id(0) == 0)
    def _():
        for r in refs:
            r[...] = jnp.zeros_like(r)


def _in_proj(x, nw, wt):
    L = x.shape[0]
    tl = min(TL, L)

    def body(x_ref, nw_ref, w_ref, h0_ref, z_ref, xbc_ref, hq_ref, hf_ref, hi_ref, hg_ref, dt_ref):
        h, _, _ = _rms(x_ref[...], nw_ref[...])
        hb = h.astype(BF)
        h0_ref[...] = hb

        def proj(a, b):
            return _dot_nt(hb, w_ref[a:b, :])

        z_ref[...] = proj(Z0, XBC0).astype(BF)
        xbc_ref[...] = proj(XBC0, DT0).astype(BF)
        dt_ref[...] = proj(DT0, DT0 + 128)
        hq_ref[...] = proj(HQ0, HF0).astype(BF)
        hf_ref[...] = proj(HF0, HI0)
        hi_ref[...] = proj(HI0, HG0).astype(BF)
        hg_ref[...] = proj(HG0, N_IN).astype(BF)

    outs = [SDS((L, D), BF), SDS((L, D), BF), SDS((L, 1536), BF), SDS((L, D), BF), SDS((L, D), F32),
            SDS((L, D), BF), SDS((L, D), BF), SDS((L, 128), F32)]
    return pl.pallas_call(
        body, grid=(L // tl,), name="in_proj",
        in_specs=[_rows(tl, D), _const((1, D)), _resident((N_IN, D))],
        out_specs=[_rows(tl, o.shape[1]) for o in outs], out_shape=outs,
        compiler_params=_cparams())(x, nw, wt)


def _mem_kv(mem, nw, wkv4):
    def body(m_ref, nw_ref, w_ref, k_ref, v_ref):
        m, _, _ = _rms(m_ref[...], nw_ref[...])
        mb = m.astype(BF)
        for i in range(2):
            sl = slice(512 * i, 512 * i + 512)
            k_ref[:, sl] = jnp.dot(mb, w_ref[i], preferred_element_type=F32).astype(BF)
            v_ref[:, sl] = jnp.dot(mb, w_ref[2 + i], preferred_element_type=F32).astype(BF)

    outs = [SDS((MEM_LEN, D), BF)] * 2
    return pl.pallas_call(
        body, grid=(1,), name="mem_kv",
        in_specs=[_const((MEM_LEN, D)), _const((1, D)), _const((4, D, 512))],
        out_specs=[_const((MEM_LEN, D))] * 2, out_shape=outs, compiler_params=_cparams())(mem, nw, wkv4)


def _mem_kv_bwd(mem, nw, wkv4, dk, dv):
    def body(m_ref, nw_ref, w_ref, dk_ref, dv_ref, gnw_ref, gw_ref):
        m, n, _ = _rms(m_ref[...], nw_ref[...])
        mb = m.astype(BF)
        dm = jnp.zeros((MEM_LEN, D), F32)
        for i in range(4):
            src = dk_ref if i < 2 else dv_ref
            d = src[:, 512 * (i % 2):512 * (i % 2) + 512].astype(BF)
            gw_ref[i] = _dot_tn(mb, d)
            dm = dm + _dot_nt(d, w_ref[i])
        gnw_ref[...] = _colsum(dm * n)

    return pl.pallas_call(
        body, grid=(1,), name="mem_kv_bwd",
        in_specs=[_const((MEM_LEN, D)), _const((1, D)), _const((4, D, 512)), _const((MEM_LEN, D)), _const((MEM_LEN, D))],
        out_specs=[_const((1, D)), _const((4, D, 512))],
        out_shape=[SDS((1, D), F32), SDS((4, D, 512), F32)], compiler_params=_cparams())(mem, nw, wkv4, dk, dv)


def _softmax_rows(sc):
    e = jnp.exp(sc - jnp.max(sc, axis=-1, keepdims=True))
    return e * (1.0 / jnp.sum(e, axis=-1, keepdims=True))


def _attn_fwd(x, ya, ob, w_out, nxa, wq, k, v, wo):
    L = x.shape[0]
    tl = min(TL, L)
    scale = XA_HD ** -0.5

    def body(x_ref, ya_ref, ob_ref, wout_ref, nxa_ref, wq_ref, k_ref, v_ref, wo_ref,
             x1_ref, x2_ref, hxa_ref, q_ref, ox_ref):
        x1 = x_ref[...] + jnp.dot(ya_ref[...], wout_ref[:D, :], preferred_element_type=F32) \
            + jnp.dot(ob_ref[...], wout_ref[D:, :], preferred_element_type=F32)
        x1_ref[...] = x1
        h, _, _ = _rms(x1, nxa_ref[...])
        hb = h.astype(BF)
        hxa_ref[...] = hb
        qb = jnp.dot(hb, wq_ref[...], preferred_element_type=F32).astype(BF)
        q_ref[...] = qb
        oxs = []
        for hd in range(XA_HEADS):
            sl = slice(hd * XA_HD, (hd + 1) * XA_HD)
            p = _softmax_rows(_dot_nt(qb[:, sl], k_ref[:, sl]) * scale)
            oxs.append(_dot(p, v_ref[:, sl]))
        oxb = jnp.concatenate(oxs, axis=1).astype(BF)
        ox_ref[...] = oxb
        x2_ref[...] = x1 + jnp.dot(oxb, wo_ref[...], preferred_element_type=F32)

    outs = [SDS((L, D), F32), SDS((L, D), F32), SDS((L, D), BF), SDS((L, D), BF), SDS((L, D), BF)]
    return pl.pallas_call(
        body, grid=(L // tl,), name="attn_fwd",
        in_specs=[_rows(tl, D), _rows(tl, D), _rows(tl, D), _resident((2 * D, D)), _const((1, D)), _resident((D, D)),
                  _resident((MEM_LEN, D)), _resident((MEM_LEN, D)), _resident((D, D))],
        out_specs=[_rows(tl, D)] * 5, out_shape=outs, compiler_params=_cparams())(x, ya, ob, w_out, nxa, wq, k, v, wo)


def _ffn_loss(x2, tgt, nffn, nfin, wgt, wut, wd):
    L = x2.shape[0]
    tl = min(TL_FFN, L)

    def body(x2_ref, t_ref, nffn_ref, nfin_ref, wg_ref, wu_ref, wd_ref,
             dx2_ref, h_ref, a_ref, dx3_ref, dg_ref, du_ref, acc_ref):
        _zero_first(acc_ref)
        x2v = x2_ref[...]
        h, n2, r2 = _rms(x2v, nffn_ref[...])
        hb = h.astype(BF)
        h_ref[...] = hb
        g = _dot_nt(hb, wg_ref[...])
        u = _dot_nt(hb, wu_ref[...])
        sg = _sigmoid(g)
        ab = (g * sg * u).astype(BF)
        a_ref[...] = ab
        x3 = x2v + jnp.dot(ab, wd_ref[...], preferred_element_type=F32)
        y, n3, r3 = _rms(x3, nfin_ref[...])
        err = y - t_ref[...]
        acc_ref[0:1, :] += _colsum(err * err)
        dx3, dwf = _rms_bwd(err * (1.0 / D), n3, r3, nfin_ref[...])
        acc_ref[1:2, :] += _colsum(dwf)
        dx3b = dx3.astype(BF)
        dx3_ref[...] = dx3b
        da = _dot_nt(dx3b, wd_ref[...])
        dgb = (da * u * sg * (1.0 + g * (1.0 - sg))).astype(BF)
        dub = (da * g * sg).astype(BF)
        dg_ref[...] = dgb
        du_ref[...] = dub
        dh = jnp.dot(dgb, wg_ref[...], preferred_element_type=F32) + jnp.dot(dub, wu_ref[...], preferred_element_type=F32)
        dn, dwn = _rms_bwd(dh, n2, r2, nffn_ref[...])
        acc_ref[2:3, :] += _colsum(dwn)
        dx2_ref[...] = dx3 + dn

    outs = [SDS((L, D), F32), SDS((L, D), BF), SDS((L, FFN), BF), SDS((L, D), BF), SDS((L, FFN), BF),
            SDS((L, FFN), BF), SDS((8, D), F32)]
    wspec = _resident((FFN, D))
    return pl.pallas_call(
        body, grid=(L // tl,), name="ffn_loss",
        in_specs=[_rows(tl, D), _rows(tl, D), _const((1, D)), _const((1, D)), wspec, wspec, wspec],
        out_specs=[_rows(tl, D), _rows(tl, D), _rows(tl, FFN), _rows(tl, D), _rows(tl, FFN), _rows(tl, FFN),
                   _const((8, D))],
        out_shape=outs, compiler_params=_cparams())(x2, tgt, nffn, nfin, wgt, wut, wd)


def _attn_bwd(dx2, x1, q, k, v, nxa, wq, wo, w_out):
    L = dx2.shape[0]
    tl = min(TL, L)
    scale = XA_HD ** -0.5

    def body(dx2_ref, x1_ref, q_ref, k_ref, v_ref, nxa_ref, wq_ref, wo_ref, wout_ref,
             dx1_ref, dya_ref, dob_ref, dq_ref, dk_ref, dv_ref, acc_ref):
        _zero_first(dk_ref, dv_ref, acc_ref)
        dx2v = dx2_ref[...]
        dox = _dot_nt(dx2v, wo_ref[...]).astype(BF)
        qb = q_ref[...]
        dqs = []
        for hd in range(XA_HEADS):
            sl = slice(hd * XA_HD, (hd + 1) * XA_HD)
            kh, vh, qh, doh = k_ref[:, sl], v_ref[:, sl], qb[:, sl], dox[:, sl]
            p = _softmax_rows(_dot_nt(qh, kh) * scale)
            dp = _dot_nt(doh, vh)
            dv_ref[:, sl] += _dot_tn(p, doh)
            ds = p * (dp - jnp.sum(dp * p, axis=-1, keepdims=True)) * scale
            dqs.append(_dot(ds, kh))
            dk_ref[:, sl] += _dot_tn(ds, qh)
        dqb = jnp.concatenate(dqs, axis=1).astype(BF)
        dq_ref[...] = dqb
        dh = _dot_nt(dqb, wq_ref[...])
        _, n1, r1 = _rms(x1_ref[...], nxa_ref[...])
        dn, dwn = _rms_bwd(dh, n1, r1, nxa_ref[...])
        acc_ref[0:1, :] += _colsum(dwn)
        dx1 = dx2v + dn
        dx1_ref[...] = dx1
        dx1b = dx1.astype(BF)
        dya_ref[...] = _dot_nt(dx1b, wout_ref[:D, :]).astype(BF)
        dob_ref[...] = _dot_nt(dx1b, wout_ref[D:, :]).astype(BF)

    outs = [SDS((L, D), F32), SDS((L, D), BF), SDS((L, D), BF), SDS((L, D), BF), SDS((MEM_LEN, D), F32),
            SDS((MEM_LEN, D), F32), SDS((8, D), F32)]
    return pl.pallas_call(
        body, grid=(L // tl,), name="attn_bwd",
        in_specs=[_rows(tl, D), _rows(tl, D), _rows(tl, D), _resident((MEM_LEN, D)), _resident((MEM_LEN, D)),
                  _const((1, D)), _resident((D, D)), _resident((D, D)), _resident((2 * D, D))],
        out_specs=[_rows(tl, D)] * 4 + [_const((MEM_LEN, D)), _const((MEM_LEN, D)), _const((8, D))],
        out_shape=outs, compiler_params=_cparams())(dx2, x1, q, k, v, nxa, wq, wo, w_out)


def _in_proj_bwd(x, dx1, dz, dxbc, dhq, dhf, dhi, dhg, ddt, nw, wt):
    L = x.shape[0]
    tl = min(TL, L)

    def body(x_ref, dx1_ref, dz_ref, dxbc_ref, dhq_ref, dhf_ref, dhi_ref, dhg_ref, ddt_ref, nw_ref, w_ref,
             gx_ref, acc_ref):
        _zero_first(acc_ref)
        dh = _dot(dz_ref[...], w_ref[Z0:XBC0, :]) + _dot(dxbc_ref[...], w_ref[XBC0:DT0, :]) \
            + _dot(ddt_ref[...], w_ref[DT0:DT0 + 128, :]) + _dot(dhq_ref[...], w_ref[HQ0:HF0, :]) \
            + _dot(dhf_ref[...], w_ref[HF0:HI0, :]) + _dot(dhi_ref[...], w_ref[HI0:HG0, :]) \
            + _dot(dhg_ref[...], w_ref[HG0:N_IN, :])
        _, n, r = _rms(x_ref[...], nw_ref[...])
        dn, dwn = _rms_bwd(dh, n, r, nw_ref[...])
        acc_ref[0:1, :] += _colsum(dwn)
        gx_ref[...] = dx1_ref[...] + dn

    return pl.pallas_call(
        body, grid=(L // tl,), name="in_proj_bwd",
        in_specs=[_rows(tl, D), _rows(tl, D), _rows(tl, D), _rows(tl, 1536), _rows(tl, D), _rows(tl, D), _rows(tl, D),
                  _rows(tl, D), _rows(tl, 128), _const((1, D)), _resident((N_IN, D))],
        out_specs=[_rows(tl, D), _const((8, D))], out_shape=[SDS((L, D), F32), SDS((8, D), F32)],
        compiler_params=_cparams())(x, dx1, dz, dxbc, dhq, dhf, dhi, dhg, ddt, nw, wt)


def _gw_in(h0, dz, dxbc, ddt, dhq, dhf, dhi, dhg):
    L = h0.shape[0]
    tl = min(512, L)

    def body(h_ref, dz_ref, dxbc_ref, ddt_ref, dhq_ref, dhf_ref, dhi_ref, dhg_ref, o_ref):
        _zero_first(o_ref)
        hb = h_ref[...]
        o_ref[Z0:XBC0, :] += _dot_tn(dz_ref[...], hb)
        o_ref[XBC0:DT0, :] += _dot_tn(dxbc_ref[...], hb)
        o_ref[DT0:HQ0, :] += _dot_tn(ddt_ref[...], hb)[0:NH_SSD, :]
        o_ref[HQ0:HF0, :] += _dot_tn(dhq_ref[...], hb)
        o_ref[HF0:HI0, :] += _dot_tn(dhf_ref[...], hb)
        o_ref[HI0:HG0, :] += _dot_tn(dhi_ref[...], hb)
        o_ref[HG0:N_IN, :] += _dot_tn(dhg_ref[...], hb)

    return pl.pallas_call(
        body, grid=(L // tl,), name="gw_in",
        in_specs=[_rows(tl, D), _rows(tl, D), _rows(tl, 1536), _rows(tl, 128), _rows(tl, D), _rows(tl, D),
                  _rows(tl, D), _rows(tl, D)],
        out_specs=_const((N_IN, D)), out_shape=SDS((N_IN, D), F32), compiler_params=_cparams())(
            h0, dz, dxbc, ddt, dhq, dhf, dhi, dhg)


def _matmul_tn(a, b, name):
    L, M = a.shape
    N = b.shape[1]
    tl = min(512, L)

    def body(a_ref, b_ref, o_ref):
        _zero_first(o_ref)
        o_ref[...] += _dot_tn(a_ref[...], b_ref[...])

    return pl.pallas_call(
        body, grid=(L // tl,), name=name, in_specs=[_rows(tl, M), _rows(tl, N)], out_specs=_const((M, N)),
        out_shape=SDS((M, N), F32), compiler_params=_cparams())(a, b)


def _matmul_tn_pair(a0, a1, b, name):
    L, M = a0.shape
    N = b.shape[1]
    tl = min(512, L)

    def body(a0_ref, a1_ref, b_ref, o_ref):
        _zero_first(o_ref)
        bv = b_ref[...].astype(BF)
        o_ref[:M, :] += _dot_tn(a0_ref[...], bv)
        o_ref[M:, :] += _dot_tn(a1_ref[...], bv)

    return pl.pallas_call(
        body, grid=(L // tl,), name=name, in_specs=[_rows(tl, M), _rows(tl, M), _rows(tl, N)],
        out_specs=_const((2 * M, N)), out_shape=SDS((2 * M, N), F32), compiler_params=_cparams())(a0, a1, b)


def _head_expand():
    e = (jnp.right_shift(_iota((128, D), 1), 6) == _iota((128, D), 0)).astype(F32)
    et = (jnp.right_shift(_iota((D, 128), 0), 6) == _iota((D, 128), 1)).astype(F32)
    return e, et


def _conv_shifts(cur, other, up):
    rows = _iota((Q, 1), 0)
    out = []
    for s in (1, 2, 3):
        if up:
            out.append(jnp.where(rows >= Q - s, pltpu.roll(other, Q - s, 0), pltpu.roll(cur, Q - s, 0)))
        else:
            out.append(jnp.where(rows < s, pltpu.roll(other, s, 0), pltpu.roll(cur, s, 0)))
    return out


def _ssd_pre(u, dtr, dtb, alog):
    e, et = _head_expand()
    sgu = _sigmoid(u)
    xc = u * sgu
    lane = _iota((1, 128), 1)
    hmask = (lane < NH_SSD).astype(F32)
    pre = dtr + dtb
    dt = (jnp.maximum(pre, 0.0) + jnp.log(1.0 + jnp.exp(-jnp.abs(pre)))) * hmask
    a_row = -jnp.exp(alog)
    causal = _iota((Q, Q), 1) <= _iota((Q, Q), 0)
    tri = causal.astype(F32)
    acum = _dot_sel(tri, dt * a_row)
    acum_full = _sel_dot(acum, e)
    alast_full = acum_full[Q - 1:Q, :]
    dt_full = _sel_dot(dt, e)
    xs = xc[:, :D]
    return dict(e=e, et=et, sgu=sgu, xs=xs, bm=xc[:, D:D + 256], cm=xc[:, D + 256:], hmask=hmask, pre=pre, dt=dt,
                a_row=a_row, causal=causal, tri=tri, acum=acum, acum_t=acum.T, eA_full=jnp.exp(acum_full),
                dte_full=jnp.exp(alast_full - acum_full), dt_full=dt_full, xdt=xs * dt_full)


def _ssd_decay(pre, hh, cb):
    seg = pre["acum"][:, hh:hh + 1] - pre["acum_t"][hh:hh + 1, :]
    lm = jnp.where(pre["causal"], jnp.exp(jnp.minimum(seg, 0.0)), 0.0)
    return lm, cb * lm


def _ssd_fwd(xbc, dtr, z, conv_w, conv_b, dtb, alog, dskip_full, nw):
    L = xbc.shape[0]
    nc = L // Q

    def body(xbc_ref, dtr_ref, z_ref, cw_ref, cb_ref, dtb_ref, alog_ref, dsk_ref, nw_ref,
             ya_ref, y_ref, u_ref, st_ref, prev_ref, s_ref):
        @pl.when(pl.program_id(0) == 0)
        def _():
            prev_ref[...] = jnp.zeros_like(prev_ref)
            s_ref[...] = jnp.zeros_like(s_ref)

        xr = xbc_ref[...].astype(F32)
        sh = _conv_shifts(xr, prev_ref[...], up=False)
        u = cb_ref[...] + cw_ref[3:4, :] * xr + cw_ref[2:3, :] * sh[0] + cw_ref[1:2, :] * sh[1] + cw_ref[0:1, :] * sh[2]
        prev_ref[...] = xr
        ub = u.astype(BF)
        u_ref[...] = ub
        pre = _ssd_pre(ub.astype(F32), dtr_ref[...], dtb_ref[...], alog_ref[...])
        lo = _iota((1, 128), 1) < SSD_P
        s_old = s_ref[...]
        st_ref[0] = s_old
        ys = []
        for g in range(2):
            bg, cg = pre["bm"][:, 128 * g:128 * g + 128], pre["cm"][:, 128 * g:128 * g + 128]
            cb = _dot_nt(cg, bg)
            gs = slice(512 * g, 512 * g + 512)
            yd = []
            for j in range(4 * g, 4 * g + 4):
                xp = pre["xdt"][:, 128 * j:128 * j + 128].astype(BF)
                _, m0 = _ssd_decay(pre, 2 * j, cb)
                _, m1 = _ssd_decay(pre, 2 * j + 1, cb)
                yd.append(jnp.where(lo, _dot(m0, xp), _dot(m1, xp)))
            yoff = _dot_nt(cg, s_old[gs, :]) * pre["eA_full"][:, gs]
            ys.append(jnp.concatenate(yd, axis=1) + yoff)
            st = _dot_tn((pre["xdt"] * pre["dte_full"])[:, gs], bg)
            cdcol = jnp.exp(_dot_sel(pre["et"][gs, :], pre["acum_t"])[:, Q - 1:Q])
            s_ref[gs, :] = s_old[gs, :] * cdcol + st
        y = jnp.concatenate(ys, axis=1) + dsk_ref[...] * pre["xs"]
        yb = y.astype(BF)
        y_ref[...] = yb
        zf = z_ref[...].astype(F32)
        yz = yb.astype(F32) * zf * _sigmoid(zf)
        outs = []
        for g in range(2):
            gs = slice(512 * g, 512 * g + 512)
            o, _, _ = _rms(yz[:, gs], nw_ref[:, gs])
            outs.append(o)
        ya_ref[...] = jnp.concatenate(outs, axis=1).astype(BF)

    outs = [SDS((L, D), BF), SDS((L, D), BF), SDS((L, 1536), BF), SDS((nc, D, 128), F32)]
    return pl.pallas_call(
        body, grid=(nc,), name="ssd_fwd",
        in_specs=[_rows(Q, 1536), _rows(Q, 128), _rows(Q, D), _const((4, 1536)), _const((1, 1536)), _const((1, 128)),
                  _const((1, 128)), _const((1, D)), _const((1, D))],
        out_specs=[_rows(Q, D), _rows(Q, D), _rows(Q, 1536), pl.BlockSpec((1, D, 128), lambda i: (i, 0, 0))],
        out_shape=outs, scratch_shapes=[pltpu.VMEM((Q, 1536), F32), pltpu.VMEM((D, 128), F32)],
        compiler_params=_cparams())(xbc, dtr, z, conv_w, conv_b, dtb, alog, dskip_full, nw)


def _ssd_bwd(dya, y, z, u, xbc, dtr, states, conv_w, dtb, alog, dskip_full, nw):
    L = dya.shape[0]
    nc = L // Q

    def body(dya_ref, y_ref, z_ref, u_ref, xc_ref, xp_ref, dtr_ref, st_ref, cw_ref, dtb_ref, alog_ref, dsk_ref, nw_ref,
             dz_ref, dxbc_ref, ddt_ref, gconv_ref, ghead_ref, glane_ref, gs_ref, ndu_ref):
        step = pl.program_id(0)

        @pl.when(step == 0)
        def _():
            for r in (gconv_ref, ghead_ref, glane_ref, gs_ref, ndu_ref):
                r[...] = jnp.zeros_like(r)

        uf = u_ref[...].astype(F32)
        pre = _ssd_pre(uf, dtr_ref[...], dtb_ref[...], alog_ref[...])
        e, et, xs, xdt = pre["e"], pre["et"], pre["xs"], pre["xdt"]
        lane = _iota((1, 128), 1)
        lo = lane < SSD_P
        sub = _iota((128, 1), 0)
        zf = z_ref[...].astype(F32)
        sgz = _sigmoid(zf)
        sz = zf * sgz
        yv = y_ref[...].astype(F32)
        yz = yv * sz
        dyav = dya_ref[...].astype(F32)
        dyz, dnw = [], []
        for g in range(2):
            gs = slice(512 * g, 512 * g + 512)
            _, n, r = _rms(yz[:, gs], nw_ref[:, gs])
            dv, dw = _rms_bwd(dyav[:, gs], n, r, nw_ref[:, gs])
            dyz.append(dv)
            dnw.append(dw)
        dyz = jnp.concatenate(dyz, axis=1)
        glane_ref[1:2, :] += _colsum(jnp.concatenate(dnw, axis=1))
        dy = dyz * sz
        dz_ref[...] = (dyz * yv * sgz * (1.0 + zf * (1.0 - sgz))).astype(BF)
        glane_ref[0:1, :] += _colsum(dy * xs)
        dxs = dsk_ref[...] * dy

        s_in = st_ref[0]
        gst = gs_ref[...]
        gy = dy * pre["eA_full"]
        xdte = xdt * pre["dte_full"]
        dacum = jnp.zeros((Q, 128), F32)
        dacum_t = jnp.zeros((128, Q), F32)
        dxdt, dacum_full, ddte_full, dbs, dcs = [], [], [], [], []
        for g in range(2):
            gs = slice(512 * g, 512 * g + 512)
            bg, cg = pre["bm"][:, 128 * g:128 * g + 128], pre["cm"][:, 128 * g:128 * g + 128]
            sg_, dg_ = s_in[gs, :], gst[gs, :]
            yoff = _dot_nt(cg, sg_) * pre["eA_full"][:, gs]
            dc = _dot(gy[:, gs], sg_)
            dsin = _dot_tn(gy[:, gs], cg)
            dacum_full.append(dy[:, gs] * yoff)
            tg = _dot_nt(bg, dg_)
            ddte_full.append(tg * xdt[:, gs])
            db = _dot(xdte[:, gs], dg_)
            cb = _dot_nt(cg, bg)
            dcb = jnp.zeros((Q, Q), F32)
            dxg = []
            for j in range(4 * g, 4 * g + 4):
                xp = xdt[:, 128 * j:128 * j + 128].astype(BF)
                dyp = dy[:, 128 * j:128 * j + 128]
                dxp = jnp.zeros((Q, 128), F32)
                for idx in range(2):
                    hh = 2 * j + idx
                    lm, m = _ssd_decay(pre, hh, cb)
                    dym = jnp.where(lo if idx == 0 else jnp.logical_not(lo), dyp, 0.0).astype(BF)
                    dm = jnp.where(pre["causal"], _dot_nt(dym, xp), 0.0)
                    w = dm * m
                    dacum = dacum + jnp.where(lane == hh, jnp.sum(w, axis=1, keepdims=True), 0.0)
                    dacum_t = dacum_t + jnp.where(sub == hh, jnp.sum(w, axis=0, keepdims=True), 0.0)
                    dcb = dcb + dm * lm
                    dxp = dxp + _dot_tn(m, dym)
                dxg.append(dxp)
            dxdt.append(jnp.concatenate(dxg, axis=1) + tg * pre["dte_full"][:, gs])
            dcs.append(dc + _dot(dcb, bg))
            dbs.append(db + _dot_tn(dcb, cg))
            cdcol = jnp.exp(_dot_sel(et[gs, :], pre["acum_t"])[:, Q - 1:Q])
            gs_ref[gs, :] = dsin + dg_ * cdcol
        dxdt = jnp.concatenate(dxdt, axis=1)
        dacum = dacum + _sel_dot(jnp.concatenate(dacum_full, axis=1), et, 2) - dacum_t.T
        alast = pre["acum"][Q - 1:Q, :]
        dte = jnp.exp(alast - pre["acum"])
        ddte = _sel_dot(jnp.concatenate(ddte_full, axis=1), et, 2) * dte
        dacum = dacum - ddte
        dcd_col = jnp.sum(_dot_sel(e, gst * s_in, 2), axis=1, keepdims=True)
        dcd_row = jnp.broadcast_to(dcd_col, (128, 128)).T[0:1, :]
        dalast = _colsum(ddte) + dcd_row * jnp.exp(alast)
        dacum = dacum + jnp.where(_iota((Q, 1), 0) == Q - 1, dalast, 0.0)
        ddt = _sel_dot(dxdt * xs, et, 2)
        dxs = dxs + dxdt * pre["dt_full"]
        dda = _dot_sel(pre["tri"].T, dacum)
        ddt = ddt + dda * pre["a_row"]
        ghead_ref[1:2, :] += _colsum(dda * pre["dt"])
        ddtr = ddt * _sigmoid(pre["pre"]) * pre["hmask"]
        ghead_ref[0:1, :] += _colsum(ddtr)
        ddt_ref[...] = ddtr

        dxc = jnp.concatenate([dxs] + dbs + dcs, axis=1)
        sgu = pre["sgu"]
        du = dxc * sgu * (1.0 + uf * (1.0 - sgu))
        shu = _conv_shifts(du, ndu_ref[...], up=True)
        dxr = cw_ref[3:4, :] * du + cw_ref[2:3, :] * shu[0] + cw_ref[1:2, :] * shu[1] + cw_ref[0:1, :] * shu[2]
        ndu_ref[...] = du
        dxbc_ref[...] = dxr.astype(BF)
        xr = xc_ref[...].astype(F32)
        xprev = jnp.where(step == nc - 1, 0.0, xp_ref[...].astype(F32))
        shx = _conv_shifts(xr, xprev, up=False)
        gconv_ref[3:4, :] += _colsum(du * xr)
        gconv_ref[2:3, :] += _colsum(du * shx[0])
        gconv_ref[1:2, :] += _colsum(du * shx[1])
        gconv_ref[0:1, :] += _colsum(du * shx[2])
        gconv_ref[4:5, :] += _colsum(du)

        @pl.when(step == nc - 1)
        def _():
            ghead_ref[2:3, :] = ghead_ref[1:2, :] * pre["a_row"]
            ghead_ref[3:4, :] = _dot_hi(glane_ref[...], et)[0:1, :]

    rev = lambda i: (nc - 1 - i, 0)
    outs = [SDS((L, D), BF), SDS((L, 1536), BF), SDS((L, 128), F32), SDS((8, 1536), F32), SDS((8, 128), F32),
            SDS((8, D), F32)]
    return pl.pallas_call(
        body, grid=(nc,), name="ssd_bwd",
        in_specs=[pl.BlockSpec((Q, D), rev), pl.BlockSpec((Q, D), rev), pl.BlockSpec((Q, D), rev),
                  pl.BlockSpec((Q, 1536), rev), pl.BlockSpec((Q, 1536), rev),
                  pl.BlockSpec((Q, 1536), lambda i: (jnp.maximum(nc - 2 - i, 0), 0)),
                  pl.BlockSpec((Q, 128), rev), pl.BlockSpec((1, D, 128), lambda i: (nc - 1 - i, 0, 0)),
                  _const((4, 1536)), _const((1, 128)), _const((1, 128)), _const((1, D)), _const((1, D))],
        out_specs=[pl.BlockSpec((Q, D), rev), pl.BlockSpec((Q, 1536), rev), pl.BlockSpec((Q, 128), rev),
                   _const((8, 1536)), _const((8, 128)), _const((8, D))],
        out_shape=outs, scratch_shapes=[pltpu.VMEM((D, 128), F32), pltpu.VMEM((Q, 1536), F32)],
        compiler_params=_cparams())(dya, y, z, u, xbc, xbc, dtr, states, conv_w, dtb, alog, dskip_full, nw)


def _hg_gates(hq, hf, hlb):
    h0, h1 = hlb[0:1, :], hlb[1:2, :]
    mx = jnp.maximum(h0, h1)
    e0, e1 = jnp.exp(h0 - mx), jnp.exp(h1 - mx)
    lb = e0 / (e0 + e1)
    sg = _sigmoid(hf)
    fg = lb + (1.0 - lb) * sg
    tri = (_iota((Q, Q), 1) <= _iota((Q, Q), 0)).astype(F32)
    return hq * _sigmoid(hq), 1.0 - fg, fg, sg, lb, e1 / (e0 + e1), _dot_sel(tri, jnp.log(fg))


def _hg_intra(b, q, k):
    rowblk = jnp.right_shift(_iota((Q, 1), 0), 4)
    mids = [b[SUB * i + SUB // 2:SUB * i + SUB // 2 + 1, :] for i in range(NSUB)]
    prevs = [mids[0]] + [b[SUB * i - 1:SUB * i, :] for i in range(1, NSUB)]
    mfull = jnp.concatenate([jnp.broadcast_to(r, (SUB, 128)) for r in mids], axis=0)
    rfull = jnp.concatenate([jnp.broadcast_to(r, (SUB, 128)) for r in prevs], axis=0)
    eqd, ek, eqo = jnp.exp(b - mfull), jnp.exp(mfull - b), jnp.exp(b - rfull)
    qd, qo, khat = q * eqd, q * eqo, k * ek
    rtab = jnp.concatenate(prevs, axis=0)
    djs = [jnp.exp(rtab - mids[j]) for j in range(NSUB)]
    zero = jnp.zeros((SUB, 128), F32)
    cols = []
    for j in range(NSUB):
        pieces = []
        for i in range(NSUB):
            rs = slice(SUB * i, SUB * i + SUB)
            pieces.append(zero if i < j else qd[rs] if i == j else qo[rs] * djs[j][i:i + 1, :])
        cols.append(jnp.concatenate(pieces, axis=0))
    qt = jnp.concatenate(cols, axis=1).astype(BF)
    kt = jnp.concatenate([jnp.where(rowblk == j, khat, 0.0) for j in range(NSUB)], axis=1).astype(BF)
    causal = _iota((Q, Q), 1) <= _iota((Q, Q), 0)
    att = jnp.where(causal, _dot_nt(qt, kt), 0.0)
    return att, qt, kt, (eqd, ek, eqo, djs), causal


def _hg_intra_bwd(dqt, dkt, qt, kt, factors):
    eqd, ek, eqo, djs = factors
    dqd, dqo, dkh, db = [], [], [], []
    for i in range(NSUB):
        rs = slice(SUB * i, SUB * i + SUB)
        diag = slice(128 * i, 128 * i + 128)
        dqd.append(dqt[rs, diag])
        dkh.append(dkt[rs, diag])
        dbi = qt[rs, diag].astype(F32) * dqt[rs, diag] - kt[rs, diag].astype(F32) * dkt[rs, diag]
        acc = jnp.zeros((SUB, 128), F32)
        for j in range(i):
            bl = slice(128 * j, 128 * j + 128)
            acc = acc + dqt[rs, bl] * djs[j][i:i + 1, :]
            dbi = dbi + qt[rs, bl].astype(F32) * dqt[rs, bl]
        dqo.append(acc)
        db.append(dbi)
    cat = lambda t: jnp.concatenate(t, axis=0)
    return cat(dqd) * eqd + cat(dqo) * eqo, cat(dkh) * ek, cat(db)


def _hg_att_exact(b, q, k, b_ref, q_ref, att_t_ref):
    b_ref[...] = b
    q_ref[...] = q
    att_t_ref[...] = jnp.zeros((Q, Q), F32)
    rows, lane = _iota((Q, 1), 0), _iota((1, Q), 1)

    def step(i, carry):
        e = jnp.exp(jnp.minimum(b_ref[pl.ds(i, 1), :] - b, 0.0))
        col = jnp.sum(q_ref[pl.ds(i, 1), :] * k * e, axis=1, keepdims=True)
        att_t_ref[...] = jnp.where(lane == i, jnp.where(rows <= i, col, 0.0), att_t_ref[...])
        return carry

    lax.fori_loop(0, Q, step, 0)
    return att_t_ref[...].T


def _hg_att_exact_bwd(da, b, q, k, b_ref, q_ref, da_t_ref, dq_ref, dk_ref):
    b_ref[...] = b
    q_ref[...] = q
    da_t_ref[...] = da.T
    dk_ref[...] = jnp.zeros((Q, 128), F32)
    lane = _iota((1, Q), 1)

    def step(i, carry):
        e = jnp.exp(jnp.minimum(b_ref[pl.ds(i, 1), :] - b, 0.0))
        g = jnp.sum(jnp.where(lane == i, da_t_ref[...], 0.0), axis=1, keepdims=True) * e
        dq_ref[pl.ds(i, 1), :] = jnp.sum(g * k, axis=0, keepdims=True)
        dk_ref[...] += g * q_ref[pl.ds(i, 1), :]
        return carry

    lax.fori_loop(0, Q, step, 0)
    dq, dk = dq_ref[...], dk_ref[...]
    return dq, dk, q * dq - k * dk


def _hg_fwd(hq, hf, hi, hg, hlb, nw, fast):
    L = hq.shape[0]
    nc = L // Q

    def run(exact, step, hq_ref, hf_ref, hi_ref, hg_ref, hlb_ref, nw_ref, ob_ref, o_ref, st_ref, s_ref, *tmp):
        @pl.when(step == 0)
        def _():
            s_ref[...] = jnp.zeros_like(s_ref)

        qf, kf, _, _, _, _, bcum = _hg_gates(hq_ref[...].astype(F32), hf_ref[...], hlb_ref[...])
        gate = hg_ref[...].astype(F32)
        heads = [slice(128 * h, 128 * h + 128) for h in range(NH_HG)]
        if exact:
            atts = [_hg_att_exact(bcum[:, sl], qf[:, sl], kf[:, sl], *tmp).astype(BF) for sl in heads]
        else:
            atts = [_hg_intra(bcum[:, sl], qf[:, sl], kf[:, sl])[0].astype(BF) for sl in heads]
        olds = [s_ref[sl, :] for sl in heads]
        outs_ = [_dot(att, hi_ref[:, sl]) + _dot(qf[:, sl] * jnp.exp(bcum[:, sl]), s)
                 for att, sl, s in zip(atts, heads, olds)]
        for sl, s, o in zip(heads, olds, outs_):
            b, k = bcum[:, sl], kf[:, sl]
            st_ref[0, sl, :] = s
            blast = b[Q - 1:Q, :]
            s_ref[sl, :] = s * jnp.exp(b.T[:, Q - 1:Q]) + _dot_tn(k * jnp.exp(blast - b), hi_ref[:, sl])
            ob = o.astype(BF)
            o_ref[:, sl] = ob
            on, _, _ = _rms(ob.astype(F32), nw_ref[...])
            gt = gate[:, sl]
            ob_ref[:, sl] = (on * gt * _sigmoid(gt)).astype(BF)

    def body(fast_ref, *refs):
        step = pl.program_id(0)
        pl.when(fast_ref[0] == 1)(lambda: run(False, step, *refs))
        pl.when(fast_ref[0] != 1)(lambda: run(True, step, *refs))

    rows = pl.BlockSpec((Q, D), lambda i, f: (i, 0))
    outs = [SDS((L, D), BF), SDS((L, D), BF), SDS((nc, D, 128), F32)]
    grid_spec = pltpu.PrefetchScalarGridSpec(
        num_scalar_prefetch=1, grid=(nc,),
        in_specs=[rows] * 4 + [pl.BlockSpec((2, D), lambda i, f: (0, 0)), pl.BlockSpec((1, 128), lambda i, f: (0, 0))],
        out_specs=[rows, rows, pl.BlockSpec((1, D, 128), lambda i, f: (i, 0, 0))],
        scratch_shapes=[pltpu.VMEM((D, 128), F32), pltpu.VMEM((Q, 128), F32), pltpu.VMEM((Q, 128), F32),
                        pltpu.VMEM((Q, Q), F32)])
    return pl.pallas_call(body, grid_spec=grid_spec, name="hg_fwd", out_shape=outs,
                          compiler_params=_cparams())(fast, hq, hf, hi, hg, hlb, nw)


def _hg_bwd(dob, o, hq, hf, hi, hg, states, hlb, nw, fast):
    L = dob.shape[0]
    nc = L // Q

    def run(exact, step, dob_ref, o_ref, hq_ref, hf_ref, hi_ref, hg_ref, st_ref, hlb_ref, nw_ref,
            dhq_ref, dhf_ref, dhi_ref, dhg_ref, acc_ref, gs_ref, *tmp):
        @pl.when(step == 0)
        def _():
            acc_ref[...] = jnp.zeros_like(acc_ref)
            gs_ref[...] = jnp.zeros_like(gs_ref)

        hqv = hq_ref[...].astype(F32)
        qf, kf, fg, sg, lb, sm1, bcum = _hg_gates(hqv, hf_ref[...], hlb_ref[...])
        gate = hg_ref[...].astype(F32)
        sgg = _sigmoid(gate)
        nwv = nw_ref[...]
        tri_t = (_iota((Q, Q), 1) >= _iota((Q, Q), 0)).astype(F32)
        ones8 = jnp.ones((8, 128), BF)
        heads = [slice(128 * h, 128 * h + 128) for h in range(NH_HG)]
        row_last = _iota((Q, 1), 0) == Q - 1
        dobs, dnws = [], []
        for sl in heads:
            gt, sgt = gate[:, sl], sgg[:, sl]
            _, n, r = _rms(o_ref[:, sl].astype(F32), nwv)
            dobv = dob_ref[:, sl].astype(F32)
            dhg_ref[:, sl] = (dobv * n * nwv * sgt * (1.0 + gt * (1.0 - sgt))).astype(BF)
            do, dw = _rms_bwd(dobv * gt * sgt, n, r, nwv)
            dnws.append(_colsum(dw))
            dobs.append(do.astype(BF))
        causal = _iota((Q, Q), 1) <= _iota((Q, Q), 0)
        if exact:
            intra = [(_hg_att_exact(bcum[:, sl], qf[:, sl], kf[:, sl], *tmp[:3]),) for sl in heads]
        else:
            intra = [_hg_intra(bcum[:, sl], qf[:, sl], kf[:, sl]) for sl in heads]
        states = [(st_ref[0, sl, :], gs_ref[sl, :]) for sl in heads]
        das = [jnp.where(causal, _dot_nt(dob_h, hi_ref[:, sl]), 0.0) for dob_h, sl in zip(dobs, heads)]
        dqhats = [_dot_nt(dob_h, s) for dob_h, (s, _) in zip(dobs, states)]
        dkhats = [_dot_nt(hi_ref[:, sl], gst) for sl, (_, gst) in zip(heads, states)]
        if not exact:
            dqts = [jnp.dot(da.astype(BF), it[2], preferred_element_type=F32) for da, it in zip(das, intra)]
            dkts = [lax.dot_general(da.astype(BF), it[1], (((0,), (0,)), ((), ())), preferred_element_type=F32)
                    for da, it in zip(das, intra)]
        dqs, dks, dgls = [], [], []
        for h, sl in enumerate(heads):
            b, q, k = bcum[:, sl], qf[:, sl], kf[:, sl]
            att = intra[h][0]
            s, gst = states[h]
            dob_h, dqhat, dkhat = dobs[h], dqhats[h], dkhats[h]
            eb = jnp.exp(b)
            blast = b[Q - 1:Q, :]
            ekl = jnp.exp(blast - b)
            qhat, khat = q * eb, k * ekl
            dhi_ref[:, sl] = (_dot_tn(att, dob_h) + _dot(khat, gst)).astype(BF)
            if exact:
                dq_i, dk_i, db = _hg_att_exact_bwd(das[h], b, q, k, *tmp)
            else:
                dq_i, dk_i, db = _hg_intra_bwd(dqts[h], dkts[h], *intra[h][1:4])
            dqs.append(dq_i + dqhat * eb)
            dks.append(dk_i + dkhat * ekl)
            qhat_r, khat_r = qhat.astype(BF).astype(F32), khat.astype(BF).astype(F32)
            decay_row = sum(_dot_nt(ones8, part) for part in _split(gst * s, 2))[0:1, :]
            dblast = _colsum(dkhat * khat_r) + decay_row * jnp.exp(blast)
            dgls.append(db + qhat_r * dqhat - khat_r * dkhat + jnp.where(row_last, dblast, 0.0))
            gs_ref[sl, :] = _dot_tn(qhat, dob_h) + gst * jnp.exp(b.T[:, Q - 1:Q])
        dq, dk, db = (jnp.concatenate(t, axis=1) for t in (dqs, dks, dgls))
        dgl = _dot_sel(tri_t, db, 2)
        sgq = _sigmoid(hqv)
        dhq_ref[...] = (dq * sgq * (1.0 + hqv * (1.0 - sgq))).astype(BF)
        dfg = dgl / fg - dk
        dhf_ref[...] = (dfg * (1.0 - lb) * sg * (1.0 - sg)).astype(BF)
        acc_ref[0:1, :] += _colsum(dfg * (1.0 - sg))
        acc_ref[1:2, :] += jnp.concatenate(dnws, axis=1)

        @pl.when(step == nc - 1)
        def _():
            dlb = acc_ref[0:1, :] * lb * sm1
            acc_ref[2:3, :] = dlb
            acc_ref[3:4, :] = -dlb
            tot = acc_ref[1:2, 0:128]
            for h in range(1, NH_HG):
                tot = tot + acc_ref[1:2, 128 * h:128 * h + 128]
            acc_ref[4:5, 0:128] = tot

    def body(fast_ref, *refs):
        step = pl.program_id(0)
        pl.when(fast_ref[0] == 1)(lambda: run(False, step, *refs))
        pl.when(fast_ref[0] != 1)(lambda: run(True, step, *refs))

    rev = pl.BlockSpec((Q, D), lambda i, f: (nc - 1 - i, 0))
    outs = [SDS((L, D), BF)] * 4 + [SDS((8, D), F32)]
    grid_spec = pltpu.PrefetchScalarGridSpec(
        num_scalar_prefetch=1, grid=(nc,),
        in_specs=[rev] * 6 + [pl.BlockSpec((1, D, 128), lambda i, f: (nc - 1 - i, 0, 0)),
                              pl.BlockSpec((2, D), lambda i, f: (0, 0)), pl.BlockSpec((1, 128), lambda i, f: (0, 0))],
        out_specs=[rev] * 4 + [pl.BlockSpec((8, D), lambda i, f: (0, 0))],
        scratch_shapes=[pltpu.VMEM((D, 128), F32), pltpu.VMEM((Q, 128), F32), pltpu.VMEM((Q, 128), F32),
                        pltpu.VMEM((Q, Q), F32), pltpu.VMEM((Q, 128), F32), pltpu.VMEM((Q, 128), F32)])
    return pl.pallas_call(body, grid_spec=grid_spec, name="hg_bwd", out_shape=outs,
                          compiler_params=_cparams())(fast, dob, o, hq, hf, hi, hg, states, hlb, nw)


def _pad_lanes(v, n=128):
    return jnp.pad(v, ((0, 0), (0, n - v.shape[1])))


def _local_step(x, mem, tgt, wg, ws):
    w_in_t = wg["w_in"].reshape(N_IN, D)
    wg_t, wu_t = wg["ffn_w_gate"].reshape(FFN, D), wg["ffn_w_up"].reshape(FFN, D)
    wd = wg["ffn_w_down"].reshape(FFN, D)
    w_out = wg["w_out"].reshape(2 * D, D)
    wq, wo = wg["xa_wq"].reshape(D, D), wg["xa_wo"].reshape(D, D)
    dtb, alog = _pad_lanes(ws["dt_bias"]), _pad_lanes(ws["a_log"])
    dskip_full = jnp.repeat(ws["d_skip"], SSD_P, axis=1)
    conv_w, conv_b = ws["conv_w"][0], ws["conv_b"]

    h0, z, xbc, hq, hf, hi, hg, dtr = _in_proj(x, ws["norm_mix_w"], w_in_t)
    ya, yssd, u, st_ssd = _ssd_fwd(xbc, dtr, z, conv_w, conv_b, dtb, alog, dskip_full, ws["ssd_norm_w"])
    hlb = ws["hg_lower_bounds"]
    hg_fast = (jnp.min(jax.nn.softmax(hlb, axis=0)[0]) >= HG_LB_FLOOR).astype(jnp.int32).reshape(1)
    ob, ohg, st_hg = _hg_fwd(hq, hf, hi, hg, hlb, ws["hg_norm_w"], hg_fast)
    kmem, vmem = _mem_kv(mem, ws["norm_mem_w"], wg["xa_wkv"])
    x1, x2, hxa, q, ox = _attn_fwd(x, ya, ob, w_out, ws["norm_xa_w"], wq, kmem, vmem, wo)
    nfin = ws["norm_final_w"].reshape(1, D)
    dx2, hffn, act, dx3, dg, du, acc_f = _ffn_loss(x2, tgt, ws["norm_ffn_w"], nfin, wg_t, wu_t, wd)
    dx1, dya, dob, dq, dk, dv, acc_a = _attn_bwd(dx2, x1, q, kmem, vmem, ws["norm_xa_w"], wq, wo, w_out)
    g_nmem, g_wkv = _mem_kv_bwd(mem, ws["norm_mem_w"], wg["xa_wkv"], dk, dv)
    dhq, dhf, dhi, dhg, acc_h = _hg_bwd(dob, ohg, hq, hf, hi, hg, st_hg, hlb, ws["hg_norm_w"], hg_fast)
    dz, dxbc, ddt, gconv, ghead, glane = _ssd_bwd(dya, yssd, z, u, xbc, dtr, st_ssd, conv_w, dtb, alog, dskip_full,
                                                  ws["ssd_norm_w"])
    gx, acc_i = _in_proj_bwd(x, dx1, dz, dxbc, dhq, dhf, dhi, dhg, ddt, ws["norm_mix_w"], w_in_t)

    gb = {
        "w_in": _gw_in(h0, dz, dxbc, ddt, dhq, dhf, dhi, dhg).reshape(4, N_IN // 4, D),
        "w_out": _matmul_tn_pair(ya, ob, dx1, "gw_out").reshape(4, D // 2, D),
        "xa_wq": _matmul_tn(hxa, dq, "gw_q").reshape(4, D // 4, D),
        "xa_wkv": g_wkv,
        "xa_wo": _matmul_tn(ox, dx2, "gw_o").reshape(4, D // 4, D),
        "ffn_w_gate": _matmul_tn(dg, hffn, "gw_gate").reshape(4, FFN // 4, D),
        "ffn_w_up": _matmul_tn(du, hffn, "gw_up").reshape(4, FFN // 4, D),
        "ffn_w_down": _matmul_tn(act, dx3, "gw_down").reshape(4, FFN // 4, D),
    }
    gs = {
        "norm_mix_w": acc_i[0:1], "conv_w": gconv[0:4][None], "conv_b": gconv[4:5],
        "dt_bias": ghead[0:1, :NH_SSD], "a_log": ghead[2:3, :NH_SSD], "d_skip": ghead[3:4, :NH_SSD],
        "ssd_norm_w": glane[1:2], "hg_lower_bounds": acc_h[2:4], "hg_norm_w": acc_h[4:5, :128],
        "norm_xa_w": acc_a[0:1], "norm_mem_w": g_nmem, "norm_ffn_w": acc_f[2:3], "norm_final_w": acc_f[1],
    }
    loss = (0.5 / D) * jnp.sum(acc_f[0])
    return loss, gx, gb, gs


def _place():
    return lax.axis_index("x"), lax.axis_index("y"), lax.axis_index("c")


def _allgather(arrays, halves, name):
    n = len(arrays)

    def body(*refs):
        ins, outs = refs[:n], refs[n:2 * n]
        send_sems, recv_sems, local_sems = refs[2 * n:]
        x, y, c = _place()
        me, sibling = (x, y, c), (x, y, 1 - c)
        chips = [(1 - x, y), (x, 1 - y), (1 - x, 1 - y)]
        waits_recv, waits_send, locals_ = [], [], []
        for a in range(n):
            hc = halves[a]

            def slot(p, ref=outs[a], hc=hc):
                if hc is None:
                    return ref.at[4 * p[0] + 2 * p[1] + p[2]]
                return ref.at[2 * p[0] + p[1], :, pl.ds(p[2] * hc, hc)]

            own = ins[a] if hc is None else ins[a].at[:, pl.ds(c * hc, hc)]

            def copy(k, piece, to, src=None, a=a, slot=slot):
                return pltpu.make_async_remote_copy(
                    src_ref=slot(piece) if src is None else src, dst_ref=slot(piece),
                    send_sem=send_sems.at[7 * a + k], recv_sem=recv_sems.at[7 * a + k],
                    device_id=to, device_id_type=MESH)

            mine = pltpu.make_async_copy(own, slot(me), local_sems.at[a])
            mine.start()
            locals_.append(mine)
            first = [copy(0, me, sibling, src=own)]
            first += [copy(1 + j, me, (*chip, c), src=own) for j, chip in enumerate(chips)]
            for cp in first:
                cp.start()
            passed = [copy(4 + j, (*chip, c), sibling) for j, chip in enumerate(chips)]
            for j, chip in enumerate(chips):
                copy(1 + j, (*chip, c), me).wait_recv()
                passed[j].start()
            waits_recv.append(copy(0, sibling, me))
            waits_recv += [copy(4 + j, (*chip, 1 - c), me) for j, chip in enumerate(chips)]
            waits_send += first + passed
        for cp in waits_recv:
            cp.wait_recv()
        for cp in waits_send:
            cp.wait_send()
        for cp in locals_:
            cp.wait()

    out_shape = [SDS((8,) + a.shape if hc is None else (4,) + a.shape, a.dtype) for a, hc in zip(arrays, halves)]
    return pl.pallas_call(
        body, name=name, in_specs=[ANY] * n, out_specs=[ANY] * n, out_shape=out_shape,
        scratch_shapes=[pltpu.SemaphoreType.DMA((7 * n,)), pltpu.SemaphoreType.DMA((7 * n,)),
                        pltpu.SemaphoreType.DMA((n,))])(*arrays)


def _send_sibling_halves(grads, halves, name):
    n = len(grads)

    def body(*refs):
        ins, outs = refs[:n], refs[n:2 * n]
        send_sems, recv_sems = refs[2 * n:]
        x, y, c = _place()
        cps = [pltpu.make_async_remote_copy(
            src_ref=ins[a].at[:, :, pl.ds((1 - c) * halves[a], halves[a])], dst_ref=outs[a],
            send_sem=send_sems.at[a], recv_sem=recv_sems.at[a], device_id=(x, y, 1 - c), device_id_type=MESH)
            for a in range(n)]
        for cp in cps:
            cp.start()
        for cp in cps:
            cp.wait()

    return pl.pallas_call(
        body, name=name, in_specs=[ANY] * n, out_specs=[ANY] * n,
        out_shape=[SDS((4, g.shape[1], hc), g.dtype) for g, hc in zip(grads, halves)],
        scratch_shapes=[pltpu.SemaphoreType.DMA((n,)), pltpu.SemaphoreType.DMA((n,))])(*grads)


def _send_chips(parts, name):
    n = len(parts)

    def body(*refs):
        ins, outs = refs[:n], refs[n:2 * n]
        send_sems, recv_sems = refs[2 * n:]
        x, y, c = _place()
        chips = [(1 - x, y), (x, 1 - y), (1 - x, 1 - y)]
        cps = [pltpu.make_async_remote_copy(
            src_ref=ins[a].at[2 * px + py], dst_ref=outs[a].at[k], send_sem=send_sems.at[3 * a + k],
            recv_sem=recv_sems.at[3 * a + k], device_id=(px, py, c), device_id_type=MESH)
            for a in range(n) for k, (px, py) in enumerate(chips)]
        for cp in cps:
            cp.start()
        for cp in cps:
            cp.wait()

    return pl.pallas_call(
        body, name=name, in_specs=[ANY] * n, out_specs=[ANY] * n,
        out_shape=[SDS((3,) + p.shape[1:], p.dtype) for p in parts],
        scratch_shapes=[pltpu.SemaphoreType.DMA((3 * n,)), pltpu.SemaphoreType.DMA((3 * n,))])(*parts)


def _swap_halves(bufs, halves, name):
    n = len(bufs)

    def body(*refs):
        outs = refs[n:2 * n]
        send_sems, recv_sems = refs[2 * n:]
        x, y, c = _place()

        def cols(a, which):
            return outs[a].at[:, pl.ds(which * halves[a], halves[a])]

        cps = [pltpu.make_async_remote_copy(
            src_ref=cols(a, c), dst_ref=cols(a, c), send_sem=send_sems.at[a], recv_sem=recv_sems.at[a],
            device_id=(x, y, 1 - c), device_id_type=MESH) for a in range(n)]
        for cp in cps:
            cp.start()
        for a in range(n):
            pltpu.make_async_remote_copy(
                src_ref=cols(a, c), dst_ref=cols(a, 1 - c), send_sem=send_sems.at[a], recv_sem=recv_sems.at[a],
                device_id=(x, y, 1 - c), device_id_type=MESH).wait_recv()
        for cp in cps:
            cp.wait_send()

    return pl.pallas_call(
        body, name=name, in_specs=[ANY] * n, out_specs=[ANY] * n,
        out_shape=[SDS(b.shape, b.dtype) for b in bufs], input_output_aliases={a: a for a in range(n)},
        scratch_shapes=[pltpu.SemaphoreType.DMA((n,)), pltpu.SemaphoreType.DMA((n,))])(*bufs)


def _tile(rows, cols, nbuf):
    budget = (VMEM_LIMIT // 3) // (2 * nbuf * 4)
    if rows % 8 == 0:
        cands = [t for t in range(8, rows + 1, 8) if rows % t == 0 and t * cols <= budget]
        pref = [t for t in cands if t % 16 == 0]
        return (max(pref) if pref else max(cands) if cands else 8), cols
    cands = [t for t in range(128, cols + 1, 128) if cols % t == 0 and rows * t <= budget]
    return rows, (max(cands) if cands else 128)


def _chip_sum(g, from_sib, place, name):
    _, rows, hc = from_sib.shape
    tr, tc = _tile(rows, hc, 4)
    ni, nj = rows // tr, hc // tc

    def body(p_ref, g_ref, s_ref, hb_ref, own_ref):
        s = g_ref[...] + s_ref[...]
        hb_ref[...] = s.astype(BF)

        @pl.when(pl.program_id(2) == p_ref[1])
        def _():
            own_ref[...] = s

    grid_spec = pltpu.PrefetchScalarGridSpec(
        num_scalar_prefetch=1, grid=(ni, nj, 4),
        in_specs=[pl.BlockSpec((None, tr, tc), lambda i, j, k, p: (k, i, p[0] * nj + j)),
                  pl.BlockSpec((None, tr, tc), lambda i, j, k, p: (k, i, j))],
        out_specs=[pl.BlockSpec((None, tr, tc), lambda i, j, k, p: (k, i, j)),
                   pl.BlockSpec((tr, tc), lambda i, j, k, p: (i, j))])
    return pl.pallas_call(
        body, grid_spec=grid_spec, name=name, out_shape=[SDS((4, rows, hc), BF), SDS((rows, hc), F32)],
        compiler_params=pltpu.CompilerParams(dimension_semantics=("arbitrary",) * 3,
                                             vmem_limit_bytes=VMEM_LIMIT))(place, g, from_sib)


def _total(own, parts, place, name):
    rows, hc = own.shape
    tr, tc = _tile(rows, hc, 5)
    ni, nj = rows // tr, hc // tc

    def body(p_ref, own_ref, parts_ref, o_ref):
        s = own_ref[...]
        for k in range(3):
            s = s + parts_ref[k].astype(F32)
        o_ref[...] = s

    grid_spec = pltpu.PrefetchScalarGridSpec(
        num_scalar_prefetch=1, grid=(ni, nj),
        in_specs=[pl.BlockSpec((tr, tc), lambda i, j, p: (i, j)),
                  pl.BlockSpec((3, tr, tc), lambda i, j, p: (0, i, j))],
        out_specs=pl.BlockSpec((tr, tc), lambda i, j, p: (i, p[0] * nj + j)))
    return pl.pallas_call(
        body, grid_spec=grid_spec, name=name, out_shape=SDS((rows, 2 * hc), F32),
        compiler_params=pltpu.CompilerParams(dimension_semantics=("arbitrary",) * 2,
                                             vmem_limit_bytes=VMEM_LIMIT))(place, own, parts)


def _sum8(parts, name):
    R = parts.shape[1]

    def body(p_ref, o_ref):
        s = p_ref[0]
        for k in range(1, 8):
            s = s + p_ref[k]
        o_ref[...] = s

    return pl.pallas_call(
        body, grid=(1,), name=name, in_specs=[_const((8, R, 128))], out_specs=_const((R, 128)),
        out_shape=SDS((R, 128), F32), compiler_params=_cparams())(parts)


def _adamw(w, g, m, v, name):
    _, R, C = w.shape
    tr, tc = _tile(R, C, 7)
    c1 = 1.0 / (1.0 - ADAM_B1 ** ADAM_STEP)
    c2 = 1.0 / (1.0 - ADAM_B2 ** ADAM_STEP)

    def body(w_ref, g_ref, m_ref, v_ref, d_ref, nm_ref, nv_ref):
        gv = g_ref[...]
        nm = ADAM_B1 * m_ref[...] + (1.0 - ADAM_B1) * gv
        nv = ADAM_B2 * v_ref[...] + (1.0 - ADAM_B2) * gv * gv
        nm_ref[...] = nm
        nv_ref[...] = nv
        d_ref[...] = -ADAM_LR * ((nm * c1) / (jnp.sqrt(nv * c2) + ADAM_EPS) + ADAM_WD * w_ref[...])

    blk3 = pl.BlockSpec((None, tr, tc), lambda i, j: (0, i, j))
    return pl.pallas_call(
        body, grid=(R // tr, C // tc), name=name,
        in_specs=[blk3, pl.BlockSpec((tr, tc), lambda i, j: (i, j)), blk3, blk3], out_specs=[blk3] * 3,
        out_shape=[SDS((1, R, C), F32)] * 3,
        compiler_params=pltpu.CompilerParams(dimension_semantics=("arbitrary",) * 2,
                                             vmem_limit_bytes=VMEM_LIMIT))(w, g, m, v)


def _pack_small(parts):
    rows = []
    for p in parts:
        p = p.reshape(-1)
        rows.append(jnp.pad(p, (0, (-p.shape[0]) % 128)).reshape(-1, 128))
    out = jnp.concatenate(rows, axis=0)
    return jnp.pad(out, ((0, (-out.shape[0]) % 8), (0, 0)))


def _unpack_small(packed, shapes):
    out, row = [], 0
    for shp in shapes:
        n = 1
        for s in shp:
            n *= s
        nr = -(-n // 128)
        out.append(packed[row:row + nr].reshape(-1)[:n].reshape(shp))
        row += nr
    return out


def kernel(x, mem, norm_mix_w, w_in, conv_w, conv_b, dt_bias, a_log, d_skip, ssd_norm_w, hg_lower_bounds, hg_norm_w, w_out, norm_xa_w, norm_mem_w, xa_wq, xa_wkv, xa_wo, norm_ffn_w, ffn_w_gate, ffn_w_up, ffn_w_down, norm_final_w, loss_target, m_norm_mix_w, m_w_in, m_conv_w, m_conv_b, m_dt_bias, m_a_log, m_d_skip, m_ssd_norm_w, m_hg_lower_bounds, m_hg_norm_w, m_w_out, m_norm_xa_w, m_norm_mem_w, m_xa_wq, m_xa_wkv, m_xa_wo, m_norm_ffn_w, m_ffn_w_gate, m_ffn_w_up, m_ffn_w_down, m_norm_final_w, v_norm_mix_w, v_w_in, v_conv_w, v_conv_b, v_dt_bias, v_a_log, v_d_skip, v_ssd_norm_w, v_hg_lower_bounds, v_hg_norm_w, v_w_out, v_norm_xa_w, v_norm_mem_w, v_xa_wq, v_xa_wkv, v_xa_wo, v_norm_ffn_w, v_ffn_w_gate, v_ffn_w_up, v_ffn_w_down, v_norm_final_w):
    w = dict(norm_mix_w=norm_mix_w, w_in=w_in, conv_w=conv_w, conv_b=conv_b, dt_bias=dt_bias, a_log=a_log, d_skip=d_skip,
             ssd_norm_w=ssd_norm_w, hg_lower_bounds=hg_lower_bounds, hg_norm_w=hg_norm_w, w_out=w_out,
             norm_xa_w=norm_xa_w, norm_mem_w=norm_mem_w, xa_wq=xa_wq, xa_wkv=xa_wkv, xa_wo=xa_wo, norm_ffn_w=norm_ffn_w,
             ffn_w_gate=ffn_w_gate, ffn_w_up=ffn_w_up, ffn_w_down=ffn_w_down, norm_final_w=norm_final_w)
    m = dict(norm_mix_w=m_norm_mix_w, w_in=m_w_in, conv_w=m_conv_w, conv_b=m_conv_b, dt_bias=m_dt_bias, a_log=m_a_log,
             d_skip=m_d_skip, ssd_norm_w=m_ssd_norm_w, hg_lower_bounds=m_hg_lower_bounds, hg_norm_w=m_hg_norm_w,
             w_out=m_w_out, norm_xa_w=m_norm_xa_w, norm_mem_w=m_norm_mem_w, xa_wq=m_xa_wq, xa_wkv=m_xa_wkv,
             xa_wo=m_xa_wo, norm_ffn_w=m_norm_ffn_w, ffn_w_gate=m_ffn_w_gate, ffn_w_up=m_ffn_w_up,
             ffn_w_down=m_ffn_w_down, norm_final_w=m_norm_final_w)
    v = dict(norm_mix_w=v_norm_mix_w, w_in=v_w_in, conv_w=v_conv_w, conv_b=v_conv_b, dt_bias=v_dt_bias, a_log=v_a_log,
             d_skip=v_d_skip, ssd_norm_w=v_ssd_norm_w, hg_lower_bounds=v_hg_lower_bounds, hg_norm_w=v_hg_norm_w,
             w_out=v_w_out, norm_xa_w=v_norm_xa_w, norm_mem_w=v_norm_mem_w, xa_wq=v_xa_wq, xa_wkv=v_xa_wkv,
             xa_wo=v_xa_wo, norm_ffn_w=v_norm_ffn_w, ffn_w_gate=v_ffn_w_gate, ffn_w_up=v_ffn_w_up,
             ffn_w_down=v_ffn_w_down, norm_final_w=v_norm_final_w)
    xi, yi, ci = _place()
    chip = 2 * xi + yi
    place = jnp.stack([ci, chip]).astype(jnp.int32)

    def shard(t, name):
        return jnp.swapaxes(t[name], 1, 2) if name in TRANSPOSED else t[name]

    wsh = {name: shard(w, name) for name in BIG}
    halves = [wsh[name].shape[2] // 2 for name in BIG]

    gathered = _allgather([wsh[name][0].astype(BF) for name in BIG] + [conv_w[0]], halves + [None], "gather_weights")
    wg = dict(zip(BIG, gathered[:-1]))
    ws = {name: w[name] for name in SMALL}
    ws["conv_w"] = gathered[-1][0::2].transpose(1, 0, 2).reshape(1, 4, 1536)

    loss, gx, gb, gs = _local_step(x[0], mem[0], loss_target[0], wg, ws)

    glist = [gb[name] for name in BIG]
    from_sib = _send_sibling_halves(glist, halves, "grads_to_sibling")
    sums = [_chip_sum(g, s, place, "grads_chip_sum_" + name) for g, s, name in zip(glist, from_sib, BIG)]
    others = _send_chips([hb for hb, _ in sums], "grads_to_chips")
    reduced = [_total(own, o, place, "grads_total_" + name) for (_, own), o, name in zip(sums, others, BIG)]
    g_big = dict(zip(BIG, _swap_halves(reduced, halves, "grads_swap_halves")))

    small_parts = [gs[name] for name in SMALL] + [loss.reshape(1)]
    small_shapes = [gs[name].shape for name in SMALL] + [(1,)]
    packed = _allgather([_pack_small(small_parts)], [None], "gather_small")[0]
    small = _unpack_small(_sum8(packed, "small_total"), small_shapes)
    g_small = dict(zip(SMALL, small[:-1]))
    loss_all = small[-1][0]
    g_small["conv_w"] = lax.dynamic_slice_in_dim(g_small["conv_w"], chip * 384, 384, 2)

    grads, delta, new_m, new_v = {}, {}, {}, {}
    for name in BIG:
        outs = (g_big[name][None],) + tuple(_adamw(wsh[name], g_big[name], shard(m, name), shard(v, name),
                                                   "adamw_" + name))
        if name in TRANSPOSED:
            outs = tuple(jnp.swapaxes(o, 1, 2) for o in outs)
        grads[name], delta[name], new_m[name], new_v[name] = outs
    shapes = [w[name].shape for name in SMALL]
    packs = [_pack_small([t[name] for name in SMALL]) for t in (w, g_small, m, v)]
    outs = _adamw(packs[0][None], packs[1], packs[2][None], packs[3][None], "adamw_small")
    for name, g_, d_, nm_, nv_ in zip(SMALL, [g_small[n] for n in SMALL], *[_unpack_small(o[0], shapes) for o in outs]):
        grads[name] = g_.reshape(w[name].shape)
        delta[name], new_m[name], new_v[name] = d_, nm_, nv_

    return (loss_all, gx[None], *[grads[n] for n in WEIGHTS], *[delta[n] for n in WEIGHTS],
            *[new_m[n] for n in WEIGHTS], *[new_v[n] for n in WEIGHTS])
```

```python
import jax
import jax.numpy as jnp
from jax import lax
from jax.experimental import pallas as pl
from jax.experimental.pallas import tpu as pltpu

F32 = jnp.float32
BF = jnp.bfloat16
HI = lax.Precision.HIGHEST
MESH = pl.DeviceIdType.MESH
SDS = jax.ShapeDtypeStruct
ANY = pl.BlockSpec(memory_space=pl.ANY)

D = 1024
EPS = 1e-6
NH_SSD = 16
SSD_P = 64
NH_HG = 8
Q = 128
SUB = 16
NSUB = Q // SUB
HG_LB_FLOOR = 1e-4
XA_HEADS = 4
XA_HD = 256
MEM_LEN = 256
FFN = 2816
TL = 512
TL_FFN = 256
VMEM_LIMIT = 56 << 20

N_IN = 6672
Z0, XBC0, DT0, HQ0, HF0, HI0, HG0 = 0, 1024, 2560, 2576, 3600, 4624, 5648

ADAM_LR, ADAM_B1, ADAM_B2, ADAM_EPS, ADAM_WD, ADAM_STEP = 0.001, 0.9, 0.999, 1e-08, 0.01, 10

BIG = ("w_in", "w_out", "xa_wq", "xa_wkv", "xa_wo", "ffn_w_gate", "ffn_w_up", "ffn_w_down")
TRANSPOSED = ("w_in", "ffn_w_gate", "ffn_w_up")
SMALL = ("norm_mix_w", "conv_w", "conv_b", "dt_bias", "a_log", "d_skip", "ssd_norm_w", "hg_lower_bounds",
         "hg_norm_w", "norm_xa_w", "norm_mem_w", "norm_ffn_w", "norm_final_w")
WEIGHTS = ("norm_mix_w", "w_in", "conv_w", "conv_b", "dt_bias", "a_log", "d_skip", "ssd_norm_w", "hg_lower_bounds",
           "hg_norm_w", "w_out", "norm_xa_w", "norm_mem_w", "xa_wq", "xa_wkv", "xa_wo", "norm_ffn_w", "ffn_w_gate",
           "ffn_w_up", "ffn_w_down", "norm_final_w")


def _cparams():
    return pltpu.CompilerParams(dimension_semantics=("arbitrary",), vmem_limit_bytes=VMEM_LIMIT)


def _const(shape):
    return pl.BlockSpec(shape, lambda i: (0,) * len(shape))


def _resident(shape):
    return pl.BlockSpec(shape, lambda i: (0,) * len(shape), pipeline_mode=pl.Buffered(1))


def _rows(tl, n):
    return pl.BlockSpec((tl, n), lambda i: (i, 0))


def _dot(a, b):
    return jnp.dot(a.astype(BF), b.astype(BF), preferred_element_type=F32)


def _dot_nt(a, b):
    return lax.dot_general(a.astype(BF), b.astype(BF), (((1,), (1,)), ((), ())), preferred_element_type=F32)


def _dot_tn(a, b):
    return lax.dot_general(a.astype(BF), b.astype(BF), (((0,), (0,)), ((), ())), preferred_element_type=F32)


def _dot_hi(a, b):
    return jnp.dot(a, b, precision=HI, preferred_element_type=F32)


def _split(v, passes):
    parts, rest = [], v
    for p in range(passes):
        hi = rest.astype(BF)
        parts.append(hi)
        if p + 1 < passes:
            rest = rest - hi.astype(F32)
    return parts


def _sel_dot(a, sel, passes=3):
    sb = sel.astype(BF)
    out = None
    for part in _split(a, passes):
        t = jnp.dot(part, sb, preferred_element_type=F32)
        out = t if out is None else out + t
    return out


def _dot_sel(sel, b, passes=3):
    sb = sel.astype(BF)
    out = None
    for part in _split(b, passes):
        t = jnp.dot(sb, part, preferred_element_type=F32)
        out = t if out is None else out + t
    return out


def _iota(shape, dim):
    return lax.broadcasted_iota(jnp.int32, shape, dim)


def _sigmoid(v):
    return 0.5 * jnp.tanh(0.5 * v) + 0.5


def _rms(v, w):
    r = lax.rsqrt(jnp.mean(v * v, axis=-1, keepdims=True) + EPS)
    n = v * r
    return n * w, n, r


def _rms_bwd(dy, n, r, w):
    dn = dy * w
    return r * (dn - n * jnp.mean(dn * n, axis=-1, keepdims=True)), dy * n


def _colsum(v):
    return jnp.sum(v, axis=0, keepdims=True)


def _zero_first(*refs):
    @pl.when(pl.program_id(0) == 0)
    def _():
        for r in refs:
            r[...] = jnp.zeros_like(r)


def _in_proj(x, nw, wt, phases=()):
    L = x.shape[0]
    tl = min(TL, L)

    def body(x_ref, nw_ref, w_ref, h0_ref, z_ref, xbc_ref, hq_ref, hf_ref, hi_ref, hg_ref, dt_ref):
        h, _, _ = _rms(x_ref[...], nw_ref[...])
        hb = h.astype(BF)
        h0_ref[...] = hb

        def proj(a, b):
            return _dot_nt(hb, w_ref[a:b, :])

        z_ref[...] = proj(Z0, XBC0).astype(BF)
        xbc_ref[...] = proj(XBC0, DT0).astype(BF)
        dt_ref[...] = proj(DT0, DT0 + 128)
        hq_ref[...] = proj(HQ0, HF0).astype(BF)
        hf_ref[...] = proj(HF0, HI0)
        hi_ref[...] = proj(HI0, HG0).astype(BF)
        hg_ref[...] = proj(HG0, N_IN).astype(BF)

    outs = [SDS((L, D), BF), SDS((L, D), BF), SDS((L, 1536), BF), SDS((L, D), BF), SDS((L, D), F32),
            SDS((L, D), BF), SDS((L, D), BF), SDS((L, 128), F32)]
    steps = L // tl
    return _call(body, (x, nw, wt), name="in_proj", grid=(steps,),
                 in_specs=[_rows(tl, D), _const((1, D)), _resident((N_IN, D))],
                 out_specs=[_rows(tl, o.shape[1]) for o in outs], out_shape=outs, phases=phases,
                 mid_step=(3 * steps) // 4)


def _mem_kv(mem, nw, wkv4):
    def body(m_ref, nw_ref, w_ref, k_ref, v_ref):
        m, _, _ = _rms(m_ref[...], nw_ref[...])
        mb = m.astype(BF)
        for i in range(2):
            sl = slice(512 * i, 512 * i + 512)
            k_ref[:, sl] = jnp.dot(mb, w_ref[i], preferred_element_type=F32).astype(BF)
            v_ref[:, sl] = jnp.dot(mb, w_ref[2 + i], preferred_element_type=F32).astype(BF)

    outs = [SDS((MEM_LEN, D), BF)] * 2
    return pl.pallas_call(
        body, grid=(1,), name="mem_kv",
        in_specs=[_const((MEM_LEN, D)), _const((1, D)), _const((4, D, 512))],
        out_specs=[_const((MEM_LEN, D))] * 2, out_shape=outs, compiler_params=_cparams())(mem, nw, wkv4)


def _mem_kv_bwd(mem, nw, wkv4, dk, dv):
    def body(m_ref, nw_ref, w_ref, dk_ref, dv_ref, gnw_ref, gw_ref):
        m, n, _ = _rms(m_ref[...], nw_ref[...])
        mb = m.astype(BF)
        dm = jnp.zeros((MEM_LEN, D), F32)
        for i in range(4):
            src = dk_ref if i < 2 else dv_ref
            d = src[:, 512 * (i % 2):512 * (i % 2) + 512].astype(BF)
            gw_ref[i] = _dot_tn(mb, d)
            dm = dm + _dot_nt(d, w_ref[i])
        gnw_ref[...] = _colsum(dm * n)

    return pl.pallas_call(
        body, grid=(1,), name="mem_kv_bwd",
        in_specs=[_const((MEM_LEN, D)), _const((1, D)), _const((4, D, 512)), _const((MEM_LEN, D)), _const((MEM_LEN, D))],
        out_specs=[_const((1, D)), _const((4, D, 512))],
        out_shape=[SDS((1, D), F32), SDS((4, D, 512), F32)], compiler_params=_cparams())(mem, nw, wkv4, dk, dv)


def _softmax_rows(sc):
    e = jnp.exp(sc - jnp.max(sc, axis=-1, keepdims=True))
    return e * (1.0 / jnp.sum(e, axis=-1, keepdims=True))


def _attn_fwd(x, ya, ob, w_out, nxa, wq, k, v, wo):
    L = x.shape[0]
    tl = min(TL, L)
    scale = XA_HD ** -0.5

    def body(x_ref, ya_ref, ob_ref, wout_ref, nxa_ref, wq_ref, k_ref, v_ref, wo_ref,
             x1_ref, x2_ref, hxa_ref, q_ref, ox_ref):
        x1 = x_ref[...] + jnp.dot(ya_ref[...], wout_ref[:D, :], preferred_element_type=F32) \
            + jnp.dot(ob_ref[...], wout_ref[D:, :], preferred_element_type=F32)
        x1_ref[...] = x1
        h, _, _ = _rms(x1, nxa_ref[...])
        hb = h.astype(BF)
        hxa_ref[...] = hb
        qb = jnp.dot(hb, wq_ref[...], preferred_element_type=F32).astype(BF)
        q_ref[...] = qb
        oxs = []
        for hd in range(XA_HEADS):
            sl = slice(hd * XA_HD, (hd + 1) * XA_HD)
            p = _softmax_rows(_dot_nt(qb[:, sl], k_ref[:, sl]) * scale)
            oxs.append(_dot(p, v_ref[:, sl]))
        oxb = jnp.concatenate(oxs, axis=1).astype(BF)
        ox_ref[...] = oxb
        x2_ref[...] = x1 + jnp.dot(oxb, wo_ref[...], preferred_element_type=F32)

    outs = [SDS((L, D), F32), SDS((L, D), F32), SDS((L, D), BF), SDS((L, D), BF), SDS((L, D), BF)]
    return pl.pallas_call(
        body, grid=(L // tl,), name="attn_fwd",
        in_specs=[_rows(tl, D), _rows(tl, D), _rows(tl, D), _resident((2 * D, D)), _const((1, D)), _resident((D, D)),
                  _resident((MEM_LEN, D)), _resident((MEM_LEN, D)), _resident((D, D))],
        out_specs=[_rows(tl, D)] * 5, out_shape=outs, compiler_params=_cparams())(x, ya, ob, w_out, nxa, wq, k, v, wo)


def _ffn_loss(x2, tgt, nffn, nfin, wgt, wut, wd):
    L = x2.shape[0]
    tl = min(TL_FFN, L)

    def body(x2_ref, t_ref, nffn_ref, nfin_ref, wg_ref, wu_ref, wd_ref,
             dx2_ref, h_ref, a_ref, dx3_ref, dg_ref, du_ref, acc_ref):
        _zero_first(acc_ref)
        x2v = x2_ref[...]
        h, n2, r2 = _rms(x2v, nffn_ref[...])
        hb = h.astype(BF)
        h_ref[...] = hb
        g = _dot_nt(hb, wg_ref[...])
        u = _dot_nt(hb, wu_ref[...])
        sg = _sigmoid(g)
        ab = (g * sg * u).astype(BF)
        a_ref[...] = ab
        x3 = x2v + jnp.dot(ab, wd_ref[...], preferred_element_type=F32)
        y, n3, r3 = _rms(x3, nfin_ref[...])
        err = y - t_ref[...]
        acc_ref[0:1, :] += _colsum(err * err)
        dx3, dwf = _rms_bwd(err * (1.0 / D), n3, r3, nfin_ref[...])
        acc_ref[1:2, :] += _colsum(dwf)
        dx3b = dx3.astype(BF)
        dx3_ref[...] = dx3b
        da = _dot_nt(dx3b, wd_ref[...])
        dgb = (da * u * sg * (1.0 + g * (1.0 - sg))).astype(BF)
        dub = (da * g * sg).astype(BF)
        dg_ref[...] = dgb
        du_ref[...] = dub
        dh = jnp.dot(dgb, wg_ref[...], preferred_element_type=F32) + jnp.dot(dub, wu_ref[...], preferred_element_type=F32)
        dn, dwn = _rms_bwd(dh, n2, r2, nffn_ref[...])
        acc_ref[2:3, :] += _colsum(dwn)
        dx2_ref[...] = dx3 + dn

    outs = [SDS((L, D), F32), SDS((L, D), BF), SDS((L, FFN), BF), SDS((L, D), BF), SDS((L, FFN), BF),
            SDS((L, FFN), BF), SDS((8, D), F32)]
    wspec = _resident((FFN, D))
    return pl.pallas_call(
        body, grid=(L // tl,), name="ffn_loss",
        in_specs=[_rows(tl, D), _rows(tl, D), _const((1, D)), _const((1, D)), wspec, wspec, wspec],
        out_specs=[_rows(tl, D), _rows(tl, D), _rows(tl, FFN), _rows(tl, D), _rows(tl, FFN), _rows(tl, FFN),
                   _const((8, D))],
        out_shape=outs, compiler_params=_cparams())(x2, tgt, nffn, nfin, wgt, wut, wd)


def _attn_bwd(dx2, x1, q, k, v, nxa, wq, wo, w_out, phases=()):
    L = dx2.shape[0]
    tl = min(TL, L)
    scale = XA_HD ** -0.5

    def body(dx2_ref, x1_ref, q_ref, k_ref, v_ref, nxa_ref, wq_ref, wo_ref, wout_ref,
             dx1_ref, dya_ref, dob_ref, dq_ref, dk_ref, dv_ref, acc_ref):
        _zero_first(dk_ref, dv_ref, acc_ref)
        dx2v = dx2_ref[...]
        dox = _dot_nt(dx2v, wo_ref[...]).astype(BF)
        qb = q_ref[...]
        dqs = []
        for hd in range(XA_HEADS):
            sl = slice(hd * XA_HD, (hd + 1) * XA_HD)
            kh, vh, qh, doh = k_ref[:, sl], v_ref[:, sl], qb[:, sl], dox[:, sl]
            p = _softmax_rows(_dot_nt(qh, kh) * scale)
            dp = _dot_nt(doh, vh)
            dv_ref[:, sl] += _dot_tn(p, doh)
            ds = p * (dp - jnp.sum(dp * p, axis=-1, keepdims=True)) * scale
            dqs.append(_dot(ds, kh))
            dk_ref[:, sl] += _dot_tn(ds, qh)
        dqb = jnp.concatenate(dqs, axis=1).astype(BF)
        dq_ref[...] = dqb
        dh = _dot_nt(dqb, wq_ref[...])
        _, n1, r1 = _rms(x1_ref[...], nxa_ref[...])
        dn, dwn = _rms_bwd(dh, n1, r1, nxa_ref[...])
        acc_ref[0:1, :] += _colsum(dwn)
        dx1 = dx2v + dn
        dx1_ref[...] = dx1
        dx1b = dx1.astype(BF)
        dya_ref[...] = _dot_nt(dx1b, wout_ref[:D, :]).astype(BF)
        dob_ref[...] = _dot_nt(dx1b, wout_ref[D:, :]).astype(BF)

    outs = [SDS((L, D), F32), SDS((L, D), BF), SDS((L, D), BF), SDS((L, D), BF), SDS((MEM_LEN, D), F32),
            SDS((MEM_LEN, D), F32), SDS((8, D), F32)]
    return _call(body, (dx2, x1, q, k, v, nxa, wq, wo, w_out), name="attn_bwd", grid=(L // tl,),
                 in_specs=[_rows(tl, D), _rows(tl, D), _rows(tl, D), _resident((MEM_LEN, D)), _resident((MEM_LEN, D)),
                           _const((1, D)), _resident((D, D)), _resident((D, D)), _resident((2 * D, D))],
                 out_specs=[_rows(tl, D)] * 4 + [_const((MEM_LEN, D)), _const((MEM_LEN, D)), _const((8, D))],
                 out_shape=outs, phases=phases)


def _in_proj_bwd(x, dx1, dz, dxbc, dhq, dhf, dhi, dhg, ddt, nw, wt):
    L = x.shape[0]
    tl = min(TL, L)

    def body(x_ref, dx1_ref, dz_ref, dxbc_ref, dhq_ref, dhf_ref, dhi_ref, dhg_ref, ddt_ref, nw_ref, w_ref,
             gx_ref, acc_ref):
        _zero_first(acc_ref)
        dh = _dot(dz_ref[...], w_ref[Z0:XBC0, :]) + _dot(dxbc_ref[...], w_ref[XBC0:DT0, :]) \
            + _dot(ddt_ref[...], w_ref[DT0:DT0 + 128, :]) + _dot(dhq_ref[...], w_ref[HQ0:HF0, :]) \
            + _dot(dhf_ref[...], w_ref[HF0:HI0, :]) + _dot(dhi_ref[...], w_ref[HI0:HG0, :]) \
            + _dot(dhg_ref[...], w_ref[HG0:N_IN, :])
        _, n, r = _rms(x_ref[...], nw_ref[...])
        dn, dwn = _rms_bwd(dh, n, r, nw_ref[...])
        acc_ref[0:1, :] += _colsum(dwn)
        gx_ref[...] = dx1_ref[...] + dn

    return pl.pallas_call(
        body, grid=(L // tl,), name="in_proj_bwd",
        in_specs=[_rows(tl, D), _rows(tl, D), _rows(tl, D), _rows(tl, 1536), _rows(tl, D), _rows(tl, D), _rows(tl, D),
                  _rows(tl, D), _rows(tl, 128), _const((1, D)), _resident((N_IN, D))],
        out_specs=[_rows(tl, D), _const((8, D))], out_shape=[SDS((L, D), F32), SDS((8, D), F32)],
        compiler_params=_cparams())(x, dx1, dz, dxbc, dhq, dhf, dhi, dhg, ddt, nw, wt)


def _gw_in(h0, dz, dxbc, ddt, dhq, dhf, dhi, dhg, phases=()):
    L = h0.shape[0]
    tl = min(512, L)

    def body(h_ref, dz_ref, dxbc_ref, ddt_ref, dhq_ref, dhf_ref, dhi_ref, dhg_ref, o_ref):
        _zero_first(o_ref)
        hb = h_ref[...]
        o_ref[Z0:XBC0, :] += _dot_tn(dz_ref[...], hb)
        o_ref[XBC0:DT0, :] += _dot_tn(dxbc_ref[...], hb)
        o_ref[DT0:HQ0, :] += _dot_tn(ddt_ref[...], hb)[0:NH_SSD, :]
        o_ref[HQ0:HF0, :] += _dot_tn(dhq_ref[...], hb)
        o_ref[HF0:HI0, :] += _dot_tn(dhf_ref[...], hb)
        o_ref[HI0:HG0, :] += _dot_tn(dhi_ref[...], hb)
        o_ref[HG0:N_IN, :] += _dot_tn(dhg_ref[...], hb)

    return _call(body, (h0, dz, dxbc, ddt, dhq, dhf, dhi, dhg), name="gw_in", grid=(L // tl,),
                 in_specs=[_rows(tl, D), _rows(tl, D), _rows(tl, 1536), _rows(tl, 128), _rows(tl, D), _rows(tl, D),
                           _rows(tl, D), _rows(tl, D)],
                 out_specs=[_const((N_IN, D))], out_shape=[SDS((N_IN, D), F32)], phases=phases)


def _matmul_tn(a, b, name):
    L, M = a.shape
    N = b.shape[1]
    tl = min(512, L)

    def body(a_ref, b_ref, o_ref):
        _zero_first(o_ref)
        o_ref[...] += _dot_tn(a_ref[...], b_ref[...])

    return pl.pallas_call(
        body, grid=(L // tl,), name=name, in_specs=[_rows(tl, M), _rows(tl, N)], out_specs=_const((M, N)),
        out_shape=SDS((M, N), F32), compiler_params=_cparams())(a, b)


def _matmul_tn_pair(a0, a1, b, name):
    L, M = a0.shape
    N = b.shape[1]
    tl = min(512, L)

    def body(a0_ref, a1_ref, b_ref, o_ref):
        _zero_first(o_ref)
        bv = b_ref[...].astype(BF)
        o_ref[:M, :] += _dot_tn(a0_ref[...], bv)
        o_ref[M:, :] += _dot_tn(a1_ref[...], bv)

    return pl.pallas_call(
        body, grid=(L // tl,), name=name, in_specs=[_rows(tl, M), _rows(tl, M), _rows(tl, N)],
        out_specs=_const((2 * M, N)), out_shape=SDS((2 * M, N), F32), compiler_params=_cparams())(a0, a1, b)


def _head_expand():
    e = (jnp.right_shift(_iota((128, D), 1), 6) == _iota((128, D), 0)).astype(F32)
    et = (jnp.right_shift(_iota((D, 128), 0), 6) == _iota((D, 128), 1)).astype(F32)
    return e, et


def _conv_shifts(cur, other, up):
    rows = _iota((Q, 1), 0)
    out = []
    for s in (1, 2, 3):
        if up:
            out.append(jnp.where(rows >= Q - s, pltpu.roll(other, Q - s, 0), pltpu.roll(cur, Q - s, 0)))
        else:
            out.append(jnp.where(rows < s, pltpu.roll(other, s, 0), pltpu.roll(cur, s, 0)))
    return out


def _ssd_pre(u, dtr, dtb, alog):
    e, et = _head_expand()
    sgu = _sigmoid(u)
    xc = u * sgu
    lane = _iota((1, 128), 1)
    hmask = (lane < NH_SSD).astype(F32)
    pre = dtr + dtb
    dt = (jnp.maximum(pre, 0.0) + jnp.log(1.0 + jnp.exp(-jnp.abs(pre)))) * hmask
    a_row = -jnp.exp(alog)
    causal = _iota((Q, Q), 1) <= _iota((Q, Q), 0)
    tri = causal.astype(F32)
    acum = _dot_sel(tri, dt * a_row)
    acum_full = _sel_dot(acum, e)
    alast_full = acum_full[Q - 1:Q, :]
    dt_full = _sel_dot(dt, e)
    xs = xc[:, :D]
    return dict(e=e, et=et, sgu=sgu, xs=xs, bm=xc[:, D:D + 256], cm=xc[:, D + 256:], hmask=hmask, pre=pre, dt=dt,
                a_row=a_row, causal=causal, tri=tri, acum=acum, acum_t=acum.T, eA_full=jnp.exp(acum_full),
                dte_full=jnp.exp(alast_full - acum_full), dt_full=dt_full, xdt=xs * dt_full)


def _ssd_decay(pre, hh, cb):
    seg = pre["acum"][:, hh:hh + 1] - pre["acum_t"][hh:hh + 1, :]
    lm = jnp.where(pre["causal"], jnp.exp(jnp.minimum(seg, 0.0)), 0.0)
    return lm, cb * lm


def _ssd_fwd(xbc, dtr, z, conv_w, conv_b, dtb, alog, dskip_full, nw):
    L = xbc.shape[0]
    nc = L // Q

    def body(xbc_ref, dtr_ref, z_ref, cw_ref, cb_ref, dtb_ref, alog_ref, dsk_ref, nw_ref,
             ya_ref, y_ref, u_ref, st_ref, prev_ref, s_ref):
        @pl.when(pl.program_id(0) == 0)
        def _():
            prev_ref[...] = jnp.zeros_like(prev_ref)
            s_ref[...] = jnp.zeros_like(s_ref)

        xr = xbc_ref[...].astype(F32)
        sh = _conv_shifts(xr, prev_ref[...], up=False)
        u = cb_ref[...] + cw_ref[3:4, :] * xr + cw_ref[2:3, :] * sh[0] + cw_ref[1:2, :] * sh[1] + cw_ref[0:1, :] * sh[2]
        prev_ref[...] = xr
        ub = u.astype(BF)
        u_ref[...] = ub
        pre = _ssd_pre(ub.astype(F32), dtr_ref[...], dtb_ref[...], alog_ref[...])
        lo = _iota((1, 128), 1) < SSD_P
        s_old = s_ref[...]
        st_ref[0] = s_old
        ys = []
        for g in range(2):
            bg, cg = pre["bm"][:, 128 * g:128 * g + 128], pre["cm"][:, 128 * g:128 * g + 128]
            cb = _dot_nt(cg, bg)
            gs = slice(512 * g, 512 * g + 512)
            yd = []
            for j in range(4 * g, 4 * g + 4):
                xp = pre["xdt"][:, 128 * j:128 * j + 128].astype(BF)
                _, m0 = _ssd_decay(pre, 2 * j, cb)
                _, m1 = _ssd_decay(pre, 2 * j + 1, cb)
                yd.append(jnp.where(lo, _dot(m0, xp), _dot(m1, xp)))
            yoff = _dot_nt(cg, s_old[gs, :]) * pre["eA_full"][:, gs]
            ys.append(jnp.concatenate(yd, axis=1) + yoff)
            st = _dot_tn((pre["xdt"] * pre["dte_full"])[:, gs], bg)
            cdcol = jnp.exp(_dot_sel(pre["et"][gs, :], pre["acum_t"])[:, Q - 1:Q])
            s_ref[gs, :] = s_old[gs, :] * cdcol + st
        y = jnp.concatenate(ys, axis=1) + dsk_ref[...] * pre["xs"]
        yb = y.astype(BF)
        y_ref[...] = yb
        zf = z_ref[...].astype(F32)
        yz = yb.astype(F32) * zf * _sigmoid(zf)
        outs = []
        for g in range(2):
            gs = slice(512 * g, 512 * g + 512)
            o, _, _ = _rms(yz[:, gs], nw_ref[:, gs])
            outs.append(o)
        ya_ref[...] = jnp.concatenate(outs, axis=1).astype(BF)

    outs = [SDS((L, D), BF), SDS((L, D), BF), SDS((L, 1536), BF), SDS((nc, D, 128), F32)]
    return pl.pallas_call(
        body, grid=(nc,), name="ssd_fwd",
        in_specs=[_rows(Q, 1536), _rows(Q, 128), _rows(Q, D), _const((4, 1536)), _const((1, 1536)), _const((1, 128)),
                  _const((1, 128)), _const((1, D)), _const((1, D))],
        out_specs=[_rows(Q, D), _rows(Q, D), _rows(Q, 1536), pl.BlockSpec((1, D, 128), lambda i: (i, 0, 0))],
        out_shape=outs, scratch_shapes=[pltpu.VMEM((Q, 1536), F32), pltpu.VMEM((D, 128), F32)],
        compiler_params=_cparams())(xbc, dtr, z, conv_w, conv_b, dtb, alog, dskip_full, nw)


def _ssd_bwd(dya, y, z, u, xbc, dtr, states, conv_w, dtb, alog, dskip_full, nw):
    L = dya.shape[0]
    nc = L // Q

    def body(dya_ref, y_ref, z_ref, u_ref, xc_ref, xp_ref, dtr_ref, st_ref, cw_ref, dtb_ref, alog_ref, dsk_ref, nw_ref,
             dz_ref, dxbc_ref, ddt_ref, gconv_ref, ghead_ref, glane_ref, gs_ref, ndu_ref):
        step = pl.program_id(0)

        @pl.when(step == 0)
        def _():
            for r in (gconv_ref, ghead_ref, glane_ref, gs_ref, ndu_ref):
                r[...] = jnp.zeros_like(r)

        uf = u_ref[...].astype(F32)
        pre = _ssd_pre(uf, dtr_ref[...], dtb_ref[...], alog_ref[...])
        e, et, xs, xdt = pre["e"], pre["et"], pre["xs"], pre["xdt"]
        lane = _iota((1, 128), 1)
        lo = lane < SSD_P
        sub = _iota((128, 1), 0)
        zf = z_ref[...].astype(F32)
        sgz = _sigmoid(zf)
        sz = zf * sgz
        yv = y_ref[...].astype(F32)
        yz = yv * sz
        dyav = dya_ref[...].astype(F32)
        dyz, dnw = [], []
        for g in range(2):
            gs = slice(512 * g, 512 * g + 512)
            _, n, r = _rms(yz[:, gs], nw_ref[:, gs])
            dv, dw = _rms_bwd(dyav[:, gs], n, r, nw_ref[:, gs])
            dyz.append(dv)
            dnw.append(dw)
        dyz = jnp.concatenate(dyz, axis=1)
        glane_ref[1:2, :] += _colsum(jnp.concatenate(dnw, axis=1))
        dy = dyz * sz
        dz_ref[...] = (dyz * yv * sgz * (1.0 + zf * (1.0 - sgz))).astype(BF)
        glane_ref[0:1, :] += _colsum(dy * xs)
        dxs = dsk_ref[...] * dy

        s_in = st_ref[0]
        gst = gs_ref[...]
        gy = dy * pre["eA_full"]
        xdte = xdt * pre["dte_full"]
        dacum = jnp.zeros((Q, 128), F32)
        dacum_t = jnp.zeros((128, Q), F32)
        dxdt, dacum_full, ddte_full, dbs, dcs = [], [], [], [], []
        for g in range(2):
            gs = slice(512 * g, 512 * g + 512)
            bg, cg = pre["bm"][:, 128 * g:128 * g + 128], pre["cm"][:, 128 * g:128 * g + 128]
            sg_, dg_ = s_in[gs, :], gst[gs, :]
            yoff = _dot_nt(cg, sg_) * pre["eA_full"][:, gs]
            dc = _dot(gy[:, gs], sg_)
            dsin = _dot_tn(gy[:, gs], cg)
            dacum_full.append(dy[:, gs] * yoff)
            tg = _dot_nt(bg, dg_)
            ddte_full.append(tg * xdt[:, gs])
            db = _dot(xdte[:, gs], dg_)
            cb = _dot_nt(cg, bg)
            dcb = jnp.zeros((Q, Q), F32)
            dxg = []
            for j in range(4 * g, 4 * g + 4):
                xp = xdt[:, 128 * j:128 * j + 128].astype(BF)
                dyp = dy[:, 128 * j:128 * j + 128]
                dxp = jnp.zeros((Q, 128), F32)
                for idx in range(2):
                    hh = 2 * j + idx
                    lm, m = _ssd_decay(pre, hh, cb)
                    dym = jnp.where(lo if idx == 0 else jnp.logical_not(lo), dyp, 0.0).astype(BF)
                    dm = jnp.where(pre["causal"], _dot_nt(dym, xp), 0.0)
                    w = dm * m
                    dacum = dacum + jnp.where(lane == hh, jnp.sum(w, axis=1, keepdims=True), 0.0)
                    dacum_t = dacum_t + jnp.where(sub == hh, jnp.sum(w, axis=0, keepdims=True), 0.0)
                    dcb = dcb + dm * lm
                    dxp = dxp + _dot_tn(m, dym)
                dxg.append(dxp)
            dxdt.append(jnp.concatenate(dxg, axis=1) + tg * pre["dte_full"][:, gs])
            dcs.append(dc + _dot(dcb, bg))
            dbs.append(db + _dot_tn(dcb, cg))
            cdcol = jnp.exp(_dot_sel(et[gs, :], pre["acum_t"])[:, Q - 1:Q])
            gs_ref[gs, :] = dsin + dg_ * cdcol
        dxdt = jnp.concatenate(dxdt, axis=1)
        dacum = dacum + _sel_dot(jnp.concatenate(dacum_full, axis=1), et, 2) - dacum_t.T
        alast = pre["acum"][Q - 1:Q, :]
        dte = jnp.exp(alast - pre["acum"])
        ddte = _sel_dot(jnp.concatenate(ddte_full, axis=1), et, 2) * dte
        dacum = dacum - ddte
        dcd_col = jnp.sum(_dot_sel(e, gst * s_in, 2), axis=1, keepdims=True)
        dcd_row = jnp.broadcast_to(dcd_col, (128, 128)).T[0:1, :]
        dalast = _colsum(ddte) + dcd_row * jnp.exp(alast)
        dacum = dacum + jnp.where(_iota((Q, 1), 0) == Q - 1, dalast, 0.0)
        ddt = _sel_dot(dxdt * xs, et, 2)
        dxs = dxs + dxdt * pre["dt_full"]
        dda = _dot_sel(pre["tri"].T, dacum)
        ddt = ddt + dda * pre["a_row"]
        ghead_ref[1:2, :] += _colsum(dda * pre["dt"])
        ddtr = ddt * _sigmoid(pre["pre"]) * pre["hmask"]
        ghead_ref[0:1, :] += _colsum(ddtr)
        ddt_ref[...] = ddtr

        dxc = jnp.concatenate([dxs] + dbs + dcs, axis=1)
        sgu = pre["sgu"]
        du = dxc * sgu * (1.0 + uf * (1.0 - sgu))
        shu = _conv_shifts(du, ndu_ref[...], up=True)
        dxr = cw_ref[3:4, :] * du + cw_ref[2:3, :] * shu[0] + cw_ref[1:2, :] * shu[1] + cw_ref[0:1, :] * shu[2]
        ndu_ref[...] = du
        dxbc_ref[...] = dxr.astype(BF)
        xr = xc_ref[...].astype(F32)
        xprev = jnp.where(step == nc - 1, 0.0, xp_ref[...].astype(F32))
        shx = _conv_shifts(xr, xprev, up=False)
        gconv_ref[3:4, :] += _colsum(du * xr)
        gconv_ref[2:3, :] += _colsum(du * shx[0])
        gconv_ref[1:2, :] += _colsum(du * shx[1])
        gconv_ref[0:1, :] += _colsum(du * shx[2])
        gconv_ref[4:5, :] += _colsum(du)

        @pl.when(step == nc - 1)
        def _():
            ghead_ref[2:3, :] = ghead_ref[1:2, :] * pre["a_row"]
            ghead_ref[3:4, :] = _dot_hi(glane_ref[...], et)[0:1, :]

    rev = lambda i: (nc - 1 - i, 0)
    outs = [SDS((L, D), BF), SDS((L, 1536), BF), SDS((L, 128), F32), SDS((8, 1536), F32), SDS((8, 128), F32),
            SDS((8, D), F32)]
    return pl.pallas_call(
        body, grid=(nc,), name="ssd_bwd",
        in_specs=[pl.BlockSpec((Q, D), rev), pl.BlockSpec((Q, D), rev), pl.BlockSpec((Q, D), rev),
                  pl.BlockSpec((Q, 1536), rev), pl.BlockSpec((Q, 1536), rev),
                  pl.BlockSpec((Q, 1536), lambda i: (jnp.maximum(nc - 2 - i, 0), 0)),
                  pl.BlockSpec((Q, 128), rev), pl.BlockSpec((1, D, 128), lambda i: (nc - 1 - i, 0, 0)),
                  _const((4, 1536)), _const((1, 128)), _const((1, 128)), _const((1, D)), _const((1, D))],
        out_specs=[pl.BlockSpec((Q, D), rev), pl.BlockSpec((Q, 1536), rev), pl.BlockSpec((Q, 128), rev),
                   _const((8, 1536)), _const((8, 128)), _const((8, D))],
        out_shape=outs, scratch_shapes=[pltpu.VMEM((D, 128), F32), pltpu.VMEM((Q, 1536), F32)],
        compiler_params=_cparams())(dya, y, z, u, xbc, xbc, dtr, states, conv_w, dtb, alog, dskip_full, nw)


def _hg_gates(hq, hf, hlb):
    h0, h1 = hlb[0:1, :], hlb[1:2, :]
    mx = jnp.maximum(h0, h1)
    e0, e1 = jnp.exp(h0 - mx), jnp.exp(h1 - mx)
    lb = e0 / (e0 + e1)
    sg = _sigmoid(hf)
    fg = lb + (1.0 - lb) * sg
    tri = (_iota((Q, Q), 1) <= _iota((Q, Q), 0)).astype(F32)
    return hq * _sigmoid(hq), 1.0 - fg, fg, sg, lb, e1 / (e0 + e1), _dot_sel(tri, jnp.log(fg))


def _hg_intra(b, q, k):
    rowblk = jnp.right_shift(_iota((Q, 1), 0), 4)
    mids = [b[SUB * i + SUB // 2:SUB * i + SUB // 2 + 1, :] for i in range(NSUB)]
    prevs = [mids[0]] + [b[SUB * i - 1:SUB * i, :] for i in range(1, NSUB)]
    mfull = jnp.concatenate([jnp.broadcast_to(r, (SUB, 128)) for r in mids], axis=0)
    rfull = jnp.concatenate([jnp.broadcast_to(r, (SUB, 128)) for r in prevs], axis=0)
    eqd, ek, eqo = jnp.exp(b - mfull), jnp.exp(mfull - b), jnp.exp(b - rfull)
    qd, qo, khat = q * eqd, q * eqo, k * ek
    rtab = jnp.concatenate(prevs, axis=0)
    djs = [jnp.exp(rtab - mids[j]) for j in range(NSUB)]
    zero = jnp.zeros((SUB, 128), F32)
    cols = []
    for j in range(NSUB):
        pieces = []
        for i in range(NSUB):
            rs = slice(SUB * i, SUB * i + SUB)
            pieces.append(zero if i < j else qd[rs] if i == j else qo[rs] * djs[j][i:i + 1, :])
        cols.append(jnp.concatenate(pieces, axis=0))
    qt = jnp.concatenate(cols, axis=1).astype(BF)
    kt = jnp.concatenate([jnp.where(rowblk == j, khat, 0.0) for j in range(NSUB)], axis=1).astype(BF)
    causal = _iota((Q, Q), 1) <= _iota((Q, Q), 0)
    att = jnp.where(causal, _dot_nt(qt, kt), 0.0)
    return att, qt, kt, (eqd, ek, eqo, djs), causal


def _hg_intra_bwd(dqt, dkt, qt, kt, factors):
    eqd, ek, eqo, djs = factors
    dqd, dqo, dkh, db = [], [], [], []
    for i in range(NSUB):
        rs = slice(SUB * i, SUB * i + SUB)
        diag = slice(128 * i, 128 * i + 128)
        dqd.append(dqt[rs, diag])
        dkh.append(dkt[rs, diag])
        dbi = qt[rs, diag].astype(F32) * dqt[rs, diag] - kt[rs, diag].astype(F32) * dkt[rs, diag]
        acc = jnp.zeros((SUB, 128), F32)
        for j in range(i):
            bl = slice(128 * j, 128 * j + 128)
            acc = acc + dqt[rs, bl] * djs[j][i:i + 1, :]
            dbi = dbi + qt[rs, bl].astype(F32) * dqt[rs, bl]
        dqo.append(acc)
        db.append(dbi)
    cat = lambda t: jnp.concatenate(t, axis=0)
    return cat(dqd) * eqd + cat(dqo) * eqo, cat(dkh) * ek, cat(db)


def _hg_att_exact(b, q, k, b_ref, q_ref, att_t_ref):
    b_ref[...] = b
    q_ref[...] = q
    att_t_ref[...] = jnp.zeros((Q, Q), F32)
    rows, lane = _iota((Q, 1), 0), _iota((1, Q), 1)

    def step(i, carry):
        e = jnp.exp(jnp.minimum(b_ref[pl.ds(i, 1), :] - b, 0.0))
        col = jnp.sum(q_ref[pl.ds(i, 1), :] * k * e, axis=1, keepdims=True)
        att_t_ref[...] = jnp.where(lane == i, jnp.where(rows <= i, col, 0.0), att_t_ref[...])
        return carry

    lax.fori_loop(0, Q, step, 0)
    return att_t_ref[...].T


def _hg_att_exact_bwd(da, b, q, k, b_ref, q_ref, da_t_ref, dq_ref, dk_ref):
    b_ref[...] = b
    q_ref[...] = q
    da_t_ref[...] = da.T
    dk_ref[...] = jnp.zeros((Q, 128), F32)
    lane = _iota((1, Q), 1)

    def step(i, carry):
        e = jnp.exp(jnp.minimum(b_ref[pl.ds(i, 1), :] - b, 0.0))
        g = jnp.sum(jnp.where(lane == i, da_t_ref[...], 0.0), axis=1, keepdims=True) * e
        dq_ref[pl.ds(i, 1), :] = jnp.sum(g * k, axis=0, keepdims=True)
        dk_ref[...] += g * q_ref[pl.ds(i, 1), :]
        return carry

    lax.fori_loop(0, Q, step, 0)
    dq, dk = dq_ref[...], dk_ref[...]
    return dq, dk, q * dq - k * dk


def _hg_fwd(hq, hf, hi, hg, hlb, nw, fast):
    L = hq.shape[0]
    nc = L // Q

    def run(exact, step, hq_ref, hf_ref, hi_ref, hg_ref, hlb_ref, nw_ref, ob_ref, o_ref, st_ref, s_ref, *tmp):
        @pl.when(step == 0)
        def _():
            s_ref[...] = jnp.zeros_like(s_ref)

        qf, kf, _, _, _, _, bcum = _hg_gates(hq_ref[...].astype(F32), hf_ref[...], hlb_ref[...])
        gate = hg_ref[...].astype(F32)
        heads = [slice(128 * h, 128 * h + 128) for h in range(NH_HG)]
        if exact:
            atts = [_hg_att_exact(bcum[:, sl], qf[:, sl], kf[:, sl], *tmp).astype(BF) for sl in heads]
        else:
            atts = [_hg_intra(bcum[:, sl], qf[:, sl], kf[:, sl])[0].astype(BF) for sl in heads]
        olds = [s_ref[sl, :] for sl in heads]
        outs_ = [_dot(att, hi_ref[:, sl]) + _dot(qf[:, sl] * jnp.exp(bcum[:, sl]), s)
                 for att, sl, s in zip(atts, heads, olds)]
        for sl, s, o in zip(heads, olds, outs_):
            b, k = bcum[:, sl], kf[:, sl]
            st_ref[0, sl, :] = s
            blast = b[Q - 1:Q, :]
            s_ref[sl, :] = s * jnp.exp(b.T[:, Q - 1:Q]) + _dot_tn(k * jnp.exp(blast - b), hi_ref[:, sl])
            ob = o.astype(BF)
            o_ref[:, sl] = ob
            on, _, _ = _rms(ob.astype(F32), nw_ref[...])
            gt = gate[:, sl]
            ob_ref[:, sl] = (on * gt * _sigmoid(gt)).astype(BF)

    def body(fast_ref, *refs):
        step = pl.program_id(0)
        pl.when(fast_ref[0] == 1)(lambda: run(False, step, *refs))
        pl.when(fast_ref[0] != 1)(lambda: run(True, step, *refs))

    rows = pl.BlockSpec((Q, D), lambda i, f: (i, 0))
    outs = [SDS((L, D), BF), SDS((L, D), BF), SDS((nc, D, 128), F32)]
    grid_spec = pltpu.PrefetchScalarGridSpec(
        num_scalar_prefetch=1, grid=(nc,),
        in_specs=[rows] * 4 + [pl.BlockSpec((2, D), lambda i, f: (0, 0)), pl.BlockSpec((1, 128), lambda i, f: (0, 0))],
        out_specs=[rows, rows, pl.BlockSpec((1, D, 128), lambda i, f: (i, 0, 0))],
        scratch_shapes=[pltpu.VMEM((D, 128), F32), pltpu.VMEM((Q, 128), F32), pltpu.VMEM((Q, 128), F32),
                        pltpu.VMEM((Q, Q), F32)])
    return pl.pallas_call(body, grid_spec=grid_spec, name="hg_fwd", out_shape=outs,
                          compiler_params=_cparams())(fast, hq, hf, hi, hg, hlb, nw)


def _hg_bwd(dob, o, hq, hf, hi, hg, states, hlb, nw, fast, phases=()):
    L = dob.shape[0]
    nc = L // Q

    def run(exact, step, dob_ref, o_ref, hq_ref, hf_ref, hi_ref, hg_ref, st_ref, hlb_ref, nw_ref,
            dhq_ref, dhf_ref, dhi_ref, dhg_ref, acc_ref, gs_ref, *tmp):
        @pl.when(step == 0)
        def _():
            acc_ref[...] = jnp.zeros_like(acc_ref)
            gs_ref[...] = jnp.zeros_like(gs_ref)

        hqv = hq_ref[...].astype(F32)
        qf, kf, fg, sg, lb, sm1, bcum = _hg_gates(hqv, hf_ref[...], hlb_ref[...])
        gate = hg_ref[...].astype(F32)
        sgg = _sigmoid(gate)
        nwv = nw_ref[...]
        tri_t = (_iota((Q, Q), 1) >= _iota((Q, Q), 0)).astype(F32)
        ones8 = jnp.ones((8, 128), BF)
        heads = [slice(128 * h, 128 * h + 128) for h in range(NH_HG)]
        row_last = _iota((Q, 1), 0) == Q - 1
        dobs, dnws = [], []
        for sl in heads:
            gt, sgt = gate[:, sl], sgg[:, sl]
            _, n, r = _rms(o_ref[:, sl].astype(F32), nwv)
            dobv = dob_ref[:, sl].astype(F32)
            dhg_ref[:, sl] = (dobv * n * nwv * sgt * (1.0 + gt * (1.0 - sgt))).astype(BF)
            do, dw = _rms_bwd(dobv * gt * sgt, n, r, nwv)
            dnws.append(_colsum(dw))
            dobs.append(do.astype(BF))
        causal = _iota((Q, Q), 1) <= _iota((Q, Q), 0)
        if exact:
            intra = [(_hg_att_exact(bcum[:, sl], qf[:, sl], kf[:, sl], *tmp[:3]),) for sl in heads]
        else:
            intra = [_hg_intra(bcum[:, sl], qf[:, sl], kf[:, sl]) for sl in heads]
        states = [(st_ref[0, sl, :], gs_ref[sl, :]) for sl in heads]
        das = [jnp.where(causal, _dot_nt(dob_h, hi_ref[:, sl]), 0.0) for dob_h, sl in zip(dobs, heads)]
        dqhats = [_dot_nt(dob_h, s) for dob_h, (s, _) in zip(dobs, states)]
        dkhats = [_dot_nt(hi_ref[:, sl], gst) for sl, (_, gst) in zip(heads, states)]
        if not exact:
            dqts = [jnp.dot(da.astype(BF), it[2], preferred_element_type=F32) for da, it in zip(das, intra)]
            dkts = [lax.dot_general(da.astype(BF), it[1], (((0,), (0,)), ((), ())), preferred_element_type=F32)
                    for da, it in zip(das, intra)]
        dqs, dks, dgls = [], [], []
        for h, sl in enumerate(heads):
            b, q, k = bcum[:, sl], qf[:, sl], kf[:, sl]
            att = intra[h][0]
            s, gst = states[h]
            dob_h, dqhat, dkhat = dobs[h], dqhats[h], dkhats[h]
            eb = jnp.exp(b)
            blast = b[Q - 1:Q, :]
            ekl = jnp.exp(blast - b)
            qhat, khat = q * eb, k * ekl
            dhi_ref[:, sl] = (_dot_tn(att, dob_h) + _dot(khat, gst)).astype(BF)
            if exact:
                dq_i, dk_i, db = _hg_att_exact_bwd(das[h], b, q, k, *tmp)
            else:
                dq_i, dk_i, db = _hg_intra_bwd(dqts[h], dkts[h], *intra[h][1:4])
            dqs.append(dq_i + dqhat * eb)
            dks.append(dk_i + dkhat * ekl)
            qhat_r, khat_r = qhat.astype(BF).astype(F32), khat.astype(BF).astype(F32)
            decay_row = sum(_dot_nt(ones8, part) for part in _split(gst * s, 2))[0:1, :]
            dblast = _colsum(dkhat * khat_r) + decay_row * jnp.exp(blast)
            dgls.append(db + qhat_r * dqhat - khat_r * dkhat + jnp.where(row_last, dblast, 0.0))
            gs_ref[sl, :] = _dot_tn(qhat, dob_h) + gst * jnp.exp(b.T[:, Q - 1:Q])
        dq, dk, db = (jnp.concatenate(t, axis=1) for t in (dqs, dks, dgls))
        dgl = _dot_sel(tri_t, db, 2)
        sgq = _sigmoid(hqv)
        dhq_ref[...] = (dq * sgq * (1.0 + hqv * (1.0 - sgq))).astype(BF)
        dfg = dgl / fg - dk
        dhf_ref[...] = (dfg * (1.0 - lb) * sg * (1.0 - sg)).astype(BF)
        acc_ref[0:1, :] += _colsum(dfg * (1.0 - sg))
        acc_ref[1:2, :] += jnp.concatenate(dnws, axis=1)

        @pl.when(step == nc - 1)
        def _():
            dlb = acc_ref[0:1, :] * lb * sm1
            acc_ref[2:3, :] = dlb
            acc_ref[3:4, :] = -dlb
            tot = acc_ref[1:2, 0:128]
            for h in range(1, NH_HG):
                tot = tot + acc_ref[1:2, 128 * h:128 * h + 128]
            acc_ref[4:5, 0:128] = tot

    def body(fast_ref, *refs):
        step = pl.program_id(0)
        pl.when(fast_ref[0] == 1)(lambda: run(False, step, *refs))
        pl.when(fast_ref[0] != 1)(lambda: run(True, step, *refs))

    rev = pl.BlockSpec((Q, D), lambda i, f: (nc - 1 - i, 0))
    outs = [SDS((L, D), BF)] * 4 + [SDS((8, D), F32)]
    return _call(
        body, (fast, dob, o, hq, hf, hi, hg, states, hlb, nw), name="hg_bwd", grid=(nc,), prefetch=1,
        in_specs=[rev] * 6 + [pl.BlockSpec((1, D, 128), lambda i, f: (nc - 1 - i, 0, 0)),
                              pl.BlockSpec((2, D), lambda i, f: (0, 0)), pl.BlockSpec((1, 128), lambda i, f: (0, 0))],
        out_specs=[rev] * 4 + [pl.BlockSpec((8, D), lambda i, f: (0, 0))], out_shape=outs,
        scratch_shapes=[pltpu.VMEM((D, 128), F32), pltpu.VMEM((Q, 128), F32), pltpu.VMEM((Q, 128), F32),
                        pltpu.VMEM((Q, Q), F32), pltpu.VMEM((Q, 128), F32), pltpu.VMEM((Q, 128), F32)], phases=phases)


def _place():
    return lax.axis_index("x"), lax.axis_index("y"), lax.axis_index("c")


def _phase_io(phase):
    kind, arrays, halves = phase
    n = len(arrays)
    dma = pltpu.SemaphoreType.DMA
    if kind == "gather":
        outs = [SDS((8,) + a.shape if hc is None else (4,) + a.shape, a.dtype) for a, hc in zip(arrays, halves)]
        return outs, [dma((7 * n,)), dma((7 * n,)), dma((n,))], {}
    if kind == "sibling":
        return [SDS((4, g.shape[1], hc), g.dtype) for g, hc in zip(arrays, halves)], [dma((n,)), dma((n,))], {}
    if kind == "chips":
        return [SDS((3,) + p.shape[1:], p.dtype) for p in arrays], [dma((3 * n,)), dma((3 * n,))], {}
    assert kind == "swap"
    return [SDS(b.shape, b.dtype) for b in arrays], [dma((n,)), dma((n,))], {a: a for a in range(n)}


def _gather_events(ins, outs, sems, halves):
    send_sems, recv_sems, local_sems = sems
    n = len(ins)

    def parts(a):
        x, y, c = _place()
        hc = halves[a]
        me, sibling = (x, y, c), (x, y, 1 - c)
        chips = [(1 - x, y), (x, 1 - y), (1 - x, 1 - y)]

        def slot(p):
            if hc is None:
                return outs[a].at[4 * p[0] + 2 * p[1] + p[2]]
            return outs[a].at[2 * p[0] + p[1], :, pl.ds(p[2] * hc, hc)]

        own = ins[a] if hc is None else ins[a].at[:, pl.ds(c * hc, hc)]

        def copy(k, piece, to, src=None):
            return pltpu.make_async_remote_copy(
                src_ref=slot(piece) if src is None else src, dst_ref=slot(piece),
                send_sem=send_sems.at[7 * a + k], recv_sem=recv_sems.at[7 * a + k], device_id=to, device_id_type=MESH)

        return dict(
            mine=lambda: pltpu.make_async_copy(own, slot(me), local_sems.at[a]),
            starts=lambda: [copy(0, me, sibling, src=own)] + [copy(1 + j, me, (*chip, c), src=own)
                                                               for j, chip in enumerate(chips)],
            arrive=lambda: [copy(1 + j, (*chip, c), me) for j, chip in enumerate(chips)],
            passed=lambda: [copy(4 + j, (*chip, c), sibling) for j, chip in enumerate(chips)],
            from_sibling=lambda: [copy(0, sibling, me)] + [copy(4 + j, (*chip, 1 - c), me)
                                                            for j, chip in enumerate(chips)])

    def first():
        for a in range(n):
            p = parts(a)
            p["mine"]().start()
            for cp in p["starts"]():
                cp.start()

    def mid():
        for a in range(n):
            p = parts(a)
            for cp_in, cp_out in zip(p["arrive"](), p["passed"]()):
                cp_in.wait_recv()
                cp_out.start()

    def last():
        for a in range(n):
            p = parts(a)
            for cp in p["from_sibling"]():
                cp.wait_recv()
            for cp in p["starts"]() + p["passed"]():
                cp.wait_send()
            p["mine"]().wait()

    return dict(first=first, mid=mid, last=last)


def _exchange_events(kind, ins, outs, sems, halves):
    send_sems, recv_sems = sems
    n = len(outs)

    def copies():
        x, y, c = _place()
        if kind == "sibling":
            return [pltpu.make_async_remote_copy(
                src_ref=ins[a].at[:, :, pl.ds((1 - c) * halves[a], halves[a])], dst_ref=outs[a],
                send_sem=send_sems.at[a], recv_sem=recv_sems.at[a], device_id=(x, y, 1 - c), device_id_type=MESH)
                for a in range(n)]
        chips = [(1 - x, y), (x, 1 - y), (1 - x, 1 - y)]
        return [pltpu.make_async_remote_copy(
            src_ref=ins[a].at[2 * px + py], dst_ref=outs[a].at[k], send_sem=send_sems.at[3 * a + k],
            recv_sem=recv_sems.at[3 * a + k], device_id=(px, py, c), device_id_type=MESH)
            for a in range(n) for k, (px, py) in enumerate(chips)]

    def first():
        for cp in copies():
            cp.start()

    def last():
        for cp in copies():
            cp.wait()

    return dict(first=first, last=last)


def _swap_events(outs, sems, halves):
    send_sems, recv_sems = sems
    n = len(outs)

    def copy(a, landing):
        x, y, c = _place()
        cols = lambda which: outs[a].at[:, pl.ds(which * halves[a], halves[a])]
        return pltpu.make_async_remote_copy(
            src_ref=cols(c), dst_ref=cols(1 - c) if landing else cols(c), send_sem=send_sems.at[a],
            recv_sem=recv_sems.at[a], device_id=(x, y, 1 - c), device_id_type=MESH)

    def first():
        for a in range(n):
            copy(a, False).start()

    def last():
        for a in range(n):
            copy(a, True).wait_recv()
        for a in range(n):
            copy(a, False).wait_send()

    return dict(first=first, last=last)


def _phase_events(phase, ins, outs, sems):
    kind, _, halves = phase
    if kind == "gather":
        return _gather_events(ins, outs, sems, halves)
    if kind == "swap":
        return _swap_events(outs, sems, halves)
    return _exchange_events(kind, ins, outs, sems, halves)


def _split_refs(refs, counts):
    out, at = [], 0
    for c in counts:
        out.append(list(refs[at:at + c]))
        at += c
    return out


def _comm_plumbing(phases, first_in, first_out):
    ios = [_phase_io(p) for p in phases]
    arrays = [a for p in phases for a in p[1]]
    out_shape = [o for io in ios for o in io[0]]
    sem_shapes = [s for io in ios for s in io[1]]
    aliases, ai, ao = {}, first_in, first_out
    for p, io in zip(phases, ios):
        aliases.update({ai + k: ao + v for k, v in io[2].items()})
        ai, ao = ai + len(p[1]), ao + len(io[0])

    def events(cins, couts, sems):
        evs = [_phase_events(p, i, o, s) for p, i, o, s in zip(
            phases, _split_refs(cins, [len(p[1]) for p in phases]), _split_refs(couts, [len(io[0]) for io in ios]),
            _split_refs(sems, [len(io[1]) for io in ios]))]

        def run(key):
            for ev in evs:
                if key in ev:
                    ev[key]()

        return {key: (lambda key=key: run(key)) for key in ("first", "mid", "last")}

    def regroup(flat):
        return _split_refs(flat, [len(io[0]) for io in ios])

    return arrays, out_shape, sem_shapes, aliases, events, regroup


def _run_phases(phases, name):
    arrays, out_shape, sem_shapes, aliases, events, regroup = _comm_plumbing(phases, 0, 0)

    def body(*refs):
        cins, couts, sems = _split_refs(refs, [len(arrays), len(out_shape), len(sem_shapes)])
        ev = events(cins, couts, sems)
        for key in ("first", "mid", "last"):
            ev[key]()

    outs = pl.pallas_call(
        body, name=name, in_specs=[ANY] * len(arrays), out_specs=[ANY] * len(out_shape), out_shape=out_shape,
        scratch_shapes=sem_shapes, input_output_aliases=aliases)(*arrays)
    return regroup(outs)


def _call(body, args, *, name, grid, in_specs, out_specs, out_shape, scratch_shapes=(), prefetch=0, phases=(),
          mid_step=None):
    steps = grid[0]
    arrays, c_shape, sem_shapes, aliases, events, regroup = _comm_plumbing(
        phases, prefetch + len(in_specs), len(out_specs))
    counts = [prefetch, len(in_specs), len(arrays), len(out_specs), len(c_shape), len(scratch_shapes), len(sem_shapes)]

    def wrapped(*refs):
        pre, ins, cins, outs, couts, scratch, sems = _split_refs(refs, counts)
        if not phases:
            return body(*pre, *ins, *outs, *scratch)
        step = pl.program_id(0)
        ev = events(cins, couts, sems)
        pl.when(step == 0)(ev["first"])
        body(*pre, *ins, *outs, *scratch)
        pl.when(step == (steps // 2 if mid_step is None else mid_step))(ev["mid"])
        pl.when(step == steps - 1)(ev["last"])

    grid_spec = pltpu.PrefetchScalarGridSpec(
        num_scalar_prefetch=prefetch, grid=grid, in_specs=list(in_specs) + [ANY] * len(arrays),
        out_specs=list(out_specs) + [ANY] * len(c_shape), scratch_shapes=list(scratch_shapes) + sem_shapes)
    outs = pl.pallas_call(
        wrapped, grid_spec=grid_spec, name=name, out_shape=list(out_shape) + c_shape, input_output_aliases=aliases,
        compiler_params=_cparams())(*args, *arrays)
    return list(outs[:len(out_specs)]), regroup(outs[len(out_specs):])


def _tile(rows, cols, nbuf):
    budget = (VMEM_LIMIT // 3) // (2 * nbuf * 4)
    if rows % 8 == 0:
        cands = [t for t in range(8, rows + 1, 8) if rows % t == 0 and t * cols <= budget]
        pref = [t for t in cands if t % 16 == 0]
        return (max(pref) if pref else max(cands) if cands else 8), cols
    cands = [t for t in range(128, cols + 1, 128) if cols % t == 0 and rows * t <= budget]
    return rows, (max(cands) if cands else 128)


def _chip_sum(g, from_sib, place, name):
    _, rows, hc = from_sib.shape
    tr, tc = _tile(rows, hc, 4)
    ni, nj = rows // tr, hc // tc

    def body(p_ref, g_ref, s_ref, hb_ref, own_ref):
        s = g_ref[...] + s_ref[...]
        hb_ref[...] = s.astype(BF)

        @pl.when(pl.program_id(2) == p_ref[1])
        def _():
            own_ref[...] = s

    grid_spec = pltpu.PrefetchScalarGridSpec(
        num_scalar_prefetch=1, grid=(ni, nj, 4),
        in_specs=[pl.BlockSpec((None, tr, tc), lambda i, j, k, p: (k, i, p[0] * nj + j)),
                  pl.BlockSpec((None, tr, tc), lambda i, j, k, p: (k, i, j))],
        out_specs=[pl.BlockSpec((None, tr, tc), lambda i, j, k, p: (k, i, j)),
                   pl.BlockSpec((tr, tc), lambda i, j, k, p: (i, j))])
    return pl.pallas_call(
        body, grid_spec=grid_spec, name=name, out_shape=[SDS((4, rows, hc), BF), SDS((rows, hc), F32)],
        compiler_params=pltpu.CompilerParams(dimension_semantics=("arbitrary",) * 3,
                                             vmem_limit_bytes=VMEM_LIMIT))(place, g, from_sib)


def _total(own, parts, place, name):
    rows, hc = own.shape
    tr, tc = _tile(rows, hc, 5)
    ni, nj = rows // tr, hc // tc

    def body(p_ref, own_ref, parts_ref, o_ref):
        s = own_ref[...]
        for k in range(3):
            s = s + parts_ref[k].astype(F32)
        o_ref[...] = s

    grid_spec = pltpu.PrefetchScalarGridSpec(
        num_scalar_prefetch=1, grid=(ni, nj),
        in_specs=[pl.BlockSpec((tr, tc), lambda i, j, p: (i, j)),
                  pl.BlockSpec((3, tr, tc), lambda i, j, p: (0, i, j))],
        out_specs=pl.BlockSpec((tr, tc), lambda i, j, p: (i, p[0] * nj + j)))
    return pl.pallas_call(
        body, grid_spec=grid_spec, name=name, out_shape=SDS((rows, 2 * hc), F32),
        compiler_params=pltpu.CompilerParams(dimension_semantics=("arbitrary",) * 2,
                                             vmem_limit_bytes=VMEM_LIMIT))(place, own, parts)


def _sum8(parts, name):
    R = parts.shape[1]

    def body(p_ref, o_ref):
        s = p_ref[0]
        for k in range(1, 8):
            s = s + p_ref[k]
        o_ref[...] = s

    return pl.pallas_call(
        body, grid=(1,), name=name, in_specs=[_const((8, R, 128))], out_specs=_const((R, 128)),
        out_shape=SDS((R, 128), F32), compiler_params=_cparams())(parts)


def _adamw(w, g, m, v, name):
    _, R, C = w.shape
    tr, tc = _tile(R, C, 7)
    c1 = 1.0 / (1.0 - ADAM_B1 ** ADAM_STEP)
    c2 = 1.0 / (1.0 - ADAM_B2 ** ADAM_STEP)

    def body(w_ref, g_ref, m_ref, v_ref, d_ref, nm_ref, nv_ref):
        gv = g_ref[...]
        nm = ADAM_B1 * m_ref[...] + (1.0 - ADAM_B1) * gv
        nv = ADAM_B2 * v_ref[...] + (1.0 - ADAM_B2) * gv * gv
        nm_ref[...] = nm
        nv_ref[...] = nv
        d_ref[...] = -ADAM_LR * ((nm * c1) / (jnp.sqrt(nv * c2) + ADAM_EPS) + ADAM_WD * w_ref[...])

    blk3 = pl.BlockSpec((None, tr, tc), lambda i, j: (0, i, j))
    return pl.pallas_call(
        body, grid=(R // tr, C // tc), name=name,
        in_specs=[blk3, pl.BlockSpec((tr, tc), lambda i, j: (i, j)), blk3, blk3], out_specs=[blk3] * 3,
        out_shape=[SDS((1, R, C), F32)] * 3,
        compiler_params=pltpu.CompilerParams(dimension_semantics=("arbitrary",) * 2,
                                             vmem_limit_bytes=VMEM_LIMIT))(w, g, m, v)


def _pack_small(parts):
    rows = []
    for p in parts:
        p = p.reshape(-1)
        rows.append(jnp.pad(p, (0, (-p.shape[0]) % 128)).reshape(-1, 128))
    out = jnp.concatenate(rows, axis=0)
    return jnp.pad(out, ((0, (-out.shape[0]) % 8), (0, 0)))


def _unpack_small(packed, shapes):
    out, row = [], 0
    for shp in shapes:
        n = 1
        for s in shp:
            n *= s
        nr = -(-n // 128)
        out.append(packed[row:row + nr].reshape(-1)[:n].reshape(shp))
        row += nr
    return out


def _pad_lanes(v, n=128):
    return jnp.pad(v, ((0, 0), (0, n - v.shape[1])))


GROUP_FFN = ("ffn_w_gate", "ffn_w_up", "ffn_w_down")
GROUP_ATTN = ("w_out", "xa_wq", "xa_wkv", "xa_wo")


def kernel(x, mem, norm_mix_w, w_in, conv_w, conv_b, dt_bias, a_log, d_skip, ssd_norm_w, hg_lower_bounds, hg_norm_w, w_out, norm_xa_w, norm_mem_w, xa_wq, xa_wkv, xa_wo, norm_ffn_w, ffn_w_gate, ffn_w_up, ffn_w_down, norm_final_w, loss_target, m_norm_mix_w, m_w_in, m_conv_w, m_conv_b, m_dt_bias, m_a_log, m_d_skip, m_ssd_norm_w, m_hg_lower_bounds, m_hg_norm_w, m_w_out, m_norm_xa_w, m_norm_mem_w, m_xa_wq, m_xa_wkv, m_xa_wo, m_norm_ffn_w, m_ffn_w_gate, m_ffn_w_up, m_ffn_w_down, m_norm_final_w, v_norm_mix_w, v_w_in, v_conv_w, v_conv_b, v_dt_bias, v_a_log, v_d_skip, v_ssd_norm_w, v_hg_lower_bounds, v_hg_norm_w, v_w_out, v_norm_xa_w, v_norm_mem_w, v_xa_wq, v_xa_wkv, v_xa_wo, v_norm_ffn_w, v_ffn_w_gate, v_ffn_w_up, v_ffn_w_down, v_norm_final_w):
    w = dict(norm_mix_w=norm_mix_w, w_in=w_in, conv_w=conv_w, conv_b=conv_b, dt_bias=dt_bias, a_log=a_log, d_skip=d_skip,
             ssd_norm_w=ssd_norm_w, hg_lower_bounds=hg_lower_bounds, hg_norm_w=hg_norm_w, w_out=w_out,
             norm_xa_w=norm_xa_w, norm_mem_w=norm_mem_w, xa_wq=xa_wq, xa_wkv=xa_wkv, xa_wo=xa_wo, norm_ffn_w=norm_ffn_w,
             ffn_w_gate=ffn_w_gate, ffn_w_up=ffn_w_up, ffn_w_down=ffn_w_down, norm_final_w=norm_final_w)
    m = dict(norm_mix_w=m_norm_mix_w, w_in=m_w_in, conv_w=m_conv_w, conv_b=m_conv_b, dt_bias=m_dt_bias, a_log=m_a_log,
             d_skip=m_d_skip, ssd_norm_w=m_ssd_norm_w, hg_lower_bounds=m_hg_lower_bounds, hg_norm_w=m_hg_norm_w,
             w_out=m_w_out, norm_xa_w=m_norm_xa_w, norm_mem_w=m_norm_mem_w, xa_wq=m_xa_wq, xa_wkv=m_xa_wkv,
             xa_wo=m_xa_wo, norm_ffn_w=m_norm_ffn_w, ffn_w_gate=m_ffn_w_gate, ffn_w_up=m_ffn_w_up,
             ffn_w_down=m_ffn_w_down, norm_final_w=m_norm_final_w)
    v = dict(norm_mix_w=v_norm_mix_w, w_in=v_w_in, conv_w=v_conv_w, conv_b=v_conv_b, dt_bias=v_dt_bias, a_log=v_a_log,
             d_skip=v_d_skip, ssd_norm_w=v_ssd_norm_w, hg_lower_bounds=v_hg_lower_bounds, hg_norm_w=v_hg_norm_w,
             w_out=v_w_out, norm_xa_w=v_norm_xa_w, norm_mem_w=v_norm_mem_w, xa_wq=v_xa_wq, xa_wkv=v_xa_wkv,
             xa_wo=v_xa_wo, norm_ffn_w=v_norm_ffn_w, ffn_w_gate=v_ffn_w_gate, ffn_w_up=v_ffn_w_up,
             ffn_w_down=v_ffn_w_down, norm_final_w=v_norm_final_w)
    xi, yi, ci = _place()
    chip = 2 * xi + yi
    place = jnp.stack([ci, chip]).astype(jnp.int32)

    def shard(t, name):
        return jnp.swapaxes(t[name], 1, 2) if name in TRANSPOSED else t[name]

    wsh = {name: shard(w, name) for name in BIG}
    half = {name: wsh[name].shape[2] // 2 for name in BIG}
    payload = {name: wsh[name][0].astype(BF) for name in BIG}
    ws = {name: w[name] for name in SMALL}
    xs, mems, tgt = x[0], mem[0], loss_target[0]

    def chip_sums(names, grads, from_sib):
        return [_chip_sum(grads[n], s, place, "grads_chip_sum_" + n) for n, s in zip(names, from_sib)]

    def totals(names, sums, others):
        return [_total(own, o, place, "grads_total_" + n) for n, (_, own), o in zip(names, sums, others)]

    ((w_in4, conv_all),) = _run_phases([("gather", [payload["w_in"], conv_w[0]], [half["w_in"], None])], "gather_w_in")
    w_in_t = w_in4.reshape(N_IN, D)
    ws["conv_w"] = conv_all[0::2].transpose(1, 0, 2).reshape(1, 4, 1536)
    rest = [n for n in BIG if n != "w_in"]
    (h0, z, xbc, hq, hf, hi, hg, dtr), (gathered,) = _in_proj(
        xs, ws["norm_mix_w"], w_in_t, phases=[("gather", [payload[n] for n in rest], [half[n] for n in rest])])
    wg = dict(zip(rest, gathered))
    wg_t, wu_t = wg["ffn_w_gate"].reshape(FFN, D), wg["ffn_w_up"].reshape(FFN, D)
    wd = wg["ffn_w_down"].reshape(FFN, D)
    w_out_f = wg["w_out"].reshape(2 * D, D)
    wq, wo = wg["xa_wq"].reshape(D, D), wg["xa_wo"].reshape(D, D)
    dtb, alog = _pad_lanes(ws["dt_bias"]), _pad_lanes(ws["a_log"])
    dskip_full = jnp.repeat(ws["d_skip"], SSD_P, axis=1)
    cw, conv_bias = ws["conv_w"][0], ws["conv_b"]
    hlb = ws["hg_lower_bounds"]
    hg_fast = (jnp.min(jax.nn.softmax(hlb, axis=0)[0]) >= HG_LB_FLOOR).astype(jnp.int32).reshape(1)

    ya, yssd, u, st_ssd = _ssd_fwd(xbc, dtr, z, cw, conv_bias, dtb, alog, dskip_full, ws["ssd_norm_w"])
    ob, ohg, st_hg = _hg_fwd(hq, hf, hi, hg, hlb, ws["hg_norm_w"], hg_fast)
    kmem, vmem = _mem_kv(mems, ws["norm_mem_w"], wg["xa_wkv"])
    x1, x2, hxa, q, ox = _attn_fwd(xs, ya, ob, w_out_f, ws["norm_xa_w"], wq, kmem, vmem, wo)
    nfin = ws["norm_final_w"].reshape(1, D)
    dx2, hffn, act, dx3, dg, du, acc_f = _ffn_loss(x2, tgt, ws["norm_ffn_w"], nfin, wg_t, wu_t, wd)

    gb = {"ffn_w_gate": _matmul_tn(dg, hffn, "gw_gate").reshape(4, FFN // 4, D),
          "ffn_w_up": _matmul_tn(du, hffn, "gw_up").reshape(4, FFN // 4, D),
          "ffn_w_down": _matmul_tn(act, dx3, "gw_down").reshape(4, FFN // 4, D)}
    (dx1, dya, dob, dq, dk, dv, acc_a), (sib_ffn,) = _attn_bwd(
        dx2, x1, q, kmem, vmem, ws["norm_xa_w"], wq, wo, w_out_f,
        phases=[("sibling", [gb[n] for n in GROUP_FFN], [half[n] for n in GROUP_FFN])])
    sums_ffn = chip_sums(GROUP_FFN, gb, sib_ffn)
    g_nmem, gb["xa_wkv"] = _mem_kv_bwd(mems, ws["norm_mem_w"], wg["xa_wkv"], dk, dv)
    gb["w_out"] = _matmul_tn_pair(ya, ob, dx1, "gw_out").reshape(4, D // 2, D)
    gb["xa_wq"] = _matmul_tn(hxa, dq, "gw_q").reshape(4, D // 4, D)
    gb["xa_wo"] = _matmul_tn(ox, dx2, "gw_o").reshape(4, D // 4, D)
    (dhq, dhf, dhi, dhg, acc_h), (others_ffn, sib_attn) = _hg_bwd(
        dob, ohg, hq, hf, hi, hg, st_hg, hlb, ws["hg_norm_w"], hg_fast,
        phases=[("chips", [hb for hb, _ in sums_ffn], None),
                ("sibling", [gb[n] for n in GROUP_ATTN], [half[n] for n in GROUP_ATTN])])
    red_ffn = totals(GROUP_FFN, sums_ffn, others_ffn)
    sums_attn = chip_sums(GROUP_ATTN, gb, sib_attn)
    dz, dxbc, ddt, gconv, ghead, glane = _ssd_bwd(dya, yssd, z, u, xbc, dtr, st_ssd, cw, dtb, alog, dskip_full,
                                                  ws["ssd_norm_w"])
    gx, acc_i = _in_proj_bwd(xs, dx1, dz, dxbc, dhq, dhf, dhi, dhg, ddt, ws["norm_mix_w"], w_in_t)
    (gw_in_t,), (g_ffn, others_attn) = _gw_in(
        h0, dz, dxbc, ddt, dhq, dhf, dhi, dhg,
        phases=[("swap", red_ffn, [half[n] for n in GROUP_FFN]), ("chips", [hb for hb, _ in sums_attn], None)])
    red_attn = totals(GROUP_ATTN, sums_attn, others_attn)
    gb["w_in"] = gw_in_t.reshape(4, N_IN // 4, D)
    g_attn, (sib_in,) = _run_phases([("swap", red_attn, [half[n] for n in GROUP_ATTN]),
                                     ("sibling", [gb["w_in"]], [half["w_in"]])], "grads_w_in_to_sibling")
    sums_in = chip_sums(("w_in",), gb, [sib_in])
    ((others_in,),) = _run_phases([("chips", [sums_in[0][0]], None)], "grads_w_in_to_chips")
    red_in = totals(("w_in",), sums_in, [others_in])

    gs = {
        "norm_mix_w": acc_i[0:1], "conv_w": gconv[0:4][None], "conv_b": gconv[4:5],
        "dt_bias": ghead[0:1, :NH_SSD], "a_log": ghead[2:3, :NH_SSD], "d_skip": ghead[3:4, :NH_SSD],
        "ssd_norm_w": glane[1:2], "hg_lower_bounds": acc_h[2:4], "hg_norm_w": acc_h[4:5, :128],
        "norm_xa_w": acc_a[0:1], "norm_mem_w": g_nmem, "norm_ffn_w": acc_f[2:3], "norm_final_w": acc_f[1],
    }
    loss = (0.5 / D) * jnp.sum(acc_f[0])
    small_parts = [gs[name] for name in SMALL] + [loss.reshape(1)]
    small_shapes = [gs[name].shape for name in SMALL] + [(1,)]
    (g_in,), (packed,) = _run_phases([("swap", red_in, [half["w_in"]]),
                                      ("gather", [_pack_small(small_parts)], [None])], "grads_finish")
    g_big = dict(zip(GROUP_FFN + GROUP_ATTN + ("w_in",), g_ffn + g_attn + [g_in]))
    small = _unpack_small(_sum8(packed, "small_total"), small_shapes)
    g_small = dict(zip(SMALL, small[:-1]))
    loss_all = small[-1][0]
    g_small["conv_w"] = lax.dynamic_slice_in_dim(g_small["conv_w"], chip * 384, 384, 2)

    grads, delta, new_m, new_v = {}, {}, {}, {}
    for name in BIG:
        outs = (g_big[name][None],) + tuple(_adamw(wsh[name], g_big[name], shard(m, name), shard(v, name),
                                                   "adamw_" + name))
        if name in TRANSPOSED:
            outs = tuple(jnp.swapaxes(o, 1, 2) for o in outs)
        grads[name], delta[name], new_m[name], new_v[name] = outs
    shapes = [w[name].shape for name in SMALL]
    packs = [_pack_small([t[name] for name in SMALL]) for t in (w, g_small, m, v)]
    outs = _adamw(packs[0][None], packs[1], packs[2][None], packs[3][None], "adamw_small")
    for name, g_, d_, nm_, nv_ in zip(SMALL, [g_small[n] for n in SMALL], *[_unpack_small(o[0], shapes) for o in outs]):
        grads[name] = g_.reshape(w[name].shape)
        delta[name], new_m[name], new_v[name] = d_, nm_, nv_

    return (loss_all, gx[None], *[grads[n] for n in WEIGHTS], *[delta[n] for n in WEIGHTS],
            *[new_m[n] for n in WEIGHTS], *[new_v[n] for n in WEIGHTS])
```

```python
import jax
import jax.numpy as jnp
from jax import lax
from jax.experimental import pallas as pl
from jax.experimental.pallas import tpu as pltpu

F32 = jnp.float32
BF = jnp.bfloat16
HI = lax.Precision.HIGHEST
MESH = pl.DeviceIdType.MESH
SDS = jax.ShapeDtypeStruct
ANY = pl.BlockSpec(memory_space=pl.ANY)

D = 1024
EPS = 1e-6
NH_SSD = 16
SSD_P = 64
NH_HG = 8
Q = 128
SUB = 32
NSUB = Q // SUB
HG_LB_FLOOR = 1e-2
XA_HEADS = 4
XA_HD = 256
MEM_LEN = 256
FFN = 2816
TL = 512
TL_FFN = 256
VMEM_LIMIT = 56 << 20

N_IN = 6672
Z0, XBC0, DT0, HQ0, HF0, HI0, HG0 = 0, 1024, 2560, 2576, 3600, 4624, 5648

ADAM_LR, ADAM_B1, ADAM_B2, ADAM_EPS, ADAM_WD, ADAM_STEP = 0.001, 0.9, 0.999, 1e-08, 0.01, 10

BIG = ("w_in", "w_out", "xa_wq", "xa_wkv", "xa_wo", "ffn_w_gate", "ffn_w_up", "ffn_w_down")
TRANSPOSED = ("w_in", "ffn_w_gate", "ffn_w_up")
SMALL = ("norm_mix_w", "conv_w", "conv_b", "dt_bias", "a_log", "d_skip", "ssd_norm_w", "hg_lower_bounds",
         "hg_norm_w", "norm_xa_w", "norm_mem_w", "norm_ffn_w", "norm_final_w")
WEIGHTS = ("norm_mix_w", "w_in", "conv_w", "conv_b", "dt_bias", "a_log", "d_skip", "ssd_norm_w", "hg_lower_bounds",
           "hg_norm_w", "w_out", "norm_xa_w", "norm_mem_w", "xa_wq", "xa_wkv", "xa_wo", "norm_ffn_w", "ffn_w_gate",
           "ffn_w_up", "ffn_w_down", "norm_final_w")


def _cparams():
    return pltpu.CompilerParams(dimension_semantics=("arbitrary",), vmem_limit_bytes=VMEM_LIMIT)


def _const(shape):
    return pl.BlockSpec(shape, lambda i: (0,) * len(shape))


def _resident(shape):
    return pl.BlockSpec(shape, lambda i: (0,) * len(shape), pipeline_mode=pl.Buffered(1))


def _rows(tl, n):
    return pl.BlockSpec((tl, n), lambda i: (i, 0))


def _dot(a, b):
    return jnp.dot(a.astype(BF), b.astype(BF), preferred_element_type=F32)


def _dot_nt(a, b):
    return lax.dot_general(a.astype(BF), b.astype(BF), (((1,), (1,)), ((), ())), preferred_element_type=F32)


def _dot_tn(a, b):
    return lax.dot_general(a.astype(BF), b.astype(BF), (((0,), (0,)), ((), ())), preferred_element_type=F32)


def _dot_hi(a, b):
    return jnp.dot(a, b, precision=HI, preferred_element_type=F32)


def _split(v, passes):
    parts, rest = [], v
    for p in range(passes):
        hi = rest.astype(BF)
        parts.append(hi)
        if p + 1 < passes:
            rest = rest - hi.astype(F32)
    return parts


def _sel_dot(a, sel, passes=3):
    sb = sel.astype(BF)
    out = None
    for part in _split(a, passes):
        t = jnp.dot(part, sb, preferred_element_type=F32)
        out = t if out is None else out + t
    return out


def _dot_sel(sel, b, passes=3):
    sb = sel.astype(BF)
    out = None
    for part in _split(b, passes):
        t = jnp.dot(sb, part, preferred_element_type=F32)
        out = t if out is None else out + t
    return out


def _iota(shape, dim):
    return lax.broadcasted_iota(jnp.int32, shape, dim)


def _sigmoid(v):
    return 0.5 * jnp.tanh(0.5 * v) + 0.5


def _rms(v, w):
    r = lax.rsqrt(jnp.mean(v * v, axis=-1, keepdims=True) + EPS)
    n = v * r
    return n * w, n, r


def _rms_bwd(dy, n, r, w):
    dn = dy * w
    return r * (dn - n * jnp.mean(dn * n, axis=-1, keepdims=True)), dy * n


def _colsum(v):
    return jnp.sum(v, axis=0, keepdims=True)


def _zero_first(*refs):
    @pl.when(pl.program_id(0) == 0)
    def _():
        for r in refs:
            r[...] = jnp.zeros_like(r)


def _in_proj(x, nw, wt, phases=()):
    L = x.shape[0]
    tl = min(TL, L)

    def body(x_ref, nw_ref, w_ref, h0_ref, z_ref, xbc_ref, hq_ref, hf_ref, hi_ref, hg_ref, dt_ref):
        h, _, _ = _rms(x_ref[...], nw_ref[...])
        hb = h.astype(BF)
        h0_ref[...] = hb

        def proj(a, b):
            return _dot_nt(hb, w_ref[a:b, :])

        z_ref[...] = proj(Z0, XBC0).astype(BF)
        xbc_ref[...] = proj(XBC0, DT0).astype(BF)
        dt_ref[...] = proj(DT0, DT0 + 128)
        hq_ref[...] = proj(HQ0, HF0).astype(BF)
        hf_ref[...] = proj(HF0, HI0)
        hi_ref[...] = proj(HI0, HG0).astype(BF)
        hg_ref[...] = proj(HG0, N_IN).astype(BF)

    outs = [SDS((L, D), BF), SDS((L, D), BF), SDS((L, 1536), BF), SDS((L, D), BF), SDS((L, D), F32),
            SDS((L, D), BF), SDS((L, D), BF), SDS((L, 128), F32)]
    steps = L // tl
    return _call(body, (x, nw, wt), name="in_proj", grid=(steps,),
                 in_specs=[_rows(tl, D), _const((1, D)), _resident((N_IN, D))],
                 out_specs=[_rows(tl, o.shape[1]) for o in outs], out_shape=outs, phases=phases,
                 mid_step=(3 * steps) // 4)


def _mem_kv(mem, nw, wkv4):
    def body(m_ref, nw_ref, w_ref, k_ref, v_ref):
        m, _, _ = _rms(m_ref[...], nw_ref[...])
        mb = m.astype(BF)
        for i in range(2):
            sl = slice(512 * i, 512 * i + 512)
            k_ref[:, sl] = jnp.dot(mb, w_ref[i], preferred_element_type=F32).astype(BF)
            v_ref[:, sl] = jnp.dot(mb, w_ref[2 + i], preferred_element_type=F32).astype(BF)

    outs = [SDS((MEM_LEN, D), BF)] * 2
    return pl.pallas_call(
        body, grid=(1,), name="mem_kv",
        in_specs=[_const((MEM_LEN, D)), _const((1, D)), _const((4, D, 512))],
        out_specs=[_const((MEM_LEN, D))] * 2, out_shape=outs, compiler_params=_cparams())(mem, nw, wkv4)


def _mem_kv_bwd(mem, nw, wkv4, dk, dv):
    def body(m_ref, nw_ref, w_ref, dk_ref, dv_ref, gnw_ref, gw_ref):
        m, n, _ = _rms(m_ref[...], nw_ref[...])
        mb = m.astype(BF)
        dm = jnp.zeros((MEM_LEN, D), F32)
        for i in range(4):
            src = dk_ref if i < 2 else dv_ref
            d = src[:, 512 * (i % 2):512 * (i % 2) + 512].astype(BF)
            gw_ref[i] = _dot_tn(mb, d)
            dm = dm + _dot_nt(d, w_ref[i])
        gnw_ref[...] = _colsum(dm * n)

    return pl.pallas_call(
        body, grid=(1,), name="mem_kv_bwd",
        in_specs=[_const((MEM_LEN, D)), _const((1, D)), _const((4, D, 512)), _const((MEM_LEN, D)), _const((MEM_LEN, D))],
        out_specs=[_const((1, D)), _const((4, D, 512))],
        out_shape=[SDS((1, D), F32), SDS((4, D, 512), F32)], compiler_params=_cparams())(mem, nw, wkv4, dk, dv)


def _softmax_rows(sc):
    e = jnp.exp(sc - jnp.max(sc, axis=-1, keepdims=True))
    return e * (1.0 / jnp.sum(e, axis=-1, keepdims=True))


def _attn_fwd(x, ya, ob, w_out, nxa, wq, k, v, wo):
    L = x.shape[0]
    tl = min(TL, L)
    scale = XA_HD ** -0.5

    def body(x_ref, ya_ref, ob_ref, wout_ref, nxa_ref, wq_ref, k_ref, v_ref, wo_ref,
             x1_ref, x2_ref, hxa_ref, q_ref, ox_ref):
        x1 = x_ref[...] + jnp.dot(ya_ref[...], wout_ref[:D, :], preferred_element_type=F32) \
            + jnp.dot(ob_ref[...], wout_ref[D:, :], preferred_element_type=F32)
        x1_ref[...] = x1
        h, _, _ = _rms(x1, nxa_ref[...])
        hb = h.astype(BF)
        hxa_ref[...] = hb
        qb = jnp.dot(hb, wq_ref[...], preferred_element_type=F32).astype(BF)
        q_ref[...] = qb
        heads = [slice(hd * XA_HD, (hd + 1) * XA_HD) for hd in range(XA_HEADS)]
        ps = [_softmax_rows(_dot_nt(qb[:, sl], k_ref[:, sl]) * scale) for sl in heads]
        oxs = [_dot(p, v_ref[:, sl]) for p, sl in zip(ps, heads)]
        oxb = jnp.concatenate(oxs, axis=1).astype(BF)
        ox_ref[...] = oxb
        x2_ref[...] = x1 + jnp.dot(oxb, wo_ref[...], preferred_element_type=F32)

    outs = [SDS((L, D), F32), SDS((L, D), F32), SDS((L, D), BF), SDS((L, D), BF), SDS((L, D), BF)]
    return pl.pallas_call(
        body, grid=(L // tl,), name="attn_fwd",
        in_specs=[_rows(tl, D), _rows(tl, D), _rows(tl, D), _resident((2 * D, D)), _const((1, D)), _resident((D, D)),
                  _resident((MEM_LEN, D)), _resident((MEM_LEN, D)), _resident((D, D))],
        out_specs=[_rows(tl, D)] * 5, out_shape=outs, compiler_params=_cparams())(x, ya, ob, w_out, nxa, wq, k, v, wo)


def _ffn_loss(x2, tgt, nffn, nfin, wgt, wut, wd):
    L = x2.shape[0]
    tl = min(TL_FFN, L)

    def body(x2_ref, t_ref, nffn_ref, nfin_ref, wg_ref, wu_ref, wd_ref,
             dx2_ref, h_ref, a_ref, dx3_ref, dg_ref, du_ref, acc_ref):
        _zero_first(acc_ref)
        x2v = x2_ref[...]
        h, n2, r2 = _rms(x2v, nffn_ref[...])
        hb = h.astype(BF)
        h_ref[...] = hb
        g = _dot_nt(hb, wg_ref[...])
        u = _dot_nt(hb, wu_ref[...])
        sg = _sigmoid(g)
        ab = (g * sg * u).astype(BF)
        a_ref[...] = ab
        x3 = x2v + jnp.dot(ab, wd_ref[...], preferred_element_type=F32)
        y, n3, r3 = _rms(x3, nfin_ref[...])
        err = y - t_ref[...]
        acc_ref[0:1, :] += _colsum(err * err)
        dx3, dwf = _rms_bwd(err * (1.0 / D), n3, r3, nfin_ref[...])
        acc_ref[1:2, :] += _colsum(dwf)
        dx3b = dx3.astype(BF)
        dx3_ref[...] = dx3b
        da = _dot_nt(dx3b, wd_ref[...])
        dgb = (da * u * sg * (1.0 + g * (1.0 - sg))).astype(BF)
        dub = (da * g * sg).astype(BF)
        dg_ref[...] = dgb
        du_ref[...] = dub
        dh = jnp.dot(dgb, wg_ref[...], preferred_element_type=F32) + jnp.dot(dub, wu_ref[...], preferred_element_type=F32)
        dn, dwn = _rms_bwd(dh, n2, r2, nffn_ref[...])
        acc_ref[2:3, :] += _colsum(dwn)
        dx2_ref[...] = dx3 + dn

    outs = [SDS((L, D), F32), SDS((L, D), BF), SDS((L, FFN), BF), SDS((L, D), BF), SDS((L, FFN), BF),
            SDS((L, FFN), BF), SDS((8, D), F32)]
    wspec = _resident((FFN, D))
    return pl.pallas_call(
        body, grid=(L // tl,), name="ffn_loss",
        in_specs=[_rows(tl, D), _rows(tl, D), _const((1, D)), _const((1, D)), wspec, wspec, wspec],
        out_specs=[_rows(tl, D), _rows(tl, D), _rows(tl, FFN), _rows(tl, D), _rows(tl, FFN), _rows(tl, FFN),
                   _const((8, D))],
        out_shape=outs, compiler_params=_cparams())(x2, tgt, nffn, nfin, wgt, wut, wd)


def _attn_bwd(dx2, x1, q, k, v, nxa, wq, wo, w_out, phases=()):
    L = dx2.shape[0]
    tl = min(TL, L)
    scale = XA_HD ** -0.5

    def body(dx2_ref, x1_ref, q_ref, k_ref, v_ref, nxa_ref, wq_ref, wo_ref, wout_ref,
             dx1_ref, dya_ref, dob_ref, dq_ref, dk_ref, dv_ref, acc_ref):
        _zero_first(dk_ref, dv_ref, acc_ref)
        dx2v = dx2_ref[...]
        dox = _dot_nt(dx2v, wo_ref[...]).astype(BF)
        qb = q_ref[...]
        heads = [slice(hd * XA_HD, (hd + 1) * XA_HD) for hd in range(XA_HEADS)]
        ps = [_softmax_rows(_dot_nt(qb[:, sl], k_ref[:, sl]) * scale) for sl in heads]
        dps = [_dot_nt(dox[:, sl], v_ref[:, sl]) for sl in heads]
        dss = [(p * (dp - jnp.sum(dp * p, axis=-1, keepdims=True)) * scale).astype(BF) for p, dp in zip(ps, dps)]
        for sl, p, ds in zip(heads, ps, dss):
            dv_ref[:, sl] += _dot_tn(p, dox[:, sl])
            dk_ref[:, sl] += _dot_tn(ds, qb[:, sl])
        dqs = [_dot(ds, k_ref[:, sl]) for sl, ds in zip(heads, dss)]
        dqb = jnp.concatenate(dqs, axis=1).astype(BF)
        dq_ref[...] = dqb
        dh = _dot_nt(dqb, wq_ref[...])
        _, n1, r1 = _rms(x1_ref[...], nxa_ref[...])
        dn, dwn = _rms_bwd(dh, n1, r1, nxa_ref[...])
        acc_ref[0:1, :] += _colsum(dwn)
        dx1 = dx2v + dn
        dx1_ref[...] = dx1
        dx1b = dx1.astype(BF)
        dya_ref[...] = _dot_nt(dx1b, wout_ref[:D, :]).astype(BF)
        dob_ref[...] = _dot_nt(dx1b, wout_ref[D:, :]).astype(BF)

    outs = [SDS((L, D), F32), SDS((L, D), BF), SDS((L, D), BF), SDS((L, D), BF), SDS((MEM_LEN, D), F32),
            SDS((MEM_LEN, D), F32), SDS((8, D), F32)]
    return _call(body, (dx2, x1, q, k, v, nxa, wq, wo, w_out), name="attn_bwd", grid=(L // tl,),
                 in_specs=[_rows(tl, D), _rows(tl, D), _rows(tl, D), _resident((MEM_LEN, D)), _resident((MEM_LEN, D)),
                           _const((1, D)), _resident((D, D)), _resident((D, D)), _resident((2 * D, D))],
                 out_specs=[_rows(tl, D)] * 4 + [_const((MEM_LEN, D)), _const((MEM_LEN, D)), _const((8, D))],
                 out_shape=outs, phases=phases)


def _in_proj_bwd(x, dx1, dz, dxbc, dhq, dhf, dhi, dhg, ddt, nw, wt):
    L = x.shape[0]
    tl = min(TL, L)

    def body(x_ref, dx1_ref, dz_ref, dxbc_ref, dhq_ref, dhf_ref, dhi_ref, dhg_ref, ddt_ref, nw_ref, w_ref,
             gx_ref, acc_ref):
        _zero_first(acc_ref)
        dh = _dot(dz_ref[...], w_ref[Z0:XBC0, :]) + _dot(dxbc_ref[...], w_ref[XBC0:DT0, :]) \
            + _dot(ddt_ref[...], w_ref[DT0:DT0 + 128, :]) + _dot(dhq_ref[...], w_ref[HQ0:HF0, :]) \
            + _dot(dhf_ref[...], w_ref[HF0:HI0, :]) + _dot(dhi_ref[...], w_ref[HI0:HG0, :]) \
            + _dot(dhg_ref[...], w_ref[HG0:N_IN, :])
        _, n, r = _rms(x_ref[...], nw_ref[...])
        dn, dwn = _rms_bwd(dh, n, r, nw_ref[...])
        acc_ref[0:1, :] += _colsum(dwn)
        gx_ref[...] = dx1_ref[...] + dn

    return pl.pallas_call(
        body, grid=(L // tl,), name="in_proj_bwd",
        in_specs=[_rows(tl, D), _rows(tl, D), _rows(tl, D), _rows(tl, 1536), _rows(tl, D), _rows(tl, D), _rows(tl, D),
                  _rows(tl, D), _rows(tl, 128), _const((1, D)), _resident((N_IN, D))],
        out_specs=[_rows(tl, D), _const((8, D))], out_shape=[SDS((L, D), F32), SDS((8, D), F32)],
        compiler_params=_cparams())(x, dx1, dz, dxbc, dhq, dhf, dhi, dhg, ddt, nw, wt)


def _gw_in(h0, dz, dxbc, ddt, dhq, dhf, dhi, dhg, phases=()):
    L = h0.shape[0]
    tl = min(512, L)

    def body(h_ref, dz_ref, dxbc_ref, ddt_ref, dhq_ref, dhf_ref, dhi_ref, dhg_ref, o_ref):
        _zero_first(o_ref)
        hb = h_ref[...]
        o_ref[Z0:XBC0, :] += _dot_tn(dz_ref[...], hb)
        o_ref[XBC0:DT0, :] += _dot_tn(dxbc_ref[...], hb)
        o_ref[DT0:HQ0, :] += _dot_tn(ddt_ref[...], hb)[0:NH_SSD, :]
        o_ref[HQ0:HF0, :] += _dot_tn(dhq_ref[...], hb)
        o_ref[HF0:HI0, :] += _dot_tn(dhf_ref[...], hb)
        o_ref[HI0:HG0, :] += _dot_tn(dhi_ref[...], hb)
        o_ref[HG0:N_IN, :] += _dot_tn(dhg_ref[...], hb)

    return _call(body, (h0, dz, dxbc, ddt, dhq, dhf, dhi, dhg), name="gw_in", grid=(L // tl,),
                 in_specs=[_rows(tl, D), _rows(tl, D), _rows(tl, 1536), _rows(tl, 128), _rows(tl, D), _rows(tl, D),
                           _rows(tl, D), _rows(tl, D)],
                 out_specs=[_const((N_IN, D))], out_shape=[SDS((N_IN, D), F32)], phases=phases)


def _matmul_tn(a, b, name):
    L, M = a.shape
    N = b.shape[1]
    tl = min(512, L)

    def body(a_ref, b_ref, o_ref):
        _zero_first(o_ref)
        o_ref[...] += _dot_tn(a_ref[...], b_ref[...])

    return pl.pallas_call(
        body, grid=(L // tl,), name=name, in_specs=[_rows(tl, M), _rows(tl, N)], out_specs=_const((M, N)),
        out_shape=SDS((M, N), F32), compiler_params=_cparams())(a, b)


def _matmul_tn_pair(a0, a1, b, name):
    L, M = a0.shape
    N = b.shape[1]
    tl = min(512, L)

    def body(a0_ref, a1_ref, b_ref, o_ref):
        _zero_first(o_ref)
        bv = b_ref[...].astype(BF)
        o_ref[:M, :] += _dot_tn(a0_ref[...], bv)
        o_ref[M:, :] += _dot_tn(a1_ref[...], bv)

    return pl.pallas_call(
        body, grid=(L // tl,), name=name, in_specs=[_rows(tl, M), _rows(tl, M), _rows(tl, N)],
        out_specs=_const((2 * M, N)), out_shape=SDS((2 * M, N), F32), compiler_params=_cparams())(a0, a1, b)


def _head_expand():
    e = (jnp.right_shift(_iota((128, D), 1), 6) == _iota((128, D), 0)).astype(F32)
    et = (jnp.right_shift(_iota((D, 128), 0), 6) == _iota((D, 128), 1)).astype(F32)
    return e, et


def _conv_shifts(cur, other, up):
    rows = _iota((Q, 1), 0)
    out = []
    for s in (1, 2, 3):
        if up:
            out.append(jnp.where(rows >= Q - s, pltpu.roll(other, Q - s, 0), pltpu.roll(cur, Q - s, 0)))
        else:
            out.append(jnp.where(rows < s, pltpu.roll(other, s, 0), pltpu.roll(cur, s, 0)))
    return out


def _ssd_pre(u, dtr, dtb, alog):
    e, et = _head_expand()
    sgu = _sigmoid(u)
    xc = u * sgu
    lane = _iota((1, 128), 1)
    hmask = (lane < NH_SSD).astype(F32)
    pre = dtr + dtb
    dt = (jnp.maximum(pre, 0.0) + jnp.log(1.0 + jnp.exp(-jnp.abs(pre)))) * hmask
    a_row = -jnp.exp(alog)
    causal = _iota((Q, Q), 1) <= _iota((Q, Q), 0)
    tri = causal.astype(F32)
    acum = _dot_sel(tri, dt * a_row)
    acum_full = _sel_dot(acum, e)
    alast_full = acum_full[Q - 1:Q, :]
    dt_full = _sel_dot(dt, e)
    xs = xc[:, :D]
    return dict(e=e, et=et, sgu=sgu, xs=xs, bm=xc[:, D:D + 256], cm=xc[:, D + 256:], hmask=hmask, pre=pre, dt=dt,
                a_row=a_row, causal=causal, tri=tri, acum=acum, acum_t=acum.T, eA_full=jnp.exp(acum_full),
                dte_full=jnp.exp(alast_full - acum_full), dt_full=dt_full, xdt=xs * dt_full)


def _ssd_decay(pre, hh, cb):
    seg = pre["acum"][:, hh:hh + 1] - pre["acum_t"][hh:hh + 1, :]
    lm = jnp.where(pre["causal"], jnp.exp(jnp.minimum(seg, 0.0)), 0.0)
    return lm, cb * lm


def _ssd_fwd(xbc, dtr, z, conv_w, conv_b, dtb, alog, dskip_full, nw):
    L = xbc.shape[0]
    nc = L // Q

    def body(xbc_ref, dtr_ref, z_ref, cw_ref, cb_ref, dtb_ref, alog_ref, dsk_ref, nw_ref,
             ya_ref, y_ref, u_ref, st_ref, prev_ref, s_ref):
        @pl.when(pl.program_id(0) == 0)
        def _():
            prev_ref[...] = jnp.zeros_like(prev_ref)
            s_ref[...] = jnp.zeros_like(s_ref)

        xr = xbc_ref[...].astype(F32)
        sh = _conv_shifts(xr, prev_ref[...], up=False)
        u = cb_ref[...] + cw_ref[3:4, :] * xr + cw_ref[2:3, :] * sh[0] + cw_ref[1:2, :] * sh[1] + cw_ref[0:1, :] * sh[2]
        prev_ref[...] = xr
        ub = u.astype(BF)
        u_ref[...] = ub
        pre = _ssd_pre(ub.astype(F32), dtr_ref[...], dtb_ref[...], alog_ref[...])
        lo = _iota((1, 128), 1) < SSD_P
        s_old = s_ref[...]
        st_ref[0] = s_old
        ys = []
        for g in range(2):
            bg, cg = pre["bm"][:, 128 * g:128 * g + 128], pre["cm"][:, 128 * g:128 * g + 128]
            cb = _dot_nt(cg, bg)
            gs = slice(512 * g, 512 * g + 512)
            yd = []
            for j in range(4 * g, 4 * g + 4):
                xp = pre["xdt"][:, 128 * j:128 * j + 128].astype(BF)
                _, m0 = _ssd_decay(pre, 2 * j, cb)
                _, m1 = _ssd_decay(pre, 2 * j + 1, cb)
                yd.append(jnp.where(lo, _dot(m0, xp), _dot(m1, xp)))
            yoff = _dot_nt(cg, s_old[gs, :]) * pre["eA_full"][:, gs]
            ys.append(jnp.concatenate(yd, axis=1) + yoff)
            st = _dot_tn((pre["xdt"] * pre["dte_full"])[:, gs], bg)
            cdcol = jnp.exp(_dot_sel(pre["et"][gs, :], pre["acum_t"])[:, Q - 1:Q])
            s_ref[gs, :] = s_old[gs, :] * cdcol + st
        y = jnp.concatenate(ys, axis=1) + dsk_ref[...] * pre["xs"]
        yb = y.astype(BF)
        y_ref[...] = yb
        zf = z_ref[...].astype(F32)
        yz = yb.astype(F32) * zf * _sigmoid(zf)
        outs = []
        for g in range(2):
            gs = slice(512 * g, 512 * g + 512)
            o, _, _ = _rms(yz[:, gs], nw_ref[:, gs])
            outs.append(o)
        ya_ref[...] = jnp.concatenate(outs, axis=1).astype(BF)

    outs = [SDS((L, D), BF), SDS((L, D), BF), SDS((L, 1536), BF), SDS((nc, D, 128), F32)]
    return pl.pallas_call(
        body, grid=(nc,), name="ssd_fwd",
        in_specs=[_rows(Q, 1536), _rows(Q, 128), _rows(Q, D), _const((4, 1536)), _const((1, 1536)), _const((1, 128)),
                  _const((1, 128)), _const((1, D)), _const((1, D))],
        out_specs=[_rows(Q, D), _rows(Q, D), _rows(Q, 1536), pl.BlockSpec((1, D, 128), lambda i: (i, 0, 0))],
        out_shape=outs, scratch_shapes=[pltpu.VMEM((Q, 1536), F32), pltpu.VMEM((D, 128), F32)],
        compiler_params=_cparams())(xbc, dtr, z, conv_w, conv_b, dtb, alog, dskip_full, nw)


def _ssd_bwd(dya, y, z, u, xbc, dtr, states, conv_w, dtb, alog, dskip_full, nw):
    L = dya.shape[0]
    nc = L // Q

    def body(dya_ref, y_ref, z_ref, u_ref, xc_ref, xp_ref, dtr_ref, st_ref, cw_ref, dtb_ref, alog_ref, dsk_ref, nw_ref,
             dz_ref, dxbc_ref, ddt_ref, gconv_ref, ghead_ref, glane_ref, gs_ref, ndu_ref):
        step = pl.program_id(0)

        @pl.when(step == 0)
        def _():
            for r in (gconv_ref, ghead_ref, glane_ref, gs_ref, ndu_ref):
                r[...] = jnp.zeros_like(r)

        uf = u_ref[...].astype(F32)
        pre = _ssd_pre(uf, dtr_ref[...], dtb_ref[...], alog_ref[...])
        e, et, xs, xdt = pre["e"], pre["et"], pre["xs"], pre["xdt"]
        lane = _iota((1, 128), 1)
        lo = lane < SSD_P
        sub = _iota((128, 1), 0)
        zf = z_ref[...].astype(F32)
        sgz = _sigmoid(zf)
        sz = zf * sgz
        yv = y_ref[...].astype(F32)
        yz = yv * sz
        dyav = dya_ref[...].astype(F32)
        dyz, dnw = [], []
        for g in range(2):
            gs = slice(512 * g, 512 * g + 512)
            _, n, r = _rms(yz[:, gs], nw_ref[:, gs])
            dv, dw = _rms_bwd(dyav[:, gs], n, r, nw_ref[:, gs])
            dyz.append(dv)
            dnw.append(dw)
        dyz = jnp.concatenate(dyz, axis=1)
        glane_ref[1:2, :] += _colsum(jnp.concatenate(dnw, axis=1))
        dy = dyz * sz
        dz_ref[...] = (dyz * yv * sgz * (1.0 + zf * (1.0 - sgz))).astype(BF)
        glane_ref[0:1, :] += _colsum(dy * xs)
        dxs = dsk_ref[...] * dy

        s_in = st_ref[0]
        gst = gs_ref[...]
        gy = dy * pre["eA_full"]
        xdte = xdt * pre["dte_full"]
        dacum = jnp.zeros((Q, 128), F32)
        dacum_t = jnp.zeros((128, Q), F32)
        dxdt, dacum_full, ddte_full, dbs, dcs = [], [], [], [], []
        for g in range(2):
            gs = slice(512 * g, 512 * g + 512)
            bg, cg = pre["bm"][:, 128 * g:128 * g + 128], pre["cm"][:, 128 * g:128 * g + 128]
            sg_, dg_ = s_in[gs, :], gst[gs, :]
            yoff = _dot_nt(cg, sg_) * pre["eA_full"][:, gs]
            dc = _dot(gy[:, gs], sg_)
            dsin = _dot_tn(gy[:, gs], cg)
            dacum_full.append(dy[:, gs] * yoff)
            tg = _dot_nt(bg, dg_)
            ddte_full.append(tg * xdt[:, gs])
            db = _dot(xdte[:, gs], dg_)
            cb = _dot_nt(cg, bg)
            dcb = jnp.zeros((Q, Q), F32)
            dxg = []
            for j in range(4 * g, 4 * g + 4):
                xp = xdt[:, 128 * j:128 * j + 128].astype(BF)
                dyp = dy[:, 128 * j:128 * j + 128]
                dxp = jnp.zeros((Q, 128), F32)
                for idx in range(2):
                    hh = 2 * j + idx
                    lm, m = _ssd_decay(pre, hh, cb)
                    dym = jnp.where(lo if idx == 0 else jnp.logical_not(lo), dyp, 0.0).astype(BF)
                    dm = jnp.where(pre["causal"], _dot_nt(dym, xp), 0.0)
                    w = dm * m
                    dacum = dacum + jnp.where(lane == hh, jnp.sum(w, axis=1, keepdims=True), 0.0)
                    dacum_t = dacum_t + jnp.where(sub == hh, jnp.sum(w, axis=0, keepdims=True), 0.0)
                    dcb = dcb + dm * lm
                    dxp = dxp + _dot_tn(m, dym)
                dxg.append(dxp)
            dxdt.append(jnp.concatenate(dxg, axis=1) + tg * pre["dte_full"][:, gs])
            dcs.append(dc + _dot(dcb, bg))
            dbs.append(db + _dot_tn(dcb, cg))
            cdcol = jnp.exp(_dot_sel(et[gs, :], pre["acum_t"])[:, Q - 1:Q])
            gs_ref[gs, :] = dsin + dg_ * cdcol
        dxdt = jnp.concatenate(dxdt, axis=1)
        dacum = dacum + _sel_dot(jnp.concatenate(dacum_full, axis=1), et, 2) - dacum_t.T
        alast = pre["acum"][Q - 1:Q, :]
        dte = jnp.exp(alast - pre["acum"])
        ddte = _sel_dot(jnp.concatenate(ddte_full, axis=1), et, 2) * dte
        dacum = dacum - ddte
        dcd_col = jnp.sum(_dot_sel(e, gst * s_in, 2), axis=1, keepdims=True)
        dcd_row = jnp.broadcast_to(dcd_col, (128, 128)).T[0:1, :]
        dalast = _colsum(ddte) + dcd_row * jnp.exp(alast)
        dacum = dacum + jnp.where(_iota((Q, 1), 0) == Q - 1, dalast, 0.0)
        ddt = _sel_dot(dxdt * xs, et, 2)
        dxs = dxs + dxdt * pre["dt_full"]
        dda = _dot_sel(pre["tri"].T, dacum)
        ddt = ddt + dda * pre["a_row"]
        ghead_ref[1:2, :] += _colsum(dda * pre["dt"])
        ddtr = ddt * _sigmoid(pre["pre"]) * pre["hmask"]
        ghead_ref[0:1, :] += _colsum(ddtr)
        ddt_ref[...] = ddtr

        dxc = jnp.concatenate([dxs] + dbs + dcs, axis=1)
        sgu = pre["sgu"]
        du = dxc * sgu * (1.0 + uf * (1.0 - sgu))
        shu = _conv_shifts(du, ndu_ref[...], up=True)
        dxr = cw_ref[3:4, :] * du + cw_ref[2:3, :] * shu[0] + cw_ref[1:2, :] * shu[1] + cw_ref[0:1, :] * shu[2]
        ndu_ref[...] = du
        dxbc_ref[...] = dxr.astype(BF)
        xr = xc_ref[...].astype(F32)
        xprev = jnp.where(step == nc - 1, 0.0, xp_ref[...].astype(F32))
        shx = _conv_shifts(xr, xprev, up=False)
        gconv_ref[3:4, :] += _colsum(du * xr)
        gconv_ref[2:3, :] += _colsum(du * shx[0])
        gconv_ref[1:2, :] += _colsum(du * shx[1])
        gconv_ref[0:1, :] += _colsum(du * shx[2])
        gconv_ref[4:5, :] += _colsum(du)

        @pl.when(step == nc - 1)
        def _():
            ghead_ref[2:3, :] = ghead_ref[1:2, :] * pre["a_row"]
            ghead_ref[3:4, :] = _dot_hi(glane_ref[...], et)[0:1, :]

    rev = lambda i: (nc - 1 - i, 0)
    outs = [SDS((L, D), BF), SDS((L, 1536), BF), SDS((L, 128), F32), SDS((8, 1536), F32), SDS((8, 128), F32),
            SDS((8, D), F32)]
    return pl.pallas_call(
        body, grid=(nc,), name="ssd_bwd",
        in_specs=[pl.BlockSpec((Q, D), rev), pl.BlockSpec((Q, D), rev), pl.BlockSpec((Q, D), rev),
                  pl.BlockSpec((Q, 1536), rev), pl.BlockSpec((Q, 1536), rev),
                  pl.BlockSpec((Q, 1536), lambda i: (jnp.maximum(nc - 2 - i, 0), 0)),
                  pl.BlockSpec((Q, 128), rev), pl.BlockSpec((1, D, 128), lambda i: (nc - 1 - i, 0, 0)),
                  _const((4, 1536)), _const((1, 128)), _const((1, 128)), _const((1, D)), _const((1, D))],
        out_specs=[pl.BlockSpec((Q, D), rev), pl.BlockSpec((Q, 1536), rev), pl.BlockSpec((Q, 128), rev),
                   _const((8, 1536)), _const((8, 128)), _const((8, D))],
        out_shape=outs, scratch_shapes=[pltpu.VMEM((D, 128), F32), pltpu.VMEM((Q, 1536), F32)],
        compiler_params=_cparams())(dya, y, z, u, xbc, xbc, dtr, states, conv_w, dtb, alog, dskip_full, nw)


def _hg_gates(hq, hf, hlb):
    h0, h1 = hlb[0:1, :], hlb[1:2, :]
    mx = jnp.maximum(h0, h1)
    e0, e1 = jnp.exp(h0 - mx), jnp.exp(h1 - mx)
    lb = e0 / (e0 + e1)
    sg = _sigmoid(hf)
    fg = lb + (1.0 - lb) * sg
    tri = (_iota((Q, Q), 1) <= _iota((Q, Q), 0)).astype(F32)
    return hq * _sigmoid(hq), 1.0 - fg, fg, sg, lb, e1 / (e0 + e1), _dot_sel(tri, jnp.log(fg))


def _hg_intra(b, q, k):
    rowblk = jnp.right_shift(_iota((Q, 1), 0), SUB.bit_length() - 1)
    mids = [b[SUB * i + SUB // 2:SUB * i + SUB // 2 + 1, :] for i in range(NSUB)]
    prevs = [mids[0]] + [b[SUB * i - 1:SUB * i, :] for i in range(1, NSUB)]
    mfull = jnp.concatenate([jnp.broadcast_to(r, (SUB, 128)) for r in mids], axis=0)
    rfull = jnp.concatenate([jnp.broadcast_to(r, (SUB, 128)) for r in prevs], axis=0)
    eqd, ek, eqo = jnp.exp(b - mfull), jnp.exp(mfull - b), jnp.exp(b - rfull)
    qd, qo, khat = q * eqd, q * eqo, k * ek
    rtab = jnp.concatenate(prevs, axis=0)
    djs = [jnp.exp(rtab - mids[j]) for j in range(NSUB)]
    zero = jnp.zeros((SUB, 128), F32)
    cols = []
    for j in range(NSUB):
        pieces = []
        for i in range(NSUB):
            rs = slice(SUB * i, SUB * i + SUB)
            pieces.append(zero if i < j else qd[rs] if i == j else qo[rs] * djs[j][i:i + 1, :])
        cols.append(jnp.concatenate(pieces, axis=0))
    qt = jnp.concatenate(cols, axis=1).astype(BF)
    kt = jnp.concatenate([jnp.where(rowblk == j, khat, 0.0) for j in range(NSUB)], axis=1).astype(BF)
    causal = _iota((Q, Q), 1) <= _iota((Q, Q), 0)
    att = jnp.where(causal, _dot_nt(qt, kt), 0.0)
    return att, qt, kt, (eqd, ek, eqo, djs), causal


def _hg_intra_bwd(dqt, dkt, qt, kt, factors):
    eqd, ek, eqo, djs = factors
    dqd, dqo, dkh, db = [], [], [], []
    for i in range(NSUB):
        rs = slice(SUB * i, SUB * i + SUB)
        diag = slice(128 * i, 128 * i + 128)
        dqd.append(dqt[rs, diag])
        dkh.append(dkt[rs, diag])
        dbi = qt[rs, diag].astype(F32) * dqt[rs, diag] - kt[rs, diag].astype(F32) * dkt[rs, diag]
        acc = jnp.zeros((SUB, 128), F32)
        for j in range(i):
            bl = slice(128 * j, 128 * j + 128)
            acc = acc + dqt[rs, bl] * djs[j][i:i + 1, :]
            dbi = dbi + qt[rs, bl].astype(F32) * dqt[rs, bl]
        dqo.append(acc)
        db.append(dbi)
    cat = lambda t: jnp.concatenate(t, axis=0)
    return cat(dqd) * eqd + cat(dqo) * eqo, cat(dkh) * ek, cat(db)


def _hg_att_exact(b, q, k, b_ref, q_ref, att_t_ref):
    b_ref[...] = b
    q_ref[...] = q
    att_t_ref[...] = jnp.zeros((Q, Q), F32)
    rows, lane = _iota((Q, 1), 0), _iota((1, Q), 1)

    def step(i, carry):
        e = jnp.exp(jnp.minimum(b_ref[pl.ds(i, 1), :] - b, 0.0))
        col = jnp.sum(q_ref[pl.ds(i, 1), :] * k * e, axis=1, keepdims=True)
        att_t_ref[...] = jnp.where(lane == i, jnp.where(rows <= i, col, 0.0), att_t_ref[...])
        return carry

    lax.fori_loop(0, Q, step, 0)
    return att_t_ref[...].T


def _hg_att_exact_bwd(da, b, q, k, b_ref, q_ref, da_t_ref, dq_ref, dk_ref):
    b_ref[...] = b
    q_ref[...] = q
    da_t_ref[...] = da.T
    dk_ref[...] = jnp.zeros((Q, 128), F32)
    lane = _iota((1, Q), 1)

    def step(i, carry):
        e = jnp.exp(jnp.minimum(b_ref[pl.ds(i, 1), :] - b, 0.0))
        g = jnp.sum(jnp.where(lane == i, da_t_ref[...], 0.0), axis=1, keepdims=True) * e
        dq_ref[pl.ds(i, 1), :] = jnp.sum(g * k, axis=0, keepdims=True)
        dk_ref[...] += g * q_ref[pl.ds(i, 1), :]
        return carry

    lax.fori_loop(0, Q, step, 0)
    dq, dk = dq_ref[...], dk_ref[...]
    return dq, dk, q * dq - k * dk


def _hg_fwd(hq, hf, hi, hg, hlb, nw, fast):
    L = hq.shape[0]
    nc = L // Q

    def run(exact, step, hq_ref, hf_ref, hi_ref, hg_ref, hlb_ref, nw_ref, ob_ref, o_ref, st_ref, s_ref, *tmp):
        @pl.when(step == 0)
        def _():
            s_ref[...] = jnp.zeros_like(s_ref)

        qf, kf, _, _, _, _, bcum = _hg_gates(hq_ref[...].astype(F32), hf_ref[...], hlb_ref[...])
        gate = hg_ref[...].astype(F32)
        heads = [slice(128 * h, 128 * h + 128) for h in range(NH_HG)]
        if exact:
            atts = [_hg_att_exact(bcum[:, sl], qf[:, sl], kf[:, sl], *tmp).astype(BF) for sl in heads]
        else:
            atts = [_hg_intra(bcum[:, sl], qf[:, sl], kf[:, sl])[0].astype(BF) for sl in heads]
        olds = [s_ref[sl, :] for sl in heads]
        outs_ = [_dot(att, hi_ref[:, sl]) + _dot(qf[:, sl] * jnp.exp(bcum[:, sl]), s)
                 for att, sl, s in zip(atts, heads, olds)]
        for sl, s, o in zip(heads, olds, outs_):
            b, k = bcum[:, sl], kf[:, sl]
            st_ref[0, sl, :] = s
            blast = b[Q - 1:Q, :]
            s_ref[sl, :] = s * jnp.exp(b.T[:, Q - 1:Q]) + _dot_tn(k * jnp.exp(blast - b), hi_ref[:, sl])
            ob = o.astype(BF)
            o_ref[:, sl] = ob
            on, _, _ = _rms(ob.astype(F32), nw_ref[...])
            gt = gate[:, sl]
            ob_ref[:, sl] = (on * gt * _sigmoid(gt)).astype(BF)

    def body(fast_ref, *refs):
        step = pl.program_id(0)
        pl.when(fast_ref[0] == 1)(lambda: run(False, step, *refs))
        pl.when(fast_ref[0] != 1)(lambda: run(True, step, *refs))

    rows = pl.BlockSpec((Q, D), lambda i, f: (i, 0))
    outs = [SDS((L, D), BF), SDS((L, D), BF), SDS((nc, D, 128), F32)]
    grid_spec = pltpu.PrefetchScalarGridSpec(
        num_scalar_prefetch=1, grid=(nc,),
        in_specs=[rows] * 4 + [pl.BlockSpec((2, D), lambda i, f: (0, 0)), pl.BlockSpec((1, 128), lambda i, f: (0, 0))],
        out_specs=[rows, rows, pl.BlockSpec((1, D, 128), lambda i, f: (i, 0, 0))],
        scratch_shapes=[pltpu.VMEM((D, 128), F32), pltpu.VMEM((Q, 128), F32), pltpu.VMEM((Q, 128), F32),
                        pltpu.VMEM((Q, Q), F32)])
    return pl.pallas_call(body, grid_spec=grid_spec, name="hg_fwd", out_shape=outs,
                          compiler_params=_cparams())(fast, hq, hf, hi, hg, hlb, nw)


def _hg_bwd(dob, o, hq, hf, hi, hg, states, hlb, nw, fast, phases=()):
    L = dob.shape[0]
    nc = L // Q

    def run(exact, step, dob_ref, o_ref, hq_ref, hf_ref, hi_ref, hg_ref, st_ref, hlb_ref, nw_ref,
            dhq_ref, dhf_ref, dhi_ref, dhg_ref, acc_ref, gs_ref, *tmp):
        @pl.when(step == 0)
        def _():
            acc_ref[...] = jnp.zeros_like(acc_ref)
            gs_ref[...] = jnp.zeros_like(gs_ref)

        hqv = hq_ref[...].astype(F32)
        qf, kf, fg, sg, lb, sm1, bcum = _hg_gates(hqv, hf_ref[...], hlb_ref[...])
        gate = hg_ref[...].astype(F32)
        sgg = _sigmoid(gate)
        nwv = nw_ref[...]
        tri_t = (_iota((Q, Q), 1) >= _iota((Q, Q), 0)).astype(F32)
        ones8 = jnp.ones((8, 128), BF)
        heads = [slice(128 * h, 128 * h + 128) for h in range(NH_HG)]
        row_last = _iota((Q, 1), 0) == Q - 1
        dobs, dnws = [], []
        for sl in heads:
            gt, sgt = gate[:, sl], sgg[:, sl]
            _, n, r = _rms(o_ref[:, sl].astype(F32), nwv)
            dobv = dob_ref[:, sl].astype(F32)
            dhg_ref[:, sl] = (dobv * n * nwv * sgt * (1.0 + gt * (1.0 - sgt))).astype(BF)
            do, dw = _rms_bwd(dobv * gt * sgt, n, r, nwv)
            dnws.append(_colsum(dw))
            dobs.append(do.astype(BF))
        causal = _iota((Q, Q), 1) <= _iota((Q, Q), 0)
        if exact:
            intra = [(_hg_att_exact(bcum[:, sl], qf[:, sl], kf[:, sl], *tmp[:3]),) for sl in heads]
        else:
            intra = [_hg_intra(bcum[:, sl], qf[:, sl], kf[:, sl]) for sl in heads]
        states = [(st_ref[0, sl, :], gs_ref[sl, :]) for sl in heads]
        das = [jnp.where(causal, _dot_nt(dob_h, hi_ref[:, sl]), 0.0) for dob_h, sl in zip(dobs, heads)]
        dqhats = [_dot_nt(dob_h, s) for dob_h, (s, _) in zip(dobs, states)]
        dkhats = [_dot_nt(hi_ref[:, sl], gst) for sl, (_, gst) in zip(heads, states)]
        if not exact:
            dqts = [jnp.dot(da.astype(BF), it[2], preferred_element_type=F32) for da, it in zip(das, intra)]
            dkts = [lax.dot_general(da.astype(BF), it[1], (((0,), (0,)), ((), ())), preferred_element_type=F32)
                    for da, it in zip(das, intra)]
        dqs, dks, dgls = [], [], []
        for h, sl in enumerate(heads):
            b, q, k = bcum[:, sl], qf[:, sl], kf[:, sl]
            att = intra[h][0]
            s, gst = states[h]
            dob_h, dqhat, dkhat = dobs[h], dqhats[h], dkhats[h]
            eb = jnp.exp(b)
            blast = b[Q - 1:Q, :]
            ekl = jnp.exp(blast - b)
            qhat, khat = q * eb, k * ekl
            dhi_ref[:, sl] = (_dot_tn(att, dob_h) + _dot(khat, gst)).astype(BF)
            if exact:
                dq_i, dk_i, db = _hg_att_exact_bwd(das[h], b, q, k, *tmp)
            else:
                dq_i, dk_i, db = _hg_intra_bwd(dqts[h], dkts[h], *intra[h][1:4])
            dqs.append(dq_i + dqhat * eb)
            dks.append(dk_i + dkhat * ekl)
            qhat_r, khat_r = qhat.astype(BF).astype(F32), khat.astype(BF).astype(F32)
            decay_row = sum(_dot_nt(ones8, part) for part in _split(gst * s, 2))[0:1, :]
            dblast = _colsum(dkhat * khat_r) + decay_row * jnp.exp(blast)
            dgls.append(db + qhat_r * dqhat - khat_r * dkhat + jnp.where(row_last, dblast, 0.0))
            gs_ref[sl, :] = _dot_tn(qhat, dob_h) + gst * jnp.exp(b.T[:, Q - 1:Q])
        dq, dk, db = (jnp.concatenate(t, axis=1) for t in (dqs, dks, dgls))
        dgl = _dot_sel(tri_t, db, 2)
        sgq = _sigmoid(hqv)
        dhq_ref[...] = (dq * sgq * (1.0 + hqv * (1.0 - sgq))).astype(BF)
        dfg = dgl / fg - dk
        dhf_ref[...] = (dfg * (1.0 - lb) * sg * (1.0 - sg)).astype(BF)
        acc_ref[0:1, :] += _colsum(dfg * (1.0 - sg))
        acc_ref[1:2, :] += jnp.concatenate(dnws, axis=1)

        @pl.when(step == nc - 1)
        def _():
            dlb = acc_ref[0:1, :] * lb * sm1
            acc_ref[2:3, :] = dlb
            acc_ref[3:4, :] = -dlb
            tot = acc_ref[1:2, 0:128]
            for h in range(1, NH_HG):
                tot = tot + acc_ref[1:2, 128 * h:128 * h + 128]
            acc_ref[4:5, 0:128] = tot

    def body(fast_ref, *refs):
        step = pl.program_id(0)
        pl.when(fast_ref[0] == 1)(lambda: run(False, step, *refs))
        pl.when(fast_ref[0] != 1)(lambda: run(True, step, *refs))

    rev = pl.BlockSpec((Q, D), lambda i, f: (nc - 1 - i, 0))
    outs = [SDS((L, D), BF)] * 4 + [SDS((8, D), F32)]
    return _call(
        body, (fast, dob, o, hq, hf, hi, hg, states, hlb, nw), name="hg_bwd", grid=(nc,), prefetch=1,
        in_specs=[rev] * 6 + [pl.BlockSpec((1, D, 128), lambda i, f: (nc - 1 - i, 0, 0)),
                              pl.BlockSpec((2, D), lambda i, f: (0, 0)), pl.BlockSpec((1, 128), lambda i, f: (0, 0))],
        out_specs=[rev] * 4 + [pl.BlockSpec((8, D), lambda i, f: (0, 0))], out_shape=outs,
        scratch_shapes=[pltpu.VMEM((D, 128), F32), pltpu.VMEM((Q, 128), F32), pltpu.VMEM((Q, 128), F32),
                        pltpu.VMEM((Q, Q), F32), pltpu.VMEM((Q, 128), F32), pltpu.VMEM((Q, 128), F32)], phases=phases)


def _place():
    return lax.axis_index("x"), lax.axis_index("y"), lax.axis_index("c")


def _phase_io(phase):
    kind, arrays, halves = phase
    n = len(arrays)
    dma = pltpu.SemaphoreType.DMA
    if kind == "gather":
        outs = [SDS((8,) + a.shape if hc is None else (4,) + a.shape, a.dtype) for a, hc in zip(arrays, halves)]
        return outs, [dma((7 * n,)), dma((7 * n,)), dma((n,))], {}
    if kind == "sibling":
        return [SDS((4, g.shape[1], hc), g.dtype) for g, hc in zip(arrays, halves)], [dma((n,)), dma((n,))], {}
    if kind == "chips":
        return [SDS((3,) + p.shape[1:], p.dtype) for p in arrays], [dma((3 * n,)), dma((3 * n,))], {}
    assert kind == "swap"
    return [SDS(b.shape, b.dtype) for b in arrays], [dma((n,)), dma((n,))], {a: a for a in range(n)}


def _gather_events(ins, outs, sems, halves):
    send_sems, recv_sems, local_sems = sems
    n = len(ins)

    def parts(a):
        x, y, c = _place()
        hc = halves[a]
        me, sibling = (x, y, c), (x, y, 1 - c)
        chips = [(1 - x, y), (x, 1 - y), (1 - x, 1 - y)]

        def slot(p):
            if hc is None:
                return outs[a].at[4 * p[0] + 2 * p[1] + p[2]]
            return outs[a].at[2 * p[0] + p[1], :, pl.ds(p[2] * hc, hc)]

        own = ins[a] if hc is None else ins[a].at[:, pl.ds(c * hc, hc)]

        def copy(k, piece, to, src=None):
            return pltpu.make_async_remote_copy(
                src_ref=slot(piece) if src is None else src, dst_ref=slot(piece),
                send_sem=send_sems.at[7 * a + k], recv_sem=recv_sems.at[7 * a + k], device_id=to, device_id_type=MESH)

        return dict(
            mine=lambda: pltpu.make_async_copy(own, slot(me), local_sems.at[a]),
            starts=lambda: [copy(0, me, sibling, src=own)] + [copy(1 + j, me, (*chip, c), src=own)
                                                               for j, chip in enumerate(chips)],
            arrive=lambda: [copy(1 + j, (*chip, c), me) for j, chip in enumerate(chips)],
            passed=lambda: [copy(4 + j, (*chip, c), sibling) for j, chip in enumerate(chips)],
            from_sibling=lambda: [copy(0, sibling, me)] + [copy(4 + j, (*chip, 1 - c), me)
                                                            for j, chip in enumerate(chips)])

    def first():
        for a in range(n):
            p = parts(a)
            p["mine"]().start()
            for cp in p["starts"]():
                cp.start()

    def mid():
        for a in range(n):
            p = parts(a)
            for cp_in, cp_out in zip(p["arrive"](), p["passed"]()):
                cp_in.wait_recv()
                cp_out.start()

    def last():
        for a in range(n):
            p = parts(a)
            for cp in p["from_sibling"]():
                cp.wait_recv()
            for cp in p["starts"]() + p["passed"]():
                cp.wait_send()
            p["mine"]().wait()

    return dict(first=first, mid=mid, last=last)


def _exchange_events(kind, ins, outs, sems, halves):
    send_sems, recv_sems = sems
    n = len(outs)

    def copies():
        x, y, c = _place()
        if kind == "sibling":
            return [pltpu.make_async_remote_copy(
                src_ref=ins[a].at[:, :, pl.ds((1 - c) * halves[a], halves[a])], dst_ref=outs[a],
                send_sem=send_sems.at[a], recv_sem=recv_sems.at[a], device_id=(x, y, 1 - c), device_id_type=MESH)
                for a in range(n)]
        chips = [(1 - x, y), (x, 1 - y), (1 - x, 1 - y)]
        return [pltpu.make_async_remote_copy(
            src_ref=ins[a].at[2 * px + py], dst_ref=outs[a].at[k], send_sem=send_sems.at[3 * a + k],
            recv_sem=recv_sems.at[3 * a + k], device_id=(px, py, c), device_id_type=MESH)
            for a in range(n) for k, (px, py) in enumerate(chips)]

    def first():
        for cp in copies():
            cp.start()

    def last():
        for cp in copies():
            cp.wait()

    return dict(first=first, last=last)


def _swap_events(outs, sems, halves):
    send_sems, recv_sems = sems
    n = len(outs)

    def copy(a, landing):
        x, y, c = _place()
        cols = lambda which: outs[a].at[:, pl.ds(which * halves[a], halves[a])]
        return pltpu.make_async_remote_copy(
            src_ref=cols(c), dst_ref=cols(1 - c) if landing else cols(c), send_sem=send_sems.at[a],
            recv_sem=recv_sems.at[a], device_id=(x, y, 1 - c), device_id_type=MESH)

    def first():
        for a in range(n):
            copy(a, False).start()

    def last():
        for a in range(n):
            copy(a, True).wait_recv()
        for a in range(n):
            copy(a, False).wait_send()

    return dict(first=first, last=last)


def _phase_events(phase, ins, outs, sems):
    kind, _, halves = phase
    if kind == "gather":
        return _gather_events(ins, outs, sems, halves)
    if kind == "swap":
        return _swap_events(outs, sems, halves)
    return _exchange_events(kind, ins, outs, sems, halves)


def _split_refs(refs, counts):
    out, at = [], 0
    for c in counts:
        out.append(list(refs[at:at + c]))
        at += c
    return out


def _comm_plumbing(phases, first_in, first_out):
    ios = [_phase_io(p) for p in phases]
    arrays = [a for p in phases for a in p[1]]
    out_shape = [o for io in ios for o in io[0]]
    sem_shapes = [s for io in ios for s in io[1]]
    aliases, ai, ao = {}, first_in, first_out
    for p, io in zip(phases, ios):
        aliases.update({ai + k: ao + v for k, v in io[2].items()})
        ai, ao = ai + len(p[1]), ao + len(io[0])

    def events(cins, couts, sems):
        evs = [_phase_events(p, i, o, s) for p, i, o, s in zip(
            phases, _split_refs(cins, [len(p[1]) for p in phases]), _split_refs(couts, [len(io[0]) for io in ios]),
            _split_refs(sems, [len(io[1]) for io in ios]))]

        def run(key):
            for ev in evs:
                if key in ev:
                    ev[key]()

        return {key: (lambda key=key: run(key)) for key in ("first", "mid", "last")}

    def regroup(flat):
        return _split_refs(flat, [len(io[0]) for io in ios])

    return arrays, out_shape, sem_shapes, aliases, events, regroup


def _run_phases(phases, name):
    arrays, out_shape, sem_shapes, aliases, events, regroup = _comm_plumbing(phases, 0, 0)

    def body(*refs):
        cins, couts, sems = _split_refs(refs, [len(arrays), len(out_shape), len(sem_shapes)])
        ev = events(cins, couts, sems)
        for key in ("first", "mid", "last"):
            ev[key]()

    outs = pl.pallas_call(
        body, name=name, in_specs=[ANY] * len(arrays), out_specs=[ANY] * len(out_shape), out_shape=out_shape,
        scratch_shapes=sem_shapes, input_output_aliases=aliases)(*arrays)
    return regroup(outs)


def _call(body, args, *, name, grid, in_specs, out_specs, out_shape, scratch_shapes=(), prefetch=0, phases=(),
          mid_step=None):
    steps = grid[0]
    arrays, c_shape, sem_shapes, aliases, events, regroup = _comm_plumbing(
        phases, prefetch + len(in_specs), len(out_specs))
    counts = [prefetch, len(in_specs), len(arrays), len(out_specs), len(c_shape), len(scratch_shapes), len(sem_shapes)]

    def wrapped(*refs):
        pre, ins, cins, outs, couts, scratch, sems = _split_refs(refs, counts)
        if not phases:
            return body(*pre, *ins, *outs, *scratch)
        step = pl.program_id(0)
        ev = events(cins, couts, sems)
        pl.when(step == 0)(ev["first"])
        body(*pre, *ins, *outs, *scratch)
        pl.when(step == (steps // 2 if mid_step is None else mid_step))(ev["mid"])
        pl.when(step == steps - 1)(ev["last"])

    grid_spec = pltpu.PrefetchScalarGridSpec(
        num_scalar_prefetch=prefetch, grid=grid, in_specs=list(in_specs) + [ANY] * len(arrays),
        out_specs=list(out_specs) + [ANY] * len(c_shape), scratch_shapes=list(scratch_shapes) + sem_shapes)
    outs = pl.pallas_call(
        wrapped, grid_spec=grid_spec, name=name, out_shape=list(out_shape) + c_shape, input_output_aliases=aliases,
        compiler_params=_cparams())(*args, *arrays)
    return list(outs[:len(out_specs)]), regroup(outs[len(out_specs):])


def _tile(rows, cols, nbuf):
    budget = (VMEM_LIMIT // 3) // (2 * nbuf * 4)
    if rows % 8 == 0:
        cands = [t for t in range(8, rows + 1, 8) if rows % t == 0 and t * cols <= budget]
        pref = [t for t in cands if t % 16 == 0]
        return (max(pref) if pref else max(cands) if cands else 8), cols
    cands = [t for t in range(128, cols + 1, 128) if cols % t == 0 and rows * t <= budget]
    return rows, (max(cands) if cands else 128)


def _chip_sum(g, from_sib, place, name):
    _, rows, hc = from_sib.shape
    tr, tc = _tile(rows, hc, 4)
    ni, nj = rows // tr, hc // tc

    def body(p_ref, g_ref, s_ref, hb_ref, own_ref):
        s = g_ref[...] + s_ref[...]
        hb_ref[...] = s.astype(BF)

        @pl.when(pl.program_id(2) == p_ref[1])
        def _():
            own_ref[...] = s

    grid_spec = pltpu.PrefetchScalarGridSpec(
        num_scalar_prefetch=1, grid=(ni, nj, 4),
        in_specs=[pl.BlockSpec((None, tr, tc), lambda i, j, k, p: (k, i, p[0] * nj + j)),
                  pl.BlockSpec((None, tr, tc), lambda i, j, k, p: (k, i, j))],
        out_specs=[pl.BlockSpec((None, tr, tc), lambda i, j, k, p: (k, i, j)),
                   pl.BlockSpec((tr, tc), lambda i, j, k, p: (i, j))])
    return pl.pallas_call(
        body, grid_spec=grid_spec, name=name, out_shape=[SDS((4, rows, hc), BF), SDS((rows, hc), F32)],
        compiler_params=pltpu.CompilerParams(dimension_semantics=("arbitrary",) * 3,
                                             vmem_limit_bytes=VMEM_LIMIT))(place, g, from_sib)


def _total(own, parts, place, name):
    rows, hc = own.shape
    tr, tc = _tile(rows, hc, 5)
    ni, nj = rows // tr, hc // tc

    def body(p_ref, own_ref, parts_ref, o_ref):
        s = own_ref[...]
        for k in range(3):
            s = s + parts_ref[k].astype(F32)
        o_ref[...] = s

    grid_spec = pltpu.PrefetchScalarGridSpec(
        num_scalar_prefetch=1, grid=(ni, nj),
        in_specs=[pl.BlockSpec((tr, tc), lambda i, j, p: (i, j)),
                  pl.BlockSpec((3, tr, tc), lambda i, j, p: (0, i, j))],
        out_specs=pl.BlockSpec((tr, tc), lambda i, j, p: (i, p[0] * nj + j)))
    return pl.pallas_call(
        body, grid_spec=grid_spec, name=name, out_shape=SDS((rows, 2 * hc), F32),
        compiler_params=pltpu.CompilerParams(dimension_semantics=("arbitrary",) * 2,
                                             vmem_limit_bytes=VMEM_LIMIT))(place, own, parts)


def _sum8(parts, name):
    R = parts.shape[1]

    def body(p_ref, o_ref):
        s = p_ref[0]
        for k in range(1, 8):
            s = s + p_ref[k]
        o_ref[...] = s

    return pl.pallas_call(
        body, grid=(1,), name=name, in_specs=[_const((8, R, 128))], out_specs=_const((R, 128)),
        out_shape=SDS((R, 128), F32), compiler_params=_cparams())(parts)


def _adamw(w, g, m, v, name):
    _, R, C = w.shape
    tr, tc = _tile(R, C, 7)
    c1 = 1.0 / (1.0 - ADAM_B1 ** ADAM_STEP)
    c2 = 1.0 / (1.0 - ADAM_B2 ** ADAM_STEP)

    def body(w_ref, g_ref, m_ref, v_ref, d_ref, nm_ref, nv_ref):
        gv = g_ref[...]
        nm = ADAM_B1 * m_ref[...] + (1.0 - ADAM_B1) * gv
        nv = ADAM_B2 * v_ref[...] + (1.0 - ADAM_B2) * gv * gv
        nm_ref[...] = nm
        nv_ref[...] = nv
        d_ref[...] = -ADAM_LR * ((nm * c1) / (jnp.sqrt(nv * c2) + ADAM_EPS) + ADAM_WD * w_ref[...])

    blk3 = pl.BlockSpec((None, tr, tc), lambda i, j: (0, i, j))
    return pl.pallas_call(
        body, grid=(R // tr, C // tc), name=name,
        in_specs=[blk3, pl.BlockSpec((tr, tc), lambda i, j: (i, j)), blk3, blk3], out_specs=[blk3] * 3,
        out_shape=[SDS((1, R, C), F32)] * 3,
        compiler_params=pltpu.CompilerParams(dimension_semantics=("arbitrary",) * 2,
                                             vmem_limit_bytes=VMEM_LIMIT))(w, g, m, v)


def _pack_small(parts):
    rows = []
    for p in parts:
        p = p.reshape(-1)
        rows.append(jnp.pad(p, (0, (-p.shape[0]) % 128)).reshape(-1, 128))
    out = jnp.concatenate(rows, axis=0)
    return jnp.pad(out, ((0, (-out.shape[0]) % 8), (0, 0)))


def _unpack_small(packed, shapes):
    out, row = [], 0
    for shp in shapes:
        n = 1
        for s in shp:
            n *= s
        nr = -(-n // 128)
        out.append(packed[row:row + nr].reshape(-1)[:n].reshape(shp))
        row += nr
    return out


def _pad_lanes(v, n=128):
    return jnp.pad(v, ((0, 0), (0, n - v.shape[1])))


GROUP_FFN = ("ffn_w_gate", "ffn_w_up", "ffn_w_down")
GROUP_ATTN = ("w_out", "xa_wq", "xa_wkv", "xa_wo")


def kernel(x, mem, norm_mix_w, w_in, conv_w, conv_b, dt_bias, a_log, d_skip, ssd_norm_w, hg_lower_bounds, hg_norm_w, w_out, norm_xa_w, norm_mem_w, xa_wq, xa_wkv, xa_wo, norm_ffn_w, ffn_w_gate, ffn_w_up, ffn_w_down, norm_final_w, loss_target, m_norm_mix_w, m_w_in, m_conv_w, m_conv_b, m_dt_bias, m_a_log, m_d_skip, m_ssd_norm_w, m_hg_lower_bounds, m_hg_norm_w, m_w_out, m_norm_xa_w, m_norm_mem_w, m_xa_wq, m_xa_wkv, m_xa_wo, m_norm_ffn_w, m_ffn_w_gate, m_ffn_w_up, m_ffn_w_down, m_norm_final_w, v_norm_mix_w, v_w_in, v_conv_w, v_conv_b, v_dt_bias, v_a_log, v_d_skip, v_ssd_norm_w, v_hg_lower_bounds, v_hg_norm_w, v_w_out, v_norm_xa_w, v_norm_mem_w, v_xa_wq, v_xa_wkv, v_xa_wo, v_norm_ffn_w, v_ffn_w_gate, v_ffn_w_up, v_ffn_w_down, v_norm_final_w):
    w = dict(norm_mix_w=norm_mix_w, w_in=w_in, conv_w=conv_w, conv_b=conv_b, dt_bias=dt_bias, a_log=a_log, d_skip=d_skip,
             ssd_norm_w=ssd_norm_w, hg_lower_bounds=hg_lower_bounds, hg_norm_w=hg_norm_w, w_out=w_out,
             norm_xa_w=norm_xa_w, norm_mem_w=norm_mem_w, xa_wq=xa_wq, xa_wkv=xa_wkv, xa_wo=xa_wo, norm_ffn_w=norm_ffn_w,
             ffn_w_gate=ffn_w_gate, ffn_w_up=ffn_w_up, ffn_w_down=ffn_w_down, norm_final_w=norm_final_w)
    m = dict(norm_mix_w=m_norm_mix_w, w_in=m_w_in, conv_w=m_conv_w, conv_b=m_conv_b, dt_bias=m_dt_bias, a_log=m_a_log,
             d_skip=m_d_skip, ssd_norm_w=m_ssd_norm_w, hg_lower_bounds=m_hg_lower_bounds, hg_norm_w=m_hg_norm_w,
             w_out=m_w_out, norm_xa_w=m_norm_xa_w, norm_mem_w=m_norm_mem_w, xa_wq=m_xa_wq, xa_wkv=m_xa_wkv,
             xa_wo=m_xa_wo, norm_ffn_w=m_norm_ffn_w, ffn_w_gate=m_ffn_w_gate, ffn_w_up=m_ffn_w_up,
             ffn_w_down=m_ffn_w_down, norm_final_w=m_norm_final_w)
    v = dict(norm_mix_w=v_norm_mix_w, w_in=v_w_in, conv_w=v_conv_w, conv_b=v_conv_b, dt_bias=v_dt_bias, a_log=v_a_log,
             d_skip=v_d_skip, ssd_norm_w=v_ssd_norm_w, hg_lower_bounds=v_hg_lower_bounds, hg_norm_w=v_hg_norm_w,
             w_out=v_w_out, norm_xa_w=v_norm_xa_w, norm_mem_w=v_norm_mem_w, xa_wq=v_xa_wq, xa_wkv=v_xa_wkv,
             xa_wo=v_xa_wo, norm_ffn_w=v_norm_ffn_w, ffn_w_gate=v_ffn_w_gate, ffn_w_up=v_ffn_w_up,
             ffn_w_down=v_ffn_w_down, norm_final_w=v_norm_final_w)
    xi, yi, ci = _place()
    chip = 2 * xi + yi
    place = jnp.stack([ci, chip]).astype(jnp.int32)

    def shard(t, name):
        return jnp.swapaxes(t[name], 1, 2) if name in TRANSPOSED else t[name]

    wsh = {name: shard(w, name) for name in BIG}
    half = {name: wsh[name].shape[2] // 2 for name in BIG}
    payload = {name: wsh[name][0].astype(BF) for name in BIG}
    ws = {name: w[name] for name in SMALL}
    xs, mems, tgt = x[0], mem[0], loss_target[0]

    def chip_sums(names, grads, from_sib):
        return [_chip_sum(grads[n], s, place, "grads_chip_sum_" + n) for n, s in zip(names, from_sib)]

    def totals(names, sums, others):
        return [_total(own, o, place, "grads_total_" + n) for n, (_, own), o in zip(names, sums, others)]

    ((w_in4, conv_all),) = _run_phases([("gather", [payload["w_in"], conv_w[0]], [half["w_in"], None])], "gather_w_in")
    w_in_t = w_in4.reshape(N_IN, D)
    ws["conv_w"] = conv_all[0::2].transpose(1, 0, 2).reshape(1, 4, 1536)
    rest = [n for n in BIG if n != "w_in"]
    (h0, z, xbc, hq, hf, hi, hg, dtr), (gathered,) = _in_proj(
        xs, ws["norm_mix_w"], w_in_t, phases=[("gather", [payload[n] for n in rest], [half[n] for n in rest])])
    wg = dict(zip(rest, gathered))
    wg_t, wu_t = wg["ffn_w_gate"].reshape(FFN, D), wg["ffn_w_up"].reshape(FFN, D)
    wd = wg["ffn_w_down"].reshape(FFN, D)
    w_out_f = wg["w_out"].reshape(2 * D, D)
    wq, wo = wg["xa_wq"].reshape(D, D), wg["xa_wo"].reshape(D, D)
    dtb, alog = _pad_lanes(ws["dt_bias"]), _pad_lanes(ws["a_log"])
    dskip_full = jnp.repeat(ws["d_skip"], SSD_P, axis=1)
    cw, conv_bias = ws["conv_w"][0], ws["conv_b"]
    hlb = ws["hg_lower_bounds"]
    hg_fast = (jnp.min(jax.nn.softmax(hlb, axis=0)[0]) >= HG_LB_FLOOR).astype(jnp.int32).reshape(1)

    ya, yssd, u, st_ssd = _ssd_fwd(xbc, dtr, z, cw, conv_bias, dtb, alog, dskip_full, ws["ssd_norm_w"])
    ob, ohg, st_hg = _hg_fwd(hq, hf, hi, hg, hlb, ws["hg_norm_w"], hg_fast)
    kmem, vmem = _mem_kv(mems, ws["norm_mem_w"], wg["xa_wkv"])
    x1, x2, hxa, q, ox = _attn_fwd(xs, ya, ob, w_out_f, ws["norm_xa_w"], wq, kmem, vmem, wo)
    nfin = ws["norm_final_w"].reshape(1, D)
    dx2, hffn, act, dx3, dg, du, acc_f = _ffn_loss(x2, tgt, ws["norm_ffn_w"], nfin, wg_t, wu_t, wd)

    gb = {"ffn_w_gate": _matmul_tn(dg, hffn, "gw_gate").reshape(4, FFN // 4, D),
          "ffn_w_up": _matmul_tn(du, hffn, "gw_up").reshape(4, FFN // 4, D),
          "ffn_w_down": _matmul_tn(act, dx3, "gw_down").reshape(4, FFN // 4, D)}
    (dx1, dya, dob, dq, dk, dv, acc_a), (sib_ffn,) = _attn_bwd(
        dx2, x1, q, kmem, vmem, ws["norm_xa_w"], wq, wo, w_out_f,
        phases=[("sibling", [gb[n] for n in GROUP_FFN], [half[n] for n in GROUP_FFN])])
    sums_ffn = chip_sums(GROUP_FFN, gb, sib_ffn)
    g_nmem, gb["xa_wkv"] = _mem_kv_bwd(mems, ws["norm_mem_w"], wg["xa_wkv"], dk, dv)
    gb["w_out"] = _matmul_tn_pair(ya, ob, dx1, "gw_out").reshape(4, D // 2, D)
    gb["xa_wq"] = _matmul_tn(hxa, dq, "gw_q").reshape(4, D // 4, D)
    gb["xa_wo"] = _matmul_tn(ox, dx2, "gw_o").reshape(4, D // 4, D)
    (dhq, dhf, dhi, dhg, acc_h), (others_ffn, sib_attn) = _hg_bwd(
        dob, ohg, hq, hf, hi, hg, st_hg, hlb, ws["hg_norm_w"], hg_fast,
        phases=[("chips", [hb for hb, _ in sums_ffn], None),
                ("sibling", [gb[n] for n in GROUP_ATTN], [half[n] for n in GROUP_ATTN])])
    red_ffn = totals(GROUP_FFN, sums_ffn, others_ffn)
    sums_attn = chip_sums(GROUP_ATTN, gb, sib_attn)
    dz, dxbc, ddt, gconv, ghead, glane = _ssd_bwd(dya, yssd, z, u, xbc, dtr, st_ssd, cw, dtb, alog, dskip_full,
                                                  ws["ssd_norm_w"])
    gx, acc_i = _in_proj_bwd(xs, dx1, dz, dxbc, dhq, dhf, dhi, dhg, ddt, ws["norm_mix_w"], w_in_t)
    (gw_in_t,), (g_ffn, others_attn) = _gw_in(
        h0, dz, dxbc, ddt, dhq, dhf, dhi, dhg,
        phases=[("swap", red_ffn, [half[n] for n in GROUP_FFN]), ("chips", [hb for hb, _ in sums_attn], None)])
    red_attn = totals(GROUP_ATTN, sums_attn, others_attn)
    gb["w_in"] = gw_in_t.reshape(4, N_IN // 4, D)
    g_attn, (sib_in,) = _run_phases([("swap", red_attn, [half[n] for n in GROUP_ATTN]),
                                     ("sibling", [gb["w_in"]], [half["w_in"]])], "grads_w_in_to_sibling")
    sums_in = chip_sums(("w_in",), gb, [sib_in])
    ((others_in,),) = _run_phases([("chips", [sums_in[0][0]], None)], "grads_w_in_to_chips")
    red_in = totals(("w_in",), sums_in, [others_in])

    gs = {
        "norm_mix_w": acc_i[0:1], "conv_w": gconv[0:4][None], "conv_b": gconv[4:5],
        "dt_bias": ghead[0:1, :NH_SSD], "a_log": ghead[2:3, :NH_SSD], "d_skip": ghead[3:4, :NH_SSD],
        "ssd_norm_w": glane[1:2], "hg_lower_bounds": acc_h[2:4], "hg_norm_w": acc_h[4:5, :128],
        "norm_xa_w": acc_a[0:1], "norm_mem_w": g_nmem, "norm_ffn_w": acc_f[2:3], "norm_final_w": acc_f[1],
    }
    loss = (0.5 / D) * jnp.sum(acc_f[0])
    small_parts = [gs[name] for name in SMALL] + [loss.reshape(1)]
    small_shapes = [gs[name].shape for name in SMALL] + [(1,)]
    (g_in,), (packed,) = _run_phases([("swap", red_in, [half["w_in"]]),
                                      ("gather", [_pack_small(small_parts)], [None])], "grads_finish")
    g_big = dict(zip(GROUP_FFN + GROUP_ATTN + ("w_in",), g_ffn + g_attn + [g_in]))
    small = _unpack_small(_sum8(packed, "small_total"), small_shapes)
    g_small = dict(zip(SMALL, small[:-1]))
    loss_all = small[-1][0]
    g_small["conv_w"] = lax.dynamic_slice_in_dim(g_small["conv_w"], chip * 384, 384, 2)

    grads, delta, new_m, new_v = {}, {}, {}, {}
    for name in BIG:
        outs = (g_big[name][None],) + tuple(_adamw(wsh[name], g_big[name], shard(m, name), shard(v, name),
                                                   "adamw_" + name))
        if name in TRANSPOSED:
            outs = tuple(jnp.swapaxes(o, 1, 2) for o in outs)
        grads[name], delta[name], new_m[name], new_v[name] = outs
    shapes = [w[name].shape for name in SMALL]
    packs = [_pack_small([t[name] for name in SMALL]) for t in (w, g_small, m, v)]
    outs = _adamw(packs[0][None], packs[1], packs[2][None], packs[3][None], "adamw_small")
    for name, g_, d_, nm_, nv_ in zip(SMALL, [g_small[n] for n in SMALL], *[_unpack_small(o[0], shapes) for o in outs]):
        grads[name] = g_.reshape(w[name].shape)
        delta[name], new_m[name], new_v[name] = d_, nm_, nv_

    return (loss_all, gx[None], *[grads[n] for n in WEIGHTS], *[delta[n] for n in WEIGHTS],
            *[new_m[n] for n in WEIGHTS], *[new_v[n] for n in WEIGHTS])
```

```python
import jax
import jax.numpy as jnp
from jax import lax
from jax.experimental import pallas as pl
from jax.experimental.pallas import tpu as pltpu

F32 = jnp.float32
BF = jnp.bfloat16
MESH = pl.DeviceIdType.MESH
SDS = jax.ShapeDtypeStruct
ANY = pl.BlockSpec(memory_space=pl.ANY)

D = 1024
EPS = 1e-6
NH_SSD = 16
SSD_P = 64
NH_HG = 8
Q = 128
SUB = 32
NSUB = Q // SUB
HG_LB_FLOOR = 1e-2
XA_HEADS = 4
XA_HD = 256
MEM_LEN = 256
FFN = 2816
TL = 512
TL_FFN = 256
VMEM_LIMIT = 56 << 20
MATMUL_VMEM = 40 << 20

N_IN = 6672
Z0, XBC0, DT0, HQ0, HF0, HI0, HG0 = 0, 1024, 2560, 2576, 3600, 4624, 5648

ADAM_LR, ADAM_B1, ADAM_B2, ADAM_EPS, ADAM_WD, ADAM_STEP = 0.001, 0.9, 0.999, 1e-08, 0.01, 10

BIG = ("w_in", "w_out", "xa_wq", "xa_wkv", "xa_wo", "ffn_w_gate", "ffn_w_up", "ffn_w_down")
TRANSPOSED = ("w_in", "ffn_w_gate", "ffn_w_up")
SMALL = ("norm_mix_w", "conv_w", "conv_b", "dt_bias", "a_log", "d_skip", "ssd_norm_w", "hg_lower_bounds",
         "hg_norm_w", "norm_xa_w", "norm_mem_w", "norm_ffn_w", "norm_final_w")
WEIGHTS = ("norm_mix_w", "w_in", "conv_w", "conv_b", "dt_bias", "a_log", "d_skip", "ssd_norm_w", "hg_lower_bounds",
           "hg_norm_w", "w_out", "norm_xa_w", "norm_mem_w", "xa_wq", "xa_wkv", "xa_wo", "norm_ffn_w", "ffn_w_gate",
           "ffn_w_up", "ffn_w_down", "norm_final_w")


def _cparams():
    return pltpu.CompilerParams(dimension_semantics=("arbitrary",), vmem_limit_bytes=VMEM_LIMIT)


def _const(shape):
    return pl.BlockSpec(shape, lambda i: (0,) * len(shape))


def _resident(shape):
    return pl.BlockSpec(shape, lambda i: (0,) * len(shape), pipeline_mode=pl.Buffered(1))


def _rows(tl, n):
    return pl.BlockSpec((tl, n), lambda i: (i, 0))


def _dot(a, b):
    return jnp.dot(a.astype(BF), b.astype(BF), preferred_element_type=F32)


def _dot_nt(a, b):
    return lax.dot_general(a.astype(BF), b.astype(BF), (((1,), (1,)), ((), ())), preferred_element_type=F32)


def _dot_tn(a, b):
    return lax.dot_general(a.astype(BF), b.astype(BF), (((0,), (0,)), ((), ())), preferred_element_type=F32)


def _split(v, passes):
    parts, rest = [], v
    for p in range(passes):
        hi = rest.astype(BF)
        parts.append(hi)
        if p + 1 < passes:
            rest = rest - hi.astype(F32)
    return parts


def _sel_dot(a, sel, passes=3):
    sb = sel.astype(BF)
    out = None
    for part in _split(a, passes):
        t = jnp.dot(part, sb, preferred_element_type=F32)
        out = t if out is None else out + t
    return out


def _dot_sel(sel, b, passes=3):
    sb = sel.astype(BF)
    out = None
    for part in _split(b, passes):
        t = jnp.dot(sb, part, preferred_element_type=F32)
        out = t if out is None else out + t
    return out


def _iota(shape, dim):
    return lax.broadcasted_iota(jnp.int32, shape, dim)


def _sigmoid(v):
    return 0.5 * jnp.tanh(0.5 * v) + 0.5


def _rms(v, w):
    r = lax.rsqrt(jnp.mean(v * v, axis=-1, keepdims=True) + EPS)
    n = v * r
    return n * w, n, r


def _rms_bwd(dy, n, r, w):
    dn = dy * w
    return r * (dn - n * jnp.mean(dn * n, axis=-1, keepdims=True)), dy * n


def _colsum(v):
    return jnp.sum(v, axis=0, keepdims=True)


def _zero_first(*refs):
    @pl.when(pl.program_id(0) == 0)
    def _():
        for r in refs:
            r[...] = jnp.zeros_like(r)


def _in_proj(x, nw, wt, phases=()):
    L = x.shape[0]
    tl = min(TL, L)

    def body(x_ref, nw_ref, w_ref, h0_ref, z_ref, xbc_ref, hq_ref, hf_ref, hi_ref, hg_ref, dt_ref):
        h, _, _ = _rms(x_ref[...], nw_ref[...])
        hb = h.astype(BF)
        h0_ref[...] = hb

        def proj(a, b):
            return _dot_nt(hb, w_ref[a:b, :])

        z_ref[...] = proj(Z0, XBC0).astype(BF)
        xbc_ref[...] = proj(XBC0, DT0).astype(BF)
        dt_ref[...] = proj(DT0, DT0 + 128)
        hq_ref[...] = proj(HQ0, HF0).astype(BF)
        hf_ref[...] = proj(HF0, HI0)
        hi_ref[...] = proj(HI0, HG0).astype(BF)
        hg_ref[...] = proj(HG0, N_IN).astype(BF)

    outs = [SDS((L, D), BF), SDS((L, D), BF), SDS((L, 1536), BF), SDS((L, D), BF), SDS((L, D), F32),
            SDS((L, D), BF), SDS((L, D), BF), SDS((L, 128), F32)]
    steps = L // tl
    return _call(body, (x, nw, wt), name="in_proj", grid=(steps,),
                 in_specs=[_rows(tl, D), _const((1, D)), _resident((N_IN, D))],
                 out_specs=[_rows(tl, o.shape[1]) for o in outs], out_shape=outs, phases=phases,
                 mid_step=(3 * steps) // 4)


def _mem_kv(mem, nw, wkv4):
    def body(m_ref, nw_ref, w_ref, k_ref, v_ref):
        m, _, _ = _rms(m_ref[...], nw_ref[...])
        mb = m.astype(BF)
        for i in range(2):
            sl = slice(512 * i, 512 * i + 512)
            k_ref[:, sl] = jnp.dot(mb, w_ref[i], preferred_element_type=F32).astype(BF)
            v_ref[:, sl] = jnp.dot(mb, w_ref[2 + i], preferred_element_type=F32).astype(BF)

    outs = [SDS((MEM_LEN, D), BF)] * 2
    return pl.pallas_call(
        body, grid=(1,), name="mem_kv",
        in_specs=[_const((MEM_LEN, D)), _const((1, D)), _const((4, D, 512))],
        out_specs=[_const((MEM_LEN, D))] * 2, out_shape=outs, compiler_params=_cparams())(mem, nw, wkv4)


def _mem_kv_bwd(mem, nw, wkv4, dk, dv):
    def body(m_ref, nw_ref, w_ref, dk_ref, dv_ref, gnw_ref, gw_ref):
        m, n, _ = _rms(m_ref[...], nw_ref[...])
        mb = m.astype(BF)
        dm = jnp.zeros((MEM_LEN, D), F32)
        for i in range(4):
            src = dk_ref if i < 2 else dv_ref
            d = src[:, 512 * (i % 2):512 * (i % 2) + 512].astype(BF)
            gw_ref[i] = _dot_tn(mb, d)
            dm = dm + _dot_nt(d, w_ref[i])
        gnw_ref[...] = _colsum(dm * n)

    return pl.pallas_call(
        body, grid=(1,), name="mem_kv_bwd",
        in_specs=[_const((MEM_LEN, D)), _const((1, D)), _const((4, D, 512)), _const((MEM_LEN, D)), _const((MEM_LEN, D))],
        out_specs=[_const((1, D)), _const((4, D, 512))],
        out_shape=[SDS((1, D), F32), SDS((4, D, 512), F32)], compiler_params=_cparams())(mem, nw, wkv4, dk, dv)


def _softmax_rows(sc):
    e = jnp.exp(sc - jnp.max(sc, axis=-1, keepdims=True))
    return e * (1.0 / jnp.sum(e, axis=-1, keepdims=True))


def _attn_fwd(x, ya, ob, w_out, nxa, wq, k, v, wo):
    L = x.shape[0]
    tl = min(TL, L)
    scale = XA_HD ** -0.5

    def body(x_ref, ya_ref, ob_ref, wout_ref, nxa_ref, wq_ref, k_ref, v_ref, wo_ref,
             x1_ref, x2_ref, hxa_ref, q_ref, ox_ref):
        x1 = x_ref[...] + jnp.dot(ya_ref[...], wout_ref[:D, :], preferred_element_type=F32) \
            + jnp.dot(ob_ref[...], wout_ref[D:, :], preferred_element_type=F32)
        x1_ref[...] = x1
        h, _, _ = _rms(x1, nxa_ref[...])
        hb = h.astype(BF)
        hxa_ref[...] = hb
        qb = jnp.dot(hb, wq_ref[...], preferred_element_type=F32).astype(BF)
        q_ref[...] = qb
        heads = [slice(hd * XA_HD, (hd + 1) * XA_HD) for hd in range(XA_HEADS)]
        ps = [_softmax_rows(_dot_nt(qb[:, sl], k_ref[:, sl]) * scale) for sl in heads]
        oxs = [_dot(p, v_ref[:, sl]) for p, sl in zip(ps, heads)]
        oxb = jnp.concatenate(oxs, axis=1).astype(BF)
        ox_ref[...] = oxb
        x2_ref[...] = x1 + jnp.dot(oxb, wo_ref[...], preferred_element_type=F32)

    outs = [SDS((L, D), F32), SDS((L, D), F32), SDS((L, D), BF), SDS((L, D), BF), SDS((L, D), BF)]
    return pl.pallas_call(
        body, grid=(L // tl,), name="attn_fwd",
        in_specs=[_rows(tl, D), _rows(tl, D), _rows(tl, D), _resident((2 * D, D)), _const((1, D)), _resident((D, D)),
                  _resident((MEM_LEN, D)), _resident((MEM_LEN, D)), _resident((D, D))],
        out_specs=[_rows(tl, D)] * 5, out_shape=outs, compiler_params=_cparams())(x, ya, ob, w_out, nxa, wq, k, v, wo)


def _ffn_loss(x2, tgt, nffn, nfin, wgt, wut, wd):
    L = x2.shape[0]
    tl = min(TL_FFN, L)

    def body(x2_ref, t_ref, nffn_ref, nfin_ref, wg_ref, wu_ref, wd_ref,
             dx2_ref, h_ref, a_ref, dx3_ref, dg_ref, du_ref, acc_ref):
        _zero_first(acc_ref)
        x2v = x2_ref[...]
        h, n2, r2 = _rms(x2v, nffn_ref[...])
        hb = h.astype(BF)
        h_ref[...] = hb
        g = _dot_nt(hb, wg_ref[...])
        u = _dot_nt(hb, wu_ref[...])
        sg = _sigmoid(g)
        ab = (g * sg * u).astype(BF)
        a_ref[...] = ab
        x3 = x2v + jnp.dot(ab, wd_ref[...], preferred_element_type=F32)
        y, n3, r3 = _rms(x3, nfin_ref[...])
        err = y - t_ref[...]
        acc_ref[0:1, :] += _colsum(err * err)
        dx3, dwf = _rms_bwd(err * (1.0 / D), n3, r3, nfin_ref[...])
        acc_ref[1:2, :] += _colsum(dwf)
        dx3b = dx3.astype(BF)
        dx3_ref[...] = dx3b
        da = _dot_nt(dx3b, wd_ref[...])
        dgb = (da * u * sg * (1.0 + g * (1.0 - sg))).astype(BF)
        dub = (da * g * sg).astype(BF)
        dg_ref[...] = dgb
        du_ref[...] = dub
        dh = jnp.dot(dgb, wg_ref[...], preferred_element_type=F32) + jnp.dot(dub, wu_ref[...], preferred_element_type=F32)
        dn, dwn = _rms_bwd(dh, n2, r2, nffn_ref[...])
        acc_ref[2:3, :] += _colsum(dwn)
        dx2_ref[...] = dx3 + dn

    outs = [SDS((L, D), F32), SDS((L, D), BF), SDS((L, FFN), BF), SDS((L, D), BF), SDS((L, FFN), BF),
            SDS((L, FFN), BF), SDS((8, D), F32)]
    wspec = _resident((FFN, D))
    return pl.pallas_call(
        body, grid=(L // tl,), name="ffn_loss",
        in_specs=[_rows(tl, D), _rows(tl, D), _const((1, D)), _const((1, D)), wspec, wspec, wspec],
        out_specs=[_rows(tl, D), _rows(tl, D), _rows(tl, FFN), _rows(tl, D), _rows(tl, FFN), _rows(tl, FFN),
                   _const((8, D))],
        out_shape=outs, compiler_params=_cparams())(x2, tgt, nffn, nfin, wgt, wut, wd)


def _attn_bwd(dx2, x1, q, k, v, nxa, wq, wo, w_out, phases=()):
    L = dx2.shape[0]
    tl = min(TL, L)
    scale = XA_HD ** -0.5

    def body(dx2_ref, x1_ref, q_ref, k_ref, v_ref, nxa_ref, wq_ref, wo_ref, wout_ref,
             dx1_ref, dya_ref, dob_ref, dq_ref, dk_ref, dv_ref, acc_ref):
        _zero_first(dk_ref, dv_ref, acc_ref)
        dx2v = dx2_ref[...]
        dox = _dot_nt(dx2v, wo_ref[...]).astype(BF)
        qb = q_ref[...]
        heads = [slice(hd * XA_HD, (hd + 1) * XA_HD) for hd in range(XA_HEADS)]
        ps = [_softmax_rows(_dot_nt(qb[:, sl], k_ref[:, sl]) * scale) for sl in heads]
        dps = [_dot_nt(dox[:, sl], v_ref[:, sl]) for sl in heads]
        dss = [(p * (dp - jnp.sum(dp * p, axis=-1, keepdims=True)) * scale).astype(BF) for p, dp in zip(ps, dps)]
        for sl, p, ds in zip(heads, ps, dss):
            dv_ref[:, sl] += _dot_tn(p, dox[:, sl])
            dk_ref[:, sl] += _dot_tn(ds, qb[:, sl])
        dqs = [_dot(ds, k_ref[:, sl]) for sl, ds in zip(heads, dss)]
        dqb = jnp.concatenate(dqs, axis=1).astype(BF)
        dq_ref[...] = dqb
        dh = _dot_nt(dqb, wq_ref[...])
        _, n1, r1 = _rms(x1_ref[...], nxa_ref[...])
        dn, dwn = _rms_bwd(dh, n1, r1, nxa_ref[...])
        acc_ref[0:1, :] += _colsum(dwn)
        dx1 = dx2v + dn
        dx1_ref[...] = dx1
        dx1b = dx1.astype(BF)
        dya_ref[...] = _dot_nt(dx1b, wout_ref[:D, :]).astype(BF)
        dob_ref[...] = _dot_nt(dx1b, wout_ref[D:, :]).astype(BF)

    outs = [SDS((L, D), F32), SDS((L, D), BF), SDS((L, D), BF), SDS((L, D), BF), SDS((MEM_LEN, D), F32),
            SDS((MEM_LEN, D), F32), SDS((8, D), F32)]
    return _call(body, (dx2, x1, q, k, v, nxa, wq, wo, w_out), name="attn_bwd", grid=(L // tl,),
                 in_specs=[_rows(tl, D), _rows(tl, D), _rows(tl, D), _resident((MEM_LEN, D)), _resident((MEM_LEN, D)),
                           _const((1, D)), _resident((D, D)), _resident((D, D)), _resident((2 * D, D))],
                 out_specs=[_rows(tl, D)] * 4 + [_const((MEM_LEN, D)), _const((MEM_LEN, D)), _const((8, D))],
                 out_shape=outs, phases=phases)


def _in_proj_bwd(x, dx1, dz, dxbc, dhq, dhf, dhi, dhg, ddt, nw, wt):
    L = x.shape[0]
    tl = min(TL, L)

    def body(x_ref, dx1_ref, dz_ref, dxbc_ref, dhq_ref, dhf_ref, dhi_ref, dhg_ref, ddt_ref, nw_ref, w_ref,
             gx_ref, acc_ref):
        _zero_first(acc_ref)
        dh = _dot(dz_ref[...], w_ref[Z0:XBC0, :]) + _dot(dxbc_ref[...], w_ref[XBC0:DT0, :]) \
            + _dot(ddt_ref[...], w_ref[DT0:DT0 + 128, :]) + _dot(dhq_ref[...], w_ref[HQ0:HF0, :]) \
            + _dot(dhf_ref[...], w_ref[HF0:HI0, :]) + _dot(dhi_ref[...], w_ref[HI0:HG0, :]) \
            + _dot(dhg_ref[...], w_ref[HG0:N_IN, :])
        _, n, r = _rms(x_ref[...], nw_ref[...])
        dn, dwn = _rms_bwd(dh, n, r, nw_ref[...])
        acc_ref[0:1, :] += _colsum(dwn)
        gx_ref[...] = dx1_ref[...] + dn

    return pl.pallas_call(
        body, grid=(L // tl,), name="in_proj_bwd",
        in_specs=[_rows(tl, D), _rows(tl, D), _rows(tl, D), _rows(tl, 1536), _rows(tl, D), _rows(tl, D), _rows(tl, D),
                  _rows(tl, D), _rows(tl, 128), _const((1, D)), _resident((N_IN, D))],
        out_specs=[_rows(tl, D), _const((8, D))], out_shape=[SDS((L, D), F32), SDS((8, D), F32)],
        compiler_params=_cparams())(x, dx1, dz, dxbc, dhq, dhf, dhi, dhg, ddt, nw, wt)


def _gw_in(h0, dz, dxbc, ddt, dhq, dhf, dhi, dhg, phases=()):
    L = h0.shape[0]
    tl = min(512, L)

    def body(h_ref, dz_ref, dxbc_ref, ddt_ref, dhq_ref, dhf_ref, dhi_ref, dhg_ref, o_ref):
        _zero_first(o_ref)
        hb = h_ref[...]
        o_ref[Z0:XBC0, :] += _dot_tn(dz_ref[...], hb)
        o_ref[XBC0:DT0, :] += _dot_tn(dxbc_ref[...], hb)
        o_ref[DT0:HQ0, :] += _dot_tn(ddt_ref[...], hb)[0:NH_SSD, :]
        o_ref[HQ0:HF0, :] += _dot_tn(dhq_ref[...], hb)
        o_ref[HF0:HI0, :] += _dot_tn(dhf_ref[...], hb)
        o_ref[HI0:HG0, :] += _dot_tn(dhi_ref[...], hb)
        o_ref[HG0:N_IN, :] += _dot_tn(dhg_ref[...], hb)

    return _call(body, (h0, dz, dxbc, ddt, dhq, dhf, dhi, dhg), name="gw_in", grid=(L // tl,),
                 in_specs=[_rows(tl, D), _rows(tl, D), _rows(tl, 1536), _rows(tl, 128), _rows(tl, D), _rows(tl, D),
                           _rows(tl, D), _rows(tl, D)],
                 out_specs=[_const((N_IN, D))], out_shape=[SDS((N_IN, D), F32)], phases=phases)


def _token_tile(L, out_bytes, row_bytes):
    tl = min(2048, L)
    while tl > 256 and out_bytes + 2 * tl * row_bytes > MATMUL_VMEM:
        tl //= 2
    return tl


def _matmul_tn(a, b, name):
    L, M = a.shape
    N = b.shape[1]
    tl = _token_tile(L, 4 * M * N, M * a.dtype.itemsize + N * b.dtype.itemsize)

    def body(a_ref, b_ref, o_ref):
        _zero_first(o_ref)
        o_ref[...] += _dot_tn(a_ref[...], b_ref[...])

    return pl.pallas_call(
        body, grid=(L // tl,), name=name, in_specs=[_rows(tl, M), _rows(tl, N)], out_specs=_const((M, N)),
        out_shape=SDS((M, N), F32), compiler_params=_cparams())(a, b)


def _matmul_tn_pair(a0, a1, b, name):
    L, M = a0.shape
    N = b.shape[1]
    tl = _token_tile(L, 8 * M * N, 2 * M * a0.dtype.itemsize + N * b.dtype.itemsize)

    def body(a0_ref, a1_ref, b_ref, o_ref):
        _zero_first(o_ref)
        bv = b_ref[...].astype(BF)
        o_ref[:M, :] += _dot_tn(a0_ref[...], bv)
        o_ref[M:, :] += _dot_tn(a1_ref[...], bv)

    return pl.pallas_call(
        body, grid=(L // tl,), name=name, in_specs=[_rows(tl, M), _rows(tl, M), _rows(tl, N)],
        out_specs=_const((2 * M, N)), out_shape=SDS((2 * M, N), F32), compiler_params=_cparams())(a0, a1, b)


def _head_expand():
    e = (jnp.right_shift(_iota((128, D), 1), 6) == _iota((128, D), 0)).astype(BF)
    et = (jnp.right_shift(_iota((D, 128), 0), 6) == _iota((D, 128), 1)).astype(BF)
    return e, et


def _conv_shifts(cur, other, up):
    rows = _iota((Q, 1), 0)
    out = []
    for s in (1, 2, 3):
        if up:
            out.append(jnp.where(rows >= Q - s, pltpu.roll(other, Q - s, 0), pltpu.roll(cur, Q - s, 0)))
        else:
            out.append(jnp.where(rows < s, pltpu.roll(other, s, 0), pltpu.roll(cur, s, 0)))
    return out


def _ssd_pre(u, dtr, dtb, alog):
    e, et = _head_expand()
    sgu = _sigmoid(u)
    xc = u * sgu
    lane = _iota((1, 128), 1)
    hmask = (lane < NH_SSD).astype(F32)
    pre = dtr + dtb
    dt = (jnp.maximum(pre, 0.0) + jnp.log(1.0 + jnp.exp(-jnp.abs(pre)))) * hmask
    a_row = -jnp.exp(alog)
    causal = _iota((Q, Q), 1) <= _iota((Q, Q), 0)
    tri = causal.astype(BF)
    acum = _dot_sel(tri, dt * a_row)
    acum_full = _sel_dot(acum, e)
    alast_full = acum_full[Q - 1:Q, :]
    dt_full = _sel_dot(dt, e)
    xs = xc[:, :D]
    return dict(e=e, et=et, sgu=sgu, xs=xs, bm=xc[:, D:D + 256], cm=xc[:, D + 256:], hmask=hmask, pre=pre, dt=dt,
                a_row=a_row, causal=causal, tri=tri, acum=acum, acum_t=acum.T, eA_full=jnp.exp(acum_full),
                dte_full=jnp.exp(alast_full - acum_full), dt_full=dt_full, xdt=xs * dt_full)


def _ssd_decay(pre, hh, cb):
    seg = pre["acum"][:, hh:hh + 1] - pre["acum_t"][hh:hh + 1, :]
    lm = jnp.where(pre["causal"], jnp.exp(jnp.minimum(seg, 0.0)), 0.0)
    return lm, cb * lm


def _ssd_fwd(xbc, dtr, z, conv_w, conv_b, dtb, alog, dskip_full, nw):
    L = xbc.shape[0]
    nc = L // Q

    def body(xbc_ref, dtr_ref, z_ref, cw_ref, cb_ref, dtb_ref, alog_ref, dsk_ref, nw_ref,
             ya_ref, y_ref, u_ref, st_ref, prev_ref, s_ref):
        @pl.when(pl.program_id(0) == 0)
        def _():
            prev_ref[...] = jnp.zeros_like(prev_ref)
            s_ref[...] = jnp.zeros_like(s_ref)

        xr = xbc_ref[...].astype(F32)
        sh = _conv_shifts(xr, prev_ref[...], up=False)
        u = cb_ref[...] + cw_ref[3:4, :] * xr + cw_ref[2:3, :] * sh[0] + cw_ref[1:2, :] * sh[1] + cw_ref[0:1, :] * sh[2]
        prev_ref[...] = xr
        ub = u.astype(BF)
        u_ref[...] = ub
        pre = _ssd_pre(ub.astype(F32), dtr_ref[...], dtb_ref[...], alog_ref[...])
        lo = _iota((1, 128), 1) < SSD_P
        s_old = s_ref[...]
        st_ref[0] = s_old
        ys = []
        for g in range(2):
            bg, cg = pre["bm"][:, 128 * g:128 * g + 128], pre["cm"][:, 128 * g:128 * g + 128]
            cb = _dot_nt(cg, bg)
            gs = slice(512 * g, 512 * g + 512)
            yd = []
            for j in range(4 * g, 4 * g + 4):
                xp = pre["xdt"][:, 128 * j:128 * j + 128].astype(BF)
                _, m0 = _ssd_decay(pre, 2 * j, cb)
                _, m1 = _ssd_decay(pre, 2 * j + 1, cb)
                yd.append(jnp.where(lo, _dot(m0, xp), _dot(m1, xp)))
            yoff = _dot_nt(cg, s_old[gs, :]) * pre["eA_full"][:, gs]
            ys.append(jnp.concatenate(yd, axis=1) + yoff)
            st = _dot_tn((pre["xdt"] * pre["dte_full"])[:, gs], bg)
            cdcol = jnp.exp(_dot_sel(pre["et"][gs, :], pre["acum_t"])[:, Q - 1:Q])
            s_ref[gs, :] = s_old[gs, :] * cdcol + st
        y = jnp.concatenate(ys, axis=1) + dsk_ref[...] * pre["xs"]
        yb = y.astype(BF)
        y_ref[...] = yb
        zf = z_ref[...].astype(F32)
        yz = yb.astype(F32) * zf * _sigmoid(zf)
        outs = []
        for g in range(2):
            gs = slice(512 * g, 512 * g + 512)
            o, _, _ = _rms(yz[:, gs], nw_ref[:, gs])
            outs.append(o)
        ya_ref[...] = jnp.concatenate(outs, axis=1).astype(BF)

    outs = [SDS((L, D), BF), SDS((L, D), BF), SDS((L, 1536), BF), SDS((nc, D, 128), F32)]
    return pl.pallas_call(
        body, grid=(nc,), name="ssd_fwd",
        in_specs=[_rows(Q, 1536), _rows(Q, 128), _rows(Q, D), _const((4, 1536)), _const((1, 1536)), _const((1, 128)),
                  _const((1, 128)), _const((1, D)), _const((1, D))],
        out_specs=[_rows(Q, D), _rows(Q, D), _rows(Q, 1536), pl.BlockSpec((1, D, 128), lambda i: (i, 0, 0))],
        out_shape=outs, scratch_shapes=[pltpu.VMEM((Q, 1536), F32), pltpu.VMEM((D, 128), F32)],
        compiler_params=_cparams())(xbc, dtr, z, conv_w, conv_b, dtb, alog, dskip_full, nw)


def _ssd_bwd(dya, y, z, u, xbc, dtr, states, conv_w, dtb, alog, dskip_full, nw):
    L = dya.shape[0]
    nc = L // Q

    def body(dya_ref, y_ref, z_ref, u_ref, xc_ref, dtr_ref, st_ref, cw_ref, dtb_ref, alog_ref, dsk_ref, nw_ref,
             dz_ref, dxbc_ref, ddt_ref, gconv_ref, ghead_ref, glane_ref, gs_ref, ndu_ref):
        step = pl.program_id(0)

        @pl.when(step == 0)
        def _():
            for r in (gconv_ref, ghead_ref, glane_ref, gs_ref, ndu_ref):
                r[...] = jnp.zeros_like(r)

        uf = u_ref[...].astype(F32)
        pre = _ssd_pre(uf, dtr_ref[...], dtb_ref[...], alog_ref[...])
        e, et, xs, xdt = pre["e"], pre["et"], pre["xs"], pre["xdt"]
        lane = _iota((1, 128), 1)
        lo = lane < SSD_P
        sub = _iota((128, 1), 0)
        zf = z_ref[...].astype(F32)
        sgz = _sigmoid(zf)
        sz = zf * sgz
        yv = y_ref[...].astype(F32)
        yz = yv * sz
        dyav = dya_ref[...].astype(F32)
        dyz, dnw = [], []
        for g in range(2):
            gs = slice(512 * g, 512 * g + 512)
            _, n, r = _rms(yz[:, gs], nw_ref[:, gs])
            dv, dw = _rms_bwd(dyav[:, gs], n, r, nw_ref[:, gs])
            dyz.append(dv)
            dnw.append(dw)
        dyz = jnp.concatenate(dyz, axis=1)
        glane_ref[1:2, :] += _colsum(jnp.concatenate(dnw, axis=1))
        dy = dyz * sz
        dz_ref[...] = (dyz * yv * sgz * (1.0 + zf * (1.0 - sgz))).astype(BF)
        glane_ref[0:1, :] += _colsum(dy * xs)
        dxs = dsk_ref[...] * dy

        s_in = st_ref[0]
        gst = gs_ref[...]
        gy = dy * pre["eA_full"]
        xdte = xdt * pre["dte_full"]
        dacum = jnp.zeros((Q, 128), F32)
        dacum_t = jnp.zeros((128, Q), F32)
        dxdt, dacum_full, ddte_full, dbs, dcs = [], [], [], [], []
        for g in range(2):
            gs = slice(512 * g, 512 * g + 512)
            bg, cg = pre["bm"][:, 128 * g:128 * g + 128], pre["cm"][:, 128 * g:128 * g + 128]
            sg_, dg_ = s_in[gs, :], gst[gs, :]
            yoff = _dot_nt(cg, sg_) * pre["eA_full"][:, gs]
            dc = _dot(gy[:, gs], sg_)
            dsin = _dot_tn(gy[:, gs], cg)
            dacum_full.append(dy[:, gs] * yoff)
            tg = _dot_nt(bg, dg_)
            ddte_full.append(tg * xdt[:, gs])
            db = _dot(xdte[:, gs], dg_)
            cb = _dot_nt(cg, bg)
            dcb = jnp.zeros((Q, Q), F32)
            dxg = []
            for j in range(4 * g, 4 * g + 4):
                xp = xdt[:, 128 * j:128 * j + 128].astype(BF)
                dyp = dy[:, 128 * j:128 * j + 128]
                dxp = jnp.zeros((Q, 128), F32)
                for idx in range(2):
                    hh = 2 * j + idx
                    lm, m = _ssd_decay(pre, hh, cb)
                    dym = jnp.where(lo if idx == 0 else jnp.logical_not(lo), dyp, 0.0).astype(BF)
                    dm = jnp.where(pre["causal"], _dot_nt(dym, xp), 0.0)
                    w = dm * m
                    dacum = dacum + jnp.where(lane == hh, jnp.sum(w, axis=1, keepdims=True), 0.0)
                    dacum_t = dacum_t + jnp.where(sub == hh, jnp.sum(w, axis=0, keepdims=True), 0.0)
                    dcb = dcb + dm * lm
                    dxp = dxp + _dot_tn(m, dym)
                dxg.append(dxp)
            dxdt.append(jnp.concatenate(dxg, axis=1) + tg * pre["dte_full"][:, gs])
            dcs.append(dc + _dot(dcb, bg))
            dbs.append(db + _dot_tn(dcb, cg))
            cdcol = jnp.exp(_dot_sel(et[gs, :], pre["acum_t"])[:, Q - 1:Q])
            gs_ref[gs, :] = dsin + dg_ * cdcol
        dxdt = jnp.concatenate(dxdt, axis=1)
        dacum = dacum + _sel_dot(jnp.concatenate(dacum_full, axis=1), et, 2) - dacum_t.T
        alast = pre["acum"][Q - 1:Q, :]
        dte = jnp.exp(alast - pre["acum"])
        ddte = _sel_dot(jnp.concatenate(ddte_full, axis=1), et, 2) * dte
        dacum = dacum - ddte
        dcd_col = jnp.sum(_dot_sel(e, gst * s_in, 2), axis=1, keepdims=True)
        dcd_row = jnp.broadcast_to(dcd_col, (128, 128)).T[0:1, :]
        dalast = _colsum(ddte) + dcd_row * jnp.exp(alast)
        dacum = dacum + jnp.where(_iota((Q, 1), 0) == Q - 1, dalast, 0.0)
        ddt = _sel_dot(dxdt * xs, et, 2)
        dxs = dxs + dxdt * pre["dt_full"]
        dda = _dot_sel((_iota((Q, Q), 1) >= _iota((Q, Q), 0)).astype(BF), dacum)
        ddt = ddt + dda * pre["a_row"]
        ghead_ref[1:2, :] += _colsum(dda * pre["dt"])
        ddtr = ddt * _sigmoid(pre["pre"]) * pre["hmask"]
        ghead_ref[0:1, :] += _colsum(ddtr)
        ddt_ref[...] = ddtr

        dxc = jnp.concatenate([dxs] + dbs + dcs, axis=1)
        sgu = pre["sgu"]
        du = dxc * sgu * (1.0 + uf * (1.0 - sgu))
        shu = _conv_shifts(du, ndu_ref[...], up=True)
        dxr = cw_ref[3:4, :] * du + cw_ref[2:3, :] * shu[0] + cw_ref[1:2, :] * shu[1] + cw_ref[0:1, :] * shu[2]
        ndu_ref[...] = du
        dxbc_ref[...] = dxr.astype(BF)
        xr = xc_ref[...].astype(F32)
        gconv_ref[3:4, :] += _colsum(du * xr)
        gconv_ref[2:3, :] += _colsum(shu[0] * xr)
        gconv_ref[1:2, :] += _colsum(shu[1] * xr)
        gconv_ref[0:1, :] += _colsum(shu[2] * xr)
        gconv_ref[4:5, :] += _colsum(du)

        @pl.when(step == nc - 1)
        def _():
            ghead_ref[2:3, :] = ghead_ref[1:2, :] * pre["a_row"]
            ghead_ref[3:4, :] = _sel_dot(glane_ref[...], et)[0:1, :]

    rev = lambda i: (nc - 1 - i, 0)
    outs = [SDS((L, D), BF), SDS((L, 1536), BF), SDS((L, 128), F32), SDS((8, 1536), F32), SDS((8, 128), F32),
            SDS((8, D), F32)]
    return pl.pallas_call(
        body, grid=(nc,), name="ssd_bwd",
        in_specs=[pl.BlockSpec((Q, D), rev), pl.BlockSpec((Q, D), rev), pl.BlockSpec((Q, D), rev),
                  pl.BlockSpec((Q, 1536), rev), pl.BlockSpec((Q, 1536), rev),
                  pl.BlockSpec((Q, 128), rev), pl.BlockSpec((1, D, 128), lambda i: (nc - 1 - i, 0, 0)),
                  _const((4, 1536)), _const((1, 128)), _const((1, 128)), _const((1, D)), _const((1, D))],
        out_specs=[pl.BlockSpec((Q, D), rev), pl.BlockSpec((Q, 1536), rev), pl.BlockSpec((Q, 128), rev),
                   _const((8, 1536)), _const((8, 128)), _const((8, D))],
        out_shape=outs, scratch_shapes=[pltpu.VMEM((D, 128), F32), pltpu.VMEM((Q, 1536), F32)],
        compiler_params=_cparams())(dya, y, z, u, xbc, dtr, states, conv_w, dtb, alog, dskip_full, nw)


def _hg_gates(hq, hf, hlb):
    h0, h1 = hlb[0:1, :], hlb[1:2, :]
    mx = jnp.maximum(h0, h1)
    e0, e1 = jnp.exp(h0 - mx), jnp.exp(h1 - mx)
    lb = e0 / (e0 + e1)
    sg = _sigmoid(hf)
    fg = lb + (1.0 - lb) * sg
    tri = (_iota((Q, Q), 1) <= _iota((Q, Q), 0)).astype(BF)
    return hq * _sigmoid(hq), 1.0 - fg, fg, sg, lb, e1 / (e0 + e1), _dot_sel(tri, jnp.log(fg))


def _hg_intra(b, q, k):
    rowblk = jnp.right_shift(_iota((Q, 1), 0), SUB.bit_length() - 1)
    mids = [b[SUB * i + SUB // 2:SUB * i + SUB // 2 + 1, :] for i in range(NSUB)]
    prevs = [mids[0]] + [b[SUB * i - 1:SUB * i, :] for i in range(1, NSUB)]
    mfull = jnp.concatenate([jnp.broadcast_to(r, (SUB, 128)) for r in mids], axis=0)
    rfull = jnp.concatenate([jnp.broadcast_to(r, (SUB, 128)) for r in prevs], axis=0)
    eqd, ek, eqo = jnp.exp(b - mfull), jnp.exp(mfull - b), jnp.exp(b - rfull)
    qd, qo, khat = q * eqd, q * eqo, k * ek
    rtab = jnp.concatenate(prevs, axis=0)
    djs = [jnp.exp(rtab - mids[j]) for j in range(NSUB)]
    zero = jnp.zeros((SUB, 128), F32)
    cols = []
    for j in range(NSUB):
        pieces = []
        for i in range(NSUB):
            rs = slice(SUB * i, SUB * i + SUB)
            pieces.append(zero if i < j else qd[rs] if i == j else qo[rs] * djs[j][i:i + 1, :])
        cols.append(jnp.concatenate(pieces, axis=0))
    qt = jnp.concatenate(cols, axis=1).astype(BF)
    kt = jnp.concatenate([jnp.where(rowblk == j, khat, 0.0) for j in range(NSUB)], axis=1).astype(BF)
    causal = _iota((Q, Q), 1) <= _iota((Q, Q), 0)
    att = jnp.where(causal, _dot_nt(qt, kt), 0.0)
    return att, qt, kt, (eqd, ek, eqo, djs), causal


def _hg_intra_bwd(dqt, dkt, qt, kt, factors):
    eqd, ek, eqo, djs = factors
    dqd, dqo, dkh, db = [], [], [], []
    for i in range(NSUB):
        rs = slice(SUB * i, SUB * i + SUB)
        diag = slice(128 * i, 128 * i + 128)
        dqd.append(dqt[rs, diag])
        dkh.append(dkt[rs, diag])
        dbi = qt[rs, diag].astype(F32) * dqt[rs, diag] - kt[rs, diag].astype(F32) * dkt[rs, diag]
        acc = jnp.zeros((SUB, 128), F32)
        for j in range(i):
            bl = slice(128 * j, 128 * j + 128)
            acc = acc + dqt[rs, bl] * djs[j][i:i + 1, :]
            dbi = dbi + qt[rs, bl].astype(F32) * dqt[rs, bl]
        dqo.append(acc)
        db.append(dbi)
    cat = lambda t: jnp.concatenate(t, axis=0)
    return cat(dqd) * eqd + cat(dqo) * eqo, cat(dkh) * ek, cat(db)


def _hg_att_exact(b, q, k, b_ref, q_ref, att_t_ref):
    b_ref[...] = b
    q_ref[...] = q
    att_t_ref[...] = jnp.zeros((Q, Q), F32)
    rows, lane = _iota((Q, 1), 0), _iota((1, Q), 1)

    def step(i, carry):
        e = jnp.exp(jnp.minimum(b_ref[pl.ds(i, 1), :] - b, 0.0))
        col = jnp.sum(q_ref[pl.ds(i, 1), :] * k * e, axis=1, keepdims=True)
        att_t_ref[...] = jnp.where(lane == i, jnp.where(rows <= i, col, 0.0), att_t_ref[...])
        return carry

    lax.fori_loop(0, Q, step, 0)
    return att_t_ref[...].T


def _hg_att_exact_bwd(da, b, q, k, b_ref, q_ref, da_t_ref, dq_ref, dk_ref):
    b_ref[...] = b
    q_ref[...] = q
    da_t_ref[...] = da.T
    dk_ref[...] = jnp.zeros((Q, 128), F32)
    lane = _iota((1, Q), 1)

    def step(i, carry):
        e = jnp.exp(jnp.minimum(b_ref[pl.ds(i, 1), :] - b, 0.0))
        g = jnp.sum(jnp.where(lane == i, da_t_ref[...], 0.0), axis=1, keepdims=True) * e
        dq_ref[pl.ds(i, 1), :] = jnp.sum(g * k, axis=0, keepdims=True)
        dk_ref[...] += g * q_ref[pl.ds(i, 1), :]
        return carry

    lax.fori_loop(0, Q, step, 0)
    dq, dk = dq_ref[...], dk_ref[...]
    return dq, dk, q * dq - k * dk


def _hg_fwd(hq, hf, hi, hg, hlb, nw, fast):
    L = hq.shape[0]
    nc = L // Q

    def run(exact, step, hq_ref, hf_ref, hi_ref, hg_ref, hlb_ref, nw_ref, ob_ref, o_ref, st_ref, s_ref, *tmp):
        @pl.when(step == 0)
        def _():
            s_ref[...] = jnp.zeros_like(s_ref)

        qf, kf, _, _, _, _, bcum = _hg_gates(hq_ref[...].astype(F32), hf_ref[...], hlb_ref[...])
        gate = hg_ref[...].astype(F32)
        heads = [slice(128 * h, 128 * h + 128) for h in range(NH_HG)]
        if exact:
            atts = [_hg_att_exact(bcum[:, sl], qf[:, sl], kf[:, sl], *tmp).astype(BF) for sl in heads]
        else:
            atts = [_hg_intra(bcum[:, sl], qf[:, sl], kf[:, sl])[0].astype(BF) for sl in heads]
        olds = [s_ref[sl, :] for sl in heads]
        outs_ = [_dot(att, hi_ref[:, sl]) + _dot(qf[:, sl] * jnp.exp(bcum[:, sl]), s)
                 for att, sl, s in zip(atts, heads, olds)]
        for sl, s, o in zip(heads, olds, outs_):
            b, k = bcum[:, sl], kf[:, sl]
            st_ref[0, sl, :] = s
            blast = b[Q - 1:Q, :]
            s_ref[sl, :] = s * jnp.exp(b.T[:, Q - 1:Q]) + _dot_tn(k * jnp.exp(blast - b), hi_ref[:, sl])
            ob = o.astype(BF)
            o_ref[:, sl] = ob
            on, _, _ = _rms(ob.astype(F32), nw_ref[...])
            gt = gate[:, sl]
            ob_ref[:, sl] = (on * gt * _sigmoid(gt)).astype(BF)

    def body(fast_ref, *refs):
        step = pl.program_id(0)
        pl.when(fast_ref[0] == 1)(lambda: run(False, step, *refs))
        pl.when(fast_ref[0] != 1)(lambda: run(True, step, *refs))

    rows = pl.BlockSpec((Q, D), lambda i, f: (i, 0))
    outs = [SDS((L, D), BF), SDS((L, D), BF), SDS((nc, D, 128), F32)]
    grid_spec = pltpu.PrefetchScalarGridSpec(
        num_scalar_prefetch=1, grid=(nc,),
        in_specs=[rows] * 4 + [pl.BlockSpec((2, D), lambda i, f: (0, 0)), pl.BlockSpec((1, 128), lambda i, f: (0, 0))],
        out_specs=[rows, rows, pl.BlockSpec((1, D, 128), lambda i, f: (i, 0, 0))],
        scratch_shapes=[pltpu.VMEM((D, 128), F32), pltpu.VMEM((Q, 128), F32), pltpu.VMEM((Q, 128), F32),
                        pltpu.VMEM((Q, Q), F32)])
    return pl.pallas_call(body, grid_spec=grid_spec, name="hg_fwd", out_shape=outs,
                          compiler_params=_cparams())(fast, hq, hf, hi, hg, hlb, nw)


def _hg_bwd(dob, o, hq, hf, hi, hg, states, hlb, nw, fast, phases=()):
    L = dob.shape[0]
    nc = L // Q

    def run(exact, step, dob_ref, o_ref, hq_ref, hf_ref, hi_ref, hg_ref, st_ref, hlb_ref, nw_ref,
            dhq_ref, dhf_ref, dhi_ref, dhg_ref, acc_ref, gs_ref, *tmp):
        @pl.when(step == 0)
        def _():
            acc_ref[...] = jnp.zeros_like(acc_ref)
            gs_ref[...] = jnp.zeros_like(gs_ref)

        hqv = hq_ref[...].astype(F32)
        qf, kf, fg, sg, lb, sm1, bcum = _hg_gates(hqv, hf_ref[...], hlb_ref[...])
        gate = hg_ref[...].astype(F32)
        sgg = _sigmoid(gate)
        nwv = nw_ref[...]
        tri_t = (_iota((Q, Q), 1) >= _iota((Q, Q), 0)).astype(BF)
        ones8 = jnp.ones((8, 128), BF)
        heads = [slice(128 * h, 128 * h + 128) for h in range(NH_HG)]
        row_last = _iota((Q, 1), 0) == Q - 1
        dobs, dnws = [], []
        for sl in heads:
            gt, sgt = gate[:, sl], sgg[:, sl]
            _, n, r = _rms(o_ref[:, sl].astype(F32), nwv)
            dobv = dob_ref[:, sl].astype(F32)
            dhg_ref[:, sl] = (dobv * n * nwv * sgt * (1.0 + gt * (1.0 - sgt))).astype(BF)
            do, dw = _rms_bwd(dobv * gt * sgt, n, r, nwv)
            dnws.append(_colsum(dw))
            dobs.append(do.astype(BF))
        causal = _iota((Q, Q), 1) <= _iota((Q, Q), 0)
        if exact:
            intra = [(_hg_att_exact(bcum[:, sl], qf[:, sl], kf[:, sl], *tmp[:3]),) for sl in heads]
        else:
            intra = [_hg_intra(bcum[:, sl], qf[:, sl], kf[:, sl]) for sl in heads]
        states = [(st_ref[0, sl, :], gs_ref[sl, :]) for sl in heads]
        das = [jnp.where(causal, _dot_nt(dob_h, hi_ref[:, sl]), 0.0) for dob_h, sl in zip(dobs, heads)]
        dqhats = [_dot_nt(dob_h, s) for dob_h, (s, _) in zip(dobs, states)]
        dkhats = [_dot_nt(hi_ref[:, sl], gst) for sl, (_, gst) in zip(heads, states)]
        if not exact:
            dqts = [jnp.dot(da.astype(BF), it[2], preferred_element_type=F32) for da, it in zip(das, intra)]
            dkts = [lax.dot_general(da.astype(BF), it[1], (((0,), (0,)), ((), ())), preferred_element_type=F32)
                    for da, it in zip(das, intra)]
        dqs, dks, dgls = [], [], []
        for h, sl in enumerate(heads):
            b, q, k = bcum[:, sl], qf[:, sl], kf[:, sl]
            att = intra[h][0]
            s, gst = states[h]
            dob_h, dqhat, dkhat = dobs[h], dqhats[h], dkhats[h]
            eb = jnp.exp(b)
            blast = b[Q - 1:Q, :]
            ekl = jnp.exp(blast - b)
            qhat, khat = q * eb, k * ekl
            dhi_ref[:, sl] = (_dot_tn(att, dob_h) + _dot(khat, gst)).astype(BF)
            if exact:
                dq_i, dk_i, db = _hg_att_exact_bwd(das[h], b, q, k, *tmp)
            else:
                dq_i, dk_i, db = _hg_intra_bwd(dqts[h], dkts[h], *intra[h][1:4])
            dqs.append(dq_i + dqhat * eb)
            dks.append(dk_i + dkhat * ekl)
            qhat_r, khat_r = qhat.astype(BF).astype(F32), khat.astype(BF).astype(F32)
            decay_row = sum(_dot_nt(ones8, part) for part in _split(gst * s, 2))[0:1, :]
            dblast = _colsum(dkhat * khat_r) + decay_row * jnp.exp(blast)
            dgls.append(db + qhat_r * dqhat - khat_r * dkhat + jnp.where(row_last, dblast, 0.0))
            gs_ref[sl, :] = _dot_tn(qhat, dob_h) + gst * jnp.exp(b.T[:, Q - 1:Q])
        dq, dk, db = (jnp.concatenate(t, axis=1) for t in (dqs, dks, dgls))
        dgl = _dot_sel(tri_t, db, 2)
        sgq = _sigmoid(hqv)
        dhq_ref[...] = (dq * sgq * (1.0 + hqv * (1.0 - sgq))).astype(BF)
        dfg = dgl / fg - dk
        dhf_ref[...] = (dfg * (1.0 - lb) * sg * (1.0 - sg)).astype(BF)
        acc_ref[0:1, :] += _colsum(dfg * (1.0 - sg))
        acc_ref[1:2, :] += jnp.concatenate(dnws, axis=1)

        @pl.when(step == nc - 1)
        def _():
            dlb = acc_ref[0:1, :] * lb * sm1
            acc_ref[2:3, :] = dlb
            acc_ref[3:4, :] = -dlb
            tot = acc_ref[1:2, 0:128]
            for h in range(1, NH_HG):
                tot = tot + acc_ref[1:2, 128 * h:128 * h + 128]
            acc_ref[4:5, 0:128] = tot

    def body(fast_ref, *refs):
        step = pl.program_id(0)
        pl.when(fast_ref[0] == 1)(lambda: run(False, step, *refs))
        pl.when(fast_ref[0] != 1)(lambda: run(True, step, *refs))

    rev = pl.BlockSpec((Q, D), lambda i, f: (nc - 1 - i, 0))
    outs = [SDS((L, D), BF)] * 4 + [SDS((8, D), F32)]
    return _call(
        body, (fast, dob, o, hq, hf, hi, hg, states, hlb, nw), name="hg_bwd", grid=(nc,), prefetch=1,
        in_specs=[rev] * 6 + [pl.BlockSpec((1, D, 128), lambda i, f: (nc - 1 - i, 0, 0)),
                              pl.BlockSpec((2, D), lambda i, f: (0, 0)), pl.BlockSpec((1, 128), lambda i, f: (0, 0))],
        out_specs=[rev] * 4 + [pl.BlockSpec((8, D), lambda i, f: (0, 0))], out_shape=outs,
        scratch_shapes=[pltpu.VMEM((D, 128), F32), pltpu.VMEM((Q, 128), F32), pltpu.VMEM((Q, 128), F32),
                        pltpu.VMEM((Q, Q), F32), pltpu.VMEM((Q, 128), F32), pltpu.VMEM((Q, 128), F32)], phases=phases)


def _place():
    return lax.axis_index("x"), lax.axis_index("y"), lax.axis_index("c")


def _phase_io(phase):
    kind, arrays, halves = phase
    n = len(arrays)
    dma = pltpu.SemaphoreType.DMA
    if kind == "gather":
        outs = [SDS((8,) + a.shape if hc is None else (4,) + a.shape, a.dtype) for a, hc in zip(arrays, halves)]
        return outs, [dma((7 * n,)), dma((7 * n,)), dma((n,))], {}
    if kind == "sibling":
        return [SDS((4, g.shape[1], hc), g.dtype) for g, hc in zip(arrays, halves)], [dma((n,)), dma((n,))], {}
    if kind == "chips":
        return [SDS((3,) + p.shape[1:], p.dtype) for p in arrays], [dma((3 * n,)), dma((3 * n,))], {}
    assert kind == "swap"
    return [SDS(b.shape, b.dtype) for b in arrays], [dma((n,)), dma((n,))], {a: a for a in range(n)}


def _gather_events(ins, outs, sems, halves):
    send_sems, recv_sems, local_sems = sems
    n = len(ins)

    def parts(a):
        x, y, c = _place()
        hc = halves[a]
        me, sibling = (x, y, c), (x, y, 1 - c)
        chips = [(1 - x, y), (x, 1 - y), (1 - x, 1 - y)]

        def slot(p):
            if hc is None:
                return outs[a].at[4 * p[0] + 2 * p[1] + p[2]]
            return outs[a].at[2 * p[0] + p[1], :, pl.ds(p[2] * hc, hc)]

        own = ins[a] if hc is None else ins[a].at[:, pl.ds(c * hc, hc)]

        def copy(k, piece, to, src=None):
            return pltpu.make_async_remote_copy(
                src_ref=slot(piece) if src is None else src, dst_ref=slot(piece),
                send_sem=send_sems.at[7 * a + k], recv_sem=recv_sems.at[7 * a + k], device_id=to, device_id_type=MESH)

        return dict(
            mine=lambda: pltpu.make_async_copy(own, slot(me), local_sems.at[a]),
            starts=lambda: [copy(0, me, sibling, src=own)] + [copy(1 + j, me, (*chip, c), src=own)
                                                               for j, chip in enumerate(chips)],
            arrive=lambda: [copy(1 + j, (*chip, c), me) for j, chip in enumerate(chips)],
            passed=lambda: [copy(4 + j, (*chip, c), sibling) for j, chip in enumerate(chips)],
            from_sibling=lambda: [copy(0, sibling, me)] + [copy(4 + j, (*chip, 1 - c), me)
                                                            for j, chip in enumerate(chips)])

    def first():
        for a in range(n):
            p = parts(a)
            p["mine"]().start()
            for cp in p["starts"]():
                cp.start()

    def mid():
        for a in range(n):
            p = parts(a)
            for cp_in, cp_out in zip(p["arrive"](), p["passed"]()):
                cp_in.wait_recv()
                cp_out.start()

    def last():
        for a in range(n):
            p = parts(a)
            for cp in p["from_sibling"]():
                cp.wait_recv()
            for cp in p["starts"]() + p["passed"]():
                cp.wait_send()
            p["mine"]().wait()

    return dict(first=first, mid=mid, last=last)


def _exchange_events(kind, ins, outs, sems, halves):
    send_sems, recv_sems = sems
    n = len(outs)

    def copies():
        x, y, c = _place()
        if kind == "sibling":
            return [pltpu.make_async_remote_copy(
                src_ref=ins[a].at[:, :, pl.ds((1 - c) * halves[a], halves[a])], dst_ref=outs[a],
                send_sem=send_sems.at[a], recv_sem=recv_sems.at[a], device_id=(x, y, 1 - c), device_id_type=MESH)
                for a in range(n)]
        chips = [(1 - x, y), (x, 1 - y), (1 - x, 1 - y)]
        return [pltpu.make_async_remote_copy(
            src_ref=ins[a].at[2 * px + py], dst_ref=outs[a].at[k], send_sem=send_sems.at[3 * a + k],
            recv_sem=recv_sems.at[3 * a + k], device_id=(px, py, c), device_id_type=MESH)
            for a in range(n) for k, (px, py) in enumerate(chips)]

    def first():
        for cp in copies():
            cp.start()

    def last():
        for cp in copies():
            cp.wait()

    return dict(first=first, last=last)


def _swap_events(outs, sems, halves):
    send_sems, recv_sems = sems
    n = len(outs)

    def copy(a, landing):
        x, y, c = _place()
        cols = lambda which: outs[a].at[:, pl.ds(which * halves[a], halves[a])]
        return pltpu.make_async_remote_copy(
            src_ref=cols(c), dst_ref=cols(1 - c) if landing else cols(c), send_sem=send_sems.at[a],
            recv_sem=recv_sems.at[a], device_id=(x, y, 1 - c), device_id_type=MESH)

    def first():
        for a in range(n):
            copy(a, False).start()

    def last():
        for a in range(n):
            copy(a, True).wait_recv()
        for a in range(n):
            copy(a, False).wait_send()

    return dict(first=first, last=last)


def _phase_events(phase, ins, outs, sems):
    kind, _, halves = phase
    if kind == "gather":
        return _gather_events(ins, outs, sems, halves)
    if kind == "swap":
        return _swap_events(outs, sems, halves)
    return _exchange_events(kind, ins, outs, sems, halves)


def _split_refs(refs, counts):
    out, at = [], 0
    for c in counts:
        out.append(list(refs[at:at + c]))
        at += c
    return out


def _comm_plumbing(phases, first_in, first_out):
    ios = [_phase_io(p) for p in phases]
    arrays = [a for p in phases for a in p[1]]
    out_shape = [o for io in ios for o in io[0]]
    sem_shapes = [s for io in ios for s in io[1]]
    aliases, ai, ao = {}, first_in, first_out
    for p, io in zip(phases, ios):
        aliases.update({ai + k: ao + v for k, v in io[2].items()})
        ai, ao = ai + len(p[1]), ao + len(io[0])

    def events(cins, couts, sems):
        evs = [_phase_events(p, i, o, s) for p, i, o, s in zip(
            phases, _split_refs(cins, [len(p[1]) for p in phases]), _split_refs(couts, [len(io[0]) for io in ios]),
            _split_refs(sems, [len(io[1]) for io in ios]))]

        def run(key):
            for ev in evs:
                if key in ev:
                    ev[key]()

        return {key: (lambda key=key: run(key)) for key in ("first", "mid", "last")}

    def regroup(flat):
        return _split_refs(flat, [len(io[0]) for io in ios])

    return arrays, out_shape, sem_shapes, aliases, events, regroup


def _run_phases(phases, name):
    arrays, out_shape, sem_shapes, aliases, events, regroup = _comm_plumbing(phases, 0, 0)

    def body(*refs):
        cins, couts, sems = _split_refs(refs, [len(arrays), len(out_shape), len(sem_shapes)])
        ev = events(cins, couts, sems)
        for key in ("first", "mid", "last"):
            ev[key]()

    outs = pl.pallas_call(
        body, name=name, in_specs=[ANY] * len(arrays), out_specs=[ANY] * len(out_shape), out_shape=out_shape,
        scratch_shapes=sem_shapes, input_output_aliases=aliases)(*arrays)
    return regroup(outs)


def _call(body, args, *, name, grid, in_specs, out_specs, out_shape, scratch_shapes=(), prefetch=0, phases=(),
          mid_step=None):
    steps = grid[0]
    arrays, c_shape, sem_shapes, aliases, events, regroup = _comm_plumbing(
        phases, prefetch + len(in_specs), len(out_specs))
    counts = [prefetch, len(in_specs), len(arrays), len(out_specs), len(c_shape), len(scratch_shapes), len(sem_shapes)]

    def wrapped(*refs):
        pre, ins, cins, outs, couts, scratch, sems = _split_refs(refs, counts)
        if not phases:
            return body(*pre, *ins, *outs, *scratch)
        step = pl.program_id(0)
        ev = events(cins, couts, sems)
        pl.when(step == 0)(ev["first"])
        body(*pre, *ins, *outs, *scratch)
        pl.when(step == (steps // 2 if mid_step is None else mid_step))(ev["mid"])
        pl.when(step == steps - 1)(ev["last"])

    grid_spec = pltpu.PrefetchScalarGridSpec(
        num_scalar_prefetch=prefetch, grid=grid, in_specs=list(in_specs) + [ANY] * len(arrays),
        out_specs=list(out_specs) + [ANY] * len(c_shape), scratch_shapes=list(scratch_shapes) + sem_shapes)
    outs = pl.pallas_call(
        wrapped, grid_spec=grid_spec, name=name, out_shape=list(out_shape) + c_shape, input_output_aliases=aliases,
        compiler_params=_cparams())(*args, *arrays)
    return list(outs[:len(out_specs)]), regroup(outs[len(out_specs):])


def _tile(rows, cols, nbuf):
    budget = (VMEM_LIMIT // 3) // (2 * nbuf * 4)
    if rows % 8 == 0:
        cands = [t for t in range(8, rows + 1, 8) if rows % t == 0 and t * cols <= budget]
        pref = [t for t in cands if t % 16 == 0]
        return (max(pref) if pref else max(cands) if cands else 8), cols
    cands = [t for t in range(128, cols + 1, 128) if cols % t == 0 and rows * t <= budget]
    return rows, (max(cands) if cands else 128)


def _chip_sum(g, from_sib, place, name):
    _, rows, hc = from_sib.shape
    tr, tc = _tile(rows, hc, 4)
    ni, nj = rows // tr, hc // tc

    def body(p_ref, g_ref, s_ref, hb_ref, own_ref):
        s = g_ref[...] + s_ref[...]
        hb_ref[...] = s.astype(BF)

        @pl.when(pl.program_id(2) == p_ref[1])
        def _():
            own_ref[...] = s

    grid_spec = pltpu.PrefetchScalarGridSpec(
        num_scalar_prefetch=1, grid=(ni, nj, 4),
        in_specs=[pl.BlockSpec((None, tr, tc), lambda i, j, k, p: (k, i, p[0] * nj + j)),
                  pl.BlockSpec((None, tr, tc), lambda i, j, k, p: (k, i, j))],
        out_specs=[pl.BlockSpec((None, tr, tc), lambda i, j, k, p: (k, i, j)),
                   pl.BlockSpec((tr, tc), lambda i, j, k, p: (i, j))])
    return pl.pallas_call(
        body, grid_spec=grid_spec, name=name, out_shape=[SDS((4, rows, hc), BF), SDS((rows, hc), F32)],
        compiler_params=pltpu.CompilerParams(dimension_semantics=("arbitrary",) * 3,
                                             vmem_limit_bytes=VMEM_LIMIT))(place, g, from_sib)


def _total(own, parts, place, name):
    rows, hc = own.shape
    tr, tc = _tile(rows, hc, 5)
    ni, nj = rows // tr, hc // tc

    def body(p_ref, own_ref, parts_ref, o_ref):
        s = own_ref[...]
        for k in range(3):
            s = s + parts_ref[k].astype(F32)
        o_ref[...] = s

    grid_spec = pltpu.PrefetchScalarGridSpec(
        num_scalar_prefetch=1, grid=(ni, nj),
        in_specs=[pl.BlockSpec((tr, tc), lambda i, j, p: (i, j)),
                  pl.BlockSpec((3, tr, tc), lambda i, j, p: (0, i, j))],
        out_specs=pl.BlockSpec((tr, tc), lambda i, j, p: (i, p[0] * nj + j)))
    return pl.pallas_call(
        body, grid_spec=grid_spec, name=name, out_shape=SDS((rows, 2 * hc), F32),
        compiler_params=pltpu.CompilerParams(dimension_semantics=("arbitrary",) * 2,
                                             vmem_limit_bytes=VMEM_LIMIT))(place, own, parts)


def _sum8(parts, name):
    R = parts.shape[1]

    def body(p_ref, o_ref):
        s = p_ref[0]
        for k in range(1, 8):
            s = s + p_ref[k]
        o_ref[...] = s

    return pl.pallas_call(
        body, grid=(1,), name=name, in_specs=[_const((8, R, 128))], out_specs=_const((R, 128)),
        out_shape=SDS((R, 128), F32), compiler_params=_cparams())(parts)


def _adamw(w, g, m, v, name):
    _, R, C = w.shape
    tr, tc = _tile(R, C, 7)
    c1 = 1.0 / (1.0 - ADAM_B1 ** ADAM_STEP)
    c2 = 1.0 / (1.0 - ADAM_B2 ** ADAM_STEP)

    def body(w_ref, g_ref, m_ref, v_ref, d_ref, nm_ref, nv_ref):
        gv = g_ref[...]
        nm = ADAM_B1 * m_ref[...] + (1.0 - ADAM_B1) * gv
        nv = ADAM_B2 * v_ref[...] + (1.0 - ADAM_B2) * gv * gv
        nm_ref[...] = nm
        nv_ref[...] = nv
        d_ref[...] = -ADAM_LR * ((nm * c1) / (jnp.sqrt(nv * c2) + ADAM_EPS) + ADAM_WD * w_ref[...])

    blk3 = pl.BlockSpec((None, tr, tc), lambda i, j: (0, i, j))
    return pl.pallas_call(
        body, grid=(R // tr, C // tc), name=name,
        in_specs=[blk3, pl.BlockSpec((tr, tc), lambda i, j: (i, j)), blk3, blk3], out_specs=[blk3] * 3,
        out_shape=[SDS((1, R, C), F32)] * 3,
        compiler_params=pltpu.CompilerParams(dimension_semantics=("arbitrary",) * 2,
                                             vmem_limit_bytes=VMEM_LIMIT))(w, g, m, v)


def _pack_small(parts):
    rows = []
    for p in parts:
        p = p.reshape(-1)
        rows.append(jnp.pad(p, (0, (-p.shape[0]) % 128)).reshape(-1, 128))
    out = jnp.concatenate(rows, axis=0)
    return jnp.pad(out, ((0, (-out.shape[0]) % 8), (0, 0)))


def _unpack_small(packed, shapes):
    out, row = [], 0
    for shp in shapes:
        n = 1
        for s in shp:
            n *= s
        nr = -(-n // 128)
        out.append(packed[row:row + nr].reshape(-1)[:n].reshape(shp))
        row += nr
    return out


def _pad_lanes(v, n=128):
    return jnp.pad(v, ((0, 0), (0, n - v.shape[1])))


GROUP_FFN = ("ffn_w_gate", "ffn_w_up", "ffn_w_down")
GROUP_ATTN = ("w_out", "xa_wq", "xa_wkv", "xa_wo")


def kernel(x, mem, norm_mix_w, w_in, conv_w, conv_b, dt_bias, a_log, d_skip, ssd_norm_w, hg_lower_bounds, hg_norm_w, w_out, norm_xa_w, norm_mem_w, xa_wq, xa_wkv, xa_wo, norm_ffn_w, ffn_w_gate, ffn_w_up, ffn_w_down, norm_final_w, loss_target, m_norm_mix_w, m_w_in, m_conv_w, m_conv_b, m_dt_bias, m_a_log, m_d_skip, m_ssd_norm_w, m_hg_lower_bounds, m_hg_norm_w, m_w_out, m_norm_xa_w, m_norm_mem_w, m_xa_wq, m_xa_wkv, m_xa_wo, m_norm_ffn_w, m_ffn_w_gate, m_ffn_w_up, m_ffn_w_down, m_norm_final_w, v_norm_mix_w, v_w_in, v_conv_w, v_conv_b, v_dt_bias, v_a_log, v_d_skip, v_ssd_norm_w, v_hg_lower_bounds, v_hg_norm_w, v_w_out, v_norm_xa_w, v_norm_mem_w, v_xa_wq, v_xa_wkv, v_xa_wo, v_norm_ffn_w, v_ffn_w_gate, v_ffn_w_up, v_ffn_w_down, v_norm_final_w):
    w = dict(norm_mix_w=norm_mix_w, w_in=w_in, conv_w=conv_w, conv_b=conv_b, dt_bias=dt_bias, a_log=a_log, d_skip=d_skip,
             ssd_norm_w=ssd_norm_w, hg_lower_bounds=hg_lower_bounds, hg_norm_w=hg_norm_w, w_out=w_out,
             norm_xa_w=norm_xa_w, norm_mem_w=norm_mem_w, xa_wq=xa_wq, xa_wkv=xa_wkv, xa_wo=xa_wo, norm_ffn_w=norm_ffn_w,
             ffn_w_gate=ffn_w_gate, ffn_w_up=ffn_w_up, ffn_w_down=ffn_w_down, norm_final_w=norm_final_w)
    m = dict(norm_mix_w=m_norm_mix_w, w_in=m_w_in, conv_w=m_conv_w, conv_b=m_conv_b, dt_bias=m_dt_bias, a_log=m_a_log,
             d_skip=m_d_skip, ssd_norm_w=m_ssd_norm_w, hg_lower_bounds=m_hg_lower_bounds, hg_norm_w=m_hg_norm_w,
             w_out=m_w_out, norm_xa_w=m_norm_xa_w, norm_mem_w=m_norm_mem_w, xa_wq=m_xa_wq, xa_wkv=m_xa_wkv,
             xa_wo=m_xa_wo, norm_ffn_w=m_norm_ffn_w, ffn_w_gate=m_ffn_w_gate, ffn_w_up=m_ffn_w_up,
             ffn_w_down=m_ffn_w_down, norm_final_w=m_norm_final_w)
    v = dict(norm_mix_w=v_norm_mix_w, w_in=v_w_in, conv_w=v_conv_w, conv_b=v_conv_b, dt_bias=v_dt_bias, a_log=v_a_log,
             d_skip=v_d_skip, ssd_norm_w=v_ssd_norm_w, hg_lower_bounds=v_hg_lower_bounds, hg_norm_w=v_hg_norm_w,
             w_out=v_w_out, norm_xa_w=v_norm_xa_w, norm_mem_w=v_norm_mem_w, xa_wq=v_xa_wq, xa_wkv=v_xa_wkv,
             xa_wo=v_xa_wo, norm_ffn_w=v_norm_ffn_w, ffn_w_gate=v_ffn_w_gate, ffn_w_up=v_ffn_w_up,
             ffn_w_down=v_ffn_w_down, norm_final_w=v_norm_final_w)
    xi, yi, ci = _place()
    chip = 2 * xi + yi
    place = jnp.stack([ci, chip]).astype(jnp.int32)

    def shard(t, name):
        return jnp.swapaxes(t[name], 1, 2) if name in TRANSPOSED else t[name]

    wsh = {name: shard(w, name) for name in BIG}
    half = {name: wsh[name].shape[2] // 2 for name in BIG}
    payload = {name: wsh[name][0].astype(BF) for name in BIG}
    ws = {name: w[name] for name in SMALL}
    xs, mems, tgt = x[0], mem[0], loss_target[0]

    def chip_sums(names, grads, from_sib):
        return [_chip_sum(grads[n], s, place, "grads_chip_sum_" + n) for n, s in zip(names, from_sib)]

    def totals(names, sums, others):
        return [_total(own, o, place, "grads_total_" + n) for n, (_, own), o in zip(names, sums, others)]

    ((w_in4, conv_all),) = _run_phases([("gather", [payload["w_in"], conv_w[0]], [half["w_in"], None])], "gather_w_in")
    w_in_t = w_in4.reshape(N_IN, D)
    ws["conv_w"] = conv_all[0::2].transpose(1, 0, 2).reshape(1, 4, 1536)
    rest = [n for n in BIG if n != "w_in"]
    (h0, z, xbc, hq, hf, hi, hg, dtr), (gathered,) = _in_proj(
        xs, ws["norm_mix_w"], w_in_t, phases=[("gather", [payload[n] for n in rest], [half[n] for n in rest])])
    wg = dict(zip(rest, gathered))
    wg_t, wu_t = wg["ffn_w_gate"].reshape(FFN, D), wg["ffn_w_up"].reshape(FFN, D)
    wd = wg["ffn_w_down"].reshape(FFN, D)
    w_out_f = wg["w_out"].reshape(2 * D, D)
    wq, wo = wg["xa_wq"].reshape(D, D), wg["xa_wo"].reshape(D, D)
    dtb, alog = _pad_lanes(ws["dt_bias"]), _pad_lanes(ws["a_log"])
    dskip_full = jnp.repeat(ws["d_skip"], SSD_P, axis=1)
    cw, conv_bias = ws["conv_w"][0], ws["conv_b"]
    hlb = ws["hg_lower_bounds"]
    hg_fast = (jnp.min(jax.nn.softmax(hlb, axis=0)[0]) >= HG_LB_FLOOR).astype(jnp.int32).reshape(1)

    ya, yssd, u, st_ssd = _ssd_fwd(xbc, dtr, z, cw, conv_bias, dtb, alog, dskip_full, ws["ssd_norm_w"])
    ob, ohg, st_hg = _hg_fwd(hq, hf, hi, hg, hlb, ws["hg_norm_w"], hg_fast)
    kmem, vmem = _mem_kv(mems, ws["norm_mem_w"], wg["xa_wkv"])
    x1, x2, hxa, q, ox = _attn_fwd(xs, ya, ob, w_out_f, ws["norm_xa_w"], wq, kmem, vmem, wo)
    nfin = ws["norm_final_w"].reshape(1, D)
    dx2, hffn, act, dx3, dg, du, acc_f = _ffn_loss(x2, tgt, ws["norm_ffn_w"], nfin, wg_t, wu_t, wd)

    gb = {"ffn_w_gate": _matmul_tn(dg, hffn, "gw_gate").reshape(4, FFN // 4, D),
          "ffn_w_up": _matmul_tn(du, hffn, "gw_up").reshape(4, FFN // 4, D),
          "ffn_w_down": _matmul_tn(act, dx3, "gw_down").reshape(4, FFN // 4, D)}
    (dx1, dya, dob, dq, dk, dv, acc_a), (sib_ffn,) = _attn_bwd(
        dx2, x1, q, kmem, vmem, ws["norm_xa_w"], wq, wo, w_out_f,
        phases=[("sibling", [gb[n] for n in GROUP_FFN], [half[n] for n in GROUP_FFN])])
    sums_ffn = chip_sums(GROUP_FFN, gb, sib_ffn)
    g_nmem, gb["xa_wkv"] = _mem_kv_bwd(mems, ws["norm_mem_w"], wg["xa_wkv"], dk, dv)
    gb["w_out"] = _matmul_tn_pair(ya, ob, dx1, "gw_out").reshape(4, D // 2, D)
    gb["xa_wq"] = _matmul_tn(hxa, dq, "gw_q").reshape(4, D // 4, D)
    gb["xa_wo"] = _matmul_tn(ox, dx2, "gw_o").reshape(4, D // 4, D)
    (dhq, dhf, dhi, dhg, acc_h), (others_ffn, sib_attn) = _hg_bwd(
        dob, ohg, hq, hf, hi, hg, st_hg, hlb, ws["hg_norm_w"], hg_fast,
        phases=[("chips", [hb for hb, _ in sums_ffn], None),
                ("sibling", [gb[n] for n in GROUP_ATTN], [half[n] for n in GROUP_ATTN])])
    red_ffn = totals(GROUP_FFN, sums_ffn, others_ffn)
    sums_attn = chip_sums(GROUP_ATTN, gb, sib_attn)
    dz, dxbc, ddt, gconv, ghead, glane = _ssd_bwd(dya, yssd, z, u, xbc, dtr, st_ssd, cw, dtb, alog, dskip_full,
                                                  ws["ssd_norm_w"])
    gx, acc_i = _in_proj_bwd(xs, dx1, dz, dxbc, dhq, dhf, dhi, dhg, ddt, ws["norm_mix_w"], w_in_t)
    (gw_in_t,), (g_ffn, others_attn) = _gw_in(
        h0, dz, dxbc, ddt, dhq, dhf, dhi, dhg,
        phases=[("swap", red_ffn, [half[n] for n in GROUP_FFN]), ("chips", [hb for hb, _ in sums_attn], None)])
    red_attn = totals(GROUP_ATTN, sums_attn, others_attn)
    gb["w_in"] = gw_in_t.reshape(4, N_IN // 4, D)
    g_attn, (sib_in,) = _run_phases([("swap", red_attn, [half[n] for n in GROUP_ATTN]),
                                     ("sibling", [gb["w_in"]], [half["w_in"]])], "grads_w_in_to_sibling")
    sums_in = chip_sums(("w_in",), gb, [sib_in])
    ((others_in,),) = _run_phases([("chips", [sums_in[0][0]], None)], "grads_w_in_to_chips")
    red_in = totals(("w_in",), sums_in, [others_in])

    gs = {
        "norm_mix_w": acc_i[0:1], "conv_w": gconv[0:4][None], "conv_b": gconv[4:5],
        "dt_bias": ghead[0:1, :NH_SSD], "a_log": ghead[2:3, :NH_SSD], "d_skip": ghead[3:4, :NH_SSD],
        "ssd_norm_w": glane[1:2], "hg_lower_bounds": acc_h[2:4], "hg_norm_w": acc_h[4:5, :128],
        "norm_xa_w": acc_a[0:1], "norm_mem_w": g_nmem, "norm_ffn_w": acc_f[2:3], "norm_final_w": acc_f[1],
    }
    loss = (0.5 / D) * jnp.sum(acc_f[0])
    small_parts = [gs[name] for name in SMALL] + [loss.reshape(1)]
    small_shapes = [gs[name].shape for name in SMALL] + [(1,)]
    (g_in,), (packed,) = _run_phases([("swap", red_in, [half["w_in"]]),
                                      ("gather", [_pack_small(small_parts)], [None])], "grads_finish")
    g_big = dict(zip(GROUP_FFN + GROUP_ATTN + ("w_in",), g_ffn + g_attn + [g_in]))
    small = _unpack_small(_sum8(packed, "small_total"), small_shapes)
    g_small = dict(zip(SMALL, small[:-1]))
    loss_all = small[-1][0]
    g_small["conv_w"] = lax.dynamic_slice_in_dim(g_small["conv_w"], chip * 384, 384, 2)

    grads, delta, new_m, new_v = {}, {}, {}, {}
    for name in BIG:
        outs = (g_big[name][None],) + tuple(_adamw(wsh[name], g_big[name], shard(m, name), shard(v, name),
                                                   "adamw_" + name))
        if name in TRANSPOSED:
            outs = tuple(jnp.swapaxes(o, 1, 2) for o in outs)
        grads[name], delta[name], new_m[name], new_v[name] = outs
    shapes = [w[name].shape for name in SMALL]
    packs = [_pack_small([t[name] for name in SMALL]) for t in (w, g_small, m, v)]
    outs = _adamw(packs[0][None], packs[1], packs[2][None], packs[3][None], "adamw_small")
    for name, g_, d_, nm_, nv_ in zip(SMALL, [g_small[n] for n in SMALL], *[_unpack_small(o[0], shapes) for o in outs]):
        grads[name] = g_.reshape(w[name].shape)
        delta[name], new_m[name], new_v[name] = d_, nm_, nv_

    return (loss_all, gx[None], *[grads[n] for n in WEIGHTS], *[delta[n] for n in WEIGHTS],
            *[new_m[n] for n in WEIGHTS], *[new_v[n] for n in WEIGHTS])
```

```python
import jax
import jax.numpy as jnp
from jax import lax
from jax.experimental import pallas as pl
from jax.experimental.pallas import tpu as pltpu

F32 = jnp.float32
BF = jnp.bfloat16
MESH = pl.DeviceIdType.MESH
SDS = jax.ShapeDtypeStruct
ANY = pl.BlockSpec(memory_space=pl.ANY)

D = 1024
EPS = 1e-6
NH_SSD = 16
SSD_P = 64
NH_HG = 8
Q = 128
CH = 2
SUB = 32
NSUB = Q // SUB
HG_LB_FLOOR = 1e-2
XA_HEADS = 4
XA_HD = 256
MEM_LEN = 256
FFN = 2816
TL = 512
TL_FFN = 256
VMEM_LIMIT = 56 << 20
MATMUL_VMEM = 40 << 20

N_IN = 6672
Z0, XBC0, DT0, HQ0, HF0, HI0, HG0 = 0, 1024, 2560, 2576, 3600, 4624, 5648

ADAM_LR, ADAM_B1, ADAM_B2, ADAM_EPS, ADAM_WD, ADAM_STEP = 0.001, 0.9, 0.999, 1e-08, 0.01, 10

BIG = ("w_in", "w_out", "xa_wq", "xa_wkv", "xa_wo", "ffn_w_gate", "ffn_w_up", "ffn_w_down")
TRANSPOSED = ("w_in", "ffn_w_gate", "ffn_w_up")
SMALL = ("norm_mix_w", "conv_w", "conv_b", "dt_bias", "a_log", "d_skip", "ssd_norm_w", "hg_lower_bounds",
         "hg_norm_w", "norm_xa_w", "norm_mem_w", "norm_ffn_w", "norm_final_w")
WEIGHTS = ("norm_mix_w", "w_in", "conv_w", "conv_b", "dt_bias", "a_log", "d_skip", "ssd_norm_w", "hg_lower_bounds",
           "hg_norm_w", "w_out", "norm_xa_w", "norm_mem_w", "xa_wq", "xa_wkv", "xa_wo", "norm_ffn_w", "ffn_w_gate",
           "ffn_w_up", "ffn_w_down", "norm_final_w")


def _cparams():
    return pltpu.CompilerParams(dimension_semantics=("arbitrary",), vmem_limit_bytes=VMEM_LIMIT)


def _const(shape):
    return pl.BlockSpec(shape, lambda i: (0,) * len(shape))


def _resident(shape):
    return pl.BlockSpec(shape, lambda i: (0,) * len(shape), pipeline_mode=pl.Buffered(1))


def _rows(tl, n):
    return pl.BlockSpec((tl, n), lambda i: (i, 0))


def _dot(a, b):
    return jnp.dot(a.astype(BF), b.astype(BF), preferred_element_type=F32)


def _dot_nt(a, b):
    return lax.dot_general(a.astype(BF), b.astype(BF), (((1,), (1,)), ((), ())), preferred_element_type=F32)


def _dot_tn(a, b):
    return lax.dot_general(a.astype(BF), b.astype(BF), (((0,), (0,)), ((), ())), preferred_element_type=F32)


def _split(v, passes):
    parts, rest = [], v
    for p in range(passes):
        hi = rest.astype(BF)
        parts.append(hi)
        if p + 1 < passes:
            rest = rest - hi.astype(F32)
    return parts


def _sel_dot(a, sel, passes=3):
    sb = sel.astype(BF)
    out = None
    for part in _split(a, passes):
        t = jnp.dot(part, sb, preferred_element_type=F32)
        out = t if out is None else out + t
    return out


def _dot_sel(sel, b, passes=3):
    sb = sel.astype(BF)
    out = None
    for part in _split(b, passes):
        t = jnp.dot(sb, part, preferred_element_type=F32)
        out = t if out is None else out + t
    return out


def _iota(shape, dim):
    return lax.broadcasted_iota(jnp.int32, shape, dim)


def _sigmoid(v):
    return 0.5 * jnp.tanh(0.5 * v) + 0.5


def _rms(v, w):
    r = lax.rsqrt(jnp.mean(v * v, axis=-1, keepdims=True) + EPS)
    n = v * r
    return n * w, n, r


def _rms_bwd(dy, n, r, w):
    dn = dy * w
    return r * (dn - n * jnp.mean(dn * n, axis=-1, keepdims=True)), dy * n


def _colsum(v):
    return jnp.sum(v, axis=0, keepdims=True)


def _zero_first(*refs):
    @pl.when(pl.program_id(0) == 0)
    def _():
        for r in refs:
            r[...] = jnp.zeros_like(r)


def _in_proj(x, nw, wt, phases=()):
    L = x.shape[0]
    tl = min(TL, L)

    def body(x_ref, nw_ref, w_ref, h0_ref, z_ref, xbc_ref, hq_ref, hf_ref, hi_ref, hg_ref, dt_ref):
        h, _, _ = _rms(x_ref[...], nw_ref[...])
        hb = h.astype(BF)
        h0_ref[...] = hb

        def proj(a, b):
            return _dot_nt(hb, w_ref[a:b, :])

        z_ref[...] = proj(Z0, XBC0).astype(BF)
        xbc_ref[...] = proj(XBC0, DT0).astype(BF)
        dt_ref[...] = proj(DT0, DT0 + 128)
        hq_ref[...] = proj(HQ0, HF0).astype(BF)
        hf_ref[...] = proj(HF0, HI0)
        hi_ref[...] = proj(HI0, HG0).astype(BF)
        hg_ref[...] = proj(HG0, N_IN).astype(BF)

    outs = [SDS((L, D), BF), SDS((L, D), BF), SDS((L, 1536), BF), SDS((L, D), BF), SDS((L, D), F32),
            SDS((L, D), BF), SDS((L, D), BF), SDS((L, 128), F32)]
    steps = L // tl
    return _call(body, (x, nw, wt), name="in_proj", grid=(steps,),
                 in_specs=[_rows(tl, D), _const((1, D)), _resident((N_IN, D))],
                 out_specs=[_rows(tl, o.shape[1]) for o in outs], out_shape=outs, phases=phases,
                 mid_step=(3 * steps) // 4)


def _mem_kv(mem, nw, wkv4):
    def body(m_ref, nw_ref, w_ref, k_ref, v_ref):
        m, _, _ = _rms(m_ref[...], nw_ref[...])
        mb = m.astype(BF)
        for i in range(2):
            sl = slice(512 * i, 512 * i + 512)
            k_ref[:, sl] = jnp.dot(mb, w_ref[i], preferred_element_type=F32).astype(BF)
            v_ref[:, sl] = jnp.dot(mb, w_ref[2 + i], preferred_element_type=F32).astype(BF)

    outs = [SDS((MEM_LEN, D), BF)] * 2
    return pl.pallas_call(
        body, grid=(1,), name="mem_kv",
        in_specs=[_const((MEM_LEN, D)), _const((1, D)), _const((4, D, 512))],
        out_specs=[_const((MEM_LEN, D))] * 2, out_shape=outs, compiler_params=_cparams())(mem, nw, wkv4)


def _mem_kv_bwd(mem, nw, wkv4, dk, dv):
    def body(m_ref, nw_ref, w_ref, dk_ref, dv_ref, gnw_ref, gw_ref):
        m, n, _ = _rms(m_ref[...], nw_ref[...])
        mb = m.astype(BF)
        dm = jnp.zeros((MEM_LEN, D), F32)
        for i in range(4):
            src = dk_ref if i < 2 else dv_ref
            d = src[:, 512 * (i % 2):512 * (i % 2) + 512].astype(BF)
            gw_ref[i] = _dot_tn(mb, d)
            dm = dm + _dot_nt(d, w_ref[i])
        gnw_ref[...] = _colsum(dm * n)

    return pl.pallas_call(
        body, grid=(1,), name="mem_kv_bwd",
        in_specs=[_const((MEM_LEN, D)), _const((1, D)), _const((4, D, 512)), _const((MEM_LEN, D)), _const((MEM_LEN, D))],
        out_specs=[_const((1, D)), _const((4, D, 512))],
        out_shape=[SDS((1, D), F32), SDS((4, D, 512), F32)], compiler_params=_cparams())(mem, nw, wkv4, dk, dv)


def _softmax_rows(sc):
    e = jnp.exp(sc - jnp.max(sc, axis=-1, keepdims=True))
    return e * (1.0 / jnp.sum(e, axis=-1, keepdims=True))


def _attn_fwd(x, ya, ob, w_out, nxa, wq, k, v, wo):
    L = x.shape[0]
    tl = min(TL, L)
    scale = XA_HD ** -0.5

    def body(x_ref, ya_ref, ob_ref, wout_ref, nxa_ref, wq_ref, k_ref, v_ref, wo_ref,
             x1_ref, x2_ref, hxa_ref, q_ref, ox_ref):
        x1 = x_ref[...] + jnp.dot(ya_ref[...], wout_ref[:D, :], preferred_element_type=F32) \
            + jnp.dot(ob_ref[...], wout_ref[D:, :], preferred_element_type=F32)
        x1_ref[...] = x1
        h, _, _ = _rms(x1, nxa_ref[...])
        hb = h.astype(BF)
        hxa_ref[...] = hb
        qb = jnp.dot(hb, wq_ref[...], preferred_element_type=F32).astype(BF)
        q_ref[...] = qb
        heads = [slice(hd * XA_HD, (hd + 1) * XA_HD) for hd in range(XA_HEADS)]
        ps = [_softmax_rows(_dot_nt(qb[:, sl], k_ref[:, sl]) * scale) for sl in heads]
        oxs = [_dot(p, v_ref[:, sl]) for p, sl in zip(ps, heads)]
        oxb = jnp.concatenate(oxs, axis=1).astype(BF)
        ox_ref[...] = oxb
        x2_ref[...] = x1 + jnp.dot(oxb, wo_ref[...], preferred_element_type=F32)

    outs = [SDS((L, D), F32), SDS((L, D), F32), SDS((L, D), BF), SDS((L, D), BF), SDS((L, D), BF)]
    return pl.pallas_call(
        body, grid=(L // tl,), name="attn_fwd",
        in_specs=[_rows(tl, D), _rows(tl, D), _rows(tl, D), _resident((2 * D, D)), _const((1, D)), _resident((D, D)),
                  _resident((MEM_LEN, D)), _resident((MEM_LEN, D)), _resident((D, D))],
        out_specs=[_rows(tl, D)] * 5, out_shape=outs, compiler_params=_cparams())(x, ya, ob, w_out, nxa, wq, k, v, wo)


def _ffn_loss(x2, tgt, nffn, nfin, wgt, wut, wd):
    L = x2.shape[0]
    tl = min(TL_FFN, L)

    def body(x2_ref, t_ref, nffn_ref, nfin_ref, wg_ref, wu_ref, wd_ref,
             dx2_ref, h_ref, a_ref, dx3_ref, dg_ref, du_ref, acc_ref):
        _zero_first(acc_ref)
        x2v = x2_ref[...]
        h, n2, r2 = _rms(x2v, nffn_ref[...])
        hb = h.astype(BF)
        h_ref[...] = hb
        g = _dot_nt(hb, wg_ref[...])
        u = _dot_nt(hb, wu_ref[...])
        sg = _sigmoid(g)
        ab = (g * sg * u).astype(BF)
        a_ref[...] = ab
        x3 = x2v + jnp.dot(ab, wd_ref[...], preferred_element_type=F32)
        y, n3, r3 = _rms(x3, nfin_ref[...])
        err = y - t_ref[...]
        acc_ref[0:1, :] += _colsum(err * err)
        dx3, dwf = _rms_bwd(err * (1.0 / D), n3, r3, nfin_ref[...])
        acc_ref[1:2, :] += _colsum(dwf)
        dx3b = dx3.astype(BF)
        dx3_ref[...] = dx3b
        da = _dot_nt(dx3b, wd_ref[...])
        dgb = (da * u * sg * (1.0 + g * (1.0 - sg))).astype(BF)
        dub = (da * g * sg).astype(BF)
        dg_ref[...] = dgb
        du_ref[...] = dub
        dh = jnp.dot(dgb, wg_ref[...], preferred_element_type=F32) + jnp.dot(dub, wu_ref[...], preferred_element_type=F32)
        dn, dwn = _rms_bwd(dh, n2, r2, nffn_ref[...])
        acc_ref[2:3, :] += _colsum(dwn)
        dx2_ref[...] = dx3 + dn

    outs = [SDS((L, D), F32), SDS((L, D), BF), SDS((L, FFN), BF), SDS((L, D), BF), SDS((L, FFN), BF),
            SDS((L, FFN), BF), SDS((8, D), F32)]
    wspec = _resident((FFN, D))
    return pl.pallas_call(
        body, grid=(L // tl,), name="ffn_loss",
        in_specs=[_rows(tl, D), _rows(tl, D), _const((1, D)), _const((1, D)), wspec, wspec, wspec],
        out_specs=[_rows(tl, D), _rows(tl, D), _rows(tl, FFN), _rows(tl, D), _rows(tl, FFN), _rows(tl, FFN),
                   _const((8, D))],
        out_shape=outs, compiler_params=_cparams())(x2, tgt, nffn, nfin, wgt, wut, wd)


def _attn_bwd(dx2, x1, q, k, v, nxa, wq, wo, w_out, phases=()):
    L = dx2.shape[0]
    tl = min(TL, L)
    scale = XA_HD ** -0.5

    def body(dx2_ref, x1_ref, q_ref, k_ref, v_ref, nxa_ref, wq_ref, wo_ref, wout_ref,
             dx1_ref, dya_ref, dob_ref, dq_ref, dk_ref, dv_ref, acc_ref):
        _zero_first(dk_ref, dv_ref, acc_ref)
        dx2v = dx2_ref[...]
        dox = _dot_nt(dx2v, wo_ref[...]).astype(BF)
        qb = q_ref[...]
        heads = [slice(hd * XA_HD, (hd + 1) * XA_HD) for hd in range(XA_HEADS)]
        ps = [_softmax_rows(_dot_nt(qb[:, sl], k_ref[:, sl]) * scale) for sl in heads]
        dps = [_dot_nt(dox[:, sl], v_ref[:, sl]) for sl in heads]
        dss = [(p * (dp - jnp.sum(dp * p, axis=-1, keepdims=True)) * scale).astype(BF) for p, dp in zip(ps, dps)]
        for sl, p, ds in zip(heads, ps, dss):
            dv_ref[:, sl] += _dot_tn(p, dox[:, sl])
            dk_ref[:, sl] += _dot_tn(ds, qb[:, sl])
        dqs = [_dot(ds, k_ref[:, sl]) for sl, ds in zip(heads, dss)]
        dqb = jnp.concatenate(dqs, axis=1).astype(BF)
        dq_ref[...] = dqb
        dh = _dot_nt(dqb, wq_ref[...])
        _, n1, r1 = _rms(x1_ref[...], nxa_ref[...])
        dn, dwn = _rms_bwd(dh, n1, r1, nxa_ref[...])
        acc_ref[0:1, :] += _colsum(dwn)
        dx1 = dx2v + dn
        dx1_ref[...] = dx1
        dx1b = dx1.astype(BF)
        dya_ref[...] = _dot_nt(dx1b, wout_ref[:D, :]).astype(BF)
        dob_ref[...] = _dot_nt(dx1b, wout_ref[D:, :]).astype(BF)

    outs = [SDS((L, D), F32), SDS((L, D), BF), SDS((L, D), BF), SDS((L, D), BF), SDS((MEM_LEN, D), F32),
            SDS((MEM_LEN, D), F32), SDS((8, D), F32)]
    return _call(body, (dx2, x1, q, k, v, nxa, wq, wo, w_out), name="attn_bwd", grid=(L // tl,),
                 in_specs=[_rows(tl, D), _rows(tl, D), _rows(tl, D), _resident((MEM_LEN, D)), _resident((MEM_LEN, D)),
                           _const((1, D)), _resident((D, D)), _resident((D, D)), _resident((2 * D, D))],
                 out_specs=[_rows(tl, D)] * 4 + [_const((MEM_LEN, D)), _const((MEM_LEN, D)), _const((8, D))],
                 out_shape=outs, phases=phases)


def _in_proj_bwd(x, dx1, dz, dxbc, dhq, dhf, dhi, dhg, ddt, nw, wt):
    L = x.shape[0]
    tl = min(TL, L)

    def body(x_ref, dx1_ref, dz_ref, dxbc_ref, dhq_ref, dhf_ref, dhi_ref, dhg_ref, ddt_ref, nw_ref, w_ref,
             gx_ref, acc_ref):
        _zero_first(acc_ref)
        dh = _dot(dz_ref[...], w_ref[Z0:XBC0, :]) + _dot(dxbc_ref[...], w_ref[XBC0:DT0, :]) \
            + _dot(ddt_ref[...], w_ref[DT0:DT0 + 128, :]) + _dot(dhq_ref[...], w_ref[HQ0:HF0, :]) \
            + _dot(dhf_ref[...], w_ref[HF0:HI0, :]) + _dot(dhi_ref[...], w_ref[HI0:HG0, :]) \
            + _dot(dhg_ref[...], w_ref[HG0:N_IN, :])
        _, n, r = _rms(x_ref[...], nw_ref[...])
        dn, dwn = _rms_bwd(dh, n, r, nw_ref[...])
        acc_ref[0:1, :] += _colsum(dwn)
        gx_ref[...] = dx1_ref[...] + dn

    return pl.pallas_call(
        body, grid=(L // tl,), name="in_proj_bwd",
        in_specs=[_rows(tl, D), _rows(tl, D), _rows(tl, D), _rows(tl, 1536), _rows(tl, D), _rows(tl, D), _rows(tl, D),
                  _rows(tl, D), _rows(tl, 128), _const((1, D)), _resident((N_IN, D))],
        out_specs=[_rows(tl, D), _const((8, D))], out_shape=[SDS((L, D), F32), SDS((8, D), F32)],
        compiler_params=_cparams())(x, dx1, dz, dxbc, dhq, dhf, dhi, dhg, ddt, nw, wt)


def _gw_in(h0, dz, dxbc, ddt, dhq, dhf, dhi, dhg, phases=()):
    L = h0.shape[0]
    tl = min(512, L)

    def body(h_ref, dz_ref, dxbc_ref, ddt_ref, dhq_ref, dhf_ref, dhi_ref, dhg_ref, o_ref):
        _zero_first(o_ref)
        hb = h_ref[...]
        o_ref[Z0:XBC0, :] += _dot_tn(dz_ref[...], hb)
        o_ref[XBC0:DT0, :] += _dot_tn(dxbc_ref[...], hb)
        o_ref[DT0:HQ0, :] += _dot_tn(ddt_ref[...], hb)[0:NH_SSD, :]
        o_ref[HQ0:HF0, :] += _dot_tn(dhq_ref[...], hb)
        o_ref[HF0:HI0, :] += _dot_tn(dhf_ref[...], hb)
        o_ref[HI0:HG0, :] += _dot_tn(dhi_ref[...], hb)
        o_ref[HG0:N_IN, :] += _dot_tn(dhg_ref[...], hb)

    return _call(body, (h0, dz, dxbc, ddt, dhq, dhf, dhi, dhg), name="gw_in", grid=(L // tl,),
                 in_specs=[_rows(tl, D), _rows(tl, D), _rows(tl, 1536), _rows(tl, 128), _rows(tl, D), _rows(tl, D),
                           _rows(tl, D), _rows(tl, D)],
                 out_specs=[_const((N_IN, D))], out_shape=[SDS((N_IN, D), F32)], phases=phases)


def _token_tile(L, out_bytes, row_bytes):
    tl = min(2048, L)
    while tl > 256 and out_bytes + 2 * tl * row_bytes > MATMUL_VMEM:
        tl //= 2
    return tl


def _matmul_tn(a, b, name):
    L, M = a.shape
    N = b.shape[1]
    tl = _token_tile(L, 4 * M * N, M * a.dtype.itemsize + N * b.dtype.itemsize)

    def body(a_ref, b_ref, o_ref):
        _zero_first(o_ref)
        o_ref[...] += _dot_tn(a_ref[...], b_ref[...])

    return pl.pallas_call(
        body, grid=(L // tl,), name=name, in_specs=[_rows(tl, M), _rows(tl, N)], out_specs=_const((M, N)),
        out_shape=SDS((M, N), F32), compiler_params=_cparams())(a, b)


def _matmul_tn_pair(a0, a1, b, name):
    L, M = a0.shape
    N = b.shape[1]
    tl = _token_tile(L, 8 * M * N, 2 * M * a0.dtype.itemsize + N * b.dtype.itemsize)

    def body(a0_ref, a1_ref, b_ref, o_ref):
        _zero_first(o_ref)
        bv = b_ref[...].astype(BF)
        o_ref[:M, :] += _dot_tn(a0_ref[...], bv)
        o_ref[M:, :] += _dot_tn(a1_ref[...], bv)

    return pl.pallas_call(
        body, grid=(L // tl,), name=name, in_specs=[_rows(tl, M), _rows(tl, M), _rows(tl, N)],
        out_specs=_const((2 * M, N)), out_shape=SDS((2 * M, N), F32), compiler_params=_cparams())(a0, a1, b)


def _head_expand():
    e = (jnp.right_shift(_iota((128, D), 1), 6) == _iota((128, D), 0)).astype(BF)
    et = (jnp.right_shift(_iota((D, 128), 0), 6) == _iota((D, 128), 1)).astype(BF)
    return e, et


def _conv_shifts(cur, other, up):
    rows = _iota((Q, 1), 0)
    out = []
    for s in (1, 2, 3):
        if up:
            out.append(jnp.where(rows >= Q - s, pltpu.roll(other, Q - s, 0), pltpu.roll(cur, Q - s, 0)))
        else:
            out.append(jnp.where(rows < s, pltpu.roll(other, s, 0), pltpu.roll(cur, s, 0)))
    return out


def _ssd_pre(u, dtr, dtb, alog):
    e, et = _head_expand()
    sgu = _sigmoid(u)
    xc = u * sgu
    lane = _iota((1, 128), 1)
    hmask = (lane < NH_SSD).astype(F32)
    pre = dtr + dtb
    dt = (jnp.maximum(pre, 0.0) + jnp.log(1.0 + jnp.exp(-jnp.abs(pre)))) * hmask
    a_row = -jnp.exp(alog)
    causal = _iota((Q, Q), 1) <= _iota((Q, Q), 0)
    tri = causal.astype(BF)
    acum = _dot_sel(tri, dt * a_row)
    acum_full = _sel_dot(acum, e)
    alast_full = acum_full[Q - 1:Q, :]
    dt_full = _sel_dot(dt, e)
    xs = xc[:, :D]
    return dict(e=e, et=et, sgu=sgu, xs=xs, bm=xc[:, D:D + 256], cm=xc[:, D + 256:], hmask=hmask, pre=pre, dt=dt,
                a_row=a_row, causal=causal, tri=tri, acum=acum, acum_t=acum.T, eA_full=jnp.exp(acum_full),
                dte_full=jnp.exp(alast_full - acum_full), dt_full=dt_full, xdt=xs * dt_full)


def _ssd_decay(pre, hh, cb):
    seg = pre["acum"][:, hh:hh + 1] - pre["acum_t"][hh:hh + 1, :]
    lm = jnp.where(pre["causal"], jnp.exp(jnp.minimum(seg, 0.0)), 0.0)
    return lm, cb * lm


def _ssd_fwd(xbc, dtr, z, conv_w, conv_b, dtb, alog, dskip_full, nw):
    L = xbc.shape[0]
    nc = L // Q

    def chunk(ck, xbc_ref, dtr_ref, z_ref, cw_ref, cb_ref, dtb_ref, alog_ref, dsk_ref, nw_ref,
              ya_ref, y_ref, u_ref, st_ref, prev_ref, s_ref):
        tok = slice(Q * ck, Q * ck + Q)
        xr = xbc_ref[tok, :].astype(F32)
        sh = _conv_shifts(xr, prev_ref[...], up=False)
        u = cb_ref[...] + cw_ref[3:4, :] * xr + cw_ref[2:3, :] * sh[0] + cw_ref[1:2, :] * sh[1] + cw_ref[0:1, :] * sh[2]
        prev_ref[...] = xr
        ub = u.astype(BF)
        u_ref[tok, :] = ub
        pre = _ssd_pre(ub.astype(F32), dtr_ref[tok, :], dtb_ref[...], alog_ref[...])
        lo = _iota((1, 128), 1) < SSD_P
        s_old = s_ref[...]
        st_ref[ck] = s_old
        ys = []
        for g in range(2):
            bg, cg = pre["bm"][:, 128 * g:128 * g + 128], pre["cm"][:, 128 * g:128 * g + 128]
            cb = _dot_nt(cg, bg)
            gs = slice(512 * g, 512 * g + 512)
            yd = []
            for j in range(4 * g, 4 * g + 4):
                xp = pre["xdt"][:, 128 * j:128 * j + 128].astype(BF)
                _, m0 = _ssd_decay(pre, 2 * j, cb)
                _, m1 = _ssd_decay(pre, 2 * j + 1, cb)
                yd.append(jnp.where(lo, _dot(m0, xp), _dot(m1, xp)))
            yoff = _dot_nt(cg, s_old[gs, :]) * pre["eA_full"][:, gs]
            ys.append(jnp.concatenate(yd, axis=1) + yoff)
            st = _dot_tn((pre["xdt"] * pre["dte_full"])[:, gs], bg)
            cdcol = jnp.exp(_dot_sel(pre["et"][gs, :], pre["acum_t"])[:, Q - 1:Q])
            s_ref[gs, :] = s_old[gs, :] * cdcol + st
        y = jnp.concatenate(ys, axis=1) + dsk_ref[...] * pre["xs"]
        yb = y.astype(BF)
        y_ref[tok, :] = yb
        zf = z_ref[tok, :].astype(F32)
        yz = yb.astype(F32) * zf * _sigmoid(zf)
        outs = []
        for g in range(2):
            gs = slice(512 * g, 512 * g + 512)
            o, _, _ = _rms(yz[:, gs], nw_ref[:, gs])
            outs.append(o)
        ya_ref[tok, :] = jnp.concatenate(outs, axis=1).astype(BF)

    def body(*refs):
        _zero_first(*refs[-2:])
        for ck in range(CH):
            chunk(ck, *refs)

    outs = [SDS((L, D), BF), SDS((L, D), BF), SDS((L, 1536), BF), SDS((nc, D, 128), F32)]
    return pl.pallas_call(
        body, grid=(nc // CH,), name="ssd_fwd",
        in_specs=[_rows(CH * Q, 1536), _rows(CH * Q, 128), _rows(CH * Q, D), _const((4, 1536)), _const((1, 1536)), _const((1, 128)),
                  _const((1, 128)), _const((1, D)), _const((1, D))],
        out_specs=[_rows(CH * Q, D), _rows(CH * Q, D), _rows(CH * Q, 1536),
                   pl.BlockSpec((CH, D, 128), lambda i: (i, 0, 0))],
        out_shape=outs, scratch_shapes=[pltpu.VMEM((Q, 1536), F32), pltpu.VMEM((D, 128), F32)],
        compiler_params=_cparams())(xbc, dtr, z, conv_w, conv_b, dtb, alog, dskip_full, nw)


def _ssd_bwd(dya, y, z, u, xbc, dtr, states, conv_w, dtb, alog, dskip_full, nw):
    L = dya.shape[0]
    nc = L // Q

    def chunk(ck, step, dya_ref, y_ref, z_ref, u_ref, xc_ref, dtr_ref, st_ref, cw_ref, dtb_ref, alog_ref, dsk_ref, nw_ref,
              dz_ref, dxbc_ref, ddt_ref, gconv_ref, ghead_ref, glane_ref, gs_ref, ndu_ref):
        tok = slice(Q * ck, Q * ck + Q)
        uf = u_ref[tok, :].astype(F32)
        pre = _ssd_pre(uf, dtr_ref[tok, :], dtb_ref[...], alog_ref[...])
        e, et, xs, xdt = pre["e"], pre["et"], pre["xs"], pre["xdt"]
        lane = _iota((1, 128), 1)
        lo = lane < SSD_P
        sub = _iota((128, 1), 0)
        zf = z_ref[tok, :].astype(F32)
        sgz = _sigmoid(zf)
        sz = zf * sgz
        yv = y_ref[tok, :].astype(F32)
        yz = yv * sz
        dyav = dya_ref[tok, :].astype(F32)
        dyz, dnw = [], []
        for g in range(2):
            gs = slice(512 * g, 512 * g + 512)
            _, n, r = _rms(yz[:, gs], nw_ref[:, gs])
            dv, dw = _rms_bwd(dyav[:, gs], n, r, nw_ref[:, gs])
            dyz.append(dv)
            dnw.append(dw)
        dyz = jnp.concatenate(dyz, axis=1)
        glane_ref[1:2, :] += _colsum(jnp.concatenate(dnw, axis=1))
        dy = dyz * sz
        dz_ref[tok, :] = (dyz * yv * sgz * (1.0 + zf * (1.0 - sgz))).astype(BF)
        glane_ref[0:1, :] += _colsum(dy * xs)
        dxs = dsk_ref[...] * dy

        s_in = st_ref[ck]
        gst = gs_ref[...]
        gy = dy * pre["eA_full"]
        xdte = xdt * pre["dte_full"]
        dacum = jnp.zeros((Q, 128), F32)
        dacum_t = jnp.zeros((128, Q), F32)
        dxdt, dacum_full, ddte_full, dbs, dcs = [], [], [], [], []
        for g in range(2):
            gs = slice(512 * g, 512 * g + 512)
            bg, cg = pre["bm"][:, 128 * g:128 * g + 128], pre["cm"][:, 128 * g:128 * g + 128]
            sg_, dg_ = s_in[gs, :], gst[gs, :]
            yoff = _dot_nt(cg, sg_) * pre["eA_full"][:, gs]
            dc = _dot(gy[:, gs], sg_)
            dsin = _dot_tn(gy[:, gs], cg)
            dacum_full.append(dy[:, gs] * yoff)
            tg = _dot_nt(bg, dg_)
            ddte_full.append(tg * xdt[:, gs])
            db = _dot(xdte[:, gs], dg_)
            cb = _dot_nt(cg, bg)
            dcb = jnp.zeros((Q, Q), F32)
            dxg = []
            for j in range(4 * g, 4 * g + 4):
                xp = xdt[:, 128 * j:128 * j + 128].astype(BF)
                dyp = dy[:, 128 * j:128 * j + 128]
                dxp = jnp.zeros((Q, 128), F32)
                for idx in range(2):
                    hh = 2 * j + idx
                    lm, m = _ssd_decay(pre, hh, cb)
                    dym = jnp.where(lo if idx == 0 else jnp.logical_not(lo), dyp, 0.0).astype(BF)
                    dm = jnp.where(pre["causal"], _dot_nt(dym, xp), 0.0)
                    w = dm * m
                    dacum = dacum + jnp.where(lane == hh, jnp.sum(w, axis=1, keepdims=True), 0.0)
                    dacum_t = dacum_t + jnp.where(sub == hh, jnp.sum(w, axis=0, keepdims=True), 0.0)
                    dcb = dcb + dm * lm
                    dxp = dxp + _dot_tn(m, dym)
                dxg.append(dxp)
            dxdt.append(jnp.concatenate(dxg, axis=1) + tg * pre["dte_full"][:, gs])
            dcs.append(dc + _dot(dcb, bg))
            dbs.append(db + _dot_tn(dcb, cg))
            cdcol = jnp.exp(_dot_sel(et[gs, :], pre["acum_t"])[:, Q - 1:Q])
            gs_ref[gs, :] = dsin + dg_ * cdcol
        dxdt = jnp.concatenate(dxdt, axis=1)
        dacum = dacum + _sel_dot(jnp.concatenate(dacum_full, axis=1), et, 2) - dacum_t.T
        alast = pre["acum"][Q - 1:Q, :]
        dte = jnp.exp(alast - pre["acum"])
        ddte = _sel_dot(jnp.concatenate(ddte_full, axis=1), et, 2) * dte
        dacum = dacum - ddte
        dcd_col = jnp.sum(_dot_sel(e, gst * s_in, 2), axis=1, keepdims=True)
        dcd_row = jnp.broadcast_to(dcd_col, (128, 128)).T[0:1, :]
        dalast = _colsum(ddte) + dcd_row * jnp.exp(alast)
        dacum = dacum + jnp.where(_iota((Q, 1), 0) == Q - 1, dalast, 0.0)
        ddt = _sel_dot(dxdt * xs, et, 2)
        dxs = dxs + dxdt * pre["dt_full"]
        dda = _dot_sel((_iota((Q, Q), 1) >= _iota((Q, Q), 0)).astype(BF), dacum)
        ddt = ddt + dda * pre["a_row"]
        ghead_ref[1:2, :] += _colsum(dda * pre["dt"])
        ddtr = ddt * _sigmoid(pre["pre"]) * pre["hmask"]
        ghead_ref[0:1, :] += _colsum(ddtr)
        ddt_ref[tok, :] = ddtr

        dxc = jnp.concatenate([dxs] + dbs + dcs, axis=1)
        sgu = pre["sgu"]
        du = dxc * sgu * (1.0 + uf * (1.0 - sgu))
        shu = _conv_shifts(du, ndu_ref[...], up=True)
        dxr = cw_ref[3:4, :] * du + cw_ref[2:3, :] * shu[0] + cw_ref[1:2, :] * shu[1] + cw_ref[0:1, :] * shu[2]
        ndu_ref[...] = du
        dxbc_ref[tok, :] = dxr.astype(BF)
        xr = xc_ref[tok, :].astype(F32)
        gconv_ref[3:4, :] += _colsum(du * xr)
        gconv_ref[2:3, :] += _colsum(shu[0] * xr)
        gconv_ref[1:2, :] += _colsum(shu[1] * xr)
        gconv_ref[0:1, :] += _colsum(shu[2] * xr)
        gconv_ref[4:5, :] += _colsum(du)

        @pl.when(jnp.logical_and(step == nc // CH - 1, ck == 0))
        def _():
            ghead_ref[2:3, :] = ghead_ref[1:2, :] * pre["a_row"]
            ghead_ref[3:4, :] = _sel_dot(glane_ref[...], et)[0:1, :]

    def body(*refs):
        _zero_first(*refs[-5:])
        for ck in reversed(range(CH)):
            chunk(ck, pl.program_id(0), *refs)

    rev = lambda i: (nc // CH - 1 - i, 0)
    outs = [SDS((L, D), BF), SDS((L, 1536), BF), SDS((L, 128), F32), SDS((8, 1536), F32), SDS((8, 128), F32),
            SDS((8, D), F32)]
    return pl.pallas_call(
        body, grid=(nc // CH,), name="ssd_bwd",
        in_specs=[pl.BlockSpec((CH * Q, D), rev), pl.BlockSpec((CH * Q, D), rev), pl.BlockSpec((CH * Q, D), rev),
                  pl.BlockSpec((CH * Q, 1536), rev), pl.BlockSpec((CH * Q, 1536), rev),
                  pl.BlockSpec((CH * Q, 128), rev), pl.BlockSpec((CH, D, 128), lambda i: (nc // CH - 1 - i, 0, 0)),
                  _const((4, 1536)), _const((1, 128)), _const((1, 128)), _const((1, D)), _const((1, D))],
        out_specs=[pl.BlockSpec((CH * Q, D), rev), pl.BlockSpec((CH * Q, 1536), rev), pl.BlockSpec((CH * Q, 128), rev),
                   _const((8, 1536)), _const((8, 128)), _const((8, D))],
        out_shape=outs, scratch_shapes=[pltpu.VMEM((D, 128), F32), pltpu.VMEM((Q, 1536), F32)],
        compiler_params=_cparams())(dya, y, z, u, xbc, dtr, states, conv_w, dtb, alog, dskip_full, nw)


def _hg_gates(hq, hf, hlb):
    h0, h1 = hlb[0:1, :], hlb[1:2, :]
    mx = jnp.maximum(h0, h1)
    e0, e1 = jnp.exp(h0 - mx), jnp.exp(h1 - mx)
    lb = e0 / (e0 + e1)
    sg = _sigmoid(hf)
    fg = lb + (1.0 - lb) * sg
    tri = (_iota((Q, Q), 1) <= _iota((Q, Q), 0)).astype(BF)
    return hq * _sigmoid(hq), 1.0 - fg, fg, sg, lb, e1 / (e0 + e1), _dot_sel(tri, jnp.log(fg))


def _hg_intra(b, q, k):
    rowblk = jnp.right_shift(_iota((Q, 1), 0), SUB.bit_length() - 1)
    mids = [b[SUB * i + SUB // 2:SUB * i + SUB // 2 + 1, :] for i in range(NSUB)]
    prevs = [mids[0]] + [b[SUB * i - 1:SUB * i, :] for i in range(1, NSUB)]
    mfull = jnp.concatenate([jnp.broadcast_to(r, (SUB, 128)) for r in mids], axis=0)
    rfull = jnp.concatenate([jnp.broadcast_to(r, (SUB, 128)) for r in prevs], axis=0)
    eqd, ek, eqo = jnp.exp(b - mfull), jnp.exp(mfull - b), jnp.exp(b - rfull)
    qd, qo, khat = q * eqd, q * eqo, k * ek
    rtab = jnp.concatenate(prevs, axis=0)
    djs = [jnp.exp(rtab - mids[j]) for j in range(NSUB)]
    zero = jnp.zeros((SUB, 128), F32)
    cols = []
    for j in range(NSUB):
        pieces = []
        for i in range(NSUB):
            rs = slice(SUB * i, SUB * i + SUB)
            pieces.append(zero if i < j else qd[rs] if i == j else qo[rs] * djs[j][i:i + 1, :])
        cols.append(jnp.concatenate(pieces, axis=0))
    qt = jnp.concatenate(cols, axis=1).astype(BF)
    kt = jnp.concatenate([jnp.where(rowblk == j, khat, 0.0) for j in range(NSUB)], axis=1).astype(BF)
    causal = _iota((Q, Q), 1) <= _iota((Q, Q), 0)
    att = jnp.where(causal, _dot_nt(qt, kt), 0.0)
    return att, qt, kt, (eqd, ek, eqo, djs), causal


def _hg_intra_bwd(dqt, dkt, qt, kt, factors):
    eqd, ek, eqo, djs = factors
    dqd, dqo, dkh, db = [], [], [], []
    for i in range(NSUB):
        rs = slice(SUB * i, SUB * i + SUB)
        diag = slice(128 * i, 128 * i + 128)
        dqd.append(dqt[rs, diag])
        dkh.append(dkt[rs, diag])
        dbi = qt[rs, diag].astype(F32) * dqt[rs, diag] - kt[rs, diag].astype(F32) * dkt[rs, diag]
        acc = jnp.zeros((SUB, 128), F32)
        for j in range(i):
            bl = slice(128 * j, 128 * j + 128)
            acc = acc + dqt[rs, bl] * djs[j][i:i + 1, :]
            dbi = dbi + qt[rs, bl].astype(F32) * dqt[rs, bl]
        dqo.append(acc)
        db.append(dbi)
    cat = lambda t: jnp.concatenate(t, axis=0)
    return cat(dqd) * eqd + cat(dqo) * eqo, cat(dkh) * ek, cat(db)


def _hg_att_exact(b, q, k, b_ref, q_ref, att_t_ref):
    b_ref[...] = b
    q_ref[...] = q
    att_t_ref[...] = jnp.zeros((Q, Q), F32)
    rows, lane = _iota((Q, 1), 0), _iota((1, Q), 1)

    def step(i, carry):
        e = jnp.exp(jnp.minimum(b_ref[pl.ds(i, 1), :] - b, 0.0))
        col = jnp.sum(q_ref[pl.ds(i, 1), :] * k * e, axis=1, keepdims=True)
        att_t_ref[...] = jnp.where(lane == i, jnp.where(rows <= i, col, 0.0), att_t_ref[...])
        return carry

    lax.fori_loop(0, Q, step, 0)
    return att_t_ref[...].T


def _hg_att_exact_bwd(da, b, q, k, b_ref, q_ref, da_t_ref, dq_ref, dk_ref):
    b_ref[...] = b
    q_ref[...] = q
    da_t_ref[...] = da.T
    dk_ref[...] = jnp.zeros((Q, 128), F32)
    lane = _iota((1, Q), 1)

    def step(i, carry):
        e = jnp.exp(jnp.minimum(b_ref[pl.ds(i, 1), :] - b, 0.0))
        g = jnp.sum(jnp.where(lane == i, da_t_ref[...], 0.0), axis=1, keepdims=True) * e
        dq_ref[pl.ds(i, 1), :] = jnp.sum(g * k, axis=0, keepdims=True)
        dk_ref[...] += g * q_ref[pl.ds(i, 1), :]
        return carry

    lax.fori_loop(0, Q, step, 0)
    dq, dk = dq_ref[...], dk_ref[...]
    return dq, dk, q * dq - k * dk


def _hg_fwd(hq, hf, hi, hg, hlb, nw, fast):
    L = hq.shape[0]
    nc = L // Q

    def chunk(exact, ck, hq_ref, hf_ref, hi_ref, hg_ref, hlb_ref, nw_ref, ob_ref, o_ref, st_ref, s_ref, *tmp):
        tok = slice(Q * ck, Q * ck + Q)
        qf, kf, _, _, _, _, bcum = _hg_gates(hq_ref[tok, :].astype(F32), hf_ref[tok, :], hlb_ref[...])
        gate = hg_ref[tok, :].astype(F32)
        heads = [slice(128 * h, 128 * h + 128) for h in range(NH_HG)]
        if exact:
            atts = [_hg_att_exact(bcum[:, sl], qf[:, sl], kf[:, sl], *tmp).astype(BF) for sl in heads]
        else:
            atts = [_hg_intra(bcum[:, sl], qf[:, sl], kf[:, sl])[0].astype(BF) for sl in heads]
        olds = [s_ref[sl, :] for sl in heads]
        outs_ = [_dot(att, hi_ref[tok, sl]) + _dot(qf[:, sl] * jnp.exp(bcum[:, sl]), s)
                 for att, sl, s in zip(atts, heads, olds)]
        for sl, s, o in zip(heads, olds, outs_):
            b, k = bcum[:, sl], kf[:, sl]
            st_ref[ck, sl, :] = s
            blast = b[Q - 1:Q, :]
            s_ref[sl, :] = s * jnp.exp(b.T[:, Q - 1:Q]) + _dot_tn(k * jnp.exp(blast - b), hi_ref[tok, sl])
            ob = o.astype(BF)
            o_ref[tok, sl] = ob
            on, _, _ = _rms(ob.astype(F32), nw_ref[...])
            gt = gate[:, sl]
            ob_ref[tok, sl] = (on * gt * _sigmoid(gt)).astype(BF)

    def run(exact, *refs):
        for ck in range(CH):
            chunk(exact, ck, *refs)

    def body(fast_ref, *refs):
        _zero_first(refs[9])
        pl.when(fast_ref[0] == 1)(lambda: run(False, *refs))
        pl.when(fast_ref[0] != 1)(lambda: run(True, *refs))

    rows = pl.BlockSpec((CH * Q, D), lambda i, f: (i, 0))
    outs = [SDS((L, D), BF), SDS((L, D), BF), SDS((nc, D, 128), F32)]
    grid_spec = pltpu.PrefetchScalarGridSpec(
        num_scalar_prefetch=1, grid=(nc // CH,),
        in_specs=[rows] * 4 + [pl.BlockSpec((2, D), lambda i, f: (0, 0)), pl.BlockSpec((1, 128), lambda i, f: (0, 0))],
        out_specs=[rows, rows, pl.BlockSpec((CH, D, 128), lambda i, f: (i, 0, 0))],
        scratch_shapes=[pltpu.VMEM((D, 128), F32), pltpu.VMEM((Q, 128), F32), pltpu.VMEM((Q, 128), F32),
                        pltpu.VMEM((Q, Q), F32)])
    return pl.pallas_call(body, grid_spec=grid_spec, name="hg_fwd", out_shape=outs,
                          compiler_params=_cparams())(fast, hq, hf, hi, hg, hlb, nw)


def _hg_bwd(dob, o, hq, hf, hi, hg, states, hlb, nw, fast, phases=()):
    L = dob.shape[0]
    nc = L // Q

    def chunk(exact, ck, step, dob_ref, o_ref, hq_ref, hf_ref, hi_ref, hg_ref, st_ref, hlb_ref, nw_ref,
              dhq_ref, dhf_ref, dhi_ref, dhg_ref, acc_ref, gs_ref, *tmp):
        tok = slice(Q * ck, Q * ck + Q)
        hqv = hq_ref[tok, :].astype(F32)
        qf, kf, fg, sg, lb, sm1, bcum = _hg_gates(hqv, hf_ref[tok, :], hlb_ref[...])
        gate = hg_ref[tok, :].astype(F32)
        sgg = _sigmoid(gate)
        nwv = nw_ref[...]
        tri_t = (_iota((Q, Q), 1) >= _iota((Q, Q), 0)).astype(BF)
        ones8 = jnp.ones((8, 128), BF)
        heads = [slice(128 * h, 128 * h + 128) for h in range(NH_HG)]
        row_last = _iota((Q, 1), 0) == Q - 1
        dobs, dnws = [], []
        for sl in heads:
            gt, sgt = gate[:, sl], sgg[:, sl]
            _, n, r = _rms(o_ref[tok, sl].astype(F32), nwv)
            dobv = dob_ref[tok, sl].astype(F32)
            dhg_ref[tok, sl] = (dobv * n * nwv * sgt * (1.0 + gt * (1.0 - sgt))).astype(BF)
            do, dw = _rms_bwd(dobv * gt * sgt, n, r, nwv)
            dnws.append(_colsum(dw))
            dobs.append(do.astype(BF))
        causal = _iota((Q, Q), 1) <= _iota((Q, Q), 0)
        if exact:
            intra = [(_hg_att_exact(bcum[:, sl], qf[:, sl], kf[:, sl], *tmp[:3]),) for sl in heads]
        else:
            intra = [_hg_intra(bcum[:, sl], qf[:, sl], kf[:, sl]) for sl in heads]
        states = [(st_ref[ck, sl, :], gs_ref[sl, :]) for sl in heads]
        das = [jnp.where(causal, _dot_nt(dob_h, hi_ref[tok, sl]), 0.0) for dob_h, sl in zip(dobs, heads)]
        dqhats = [_dot_nt(dob_h, s) for dob_h, (s, _) in zip(dobs, states)]
        dkhats = [_dot_nt(hi_ref[tok, sl], gst) for sl, (_, gst) in zip(heads, states)]
        if not exact:
            dqts = [jnp.dot(da.astype(BF), it[2], preferred_element_type=F32) for da, it in zip(das, intra)]
            dkts = [lax.dot_general(da.astype(BF), it[1], (((0,), (0,)), ((), ())), preferred_element_type=F32)
                    for da, it in zip(das, intra)]
        dqs, dks, dgls = [], [], []
        for h, sl in enumerate(heads):
            b, q, k = bcum[:, sl], qf[:, sl], kf[:, sl]
            att = intra[h][0]
            s, gst = states[h]
            dob_h, dqhat, dkhat = dobs[h], dqhats[h], dkhats[h]
            eb = jnp.exp(b)
            blast = b[Q - 1:Q, :]
            ekl = jnp.exp(blast - b)
            qhat, khat = q * eb, k * ekl
            dhi_ref[tok, sl] = (_dot_tn(att, dob_h) + _dot(khat, gst)).astype(BF)
            if exact:
                dq_i, dk_i, db = _hg_att_exact_bwd(das[h], b, q, k, *tmp)
            else:
                dq_i, dk_i, db = _hg_intra_bwd(dqts[h], dkts[h], *intra[h][1:4])
            dqs.append(dq_i + dqhat * eb)
            dks.append(dk_i + dkhat * ekl)
            qhat_r, khat_r = qhat.astype(BF).astype(F32), khat.astype(BF).astype(F32)
            decay_row = sum(_dot_nt(ones8, part) for part in _split(gst * s, 2))[0:1, :]
            dblast = _colsum(dkhat * khat_r) + decay_row * jnp.exp(blast)
            dgls.append(db + qhat_r * dqhat - khat_r * dkhat + jnp.where(row_last, dblast, 0.0))
            gs_ref[sl, :] = _dot_tn(qhat, dob_h) + gst * jnp.exp(b.T[:, Q - 1:Q])
        dq, dk, db = (jnp.concatenate(t, axis=1) for t in (dqs, dks, dgls))
        dgl = _dot_sel(tri_t, db, 2)
        sgq = _sigmoid(hqv)
        dhq_ref[tok, :] = (dq * sgq * (1.0 + hqv * (1.0 - sgq))).astype(BF)
        dfg = dgl / fg - dk
        dhf_ref[tok, :] = (dfg * (1.0 - lb) * sg * (1.0 - sg)).astype(BF)
        acc_ref[0:1, :] += _colsum(dfg * (1.0 - sg))
        acc_ref[1:2, :] += jnp.concatenate(dnws, axis=1)

        @pl.when(jnp.logical_and(step == nc // CH - 1, ck == 0))
        def _():
            dlb = acc_ref[0:1, :] * lb * sm1
            acc_ref[2:3, :] = dlb
            acc_ref[3:4, :] = -dlb
            tot = acc_ref[1:2, 0:128]
            for h in range(1, NH_HG):
                tot = tot + acc_ref[1:2, 128 * h:128 * h + 128]
            acc_ref[4:5, 0:128] = tot

    def run(exact, step, *refs):
        for ck in reversed(range(CH)):
            chunk(exact, ck, step, *refs)

    def body(fast_ref, *refs):
        step = pl.program_id(0)
        _zero_first(refs[13], refs[14])
        pl.when(fast_ref[0] == 1)(lambda: run(False, step, *refs))
        pl.when(fast_ref[0] != 1)(lambda: run(True, step, *refs))

    rev = pl.BlockSpec((CH * Q, D), lambda i, f: (nc // CH - 1 - i, 0))
    outs = [SDS((L, D), BF)] * 4 + [SDS((8, D), F32)]
    return _call(
        body, (fast, dob, o, hq, hf, hi, hg, states, hlb, nw), name="hg_bwd", grid=(nc // CH,), prefetch=1,
        in_specs=[rev] * 6 + [pl.BlockSpec((CH, D, 128), lambda i, f: (nc // CH - 1 - i, 0, 0)),
                              pl.BlockSpec((2, D), lambda i, f: (0, 0)), pl.BlockSpec((1, 128), lambda i, f: (0, 0))],
        out_specs=[rev] * 4 + [pl.BlockSpec((8, D), lambda i, f: (0, 0))], out_shape=outs,
        scratch_shapes=[pltpu.VMEM((D, 128), F32), pltpu.VMEM((Q, 128), F32), pltpu.VMEM((Q, 128), F32),
                        pltpu.VMEM((Q, Q), F32), pltpu.VMEM((Q, 128), F32), pltpu.VMEM((Q, 128), F32)], phases=phases)


def _place():
    return lax.axis_index("x"), lax.axis_index("y"), lax.axis_index("c")


def _phase_io(phase):
    kind, arrays, halves = phase
    n = len(arrays)
    dma = pltpu.SemaphoreType.DMA
    if kind == "gather":
        outs = [SDS((8,) + a.shape if hc is None else (4,) + a.shape, a.dtype) for a, hc in zip(arrays, halves)]
        return outs, [dma((7 * n,)), dma((7 * n,)), dma((n,))], {}
    if kind == "sibling":
        return [SDS((4, g.shape[1], hc), g.dtype) for g, hc in zip(arrays, halves)], [dma((n,)), dma((n,))], {}
    if kind == "chips":
        return [SDS((3,) + p.shape[1:], p.dtype) for p in arrays], [dma((3 * n,)), dma((3 * n,))], {}
    assert kind == "swap"
    return [SDS(b.shape, b.dtype) for b in arrays], [dma((n,)), dma((n,))], {a: a for a in range(n)}


def _gather_events(ins, outs, sems, halves):
    send_sems, recv_sems, local_sems = sems
    n = len(ins)

    def parts(a):
        x, y, c = _place()
        hc = halves[a]
        me, sibling = (x, y, c), (x, y, 1 - c)
        chips = [(1 - x, y), (x, 1 - y), (1 - x, 1 - y)]

        def slot(p):
            if hc is None:
                return outs[a].at[4 * p[0] + 2 * p[1] + p[2]]
            return outs[a].at[2 * p[0] + p[1], :, pl.ds(p[2] * hc, hc)]

        own = ins[a] if hc is None else ins[a].at[:, pl.ds(c * hc, hc)]

        def copy(k, piece, to, src=None):
            return pltpu.make_async_remote_copy(
                src_ref=slot(piece) if src is None else src, dst_ref=slot(piece),
                send_sem=send_sems.at[7 * a + k], recv_sem=recv_sems.at[7 * a + k], device_id=to, device_id_type=MESH)

        return dict(
            mine=lambda: pltpu.make_async_copy(own, slot(me), local_sems.at[a]),
            starts=lambda: [copy(0, me, sibling, src=own)] + [copy(1 + j, me, (*chip, c), src=own)
                                                               for j, chip in enumerate(chips)],
            arrive=lambda: [copy(1 + j, (*chip, c), me) for j, chip in enumerate(chips)],
            passed=lambda: [copy(4 + j, (*chip, c), sibling) for j, chip in enumerate(chips)],
            from_sibling=lambda: [copy(0, sibling, me)] + [copy(4 + j, (*chip, 1 - c), me)
                                                            for j, chip in enumerate(chips)])

    def first():
        for a in range(n):
            p = parts(a)
            p["mine"]().start()
            for cp in p["starts"]():
                cp.start()

    def mid():
        for a in range(n):
            p = parts(a)
            for cp_in, cp_out in zip(p["arrive"](), p["passed"]()):
                cp_in.wait_recv()
                cp_out.start()

    def last():
        for a in range(n):
            p = parts(a)
            for cp in p["from_sibling"]():
                cp.wait_recv()
            for cp in p["starts"]() + p["passed"]():
                cp.wait_send()
            p["mine"]().wait()

    return dict(first=first, mid=mid, last=last)


def _exchange_events(kind, ins, outs, sems, halves):
    send_sems, recv_sems = sems
    n = len(outs)

    def copies():
        x, y, c = _place()
        if kind == "sibling":
            return [pltpu.make_async_remote_copy(
                src_ref=ins[a].at[:, :, pl.ds((1 - c) * halves[a], halves[a])], dst_ref=outs[a],
                send_sem=send_sems.at[a], recv_sem=recv_sems.at[a], device_id=(x, y, 1 - c), device_id_type=MESH)
                for a in range(n)]
        chips = [(1 - x, y), (x, 1 - y), (1 - x, 1 - y)]
        return [pltpu.make_async_remote_copy(
            src_ref=ins[a].at[2 * px + py], dst_ref=outs[a].at[k], send_sem=send_sems.at[3 * a + k],
            recv_sem=recv_sems.at[3 * a + k], device_id=(px, py, c), device_id_type=MESH)
            for a in range(n) for k, (px, py) in enumerate(chips)]

    def first():
        for cp in copies():
            cp.start()

    def last():
        for cp in copies():
            cp.wait()

    return dict(first=first, last=last)


def _swap_events(outs, sems, halves):
    send_sems, recv_sems = sems
    n = len(outs)

    def copy(a, landing):
        x, y, c = _place()
        cols = lambda which: outs[a].at[:, pl.ds(which * halves[a], halves[a])]
        return pltpu.make_async_remote_copy(
            src_ref=cols(c), dst_ref=cols(1 - c) if landing else cols(c), send_sem=send_sems.at[a],
            recv_sem=recv_sems.at[a], device_id=(x, y, 1 - c), device_id_type=MESH)

    def first():
        for a in range(n):
            copy(a, False).start()

    def last():
        for a in range(n):
            copy(a, True).wait_recv()
        for a in range(n):
            copy(a, False).wait_send()

    return dict(first=first, last=last)


def _phase_events(phase, ins, outs, sems):
    kind, _, halves = phase
    if kind == "gather":
        return _gather_events(ins, outs, sems, halves)
    if kind == "swap":
        return _swap_events(outs, sems, halves)
    return _exchange_events(kind, ins, outs, sems, halves)


def _split_refs(refs, counts):
    out, at = [], 0
    for c in counts:
        out.append(list(refs[at:at + c]))
        at += c
    return out


def _comm_plumbing(phases, first_in, first_out):
    ios = [_phase_io(p) for p in phases]
    arrays = [a for p in phases for a in p[1]]
    out_shape = [o for io in ios for o in io[0]]
    sem_shapes = [s for io in ios for s in io[1]]
    aliases, ai, ao = {}, first_in, first_out
    for p, io in zip(phases, ios):
        aliases.update({ai + k: ao + v for k, v in io[2].items()})
        ai, ao = ai + len(p[1]), ao + len(io[0])

    def events(cins, couts, sems):
        evs = [_phase_events(p, i, o, s) for p, i, o, s in zip(
            phases, _split_refs(cins, [len(p[1]) for p in phases]), _split_refs(couts, [len(io[0]) for io in ios]),
            _split_refs(sems, [len(io[1]) for io in ios]))]

        def run(key):
            for ev in evs:
                if key in ev:
                    ev[key]()

        return {key: (lambda key=key: run(key)) for key in ("first", "mid", "last")}

    def regroup(flat):
        return _split_refs(flat, [len(io[0]) for io in ios])

    return arrays, out_shape, sem_shapes, aliases, events, regroup


def _run_phases(phases, name):
    arrays, out_shape, sem_shapes, aliases, events, regroup = _comm_plumbing(phases, 0, 0)

    def body(*refs):
        cins, couts, sems = _split_refs(refs, [len(arrays), len(out_shape), len(sem_shapes)])
        ev = events(cins, couts, sems)
        for key in ("first", "mid", "last"):
            ev[key]()

    outs = pl.pallas_call(
        body, name=name, in_specs=[ANY] * len(arrays), out_specs=[ANY] * len(out_shape), out_shape=out_shape,
        scratch_shapes=sem_shapes, input_output_aliases=aliases)(*arrays)
    return regroup(outs)


def _call(body, args, *, name, grid, in_specs, out_specs, out_shape, scratch_shapes=(), prefetch=0, phases=(),
          mid_step=None):
    steps = grid[0]
    arrays, c_shape, sem_shapes, aliases, events, regroup = _comm_plumbing(
        phases, prefetch + len(in_specs), len(out_specs))
    counts = [prefetch, len(in_specs), len(arrays), len(out_specs), len(c_shape), len(scratch_shapes), len(sem_shapes)]

    def wrapped(*refs):
        pre, ins, cins, outs, couts, scratch, sems = _split_refs(refs, counts)
        if not phases:
            return body(*pre, *ins, *outs, *scratch)
        step = pl.program_id(0)
        ev = events(cins, couts, sems)
        pl.when(step == 0)(ev["first"])
        body(*pre, *ins, *outs, *scratch)
        pl.when(step == (steps // 2 if mid_step is None else mid_step))(ev["mid"])
        pl.when(step == steps - 1)(ev["last"])

    grid_spec = pltpu.PrefetchScalarGridSpec(
        num_scalar_prefetch=prefetch, grid=grid, in_specs=list(in_specs) + [ANY] * len(arrays),
        out_specs=list(out_specs) + [ANY] * len(c_shape), scratch_shapes=list(scratch_shapes) + sem_shapes)
    outs = pl.pallas_call(
        wrapped, grid_spec=grid_spec, name=name, out_shape=list(out_shape) + c_shape, input_output_aliases=aliases,
        compiler_params=_cparams())(*args, *arrays)
    return list(outs[:len(out_specs)]), regroup(outs[len(out_specs):])


def _tile(rows, cols, nbuf):
    budget = (VMEM_LIMIT // 3) // (2 * nbuf * 4)
    if rows % 8 == 0:
        cands = [t for t in range(8, rows + 1, 8) if rows % t == 0 and t * cols <= budget]
        pref = [t for t in cands if t % 16 == 0]
        return (max(pref) if pref else max(cands) if cands else 8), cols
    cands = [t for t in range(128, cols + 1, 128) if cols % t == 0 and rows * t <= budget]
    return rows, (max(cands) if cands else 128)


def _chip_sum(g, from_sib, place, name):
    _, rows, hc = from_sib.shape
    tr, tc = _tile(rows, hc, 4)
    ni, nj = rows // tr, hc // tc

    def body(p_ref, g_ref, s_ref, hb_ref, own_ref):
        s = g_ref[...] + s_ref[...]
        hb_ref[...] = s.astype(BF)

        @pl.when(pl.program_id(2) == p_ref[1])
        def _():
            own_ref[...] = s

    grid_spec = pltpu.PrefetchScalarGridSpec(
        num_scalar_prefetch=1, grid=(ni, nj, 4),
        in_specs=[pl.BlockSpec((None, tr, tc), lambda i, j, k, p: (k, i, p[0] * nj + j)),
                  pl.BlockSpec((None, tr, tc), lambda i, j, k, p: (k, i, j))],
        out_specs=[pl.BlockSpec((None, tr, tc), lambda i, j, k, p: (k, i, j)),
                   pl.BlockSpec((tr, tc), lambda i, j, k, p: (i, j))])
    return pl.pallas_call(
        body, grid_spec=grid_spec, name=name, out_shape=[SDS((4, rows, hc), BF), SDS((rows, hc), F32)],
        compiler_params=pltpu.CompilerParams(dimension_semantics=("arbitrary",) * 3,
                                             vmem_limit_bytes=VMEM_LIMIT))(place, g, from_sib)


def _total(own, parts, place, name):
    rows, hc = own.shape
    tr, tc = _tile(rows, hc, 5)
    ni, nj = rows // tr, hc // tc

    def body(p_ref, own_ref, parts_ref, o_ref):
        s = own_ref[...]
        for k in range(3):
            s = s + parts_ref[k].astype(F32)
        o_ref[...] = s

    grid_spec = pltpu.PrefetchScalarGridSpec(
        num_scalar_prefetch=1, grid=(ni, nj),
        in_specs=[pl.BlockSpec((tr, tc), lambda i, j, p: (i, j)),
                  pl.BlockSpec((3, tr, tc), lambda i, j, p: (0, i, j))],
        out_specs=pl.BlockSpec((tr, tc), lambda i, j, p: (i, p[0] * nj + j)))
    return pl.pallas_call(
        body, grid_spec=grid_spec, name=name, out_shape=SDS((rows, 2 * hc), F32),
        compiler_params=pltpu.CompilerParams(dimension_semantics=("arbitrary",) * 2,
                                             vmem_limit_bytes=VMEM_LIMIT))(place, own, parts)


def _sum8(parts, name):
    R = parts.shape[1]

    def body(p_ref, o_ref):
        s = p_ref[0]
        for k in range(1, 8):
            s = s + p_ref[k]
        o_ref[...] = s

    return pl.pallas_call(
        body, grid=(1,), name=name, in_specs=[_const((8, R, 128))], out_specs=_const((R, 128)),
        out_shape=SDS((R, 128), F32), compiler_params=_cparams())(parts)


def _adamw(w, g, m, v, name):
    _, R, C = w.shape
    tr, tc = _tile(R, C, 7)
    c1 = 1.0 / (1.0 - ADAM_B1 ** ADAM_STEP)
    c2 = 1.0 / (1.0 - ADAM_B2 ** ADAM_STEP)

    def body(w_ref, g_ref, m_ref, v_ref, d_ref, nm_ref, nv_ref):
        gv = g_ref[...]
        nm = ADAM_B1 * m_ref[...] + (1.0 - ADAM_B1) * gv
        nv = ADAM_B2 * v_ref[...] + (1.0 - ADAM_B2) * gv * gv
        nm_ref[...] = nm
        nv_ref[...] = nv
        d_ref[...] = -ADAM_LR * ((nm * c1) / (jnp.sqrt(nv * c2) + ADAM_EPS) + ADAM_WD * w_ref[...])

    blk3 = pl.BlockSpec((None, tr, tc), lambda i, j: (0, i, j))
    return pl.pallas_call(
        body, grid=(R // tr, C // tc), name=name,
        in_specs=[blk3, pl.BlockSpec((tr, tc), lambda i, j: (i, j)), blk3, blk3], out_specs=[blk3] * 3,
        out_shape=[SDS((1, R, C), F32)] * 3,
        compiler_params=pltpu.CompilerParams(dimension_semantics=("arbitrary",) * 2,
                                             vmem_limit_bytes=VMEM_LIMIT))(w, g, m, v)


def _pack_small(parts):
    rows = []
    for p in parts:
        p = p.reshape(-1)
        rows.append(jnp.pad(p, (0, (-p.shape[0]) % 128)).reshape(-1, 128))
    out = jnp.concatenate(rows, axis=0)
    return jnp.pad(out, ((0, (-out.shape[0]) % 8), (0, 0)))


def _unpack_small(packed, shapes):
    out, row = [], 0
    for shp in shapes:
        n = 1
        for s in shp:
            n *= s
        nr = -(-n // 128)
        out.append(packed[row:row + nr].reshape(-1)[:n].reshape(shp))
        row += nr
    return out


def _pad_lanes(v, n=128):
    return jnp.pad(v, ((0, 0), (0, n - v.shape[1])))


GROUP_FFN = ("ffn_w_gate", "ffn_w_up", "ffn_w_down")
GROUP_ATTN = ("w_out", "xa_wq", "xa_wkv", "xa_wo")


def kernel(x, mem, norm_mix_w, w_in, conv_w, conv_b, dt_bias, a_log, d_skip, ssd_norm_w, hg_lower_bounds, hg_norm_w, w_out, norm_xa_w, norm_mem_w, xa_wq, xa_wkv, xa_wo, norm_ffn_w, ffn_w_gate, ffn_w_up, ffn_w_down, norm_final_w, loss_target, m_norm_mix_w, m_w_in, m_conv_w, m_conv_b, m_dt_bias, m_a_log, m_d_skip, m_ssd_norm_w, m_hg_lower_bounds, m_hg_norm_w, m_w_out, m_norm_xa_w, m_norm_mem_w, m_xa_wq, m_xa_wkv, m_xa_wo, m_norm_ffn_w, m_ffn_w_gate, m_ffn_w_up, m_ffn_w_down, m_norm_final_w, v_norm_mix_w, v_w_in, v_conv_w, v_conv_b, v_dt_bias, v_a_log, v_d_skip, v_ssd_norm_w, v_hg_lower_bounds, v_hg_norm_w, v_w_out, v_norm_xa_w, v_norm_mem_w, v_xa_wq, v_xa_wkv, v_xa_wo, v_norm_ffn_w, v_ffn_w_gate, v_ffn_w_up, v_ffn_w_down, v_norm_final_w):
    w = dict(norm_mix_w=norm_mix_w, w_in=w_in, conv_w=conv_w, conv_b=conv_b, dt_bias=dt_bias, a_log=a_log, d_skip=d_skip,
             ssd_norm_w=ssd_norm_w, hg_lower_bounds=hg_lower_bounds, hg_norm_w=hg_norm_w, w_out=w_out,
             norm_xa_w=norm_xa_w, norm_mem_w=norm_mem_w, xa_wq=xa_wq, xa_wkv=xa_wkv, xa_wo=xa_wo, norm_ffn_w=norm_ffn_w,
             ffn_w_gate=ffn_w_gate, ffn_w_up=ffn_w_up, ffn_w_down=ffn_w_down, norm_final_w=norm_final_w)
    m = dict(norm_mix_w=m_norm_mix_w, w_in=m_w_in, conv_w=m_conv_w, conv_b=m_conv_b, dt_bias=m_dt_bias, a_log=m_a_log,
             d_skip=m_d_skip, ssd_norm_w=m_ssd_norm_w, hg_lower_bounds=m_hg_lower_bounds, hg_norm_w=m_hg_norm_w,
             w_out=m_w_out, norm_xa_w=m_norm_xa_w, norm_mem_w=m_norm_mem_w, xa_wq=m_xa_wq, xa_wkv=m_xa_wkv,
             xa_wo=m_xa_wo, norm_ffn_w=m_norm_ffn_w, ffn_w_gate=m_ffn_w_gate, ffn_w_up=m_ffn_w_up,
             ffn_w_down=m_ffn_w_down, norm_final_w=m_norm_final_w)
    v = dict(norm_mix_w=v_norm_mix_w, w_in=v_w_in, conv_w=v_conv_w, conv_b=v_conv_b, dt_bias=v_dt_bias, a_log=v_a_log,
             d_skip=v_d_skip, ssd_norm_w=v_ssd_norm_w, hg_lower_bounds=v_hg_lower_bounds, hg_norm_w=v_hg_norm_w,
             w_out=v_w_out, norm_xa_w=v_norm_xa_w, norm_mem_w=v_norm_mem_w, xa_wq=v_xa_wq, xa_wkv=v_xa_wkv,
             xa_wo=v_xa_wo, norm_ffn_w=v_norm_ffn_w, ffn_w_gate=v_ffn_w_gate, ffn_w_up=v_ffn_w_up,
             ffn_w_down=v_ffn_w_down, norm_final_w=v_norm_final_w)
    xi, yi, ci = _place()
    chip = 2 * xi + yi
    place = jnp.stack([ci, chip]).astype(jnp.int32)

    def shard(t, name):
        return jnp.swapaxes(t[name], 1, 2) if name in TRANSPOSED else t[name]

    wsh = {name: shard(w, name) for name in BIG}
    half = {name: wsh[name].shape[2] // 2 for name in BIG}
    payload = {name: wsh[name][0].astype(BF) for name in BIG}
    ws = {name: w[name] for name in SMALL}
    xs, mems, tgt = x[0], mem[0], loss_target[0]

    def chip_sums(names, grads, from_sib):
        return [_chip_sum(grads[n], s, place, "grads_chip_sum_" + n) for n, s in zip(names, from_sib)]

    def totals(names, sums, others):
        return [_total(own, o, place, "grads_total_" + n) for n, (_, own), o in zip(names, sums, others)]

    ((w_in4, conv_all),) = _run_phases([("gather", [payload["w_in"], conv_w[0]], [half["w_in"], None])], "gather_w_in")
    w_in_t = w_in4.reshape(N_IN, D)
    ws["conv_w"] = conv_all[0::2].transpose(1, 0, 2).reshape(1, 4, 1536)
    rest = [n for n in BIG if n != "w_in"]
    (h0, z, xbc, hq, hf, hi, hg, dtr), (gathered,) = _in_proj(
        xs, ws["norm_mix_w"], w_in_t, phases=[("gather", [payload[n] for n in rest], [half[n] for n in rest])])
    wg = dict(zip(rest, gathered))
    wg_t, wu_t = wg["ffn_w_gate"].reshape(FFN, D), wg["ffn_w_up"].reshape(FFN, D)
    wd = wg["ffn_w_down"].reshape(FFN, D)
    w_out_f = wg["w_out"].reshape(2 * D, D)
    wq, wo = wg["xa_wq"].reshape(D, D), wg["xa_wo"].reshape(D, D)
    dtb, alog = _pad_lanes(ws["dt_bias"]), _pad_lanes(ws["a_log"])
    dskip_full = jnp.repeat(ws["d_skip"], SSD_P, axis=1)
    cw, conv_bias = ws["conv_w"][0], ws["conv_b"]
    hlb = ws["hg_lower_bounds"]
    hg_fast = (jnp.min(jax.nn.softmax(hlb, axis=0)[0]) >= HG_LB_FLOOR).astype(jnp.int32).reshape(1)

    ya, yssd, u, st_ssd = _ssd_fwd(xbc, dtr, z, cw, conv_bias, dtb, alog, dskip_full, ws["ssd_norm_w"])
    ob, ohg, st_hg = _hg_fwd(hq, hf, hi, hg, hlb, ws["hg_norm_w"], hg_fast)
    kmem, vmem = _mem_kv(mems, ws["norm_mem_w"], wg["xa_wkv"])
    x1, x2, hxa, q, ox = _attn_fwd(xs, ya, ob, w_out_f, ws["norm_xa_w"], wq, kmem, vmem, wo)
    nfin = ws["norm_final_w"].reshape(1, D)
    dx2, hffn, act, dx3, dg, du, acc_f = _ffn_loss(x2, tgt, ws["norm_ffn_w"], nfin, wg_t, wu_t, wd)

    gb = {"ffn_w_gate": _matmul_tn(dg, hffn, "gw_gate").reshape(4, FFN // 4, D),
          "ffn_w_up": _matmul_tn(du, hffn, "gw_up").reshape(4, FFN // 4, D),
          "ffn_w_down": _matmul_tn(act, dx3, "gw_down").reshape(4, FFN // 4, D)}
    (dx1, dya, dob, dq, dk, dv, acc_a), (sib_ffn,) = _attn_bwd(
        dx2, x1, q, kmem, vmem, ws["norm_xa_w"], wq, wo, w_out_f,
        phases=[("sibling", [gb[n] for n in GROUP_FFN], [half[n] for n in GROUP_FFN])])
    sums_ffn = chip_sums(GROUP_FFN, gb, sib_ffn)
    g_nmem, gb["xa_wkv"] = _mem_kv_bwd(mems, ws["norm_mem_w"], wg["xa_wkv"], dk, dv)
    gb["w_out"] = _matmul_tn_pair(ya, ob, dx1, "gw_out").reshape(4, D // 2, D)
    gb["xa_wq"] = _matmul_tn(hxa, dq, "gw_q").reshape(4, D // 4, D)
    gb["xa_wo"] = _matmul_tn(ox, dx2, "gw_o").reshape(4, D // 4, D)
    (dhq, dhf, dhi, dhg, acc_h), (others_ffn, sib_attn) = _hg_bwd(
        dob, ohg, hq, hf, hi, hg, st_hg, hlb, ws["hg_norm_w"], hg_fast,
        phases=[("chips", [hb for hb, _ in sums_ffn], None),
                ("sibling", [gb[n] for n in GROUP_ATTN], [half[n] for n in GROUP_ATTN])])
    red_ffn = totals(GROUP_FFN, sums_ffn, others_ffn)
    sums_attn = chip_sums(GROUP_ATTN, gb, sib_attn)
    dz, dxbc, ddt, gconv, ghead, glane = _ssd_bwd(dya, yssd, z, u, xbc, dtr, st_ssd, cw, dtb, alog, dskip_full,
                                                  ws["ssd_norm_w"])
    gx, acc_i = _in_proj_bwd(xs, dx1, dz, dxbc, dhq, dhf, dhi, dhg, ddt, ws["norm_mix_w"], w_in_t)
    (gw_in_t,), (g_ffn, others_attn) = _gw_in(
        h0, dz, dxbc, ddt, dhq, dhf, dhi, dhg,
        phases=[("swap", red_ffn, [half[n] for n in GROUP_FFN]), ("chips", [hb for hb, _ in sums_attn], None)])
    red_attn = totals(GROUP_ATTN, sums_attn, others_attn)
    gb["w_in"] = gw_in_t.reshape(4, N_IN // 4, D)
    g_attn, (sib_in,) = _run_phases([("swap", red_attn, [half[n] for n in GROUP_ATTN]),
                                     ("sibling", [gb["w_in"]], [half["w_in"]])], "grads_w_in_to_sibling")
    sums_in = chip_sums(("w_in",), gb, [sib_in])
    ((others_in,),) = _run_phases([("chips", [sums_in[0][0]], None)], "grads_w_in_to_chips")
    red_in = totals(("w_in",), sums_in, [others_in])

    gs = {
        "norm_mix_w": acc_i[0:1], "conv_w": gconv[0:4][None], "conv_b": gconv[4:5],
        "dt_bias": ghead[0:1, :NH_SSD], "a_log": ghead[2:3, :NH_SSD], "d_skip": ghead[3:4, :NH_SSD],
        "ssd_norm_w": glane[1:2], "hg_lower_bounds": acc_h[2:4], "hg_norm_w": acc_h[4:5, :128],
        "norm_xa_w": acc_a[0:1], "norm_mem_w": g_nmem, "norm_ffn_w": acc_f[2:3], "norm_final_w": acc_f[1],
    }
    loss = (0.5 / D) * jnp.sum(acc_f[0])
    small_parts = [gs[name] for name in SMALL] + [loss.reshape(1)]
    small_shapes = [gs[name].shape for name in SMALL] + [(1,)]
    (g_in,), (packed,) = _run_phases([("swap", red_in, [half["w_in"]]),
                                      ("gather", [_pack_small(small_parts)], [None])], "grads_finish")
    g_big = dict(zip(GROUP_FFN + GROUP_ATTN + ("w_in",), g_ffn + g_attn + [g_in]))
    small = _unpack_small(_sum8(packed, "small_total"), small_shapes)
    g_small = dict(zip(SMALL, small[:-1]))
    loss_all = small[-1][0]
    g_small["conv_w"] = lax.dynamic_slice_in_dim(g_small["conv_w"], chip * 384, 384, 2)

    grads, delta, new_m, new_v = {}, {}, {}, {}
    for name in BIG:
        outs = (g_big[name][None],) + tuple(_adamw(wsh[name], g_big[name], shard(m, name), shard(v, name),
                                                   "adamw_" + name))
        if name in TRANSPOSED:
            outs = tuple(jnp.swapaxes(o, 1, 2) for o in outs)
        grads[name], delta[name], new_m[name], new_v[name] = outs
    shapes = [w[name].shape for name in SMALL]
    packs = [_pack_small([t[name] for name in SMALL]) for t in (w, g_small, m, v)]
    outs = _adamw(packs[0][None], packs[1], packs[2][None], packs[3][None], "adamw_small")
    for name, g_, d_, nm_, nv_ in zip(SMALL, [g_small[n] for n in SMALL], *[_unpack_small(o[0], shapes) for o in outs]):
        grads[name] = g_.reshape(w[name].shape)
        delta[name], new_m[name], new_v[name] = d_, nm_, nv_

    return (loss_all, gx[None], *[grads[n] for n in WEIGHTS], *[delta[n] for n in WEIGHTS],
            *[new_m[n] for n in WEIGHTS], *[new_v[n] for n in WEIGHTS])
```

```python
import jax
import jax.numpy as jnp
from jax import lax
from jax.experimental import pallas as pl
from jax.experimental.pallas import tpu as pltpu

F32 = jnp.float32
BF = jnp.bfloat16
MESH = pl.DeviceIdType.MESH
SDS = jax.ShapeDtypeStruct
ANY = pl.BlockSpec(memory_space=pl.ANY)

D = 1024
EPS = 1e-6
NH_SSD = 16
SSD_P = 64
NH_HG = 8
Q = 128
CH = 4
SUB = 32
NSUB = Q // SUB
HG_LB_FLOOR = 1e-2
XA_HEADS = 4
XA_HD = 256
MEM_LEN = 256
FFN = 2816
TL = 512
TL_FFN = 256
VMEM_LIMIT = 56 << 20
MATMUL_VMEM = 40 << 20

N_IN = 6672
Z0, XBC0, DT0, HQ0, HF0, HI0, HG0 = 0, 1024, 2560, 2576, 3600, 4624, 5648

ADAM_LR, ADAM_B1, ADAM_B2, ADAM_EPS, ADAM_WD, ADAM_STEP = 0.001, 0.9, 0.999, 1e-08, 0.01, 10

BIG = ("w_in", "w_out", "xa_wq", "xa_wkv", "xa_wo", "ffn_w_gate", "ffn_w_up", "ffn_w_down")
TRANSPOSED = ("w_in", "ffn_w_gate", "ffn_w_up")
SMALL = ("norm_mix_w", "conv_w", "conv_b", "dt_bias", "a_log", "d_skip", "ssd_norm_w", "hg_lower_bounds",
         "hg_norm_w", "norm_xa_w", "norm_mem_w", "norm_ffn_w", "norm_final_w")
WEIGHTS = ("norm_mix_w", "w_in", "conv_w", "conv_b", "dt_bias", "a_log", "d_skip", "ssd_norm_w", "hg_lower_bounds",
           "hg_norm_w", "w_out", "norm_xa_w", "norm_mem_w", "xa_wq", "xa_wkv", "xa_wo", "norm_ffn_w", "ffn_w_gate",
           "ffn_w_up", "ffn_w_down", "norm_final_w")


def _cparams():
    return pltpu.CompilerParams(dimension_semantics=("arbitrary",), vmem_limit_bytes=VMEM_LIMIT)


def _const(shape):
    return pl.BlockSpec(shape, lambda i: (0,) * len(shape))


def _resident(shape):
    return pl.BlockSpec(shape, lambda i: (0,) * len(shape), pipeline_mode=pl.Buffered(1))


def _rows(tl, n):
    return pl.BlockSpec((tl, n), lambda i: (i, 0))


def _dot(a, b):
    return jnp.dot(a.astype(BF), b.astype(BF), preferred_element_type=F32)


def _dot_nt(a, b):
    return lax.dot_general(a.astype(BF), b.astype(BF), (((1,), (1,)), ((), ())), preferred_element_type=F32)


def _dot_tn(a, b):
    return lax.dot_general(a.astype(BF), b.astype(BF), (((0,), (0,)), ((), ())), preferred_element_type=F32)


def _split(v, passes):
    parts, rest = [], v
    for p in range(passes):
        hi = rest.astype(BF)
        parts.append(hi)
        if p + 1 < passes:
            rest = rest - hi.astype(F32)
    return parts


def _sel_dot(a, sel, passes=3):
    sb = sel.astype(BF)
    out = None
    for part in _split(a, passes):
        t = jnp.dot(part, sb, preferred_element_type=F32)
        out = t if out is None else out + t
    return out


def _dot_sel(sel, b, passes=3):
    sb = sel.astype(BF)
    out = None
    for part in _split(b, passes):
        t = jnp.dot(sb, part, preferred_element_type=F32)
        out = t if out is None else out + t
    return out


def _iota(shape, dim):
    return lax.broadcasted_iota(jnp.int32, shape, dim)


def _sigmoid(v):
    return 0.5 * jnp.tanh(0.5 * v) + 0.5


def _rms(v, w):
    r = lax.rsqrt(jnp.mean(v * v, axis=-1, keepdims=True) + EPS)
    n = v * r
    return n * w, n, r


def _rms_bwd(dy, n, r, w):
    dn = dy * w
    return r * (dn - n * jnp.mean(dn * n, axis=-1, keepdims=True)), dy * n


def _colsum(v):
    return jnp.sum(v, axis=0, keepdims=True)


def _zero_first(*refs):
    @pl.when(pl.program_id(0) == 0)
    def _():
        for r in refs:
            r[...] = jnp.zeros_like(r)


def _in_proj(x, nw, wt, phases=()):
    L = x.shape[0]
    tl = min(TL, L)

    def body(x_ref, nw_ref, w_ref, h0_ref, z_ref, xbc_ref, hq_ref, hf_ref, hi_ref, hg_ref, dt_ref):
        h, _, _ = _rms(x_ref[...], nw_ref[...])
        hb = h.astype(BF)
        h0_ref[...] = hb

        def proj(a, b):
            return _dot_nt(hb, w_ref[a:b, :])

        z_ref[...] = proj(Z0, XBC0).astype(BF)
        xbc_ref[...] = proj(XBC0, DT0).astype(BF)
        dt_ref[...] = proj(DT0, DT0 + 128)
        hq_ref[...] = proj(HQ0, HF0).astype(BF)
        hf_ref[...] = proj(HF0, HI0)
        hi_ref[...] = proj(HI0, HG0).astype(BF)
        hg_ref[...] = proj(HG0, N_IN).astype(BF)

    outs = [SDS((L, D), BF), SDS((L, D), BF), SDS((L, 1536), BF), SDS((L, D), BF), SDS((L, D), F32),
            SDS((L, D), BF), SDS((L, D), BF), SDS((L, 128), F32)]
    steps = L // tl
    return _call(body, (x, nw, wt), name="in_proj", grid=(steps,),
                 in_specs=[_rows(tl, D), _const((1, D)), _resident((N_IN, D))],
                 out_specs=[_rows(tl, o.shape[1]) for o in outs], out_shape=outs, phases=phases,
                 mid_step=(3 * steps) // 4)


def _mem_kv(mem, nw, wkv4):
    def body(m_ref, nw_ref, w_ref, k_ref, v_ref):
        m, _, _ = _rms(m_ref[...], nw_ref[...])
        mb = m.astype(BF)
        for i in range(2):
            sl = slice(512 * i, 512 * i + 512)
            k_ref[:, sl] = jnp.dot(mb, w_ref[i], preferred_element_type=F32).astype(BF)
            v_ref[:, sl] = jnp.dot(mb, w_ref[2 + i], preferred_element_type=F32).astype(BF)

    outs = [SDS((MEM_LEN, D), BF)] * 2
    return pl.pallas_call(
        body, grid=(1,), name="mem_kv",
        in_specs=[_const((MEM_LEN, D)), _const((1, D)), _const((4, D, 512))],
        out_specs=[_const((MEM_LEN, D))] * 2, out_shape=outs, compiler_params=_cparams())(mem, nw, wkv4)


def _mem_kv_bwd(mem, nw, wkv4, dk, dv):
    def body(m_ref, nw_ref, w_ref, dk_ref, dv_ref, gnw_ref, gw_ref):
        m, n, _ = _rms(m_ref[...], nw_ref[...])
        mb = m.astype(BF)
        dm = jnp.zeros((MEM_LEN, D), F32)
        for i in range(4):
            src = dk_ref if i < 2 else dv_ref
            d = src[:, 512 * (i % 2):512 * (i % 2) + 512].astype(BF)
            gw_ref[i] = _dot_tn(mb, d)
            dm = dm + _dot_nt(d, w_ref[i])
        gnw_ref[...] = _colsum(dm * n)

    return pl.pallas_call(
        body, grid=(1,), name="mem_kv_bwd",
        in_specs=[_const((MEM_LEN, D)), _const((1, D)), _const((4, D, 512)), _const((MEM_LEN, D)), _const((MEM_LEN, D))],
        out_specs=[_const((1, D)), _const((4, D, 512))],
        out_shape=[SDS((1, D), F32), SDS((4, D, 512), F32)], compiler_params=_cparams())(mem, nw, wkv4, dk, dv)


def _softmax_rows(sc):
    e = jnp.exp(sc - jnp.max(sc, axis=-1, keepdims=True))
    return e * (1.0 / jnp.sum(e, axis=-1, keepdims=True))


def _attn_fwd(x, ya, ob, w_out, nxa, wq, k, v, wo):
    L = x.shape[0]
    tl = min(TL, L)
    scale = XA_HD ** -0.5

    def body(x_ref, ya_ref, ob_ref, wout_ref, nxa_ref, wq_ref, k_ref, v_ref, wo_ref,
             x1_ref, x2_ref, hxa_ref, q_ref, ox_ref):
        x1 = x_ref[...] + jnp.dot(ya_ref[...], wout_ref[:D, :], preferred_element_type=F32) \
            + jnp.dot(ob_ref[...], wout_ref[D:, :], preferred_element_type=F32)
        x1_ref[...] = x1
        h, _, _ = _rms(x1, nxa_ref[...])
        hb = h.astype(BF)
        hxa_ref[...] = hb
        qb = jnp.dot(hb, wq_ref[...], preferred_element_type=F32).astype(BF)
        q_ref[...] = qb
        heads = [slice(hd * XA_HD, (hd + 1) * XA_HD) for hd in range(XA_HEADS)]
        ps = [_softmax_rows(_dot_nt(qb[:, sl], k_ref[:, sl]) * scale) for sl in heads]
        oxs = [_dot(p, v_ref[:, sl]) for p, sl in zip(ps, heads)]
        oxb = jnp.concatenate(oxs, axis=1).astype(BF)
        ox_ref[...] = oxb
        x2_ref[...] = x1 + jnp.dot(oxb, wo_ref[...], preferred_element_type=F32)

    outs = [SDS((L, D), F32), SDS((L, D), F32), SDS((L, D), BF), SDS((L, D), BF), SDS((L, D), BF)]
    return pl.pallas_call(
        body, grid=(L // tl,), name="attn_fwd",
        in_specs=[_rows(tl, D), _rows(tl, D), _rows(tl, D), _resident((2 * D, D)), _const((1, D)), _resident((D, D)),
                  _resident((MEM_LEN, D)), _resident((MEM_LEN, D)), _resident((D, D))],
        out_specs=[_rows(tl, D)] * 5, out_shape=outs, compiler_params=_cparams())(x, ya, ob, w_out, nxa, wq, k, v, wo)


def _ffn_loss(x2, tgt, nffn, nfin, wgt, wut, wd):
    L = x2.shape[0]
    tl = min(TL_FFN, L)

    def body(x2_ref, t_ref, nffn_ref, nfin_ref, wg_ref, wu_ref, wd_ref,
             dx2_ref, h_ref, a_ref, dx3_ref, dg_ref, du_ref, acc_ref):
        _zero_first(acc_ref)
        x2v = x2_ref[...]
        h, n2, r2 = _rms(x2v, nffn_ref[...])
        hb = h.astype(BF)
        h_ref[...] = hb
        g = _dot_nt(hb, wg_ref[...])
        u = _dot_nt(hb, wu_ref[...])
        sg = _sigmoid(g)
        ab = (g * sg * u).astype(BF)
        a_ref[...] = ab
        x3 = x2v + jnp.dot(ab, wd_ref[...], preferred_element_type=F32)
        y, n3, r3 = _rms(x3, nfin_ref[...])
        err = y - t_ref[...]
        acc_ref[0:1, :] += _colsum(err * err)
        dx3, dwf = _rms_bwd(err * (1.0 / D), n3, r3, nfin_ref[...])
        acc_ref[1:2, :] += _colsum(dwf)
        dx3b = dx3.astype(BF)
        dx3_ref[...] = dx3b
        da = _dot_nt(dx3b, wd_ref[...])
        dgb = (da * u * sg * (1.0 + g * (1.0 - sg))).astype(BF)
        dub = (da * g * sg).astype(BF)
        dg_ref[...] = dgb
        du_ref[...] = dub
        dh = jnp.dot(dgb, wg_ref[...], preferred_element_type=F32) + jnp.dot(dub, wu_ref[...], preferred_element_type=F32)
        dn, dwn = _rms_bwd(dh, n2, r2, nffn_ref[...])
        acc_ref[2:3, :] += _colsum(dwn)
        dx2_ref[...] = dx3 + dn

    outs = [SDS((L, D), F32), SDS((L, D), BF), SDS((L, FFN), BF), SDS((L, D), BF), SDS((L, FFN), BF),
            SDS((L, FFN), BF), SDS((8, D), F32)]
    wspec = _resident((FFN, D))
    return pl.pallas_call(
        body, grid=(L // tl,), name="ffn_loss",
        in_specs=[_rows(tl, D), _rows(tl, D), _const((1, D)), _const((1, D)), wspec, wspec, wspec],
        out_specs=[_rows(tl, D), _rows(tl, D), _rows(tl, FFN), _rows(tl, D), _rows(tl, FFN), _rows(tl, FFN),
                   _const((8, D))],
        out_shape=outs, compiler_params=_cparams())(x2, tgt, nffn, nfin, wgt, wut, wd)


def _attn_bwd(dx2, x1, q, k, v, nxa, wq, wo, w_out, phases=()):
    L = dx2.shape[0]
    tl = min(TL, L)
    scale = XA_HD ** -0.5

    def body(dx2_ref, x1_ref, q_ref, k_ref, v_ref, nxa_ref, wq_ref, wo_ref, wout_ref,
             dx1_ref, dya_ref, dob_ref, dq_ref, dk_ref, dv_ref, acc_ref):
        _zero_first(dk_ref, dv_ref, acc_ref)
        dx2v = dx2_ref[...]
        dox = _dot_nt(dx2v, wo_ref[...]).astype(BF)
        qb = q_ref[...]
        heads = [slice(hd * XA_HD, (hd + 1) * XA_HD) for hd in range(XA_HEADS)]
        ps = [_softmax_rows(_dot_nt(qb[:, sl], k_ref[:, sl]) * scale) for sl in heads]
        dps = [_dot_nt(dox[:, sl], v_ref[:, sl]) for sl in heads]
        dss = [(p * (dp - jnp.sum(dp * p, axis=-1, keepdims=True)) * scale).astype(BF) for p, dp in zip(ps, dps)]
        for sl, p, ds in zip(heads, ps, dss):
            dv_ref[:, sl] += _dot_tn(p, dox[:, sl])
            dk_ref[:, sl] += _dot_tn(ds, qb[:, sl])
        dqs = [_dot(ds, k_ref[:, sl]) for sl, ds in zip(heads, dss)]
        dqb = jnp.concatenate(dqs, axis=1).astype(BF)
        dq_ref[...] = dqb
        dh = _dot_nt(dqb, wq_ref[...])
        _, n1, r1 = _rms(x1_ref[...], nxa_ref[...])
        dn, dwn = _rms_bwd(dh, n1, r1, nxa_ref[...])
        acc_ref[0:1, :] += _colsum(dwn)
        dx1 = dx2v + dn
        dx1_ref[...] = dx1
        dx1b = dx1.astype(BF)
        dya_ref[...] = _dot_nt(dx1b, wout_ref[:D, :]).astype(BF)
        dob_ref[...] = _dot_nt(dx1b, wout_ref[D:, :]).astype(BF)

    outs = [SDS((L, D), F32), SDS((L, D), BF), SDS((L, D), BF), SDS((L, D), BF), SDS((MEM_LEN, D), F32),
            SDS((MEM_LEN, D), F32), SDS((8, D), F32)]
    return _call(body, (dx2, x1, q, k, v, nxa, wq, wo, w_out), name="attn_bwd", grid=(L // tl,),
                 in_specs=[_rows(tl, D), _rows(tl, D), _rows(tl, D), _resident((MEM_LEN, D)), _resident((MEM_LEN, D)),
                           _const((1, D)), _resident((D, D)), _resident((D, D)), _resident((2 * D, D))],
                 out_specs=[_rows(tl, D)] * 4 + [_const((MEM_LEN, D)), _const((MEM_LEN, D)), _const((8, D))],
                 out_shape=outs, phases=phases)


def _in_proj_bwd(x, dx1, dz, dxbc, dhq, dhf, dhi, dhg, ddt, nw, wt):
    L = x.shape[0]
    tl = min(TL, L)

    def body(x_ref, dx1_ref, dz_ref, dxbc_ref, dhq_ref, dhf_ref, dhi_ref, dhg_ref, ddt_ref, nw_ref, w_ref,
             gx_ref, acc_ref):
        _zero_first(acc_ref)
        dh = _dot(dz_ref[...], w_ref[Z0:XBC0, :]) + _dot(dxbc_ref[...], w_ref[XBC0:DT0, :]) \
            + _dot(ddt_ref[...], w_ref[DT0:DT0 + 128, :]) + _dot(dhq_ref[...], w_ref[HQ0:HF0, :]) \
            + _dot(dhf_ref[...], w_ref[HF0:HI0, :]) + _dot(dhi_ref[...], w_ref[HI0:HG0, :]) \
            + _dot(dhg_ref[...], w_ref[HG0:N_IN, :])
        _, n, r = _rms(x_ref[...], nw_ref[...])
        dn, dwn = _rms_bwd(dh, n, r, nw_ref[...])
        acc_ref[0:1, :] += _colsum(dwn)
        gx_ref[...] = dx1_ref[...] + dn

    return pl.pallas_call(
        body, grid=(L // tl,), name="in_proj_bwd",
        in_specs=[_rows(tl, D), _rows(tl, D), _rows(tl, D), _rows(tl, 1536), _rows(tl, D), _rows(tl, D), _rows(tl, D),
                  _rows(tl, D), _rows(tl, 128), _const((1, D)), _resident((N_IN, D))],
        out_specs=[_rows(tl, D), _const((8, D))], out_shape=[SDS((L, D), F32), SDS((8, D), F32)],
        compiler_params=_cparams())(x, dx1, dz, dxbc, dhq, dhf, dhi, dhg, ddt, nw, wt)


def _gw_in(h0, dz, dxbc, ddt, dhq, dhf, dhi, dhg, phases=()):
    L = h0.shape[0]
    tl = min(512, L)

    def body(h_ref, dz_ref, dxbc_ref, ddt_ref, dhq_ref, dhf_ref, dhi_ref, dhg_ref, o_ref):
        _zero_first(o_ref)
        hb = h_ref[...]
        o_ref[Z0:XBC0, :] += _dot_tn(dz_ref[...], hb)
        o_ref[XBC0:DT0, :] += _dot_tn(dxbc_ref[...], hb)
        o_ref[DT0:HQ0, :] += _dot_tn(ddt_ref[...], hb)[0:NH_SSD, :]
        o_ref[HQ0:HF0, :] += _dot_tn(dhq_ref[...], hb)
        o_ref[HF0:HI0, :] += _dot_tn(dhf_ref[...], hb)
        o_ref[HI0:HG0, :] += _dot_tn(dhi_ref[...], hb)
        o_ref[HG0:N_IN, :] += _dot_tn(dhg_ref[...], hb)

    return _call(body, (h0, dz, dxbc, ddt, dhq, dhf, dhi, dhg), name="gw_in", grid=(L // tl,),
                 in_specs=[_rows(tl, D), _rows(tl, D), _rows(tl, 1536), _rows(tl, 128), _rows(tl, D), _rows(tl, D),
                           _rows(tl, D), _rows(tl, D)],
                 out_specs=[_const((N_IN, D))], out_shape=[SDS((N_IN, D), F32)], phases=phases)


def _token_tile(L, out_bytes, row_bytes):
    tl = min(2048, L)
    while tl > 256 and out_bytes + 2 * tl * row_bytes > MATMUL_VMEM:
        tl //= 2
    return tl


def _matmul_tn(a, b, name):
    L, M = a.shape
    N = b.shape[1]
    tl = _token_tile(L, 4 * M * N, M * a.dtype.itemsize + N * b.dtype.itemsize)

    def body(a_ref, b_ref, o_ref):
        _zero_first(o_ref)
        o_ref[...] += _dot_tn(a_ref[...], b_ref[...])

    return pl.pallas_call(
        body, grid=(L // tl,), name=name, in_specs=[_rows(tl, M), _rows(tl, N)], out_specs=_const((M, N)),
        out_shape=SDS((M, N), F32), compiler_params=_cparams())(a, b)


def _matmul_tn_pair(a0, a1, b, name):
    L, M = a0.shape
    N = b.shape[1]
    tl = _token_tile(L, 8 * M * N, 2 * M * a0.dtype.itemsize + N * b.dtype.itemsize)

    def body(a0_ref, a1_ref, b_ref, o_ref):
        _zero_first(o_ref)
        bv = b_ref[...].astype(BF)
        o_ref[:M, :] += _dot_tn(a0_ref[...], bv)
        o_ref[M:, :] += _dot_tn(a1_ref[...], bv)

    return pl.pallas_call(
        body, grid=(L // tl,), name=name, in_specs=[_rows(tl, M), _rows(tl, M), _rows(tl, N)],
        out_specs=_const((2 * M, N)), out_shape=SDS((2 * M, N), F32), compiler_params=_cparams())(a0, a1, b)


def _head_expand():
    e = (jnp.right_shift(_iota((128, D), 1), 6) == _iota((128, D), 0)).astype(BF)
    et = (jnp.right_shift(_iota((D, 128), 0), 6) == _iota((D, 128), 1)).astype(BF)
    return e, et


def _conv_shifts(cur, other, up):
    rows = _iota((Q, 1), 0)
    out = []
    for s in (1, 2, 3):
        if up:
            out.append(jnp.where(rows >= Q - s, pltpu.roll(other, Q - s, 0), pltpu.roll(cur, Q - s, 0)))
        else:
            out.append(jnp.where(rows < s, pltpu.roll(other, s, 0), pltpu.roll(cur, s, 0)))
    return out


def _ssd_pre(u, dtr, dtb, alog):
    e, et = _head_expand()
    sgu = _sigmoid(u)
    xc = u * sgu
    lane = _iota((1, 128), 1)
    hmask = (lane < NH_SSD).astype(F32)
    pre = dtr + dtb
    dt = (jnp.maximum(pre, 0.0) + jnp.log(1.0 + jnp.exp(-jnp.abs(pre)))) * hmask
    a_row = -jnp.exp(alog)
    causal = _iota((Q, Q), 1) <= _iota((Q, Q), 0)
    tri = causal.astype(BF)
    acum = _dot_sel(tri, dt * a_row)
    acum_full = _sel_dot(acum, e)
    alast_full = acum_full[Q - 1:Q, :]
    dt_full = _sel_dot(dt, e)
    xs = xc[:, :D]
    return dict(e=e, et=et, sgu=sgu, xs=xs, bm=xc[:, D:D + 256], cm=xc[:, D + 256:], hmask=hmask, pre=pre, dt=dt,
                a_row=a_row, causal=causal, tri=tri, acum=acum, acum_t=acum.T, eA_full=jnp.exp(acum_full),
                dte_full=jnp.exp(alast_full - acum_full), dt_full=dt_full, xdt=xs * dt_full)


def _ssd_decay(pre, hh, cb):
    seg = pre["acum"][:, hh:hh + 1] - pre["acum_t"][hh:hh + 1, :]
    lm = jnp.where(pre["causal"], jnp.exp(jnp.minimum(seg, 0.0)), 0.0)
    return lm, cb * lm


def _ssd_fwd(xbc, dtr, z, conv_w, conv_b, dtb, alog, dskip_full, nw):
    L = xbc.shape[0]
    nc = L // Q

    def chunk(ck, xbc_ref, dtr_ref, z_ref, cw_ref, cb_ref, dtb_ref, alog_ref, dsk_ref, nw_ref,
              ya_ref, y_ref, u_ref, st_ref, prev_ref, s_ref):
        tok = slice(Q * ck, Q * ck + Q)
        xr = xbc_ref[tok, :].astype(F32)
        sh = _conv_shifts(xr, prev_ref[...], up=False)
        u = cb_ref[...] + cw_ref[3:4, :] * xr + cw_ref[2:3, :] * sh[0] + cw_ref[1:2, :] * sh[1] + cw_ref[0:1, :] * sh[2]
        prev_ref[...] = xr
        ub = u.astype(BF)
        u_ref[tok, :] = ub
        pre = _ssd_pre(ub.astype(F32), dtr_ref[tok, :], dtb_ref[...], alog_ref[...])
        lo = _iota((1, 128), 1) < SSD_P
        s_old = s_ref[...]
        st_ref[ck] = s_old
        ys = []
        for g in range(2):
            bg, cg = pre["bm"][:, 128 * g:128 * g + 128], pre["cm"][:, 128 * g:128 * g + 128]
            cb = _dot_nt(cg, bg)
            gs = slice(512 * g, 512 * g + 512)
            yd = []
            for j in range(4 * g, 4 * g + 4):
                xp = pre["xdt"][:, 128 * j:128 * j + 128].astype(BF)
                _, m0 = _ssd_decay(pre, 2 * j, cb)
                _, m1 = _ssd_decay(pre, 2 * j + 1, cb)
                yd.append(jnp.where(lo, _dot(m0, xp), _dot(m1, xp)))
            yoff = _dot_nt(cg, s_old[gs, :]) * pre["eA_full"][:, gs]
            ys.append(jnp.concatenate(yd, axis=1) + yoff)
            st = _dot_tn((pre["xdt"] * pre["dte_full"])[:, gs], bg)
            cdcol = jnp.exp(_dot_sel(pre["et"][gs, :], pre["acum_t"])[:, Q - 1:Q])
            s_ref[gs, :] = s_old[gs, :] * cdcol + st
        y = jnp.concatenate(ys, axis=1) + dsk_ref[...] * pre["xs"]
        yb = y.astype(BF)
        y_ref[tok, :] = yb
        zf = z_ref[tok, :].astype(F32)
        yz = yb.astype(F32) * zf * _sigmoid(zf)
        outs = []
        for g in range(2):
            gs = slice(512 * g, 512 * g + 512)
            o, _, _ = _rms(yz[:, gs], nw_ref[:, gs])
            outs.append(o)
        ya_ref[tok, :] = jnp.concatenate(outs, axis=1).astype(BF)

    def body(*refs):
        _zero_first(*refs[-2:])
        for ck in range(CH):
            chunk(ck, *refs)

    outs = [SDS((L, D), BF), SDS((L, D), BF), SDS((L, 1536), BF), SDS((nc, D, 128), F32)]
    return pl.pallas_call(
        body, grid=(nc // CH,), name="ssd_fwd",
        in_specs=[_rows(CH * Q, 1536), _rows(CH * Q, 128), _rows(CH * Q, D), _const((4, 1536)), _const((1, 1536)), _const((1, 128)),
                  _const((1, 128)), _const((1, D)), _const((1, D))],
        out_specs=[_rows(CH * Q, D), _rows(CH * Q, D), _rows(CH * Q, 1536),
                   pl.BlockSpec((CH, D, 128), lambda i: (i, 0, 0))],
        out_shape=outs, scratch_shapes=[pltpu.VMEM((Q, 1536), F32), pltpu.VMEM((D, 128), F32)],
        compiler_params=_cparams())(xbc, dtr, z, conv_w, conv_b, dtb, alog, dskip_full, nw)


def _ssd_bwd(dya, y, z, u, xbc, dtr, states, conv_w, dtb, alog, dskip_full, nw):
    L = dya.shape[0]
    nc = L // Q

    def chunk(ck, step, dya_ref, y_ref, z_ref, u_ref, xc_ref, dtr_ref, st_ref, cw_ref, dtb_ref, alog_ref, dsk_ref, nw_ref,
              dz_ref, dxbc_ref, ddt_ref, gconv_ref, ghead_ref, glane_ref, gs_ref, ndu_ref):
        tok = slice(Q * ck, Q * ck + Q)
        uf = u_ref[tok, :].astype(F32)
        pre = _ssd_pre(uf, dtr_ref[tok, :], dtb_ref[...], alog_ref[...])
        e, et, xs, xdt = pre["e"], pre["et"], pre["xs"], pre["xdt"]
        lane = _iota((1, 128), 1)
        lo = lane < SSD_P
        sub = _iota((128, 1), 0)
        zf = z_ref[tok, :].astype(F32)
        sgz = _sigmoid(zf)
        sz = zf * sgz
        yv = y_ref[tok, :].astype(F32)
        yz = yv * sz
        dyav = dya_ref[tok, :].astype(F32)
        dyz, dnw = [], []
        for g in range(2):
            gs = slice(512 * g, 512 * g + 512)
            _, n, r = _rms(yz[:, gs], nw_ref[:, gs])
            dv, dw = _rms_bwd(dyav[:, gs], n, r, nw_ref[:, gs])
            dyz.append(dv)
            dnw.append(dw)
        dyz = jnp.concatenate(dyz, axis=1)
        glane_ref[1:2, :] += _colsum(jnp.concatenate(dnw, axis=1))
        dy = dyz * sz
        dz_ref[tok, :] = (dyz * yv * sgz * (1.0 + zf * (1.0 - sgz))).astype(BF)
        glane_ref[0:1, :] += _colsum(dy * xs)
        dxs = dsk_ref[...] * dy

        s_in = st_ref[ck]
        gst = gs_ref[...]
        gy = dy * pre["eA_full"]
        xdte = xdt * pre["dte_full"]
        dacum = jnp.zeros((Q, 128), F32)
        dacum_t = jnp.zeros((128, Q), F32)
        dxdt, dacum_full, ddte_full, dbs, dcs = [], [], [], [], []
        for g in range(2):
            gs = slice(512 * g, 512 * g + 512)
            bg, cg = pre["bm"][:, 128 * g:128 * g + 128], pre["cm"][:, 128 * g:128 * g + 128]
            sg_, dg_ = s_in[gs, :], gst[gs, :]
            yoff = _dot_nt(cg, sg_) * pre["eA_full"][:, gs]
            dc = _dot(gy[:, gs], sg_)
            dsin = _dot_tn(gy[:, gs], cg)
            dacum_full.append(dy[:, gs] * yoff)
            tg = _dot_nt(bg, dg_)
            ddte_full.append(tg * xdt[:, gs])
            db = _dot(xdte[:, gs], dg_)
            cb = _dot_nt(cg, bg)
            dcb = jnp.zeros((Q, Q), F32)
            dxg = []
            for j in range(4 * g, 4 * g + 4):
                xp = xdt[:, 128 * j:128 * j + 128].astype(BF)
                dyp = dy[:, 128 * j:128 * j + 128]
                dxp = jnp.zeros((Q, 128), F32)
                for idx in range(2):
                    hh = 2 * j + idx
                    lm, m = _ssd_decay(pre, hh, cb)
                    dym = jnp.where(lo if idx == 0 else jnp.logical_not(lo), dyp, 0.0).astype(BF)
                    dm = jnp.where(pre["causal"], _dot_nt(dym, xp), 0.0)
                    w = dm * m
                    dacum = dacum + jnp.where(lane == hh, jnp.sum(w, axis=1, keepdims=True), 0.0)
                    dacum_t = dacum_t + jnp.where(sub == hh, jnp.sum(w, axis=0, keepdims=True), 0.0)
                    dcb = dcb + dm * lm
                    dxp = dxp + _dot_tn(m, dym)
                dxg.append(dxp)
            dxdt.append(jnp.concatenate(dxg, axis=1) + tg * pre["dte_full"][:, gs])
            dcs.append(dc + _dot(dcb, bg))
            dbs.append(db + _dot_tn(dcb, cg))
            cdcol = jnp.exp(_dot_sel(et[gs, :], pre["acum_t"])[:, Q - 1:Q])
            gs_ref[gs, :] = dsin + dg_ * cdcol
        dxdt = jnp.concatenate(dxdt, axis=1)
        dacum = dacum + _sel_dot(jnp.concatenate(dacum_full, axis=1), et, 2) - dacum_t.T
        alast = pre["acum"][Q - 1:Q, :]
        dte = jnp.exp(alast - pre["acum"])
        ddte = _sel_dot(jnp.concatenate(ddte_full, axis=1), et, 2) * dte
        dacum = dacum - ddte
        dcd_col = jnp.sum(_dot_sel(e, gst * s_in, 2), axis=1, keepdims=True)
        dcd_row = jnp.broadcast_to(dcd_col, (128, 128)).T[0:1, :]
        dalast = _colsum(ddte) + dcd_row * jnp.exp(alast)
        dacum = dacum + jnp.where(_iota((Q, 1), 0) == Q - 1, dalast, 0.0)
        ddt = _sel_dot(dxdt * xs, et, 2)
        dxs = dxs + dxdt * pre["dt_full"]
        dda = _dot_sel((_iota((Q, Q), 1) >= _iota((Q, Q), 0)).astype(BF), dacum)
        ddt = ddt + dda * pre["a_row"]
        ghead_ref[1:2, :] += _colsum(dda * pre["dt"])
        ddtr = ddt * _sigmoid(pre["pre"]) * pre["hmask"]
        ghead_ref[0:1, :] += _colsum(ddtr)
        ddt_ref[tok, :] = ddtr

        dxc = jnp.concatenate([dxs] + dbs + dcs, axis=1)
        sgu = pre["sgu"]
        du = dxc * sgu * (1.0 + uf * (1.0 - sgu))
        shu = _conv_shifts(du, ndu_ref[...], up=True)
        dxr = cw_ref[3:4, :] * du + cw_ref[2:3, :] * shu[0] + cw_ref[1:2, :] * shu[1] + cw_ref[0:1, :] * shu[2]
        ndu_ref[...] = du
        dxbc_ref[tok, :] = dxr.astype(BF)
        xr = xc_ref[tok, :].astype(F32)
        gconv_ref[3:4, :] += _colsum(du * xr)
        gconv_ref[2:3, :] += _colsum(shu[0] * xr)
        gconv_ref[1:2, :] += _colsum(shu[1] * xr)
        gconv_ref[0:1, :] += _colsum(shu[2] * xr)
        gconv_ref[4:5, :] += _colsum(du)

        @pl.when(jnp.logical_and(step == nc // CH - 1, ck == 0))
        def _():
            ghead_ref[2:3, :] = ghead_ref[1:2, :] * pre["a_row"]
            ghead_ref[3:4, :] = _sel_dot(glane_ref[...], et)[0:1, :]

    def body(*refs):
        _zero_first(*refs[-5:])
        for ck in reversed(range(CH)):
            chunk(ck, pl.program_id(0), *refs)

    rev = lambda i: (nc // CH - 1 - i, 0)
    outs = [SDS((L, D), BF), SDS((L, 1536), BF), SDS((L, 128), F32), SDS((8, 1536), F32), SDS((8, 128), F32),
            SDS((8, D), F32)]
    return pl.pallas_call(
        body, grid=(nc // CH,), name="ssd_bwd",
        in_specs=[pl.BlockSpec((CH * Q, D), rev), pl.BlockSpec((CH * Q, D), rev), pl.BlockSpec((CH * Q, D), rev),
                  pl.BlockSpec((CH * Q, 1536), rev), pl.BlockSpec((CH * Q, 1536), rev),
                  pl.BlockSpec((CH * Q, 128), rev), pl.BlockSpec((CH, D, 128), lambda i: (nc // CH - 1 - i, 0, 0)),
                  _const((4, 1536)), _const((1, 128)), _const((1, 128)), _const((1, D)), _const((1, D))],
        out_specs=[pl.BlockSpec((CH * Q, D), rev), pl.BlockSpec((CH * Q, 1536), rev), pl.BlockSpec((CH * Q, 128), rev),
                   _const((8, 1536)), _const((8, 128)), _const((8, D))],
        out_shape=outs, scratch_shapes=[pltpu.VMEM((D, 128), F32), pltpu.VMEM((Q, 1536), F32)],
        compiler_params=_cparams())(dya, y, z, u, xbc, dtr, states, conv_w, dtb, alog, dskip_full, nw)


def _hg_gates(hq, hf, hlb):
    h0, h1 = hlb[0:1, :], hlb[1:2, :]
    mx = jnp.maximum(h0, h1)
    e0, e1 = jnp.exp(h0 - mx), jnp.exp(h1 - mx)
    lb = e0 / (e0 + e1)
    sg = _sigmoid(hf)
    fg = lb + (1.0 - lb) * sg
    tri = (_iota((Q, Q), 1) <= _iota((Q, Q), 0)).astype(BF)
    return hq * _sigmoid(hq), 1.0 - fg, fg, sg, lb, e1 / (e0 + e1), _dot_sel(tri, jnp.log(fg))


def _hg_intra(b, q, k):
    rowblk = jnp.right_shift(_iota((Q, 1), 0), SUB.bit_length() - 1)
    mids = [b[SUB * i + SUB // 2:SUB * i + SUB // 2 + 1, :] for i in range(NSUB)]
    prevs = [mids[0]] + [b[SUB * i - 1:SUB * i, :] for i in range(1, NSUB)]
    mfull = jnp.concatenate([jnp.broadcast_to(r, (SUB, 128)) for r in mids], axis=0)
    rfull = jnp.concatenate([jnp.broadcast_to(r, (SUB, 128)) for r in prevs], axis=0)
    eqd, ek, eqo = jnp.exp(b - mfull), jnp.exp(mfull - b), jnp.exp(b - rfull)
    qd, qo, khat = q * eqd, q * eqo, k * ek
    rtab = jnp.concatenate(prevs, axis=0)
    djs = [jnp.exp(rtab - mids[j]) for j in range(NSUB)]
    zero = jnp.zeros((SUB, 128), F32)
    cols = []
    for j in range(NSUB):
        pieces = []
        for i in range(NSUB):
            rs = slice(SUB * i, SUB * i + SUB)
            pieces.append(zero if i < j else qd[rs] if i == j else qo[rs] * djs[j][i:i + 1, :])
        cols.append(jnp.concatenate(pieces, axis=0))
    qt = jnp.concatenate(cols, axis=1).astype(BF)
    kt = jnp.concatenate([jnp.where(rowblk == j, khat, 0.0) for j in range(NSUB)], axis=1).astype(BF)
    causal = _iota((Q, Q), 1) <= _iota((Q, Q), 0)
    att = jnp.where(causal, _dot_nt(qt, kt), 0.0)
    return att, qt, kt, (eqd, ek, eqo, djs), causal


def _hg_intra_bwd(dqt, dkt, qt, kt, factors):
    eqd, ek, eqo, djs = factors
    dqd, dqo, dkh, db = [], [], [], []
    for i in range(NSUB):
        rs = slice(SUB * i, SUB * i + SUB)
        diag = slice(128 * i, 128 * i + 128)
        dqd.append(dqt[rs, diag])
        dkh.append(dkt[rs, diag])
        dbi = qt[rs, diag].astype(F32) * dqt[rs, diag] - kt[rs, diag].astype(F32) * dkt[rs, diag]
        acc = jnp.zeros((SUB, 128), F32)
        for j in range(i):
            bl = slice(128 * j, 128 * j + 128)
            acc = acc + dqt[rs, bl] * djs[j][i:i + 1, :]
            dbi = dbi + qt[rs, bl].astype(F32) * dqt[rs, bl]
        dqo.append(acc)
        db.append(dbi)
    cat = lambda t: jnp.concatenate(t, axis=0)
    return cat(dqd) * eqd + cat(dqo) * eqo, cat(dkh) * ek, cat(db)


def _hg_att_exact(b, q, k, b_ref, q_ref, att_t_ref):
    b_ref[...] = b
    q_ref[...] = q
    att_t_ref[...] = jnp.zeros((Q, Q), F32)
    rows, lane = _iota((Q, 1), 0), _iota((1, Q), 1)

    def step(i, carry):
        e = jnp.exp(jnp.minimum(b_ref[pl.ds(i, 1), :] - b, 0.0))
        col = jnp.sum(q_ref[pl.ds(i, 1), :] * k * e, axis=1, keepdims=True)
        att_t_ref[...] = jnp.where(lane == i, jnp.where(rows <= i, col, 0.0), att_t_ref[...])
        return carry

    lax.fori_loop(0, Q, step, 0)
    return att_t_ref[...].T


def _hg_att_exact_bwd(da, b, q, k, b_ref, q_ref, da_t_ref, dq_ref, dk_ref):
    b_ref[...] = b
    q_ref[...] = q
    da_t_ref[...] = da.T
    dk_ref[...] = jnp.zeros((Q, 128), F32)
    lane = _iota((1, Q), 1)

    def step(i, carry):
        e = jnp.exp(jnp.minimum(b_ref[pl.ds(i, 1), :] - b, 0.0))
        g = jnp.sum(jnp.where(lane == i, da_t_ref[...], 0.0), axis=1, keepdims=True) * e
        dq_ref[pl.ds(i, 1), :] = jnp.sum(g * k, axis=0, keepdims=True)
        dk_ref[...] += g * q_ref[pl.ds(i, 1), :]
        return carry

    lax.fori_loop(0, Q, step, 0)
    dq, dk = dq_ref[...], dk_ref[...]
    return dq, dk, q * dq - k * dk


def _hg_fwd(hq, hf, hi, hg, hlb, nw, fast):
    L = hq.shape[0]
    nc = L // Q

    def chunk(exact, ck, hq_ref, hf_ref, hi_ref, hg_ref, hlb_ref, nw_ref, ob_ref, o_ref, st_ref, s_ref, *tmp):
        tok = slice(Q * ck, Q * ck + Q)
        qf, kf, _, _, _, _, bcum = _hg_gates(hq_ref[tok, :].astype(F32), hf_ref[tok, :], hlb_ref[...])
        gate = hg_ref[tok, :].astype(F32)
        heads = [slice(128 * h, 128 * h + 128) for h in range(NH_HG)]
        if exact:
            atts = [_hg_att_exact(bcum[:, sl], qf[:, sl], kf[:, sl], *tmp).astype(BF) for sl in heads]
        else:
            atts = [_hg_intra(bcum[:, sl], qf[:, sl], kf[:, sl])[0].astype(BF) for sl in heads]
        olds = [s_ref[sl, :] for sl in heads]
        outs_ = [_dot(att, hi_ref[tok, sl]) + _dot(qf[:, sl] * jnp.exp(bcum[:, sl]), s)
                 for att, sl, s in zip(atts, heads, olds)]
        for sl, s, o in zip(heads, olds, outs_):
            b, k = bcum[:, sl], kf[:, sl]
            st_ref[ck, sl, :] = s
            blast = b[Q - 1:Q, :]
            s_ref[sl, :] = s * jnp.exp(b.T[:, Q - 1:Q]) + _dot_tn(k * jnp.exp(blast - b), hi_ref[tok, sl])
            ob = o.astype(BF)
            o_ref[tok, sl] = ob
            on, _, _ = _rms(ob.astype(F32), nw_ref[...])
            gt = gate[:, sl]
            ob_ref[tok, sl] = (on * gt * _sigmoid(gt)).astype(BF)

    def run(exact, *refs):
        for ck in range(CH):
            chunk(exact, ck, *refs)

    def body(fast_ref, *refs):
        _zero_first(refs[9])
        pl.when(fast_ref[0] == 1)(lambda: run(False, *refs))
        pl.when(fast_ref[0] != 1)(lambda: run(True, *refs))

    rows = pl.BlockSpec((CH * Q, D), lambda i, f: (i, 0))
    outs = [SDS((L, D), BF), SDS((L, D), BF), SDS((nc, D, 128), F32)]
    grid_spec = pltpu.PrefetchScalarGridSpec(
        num_scalar_prefetch=1, grid=(nc // CH,),
        in_specs=[rows] * 4 + [pl.BlockSpec((2, D), lambda i, f: (0, 0)), pl.BlockSpec((1, 128), lambda i, f: (0, 0))],
        out_specs=[rows, rows, pl.BlockSpec((CH, D, 128), lambda i, f: (i, 0, 0))],
        scratch_shapes=[pltpu.VMEM((D, 128), F32), pltpu.VMEM((Q, 128), F32), pltpu.VMEM((Q, 128), F32),
                        pltpu.VMEM((Q, Q), F32)])
    return pl.pallas_call(body, grid_spec=grid_spec, name="hg_fwd", out_shape=outs,
                          compiler_params=_cparams())(fast, hq, hf, hi, hg, hlb, nw)


def _hg_bwd(dob, o, hq, hf, hi, hg, states, hlb, nw, fast, phases=()):
    L = dob.shape[0]
    nc = L // Q

    def chunk(exact, ck, step, dob_ref, o_ref, hq_ref, hf_ref, hi_ref, hg_ref, st_ref, hlb_ref, nw_ref,
              dhq_ref, dhf_ref, dhi_ref, dhg_ref, acc_ref, gs_ref, *tmp):
        tok = slice(Q * ck, Q * ck + Q)
        hqv = hq_ref[tok, :].astype(F32)
        qf, kf, fg, sg, lb, sm1, bcum = _hg_gates(hqv, hf_ref[tok, :], hlb_ref[...])
        gate = hg_ref[tok, :].astype(F32)
        sgg = _sigmoid(gate)
        nwv = nw_ref[...]
        tri_t = (_iota((Q, Q), 1) >= _iota((Q, Q), 0)).astype(BF)
        ones8 = jnp.ones((8, 128), BF)
        heads = [slice(128 * h, 128 * h + 128) for h in range(NH_HG)]
        row_last = _iota((Q, 1), 0) == Q - 1
        dobs, dnws = [], []
        for sl in heads:
            gt, sgt = gate[:, sl], sgg[:, sl]
            _, n, r = _rms(o_ref[tok, sl].astype(F32), nwv)
            dobv = dob_ref[tok, sl].astype(F32)
            dhg_ref[tok, sl] = (dobv * n * nwv * sgt * (1.0 + gt * (1.0 - sgt))).astype(BF)
            do, dw = _rms_bwd(dobv * gt * sgt, n, r, nwv)
            dnws.append(_colsum(dw))
            dobs.append(do.astype(BF))
        causal = _iota((Q, Q), 1) <= _iota((Q, Q), 0)
        if exact:
            intra = [(_hg_att_exact(bcum[:, sl], qf[:, sl], kf[:, sl], *tmp[:3]),) for sl in heads]
        else:
            intra = [_hg_intra(bcum[:, sl], qf[:, sl], kf[:, sl]) for sl in heads]
        states = [(st_ref[ck, sl, :], gs_ref[sl, :]) for sl in heads]
        das = [jnp.where(causal, _dot_nt(dob_h, hi_ref[tok, sl]), 0.0) for dob_h, sl in zip(dobs, heads)]
        dqhats = [_dot_nt(dob_h, s) for dob_h, (s, _) in zip(dobs, states)]
        dkhats = [_dot_nt(hi_ref[tok, sl], gst) for sl, (_, gst) in zip(heads, states)]
        if not exact:
            dqts = [jnp.dot(da.astype(BF), it[2], preferred_element_type=F32) for da, it in zip(das, intra)]
            dkts = [lax.dot_general(da.astype(BF), it[1], (((0,), (0,)), ((), ())), preferred_element_type=F32)
                    for da, it in zip(das, intra)]
        dqs, dks, dgls = [], [], []
        for h, sl in enumerate(heads):
            b, q, k = bcum[:, sl], qf[:, sl], kf[:, sl]
            att = intra[h][0]
            s, gst = states[h]
            dob_h, dqhat, dkhat = dobs[h], dqhats[h], dkhats[h]
            eb = jnp.exp(b)
            blast = b[Q - 1:Q, :]
            ekl = jnp.exp(blast - b)
            qhat, khat = q * eb, k * ekl
            dhi_ref[tok, sl] = (_dot_tn(att, dob_h) + _dot(khat, gst)).astype(BF)
            if exact:
                dq_i, dk_i, db = _hg_att_exact_bwd(das[h], b, q, k, *tmp)
            else:
                dq_i, dk_i, db = _hg_intra_bwd(dqts[h], dkts[h], *intra[h][1:4])
            dqs.append(dq_i + dqhat * eb)
            dks.append(dk_i + dkhat * ekl)
            qhat_r, khat_r = qhat.astype(BF).astype(F32), khat.astype(BF).astype(F32)
            decay_row = sum(_dot_nt(ones8, part) for part in _split(gst * s, 2))[0:1, :]
            dblast = _colsum(dkhat * khat_r) + decay_row * jnp.exp(blast)
            dgls.append(db + qhat_r * dqhat - khat_r * dkhat + jnp.where(row_last, dblast, 0.0))
            gs_ref[sl, :] = _dot_tn(qhat, dob_h) + gst * jnp.exp(b.T[:, Q - 1:Q])
        dq, dk, db = (jnp.concatenate(t, axis=1) for t in (dqs, dks, dgls))
        dgl = _dot_sel(tri_t, db, 2)
        sgq = _sigmoid(hqv)
        dhq_ref[tok, :] = (dq * sgq * (1.0 + hqv * (1.0 - sgq))).astype(BF)
        dfg = dgl / fg - dk
        dhf_ref[tok, :] = (dfg * (1.0 - lb) * sg * (1.0 - sg)).astype(BF)
        acc_ref[0:1, :] += _colsum(dfg * (1.0 - sg))
        acc_ref[1:2, :] += jnp.concatenate(dnws, axis=1)

        @pl.when(jnp.logical_and(step == nc // CH - 1, ck == 0))
        def _():
            dlb = acc_ref[0:1, :] * lb * sm1
            acc_ref[2:3, :] = dlb
            acc_ref[3:4, :] = -dlb
            tot = acc_ref[1:2, 0:128]
            for h in range(1, NH_HG):
                tot = tot + acc_ref[1:2, 128 * h:128 * h + 128]
            acc_ref[4:5, 0:128] = tot

    def run(exact, step, *refs):
        for ck in reversed(range(CH)):
            chunk(exact, ck, step, *refs)

    def body(fast_ref, *refs):
        step = pl.program_id(0)
        _zero_first(refs[13], refs[14])
        pl.when(fast_ref[0] == 1)(lambda: run(False, step, *refs))
        pl.when(fast_ref[0] != 1)(lambda: run(True, step, *refs))

    rev = pl.BlockSpec((CH * Q, D), lambda i, f: (nc // CH - 1 - i, 0))
    outs = [SDS((L, D), BF)] * 4 + [SDS((8, D), F32)]
    return _call(
        body, (fast, dob, o, hq, hf, hi, hg, states, hlb, nw), name="hg_bwd", grid=(nc // CH,), prefetch=1,
        in_specs=[rev] * 6 + [pl.BlockSpec((CH, D, 128), lambda i, f: (nc // CH - 1 - i, 0, 0)),
                              pl.BlockSpec((2, D), lambda i, f: (0, 0)), pl.BlockSpec((1, 128), lambda i, f: (0, 0))],
        out_specs=[rev] * 4 + [pl.BlockSpec((8, D), lambda i, f: (0, 0))], out_shape=outs,
        scratch_shapes=[pltpu.VMEM((D, 128), F32), pltpu.VMEM((Q, 128), F32), pltpu.VMEM((Q, 128), F32),
                        pltpu.VMEM((Q, Q), F32), pltpu.VMEM((Q, 128), F32), pltpu.VMEM((Q, 128), F32)], phases=phases)


def _place():
    return lax.axis_index("x"), lax.axis_index("y"), lax.axis_index("c")


def _phase_io(phase):
    kind, arrays, halves = phase
    n = len(arrays)
    dma = pltpu.SemaphoreType.DMA
    if kind == "gather":
        outs = [SDS((8,) + a.shape if hc is None else (4,) + a.shape, a.dtype) for a, hc in zip(arrays, halves)]
        return outs, [dma((7 * n,)), dma((7 * n,)), dma((n,))], {}
    if kind == "sibling":
        return [SDS((4, g.shape[1], hc), g.dtype) for g, hc in zip(arrays, halves)], [dma((n,)), dma((n,))], {}
    if kind == "chips":
        return [SDS((3,) + p.shape[1:], p.dtype) for p in arrays], [dma((3 * n,)), dma((3 * n,))], {}
    assert kind == "swap"
    return [SDS(b.shape, b.dtype) for b in arrays], [dma((n,)), dma((n,))], {a: a for a in range(n)}


def _gather_events(ins, outs, sems, halves):
    send_sems, recv_sems, local_sems = sems
    n = len(ins)

    def parts(a):
        x, y, c = _place()
        hc = halves[a]
        me, sibling = (x, y, c), (x, y, 1 - c)
        chips = [(1 - x, y), (x, 1 - y), (1 - x, 1 - y)]

        def slot(p):
            if hc is None:
                return outs[a].at[4 * p[0] + 2 * p[1] + p[2]]
            return outs[a].at[2 * p[0] + p[1], :, pl.ds(p[2] * hc, hc)]

        own = ins[a] if hc is None else ins[a].at[:, pl.ds(c * hc, hc)]

        def copy(k, piece, to, src=None):
            return pltpu.make_async_remote_copy(
                src_ref=slot(piece) if src is None else src, dst_ref=slot(piece),
                send_sem=send_sems.at[7 * a + k], recv_sem=recv_sems.at[7 * a + k], device_id=to, device_id_type=MESH)

        return dict(
            mine=lambda: pltpu.make_async_copy(own, slot(me), local_sems.at[a]),
            starts=lambda: [copy(0, me, sibling, src=own)] + [copy(1 + j, me, (*chip, c), src=own)
                                                               for j, chip in enumerate(chips)],
            arrive=lambda: [copy(1 + j, (*chip, c), me) for j, chip in enumerate(chips)],
            passed=lambda: [copy(4 + j, (*chip, c), sibling) for j, chip in enumerate(chips)],
            from_sibling=lambda: [copy(0, sibling, me)] + [copy(4 + j, (*chip, 1 - c), me)
                                                            for j, chip in enumerate(chips)])

    def first():
        for a in range(n):
            p = parts(a)
            p["mine"]().start()
            for cp in p["starts"]():
                cp.start()

    def mid():
        for a in range(n):
            p = parts(a)
            for cp_in, cp_out in zip(p["arrive"](), p["passed"]()):
                cp_in.wait_recv()
                cp_out.start()

    def last():
        for a in range(n):
            p = parts(a)
            for cp in p["from_sibling"]():
                cp.wait_recv()
            for cp in p["starts"]() + p["passed"]():
                cp.wait_send()
            p["mine"]().wait()

    return dict(first=first, mid=mid, last=last)


def _exchange_events(kind, ins, outs, sems, halves):
    send_sems, recv_sems = sems
    n = len(outs)

    def copies():
        x, y, c = _place()
        if kind == "sibling":
            return [pltpu.make_async_remote_copy(
                src_ref=ins[a].at[:, :, pl.ds((1 - c) * halves[a], halves[a])], dst_ref=outs[a],
                send_sem=send_sems.at[a], recv_sem=recv_sems.at[a], device_id=(x, y, 1 - c), device_id_type=MESH)
                for a in range(n)]
        chips = [(1 - x, y), (x, 1 - y), (1 - x, 1 - y)]
        return [pltpu.make_async_remote_copy(
            src_ref=ins[a].at[2 * px + py], dst_ref=outs[a].at[k], send_sem=send_sems.at[3 * a + k],
            recv_sem=recv_sems.at[3 * a + k], device_id=(px, py, c), device_id_type=MESH)
            for a in range(n) for k, (px, py) in enumerate(chips)]

    def first():
        for cp in copies():
            cp.start()

    def last():
        for cp in copies():
            cp.wait()

    return dict(first=first, last=last)


def _swap_events(outs, sems, halves):
    send_sems, recv_sems = sems
    n = len(outs)

    def copy(a, landing):
        x, y, c = _place()
        cols = lambda which: outs[a].at[:, pl.ds(which * halves[a], halves[a])]
        return pltpu.make_async_remote_copy(
            src_ref=cols(c), dst_ref=cols(1 - c) if landing else cols(c), send_sem=send_sems.at[a],
            recv_sem=recv_sems.at[a], device_id=(x, y, 1 - c), device_id_type=MESH)

    def first():
        for a in range(n):
            copy(a, False).start()

    def last():
        for a in range(n):
            copy(a, True).wait_recv()
        for a in range(n):
            copy(a, False).wait_send()

    return dict(first=first, last=last)


def _phase_events(phase, ins, outs, sems):
    kind, _, halves = phase
    if kind == "gather":
        return _gather_events(ins, outs, sems, halves)
    if kind == "swap":
        return _swap_events(outs, sems, halves)
    return _exchange_events(kind, ins, outs, sems, halves)


def _split_refs(refs, counts):
    out, at = [], 0
    for c in counts:
        out.append(list(refs[at:at + c]))
        at += c
    return out


def _comm_plumbing(phases, first_in, first_out):
    ios = [_phase_io(p) for p in phases]
    arrays = [a for p in phases for a in p[1]]
    out_shape = [o for io in ios for o in io[0]]
    sem_shapes = [s for io in ios for s in io[1]]
    aliases, ai, ao = {}, first_in, first_out
    for p, io in zip(phases, ios):
        aliases.update({ai + k: ao + v for k, v in io[2].items()})
        ai, ao = ai + len(p[1]), ao + len(io[0])

    def events(cins, couts, sems):
        evs = [_phase_events(p, i, o, s) for p, i, o, s in zip(
            phases, _split_refs(cins, [len(p[1]) for p in phases]), _split_refs(couts, [len(io[0]) for io in ios]),
            _split_refs(sems, [len(io[1]) for io in ios]))]

        def run(key):
            for ev in evs:
                if key in ev:
                    ev[key]()

        return {key: (lambda key=key: run(key)) for key in ("first", "mid", "last")}

    def regroup(flat):
        return _split_refs(flat, [len(io[0]) for io in ios])

    return arrays, out_shape, sem_shapes, aliases, events, regroup


def _run_phases(phases, name):
    arrays, out_shape, sem_shapes, aliases, events, regroup = _comm_plumbing(phases, 0, 0)

    def body(*refs):
        cins, couts, sems = _split_refs(refs, [len(arrays), len(out_shape), len(sem_shapes)])
        ev = events(cins, couts, sems)
        for key in ("first", "mid", "last"):
            ev[key]()

    outs = pl.pallas_call(
        body, name=name, in_specs=[ANY] * len(arrays), out_specs=[ANY] * len(out_shape), out_shape=out_shape,
        scratch_shapes=sem_shapes, input_output_aliases=aliases)(*arrays)
    return regroup(outs)


def _call(body, args, *, name, grid, in_specs, out_specs, out_shape, scratch_shapes=(), prefetch=0, phases=(),
          mid_step=None):
    steps = grid[0]
    arrays, c_shape, sem_shapes, aliases, events, regroup = _comm_plumbing(
        phases, prefetch + len(in_specs), len(out_specs))
    counts = [prefetch, len(in_specs), len(arrays), len(out_specs), len(c_shape), len(scratch_shapes), len(sem_shapes)]

    def wrapped(*refs):
        pre, ins, cins, outs, couts, scratch, sems = _split_refs(refs, counts)
        if not phases:
            return body(*pre, *ins, *outs, *scratch)
        step = pl.program_id(0)
        ev = events(cins, couts, sems)
        pl.when(step == 0)(ev["first"])
        body(*pre, *ins, *outs, *scratch)
        pl.when(step == (steps // 2 if mid_step is None else mid_step))(ev["mid"])
        pl.when(step == steps - 1)(ev["last"])

    grid_spec = pltpu.PrefetchScalarGridSpec(
        num_scalar_prefetch=prefetch, grid=grid, in_specs=list(in_specs) + [ANY] * len(arrays),
        out_specs=list(out_specs) + [ANY] * len(c_shape), scratch_shapes=list(scratch_shapes) + sem_shapes)
    outs = pl.pallas_call(
        wrapped, grid_spec=grid_spec, name=name, out_shape=list(out_shape) + c_shape, input_output_aliases=aliases,
        compiler_params=_cparams())(*args, *arrays)
    return list(outs[:len(out_specs)]), regroup(outs[len(out_specs):])


def _tile(rows, cols, nbuf):
    budget = (VMEM_LIMIT // 3) // (2 * nbuf * 4)
    if rows % 8 == 0:
        cands = [t for t in range(8, rows + 1, 8) if rows % t == 0 and t * cols <= budget]
        pref = [t for t in cands if t % 16 == 0]
        return (max(pref) if pref else max(cands) if cands else 8), cols
    cands = [t for t in range(128, cols + 1, 128) if cols % t == 0 and rows * t <= budget]
    return rows, (max(cands) if cands else 128)


def _chip_sum(g, from_sib, place, name):
    _, rows, hc = from_sib.shape
    tr, tc = _tile(rows, hc, 4)
    ni, nj = rows // tr, hc // tc

    def body(p_ref, g_ref, s_ref, hb_ref, own_ref):
        s = g_ref[...] + s_ref[...]
        hb_ref[...] = s.astype(BF)

        @pl.when(pl.program_id(2) == p_ref[1])
        def _():
            own_ref[...] = s

    grid_spec = pltpu.PrefetchScalarGridSpec(
        num_scalar_prefetch=1, grid=(ni, nj, 4),
        in_specs=[pl.BlockSpec((None, tr, tc), lambda i, j, k, p: (k, i, p[0] * nj + j)),
                  pl.BlockSpec((None, tr, tc), lambda i, j, k, p: (k, i, j))],
        out_specs=[pl.BlockSpec((None, tr, tc), lambda i, j, k, p: (k, i, j)),
                   pl.BlockSpec((tr, tc), lambda i, j, k, p: (i, j))])
    return pl.pallas_call(
        body, grid_spec=grid_spec, name=name, out_shape=[SDS((4, rows, hc), BF), SDS((rows, hc), F32)],
        compiler_params=pltpu.CompilerParams(dimension_semantics=("arbitrary",) * 3,
                                             vmem_limit_bytes=VMEM_LIMIT))(place, g, from_sib)


def _total(own, parts, place, name):
    rows, hc = own.shape
    tr, tc = _tile(rows, hc, 5)
    ni, nj = rows // tr, hc // tc

    def body(p_ref, own_ref, parts_ref, o_ref):
        s = own_ref[...]
        for k in range(3):
            s = s + parts_ref[k].astype(F32)
        o_ref[...] = s

    grid_spec = pltpu.PrefetchScalarGridSpec(
        num_scalar_prefetch=1, grid=(ni, nj),
        in_specs=[pl.BlockSpec((tr, tc), lambda i, j, p: (i, j)),
                  pl.BlockSpec((3, tr, tc), lambda i, j, p: (0, i, j))],
        out_specs=pl.BlockSpec((tr, tc), lambda i, j, p: (i, p[0] * nj + j)))
    return pl.pallas_call(
        body, grid_spec=grid_spec, name=name, out_shape=SDS((rows, 2 * hc), F32),
        compiler_params=pltpu.CompilerParams(dimension_semantics=("arbitrary",) * 2,
                                             vmem_limit_bytes=VMEM_LIMIT))(place, own, parts)


def _sum8(parts, name):
    R = parts.shape[1]

    def body(p_ref, o_ref):
        s = p_ref[0]
        for k in range(1, 8):
            s = s + p_ref[k]
        o_ref[...] = s

    return pl.pallas_call(
        body, grid=(1,), name=name, in_specs=[_const((8, R, 128))], out_specs=_const((R, 128)),
        out_shape=SDS((R, 128), F32), compiler_params=_cparams())(parts)


def _adamw(w, g, m, v, name):
    _, R, C = w.shape
    tr, tc = _tile(R, C, 7)
    c1 = 1.0 / (1.0 - ADAM_B1 ** ADAM_STEP)
    c2 = 1.0 / (1.0 - ADAM_B2 ** ADAM_STEP)

    def body(w_ref, g_ref, m_ref, v_ref, d_ref, nm_ref, nv_ref):
        gv = g_ref[...]
        nm = ADAM_B1 * m_ref[...] + (1.0 - ADAM_B1) * gv
        nv = ADAM_B2 * v_ref[...] + (1.0 - ADAM_B2) * gv * gv
        nm_ref[...] = nm
        nv_ref[...] = nv
        d_ref[...] = -ADAM_LR * ((nm * c1) / (jnp.sqrt(nv * c2) + ADAM_EPS) + ADAM_WD * w_ref[...])

    blk3 = pl.BlockSpec((None, tr, tc), lambda i, j: (0, i, j))
    return pl.pallas_call(
        body, grid=(R // tr, C // tc), name=name,
        in_specs=[blk3, pl.BlockSpec((tr, tc), lambda i, j: (i, j)), blk3, blk3], out_specs=[blk3] * 3,
        out_shape=[SDS((1, R, C), F32)] * 3,
        compiler_params=pltpu.CompilerParams(dimension_semantics=("arbitrary",) * 2,
                                             vmem_limit_bytes=VMEM_LIMIT))(w, g, m, v)


def _pack_small(parts):
    rows = []
    for p in parts:
        p = p.reshape(-1)
        rows.append(jnp.pad(p, (0, (-p.shape[0]) % 128)).reshape(-1, 128))
    out = jnp.concatenate(rows, axis=0)
    return jnp.pad(out, ((0, (-out.shape[0]) % 8), (0, 0)))


def _unpack_small(packed, shapes):
    out, row = [], 0
    for shp in shapes:
        n = 1
        for s in shp:
            n *= s
        nr = -(-n // 128)
        out.append(packed[row:row + nr].reshape(-1)[:n].reshape(shp))
        row += nr
    return out


def _pad_lanes(v, n=128):
    return jnp.pad(v, ((0, 0), (0, n - v.shape[1])))


GROUP_FFN = ("ffn_w_gate", "ffn_w_up", "ffn_w_down")
GROUP_ATTN = ("w_out", "xa_wq", "xa_wkv", "xa_wo")


def kernel(x, mem, norm_mix_w, w_in, conv_w, conv_b, dt_bias, a_log, d_skip, ssd_norm_w, hg_lower_bounds, hg_norm_w, w_out, norm_xa_w, norm_mem_w, xa_wq, xa_wkv, xa_wo, norm_ffn_w, ffn_w_gate, ffn_w_up, ffn_w_down, norm_final_w, loss_target, m_norm_mix_w, m_w_in, m_conv_w, m_conv_b, m_dt_bias, m_a_log, m_d_skip, m_ssd_norm_w, m_hg_lower_bounds, m_hg_norm_w, m_w_out, m_norm_xa_w, m_norm_mem_w, m_xa_wq, m_xa_wkv, m_xa_wo, m_norm_ffn_w, m_ffn_w_gate, m_ffn_w_up, m_ffn_w_down, m_norm_final_w, v_norm_mix_w, v_w_in, v_conv_w, v_conv_b, v_dt_bias, v_a_log, v_d_skip, v_ssd_norm_w, v_hg_lower_bounds, v_hg_norm_w, v_w_out, v_norm_xa_w, v_norm_mem_w, v_xa_wq, v_xa_wkv, v_xa_wo, v_norm_ffn_w, v_ffn_w_gate, v_ffn_w_up, v_ffn_w_down, v_norm_final_w):
    w = dict(norm_mix_w=norm_mix_w, w_in=w_in, conv_w=conv_w, conv_b=conv_b, dt_bias=dt_bias, a_log=a_log, d_skip=d_skip,
             ssd_norm_w=ssd_norm_w, hg_lower_bounds=hg_lower_bounds, hg_norm_w=hg_norm_w, w_out=w_out,
             norm_xa_w=norm_xa_w, norm_mem_w=norm_mem_w, xa_wq=xa_wq, xa_wkv=xa_wkv, xa_wo=xa_wo, norm_ffn_w=norm_ffn_w,
             ffn_w_gate=ffn_w_gate, ffn_w_up=ffn_w_up, ffn_w_down=ffn_w_down, norm_final_w=norm_final_w)
    m = dict(norm_mix_w=m_norm_mix_w, w_in=m_w_in, conv_w=m_conv_w, conv_b=m_conv_b, dt_bias=m_dt_bias, a_log=m_a_log,
             d_skip=m_d_skip, ssd_norm_w=m_ssd_norm_w, hg_lower_bounds=m_hg_lower_bounds, hg_norm_w=m_hg_norm_w,
             w_out=m_w_out, norm_xa_w=m_norm_xa_w, norm_mem_w=m_norm_mem_w, xa_wq=m_xa_wq, xa_wkv=m_xa_wkv,
             xa_wo=m_xa_wo, norm_ffn_w=m_norm_ffn_w, ffn_w_gate=m_ffn_w_gate, ffn_w_up=m_ffn_w_up,
             ffn_w_down=m_ffn_w_down, norm_final_w=m_norm_final_w)
    v = dict(norm_mix_w=v_norm_mix_w, w_in=v_w_in, conv_w=v_conv_w, conv_b=v_conv_b, dt_bias=v_dt_bias, a_log=v_a_log,
             d_skip=v_d_skip, ssd_norm_w=v_ssd_norm_w, hg_lower_bounds=v_hg_lower_bounds, hg_norm_w=v_hg_norm_w,
             w_out=v_w_out, norm_xa_w=v_norm_xa_w, norm_mem_w=v_norm_mem_w, xa_wq=v_xa_wq, xa_wkv=v_xa_wkv,
             xa_wo=v_xa_wo, norm_ffn_w=v_norm_ffn_w, ffn_w_gate=v_ffn_w_gate, ffn_w_up=v_ffn_w_up,
             ffn_w_down=v_ffn_w_down, norm_final_w=v_norm_final_w)
    xi, yi, ci = _place()
    chip = 2 * xi + yi
    place = jnp.stack([ci, chip]).astype(jnp.int32)

    def shard(t, name):
        return jnp.swapaxes(t[name], 1, 2) if name in TRANSPOSED else t[name]

    wsh = {name: shard(w, name) for name in BIG}
    half = {name: wsh[name].shape[2] // 2 for name in BIG}
    payload = {name: wsh[name][0].astype(BF) for name in BIG}
    ws = {name: w[name] for name in SMALL}
    xs, mems, tgt = x[0], mem[0], loss_target[0]

    def chip_sums(names, grads, from_sib):
        return [_chip_sum(grads[n], s, place, "grads_chip_sum_" + n) for n, s in zip(names, from_sib)]

    def totals(names, sums, others):
        return [_total(own, o, place, "grads_total_" + n) for n, (_, own), o in zip(names, sums, others)]

    ((w_in4, conv_all),) = _run_phases([("gather", [payload["w_in"], conv_w[0]], [half["w_in"], None])], "gather_w_in")
    w_in_t = w_in4.reshape(N_IN, D)
    ws["conv_w"] = conv_all[0::2].transpose(1, 0, 2).reshape(1, 4, 1536)
    rest = [n for n in BIG if n != "w_in"]
    (h0, z, xbc, hq, hf, hi, hg, dtr), (gathered,) = _in_proj(
        xs, ws["norm_mix_w"], w_in_t, phases=[("gather", [payload[n] for n in rest], [half[n] for n in rest])])
    wg = dict(zip(rest, gathered))
    wg_t, wu_t = wg["ffn_w_gate"].reshape(FFN, D), wg["ffn_w_up"].reshape(FFN, D)
    wd = wg["ffn_w_down"].reshape(FFN, D)
    w_out_f = wg["w_out"].reshape(2 * D, D)
    wq, wo = wg["xa_wq"].reshape(D, D), wg["xa_wo"].reshape(D, D)
    dtb, alog = _pad_lanes(ws["dt_bias"]), _pad_lanes(ws["a_log"])
    dskip_full = jnp.repeat(ws["d_skip"], SSD_P, axis=1)
    cw, conv_bias = ws["conv_w"][0], ws["conv_b"]
    hlb = ws["hg_lower_bounds"]
    hg_fast = (jnp.min(jax.nn.softmax(hlb, axis=0)[0]) >= HG_LB_FLOOR).astype(jnp.int32).reshape(1)

    ya, yssd, u, st_ssd = _ssd_fwd(xbc, dtr, z, cw, conv_bias, dtb, alog, dskip_full, ws["ssd_norm_w"])
    ob, ohg, st_hg = _hg_fwd(hq, hf, hi, hg, hlb, ws["hg_norm_w"], hg_fast)
    kmem, vmem = _mem_kv(mems, ws["norm_mem_w"], wg["xa_wkv"])
    x1, x2, hxa, q, ox = _attn_fwd(xs, ya, ob, w_out_f, ws["norm_xa_w"], wq, kmem, vmem, wo)
    nfin = ws["norm_final_w"].reshape(1, D)
    dx2, hffn, act, dx3, dg, du, acc_f = _ffn_loss(x2, tgt, ws["norm_ffn_w"], nfin, wg_t, wu_t, wd)

    gb = {"ffn_w_gate": _matmul_tn(dg, hffn, "gw_gate").reshape(4, FFN // 4, D),
          "ffn_w_up": _matmul_tn(du, hffn, "gw_up").reshape(4, FFN // 4, D),
          "ffn_w_down": _matmul_tn(act, dx3, "gw_down").reshape(4, FFN // 4, D)}
    (dx1, dya, dob, dq, dk, dv, acc_a), (sib_ffn,) = _attn_bwd(
        dx2, x1, q, kmem, vmem, ws["norm_xa_w"], wq, wo, w_out_f,
        phases=[("sibling", [gb[n] for n in GROUP_FFN], [half[n] for n in GROUP_FFN])])
    sums_ffn = chip_sums(GROUP_FFN, gb, sib_ffn)
    g_nmem, gb["xa_wkv"] = _mem_kv_bwd(mems, ws["norm_mem_w"], wg["xa_wkv"], dk, dv)
    gb["w_out"] = _matmul_tn_pair(ya, ob, dx1, "gw_out").reshape(4, D // 2, D)
    gb["xa_wq"] = _matmul_tn(hxa, dq, "gw_q").reshape(4, D // 4, D)
    gb["xa_wo"] = _matmul_tn(ox, dx2, "gw_o").reshape(4, D // 4, D)
    (dhq, dhf, dhi, dhg, acc_h), (others_ffn, sib_attn) = _hg_bwd(
        dob, ohg, hq, hf, hi, hg, st_hg, hlb, ws["hg_norm_w"], hg_fast,
        phases=[("chips", [hb for hb, _ in sums_ffn], None),
                ("sibling", [gb[n] for n in GROUP_ATTN], [half[n] for n in GROUP_ATTN])])
    red_ffn = totals(GROUP_FFN, sums_ffn, others_ffn)
    sums_attn = chip_sums(GROUP_ATTN, gb, sib_attn)
    dz, dxbc, ddt, gconv, ghead, glane = _ssd_bwd(dya, yssd, z, u, xbc, dtr, st_ssd, cw, dtb, alog, dskip_full,
                                                  ws["ssd_norm_w"])
    gx, acc_i = _in_proj_bwd(xs, dx1, dz, dxbc, dhq, dhf, dhi, dhg, ddt, ws["norm_mix_w"], w_in_t)
    (gw_in_t,), (g_ffn, others_attn) = _gw_in(
        h0, dz, dxbc, ddt, dhq, dhf, dhi, dhg,
        phases=[("swap", red_ffn, [half[n] for n in GROUP_FFN]), ("chips", [hb for hb, _ in sums_attn], None)])
    red_attn = totals(GROUP_ATTN, sums_attn, others_attn)
    gb["w_in"] = gw_in_t.reshape(4, N_IN // 4, D)
    g_attn, (sib_in,) = _run_phases([("swap", red_attn, [half[n] for n in GROUP_ATTN]),
                                     ("sibling", [gb["w_in"]], [half["w_in"]])], "grads_w_in_to_sibling")
    sums_in = chip_sums(("w_in",), gb, [sib_in])
    ((others_in,),) = _run_phases([("chips", [sums_in[0][0]], None)], "grads_w_in_to_chips")
    red_in = totals(("w_in",), sums_in, [others_in])

    gs = {
        "norm_mix_w": acc_i[0:1], "conv_w": gconv[0:4][None], "conv_b": gconv[4:5],
        "dt_bias": ghead[0:1, :NH_SSD], "a_log": ghead[2:3, :NH_SSD], "d_skip": ghead[3:4, :NH_SSD],
        "ssd_norm_w": glane[1:2], "hg_lower_bounds": acc_h[2:4], "hg_norm_w": acc_h[4:5, :128],
        "norm_xa_w": acc_a[0:1], "norm_mem_w": g_nmem, "norm_ffn_w": acc_f[2:3], "norm_final_w": acc_f[1],
    }
    loss = (0.5 / D) * jnp.sum(acc_f[0])
    small_parts = [gs[name] for name in SMALL] + [loss.reshape(1)]
    small_shapes = [gs[name].shape for name in SMALL] + [(1,)]
    (g_in,), (packed,) = _run_phases([("swap", red_in, [half["w_in"]]),
                                      ("gather", [_pack_small(small_parts)], [None])], "grads_finish")
    g_big = dict(zip(GROUP_FFN + GROUP_ATTN + ("w_in",), g_ffn + g_attn + [g_in]))
    small = _unpack_small(_sum8(packed, "small_total"), small_shapes)
    g_small = dict(zip(SMALL, small[:-1]))
    loss_all = small[-1][0]
    g_small["conv_w"] = lax.dynamic_slice_in_dim(g_small["conv_w"], chip * 384, 384, 2)

    grads, delta, new_m, new_v = {}, {}, {}, {}
    for name in BIG:
        outs = (g_big[name][None],) + tuple(_adamw(wsh[name], g_big[name], shard(m, name), shard(v, name),
                                                   "adamw_" + name))
        if name in TRANSPOSED:
            outs = tuple(jnp.swapaxes(o, 1, 2) for o in outs)
        grads[name], delta[name], new_m[name], new_v[name] = outs
    shapes = [w[name].shape for name in SMALL]
    packs = [_pack_small([t[name] for name in SMALL]) for t in (w, g_small, m, v)]
    outs = _adamw(packs[0][None], packs[1], packs[2][None], packs[3][None], "adamw_small")
    for name, g_, d_, nm_, nv_ in zip(SMALL, [g_small[n] for n in SMALL], *[_unpack_small(o[0], shapes) for o in outs]):
        grads[name] = g_.reshape(w[name].shape)
        delta[name], new_m[name], new_v[name] = d_, nm_, nv_

    return (loss_all, gx[None], *[grads[n] for n in WEIGHTS], *[delta[n] for n in WEIGHTS],
            *[new_m[n] for n in WEIGHTS], *[new_v[n] for n in WEIGHTS])
```

```python
import jax
import jax.numpy as jnp
from jax import lax
from jax.experimental import pallas as pl
from jax.experimental.pallas import tpu as pltpu

F32 = jnp.float32
BF = jnp.bfloat16
MESH = pl.DeviceIdType.MESH
SDS = jax.ShapeDtypeStruct
ANY = pl.BlockSpec(memory_space=pl.ANY)

D = 1024
EPS = 1e-6
NH_SSD = 16
SSD_P = 64
NH_HG = 8
Q = 128
CH = 2
SUB = 32
NSUB = Q // SUB
HG_LB_FLOOR = 1e-2
XA_HEADS = 4
XA_HD = 256
MEM_LEN = 256
FFN = 2816
TL = 512
TL_FFN = 256
VMEM_LIMIT = 56 << 20
MATMUL_VMEM = 40 << 20

N_IN = 6672
Z0, XBC0, DT0, HQ0, HF0, HI0, HG0 = 0, 1024, 2560, 2576, 3600, 4624, 5648

ADAM_LR, ADAM_B1, ADAM_B2, ADAM_EPS, ADAM_WD, ADAM_STEP = 0.001, 0.9, 0.999, 1e-08, 0.01, 10

BIG = ("w_in", "w_out", "xa_wq", "xa_wkv", "xa_wo", "ffn_w_gate", "ffn_w_up", "ffn_w_down")
TRANSPOSED = ("w_in", "ffn_w_gate", "ffn_w_up")
SMALL = ("norm_mix_w", "conv_w", "conv_b", "dt_bias", "a_log", "d_skip", "ssd_norm_w", "hg_lower_bounds",
         "hg_norm_w", "norm_xa_w", "norm_mem_w", "norm_ffn_w", "norm_final_w")
WEIGHTS = ("norm_mix_w", "w_in", "conv_w", "conv_b", "dt_bias", "a_log", "d_skip", "ssd_norm_w", "hg_lower_bounds",
           "hg_norm_w", "w_out", "norm_xa_w", "norm_mem_w", "xa_wq", "xa_wkv", "xa_wo", "norm_ffn_w", "ffn_w_gate",
           "ffn_w_up", "ffn_w_down", "norm_final_w")


def _cparams():
    return pltpu.CompilerParams(dimension_semantics=("arbitrary",), vmem_limit_bytes=VMEM_LIMIT)


def _const(shape):
    return pl.BlockSpec(shape, lambda i: (0,) * len(shape))


def _resident(shape):
    return pl.BlockSpec(shape, lambda i: (0,) * len(shape), pipeline_mode=pl.Buffered(1))


def _rows(tl, n):
    return pl.BlockSpec((tl, n), lambda i: (i, 0))


def _dot(a, b):
    return jnp.dot(a.astype(BF), b.astype(BF), preferred_element_type=F32)


def _dot_nt(a, b):
    return lax.dot_general(a.astype(BF), b.astype(BF), (((1,), (1,)), ((), ())), preferred_element_type=F32)


def _dot_tn(a, b):
    return lax.dot_general(a.astype(BF), b.astype(BF), (((0,), (0,)), ((), ())), preferred_element_type=F32)


def _split(v, passes):
    parts, rest = [], v
    for p in range(passes):
        hi = rest.astype(BF)
        parts.append(hi)
        if p + 1 < passes:
            rest = rest - hi.astype(F32)
    return parts


def _sel_dot(a, sel, passes=3):
    sb = sel.astype(BF)
    out = None
    for part in _split(a, passes):
        t = jnp.dot(part, sb, preferred_element_type=F32)
        out = t if out is None else out + t
    return out


def _dot_sel(sel, b, passes=3):
    sb = sel.astype(BF)
    out = None
    for part in _split(b, passes):
        t = jnp.dot(sb, part, preferred_element_type=F32)
        out = t if out is None else out + t
    return out


def _iota(shape, dim):
    return lax.broadcasted_iota(jnp.int32, shape, dim)


def _sigmoid(v):
    return 0.5 * jnp.tanh(0.5 * v) + 0.5


def _rms(v, w):
    r = lax.rsqrt(jnp.mean(v * v, axis=-1, keepdims=True) + EPS)
    n = v * r
    return n * w, n, r


def _rms_bwd(dy, n, r, w):
    dn = dy * w
    return r * (dn - n * jnp.mean(dn * n, axis=-1, keepdims=True)), dy * n


def _colsum(v):
    return jnp.sum(v, axis=0, keepdims=True)


def _zero_first(*refs):
    @pl.when(pl.program_id(0) == 0)
    def _():
        for r in refs:
            r[...] = jnp.zeros_like(r)


def _in_proj(x, nw, wt, phases=()):
    L = x.shape[0]
    tl = min(TL, L)

    def body(x_ref, nw_ref, w_ref, h0_ref, z_ref, xbc_ref, hq_ref, hf_ref, hi_ref, hg_ref, dt_ref):
        h, _, _ = _rms(x_ref[...], nw_ref[...])
        hb = h.astype(BF)
        h0_ref[...] = hb

        def proj(a, b):
            return _dot_nt(hb, w_ref[a:b, :])

        z_ref[...] = proj(Z0, XBC0).astype(BF)
        xbc_ref[...] = proj(XBC0, DT0).astype(BF)
        dt_ref[...] = proj(DT0, DT0 + 128)
        hq_ref[...] = proj(HQ0, HF0).astype(BF)
        hf_ref[...] = proj(HF0, HI0)
        hi_ref[...] = proj(HI0, HG0).astype(BF)
        hg_ref[...] = proj(HG0, N_IN).astype(BF)

    outs = [SDS((L, D), BF), SDS((L, D), BF), SDS((L, 1536), BF), SDS((L, D), BF), SDS((L, D), F32),
            SDS((L, D), BF), SDS((L, D), BF), SDS((L, 128), F32)]
    steps = L // tl
    return _call(body, (x, nw, wt), name="in_proj", grid=(steps,),
                 in_specs=[_rows(tl, D), _const((1, D)), _resident((N_IN, D))],
                 out_specs=[_rows(tl, o.shape[1]) for o in outs], out_shape=outs, phases=phases,
                 mid_step=(3 * steps) // 4)


def _mem_kv(mem, nw, wkv4):
    def body(m_ref, nw_ref, w_ref, k_ref, v_ref):
        m, _, _ = _rms(m_ref[...], nw_ref[...])
        mb = m.astype(BF)
        for i in range(2):
            sl = slice(512 * i, 512 * i + 512)
            k_ref[:, sl] = jnp.dot(mb, w_ref[i], preferred_element_type=F32).astype(BF)
            v_ref[:, sl] = jnp.dot(mb, w_ref[2 + i], preferred_element_type=F32).astype(BF)

    outs = [SDS((MEM_LEN, D), BF)] * 2
    return pl.pallas_call(
        body, grid=(1,), name="mem_kv",
        in_specs=[_const((MEM_LEN, D)), _const((1, D)), _const((4, D, 512))],
        out_specs=[_const((MEM_LEN, D))] * 2, out_shape=outs, compiler_params=_cparams())(mem, nw, wkv4)


def _mem_kv_bwd(mem, nw, wkv4, dk, dv):
    def body(m_ref, nw_ref, w_ref, dk_ref, dv_ref, gnw_ref, gw_ref):
        m, n, _ = _rms(m_ref[...], nw_ref[...])
        mb = m.astype(BF)
        dm = jnp.zeros((MEM_LEN, D), F32)
        for i in range(4):
            src = dk_ref if i < 2 else dv_ref
            d = src[:, 512 * (i % 2):512 * (i % 2) + 512].astype(BF)
            gw_ref[i] = _dot_tn(mb, d)
            dm = dm + _dot_nt(d, w_ref[i])
        gnw_ref[...] = _colsum(dm * n)

    return pl.pallas_call(
        body, grid=(1,), name="mem_kv_bwd",
        in_specs=[_const((MEM_LEN, D)), _const((1, D)), _const((4, D, 512)), _const((MEM_LEN, D)), _const((MEM_LEN, D))],
        out_specs=[_const((1, D)), _const((4, D, 512))],
        out_shape=[SDS((1, D), F32), SDS((4, D, 512), F32)], compiler_params=_cparams())(mem, nw, wkv4, dk, dv)


def _softmax_rows(sc):
    e = jnp.exp(sc - jnp.max(sc, axis=-1, keepdims=True))
    return e * (1.0 / jnp.sum(e, axis=-1, keepdims=True))


def _attn_fwd(x, ya, ob, w_out, nxa, wq, k, v, wo):
    L = x.shape[0]
    tl = min(TL, L)
    scale = XA_HD ** -0.5

    def body(x_ref, ya_ref, ob_ref, wout_ref, nxa_ref, wq_ref, k_ref, v_ref, wo_ref,
             x1_ref, x2_ref, hxa_ref, q_ref, ox_ref):
        x1 = x_ref[...] + jnp.dot(ya_ref[...], wout_ref[:D, :], preferred_element_type=F32) \
            + jnp.dot(ob_ref[...], wout_ref[D:, :], preferred_element_type=F32)
        x1_ref[...] = x1
        h, _, _ = _rms(x1, nxa_ref[...])
        hb = h.astype(BF)
        hxa_ref[...] = hb
        qb = jnp.dot(hb, wq_ref[...], preferred_element_type=F32).astype(BF)
        q_ref[...] = qb
        heads = [slice(hd * XA_HD, (hd + 1) * XA_HD) for hd in range(XA_HEADS)]
        ps = [_softmax_rows(_dot_nt(qb[:, sl], k_ref[:, sl]) * scale) for sl in heads]
        oxs = [_dot(p, v_ref[:, sl]) for p, sl in zip(ps, heads)]
        oxb = jnp.concatenate(oxs, axis=1).astype(BF)
        ox_ref[...] = oxb
        x2_ref[...] = x1 + jnp.dot(oxb, wo_ref[...], preferred_element_type=F32)

    outs = [SDS((L, D), F32), SDS((L, D), F32), SDS((L, D), BF), SDS((L, D), BF), SDS((L, D), BF)]
    return pl.pallas_call(
        body, grid=(L // tl,), name="attn_fwd",
        in_specs=[_rows(tl, D), _rows(tl, D), _rows(tl, D), _resident((2 * D, D)), _const((1, D)), _resident((D, D)),
                  _resident((MEM_LEN, D)), _resident((MEM_LEN, D)), _resident((D, D))],
        out_specs=[_rows(tl, D)] * 5, out_shape=outs, compiler_params=_cparams())(x, ya, ob, w_out, nxa, wq, k, v, wo)


def _ffn_loss(x2, tgt, nffn, nfin, wgt, wut, wd):
    L = x2.shape[0]
    tl = min(TL_FFN, L)

    def body(x2_ref, t_ref, nffn_ref, nfin_ref, wg_ref, wu_ref, wd_ref,
             dx2_ref, h_ref, a_ref, dx3_ref, dg_ref, du_ref, acc_ref):
        _zero_first(acc_ref)
        x2v = x2_ref[...]
        h, n2, r2 = _rms(x2v, nffn_ref[...])
        hb = h.astype(BF)
        h_ref[...] = hb
        g = _dot_nt(hb, wg_ref[...])
        u = _dot_nt(hb, wu_ref[...])
        sg = _sigmoid(g)
        ab = (g * sg * u).astype(BF)
        a_ref[...] = ab
        x3 = x2v + jnp.dot(ab, wd_ref[...], preferred_element_type=F32)
        y, n3, r3 = _rms(x3, nfin_ref[...])
        err = y - t_ref[...]
        acc_ref[0:1, :] += _colsum(err * err)
        dx3, dwf = _rms_bwd(err * (1.0 / D), n3, r3, nfin_ref[...])
        acc_ref[1:2, :] += _colsum(dwf)
        dx3b = dx3.astype(BF)
        dx3_ref[...] = dx3b
        da = _dot_nt(dx3b, wd_ref[...])
        dgb = (da * u * sg * (1.0 + g * (1.0 - sg))).astype(BF)
        dub = (da * g * sg).astype(BF)
        dg_ref[...] = dgb
        du_ref[...] = dub
        dh = jnp.dot(dgb, wg_ref[...], preferred_element_type=F32) + jnp.dot(dub, wu_ref[...], preferred_element_type=F32)
        dn, dwn = _rms_bwd(dh, n2, r2, nffn_ref[...])
        acc_ref[2:3, :] += _colsum(dwn)
        dx2_ref[...] = dx3 + dn

    outs = [SDS((L, D), F32), SDS((L, D), BF), SDS((L, FFN), BF), SDS((L, D), BF), SDS((L, FFN), BF),
            SDS((L, FFN), BF), SDS((8, D), F32)]
    wspec = _resident((FFN, D))
    return pl.pallas_call(
        body, grid=(L // tl,), name="ffn_loss",
        in_specs=[_rows(tl, D), _rows(tl, D), _const((1, D)), _const((1, D)), wspec, wspec, wspec],
        out_specs=[_rows(tl, D), _rows(tl, D), _rows(tl, FFN), _rows(tl, D), _rows(tl, FFN), _rows(tl, FFN),
                   _const((8, D))],
        out_shape=outs, compiler_params=_cparams())(x2, tgt, nffn, nfin, wgt, wut, wd)


def _attn_bwd(dx2, x1, q, k, v, nxa, wq, wo, w_out, phases=()):
    L = dx2.shape[0]
    tl = min(TL, L)
    scale = XA_HD ** -0.5

    def body(dx2_ref, x1_ref, q_ref, k_ref, v_ref, nxa_ref, wq_ref, wo_ref, wout_ref,
             dx1_ref, dya_ref, dob_ref, dq_ref, dk_ref, dv_ref, acc_ref):
        _zero_first(dk_ref, dv_ref, acc_ref)
        dx2v = dx2_ref[...]
        dox = _dot_nt(dx2v, wo_ref[...]).astype(BF)
        qb = q_ref[...]
        heads = [slice(hd * XA_HD, (hd + 1) * XA_HD) for hd in range(XA_HEADS)]
        ps = [_softmax_rows(_dot_nt(qb[:, sl], k_ref[:, sl]) * scale) for sl in heads]
        dps = [_dot_nt(dox[:, sl], v_ref[:, sl]) for sl in heads]
        dss = [(p * (dp - jnp.sum(dp * p, axis=-1, keepdims=True)) * scale).astype(BF) for p, dp in zip(ps, dps)]
        for sl, p, ds in zip(heads, ps, dss):
            dv_ref[:, sl] += _dot_tn(p, dox[:, sl])
            dk_ref[:, sl] += _dot_tn(ds, qb[:, sl])
        dqs = [_dot(ds, k_ref[:, sl]) for sl, ds in zip(heads, dss)]
        dqb = jnp.concatenate(dqs, axis=1).astype(BF)
        dq_ref[...] = dqb
        dh = _dot_nt(dqb, wq_ref[...])
        _, n1, r1 = _rms(x1_ref[...], nxa_ref[...])
        dn, dwn = _rms_bwd(dh, n1, r1, nxa_ref[...])
        acc_ref[0:1, :] += _colsum(dwn)
        dx1 = dx2v + dn
        dx1_ref[...] = dx1
        dx1b = dx1.astype(BF)
        dya_ref[...] = _dot_nt(dx1b, wout_ref[:D, :]).astype(BF)
        dob_ref[...] = _dot_nt(dx1b, wout_ref[D:, :]).astype(BF)

    outs = [SDS((L, D), F32), SDS((L, D), BF), SDS((L, D), BF), SDS((L, D), BF), SDS((MEM_LEN, D), F32),
            SDS((MEM_LEN, D), F32), SDS((8, D), F32)]
    return _call(body, (dx2, x1, q, k, v, nxa, wq, wo, w_out), name="attn_bwd", grid=(L // tl,),
                 in_specs=[_rows(tl, D), _rows(tl, D), _rows(tl, D), _resident((MEM_LEN, D)), _resident((MEM_LEN, D)),
                           _const((1, D)), _resident((D, D)), _resident((D, D)), _resident((2 * D, D))],
                 out_specs=[_rows(tl, D)] * 4 + [_const((MEM_LEN, D)), _const((MEM_LEN, D)), _const((8, D))],
                 out_shape=outs, phases=phases)


def _in_proj_bwd(x, dx1, dz, dxbc, dhq, dhf, dhi, dhg, ddt, nw, wt, phases=()):
    L = x.shape[0]
    tl = min(TL, L)

    def body(x_ref, dx1_ref, dz_ref, dxbc_ref, dhq_ref, dhf_ref, dhi_ref, dhg_ref, ddt_ref, nw_ref, w_ref,
             gx_ref, acc_ref):
        _zero_first(acc_ref)
        dh = _dot(dz_ref[...], w_ref[Z0:XBC0, :]) + _dot(dxbc_ref[...], w_ref[XBC0:DT0, :]) \
            + _dot(ddt_ref[...], w_ref[DT0:DT0 + 128, :]) + _dot(dhq_ref[...], w_ref[HQ0:HF0, :]) \
            + _dot(dhf_ref[...], w_ref[HF0:HI0, :]) + _dot(dhi_ref[...], w_ref[HI0:HG0, :]) \
            + _dot(dhg_ref[...], w_ref[HG0:N_IN, :])
        _, n, r = _rms(x_ref[...], nw_ref[...])
        dn, dwn = _rms_bwd(dh, n, r, nw_ref[...])
        acc_ref[0:1, :] += _colsum(dwn)
        gx_ref[...] = dx1_ref[...] + dn

    return _call(
        body, (x, dx1, dz, dxbc, dhq, dhf, dhi, dhg, ddt, nw, wt), grid=(L // tl,), name="in_proj_bwd",
        in_specs=[_rows(tl, D), _rows(tl, D), _rows(tl, D), _rows(tl, 1536), _rows(tl, D), _rows(tl, D), _rows(tl, D),
                  _rows(tl, D), _rows(tl, 128), _const((1, D)), _resident((N_IN, D))],
        out_specs=[_rows(tl, D), _const((8, D))], out_shape=[SDS((L, D), F32), SDS((8, D), F32)], phases=phases)


def _gw_in(h0, dz, dxbc, ddt, dhq, dhf, dhi, dhg, phases=()):
    L = h0.shape[0]
    tl = min(512, L)

    def body(h_ref, dz_ref, dxbc_ref, ddt_ref, dhq_ref, dhf_ref, dhi_ref, dhg_ref, o_ref):
        _zero_first(o_ref)
        hb = h_ref[...]
        o_ref[Z0:XBC0, :] += _dot_tn(dz_ref[...], hb)
        o_ref[XBC0:DT0, :] += _dot_tn(dxbc_ref[...], hb)
        o_ref[DT0:HQ0, :] += _dot_tn(ddt_ref[...], hb)[0:NH_SSD, :]
        o_ref[HQ0:HF0, :] += _dot_tn(dhq_ref[...], hb)
        o_ref[HF0:HI0, :] += _dot_tn(dhf_ref[...], hb)
        o_ref[HI0:HG0, :] += _dot_tn(dhi_ref[...], hb)
        o_ref[HG0:N_IN, :] += _dot_tn(dhg_ref[...], hb)

    return _call(body, (h0, dz, dxbc, ddt, dhq, dhf, dhi, dhg), name="gw_in", grid=(L // tl,),
                 in_specs=[_rows(tl, D), _rows(tl, D), _rows(tl, 1536), _rows(tl, 128), _rows(tl, D), _rows(tl, D),
                           _rows(tl, D), _rows(tl, D)],
                 out_specs=[_const((N_IN, D))], out_shape=[SDS((N_IN, D), F32)], phases=phases)


def _token_tile(L, out_bytes, row_bytes):
    tl = min(2048, L)
    while tl > 256 and out_bytes + 2 * tl * row_bytes > MATMUL_VMEM:
        tl //= 2
    return tl


def _matmul_tn(a, b, name):
    L, M = a.shape
    N = b.shape[1]
    tl = _token_tile(L, 4 * M * N, M * a.dtype.itemsize + N * b.dtype.itemsize)

    def body(a_ref, b_ref, o_ref):
        _zero_first(o_ref)
        o_ref[...] += _dot_tn(a_ref[...], b_ref[...])

    return pl.pallas_call(
        body, grid=(L // tl,), name=name, in_specs=[_rows(tl, M), _rows(tl, N)], out_specs=_const((M, N)),
        out_shape=SDS((M, N), F32), compiler_params=_cparams())(a, b)


def _matmul_tn_pair(a0, a1, b, name):
    L, M = a0.shape
    N = b.shape[1]
    tl = _token_tile(L, 8 * M * N, 2 * M * a0.dtype.itemsize + N * b.dtype.itemsize)

    def body(a0_ref, a1_ref, b_ref, o_ref):
        _zero_first(o_ref)
        bv = b_ref[...].astype(BF)
        o_ref[:M, :] += _dot_tn(a0_ref[...], bv)
        o_ref[M:, :] += _dot_tn(a1_ref[...], bv)

    return pl.pallas_call(
        body, grid=(L // tl,), name=name, in_specs=[_rows(tl, M), _rows(tl, M), _rows(tl, N)],
        out_specs=_const((2 * M, N)), out_shape=SDS((2 * M, N), F32), compiler_params=_cparams())(a0, a1, b)


def _head_expand():
    e = (jnp.right_shift(_iota((128, D), 1), 6) == _iota((128, D), 0)).astype(BF)
    et = (jnp.right_shift(_iota((D, 128), 0), 6) == _iota((D, 128), 1)).astype(BF)
    return e, et


def _conv_shifts(cur, other, up):
    rows = _iota((Q, 1), 0)
    out = []
    for s in (1, 2, 3):
        if up:
            out.append(jnp.where(rows >= Q - s, pltpu.roll(other, Q - s, 0), pltpu.roll(cur, Q - s, 0)))
        else:
            out.append(jnp.where(rows < s, pltpu.roll(other, s, 0), pltpu.roll(cur, s, 0)))
    return out


def _ssd_pre(u, dtr, dtb, alog):
    e, et = _head_expand()
    sgu = _sigmoid(u)
    xc = u * sgu
    lane = _iota((1, 128), 1)
    hmask = (lane < NH_SSD).astype(F32)
    pre = dtr + dtb
    dt = (jnp.maximum(pre, 0.0) + jnp.log(1.0 + jnp.exp(-jnp.abs(pre)))) * hmask
    a_row = -jnp.exp(alog)
    causal = _iota((Q, Q), 1) <= _iota((Q, Q), 0)
    tri = causal.astype(BF)
    acum = _dot_sel(tri, dt * a_row)
    acum_full = _sel_dot(acum, e)
    alast_full = acum_full[Q - 1:Q, :]
    dt_full = _sel_dot(dt, e)
    xs = xc[:, :D]
    return dict(e=e, et=et, sgu=sgu, xs=xs, bm=xc[:, D:D + 256], cm=xc[:, D + 256:], hmask=hmask, pre=pre, dt=dt,
                a_row=a_row, causal=causal, tri=tri, acum=acum, acum_t=acum.T, eA_full=jnp.exp(acum_full),
                dte_full=jnp.exp(alast_full - acum_full), dt_full=dt_full, xdt=xs * dt_full)


def _ssd_decay(pre, hh, cb):
    seg = pre["acum"][:, hh:hh + 1] - pre["acum_t"][hh:hh + 1, :]
    lm = jnp.where(pre["causal"], jnp.exp(jnp.minimum(seg, 0.0)), 0.0)
    return lm, cb * lm


def _ssd_fwd(xbc, dtr, z, conv_w, conv_b, dtb, alog, dskip_full, nw):
    L = xbc.shape[0]
    nc = L // Q

    def chunk(ck, xbc_ref, dtr_ref, z_ref, cw_ref, cb_ref, dtb_ref, alog_ref, dsk_ref, nw_ref,
              ya_ref, y_ref, u_ref, st_ref, prev_ref, s_ref):
        tok = slice(Q * ck, Q * ck + Q)
        xr = xbc_ref[tok, :].astype(F32)
        sh = _conv_shifts(xr, prev_ref[...], up=False)
        u = cb_ref[...] + cw_ref[3:4, :] * xr + cw_ref[2:3, :] * sh[0] + cw_ref[1:2, :] * sh[1] + cw_ref[0:1, :] * sh[2]
        prev_ref[...] = xr
        ub = u.astype(BF)
        u_ref[tok, :] = ub
        pre = _ssd_pre(ub.astype(F32), dtr_ref[tok, :], dtb_ref[...], alog_ref[...])
        lo = _iota((1, 128), 1) < SSD_P
        s_old = s_ref[...]
        st_ref[ck] = s_old
        ys = []
        for g in range(2):
            bg, cg = pre["bm"][:, 128 * g:128 * g + 128], pre["cm"][:, 128 * g:128 * g + 128]
            cb = _dot_nt(cg, bg)
            gs = slice(512 * g, 512 * g + 512)
            yd = []
            for j in range(4 * g, 4 * g + 4):
                xp = pre["xdt"][:, 128 * j:128 * j + 128].astype(BF)
                _, m0 = _ssd_decay(pre, 2 * j, cb)
                _, m1 = _ssd_decay(pre, 2 * j + 1, cb)
                yd.append(jnp.where(lo, _dot(m0, xp), _dot(m1, xp)))
            yoff = _dot_nt(cg, s_old[gs, :]) * pre["eA_full"][:, gs]
            ys.append(jnp.concatenate(yd, axis=1) + yoff)
            st = _dot_tn((pre["xdt"] * pre["dte_full"])[:, gs], bg)
            cdcol = jnp.exp(_dot_sel(pre["et"][gs, :], pre["acum_t"])[:, Q - 1:Q])
            s_ref[gs, :] = s_old[gs, :] * cdcol + st
        y = jnp.concatenate(ys, axis=1) + dsk_ref[...] * pre["xs"]
        yb = y.astype(BF)
        y_ref[tok, :] = yb
        zf = z_ref[tok, :].astype(F32)
        yz = yb.astype(F32) * zf * _sigmoid(zf)
        outs = []
        for g in range(2):
            gs = slice(512 * g, 512 * g + 512)
            o, _, _ = _rms(yz[:, gs], nw_ref[:, gs])
            outs.append(o)
        ya_ref[tok, :] = jnp.concatenate(outs, axis=1).astype(BF)

    def body(*refs):
        _zero_first(*refs[-2:])
        for ck in range(CH):
            chunk(ck, *refs)

    outs = [SDS((L, D), BF), SDS((L, D), BF), SDS((L, 1536), BF), SDS((nc, D, 128), F32)]
    return pl.pallas_call(
        body, grid=(nc // CH,), name="ssd_fwd",
        in_specs=[_rows(CH * Q, 1536), _rows(CH * Q, 128), _rows(CH * Q, D), _const((4, 1536)), _const((1, 1536)), _const((1, 128)),
                  _const((1, 128)), _const((1, D)), _const((1, D))],
        out_specs=[_rows(CH * Q, D), _rows(CH * Q, D), _rows(CH * Q, 1536),
                   pl.BlockSpec((CH, D, 128), lambda i: (i, 0, 0))],
        out_shape=outs, scratch_shapes=[pltpu.VMEM((Q, 1536), F32), pltpu.VMEM((D, 128), F32)],
        compiler_params=_cparams())(xbc, dtr, z, conv_w, conv_b, dtb, alog, dskip_full, nw)


def _ssd_bwd(dya, y, z, u, xbc, dtr, states, conv_w, dtb, alog, dskip_full, nw):
    L = dya.shape[0]
    nc = L // Q

    def chunk(ck, step, dya_ref, y_ref, z_ref, u_ref, xc_ref, dtr_ref, st_ref, cw_ref, dtb_ref, alog_ref, dsk_ref, nw_ref,
              dz_ref, dxbc_ref, ddt_ref, gconv_ref, ghead_ref, glane_ref, gs_ref, ndu_ref):
        tok = slice(Q * ck, Q * ck + Q)
        uf = u_ref[tok, :].astype(F32)
        pre = _ssd_pre(uf, dtr_ref[tok, :], dtb_ref[...], alog_ref[...])
        e, et, xs, xdt = pre["e"], pre["et"], pre["xs"], pre["xdt"]
        lane = _iota((1, 128), 1)
        lo = lane < SSD_P
        sub = _iota((128, 1), 0)
        zf = z_ref[tok, :].astype(F32)
        sgz = _sigmoid(zf)
        sz = zf * sgz
        yv = y_ref[tok, :].astype(F32)
        yz = yv * sz
        dyav = dya_ref[tok, :].astype(F32)
        dyz, dnw = [], []
        for g in range(2):
            gs = slice(512 * g, 512 * g + 512)
            _, n, r = _rms(yz[:, gs], nw_ref[:, gs])
            dv, dw = _rms_bwd(dyav[:, gs], n, r, nw_ref[:, gs])
            dyz.append(dv)
            dnw.append(dw)
        dyz = jnp.concatenate(dyz, axis=1)
        glane_ref[1:2, :] += _colsum(jnp.concatenate(dnw, axis=1))
        dy = dyz * sz
        dz_ref[tok, :] = (dyz * yv * sgz * (1.0 + zf * (1.0 - sgz))).astype(BF)
        glane_ref[0:1, :] += _colsum(dy * xs)
        dxs = dsk_ref[...] * dy

        s_in = st_ref[ck]
        gst = gs_ref[...]
        gy = dy * pre["eA_full"]
        xdte = xdt * pre["dte_full"]
        dacum = jnp.zeros((Q, 128), F32)
        dacum_t = jnp.zeros((128, Q), F32)
        dxdt, dacum_full, ddte_full, dbs, dcs = [], [], [], [], []
        for g in range(2):
            gs = slice(512 * g, 512 * g + 512)
            bg, cg = pre["bm"][:, 128 * g:128 * g + 128], pre["cm"][:, 128 * g:128 * g + 128]
            sg_, dg_ = s_in[gs, :], gst[gs, :]
            yoff = _dot_nt(cg, sg_) * pre["eA_full"][:, gs]
            dc = _dot(gy[:, gs], sg_)
            dsin = _dot_tn(gy[:, gs], cg)
            dacum_full.append(dy[:, gs] * yoff)
            tg = _dot_nt(bg, dg_)
            ddte_full.append(tg * xdt[:, gs])
            db = _dot(xdte[:, gs], dg_)
            cb = _dot_nt(cg, bg)
            dcb = jnp.zeros((Q, Q), F32)
            dxg = []
            for j in range(4 * g, 4 * g + 4):
                xp = xdt[:, 128 * j:128 * j + 128].astype(BF)
                dyp = dy[:, 128 * j:128 * j + 128]
                dxp = jnp.zeros((Q, 128), F32)
                for idx in range(2):
                    hh = 2 * j + idx
                    lm, m = _ssd_decay(pre, hh, cb)
                    dym = jnp.where(lo if idx == 0 else jnp.logical_not(lo), dyp, 0.0).astype(BF)
                    dm = jnp.where(pre["causal"], _dot_nt(dym, xp), 0.0)
                    w = dm * m
                    dacum = dacum + jnp.where(lane == hh, jnp.sum(w, axis=1, keepdims=True), 0.0)
                    dacum_t = dacum_t + jnp.where(sub == hh, jnp.sum(w, axis=0, keepdims=True), 0.0)
                    dcb = dcb + dm * lm
                    dxp = dxp + _dot_tn(m, dym)
                dxg.append(dxp)
            dxdt.append(jnp.concatenate(dxg, axis=1) + tg * pre["dte_full"][:, gs])
            dcs.append(dc + _dot(dcb, bg))
            dbs.append(db + _dot_tn(dcb, cg))
            cdcol = jnp.exp(_dot_sel(et[gs, :], pre["acum_t"])[:, Q - 1:Q])
            gs_ref[gs, :] = dsin + dg_ * cdcol
        dxdt = jnp.concatenate(dxdt, axis=1)
        dacum = dacum + _sel_dot(jnp.concatenate(dacum_full, axis=1), et, 2) - dacum_t.T
        alast = pre["acum"][Q - 1:Q, :]
        dte = jnp.exp(alast - pre["acum"])
        ddte = _sel_dot(jnp.concatenate(ddte_full, axis=1), et, 2) * dte
        dacum = dacum - ddte
        dcd_col = jnp.sum(_dot_sel(e, gst * s_in, 2), axis=1, keepdims=True)
        dcd_row = jnp.broadcast_to(dcd_col, (128, 128)).T[0:1, :]
        dalast = _colsum(ddte) + dcd_row * jnp.exp(alast)
        dacum = dacum + jnp.where(_iota((Q, 1), 0) == Q - 1, dalast, 0.0)
        ddt = _sel_dot(dxdt * xs, et, 2)
        dxs = dxs + dxdt * pre["dt_full"]
        dda = _dot_sel((_iota((Q, Q), 1) >= _iota((Q, Q), 0)).astype(BF), dacum)
        ddt = ddt + dda * pre["a_row"]
        ghead_ref[1:2, :] += _colsum(dda * pre["dt"])
        ddtr = ddt * _sigmoid(pre["pre"]) * pre["hmask"]
        ghead_ref[0:1, :] += _colsum(ddtr)
        ddt_ref[tok, :] = ddtr

        dxc = jnp.concatenate([dxs] + dbs + dcs, axis=1)
        sgu = pre["sgu"]
        du = dxc * sgu * (1.0 + uf * (1.0 - sgu))
        shu = _conv_shifts(du, ndu_ref[...], up=True)
        dxr = cw_ref[3:4, :] * du + cw_ref[2:3, :] * shu[0] + cw_ref[1:2, :] * shu[1] + cw_ref[0:1, :] * shu[2]
        ndu_ref[...] = du
        dxbc_ref[tok, :] = dxr.astype(BF)
        xr = xc_ref[tok, :].astype(F32)
        gconv_ref[3:4, :] += _colsum(du * xr)
        gconv_ref[2:3, :] += _colsum(shu[0] * xr)
        gconv_ref[1:2, :] += _colsum(shu[1] * xr)
        gconv_ref[0:1, :] += _colsum(shu[2] * xr)
        gconv_ref[4:5, :] += _colsum(du)

        @pl.when(jnp.logical_and(step == nc // CH - 1, ck == 0))
        def _():
            ghead_ref[2:3, :] = ghead_ref[1:2, :] * pre["a_row"]
            ghead_ref[3:4, :] = _sel_dot(glane_ref[...], et)[0:1, :]

    def body(*refs):
        _zero_first(*refs[-5:])
        for ck in reversed(range(CH)):
            chunk(ck, pl.program_id(0), *refs)

    rev = lambda i: (nc // CH - 1 - i, 0)
    outs = [SDS((L, D), BF), SDS((L, 1536), BF), SDS((L, 128), F32), SDS((8, 1536), F32), SDS((8, 128), F32),
            SDS((8, D), F32)]
    return pl.pallas_call(
        body, grid=(nc // CH,), name="ssd_bwd",
        in_specs=[pl.BlockSpec((CH * Q, D), rev), pl.BlockSpec((CH * Q, D), rev), pl.BlockSpec((CH * Q, D), rev),
                  pl.BlockSpec((CH * Q, 1536), rev), pl.BlockSpec((CH * Q, 1536), rev),
                  pl.BlockSpec((CH * Q, 128), rev), pl.BlockSpec((CH, D, 128), lambda i: (nc // CH - 1 - i, 0, 0)),
                  _const((4, 1536)), _const((1, 128)), _const((1, 128)), _const((1, D)), _const((1, D))],
        out_specs=[pl.BlockSpec((CH * Q, D), rev), pl.BlockSpec((CH * Q, 1536), rev), pl.BlockSpec((CH * Q, 128), rev),
                   _const((8, 1536)), _const((8, 128)), _const((8, D))],
        out_shape=outs, scratch_shapes=[pltpu.VMEM((D, 128), F32), pltpu.VMEM((Q, 1536), F32)],
        compiler_params=_cparams())(dya, y, z, u, xbc, dtr, states, conv_w, dtb, alog, dskip_full, nw)


def _hg_gates(hq, hf, hlb):
    h0, h1 = hlb[0:1, :], hlb[1:2, :]
    mx = jnp.maximum(h0, h1)
    e0, e1 = jnp.exp(h0 - mx), jnp.exp(h1 - mx)
    lb = e0 / (e0 + e1)
    sg = _sigmoid(hf)
    fg = lb + (1.0 - lb) * sg
    tri = (_iota((Q, Q), 1) <= _iota((Q, Q), 0)).astype(BF)
    return hq * _sigmoid(hq), 1.0 - fg, fg, sg, lb, e1 / (e0 + e1), _dot_sel(tri, jnp.log(fg))


def _hg_intra(b, q, k):
    rowblk = jnp.right_shift(_iota((Q, 1), 0), SUB.bit_length() - 1)
    mids = [b[SUB * i + SUB // 2:SUB * i + SUB // 2 + 1, :] for i in range(NSUB)]
    prevs = [mids[0]] + [b[SUB * i - 1:SUB * i, :] for i in range(1, NSUB)]
    mfull = jnp.concatenate([jnp.broadcast_to(r, (SUB, 128)) for r in mids], axis=0)
    rfull = jnp.concatenate([jnp.broadcast_to(r, (SUB, 128)) for r in prevs], axis=0)
    eqd, ek, eqo = jnp.exp(b - mfull), jnp.exp(mfull - b), jnp.exp(b - rfull)
    qd, qo, khat = q * eqd, q * eqo, k * ek
    rtab = jnp.concatenate(prevs, axis=0)
    djs = [jnp.exp(rtab - mids[j]) for j in range(NSUB)]
    zero = jnp.zeros((SUB, 128), F32)
    cols = []
    for j in range(NSUB):
        pieces = []
        for i in range(NSUB):
            rs = slice(SUB * i, SUB * i + SUB)
            pieces.append(zero if i < j else qd[rs] if i == j else qo[rs] * djs[j][i:i + 1, :])
        cols.append(jnp.concatenate(pieces, axis=0))
    qt = jnp.concatenate(cols, axis=1).astype(BF)
    kt = jnp.concatenate([jnp.where(rowblk == j, khat, 0.0) for j in range(NSUB)], axis=1).astype(BF)
    causal = _iota((Q, Q), 1) <= _iota((Q, Q), 0)
    att = jnp.where(causal, _dot_nt(qt, kt), 0.0)
    return att, qt, kt, (eqd, ek, eqo, djs), causal


def _hg_intra_bwd(dqt, dkt, qt, kt, factors):
    eqd, ek, eqo, djs = factors
    dqd, dqo, dkh, db = [], [], [], []
    for i in range(NSUB):
        rs = slice(SUB * i, SUB * i + SUB)
        diag = slice(128 * i, 128 * i + 128)
        dqd.append(dqt[rs, diag])
        dkh.append(dkt[rs, diag])
        dbi = qt[rs, diag].astype(F32) * dqt[rs, diag] - kt[rs, diag].astype(F32) * dkt[rs, diag]
        acc = jnp.zeros((SUB, 128), F32)
        for j in range(i):
            bl = slice(128 * j, 128 * j + 128)
            acc = acc + dqt[rs, bl] * djs[j][i:i + 1, :]
            dbi = dbi + qt[rs, bl].astype(F32) * dqt[rs, bl]
        dqo.append(acc)
        db.append(dbi)
    cat = lambda t: jnp.concatenate(t, axis=0)
    return cat(dqd) * eqd + cat(dqo) * eqo, cat(dkh) * ek, cat(db)


def _hg_att_exact(b, q, k, b_ref, q_ref, att_t_ref):
    b_ref[...] = b
    q_ref[...] = q
    att_t_ref[...] = jnp.zeros((Q, Q), F32)
    rows, lane = _iota((Q, 1), 0), _iota((1, Q), 1)

    def step(i, carry):
        e = jnp.exp(jnp.minimum(b_ref[pl.ds(i, 1), :] - b, 0.0))
        col = jnp.sum(q_ref[pl.ds(i, 1), :] * k * e, axis=1, keepdims=True)
        att_t_ref[...] = jnp.where(lane == i, jnp.where(rows <= i, col, 0.0), att_t_ref[...])
        return carry

    lax.fori_loop(0, Q, step, 0)
    return att_t_ref[...].T


def _hg_att_exact_bwd(da, b, q, k, b_ref, q_ref, da_t_ref, dq_ref, dk_ref):
    b_ref[...] = b
    q_ref[...] = q
    da_t_ref[...] = da.T
    dk_ref[...] = jnp.zeros((Q, 128), F32)
    lane = _iota((1, Q), 1)

    def step(i, carry):
        e = jnp.exp(jnp.minimum(b_ref[pl.ds(i, 1), :] - b, 0.0))
        g = jnp.sum(jnp.where(lane == i, da_t_ref[...], 0.0), axis=1, keepdims=True) * e
        dq_ref[pl.ds(i, 1), :] = jnp.sum(g * k, axis=0, keepdims=True)
        dk_ref[...] += g * q_ref[pl.ds(i, 1), :]
        return carry

    lax.fori_loop(0, Q, step, 0)
    dq, dk = dq_ref[...], dk_ref[...]
    return dq, dk, q * dq - k * dk


def _hg_fwd(hq, hf, hi, hg, hlb, nw, fast):
    L = hq.shape[0]
    nc = L // Q

    def chunk(exact, ck, hq_ref, hf_ref, hi_ref, hg_ref, hlb_ref, nw_ref, ob_ref, o_ref, st_ref, s_ref, *tmp):
        tok = slice(Q * ck, Q * ck + Q)
        qf, kf, _, _, _, _, bcum = _hg_gates(hq_ref[tok, :].astype(F32), hf_ref[tok, :], hlb_ref[...])
        gate = hg_ref[tok, :].astype(F32)
        heads = [slice(128 * h, 128 * h + 128) for h in range(NH_HG)]
        if exact:
            atts = [_hg_att_exact(bcum[:, sl], qf[:, sl], kf[:, sl], *tmp).astype(BF) for sl in heads]
        else:
            atts = [_hg_intra(bcum[:, sl], qf[:, sl], kf[:, sl])[0].astype(BF) for sl in heads]
        olds = [s_ref[sl, :] for sl in heads]
        outs_ = [_dot(att, hi_ref[tok, sl]) + _dot(qf[:, sl] * jnp.exp(bcum[:, sl]), s)
                 for att, sl, s in zip(atts, heads, olds)]
        for sl, s, o in zip(heads, olds, outs_):
            b, k = bcum[:, sl], kf[:, sl]
            st_ref[ck, sl, :] = s
            blast = b[Q - 1:Q, :]
            s_ref[sl, :] = s * jnp.exp(b.T[:, Q - 1:Q]) + _dot_tn(k * jnp.exp(blast - b), hi_ref[tok, sl])
            ob = o.astype(BF)
            o_ref[tok, sl] = ob
            on, _, _ = _rms(ob.astype(F32), nw_ref[...])
            gt = gate[:, sl]
            ob_ref[tok, sl] = (on * gt * _sigmoid(gt)).astype(BF)

    def run(exact, *refs):
        for ck in range(CH):
            chunk(exact, ck, *refs)

    def body(fast_ref, *refs):
        _zero_first(refs[9])
        pl.when(fast_ref[0] == 1)(lambda: run(False, *refs))
        pl.when(fast_ref[0] != 1)(lambda: run(True, *refs))

    rows = pl.BlockSpec((CH * Q, D), lambda i, f: (i, 0))
    outs = [SDS((L, D), BF), SDS((L, D), BF), SDS((nc, D, 128), F32)]
    grid_spec = pltpu.PrefetchScalarGridSpec(
        num_scalar_prefetch=1, grid=(nc // CH,),
        in_specs=[rows] * 4 + [pl.BlockSpec((2, D), lambda i, f: (0, 0)), pl.BlockSpec((1, 128), lambda i, f: (0, 0))],
        out_specs=[rows, rows, pl.BlockSpec((CH, D, 128), lambda i, f: (i, 0, 0))],
        scratch_shapes=[pltpu.VMEM((D, 128), F32), pltpu.VMEM((Q, 128), F32), pltpu.VMEM((Q, 128), F32),
                        pltpu.VMEM((Q, Q), F32)])
    return pl.pallas_call(body, grid_spec=grid_spec, name="hg_fwd", out_shape=outs,
                          compiler_params=_cparams())(fast, hq, hf, hi, hg, hlb, nw)


def _hg_bwd(dob, o, hq, hf, hi, hg, states, hlb, nw, fast, phases=()):
    L = dob.shape[0]
    nc = L // Q

    def chunk(exact, ck, step, dob_ref, o_ref, hq_ref, hf_ref, hi_ref, hg_ref, st_ref, hlb_ref, nw_ref,
              dhq_ref, dhf_ref, dhi_ref, dhg_ref, acc_ref, gs_ref, *tmp):
        tok = slice(Q * ck, Q * ck + Q)
        hqv = hq_ref[tok, :].astype(F32)
        qf, kf, fg, sg, lb, sm1, bcum = _hg_gates(hqv, hf_ref[tok, :], hlb_ref[...])
        gate = hg_ref[tok, :].astype(F32)
        sgg = _sigmoid(gate)
        nwv = nw_ref[...]
        tri_t = (_iota((Q, Q), 1) >= _iota((Q, Q), 0)).astype(BF)
        ones8 = jnp.ones((8, 128), BF)
        heads = [slice(128 * h, 128 * h + 128) for h in range(NH_HG)]
        row_last = _iota((Q, 1), 0) == Q - 1
        dobs, dnws = [], []
        for sl in heads:
            gt, sgt = gate[:, sl], sgg[:, sl]
            _, n, r = _rms(o_ref[tok, sl].astype(F32), nwv)
            dobv = dob_ref[tok, sl].astype(F32)
            dhg_ref[tok, sl] = (dobv * n * nwv * sgt * (1.0 + gt * (1.0 - sgt))).astype(BF)
            do, dw = _rms_bwd(dobv * gt * sgt, n, r, nwv)
            dnws.append(_colsum(dw))
            dobs.append(do.astype(BF))
        causal = _iota((Q, Q), 1) <= _iota((Q, Q), 0)
        if exact:
            intra = [(_hg_att_exact(bcum[:, sl], qf[:, sl], kf[:, sl], *tmp[:3]),) for sl in heads]
        else:
            intra = [_hg_intra(bcum[:, sl], qf[:, sl], kf[:, sl]) for sl in heads]
        states = [(st_ref[ck, sl, :], gs_ref[sl, :]) for sl in heads]
        das = [jnp.where(causal, _dot_nt(dob_h, hi_ref[tok, sl]), 0.0) for dob_h, sl in zip(dobs, heads)]
        dqhats = [_dot_nt(dob_h, s) for dob_h, (s, _) in zip(dobs, states)]
        dkhats = [_dot_nt(hi_ref[tok, sl], gst) for sl, (_, gst) in zip(heads, states)]
        if not exact:
            dqts = [jnp.dot(da.astype(BF), it[2], preferred_element_type=F32) for da, it in zip(das, intra)]
            dkts = [lax.dot_general(da.astype(BF), it[1], (((0,), (0,)), ((), ())), preferred_element_type=F32)
                    for da, it in zip(das, intra)]
        dqs, dks, dgls = [], [], []
        for h, sl in enumerate(heads):
            b, q, k = bcum[:, sl], qf[:, sl], kf[:, sl]
            att = intra[h][0]
            s, gst = states[h]
            dob_h, dqhat, dkhat = dobs[h], dqhats[h], dkhats[h]
            eb = jnp.exp(b)
            blast = b[Q - 1:Q, :]
            ekl = jnp.exp(blast - b)
            qhat, khat = q * eb, k * ekl
            dhi_ref[tok, sl] = (_dot_tn(att, dob_h) + _dot(khat, gst)).astype(BF)
            if exact:
                dq_i, dk_i, db = _hg_att_exact_bwd(das[h], b, q, k, *tmp)
            else:
                dq_i, dk_i, db = _hg_intra_bwd(dqts[h], dkts[h], *intra[h][1:4])
            dqs.append(dq_i + dqhat * eb)
            dks.append(dk_i + dkhat * ekl)
            qhat_r, khat_r = qhat.astype(BF).astype(F32), khat.astype(BF).astype(F32)
            decay_row = sum(_dot_nt(ones8, part) for part in _split(gst * s, 2))[0:1, :]
            dblast = _colsum(dkhat * khat_r) + decay_row * jnp.exp(blast)
            dgls.append(db + qhat_r * dqhat - khat_r * dkhat + jnp.where(row_last, dblast, 0.0))
            gs_ref[sl, :] = _dot_tn(qhat, dob_h) + gst * jnp.exp(b.T[:, Q - 1:Q])
        dq, dk, db = (jnp.concatenate(t, axis=1) for t in (dqs, dks, dgls))
        dgl = _dot_sel(tri_t, db, 2)
        sgq = _sigmoid(hqv)
        dhq_ref[tok, :] = (dq * sgq * (1.0 + hqv * (1.0 - sgq))).astype(BF)
        dfg = dgl / fg - dk
        dhf_ref[tok, :] = (dfg * (1.0 - lb) * sg * (1.0 - sg)).astype(BF)
        acc_ref[0:1, :] += _colsum(dfg * (1.0 - sg))
        acc_ref[1:2, :] += jnp.concatenate(dnws, axis=1)

        @pl.when(jnp.logical_and(step == nc // CH - 1, ck == 0))
        def _():
            dlb = acc_ref[0:1, :] * lb * sm1
            acc_ref[2:3, :] = dlb
            acc_ref[3:4, :] = -dlb
            tot = acc_ref[1:2, 0:128]
            for h in range(1, NH_HG):
                tot = tot + acc_ref[1:2, 128 * h:128 * h + 128]
            acc_ref[4:5, 0:128] = tot

    def run(exact, step, *refs):
        for ck in reversed(range(CH)):
            chunk(exact, ck, step, *refs)

    def body(fast_ref, *refs):
        step = pl.program_id(0)
        _zero_first(refs[13], refs[14])
        pl.when(fast_ref[0] == 1)(lambda: run(False, step, *refs))
        pl.when(fast_ref[0] != 1)(lambda: run(True, step, *refs))

    rev = pl.BlockSpec((CH * Q, D), lambda i, f: (nc // CH - 1 - i, 0))
    outs = [SDS((L, D), BF)] * 4 + [SDS((8, D), F32)]
    return _call(
        body, (fast, dob, o, hq, hf, hi, hg, states, hlb, nw), name="hg_bwd", grid=(nc // CH,), prefetch=1,
        in_specs=[rev] * 6 + [pl.BlockSpec((CH, D, 128), lambda i, f: (nc // CH - 1 - i, 0, 0)),
                              pl.BlockSpec((2, D), lambda i, f: (0, 0)), pl.BlockSpec((1, 128), lambda i, f: (0, 0))],
        out_specs=[rev] * 4 + [pl.BlockSpec((8, D), lambda i, f: (0, 0))], out_shape=outs,
        scratch_shapes=[pltpu.VMEM((D, 128), F32), pltpu.VMEM((Q, 128), F32), pltpu.VMEM((Q, 128), F32),
                        pltpu.VMEM((Q, Q), F32), pltpu.VMEM((Q, 128), F32), pltpu.VMEM((Q, 128), F32)], phases=phases)


def _place():
    return lax.axis_index("x"), lax.axis_index("y"), lax.axis_index("c")


def _phase_io(phase):
    kind, arrays, halves = phase
    n = len(arrays)
    dma = pltpu.SemaphoreType.DMA
    if kind == "gather":
        outs = [SDS((8,) + a.shape if hc is None else (4,) + a.shape, a.dtype) for a, hc in zip(arrays, halves)]
        return outs, [dma((7 * n,)), dma((7 * n,)), dma((n,))], {}
    if kind == "sibling":
        return [SDS((4, g.shape[1], hc), g.dtype) for g, hc in zip(arrays, halves)], [dma((n,)), dma((n,))], {}
    if kind == "chips":
        return [SDS((3,) + p.shape[1:], p.dtype) for p in arrays], [dma((3 * n,)), dma((3 * n,))], {}
    assert kind == "swap"
    return [SDS(b.shape, b.dtype) for b in arrays], [dma((n,)), dma((n,))], {a: a for a in range(n)}


def _gather_events(ins, outs, sems, halves):
    send_sems, recv_sems, local_sems = sems
    n = len(ins)

    def parts(a):
        x, y, c = _place()
        hc = halves[a]
        me, sibling = (x, y, c), (x, y, 1 - c)
        chips = [(1 - x, y), (x, 1 - y), (1 - x, 1 - y)]

        def slot(p):
            if hc is None:
                return outs[a].at[4 * p[0] + 2 * p[1] + p[2]]
            return outs[a].at[2 * p[0] + p[1], :, pl.ds(p[2] * hc, hc)]

        own = ins[a] if hc is None else ins[a].at[:, pl.ds(c * hc, hc)]

        def copy(k, piece, to, src=None):
            return pltpu.make_async_remote_copy(
                src_ref=slot(piece) if src is None else src, dst_ref=slot(piece),
                send_sem=send_sems.at[7 * a + k], recv_sem=recv_sems.at[7 * a + k], device_id=to, device_id_type=MESH)

        return dict(
            mine=lambda: pltpu.make_async_copy(own, slot(me), local_sems.at[a]),
            starts=lambda: [copy(0, me, sibling, src=own)] + [copy(1 + j, me, (*chip, c), src=own)
                                                               for j, chip in enumerate(chips)],
            arrive=lambda: [copy(1 + j, (*chip, c), me) for j, chip in enumerate(chips)],
            passed=lambda: [copy(4 + j, (*chip, c), sibling) for j, chip in enumerate(chips)],
            from_sibling=lambda: [copy(0, sibling, me)] + [copy(4 + j, (*chip, 1 - c), me)
                                                            for j, chip in enumerate(chips)])

    def first():
        for a in range(n):
            p = parts(a)
            p["mine"]().start()
            for cp in p["starts"]():
                cp.start()

    def mid():
        for a in range(n):
            p = parts(a)
            for cp_in, cp_out in zip(p["arrive"](), p["passed"]()):
                cp_in.wait_recv()
                cp_out.start()

    def last():
        for a in range(n):
            p = parts(a)
            for cp in p["from_sibling"]():
                cp.wait_recv()
            for cp in p["starts"]() + p["passed"]():
                cp.wait_send()
            p["mine"]().wait()

    return dict(first=first, mid=mid, last=last)


def _exchange_events(kind, ins, outs, sems, halves):
    send_sems, recv_sems = sems
    n = len(outs)

    def copies():
        x, y, c = _place()
        if kind == "sibling":
            return [pltpu.make_async_remote_copy(
                src_ref=ins[a].at[:, :, pl.ds((1 - c) * halves[a], halves[a])], dst_ref=outs[a],
                send_sem=send_sems.at[a], recv_sem=recv_sems.at[a], device_id=(x, y, 1 - c), device_id_type=MESH)
                for a in range(n)]
        chips = [(1 - x, y), (x, 1 - y), (1 - x, 1 - y)]
        return [pltpu.make_async_remote_copy(
            src_ref=ins[a].at[2 * px + py], dst_ref=outs[a].at[k], send_sem=send_sems.at[3 * a + k],
            recv_sem=recv_sems.at[3 * a + k], device_id=(px, py, c), device_id_type=MESH)
            for a in range(n) for k, (px, py) in enumerate(chips)]

    def first():
        for cp in copies():
            cp.start()

    def last():
        for cp in copies():
            cp.wait()

    return dict(first=first, last=last)


def _swap_events(outs, sems, halves):
    send_sems, recv_sems = sems
    n = len(outs)

    def copy(a, landing):
        x, y, c = _place()
        cols = lambda which: outs[a].at[:, pl.ds(which * halves[a], halves[a])]
        return pltpu.make_async_remote_copy(
            src_ref=cols(c), dst_ref=cols(1 - c) if landing else cols(c), send_sem=send_sems.at[a],
            recv_sem=recv_sems.at[a], device_id=(x, y, 1 - c), device_id_type=MESH)

    def first():
        for a in range(n):
            copy(a, False).start()

    def last():
        for a in range(n):
            copy(a, True).wait_recv()
        for a in range(n):
            copy(a, False).wait_send()

    return dict(first=first, last=last)


def _phase_events(phase, ins, outs, sems):
    kind, _, halves = phase
    if kind == "gather":
        return _gather_events(ins, outs, sems, halves)
    if kind == "swap":
        return _swap_events(outs, sems, halves)
    return _exchange_events(kind, ins, outs, sems, halves)


def _split_refs(refs, counts):
    out, at = [], 0
    for c in counts:
        out.append(list(refs[at:at + c]))
        at += c
    return out


def _comm_plumbing(phases, first_in, first_out):
    ios = [_phase_io(p) for p in phases]
    arrays = [a for p in phases for a in p[1]]
    out_shape = [o for io in ios for o in io[0]]
    sem_shapes = [s for io in ios for s in io[1]]
    aliases, ai, ao = {}, first_in, first_out
    for p, io in zip(phases, ios):
        aliases.update({ai + k: ao + v for k, v in io[2].items()})
        ai, ao = ai + len(p[1]), ao + len(io[0])

    def events(cins, couts, sems):
        evs = [_phase_events(p, i, o, s) for p, i, o, s in zip(
            phases, _split_refs(cins, [len(p[1]) for p in phases]), _split_refs(couts, [len(io[0]) for io in ios]),
            _split_refs(sems, [len(io[1]) for io in ios]))]

        def run(key):
            for ev in evs:
                if key in ev:
                    ev[key]()

        return {key: (lambda key=key: run(key)) for key in ("first", "mid", "last")}

    def regroup(flat):
        return _split_refs(flat, [len(io[0]) for io in ios])

    return arrays, out_shape, sem_shapes, aliases, events, regroup


def _run_phases(phases, name):
    arrays, out_shape, sem_shapes, aliases, events, regroup = _comm_plumbing(phases, 0, 0)

    def body(*refs):
        cins, couts, sems = _split_refs(refs, [len(arrays), len(out_shape), len(sem_shapes)])
        ev = events(cins, couts, sems)
        for key in ("first", "mid", "last"):
            ev[key]()

    outs = pl.pallas_call(
        body, name=name, in_specs=[ANY] * len(arrays), out_specs=[ANY] * len(out_shape), out_shape=out_shape,
        scratch_shapes=sem_shapes, input_output_aliases=aliases)(*arrays)
    return regroup(outs)


def _call(body, args, *, name, grid, in_specs, out_specs, out_shape, scratch_shapes=(), prefetch=0, phases=(),
          mid_step=None):
    steps = grid[0]
    arrays, c_shape, sem_shapes, aliases, events, regroup = _comm_plumbing(
        phases, prefetch + len(in_specs), len(out_specs))
    counts = [prefetch, len(in_specs), len(arrays), len(out_specs), len(c_shape), len(scratch_shapes), len(sem_shapes)]

    def wrapped(*refs):
        pre, ins, cins, outs, couts, scratch, sems = _split_refs(refs, counts)
        if not phases:
            return body(*pre, *ins, *outs, *scratch)
        step = pl.program_id(0)
        ev = events(cins, couts, sems)
        pl.when(step == 0)(ev["first"])
        body(*pre, *ins, *outs, *scratch)
        pl.when(step == (steps // 2 if mid_step is None else mid_step))(ev["mid"])
        pl.when(step == steps - 1)(ev["last"])

    grid_spec = pltpu.PrefetchScalarGridSpec(
        num_scalar_prefetch=prefetch, grid=grid, in_specs=list(in_specs) + [ANY] * len(arrays),
        out_specs=list(out_specs) + [ANY] * len(c_shape), scratch_shapes=list(scratch_shapes) + sem_shapes)
    outs = pl.pallas_call(
        wrapped, grid_spec=grid_spec, name=name, out_shape=list(out_shape) + c_shape, input_output_aliases=aliases,
        compiler_params=_cparams())(*args, *arrays)
    return list(outs[:len(out_specs)]), regroup(outs[len(out_specs):])


def _tile(rows, cols, nbuf):
    budget = (VMEM_LIMIT // 3) // (2 * nbuf * 4)
    if rows % 8 == 0:
        cands = [t for t in range(8, rows + 1, 8) if rows % t == 0 and t * cols <= budget]
        pref = [t for t in cands if t % 16 == 0]
        return (max(pref) if pref else max(cands) if cands else 8), cols
    cands = [t for t in range(128, cols + 1, 128) if cols % t == 0 and rows * t <= budget]
    return rows, (max(cands) if cands else 128)


def _chip_sum(g, from_sib, place, name):
    _, rows, hc = from_sib.shape
    tr, tc = _tile(rows, hc, 4)
    ni, nj = rows // tr, hc // tc

    def body(p_ref, g_ref, s_ref, hb_ref, own_ref):
        s = g_ref[...] + s_ref[...]
        hb_ref[...] = s.astype(BF)

        @pl.when(pl.program_id(2) == p_ref[1])
        def _():
            own_ref[...] = s

    grid_spec = pltpu.PrefetchScalarGridSpec(
        num_scalar_prefetch=1, grid=(ni, nj, 4),
        in_specs=[pl.BlockSpec((None, tr, tc), lambda i, j, k, p: (k, i, p[0] * nj + j)),
                  pl.BlockSpec((None, tr, tc), lambda i, j, k, p: (k, i, j))],
        out_specs=[pl.BlockSpec((None, tr, tc), lambda i, j, k, p: (k, i, j)),
                   pl.BlockSpec((tr, tc), lambda i, j, k, p: (i, j))])
    return pl.pallas_call(
        body, grid_spec=grid_spec, name=name, out_shape=[SDS((4, rows, hc), BF), SDS((rows, hc), F32)],
        compiler_params=pltpu.CompilerParams(dimension_semantics=("arbitrary",) * 3,
                                             vmem_limit_bytes=VMEM_LIMIT))(place, g, from_sib)


def _total(own, parts, place, name):
    rows, hc = own.shape
    tr, tc = _tile(rows, hc, 5)
    ni, nj = rows // tr, hc // tc

    def body(p_ref, own_ref, parts_ref, o_ref):
        s = own_ref[...]
        for k in range(3):
            s = s + parts_ref[k].astype(F32)
        o_ref[...] = s

    grid_spec = pltpu.PrefetchScalarGridSpec(
        num_scalar_prefetch=1, grid=(ni, nj),
        in_specs=[pl.BlockSpec((tr, tc), lambda i, j, p: (i, j)),
                  pl.BlockSpec((3, tr, tc), lambda i, j, p: (0, i, j))],
        out_specs=pl.BlockSpec((tr, tc), lambda i, j, p: (i, p[0] * nj + j)))
    return pl.pallas_call(
        body, grid_spec=grid_spec, name=name, out_shape=SDS((rows, 2 * hc), F32),
        compiler_params=pltpu.CompilerParams(dimension_semantics=("arbitrary",) * 2,
                                             vmem_limit_bytes=VMEM_LIMIT))(place, own, parts)


def _sum8(parts, name):
    R = parts.shape[1]

    def body(p_ref, o_ref):
        s = p_ref[0]
        for k in range(1, 8):
            s = s + p_ref[k]
        o_ref[...] = s

    return pl.pallas_call(
        body, grid=(1,), name=name, in_specs=[_const((8, R, 128))], out_specs=_const((R, 128)),
        out_shape=SDS((R, 128), F32), compiler_params=_cparams())(parts)


def _adamw(w, g, m, v, name):
    _, R, C = w.shape
    tr, tc = _tile(R, C, 7)
    c1 = 1.0 / (1.0 - ADAM_B1 ** ADAM_STEP)
    c2 = 1.0 / (1.0 - ADAM_B2 ** ADAM_STEP)

    def body(w_ref, g_ref, m_ref, v_ref, d_ref, nm_ref, nv_ref):
        gv = g_ref[...]
        nm = ADAM_B1 * m_ref[...] + (1.0 - ADAM_B1) * gv
        nv = ADAM_B2 * v_ref[...] + (1.0 - ADAM_B2) * gv * gv
        nm_ref[...] = nm
        nv_ref[...] = nv
        d_ref[...] = -ADAM_LR * ((nm * c1) / (jnp.sqrt(nv * c2) + ADAM_EPS) + ADAM_WD * w_ref[...])

    blk3 = pl.BlockSpec((None, tr, tc), lambda i, j: (0, i, j))
    return pl.pallas_call(
        body, grid=(R // tr, C // tc), name=name,
        in_specs=[blk3, pl.BlockSpec((tr, tc), lambda i, j: (i, j)), blk3, blk3], out_specs=[blk3] * 3,
        out_shape=[SDS((1, R, C), F32)] * 3,
        compiler_params=pltpu.CompilerParams(dimension_semantics=("arbitrary",) * 2,
                                             vmem_limit_bytes=VMEM_LIMIT))(w, g, m, v)


def _pack_small(parts):
    rows = []
    for p in parts:
        p = p.reshape(-1)
        rows.append(jnp.pad(p, (0, (-p.shape[0]) % 128)).reshape(-1, 128))
    out = jnp.concatenate(rows, axis=0)
    return jnp.pad(out, ((0, (-out.shape[0]) % 8), (0, 0)))


def _unpack_small(packed, shapes):
    out, row = [], 0
    for shp in shapes:
        n = 1
        for s in shp:
            n *= s
        nr = -(-n // 128)
        out.append(packed[row:row + nr].reshape(-1)[:n].reshape(shp))
        row += nr
    return out


def _pad_lanes(v, n=128):
    return jnp.pad(v, ((0, 0), (0, n - v.shape[1])))


GROUP_FFN = ("ffn_w_gate", "ffn_w_up", "ffn_w_down")
GROUP_ATTN = ("w_out", "xa_wq", "xa_wkv", "xa_wo")


def kernel(x, mem, norm_mix_w, w_in, conv_w, conv_b, dt_bias, a_log, d_skip, ssd_norm_w, hg_lower_bounds, hg_norm_w, w_out, norm_xa_w, norm_mem_w, xa_wq, xa_wkv, xa_wo, norm_ffn_w, ffn_w_gate, ffn_w_up, ffn_w_down, norm_final_w, loss_target, m_norm_mix_w, m_w_in, m_conv_w, m_conv_b, m_dt_bias, m_a_log, m_d_skip, m_ssd_norm_w, m_hg_lower_bounds, m_hg_norm_w, m_w_out, m_norm_xa_w, m_norm_mem_w, m_xa_wq, m_xa_wkv, m_xa_wo, m_norm_ffn_w, m_ffn_w_gate, m_ffn_w_up, m_ffn_w_down, m_norm_final_w, v_norm_mix_w, v_w_in, v_conv_w, v_conv_b, v_dt_bias, v_a_log, v_d_skip, v_ssd_norm_w, v_hg_lower_bounds, v_hg_norm_w, v_w_out, v_norm_xa_w, v_norm_mem_w, v_xa_wq, v_xa_wkv, v_xa_wo, v_norm_ffn_w, v_ffn_w_gate, v_ffn_w_up, v_ffn_w_down, v_norm_final_w):
    w = dict(norm_mix_w=norm_mix_w, w_in=w_in, conv_w=conv_w, conv_b=conv_b, dt_bias=dt_bias, a_log=a_log, d_skip=d_skip,
             ssd_norm_w=ssd_norm_w, hg_lower_bounds=hg_lower_bounds, hg_norm_w=hg_norm_w, w_out=w_out,
             norm_xa_w=norm_xa_w, norm_mem_w=norm_mem_w, xa_wq=xa_wq, xa_wkv=xa_wkv, xa_wo=xa_wo, norm_ffn_w=norm_ffn_w,
             ffn_w_gate=ffn_w_gate, ffn_w_up=ffn_w_up, ffn_w_down=ffn_w_down, norm_final_w=norm_final_w)
    m = dict(norm_mix_w=m_norm_mix_w, w_in=m_w_in, conv_w=m_conv_w, conv_b=m_conv_b, dt_bias=m_dt_bias, a_log=m_a_log,
             d_skip=m_d_skip, ssd_norm_w=m_ssd_norm_w, hg_lower_bounds=m_hg_lower_bounds, hg_norm_w=m_hg_norm_w,
             w_out=m_w_out, norm_xa_w=m_norm_xa_w, norm_mem_w=m_norm_mem_w, xa_wq=m_xa_wq, xa_wkv=m_xa_wkv,
             xa_wo=m_xa_wo, norm_ffn_w=m_norm_ffn_w, ffn_w_gate=m_ffn_w_gate, ffn_w_up=m_ffn_w_up,
             ffn_w_down=m_ffn_w_down, norm_final_w=m_norm_final_w)
    v = dict(norm_mix_w=v_norm_mix_w, w_in=v_w_in, conv_w=v_conv_w, conv_b=v_conv_b, dt_bias=v_dt_bias, a_log=v_a_log,
             d_skip=v_d_skip, ssd_norm_w=v_ssd_norm_w, hg_lower_bounds=v_hg_lower_bounds, hg_norm_w=v_hg_norm_w,
             w_out=v_w_out, norm_xa_w=v_norm_xa_w, norm_mem_w=v_norm_mem_w, xa_wq=v_xa_wq, xa_wkv=v_xa_wkv,
             xa_wo=v_xa_wo, norm_ffn_w=v_norm_ffn_w, ffn_w_gate=v_ffn_w_gate, ffn_w_up=v_ffn_w_up,
             ffn_w_down=v_ffn_w_down, norm_final_w=v_norm_final_w)
    xi, yi, ci = _place()
    chip = 2 * xi + yi
    place = jnp.stack([ci, chip]).astype(jnp.int32)

    def shard(t, name):
        return jnp.swapaxes(t[name], 1, 2) if name in TRANSPOSED else t[name]

    wsh = {name: shard(w, name) for name in BIG}
    half = {name: wsh[name].shape[2] // 2 for name in BIG}
    payload = {name: wsh[name][0].astype(BF) for name in BIG}
    ws = {name: w[name] for name in SMALL}
    xs, mems, tgt = x[0], mem[0], loss_target[0]

    def chip_sums(names, grads, from_sib):
        return [_chip_sum(grads[n], s, place, "grads_chip_sum_" + n) for n, s in zip(names, from_sib)]

    def totals(names, sums, others):
        return [_total(own, o, place, "grads_total_" + n) for n, (_, own), o in zip(names, sums, others)]

    ((w_in4, conv_all),) = _run_phases([("gather", [payload["w_in"], conv_w[0]], [half["w_in"], None])], "gather_w_in")
    w_in_t = w_in4.reshape(N_IN, D)
    ws["conv_w"] = conv_all[0::2].transpose(1, 0, 2).reshape(1, 4, 1536)
    rest = [n for n in BIG if n != "w_in"]
    (h0, z, xbc, hq, hf, hi, hg, dtr), (gathered,) = _in_proj(
        xs, ws["norm_mix_w"], w_in_t, phases=[("gather", [payload[n] for n in rest], [half[n] for n in rest])])
    wg = dict(zip(rest, gathered))
    wg_t, wu_t = wg["ffn_w_gate"].reshape(FFN, D), wg["ffn_w_up"].reshape(FFN, D)
    wd = wg["ffn_w_down"].reshape(FFN, D)
    w_out_f = wg["w_out"].reshape(2 * D, D)
    wq, wo = wg["xa_wq"].reshape(D, D), wg["xa_wo"].reshape(D, D)
    dtb, alog = _pad_lanes(ws["dt_bias"]), _pad_lanes(ws["a_log"])
    dskip_full = jnp.repeat(ws["d_skip"], SSD_P, axis=1)
    cw, conv_bias = ws["conv_w"][0], ws["conv_b"]
    hlb = ws["hg_lower_bounds"]
    hg_fast = (jnp.min(jax.nn.softmax(hlb, axis=0)[0]) >= HG_LB_FLOOR).astype(jnp.int32).reshape(1)

    ya, yssd, u, st_ssd = _ssd_fwd(xbc, dtr, z, cw, conv_bias, dtb, alog, dskip_full, ws["ssd_norm_w"])
    ob, ohg, st_hg = _hg_fwd(hq, hf, hi, hg, hlb, ws["hg_norm_w"], hg_fast)
    kmem, vmem = _mem_kv(mems, ws["norm_mem_w"], wg["xa_wkv"])
    x1, x2, hxa, q, ox = _attn_fwd(xs, ya, ob, w_out_f, ws["norm_xa_w"], wq, kmem, vmem, wo)
    nfin = ws["norm_final_w"].reshape(1, D)
    dx2, hffn, act, dx3, dg, du, acc_f = _ffn_loss(x2, tgt, ws["norm_ffn_w"], nfin, wg_t, wu_t, wd)

    gb = {"ffn_w_gate": _matmul_tn(dg, hffn, "gw_gate").reshape(4, FFN // 4, D),
          "ffn_w_up": _matmul_tn(du, hffn, "gw_up").reshape(4, FFN // 4, D),
          "ffn_w_down": _matmul_tn(act, dx3, "gw_down").reshape(4, FFN // 4, D)}
    (dx1, dya, dob, dq, dk, dv, acc_a), (sib_ffn,) = _attn_bwd(
        dx2, x1, q, kmem, vmem, ws["norm_xa_w"], wq, wo, w_out_f,
        phases=[("sibling", [gb[n] for n in GROUP_FFN], [half[n] for n in GROUP_FFN])])
    sums_ffn = chip_sums(GROUP_FFN, gb, sib_ffn)
    g_nmem, gb["xa_wkv"] = _mem_kv_bwd(mems, ws["norm_mem_w"], wg["xa_wkv"], dk, dv)
    gb["w_out"] = _matmul_tn_pair(ya, ob, dx1, "gw_out").reshape(4, D // 2, D)
    gb["xa_wq"] = _matmul_tn(hxa, dq, "gw_q").reshape(4, D // 4, D)
    gb["xa_wo"] = _matmul_tn(ox, dx2, "gw_o").reshape(4, D // 4, D)
    (dhq, dhf, dhi, dhg, acc_h), (others_ffn, sib_attn) = _hg_bwd(
        dob, ohg, hq, hf, hi, hg, st_hg, hlb, ws["hg_norm_w"], hg_fast,
        phases=[("chips", [hb for hb, _ in sums_ffn], None),
                ("sibling", [gb[n] for n in GROUP_ATTN], [half[n] for n in GROUP_ATTN])])
    red_ffn = totals(GROUP_FFN, sums_ffn, others_ffn)
    sums_attn = chip_sums(GROUP_ATTN, gb, sib_attn)
    dz, dxbc, ddt, gconv, ghead, glane = _ssd_bwd(dya, yssd, z, u, xbc, dtr, st_ssd, cw, dtb, alog, dskip_full,
                                                  ws["ssd_norm_w"])
    (gw_in_t,), (g_ffn, others_attn) = _gw_in(
        h0, dz, dxbc, ddt, dhq, dhf, dhi, dhg,
        phases=[("swap", red_ffn, [half[n] for n in GROUP_FFN]), ("chips", [hb for hb, _ in sums_attn], None)])
    red_attn = totals(GROUP_ATTN, sums_attn, others_attn)
    gb["w_in"] = gw_in_t.reshape(4, N_IN // 4, D)
    g_attn, (sib_in,) = _run_phases([("swap", red_attn, [half[n] for n in GROUP_ATTN]),
                                     ("sibling", [gb["w_in"]], [half["w_in"]])], "grads_w_in_to_sibling")
    sums_in = chip_sums(("w_in",), gb, [sib_in])
    (gx, acc_i), ((others_in,),) = _in_proj_bwd(xs, dx1, dz, dxbc, dhq, dhf, dhi, dhg, ddt, ws["norm_mix_w"], w_in_t,
                                                phases=[("chips", [sums_in[0][0]], None)])
    red_in = totals(("w_in",), sums_in, [others_in])

    gs = {
        "norm_mix_w": acc_i[0:1], "conv_w": gconv[0:4][None], "conv_b": gconv[4:5],
        "dt_bias": ghead[0:1, :NH_SSD], "a_log": ghead[2:3, :NH_SSD], "d_skip": ghead[3:4, :NH_SSD],
        "ssd_norm_w": glane[1:2], "hg_lower_bounds": acc_h[2:4], "hg_norm_w": acc_h[4:5, :128],
        "norm_xa_w": acc_a[0:1], "norm_mem_w": g_nmem, "norm_ffn_w": acc_f[2:3], "norm_final_w": acc_f[1],
    }
    loss = (0.5 / D) * jnp.sum(acc_f[0])
    small_parts = [gs[name] for name in SMALL] + [loss.reshape(1)]
    small_shapes = [gs[name].shape for name in SMALL] + [(1,)]
    (g_in,), (packed,) = _run_phases([("swap", red_in, [half["w_in"]]),
                                      ("gather", [_pack_small(small_parts)], [None])], "grads_finish")
    g_big = dict(zip(GROUP_FFN + GROUP_ATTN + ("w_in",), g_ffn + g_attn + [g_in]))
    small = _unpack_small(_sum8(packed, "small_total"), small_shapes)
    g_small = dict(zip(SMALL, small[:-1]))
    loss_all = small[-1][0]
    g_small["conv_w"] = lax.dynamic_slice_in_dim(g_small["conv_w"], chip * 384, 384, 2)

    grads, delta, new_m, new_v = {}, {}, {}, {}
    for name in BIG:
        outs = (g_big[name][None],) + tuple(_adamw(wsh[name], g_big[name], shard(m, name), shard(v, name),
                                                   "adamw_" + name))
        if name in TRANSPOSED:
            outs = tuple(jnp.swapaxes(o, 1, 2) for o in outs)
        grads[name], delta[name], new_m[name], new_v[name] = outs
    shapes = [w[name].shape for name in SMALL]
    packs = [_pack_small([t[name] for name in SMALL]) for t in (w, g_small, m, v)]
    outs = _adamw(packs[0][None], packs[1], packs[2][None], packs[3][None], "adamw_small")
    for name, g_, d_, nm_, nv_ in zip(SMALL, [g_small[n] for n in SMALL], *[_unpack_small(o[0], shapes) for o in outs]):
        grads[name] = g_.reshape(w[name].shape)
        delta[name], new_m[name], new_v[name] = d_, nm_, nv_

    return (loss_all, gx[None], *[grads[n] for n in WEIGHTS], *[delta[n] for n in WEIGHTS],
            *[new_m[n] for n in WEIGHTS], *[new_v[n] for n in WEIGHTS])
```

```python
import jax
import jax.numpy as jnp
from jax import lax
from jax.experimental import pallas as pl
from jax.experimental.pallas import tpu as pltpu

F32 = jnp.float32
BF = jnp.bfloat16
MESH = pl.DeviceIdType.MESH
SDS = jax.ShapeDtypeStruct
ANY = pl.BlockSpec(memory_space=pl.ANY)

D = 1024
EPS = 1e-6
NH_SSD = 16
SSD_P = 64
NH_HG = 8
Q = 128
CH = 2
SUB = 32
NSUB = Q // SUB
HG_LB_FLOOR = 1e-2
XA_HEADS = 4
XA_HD = 256
MEM_LEN = 256
FFN = 2816
TL = 512
TL_FFN = 256
VMEM_LIMIT = 56 << 20
MATMUL_VMEM = 40 << 20

N_IN = 6672
Z0, XBC0, DT0, HQ0, HF0, HI0, HG0 = 0, 1024, 2560, 2576, 3600, 4624, 5648

ADAM_LR, ADAM_B1, ADAM_B2, ADAM_EPS, ADAM_WD, ADAM_STEP = 0.001, 0.9, 0.999, 1e-08, 0.01, 10

BIG = ("w_in", "w_out", "xa_wq", "xa_wkv", "xa_wo", "ffn_w_gate", "ffn_w_up", "ffn_w_down")
TRANSPOSED = ("w_in", "ffn_w_gate", "ffn_w_up")
SMALL = ("norm_mix_w", "conv_w", "conv_b", "dt_bias", "a_log", "d_skip", "ssd_norm_w", "hg_lower_bounds",
         "hg_norm_w", "norm_xa_w", "norm_mem_w", "norm_ffn_w", "norm_final_w")
WEIGHTS = ("norm_mix_w", "w_in", "conv_w", "conv_b", "dt_bias", "a_log", "d_skip", "ssd_norm_w", "hg_lower_bounds",
           "hg_norm_w", "w_out", "norm_xa_w", "norm_mem_w", "xa_wq", "xa_wkv", "xa_wo", "norm_ffn_w", "ffn_w_gate",
           "ffn_w_up", "ffn_w_down", "norm_final_w")


def _cparams():
    return pltpu.CompilerParams(dimension_semantics=("arbitrary",), vmem_limit_bytes=VMEM_LIMIT)


def _const(shape):
    return pl.BlockSpec(shape, lambda i: (0,) * len(shape))


def _resident(shape):
    return pl.BlockSpec(shape, lambda i: (0,) * len(shape), pipeline_mode=pl.Buffered(1))


def _rows(tl, n):
    return pl.BlockSpec((tl, n), lambda i: (i, 0))


def _dot(a, b):
    return jnp.dot(a.astype(BF), b.astype(BF), preferred_element_type=F32)


def _dot_nt(a, b):
    return lax.dot_general(a.astype(BF), b.astype(BF), (((1,), (1,)), ((), ())), preferred_element_type=F32)


def _dot_tn(a, b):
    return lax.dot_general(a.astype(BF), b.astype(BF), (((0,), (0,)), ((), ())), preferred_element_type=F32)


def _split(v, passes):
    parts, rest = [], v
    for p in range(passes):
        hi = rest.astype(BF)
        parts.append(hi)
        if p + 1 < passes:
            rest = rest - hi.astype(F32)
    return parts


def _sel_dot(a, sel, passes=3):
    sb = sel.astype(BF)
    out = None
    for part in _split(a, passes):
        t = jnp.dot(part, sb, preferred_element_type=F32)
        out = t if out is None else out + t
    return out


def _dot_sel(sel, b, passes=3):
    sb = sel.astype(BF)
    out = None
    for part in _split(b, passes):
        t = jnp.dot(sb, part, preferred_element_type=F32)
        out = t if out is None else out + t
    return out


def _iota(shape, dim):
    return lax.broadcasted_iota(jnp.int32, shape, dim)


def _sigmoid(v):
    return 0.5 * jnp.tanh(0.5 * v) + 0.5


def _rms(v, w):
    r = lax.rsqrt(jnp.mean(v * v, axis=-1, keepdims=True) + EPS)
    n = v * r
    return n * w, n, r


def _rms_bwd(dy, n, r, w):
    dn = dy * w
    return r * (dn - n * jnp.mean(dn * n, axis=-1, keepdims=True)), dy * n


def _colsum(v):
    return jnp.sum(v, axis=0, keepdims=True)


def _zero_first(*refs):
    @pl.when(pl.program_id(0) == 0)
    def _():
        for r in refs:
            r[...] = jnp.zeros_like(r)


def _in_proj(x, nw, wt, phases=()):
    L = x.shape[0]
    tl = min(TL, L)

    def body(x_ref, nw_ref, w_ref, h0_ref, z_ref, xbc_ref, hq_ref, hf_ref, hi_ref, hg_ref, dt_ref):
        h, _, _ = _rms(x_ref[...], nw_ref[...])
        hb = h.astype(BF)
        h0_ref[...] = hb

        def proj(a, b):
            return _dot_nt(hb, w_ref[a:b, :])

        z_ref[...] = proj(Z0, XBC0).astype(BF)
        xbc_ref[...] = proj(XBC0, DT0).astype(BF)
        dt_ref[...] = proj(DT0, DT0 + 128)
        hq_ref[...] = proj(HQ0, HF0).astype(BF)
        hf_ref[...] = proj(HF0, HI0)
        hi_ref[...] = proj(HI0, HG0).astype(BF)
        hg_ref[...] = proj(HG0, N_IN).astype(BF)

    outs = [SDS((L, D), BF), SDS((L, D), BF), SDS((L, 1536), BF), SDS((L, D), BF), SDS((L, D), F32),
            SDS((L, D), BF), SDS((L, D), BF), SDS((L, 128), F32)]
    steps = L // tl
    return _call(body, (x, nw, wt), name="in_proj", grid=(steps,),
                 in_specs=[_rows(tl, D), _const((1, D)), _resident((N_IN, D))],
                 out_specs=[_rows(tl, o.shape[1]) for o in outs], out_shape=outs, phases=phases,
                 mid_step=(3 * steps) // 4)


def _mem_kv(mem, nw, wkv4):
    def body(m_ref, nw_ref, w_ref, k_ref, v_ref):
        m, _, _ = _rms(m_ref[...], nw_ref[...])
        mb = m.astype(BF)
        for i in range(2):
            sl = slice(512 * i, 512 * i + 512)
            k_ref[:, sl] = jnp.dot(mb, w_ref[i], preferred_element_type=F32).astype(BF)
            v_ref[:, sl] = jnp.dot(mb, w_ref[2 + i], preferred_element_type=F32).astype(BF)

    outs = [SDS((MEM_LEN, D), BF)] * 2
    return pl.pallas_call(
        body, grid=(1,), name="mem_kv",
        in_specs=[_const((MEM_LEN, D)), _const((1, D)), _const((4, D, 512))],
        out_specs=[_const((MEM_LEN, D))] * 2, out_shape=outs, compiler_params=_cparams())(mem, nw, wkv4)


def _mem_kv_bwd(mem, nw, wkv4, dk, dv):
    def body(m_ref, nw_ref, w_ref, dk_ref, dv_ref, gnw_ref, gw_ref):
        m, n, _ = _rms(m_ref[...], nw_ref[...])
        mb = m.astype(BF)
        dm = jnp.zeros((MEM_LEN, D), F32)
        for i in range(4):
            src = dk_ref if i < 2 else dv_ref
            d = src[:, 512 * (i % 2):512 * (i % 2) + 512].astype(BF)
            gw_ref[i] = _dot_tn(mb, d)
            dm = dm + _dot_nt(d, w_ref[i])
        gnw_ref[...] = _colsum(dm * n)

    return pl.pallas_call(
        body, grid=(1,), name="mem_kv_bwd",
        in_specs=[_const((MEM_LEN, D)), _const((1, D)), _const((4, D, 512)), _const((MEM_LEN, D)), _const((MEM_LEN, D))],
        out_specs=[_const((1, D)), _const((4, D, 512))],
        out_shape=[SDS((1, D), F32), SDS((4, D, 512), F32)], compiler_params=_cparams())(mem, nw, wkv4, dk, dv)


def _softmax_rows(sc):
    e = jnp.exp(sc - jnp.max(sc, axis=-1, keepdims=True))
    return e * (1.0 / jnp.sum(e, axis=-1, keepdims=True))


def _attn_fwd(x, ya, ob, w_out, nxa, wq, k, v, wo):
    L = x.shape[0]
    tl = min(TL, L)
    scale = XA_HD ** -0.5

    def body(x_ref, ya_ref, ob_ref, wout_ref, nxa_ref, wq_ref, k_ref, v_ref, wo_ref,
             x1_ref, x2_ref, hxa_ref, q_ref, ox_ref):
        x1 = x_ref[...] + jnp.dot(ya_ref[...], wout_ref[:D, :], preferred_element_type=F32) \
            + jnp.dot(ob_ref[...], wout_ref[D:, :], preferred_element_type=F32)
        x1_ref[...] = x1
        h, _, _ = _rms(x1, nxa_ref[...])
        hb = h.astype(BF)
        hxa_ref[...] = hb
        qb = jnp.dot(hb, wq_ref[...], preferred_element_type=F32).astype(BF)
        q_ref[...] = qb
        heads = [slice(hd * XA_HD, (hd + 1) * XA_HD) for hd in range(XA_HEADS)]
        ps = [_softmax_rows(_dot_nt(qb[:, sl], k_ref[:, sl]) * scale) for sl in heads]
        oxs = [_dot(p, v_ref[:, sl]) for p, sl in zip(ps, heads)]
        oxb = jnp.concatenate(oxs, axis=1).astype(BF)
        ox_ref[...] = oxb
        x2_ref[...] = x1 + jnp.dot(oxb, wo_ref[...], preferred_element_type=F32)

    outs = [SDS((L, D), F32), SDS((L, D), F32), SDS((L, D), BF), SDS((L, D), BF), SDS((L, D), BF)]
    return pl.pallas_call(
        body, grid=(L // tl,), name="attn_fwd",
        in_specs=[_rows(tl, D), _rows(tl, D), _rows(tl, D), _resident((2 * D, D)), _const((1, D)), _resident((D, D)),
                  _resident((MEM_LEN, D)), _resident((MEM_LEN, D)), _resident((D, D))],
        out_specs=[_rows(tl, D)] * 5, out_shape=outs, compiler_params=_cparams())(x, ya, ob, w_out, nxa, wq, k, v, wo)


def _ffn_loss(x2, tgt, nffn, nfin, wgt, wut, wd):
    L = x2.shape[0]
    tl = min(TL_FFN, L)

    def body(x2_ref, t_ref, nffn_ref, nfin_ref, wg_ref, wu_ref, wd_ref,
             dx2_ref, h_ref, a_ref, dx3_ref, dg_ref, du_ref, acc_ref):
        _zero_first(acc_ref)
        x2v = x2_ref[...]
        h, n2, r2 = _rms(x2v, nffn_ref[...])
        hb = h.astype(BF)
        h_ref[...] = hb
        g = _dot_nt(hb, wg_ref[...])
        u = _dot_nt(hb, wu_ref[...])
        sg = _sigmoid(g)
        ab = (g * sg * u).astype(BF)
        a_ref[...] = ab
        x3 = x2v + jnp.dot(ab, wd_ref[...], preferred_element_type=F32)
        y, n3, r3 = _rms(x3, nfin_ref[...])
        err = y - t_ref[...]
        acc_ref[0:1, :] += _colsum(err * err)
        dx3, dwf = _rms_bwd(err * (1.0 / D), n3, r3, nfin_ref[...])
        acc_ref[1:2, :] += _colsum(dwf)
        dx3b = dx3.astype(BF)
        dx3_ref[...] = dx3b
        da = _dot_nt(dx3b, wd_ref[...])
        dgb = (da * u * sg * (1.0 + g * (1.0 - sg))).astype(BF)
        dub = (da * g * sg).astype(BF)
        dg_ref[...] = dgb
        du_ref[...] = dub
        dh = jnp.dot(dgb, wg_ref[...], preferred_element_type=F32) + jnp.dot(dub, wu_ref[...], preferred_element_type=F32)
        dn, dwn = _rms_bwd(dh, n2, r2, nffn_ref[...])
        acc_ref[2:3, :] += _colsum(dwn)
        dx2_ref[...] = dx3 + dn

    outs = [SDS((L, D), F32), SDS((L, D), BF), SDS((L, FFN), BF), SDS((L, D), BF), SDS((L, FFN), BF),
            SDS((L, FFN), BF), SDS((8, D), F32)]
    wspec = _resident((FFN, D))
    return pl.pallas_call(
        body, grid=(L // tl,), name="ffn_loss",
        in_specs=[_rows(tl, D), _rows(tl, D), _const((1, D)), _const((1, D)), wspec, wspec, wspec],
        out_specs=[_rows(tl, D), _rows(tl, D), _rows(tl, FFN), _rows(tl, D), _rows(tl, FFN), _rows(tl, FFN),
                   _const((8, D))],
        out_shape=outs, compiler_params=_cparams())(x2, tgt, nffn, nfin, wgt, wut, wd)


def _attn_bwd(dx2, x1, q, k, v, nxa, wq, wo, w_out, phases=()):
    L = dx2.shape[0]
    tl = min(TL, L)
    scale = XA_HD ** -0.5

    def body(dx2_ref, x1_ref, q_ref, k_ref, v_ref, nxa_ref, wq_ref, wo_ref, wout_ref,
             dx1_ref, dya_ref, dob_ref, dq_ref, dk_ref, dv_ref, acc_ref):
        _zero_first(dk_ref, dv_ref, acc_ref)
        dx2v = dx2_ref[...]
        dox = _dot_nt(dx2v, wo_ref[...]).astype(BF)
        qb = q_ref[...]
        heads = [slice(hd * XA_HD, (hd + 1) * XA_HD) for hd in range(XA_HEADS)]
        ps = [_softmax_rows(_dot_nt(qb[:, sl], k_ref[:, sl]) * scale) for sl in heads]
        dps = [_dot_nt(dox[:, sl], v_ref[:, sl]) for sl in heads]
        dss = [(p * (dp - jnp.sum(dp * p, axis=-1, keepdims=True)) * scale).astype(BF) for p, dp in zip(ps, dps)]
        for sl, p, ds in zip(heads, ps, dss):
            dv_ref[:, sl] += _dot_tn(p, dox[:, sl])
            dk_ref[:, sl] += _dot_tn(ds, qb[:, sl])
        dqs = [_dot(ds, k_ref[:, sl]) for sl, ds in zip(heads, dss)]
        dqb = jnp.concatenate(dqs, axis=1).astype(BF)
        dq_ref[...] = dqb
        dh = _dot_nt(dqb, wq_ref[...])
        _, n1, r1 = _rms(x1_ref[...], nxa_ref[...])
        dn, dwn = _rms_bwd(dh, n1, r1, nxa_ref[...])
        acc_ref[0:1, :] += _colsum(dwn)
        dx1 = dx2v + dn
        dx1_ref[...] = dx1
        dx1b = dx1.astype(BF)
        dya_ref[...] = _dot_nt(dx1b, wout_ref[:D, :]).astype(BF)
        dob_ref[...] = _dot_nt(dx1b, wout_ref[D:, :]).astype(BF)

    outs = [SDS((L, D), F32), SDS((L, D), BF), SDS((L, D), BF), SDS((L, D), BF), SDS((MEM_LEN, D), F32),
            SDS((MEM_LEN, D), F32), SDS((8, D), F32)]
    return _call(body, (dx2, x1, q, k, v, nxa, wq, wo, w_out), name="attn_bwd", grid=(L // tl,),
                 in_specs=[_rows(tl, D), _rows(tl, D), _rows(tl, D), _resident((MEM_LEN, D)), _resident((MEM_LEN, D)),
                           _const((1, D)), _resident((D, D)), _resident((D, D)), _resident((2 * D, D))],
                 out_specs=[_rows(tl, D)] * 4 + [_const((MEM_LEN, D)), _const((MEM_LEN, D)), _const((8, D))],
                 out_shape=outs, phases=phases)


def _in_proj_bwd(x, dx1, dz, dxbc, dhq, dhf, dhi, dhg, ddt, nw, wt, part, phases=()):
    tl = min(TL, x.shape[0] // 2)
    L = x.shape[0] // 2
    first = part * (L // tl)
    rows = lambda n: pl.BlockSpec((tl, n), lambda i: (i + first, 0))

    def body(x_ref, dx1_ref, dz_ref, dxbc_ref, dhq_ref, dhf_ref, dhi_ref, dhg_ref, ddt_ref, nw_ref, w_ref,
             gx_ref, acc_ref):
        _zero_first(acc_ref)
        dh = _dot(dz_ref[...], w_ref[Z0:XBC0, :]) + _dot(dxbc_ref[...], w_ref[XBC0:DT0, :]) \
            + _dot(ddt_ref[...], w_ref[DT0:DT0 + 128, :]) + _dot(dhq_ref[...], w_ref[HQ0:HF0, :]) \
            + _dot(dhf_ref[...], w_ref[HF0:HI0, :]) + _dot(dhi_ref[...], w_ref[HI0:HG0, :]) \
            + _dot(dhg_ref[...], w_ref[HG0:N_IN, :])
        _, n, r = _rms(x_ref[...], nw_ref[...])
        dn, dwn = _rms_bwd(dh, n, r, nw_ref[...])
        acc_ref[0:1, :] += _colsum(dwn)
        gx_ref[...] = dx1_ref[...] + dn

    return _call(
        body, (x, dx1, dz, dxbc, dhq, dhf, dhi, dhg, ddt, nw, wt), grid=(L // tl,), name="in_proj_bwd_%d" % part,
        in_specs=[rows(D), rows(D), rows(D), rows(1536), rows(D), rows(D), rows(D), rows(D), rows(128),
                  _const((1, D)), _resident((N_IN, D))],
        out_specs=[_rows(tl, D), _const((8, D))], out_shape=[SDS((L, D), F32), SDS((8, D), F32)], phases=phases)


def _gw_in(h0, dz, dxbc, ddt, dhq, dhf, dhi, dhg, phases=()):
    L = h0.shape[0]
    tl = min(512, L)

    def body(h_ref, dz_ref, dxbc_ref, ddt_ref, dhq_ref, dhf_ref, dhi_ref, dhg_ref, o_ref):
        _zero_first(o_ref)
        hb = h_ref[...]
        o_ref[Z0:XBC0, :] += _dot_tn(dz_ref[...], hb)
        o_ref[XBC0:DT0, :] += _dot_tn(dxbc_ref[...], hb)
        o_ref[DT0:HQ0, :] += _dot_tn(ddt_ref[...], hb)[0:NH_SSD, :]
        o_ref[HQ0:HF0, :] += _dot_tn(dhq_ref[...], hb)
        o_ref[HF0:HI0, :] += _dot_tn(dhf_ref[...], hb)
        o_ref[HI0:HG0, :] += _dot_tn(dhi_ref[...], hb)
        o_ref[HG0:N_IN, :] += _dot_tn(dhg_ref[...], hb)

    return _call(body, (h0, dz, dxbc, ddt, dhq, dhf, dhi, dhg), name="gw_in", grid=(L // tl,),
                 in_specs=[_rows(tl, D), _rows(tl, D), _rows(tl, 1536), _rows(tl, 128), _rows(tl, D), _rows(tl, D),
                           _rows(tl, D), _rows(tl, D)],
                 out_specs=[_const((N_IN, D))], out_shape=[SDS((N_IN, D), F32)], phases=phases)


def _token_tile(L, out_bytes, row_bytes):
    tl = min(2048, L)
    while tl > 256 and out_bytes + 2 * tl * row_bytes > MATMUL_VMEM:
        tl //= 2
    return tl


def _matmul_tn(a, b, name):
    L, M = a.shape
    N = b.shape[1]
    tl = _token_tile(L, 4 * M * N, M * a.dtype.itemsize + N * b.dtype.itemsize)

    def body(a_ref, b_ref, o_ref):
        _zero_first(o_ref)
        o_ref[...] += _dot_tn(a_ref[...], b_ref[...])

    return pl.pallas_call(
        body, grid=(L // tl,), name=name, in_specs=[_rows(tl, M), _rows(tl, N)], out_specs=_const((M, N)),
        out_shape=SDS((M, N), F32), compiler_params=_cparams())(a, b)


def _matmul_tn_pair(a0, a1, b, name):
    L, M = a0.shape
    N = b.shape[1]
    tl = _token_tile(L, 8 * M * N, 2 * M * a0.dtype.itemsize + N * b.dtype.itemsize)

    def body(a0_ref, a1_ref, b_ref, o_ref):
        _zero_first(o_ref)
        bv = b_ref[...].astype(BF)
        o_ref[:M, :] += _dot_tn(a0_ref[...], bv)
        o_ref[M:, :] += _dot_tn(a1_ref[...], bv)

    return pl.pallas_call(
        body, grid=(L // tl,), name=name, in_specs=[_rows(tl, M), _rows(tl, M), _rows(tl, N)],
        out_specs=_const((2 * M, N)), out_shape=SDS((2 * M, N), F32), compiler_params=_cparams())(a0, a1, b)


def _head_expand():
    e = (jnp.right_shift(_iota((128, D), 1), 6) == _iota((128, D), 0)).astype(BF)
    et = (jnp.right_shift(_iota((D, 128), 0), 6) == _iota((D, 128), 1)).astype(BF)
    return e, et


def _conv_shifts(cur, other, up):
    rows = _iota((Q, 1), 0)
    out = []
    for s in (1, 2, 3):
        if up:
            out.append(jnp.where(rows >= Q - s, pltpu.roll(other, Q - s, 0), pltpu.roll(cur, Q - s, 0)))
        else:
            out.append(jnp.where(rows < s, pltpu.roll(other, s, 0), pltpu.roll(cur, s, 0)))
    return out


def _ssd_pre(u, dtr, dtb, alog):
    e, et = _head_expand()
    sgu = _sigmoid(u)
    xc = u * sgu
    lane = _iota((1, 128), 1)
    hmask = (lane < NH_SSD).astype(F32)
    pre = dtr + dtb
    dt = (jnp.maximum(pre, 0.0) + jnp.log(1.0 + jnp.exp(-jnp.abs(pre)))) * hmask
    a_row = -jnp.exp(alog)
    causal = _iota((Q, Q), 1) <= _iota((Q, Q), 0)
    tri = causal.astype(BF)
    acum = _dot_sel(tri, dt * a_row)
    acum_full = _sel_dot(acum, e)
    alast_full = acum_full[Q - 1:Q, :]
    dt_full = _sel_dot(dt, e)
    xs = xc[:, :D]
    return dict(e=e, et=et, sgu=sgu, xs=xs, bm=xc[:, D:D + 256], cm=xc[:, D + 256:], hmask=hmask, pre=pre, dt=dt,
                a_row=a_row, causal=causal, tri=tri, acum=acum, acum_t=acum.T, eA_full=jnp.exp(acum_full),
                dte_full=jnp.exp(alast_full - acum_full), dt_full=dt_full, xdt=xs * dt_full)


def _ssd_decay(pre, hh, cb):
    seg = pre["acum"][:, hh:hh + 1] - pre["acum_t"][hh:hh + 1, :]
    lm = jnp.where(pre["causal"], jnp.exp(jnp.minimum(seg, 0.0)), 0.0)
    return lm, cb * lm


def _ssd_fwd(xbc, dtr, z, conv_w, conv_b, dtb, alog, dskip_full, nw):
    L = xbc.shape[0]
    nc = L // Q

    def chunk(ck, xbc_ref, dtr_ref, z_ref, cw_ref, cb_ref, dtb_ref, alog_ref, dsk_ref, nw_ref,
              ya_ref, y_ref, u_ref, st_ref, prev_ref, s_ref):
        tok = slice(Q * ck, Q * ck + Q)
        xr = xbc_ref[tok, :].astype(F32)
        sh = _conv_shifts(xr, prev_ref[...], up=False)
        u = cb_ref[...] + cw_ref[3:4, :] * xr + cw_ref[2:3, :] * sh[0] + cw_ref[1:2, :] * sh[1] + cw_ref[0:1, :] * sh[2]
        prev_ref[...] = xr
        ub = u.astype(BF)
        u_ref[tok, :] = ub
        pre = _ssd_pre(ub.astype(F32), dtr_ref[tok, :], dtb_ref[...], alog_ref[...])
        lo = _iota((1, 128), 1) < SSD_P
        s_old = s_ref[...]
        st_ref[ck] = s_old
        ys = []
        for g in range(2):
            bg, cg = pre["bm"][:, 128 * g:128 * g + 128], pre["cm"][:, 128 * g:128 * g + 128]
            cb = _dot_nt(cg, bg)
            gs = slice(512 * g, 512 * g + 512)
            yd = []
            for j in range(4 * g, 4 * g + 4):
                xp = pre["xdt"][:, 128 * j:128 * j + 128].astype(BF)
                _, m0 = _ssd_decay(pre, 2 * j, cb)
                _, m1 = _ssd_decay(pre, 2 * j + 1, cb)
                yd.append(jnp.where(lo, _dot(m0, xp), _dot(m1, xp)))
            yoff = _dot_nt(cg, s_old[gs, :]) * pre["eA_full"][:, gs]
            ys.append(jnp.concatenate(yd, axis=1) + yoff)
            st = _dot_tn((pre["xdt"] * pre["dte_full"])[:, gs], bg)
            cdcol = jnp.exp(_dot_sel(pre["et"][gs, :], pre["acum_t"])[:, Q - 1:Q])
            s_ref[gs, :] = s_old[gs, :] * cdcol + st
        y = jnp.concatenate(ys, axis=1) + dsk_ref[...] * pre["xs"]
        yb = y.astype(BF)
        y_ref[tok, :] = yb
        zf = z_ref[tok, :].astype(F32)
        yz = yb.astype(F32) * zf * _sigmoid(zf)
        outs = []
        for g in range(2):
            gs = slice(512 * g, 512 * g + 512)
            o, _, _ = _rms(yz[:, gs], nw_ref[:, gs])
            outs.append(o)
        ya_ref[tok, :] = jnp.concatenate(outs, axis=1).astype(BF)

    def body(*refs):
        _zero_first(*refs[-2:])
        for ck in range(CH):
            chunk(ck, *refs)

    outs = [SDS((L, D), BF), SDS((L, D), BF), SDS((L, 1536), BF), SDS((nc, D, 128), F32)]
    return pl.pallas_call(
        body, grid=(nc // CH,), name="ssd_fwd",
        in_specs=[_rows(CH * Q, 1536), _rows(CH * Q, 128), _rows(CH * Q, D), _const((4, 1536)), _const((1, 1536)), _const((1, 128)),
                  _const((1, 128)), _const((1, D)), _const((1, D))],
        out_specs=[_rows(CH * Q, D), _rows(CH * Q, D), _rows(CH * Q, 1536),
                   pl.BlockSpec((CH, D, 128), lambda i: (i, 0, 0))],
        out_shape=outs, scratch_shapes=[pltpu.VMEM((Q, 1536), F32), pltpu.VMEM((D, 128), F32)],
        compiler_params=_cparams())(xbc, dtr, z, conv_w, conv_b, dtb, alog, dskip_full, nw)


def _ssd_bwd(dya, y, z, u, xbc, dtr, states, conv_w, dtb, alog, dskip_full, nw):
    L = dya.shape[0]
    nc = L // Q

    def chunk(ck, step, dya_ref, y_ref, z_ref, u_ref, xc_ref, dtr_ref, st_ref, cw_ref, dtb_ref, alog_ref, dsk_ref, nw_ref,
              dz_ref, dxbc_ref, ddt_ref, gconv_ref, ghead_ref, glane_ref, gs_ref, ndu_ref):
        tok = slice(Q * ck, Q * ck + Q)
        uf = u_ref[tok, :].astype(F32)
        pre = _ssd_pre(uf, dtr_ref[tok, :], dtb_ref[...], alog_ref[...])
        e, et, xs, xdt = pre["e"], pre["et"], pre["xs"], pre["xdt"]
        lane = _iota((1, 128), 1)
        lo = lane < SSD_P
        sub = _iota((128, 1), 0)
        zf = z_ref[tok, :].astype(F32)
        sgz = _sigmoid(zf)
        sz = zf * sgz
        yv = y_ref[tok, :].astype(F32)
        yz = yv * sz
        dyav = dya_ref[tok, :].astype(F32)
        dyz, dnw = [], []
        for g in range(2):
            gs = slice(512 * g, 512 * g + 512)
            _, n, r = _rms(yz[:, gs], nw_ref[:, gs])
            dv, dw = _rms_bwd(dyav[:, gs], n, r, nw_ref[:, gs])
            dyz.append(dv)
            dnw.append(dw)
        dyz = jnp.concatenate(dyz, axis=1)
        glane_ref[1:2, :] += _colsum(jnp.concatenate(dnw, axis=1))
        dy = dyz * sz
        dz_ref[tok, :] = (dyz * yv * sgz * (1.0 + zf * (1.0 - sgz))).astype(BF)
        glane_ref[0:1, :] += _colsum(dy * xs)
        dxs = dsk_ref[...] * dy

        s_in = st_ref[ck]
        gst = gs_ref[...]
        gy = dy * pre["eA_full"]
        xdte = xdt * pre["dte_full"]
        dacum = jnp.zeros((Q, 128), F32)
        dacum_t = jnp.zeros((128, Q), F32)
        dxdt, dacum_full, ddte_full, dbs, dcs = [], [], [], [], []
        for g in range(2):
            gs = slice(512 * g, 512 * g + 512)
            bg, cg = pre["bm"][:, 128 * g:128 * g + 128], pre["cm"][:, 128 * g:128 * g + 128]
            sg_, dg_ = s_in[gs, :], gst[gs, :]
            yoff = _dot_nt(cg, sg_) * pre["eA_full"][:, gs]
            dc = _dot(gy[:, gs], sg_)
            dsin = _dot_tn(gy[:, gs], cg)
            dacum_full.append(dy[:, gs] * yoff)
            tg = _dot_nt(bg, dg_)
            ddte_full.append(tg * xdt[:, gs])
            db = _dot(xdte[:, gs], dg_)
            cb = _dot_nt(cg, bg)
            dcb = jnp.zeros((Q, Q), F32)
            dxg = []
            for j in range(4 * g, 4 * g + 4):
                xp = xdt[:, 128 * j:128 * j + 128].astype(BF)
                dyp = dy[:, 128 * j:128 * j + 128]
                dxp = jnp.zeros((Q, 128), F32)
                for idx in range(2):
                    hh = 2 * j + idx
                    lm, m = _ssd_decay(pre, hh, cb)
                    dym = jnp.where(lo if idx == 0 else jnp.logical_not(lo), dyp, 0.0).astype(BF)
                    dm = jnp.where(pre["causal"], _dot_nt(dym, xp), 0.0)
                    w = dm * m
                    dacum = dacum + jnp.where(lane == hh, jnp.sum(w, axis=1, keepdims=True), 0.0)
                    dacum_t = dacum_t + jnp.where(sub == hh, jnp.sum(w, axis=0, keepdims=True), 0.0)
                    dcb = dcb + dm * lm
                    dxp = dxp + _dot_tn(m, dym)
                dxg.append(dxp)
            dxdt.append(jnp.concatenate(dxg, axis=1) + tg * pre["dte_full"][:, gs])
            dcs.append(dc + _dot(dcb, bg))
            dbs.append(db + _dot_tn(dcb, cg))
            cdcol = jnp.exp(_dot_sel(et[gs, :], pre["acum_t"])[:, Q - 1:Q])
            gs_ref[gs, :] = dsin + dg_ * cdcol
        dxdt = jnp.concatenate(dxdt, axis=1)
        dacum = dacum + _sel_dot(jnp.concatenate(dacum_full, axis=1), et, 2) - dacum_t.T
        alast = pre["acum"][Q - 1:Q, :]
        dte = jnp.exp(alast - pre["acum"])
        ddte = _sel_dot(jnp.concatenate(ddte_full, axis=1), et, 2) * dte
        dacum = dacum - ddte
        dcd_col = jnp.sum(_dot_sel(e, gst * s_in, 2), axis=1, keepdims=True)
        dcd_row = jnp.broadcast_to(dcd_col, (128, 128)).T[0:1, :]
        dalast = _colsum(ddte) + dcd_row * jnp.exp(alast)
        dacum = dacum + jnp.where(_iota((Q, 1), 0) == Q - 1, dalast, 0.0)
        ddt = _sel_dot(dxdt * xs, et, 2)
        dxs = dxs + dxdt * pre["dt_full"]
        dda = _dot_sel((_iota((Q, Q), 1) >= _iota((Q, Q), 0)).astype(BF), dacum)
        ddt = ddt + dda * pre["a_row"]
        ghead_ref[1:2, :] += _colsum(dda * pre["dt"])
        ddtr = ddt * _sigmoid(pre["pre"]) * pre["hmask"]
        ghead_ref[0:1, :] += _colsum(ddtr)
        ddt_ref[tok, :] = ddtr

        dxc = jnp.concatenate([dxs] + dbs + dcs, axis=1)
        sgu = pre["sgu"]
        du = dxc * sgu * (1.0 + uf * (1.0 - sgu))
        shu = _conv_shifts(du, ndu_ref[...], up=True)
        dxr = cw_ref[3:4, :] * du + cw_ref[2:3, :] * shu[0] + cw_ref[1:2, :] * shu[1] + cw_ref[0:1, :] * shu[2]
        ndu_ref[...] = du
        dxbc_ref[tok, :] = dxr.astype(BF)
        xr = xc_ref[tok, :].astype(F32)
        gconv_ref[3:4, :] += _colsum(du * xr)
        gconv_ref[2:3, :] += _colsum(shu[0] * xr)
        gconv_ref[1:2, :] += _colsum(shu[1] * xr)
        gconv_ref[0:1, :] += _colsum(shu[2] * xr)
        gconv_ref[4:5, :] += _colsum(du)

        @pl.when(jnp.logical_and(step == nc // CH - 1, ck == 0))
        def _():
            ghead_ref[2:3, :] = ghead_ref[1:2, :] * pre["a_row"]
            ghead_ref[3:4, :] = _sel_dot(glane_ref[...], et)[0:1, :]

    def body(*refs):
        _zero_first(*refs[-5:])
        for ck in reversed(range(CH)):
            chunk(ck, pl.program_id(0), *refs)

    rev = lambda i: (nc // CH - 1 - i, 0)
    outs = [SDS((L, D), BF), SDS((L, 1536), BF), SDS((L, 128), F32), SDS((8, 1536), F32), SDS((8, 128), F32),
            SDS((8, D), F32)]
    return pl.pallas_call(
        body, grid=(nc // CH,), name="ssd_bwd",
        in_specs=[pl.BlockSpec((CH * Q, D), rev), pl.BlockSpec((CH * Q, D), rev), pl.BlockSpec((CH * Q, D), rev),
                  pl.BlockSpec((CH * Q, 1536), rev), pl.BlockSpec((CH * Q, 1536), rev),
                  pl.BlockSpec((CH * Q, 128), rev), pl.BlockSpec((CH, D, 128), lambda i: (nc // CH - 1 - i, 0, 0)),
                  _const((4, 1536)), _const((1, 128)), _const((1, 128)), _const((1, D)), _const((1, D))],
        out_specs=[pl.BlockSpec((CH * Q, D), rev), pl.BlockSpec((CH * Q, 1536), rev), pl.BlockSpec((CH * Q, 128), rev),
                   _const((8, 1536)), _const((8, 128)), _const((8, D))],
        out_shape=outs, scratch_shapes=[pltpu.VMEM((D, 128), F32), pltpu.VMEM((Q, 1536), F32)],
        compiler_params=_cparams())(dya, y, z, u, xbc, dtr, states, conv_w, dtb, alog, dskip_full, nw)


def _hg_gates(hq, hf, hlb):
    h0, h1 = hlb[0:1, :], hlb[1:2, :]
    mx = jnp.maximum(h0, h1)
    e0, e1 = jnp.exp(h0 - mx), jnp.exp(h1 - mx)
    lb = e0 / (e0 + e1)
    sg = _sigmoid(hf)
    fg = lb + (1.0 - lb) * sg
    tri = (_iota((Q, Q), 1) <= _iota((Q, Q), 0)).astype(BF)
    return hq * _sigmoid(hq), 1.0 - fg, fg, sg, lb, e1 / (e0 + e1), _dot_sel(tri, jnp.log(fg))


def _hg_intra(b, q, k):
    rowblk = jnp.right_shift(_iota((Q, 1), 0), SUB.bit_length() - 1)
    mids = [b[SUB * i + SUB // 2:SUB * i + SUB // 2 + 1, :] for i in range(NSUB)]
    prevs = [mids[0]] + [b[SUB * i - 1:SUB * i, :] for i in range(1, NSUB)]
    mfull = jnp.concatenate([jnp.broadcast_to(r, (SUB, 128)) for r in mids], axis=0)
    rfull = jnp.concatenate([jnp.broadcast_to(r, (SUB, 128)) for r in prevs], axis=0)
    eqd, ek, eqo = jnp.exp(b - mfull), jnp.exp(mfull - b), jnp.exp(b - rfull)
    qd, qo, khat = q * eqd, q * eqo, k * ek
    rtab = jnp.concatenate(prevs, axis=0)
    djs = [jnp.exp(rtab - mids[j]) for j in range(NSUB)]
    zero = jnp.zeros((SUB, 128), F32)
    cols = []
    for j in range(NSUB):
        pieces = []
        for i in range(NSUB):
            rs = slice(SUB * i, SUB * i + SUB)
            pieces.append(zero if i < j else qd[rs] if i == j else qo[rs] * djs[j][i:i + 1, :])
        cols.append(jnp.concatenate(pieces, axis=0))
    qt = jnp.concatenate(cols, axis=1).astype(BF)
    kt = jnp.concatenate([jnp.where(rowblk == j, khat, 0.0) for j in range(NSUB)], axis=1).astype(BF)
    causal = _iota((Q, Q), 1) <= _iota((Q, Q), 0)
    att = jnp.where(causal, _dot_nt(qt, kt), 0.0)
    return att, qt, kt, (eqd, ek, eqo, djs), causal


def _hg_intra_bwd(dqt, dkt, qt, kt, factors):
    eqd, ek, eqo, djs = factors
    dqd, dqo, dkh, db = [], [], [], []
    for i in range(NSUB):
        rs = slice(SUB * i, SUB * i + SUB)
        diag = slice(128 * i, 128 * i + 128)
        dqd.append(dqt[rs, diag])
        dkh.append(dkt[rs, diag])
        dbi = qt[rs, diag].astype(F32) * dqt[rs, diag] - kt[rs, diag].astype(F32) * dkt[rs, diag]
        acc = jnp.zeros((SUB, 128), F32)
        for j in range(i):
            bl = slice(128 * j, 128 * j + 128)
            acc = acc + dqt[rs, bl] * djs[j][i:i + 1, :]
            dbi = dbi + qt[rs, bl].astype(F32) * dqt[rs, bl]
        dqo.append(acc)
        db.append(dbi)
    cat = lambda t: jnp.concatenate(t, axis=0)
    return cat(dqd) * eqd + cat(dqo) * eqo, cat(dkh) * ek, cat(db)


def _hg_att_exact(b, q, k, b_ref, q_ref, att_t_ref):
    b_ref[...] = b
    q_ref[...] = q
    att_t_ref[...] = jnp.zeros((Q, Q), F32)
    rows, lane = _iota((Q, 1), 0), _iota((1, Q), 1)

    def step(i, carry):
        e = jnp.exp(jnp.minimum(b_ref[pl.ds(i, 1), :] - b, 0.0))
        col = jnp.sum(q_ref[pl.ds(i, 1), :] * k * e, axis=1, keepdims=True)
        att_t_ref[...] = jnp.where(lane == i, jnp.where(rows <= i, col, 0.0), att_t_ref[...])
        return carry

    lax.fori_loop(0, Q, step, 0)
    return att_t_ref[...].T


def _hg_att_exact_bwd(da, b, q, k, b_ref, q_ref, da_t_ref, dq_ref, dk_ref):
    b_ref[...] = b
    q_ref[...] = q
    da_t_ref[...] = da.T
    dk_ref[...] = jnp.zeros((Q, 128), F32)
    lane = _iota((1, Q), 1)

    def step(i, carry):
        e = jnp.exp(jnp.minimum(b_ref[pl.ds(i, 1), :] - b, 0.0))
        g = jnp.sum(jnp.where(lane == i, da_t_ref[...], 0.0), axis=1, keepdims=True) * e
        dq_ref[pl.ds(i, 1), :] = jnp.sum(g * k, axis=0, keepdims=True)
        dk_ref[...] += g * q_ref[pl.ds(i, 1), :]
        return carry

    lax.fori_loop(0, Q, step, 0)
    dq, dk = dq_ref[...], dk_ref[...]
    return dq, dk, q * dq - k * dk


def _hg_fwd(hq, hf, hi, hg, hlb, nw, fast):
    L = hq.shape[0]
    nc = L // Q

    def chunk(exact, ck, hq_ref, hf_ref, hi_ref, hg_ref, hlb_ref, nw_ref, ob_ref, o_ref, st_ref, s_ref, *tmp):
        tok = slice(Q * ck, Q * ck + Q)
        qf, kf, _, _, _, _, bcum = _hg_gates(hq_ref[tok, :].astype(F32), hf_ref[tok, :], hlb_ref[...])
        gate = hg_ref[tok, :].astype(F32)
        heads = [slice(128 * h, 128 * h + 128) for h in range(NH_HG)]
        if exact:
            atts = [_hg_att_exact(bcum[:, sl], qf[:, sl], kf[:, sl], *tmp).astype(BF) for sl in heads]
        else:
            atts = [_hg_intra(bcum[:, sl], qf[:, sl], kf[:, sl])[0].astype(BF) for sl in heads]
        olds = [s_ref[sl, :] for sl in heads]
        outs_ = [_dot(att, hi_ref[tok, sl]) + _dot(qf[:, sl] * jnp.exp(bcum[:, sl]), s)
                 for att, sl, s in zip(atts, heads, olds)]
        for sl, s, o in zip(heads, olds, outs_):
            b, k = bcum[:, sl], kf[:, sl]
            st_ref[ck, sl, :] = s
            blast = b[Q - 1:Q, :]
            s_ref[sl, :] = s * jnp.exp(b.T[:, Q - 1:Q]) + _dot_tn(k * jnp.exp(blast - b), hi_ref[tok, sl])
            ob = o.astype(BF)
            o_ref[tok, sl] = ob
            on, _, _ = _rms(ob.astype(F32), nw_ref[...])
            gt = gate[:, sl]
            ob_ref[tok, sl] = (on * gt * _sigmoid(gt)).astype(BF)

    def run(exact, *refs):
        for ck in range(CH):
            chunk(exact, ck, *refs)

    def body(fast_ref, *refs):
        _zero_first(refs[9])
        pl.when(fast_ref[0] == 1)(lambda: run(False, *refs))
        pl.when(fast_ref[0] != 1)(lambda: run(True, *refs))

    rows = pl.BlockSpec((CH * Q, D), lambda i, f: (i, 0))
    outs = [SDS((L, D), BF), SDS((L, D), BF), SDS((nc, D, 128), F32)]
    grid_spec = pltpu.PrefetchScalarGridSpec(
        num_scalar_prefetch=1, grid=(nc // CH,),
        in_specs=[rows] * 4 + [pl.BlockSpec((2, D), lambda i, f: (0, 0)), pl.BlockSpec((1, 128), lambda i, f: (0, 0))],
        out_specs=[rows, rows, pl.BlockSpec((CH, D, 128), lambda i, f: (i, 0, 0))],
        scratch_shapes=[pltpu.VMEM((D, 128), F32), pltpu.VMEM((Q, 128), F32), pltpu.VMEM((Q, 128), F32),
                        pltpu.VMEM((Q, Q), F32)])
    return pl.pallas_call(body, grid_spec=grid_spec, name="hg_fwd", out_shape=outs,
                          compiler_params=_cparams())(fast, hq, hf, hi, hg, hlb, nw)


def _hg_bwd(dob, o, hq, hf, hi, hg, states, hlb, nw, fast, phases=()):
    L = dob.shape[0]
    nc = L // Q

    def chunk(exact, ck, step, dob_ref, o_ref, hq_ref, hf_ref, hi_ref, hg_ref, st_ref, hlb_ref, nw_ref,
              dhq_ref, dhf_ref, dhi_ref, dhg_ref, acc_ref, gs_ref, *tmp):
        tok = slice(Q * ck, Q * ck + Q)
        hqv = hq_ref[tok, :].astype(F32)
        qf, kf, fg, sg, lb, sm1, bcum = _hg_gates(hqv, hf_ref[tok, :], hlb_ref[...])
        gate = hg_ref[tok, :].astype(F32)
        sgg = _sigmoid(gate)
        nwv = nw_ref[...]
        tri_t = (_iota((Q, Q), 1) >= _iota((Q, Q), 0)).astype(BF)
        ones8 = jnp.ones((8, 128), BF)
        heads = [slice(128 * h, 128 * h + 128) for h in range(NH_HG)]
        row_last = _iota((Q, 1), 0) == Q - 1
        dobs, dnws = [], []
        for sl in heads:
            gt, sgt = gate[:, sl], sgg[:, sl]
            _, n, r = _rms(o_ref[tok, sl].astype(F32), nwv)
            dobv = dob_ref[tok, sl].astype(F32)
            dhg_ref[tok, sl] = (dobv * n * nwv * sgt * (1.0 + gt * (1.0 - sgt))).astype(BF)
            do, dw = _rms_bwd(dobv * gt * sgt, n, r, nwv)
            dnws.append(_colsum(dw))
            dobs.append(do.astype(BF))
        causal = _iota((Q, Q), 1) <= _iota((Q, Q), 0)
        if exact:
            intra = [(_hg_att_exact(bcum[:, sl], qf[:, sl], kf[:, sl], *tmp[:3]),) for sl in heads]
        else:
            intra = [_hg_intra(bcum[:, sl], qf[:, sl], kf[:, sl]) for sl in heads]
        states = [(st_ref[ck, sl, :], gs_ref[sl, :]) for sl in heads]
        das = [jnp.where(causal, _dot_nt(dob_h, hi_ref[tok, sl]), 0.0) for dob_h, sl in zip(dobs, heads)]
        dqhats = [_dot_nt(dob_h, s) for dob_h, (s, _) in zip(dobs, states)]
        dkhats = [_dot_nt(hi_ref[tok, sl], gst) for sl, (_, gst) in zip(heads, states)]
        if not exact:
            dqts = [jnp.dot(da.astype(BF), it[2], preferred_element_type=F32) for da, it in zip(das, intra)]
            dkts = [lax.dot_general(da.astype(BF), it[1], (((0,), (0,)), ((), ())), preferred_element_type=F32)
                    for da, it in zip(das, intra)]
        dqs, dks, dgls = [], [], []
        for h, sl in enumerate(heads):
            b, q, k = bcum[:, sl], qf[:, sl], kf[:, sl]
            att = intra[h][0]
            s, gst = states[h]
            dob_h, dqhat, dkhat = dobs[h], dqhats[h], dkhats[h]
            eb = jnp.exp(b)
            blast = b[Q - 1:Q, :]
            ekl = jnp.exp(blast - b)
            qhat, khat = q * eb, k * ekl
            dhi_ref[tok, sl] = (_dot_tn(att, dob_h) + _dot(khat, gst)).astype(BF)
            if exact:
                dq_i, dk_i, db = _hg_att_exact_bwd(das[h], b, q, k, *tmp)
            else:
                dq_i, dk_i, db = _hg_intra_bwd(dqts[h], dkts[h], *intra[h][1:4])
            dqs.append(dq_i + dqhat * eb)
            dks.append(dk_i + dkhat * ekl)
            qhat_r, khat_r = qhat.astype(BF).astype(F32), khat.astype(BF).astype(F32)
            decay_row = sum(_dot_nt(ones8, part) for part in _split(gst * s, 2))[0:1, :]
            dblast = _colsum(dkhat * khat_r) + decay_row * jnp.exp(blast)
            dgls.append(db + qhat_r * dqhat - khat_r * dkhat + jnp.where(row_last, dblast, 0.0))
            gs_ref[sl, :] = _dot_tn(qhat, dob_h) + gst * jnp.exp(b.T[:, Q - 1:Q])
        dq, dk, db = (jnp.concatenate(t, axis=1) for t in (dqs, dks, dgls))
        dgl = _dot_sel(tri_t, db, 2)
        sgq = _sigmoid(hqv)
        dhq_ref[tok, :] = (dq * sgq * (1.0 + hqv * (1.0 - sgq))).astype(BF)
        dfg = dgl / fg - dk
        dhf_ref[tok, :] = (dfg * (1.0 - lb) * sg * (1.0 - sg)).astype(BF)
        acc_ref[0:1, :] += _colsum(dfg * (1.0 - sg))
        acc_ref[1:2, :] += jnp.concatenate(dnws, axis=1)

        @pl.when(jnp.logical_and(step == nc // CH - 1, ck == 0))
        def _():
            dlb = acc_ref[0:1, :] * lb * sm1
            acc_ref[2:3, :] = dlb
            acc_ref[3:4, :] = -dlb
            tot = acc_ref[1:2, 0:128]
            for h in range(1, NH_HG):
                tot = tot + acc_ref[1:2, 128 * h:128 * h + 128]
            acc_ref[4:5, 0:128] = tot

    def run(exact, step, *refs):
        for ck in reversed(range(CH)):
            chunk(exact, ck, step, *refs)

    def body(fast_ref, *refs):
        step = pl.program_id(0)
        _zero_first(refs[13], refs[14])
        pl.when(fast_ref[0] == 1)(lambda: run(False, step, *refs))
        pl.when(fast_ref[0] != 1)(lambda: run(True, step, *refs))

    rev = pl.BlockSpec((CH * Q, D), lambda i, f: (nc // CH - 1 - i, 0))
    outs = [SDS((L, D), BF)] * 4 + [SDS((8, D), F32)]
    return _call(
        body, (fast, dob, o, hq, hf, hi, hg, states, hlb, nw), name="hg_bwd", grid=(nc // CH,), prefetch=1,
        in_specs=[rev] * 6 + [pl.BlockSpec((CH, D, 128), lambda i, f: (nc // CH - 1 - i, 0, 0)),
                              pl.BlockSpec((2, D), lambda i, f: (0, 0)), pl.BlockSpec((1, 128), lambda i, f: (0, 0))],
        out_specs=[rev] * 4 + [pl.BlockSpec((8, D), lambda i, f: (0, 0))], out_shape=outs,
        scratch_shapes=[pltpu.VMEM((D, 128), F32), pltpu.VMEM((Q, 128), F32), pltpu.VMEM((Q, 128), F32),
                        pltpu.VMEM((Q, Q), F32), pltpu.VMEM((Q, 128), F32), pltpu.VMEM((Q, 128), F32)], phases=phases)


def _place():
    return lax.axis_index("x"), lax.axis_index("y"), lax.axis_index("c")


def _phase_io(phase):
    kind, arrays, halves = phase
    n = len(arrays)
    dma = pltpu.SemaphoreType.DMA
    if kind == "gather":
        outs = [SDS((8,) + a.shape if hc is None else (4,) + a.shape, a.dtype) for a, hc in zip(arrays, halves)]
        return outs, [dma((7 * n,)), dma((7 * n,)), dma((n,))], {}
    if kind == "sibling":
        return [SDS((4, g.shape[1], hc), g.dtype) for g, hc in zip(arrays, halves)], [dma((n,)), dma((n,))], {}
    if kind == "chips":
        return [SDS((3,) + p.shape[1:], p.dtype) for p in arrays], [dma((3 * n,)), dma((3 * n,))], {}
    assert kind == "swap"
    return [SDS(b.shape, b.dtype) for b in arrays], [dma((n,)), dma((n,))], {a: a for a in range(n)}


def _gather_events(ins, outs, sems, halves):
    send_sems, recv_sems, local_sems = sems
    n = len(ins)

    def parts(a):
        x, y, c = _place()
        hc = halves[a]
        me, sibling = (x, y, c), (x, y, 1 - c)
        chips = [(1 - x, y), (x, 1 - y), (1 - x, 1 - y)]

        def slot(p):
            if hc is None:
                return outs[a].at[4 * p[0] + 2 * p[1] + p[2]]
            return outs[a].at[2 * p[0] + p[1], :, pl.ds(p[2] * hc, hc)]

        own = ins[a] if hc is None else ins[a].at[:, pl.ds(c * hc, hc)]

        def copy(k, piece, to, src=None):
            return pltpu.make_async_remote_copy(
                src_ref=slot(piece) if src is None else src, dst_ref=slot(piece),
                send_sem=send_sems.at[7 * a + k], recv_sem=recv_sems.at[7 * a + k], device_id=to, device_id_type=MESH)

        return dict(
            mine=lambda: pltpu.make_async_copy(own, slot(me), local_sems.at[a]),
            starts=lambda: [copy(0, me, sibling, src=own)] + [copy(1 + j, me, (*chip, c), src=own)
                                                               for j, chip in enumerate(chips)],
            arrive=lambda: [copy(1 + j, (*chip, c), me) for j, chip in enumerate(chips)],
            passed=lambda: [copy(4 + j, (*chip, c), sibling) for j, chip in enumerate(chips)],
            from_sibling=lambda: [copy(0, sibling, me)] + [copy(4 + j, (*chip, 1 - c), me)
                                                            for j, chip in enumerate(chips)])

    def first():
        for a in range(n):
            p = parts(a)
            p["mine"]().start()
            for cp in p["starts"]():
                cp.start()

    def mid():
        for a in range(n):
            p = parts(a)
            for cp_in, cp_out in zip(p["arrive"](), p["passed"]()):
                cp_in.wait_recv()
                cp_out.start()

    def last():
        for a in range(n):
            p = parts(a)
            for cp in p["from_sibling"]():
                cp.wait_recv()
            for cp in p["starts"]() + p["passed"]():
                cp.wait_send()
            p["mine"]().wait()

    return dict(first=first, mid=mid, last=last)


def _exchange_events(kind, ins, outs, sems, halves):
    send_sems, recv_sems = sems
    n = len(outs)

    def copies():
        x, y, c = _place()
        if kind == "sibling":
            return [pltpu.make_async_remote_copy(
                src_ref=ins[a].at[:, :, pl.ds((1 - c) * halves[a], halves[a])], dst_ref=outs[a],
                send_sem=send_sems.at[a], recv_sem=recv_sems.at[a], device_id=(x, y, 1 - c), device_id_type=MESH)
                for a in range(n)]
        chips = [(1 - x, y), (x, 1 - y), (1 - x, 1 - y)]
        return [pltpu.make_async_remote_copy(
            src_ref=ins[a].at[2 * px + py], dst_ref=outs[a].at[k], send_sem=send_sems.at[3 * a + k],
            recv_sem=recv_sems.at[3 * a + k], device_id=(px, py, c), device_id_type=MESH)
            for a in range(n) for k, (px, py) in enumerate(chips)]

    def first():
        for cp in copies():
            cp.start()

    def last():
        for cp in copies():
            cp.wait()

    return dict(first=first, last=last)


def _swap_events(outs, sems, halves):
    send_sems, recv_sems = sems
    n = len(outs)

    def copy(a, landing):
        x, y, c = _place()
        cols = lambda which: outs[a].at[:, pl.ds(which * halves[a], halves[a])]
        return pltpu.make_async_remote_copy(
            src_ref=cols(c), dst_ref=cols(1 - c) if landing else cols(c), send_sem=send_sems.at[a],
            recv_sem=recv_sems.at[a], device_id=(x, y, 1 - c), device_id_type=MESH)

    def first():
        for a in range(n):
            copy(a, False).start()

    def last():
        for a in range(n):
            copy(a, True).wait_recv()
        for a in range(n):
            copy(a, False).wait_send()

    return dict(first=first, last=last)


def _phase_events(phase, ins, outs, sems):
    kind, _, halves = phase
    if kind == "gather":
        return _gather_events(ins, outs, sems, halves)
    if kind == "swap":
        return _swap_events(outs, sems, halves)
    return _exchange_events(kind, ins, outs, sems, halves)


def _split_refs(refs, counts):
    out, at = [], 0
    for c in counts:
        out.append(list(refs[at:at + c]))
        at += c
    return out


def _comm_plumbing(phases, first_in, first_out):
    ios = [_phase_io(p) for p in phases]
    arrays = [a for p in phases for a in p[1]]
    out_shape = [o for io in ios for o in io[0]]
    sem_shapes = [s for io in ios for s in io[1]]
    aliases, ai, ao = {}, first_in, first_out
    for p, io in zip(phases, ios):
        aliases.update({ai + k: ao + v for k, v in io[2].items()})
        ai, ao = ai + len(p[1]), ao + len(io[0])

    def events(cins, couts, sems):
        evs = [_phase_events(p, i, o, s) for p, i, o, s in zip(
            phases, _split_refs(cins, [len(p[1]) for p in phases]), _split_refs(couts, [len(io[0]) for io in ios]),
            _split_refs(sems, [len(io[1]) for io in ios]))]

        def run(key):
            for ev in evs:
                if key in ev:
                    ev[key]()

        return {key: (lambda key=key: run(key)) for key in ("first", "mid", "last")}

    def regroup(flat):
        return _split_refs(flat, [len(io[0]) for io in ios])

    return arrays, out_shape, sem_shapes, aliases, events, regroup


def _run_phases(phases, name):
    arrays, out_shape, sem_shapes, aliases, events, regroup = _comm_plumbing(phases, 0, 0)

    def body(*refs):
        cins, couts, sems = _split_refs(refs, [len(arrays), len(out_shape), len(sem_shapes)])
        ev = events(cins, couts, sems)
        for key in ("first", "mid", "last"):
            ev[key]()

    outs = pl.pallas_call(
        body, name=name, in_specs=[ANY] * len(arrays), out_specs=[ANY] * len(out_shape), out_shape=out_shape,
        scratch_shapes=sem_shapes, input_output_aliases=aliases)(*arrays)
    return regroup(outs)


def _call(body, args, *, name, grid, in_specs, out_specs, out_shape, scratch_shapes=(), prefetch=0, phases=(),
          mid_step=None):
    steps = grid[0]
    arrays, c_shape, sem_shapes, aliases, events, regroup = _comm_plumbing(
        phases, prefetch + len(in_specs), len(out_specs))
    counts = [prefetch, len(in_specs), len(arrays), len(out_specs), len(c_shape), len(scratch_shapes), len(sem_shapes)]

    def wrapped(*refs):
        pre, ins, cins, outs, couts, scratch, sems = _split_refs(refs, counts)
        if not phases:
            return body(*pre, *ins, *outs, *scratch)
        step = pl.program_id(0)
        ev = events(cins, couts, sems)
        pl.when(step == 0)(ev["first"])
        body(*pre, *ins, *outs, *scratch)
        pl.when(step == (steps // 2 if mid_step is None else mid_step))(ev["mid"])
        pl.when(step == steps - 1)(ev["last"])

    grid_spec = pltpu.PrefetchScalarGridSpec(
        num_scalar_prefetch=prefetch, grid=grid, in_specs=list(in_specs) + [ANY] * len(arrays),
        out_specs=list(out_specs) + [ANY] * len(c_shape), scratch_shapes=list(scratch_shapes) + sem_shapes)
    outs = pl.pallas_call(
        wrapped, grid_spec=grid_spec, name=name, out_shape=list(out_shape) + c_shape, input_output_aliases=aliases,
        compiler_params=_cparams())(*args, *arrays)
    return list(outs[:len(out_specs)]), regroup(outs[len(out_specs):])


def _tile(rows, cols, nbuf):
    budget = (VMEM_LIMIT // 3) // (2 * nbuf * 4)
    if rows % 8 == 0:
        cands = [t for t in range(8, rows + 1, 8) if rows % t == 0 and t * cols <= budget]
        pref = [t for t in cands if t % 16 == 0]
        return (max(pref) if pref else max(cands) if cands else 8), cols
    cands = [t for t in range(128, cols + 1, 128) if cols % t == 0 and rows * t <= budget]
    return rows, (max(cands) if cands else 128)


def _chip_sum(g, from_sib, place, name):
    _, rows, hc = from_sib.shape
    tr, tc = _tile(rows, hc, 4)
    ni, nj = rows // tr, hc // tc

    def body(p_ref, g_ref, s_ref, hb_ref, own_ref):
        s = g_ref[...] + s_ref[...]
        hb_ref[...] = s.astype(BF)

        @pl.when(pl.program_id(2) == p_ref[1])
        def _():
            own_ref[...] = s

    grid_spec = pltpu.PrefetchScalarGridSpec(
        num_scalar_prefetch=1, grid=(ni, nj, 4),
        in_specs=[pl.BlockSpec((None, tr, tc), lambda i, j, k, p: (k, i, p[0] * nj + j)),
                  pl.BlockSpec((None, tr, tc), lambda i, j, k, p: (k, i, j))],
        out_specs=[pl.BlockSpec((None, tr, tc), lambda i, j, k, p: (k, i, j)),
                   pl.BlockSpec((tr, tc), lambda i, j, k, p: (i, j))])
    return pl.pallas_call(
        body, grid_spec=grid_spec, name=name, out_shape=[SDS((4, rows, hc), BF), SDS((rows, hc), F32)],
        compiler_params=pltpu.CompilerParams(dimension_semantics=("arbitrary",) * 3,
                                             vmem_limit_bytes=VMEM_LIMIT))(place, g, from_sib)


def _total(own, parts, place, name):
    rows, hc = own.shape
    tr, tc = _tile(rows, hc, 5)
    ni, nj = rows // tr, hc // tc

    def body(p_ref, own_ref, parts_ref, o_ref):
        s = own_ref[...]
        for k in range(3):
            s = s + parts_ref[k].astype(F32)
        o_ref[...] = s

    grid_spec = pltpu.PrefetchScalarGridSpec(
        num_scalar_prefetch=1, grid=(ni, nj),
        in_specs=[pl.BlockSpec((tr, tc), lambda i, j, p: (i, j)),
                  pl.BlockSpec((3, tr, tc), lambda i, j, p: (0, i, j))],
        out_specs=pl.BlockSpec((tr, tc), lambda i, j, p: (i, p[0] * nj + j)))
    return pl.pallas_call(
        body, grid_spec=grid_spec, name=name, out_shape=SDS((rows, 2 * hc), F32),
        compiler_params=pltpu.CompilerParams(dimension_semantics=("arbitrary",) * 2,
                                             vmem_limit_bytes=VMEM_LIMIT))(place, own, parts)


def _sum8(parts, name):
    R = parts.shape[1]

    def body(p_ref, o_ref):
        s = p_ref[0]
        for k in range(1, 8):
            s = s + p_ref[k]
        o_ref[...] = s

    return pl.pallas_call(
        body, grid=(1,), name=name, in_specs=[_const((8, R, 128))], out_specs=_const((R, 128)),
        out_shape=SDS((R, 128), F32), compiler_params=_cparams())(parts)


def _adamw(w, g, m, v, name):
    _, R, C = w.shape
    tr, tc = _tile(R, C, 7)
    c1 = 1.0 / (1.0 - ADAM_B1 ** ADAM_STEP)
    c2 = 1.0 / (1.0 - ADAM_B2 ** ADAM_STEP)

    def body(w_ref, g_ref, m_ref, v_ref, d_ref, nm_ref, nv_ref):
        gv = g_ref[...]
        nm = ADAM_B1 * m_ref[...] + (1.0 - ADAM_B1) * gv
        nv = ADAM_B2 * v_ref[...] + (1.0 - ADAM_B2) * gv * gv
        nm_ref[...] = nm
        nv_ref[...] = nv
        d_ref[...] = -ADAM_LR * ((nm * c1) / (jnp.sqrt(nv * c2) + ADAM_EPS) + ADAM_WD * w_ref[...])

    blk3 = pl.BlockSpec((None, tr, tc), lambda i, j: (0, i, j))
    return pl.pallas_call(
        body, grid=(R // tr, C // tc), name=name,
        in_specs=[blk3, pl.BlockSpec((tr, tc), lambda i, j: (i, j)), blk3, blk3], out_specs=[blk3] * 3,
        out_shape=[SDS((1, R, C), F32)] * 3,
        compiler_params=pltpu.CompilerParams(dimension_semantics=("arbitrary",) * 2,
                                             vmem_limit_bytes=VMEM_LIMIT))(w, g, m, v)


def _pack_small(parts):
    rows = []
    for p in parts:
        p = p.reshape(-1)
        rows.append(jnp.pad(p, (0, (-p.shape[0]) % 128)).reshape(-1, 128))
    out = jnp.concatenate(rows, axis=0)
    return jnp.pad(out, ((0, (-out.shape[0]) % 8), (0, 0)))


def _unpack_small(packed, shapes):
    out, row = [], 0
    for shp in shapes:
        n = 1
        for s in shp:
            n *= s
        nr = -(-n // 128)
        out.append(packed[row:row + nr].reshape(-1)[:n].reshape(shp))
        row += nr
    return out


def _pad_lanes(v, n=128):
    return jnp.pad(v, ((0, 0), (0, n - v.shape[1])))


GROUP_FFN = ("ffn_w_gate", "ffn_w_up", "ffn_w_down")
GROUP_ATTN = ("w_out", "xa_wq", "xa_wkv", "xa_wo")


def kernel(x, mem, norm_mix_w, w_in, conv_w, conv_b, dt_bias, a_log, d_skip, ssd_norm_w, hg_lower_bounds, hg_norm_w, w_out, norm_xa_w, norm_mem_w, xa_wq, xa_wkv, xa_wo, norm_ffn_w, ffn_w_gate, ffn_w_up, ffn_w_down, norm_final_w, loss_target, m_norm_mix_w, m_w_in, m_conv_w, m_conv_b, m_dt_bias, m_a_log, m_d_skip, m_ssd_norm_w, m_hg_lower_bounds, m_hg_norm_w, m_w_out, m_norm_xa_w, m_norm_mem_w, m_xa_wq, m_xa_wkv, m_xa_wo, m_norm_ffn_w, m_ffn_w_gate, m_ffn_w_up, m_ffn_w_down, m_norm_final_w, v_norm_mix_w, v_w_in, v_conv_w, v_conv_b, v_dt_bias, v_a_log, v_d_skip, v_ssd_norm_w, v_hg_lower_bounds, v_hg_norm_w, v_w_out, v_norm_xa_w, v_norm_mem_w, v_xa_wq, v_xa_wkv, v_xa_wo, v_norm_ffn_w, v_ffn_w_gate, v_ffn_w_up, v_ffn_w_down, v_norm_final_w):
    w = dict(norm_mix_w=norm_mix_w, w_in=w_in, conv_w=conv_w, conv_b=conv_b, dt_bias=dt_bias, a_log=a_log, d_skip=d_skip,
             ssd_norm_w=ssd_norm_w, hg_lower_bounds=hg_lower_bounds, hg_norm_w=hg_norm_w, w_out=w_out,
             norm_xa_w=norm_xa_w, norm_mem_w=norm_mem_w, xa_wq=xa_wq, xa_wkv=xa_wkv, xa_wo=xa_wo, norm_ffn_w=norm_ffn_w,
             ffn_w_gate=ffn_w_gate, ffn_w_up=ffn_w_up, ffn_w_down=ffn_w_down, norm_final_w=norm_final_w)
    m = dict(norm_mix_w=m_norm_mix_w, w_in=m_w_in, conv_w=m_conv_w, conv_b=m_conv_b, dt_bias=m_dt_bias, a_log=m_a_log,
             d_skip=m_d_skip, ssd_norm_w=m_ssd_norm_w, hg_lower_bounds=m_hg_lower_bounds, hg_norm_w=m_hg_norm_w,
             w_out=m_w_out, norm_xa_w=m_norm_xa_w, norm_mem_w=m_norm_mem_w, xa_wq=m_xa_wq, xa_wkv=m_xa_wkv,
             xa_wo=m_xa_wo, norm_ffn_w=m_norm_ffn_w, ffn_w_gate=m_ffn_w_gate, ffn_w_up=m_ffn_w_up,
             ffn_w_down=m_ffn_w_down, norm_final_w=m_norm_final_w)
    v = dict(norm_mix_w=v_norm_mix_w, w_in=v_w_in, conv_w=v_conv_w, conv_b=v_conv_b, dt_bias=v_dt_bias, a_log=v_a_log,
             d_skip=v_d_skip, ssd_norm_w=v_ssd_norm_w, hg_lower_bounds=v_hg_lower_bounds, hg_norm_w=v_hg_norm_w,
             w_out=v_w_out, norm_xa_w=v_norm_xa_w, norm_mem_w=v_norm_mem_w, xa_wq=v_xa_wq, xa_wkv=v_xa_wkv,
             xa_wo=v_xa_wo, norm_ffn_w=v_norm_ffn_w, ffn_w_gate=v_ffn_w_gate, ffn_w_up=v_ffn_w_up,
             ffn_w_down=v_ffn_w_down, norm_final_w=v_norm_final_w)
    xi, yi, ci = _place()
    chip = 2 * xi + yi
    place = jnp.stack([ci, chip]).astype(jnp.int32)

    def shard(t, name):
        return jnp.swapaxes(t[name], 1, 2) if name in TRANSPOSED else t[name]

    wsh = {name: shard(w, name) for name in BIG}
    half = {name: wsh[name].shape[2] // 2 for name in BIG}
    payload = {name: wsh[name][0].astype(BF) for name in BIG}
    ws = {name: w[name] for name in SMALL}
    xs, mems, tgt = x[0], mem[0], loss_target[0]

    def chip_sums(names, grads, from_sib):
        return [_chip_sum(grads[n], s, place, "grads_chip_sum_" + n) for n, s in zip(names, from_sib)]

    def totals(names, sums, others):
        return [_total(own, o, place, "grads_total_" + n) for n, (_, own), o in zip(names, sums, others)]

    ((w_in4, conv_all),) = _run_phases([("gather", [payload["w_in"], conv_w[0]], [half["w_in"], None])], "gather_w_in")
    w_in_t = w_in4.reshape(N_IN, D)
    ws["conv_w"] = conv_all[0::2].transpose(1, 0, 2).reshape(1, 4, 1536)
    rest = [n for n in BIG if n != "w_in"]
    (h0, z, xbc, hq, hf, hi, hg, dtr), (gathered,) = _in_proj(
        xs, ws["norm_mix_w"], w_in_t, phases=[("gather", [payload[n] for n in rest], [half[n] for n in rest])])
    wg = dict(zip(rest, gathered))
    wg_t, wu_t = wg["ffn_w_gate"].reshape(FFN, D), wg["ffn_w_up"].reshape(FFN, D)
    wd = wg["ffn_w_down"].reshape(FFN, D)
    w_out_f = wg["w_out"].reshape(2 * D, D)
    wq, wo = wg["xa_wq"].reshape(D, D), wg["xa_wo"].reshape(D, D)
    dtb, alog = _pad_lanes(ws["dt_bias"]), _pad_lanes(ws["a_log"])
    dskip_full = jnp.repeat(ws["d_skip"], SSD_P, axis=1)
    cw, conv_bias = ws["conv_w"][0], ws["conv_b"]
    hlb = ws["hg_lower_bounds"]
    hg_fast = (jnp.min(jax.nn.softmax(hlb, axis=0)[0]) >= HG_LB_FLOOR).astype(jnp.int32).reshape(1)

    ya, yssd, u, st_ssd = _ssd_fwd(xbc, dtr, z, cw, conv_bias, dtb, alog, dskip_full, ws["ssd_norm_w"])
    ob, ohg, st_hg = _hg_fwd(hq, hf, hi, hg, hlb, ws["hg_norm_w"], hg_fast)
    kmem, vmem = _mem_kv(mems, ws["norm_mem_w"], wg["xa_wkv"])
    x1, x2, hxa, q, ox = _attn_fwd(xs, ya, ob, w_out_f, ws["norm_xa_w"], wq, kmem, vmem, wo)
    nfin = ws["norm_final_w"].reshape(1, D)
    dx2, hffn, act, dx3, dg, du, acc_f = _ffn_loss(x2, tgt, ws["norm_ffn_w"], nfin, wg_t, wu_t, wd)

    gb = {"ffn_w_gate": _matmul_tn(dg, hffn, "gw_gate").reshape(4, FFN // 4, D),
          "ffn_w_up": _matmul_tn(du, hffn, "gw_up").reshape(4, FFN // 4, D),
          "ffn_w_down": _matmul_tn(act, dx3, "gw_down").reshape(4, FFN // 4, D)}
    (dx1, dya, dob, dq, dk, dv, acc_a), (sib_ffn,) = _attn_bwd(
        dx2, x1, q, kmem, vmem, ws["norm_xa_w"], wq, wo, w_out_f,
        phases=[("sibling", [gb[n] for n in GROUP_FFN], [half[n] for n in GROUP_FFN])])
    sums_ffn = chip_sums(GROUP_FFN, gb, sib_ffn)
    g_nmem, gb["xa_wkv"] = _mem_kv_bwd(mems, ws["norm_mem_w"], wg["xa_wkv"], dk, dv)
    gb["w_out"] = _matmul_tn_pair(ya, ob, dx1, "gw_out").reshape(4, D // 2, D)
    gb["xa_wq"] = _matmul_tn(hxa, dq, "gw_q").reshape(4, D // 4, D)
    gb["xa_wo"] = _matmul_tn(ox, dx2, "gw_o").reshape(4, D // 4, D)
    (dhq, dhf, dhi, dhg, acc_h), (others_ffn, sib_attn) = _hg_bwd(
        dob, ohg, hq, hf, hi, hg, st_hg, hlb, ws["hg_norm_w"], hg_fast,
        phases=[("chips", [hb for hb, _ in sums_ffn], None),
                ("sibling", [gb[n] for n in GROUP_ATTN], [half[n] for n in GROUP_ATTN])])
    red_ffn = totals(GROUP_FFN, sums_ffn, others_ffn)
    sums_attn = chip_sums(GROUP_ATTN, gb, sib_attn)
    dz, dxbc, ddt, gconv, ghead, glane = _ssd_bwd(dya, yssd, z, u, xbc, dtr, st_ssd, cw, dtb, alog, dskip_full,
                                                  ws["ssd_norm_w"])
    (gw_in_t,), (g_ffn, others_attn) = _gw_in(
        h0, dz, dxbc, ddt, dhq, dhf, dhi, dhg,
        phases=[("swap", red_ffn, [half[n] for n in GROUP_FFN]), ("chips", [hb for hb, _ in sums_attn], None)])
    red_attn = totals(GROUP_ATTN, sums_attn, others_attn)
    gb["w_in"] = gw_in_t.reshape(4, N_IN // 4, D)
    dproj = (xs, dx1, dz, dxbc, dhq, dhf, dhi, dhg, ddt, ws["norm_mix_w"], w_in_t)
    (gx_a, acc_ia), (g_attn, (sib_in,)) = _in_proj_bwd(
        *dproj, 0, phases=[("swap", red_attn, [half[n] for n in GROUP_ATTN]), ("sibling", [gb["w_in"]], [half["w_in"]])])
    sums_in = chip_sums(("w_in",), gb, [sib_in])
    (gx_b, acc_ib), ((others_in,),) = _in_proj_bwd(*dproj, 1, phases=[("chips", [sums_in[0][0]], None)])
    gx, acc_i = jnp.concatenate([gx_a, gx_b]), acc_ia + acc_ib
    red_in = totals(("w_in",), sums_in, [others_in])

    gs = {
        "norm_mix_w": acc_i[0:1], "conv_w": gconv[0:4][None], "conv_b": gconv[4:5],
        "dt_bias": ghead[0:1, :NH_SSD], "a_log": ghead[2:3, :NH_SSD], "d_skip": ghead[3:4, :NH_SSD],
        "ssd_norm_w": glane[1:2], "hg_lower_bounds": acc_h[2:4], "hg_norm_w": acc_h[4:5, :128],
        "norm_xa_w": acc_a[0:1], "norm_mem_w": g_nmem, "norm_ffn_w": acc_f[2:3], "norm_final_w": acc_f[1],
    }
    loss = (0.5 / D) * jnp.sum(acc_f[0])
    small_parts = [gs[name] for name in SMALL] + [loss.reshape(1)]
    small_shapes = [gs[name].shape for name in SMALL] + [(1,)]
    (g_in,), (packed,) = _run_phases([("swap", red_in, [half["w_in"]]),
                                      ("gather", [_pack_small(small_parts)], [None])], "grads_finish")
    g_big = dict(zip(GROUP_FFN + GROUP_ATTN + ("w_in",), g_ffn + g_attn + [g_in]))
    small = _unpack_small(_sum8(packed, "small_total"), small_shapes)
    g_small = dict(zip(SMALL, small[:-1]))
    loss_all = small[-1][0]
    g_small["conv_w"] = lax.dynamic_slice_in_dim(g_small["conv_w"], chip * 384, 384, 2)

    grads, delta, new_m, new_v = {}, {}, {}, {}
    for name in BIG:
        outs = (g_big[name][None],) + tuple(_adamw(wsh[name], g_big[name], shard(m, name), shard(v, name),
                                                   "adamw_" + name))
        if name in TRANSPOSED:
            outs = tuple(jnp.swapaxes(o, 1, 2) for o in outs)
        grads[name], delta[name], new_m[name], new_v[name] = outs
    shapes = [w[name].shape for name in SMALL]
    packs = [_pack_small([t[name] for name in SMALL]) for t in (w, g_small, m, v)]
    outs = _adamw(packs[0][None], packs[1], packs[2][None], packs[3][None], "adamw_small")
    for name, g_, d_, nm_, nv_ in zip(SMALL, [g_small[n] for n in SMALL], *[_unpack_small(o[0], shapes) for o in outs]):
        grads[name] = g_.reshape(w[name].shape)
        delta[name], new_m[name], new_v[name] = d_, nm_, nv_

    return (loss_all, gx[None], *[grads[n] for n in WEIGHTS], *[delta[n] for n in WEIGHTS],
            *[new_m[n] for n in WEIGHTS], *[new_v[n] for n in WEIGHTS])
```

```python
import jax
import jax.numpy as jnp
from jax import lax
from jax.experimental import pallas as pl
from jax.experimental.pallas import tpu as pltpu

F32 = jnp.float32
BF = jnp.bfloat16
MESH = pl.DeviceIdType.MESH
SDS = jax.ShapeDtypeStruct
ANY = pl.BlockSpec(memory_space=pl.ANY)

D = 1024
EPS = 1e-6
NH_SSD = 16
SSD_P = 64
NH_HG = 8
Q = 128
CH = 2
SUB = 32
NSUB = Q // SUB
HG_LB_FLOOR = 1e-2
XA_HEADS = 4
XA_HD = 256
MEM_LEN = 256
FFN = 2816
TL = 512
TL_FFN = 256
VMEM_LIMIT = 56 << 20
MATMUL_VMEM = 40 << 20

N_IN = 6672
Z0, XBC0, DT0, HQ0, HF0, HI0, HG0 = 0, 1024, 2560, 2576, 3600, 4624, 5648

ADAM_LR, ADAM_B1, ADAM_B2, ADAM_EPS, ADAM_WD, ADAM_STEP = 0.001, 0.9, 0.999, 1e-08, 0.01, 10

BIG = ("w_in", "w_out", "xa_wq", "xa_wkv", "xa_wo", "ffn_w_gate", "ffn_w_up", "ffn_w_down")
TRANSPOSED = ("w_in", "ffn_w_gate", "ffn_w_up")
SMALL = ("norm_mix_w", "conv_w", "conv_b", "dt_bias", "a_log", "d_skip", "ssd_norm_w", "hg_lower_bounds",
         "hg_norm_w", "norm_xa_w", "norm_mem_w", "norm_ffn_w", "norm_final_w")
WEIGHTS = ("norm_mix_w", "w_in", "conv_w", "conv_b", "dt_bias", "a_log", "d_skip", "ssd_norm_w", "hg_lower_bounds",
           "hg_norm_w", "w_out", "norm_xa_w", "norm_mem_w", "xa_wq", "xa_wkv", "xa_wo", "norm_ffn_w", "ffn_w_gate",
           "ffn_w_up", "ffn_w_down", "norm_final_w")


def _cparams():
    return pltpu.CompilerParams(dimension_semantics=("arbitrary",), vmem_limit_bytes=VMEM_LIMIT)


def _const(shape):
    return pl.BlockSpec(shape, lambda i: (0,) * len(shape))


def _resident(shape):
    return pl.BlockSpec(shape, lambda i: (0,) * len(shape), pipeline_mode=pl.Buffered(1))


def _rows(tl, n):
    return pl.BlockSpec((tl, n), lambda i: (i, 0))


def _dot(a, b):
    return jnp.dot(a.astype(BF), b.astype(BF), preferred_element_type=F32)


def _dot_nt(a, b):
    return lax.dot_general(a.astype(BF), b.astype(BF), (((1,), (1,)), ((), ())), preferred_element_type=F32)


def _dot_tn(a, b):
    return lax.dot_general(a.astype(BF), b.astype(BF), (((0,), (0,)), ((), ())), preferred_element_type=F32)


def _split(v, passes):
    parts, rest = [], v
    for p in range(passes):
        hi = rest.astype(BF)
        parts.append(hi)
        if p + 1 < passes:
            rest = rest - hi.astype(F32)
    return parts


def _sel_dot(a, sel, passes=3):
    sb = sel.astype(BF)
    out = None
    for part in _split(a, passes):
        t = jnp.dot(part, sb, preferred_element_type=F32)
        out = t if out is None else out + t
    return out


def _dot_sel(sel, b, passes=3):
    sb = sel.astype(BF)
    out = None
    for part in _split(b, passes):
        t = jnp.dot(sb, part, preferred_element_type=F32)
        out = t if out is None else out + t
    return out


def _iota(shape, dim):
    return lax.broadcasted_iota(jnp.int32, shape, dim)


def _sigmoid(v):
    return 0.5 * jnp.tanh(0.5 * v) + 0.5


def _rms(v, w):
    r = lax.rsqrt(jnp.mean(v * v, axis=-1, keepdims=True) + EPS)
    n = v * r
    return n * w, n, r


def _rms_bwd(dy, n, r, w):
    dn = dy * w
    return r * (dn - n * jnp.mean(dn * n, axis=-1, keepdims=True)), dy * n


def _colsum(v):
    return jnp.sum(v, axis=0, keepdims=True)


def _zero_first(*refs):
    @pl.when(pl.program_id(0) == 0)
    def _():
        for r in refs:
            r[...] = jnp.zeros_like(r)


def _in_proj(x, nw, wt, phases=()):
    L = x.shape[0]
    tl = min(TL, L)

    def body(x_ref, nw_ref, w_ref, h0_ref, z_ref, xbc_ref, hq_ref, hf_ref, hi_ref, hg_ref, dt_ref):
        h, _, _ = _rms(x_ref[...], nw_ref[...])
        hb = h.astype(BF)
        h0_ref[...] = hb

        def proj(a, b):
            return _dot_nt(hb, w_ref[a:b, :])

        z_ref[...] = proj(Z0, XBC0).astype(BF)
        xbc_ref[...] = proj(XBC0, DT0).astype(BF)
        dt_ref[...] = proj(DT0, DT0 + 128)
        hq_ref[...] = proj(HQ0, HF0).astype(BF)
        hf_ref[...] = proj(HF0, HI0)
        hi_ref[...] = proj(HI0, HG0).astype(BF)
        hg_ref[...] = proj(HG0, N_IN).astype(BF)

    outs = [SDS((L, D), BF), SDS((L, D), BF), SDS((L, 1536), BF), SDS((L, D), BF), SDS((L, D), F32),
            SDS((L, D), BF), SDS((L, D), BF), SDS((L, 128), F32)]
    steps = L // tl
    return _call(body, (x, nw, wt), name="in_proj", grid=(steps,),
                 in_specs=[_rows(tl, D), _const((1, D)), _resident((N_IN, D))],
                 out_specs=[_rows(tl, o.shape[1]) for o in outs], out_shape=outs, phases=phases,
                 mid_step=(3 * steps) // 4)


def _mem_kv(mem, nw, wkv4):
    def body(m_ref, nw_ref, w_ref, k_ref, v_ref):
        m, _, _ = _rms(m_ref[...], nw_ref[...])
        mb = m.astype(BF)
        for i in range(2):
            sl = slice(512 * i, 512 * i + 512)
            k_ref[:, sl] = jnp.dot(mb, w_ref[i], preferred_element_type=F32).astype(BF)
            v_ref[:, sl] = jnp.dot(mb, w_ref[2 + i], preferred_element_type=F32).astype(BF)

    outs = [SDS((MEM_LEN, D), BF)] * 2
    return pl.pallas_call(
        body, grid=(1,), name="mem_kv",
        in_specs=[_const((MEM_LEN, D)), _const((1, D)), _const((4, D, 512))],
        out_specs=[_const((MEM_LEN, D))] * 2, out_shape=outs, compiler_params=_cparams())(mem, nw, wkv4)


def _mem_kv_bwd(mem, nw, wkv4, dk, dv):
    def body(m_ref, nw_ref, w_ref, dk_ref, dv_ref, gnw_ref, gw_ref):
        m, n, _ = _rms(m_ref[...], nw_ref[...])
        mb = m.astype(BF)
        dm = jnp.zeros((MEM_LEN, D), F32)
        for i in range(4):
            src = dk_ref if i < 2 else dv_ref
            d = src[:, 512 * (i % 2):512 * (i % 2) + 512].astype(BF)
            gw_ref[i] = _dot_tn(mb, d)
            dm = dm + _dot_nt(d, w_ref[i])
        gnw_ref[...] = _colsum(dm * n)

    return pl.pallas_call(
        body, grid=(1,), name="mem_kv_bwd",
        in_specs=[_const((MEM_LEN, D)), _const((1, D)), _const((4, D, 512)), _const((MEM_LEN, D)), _const((MEM_LEN, D))],
        out_specs=[_const((1, D)), _const((4, D, 512))],
        out_shape=[SDS((1, D), F32), SDS((4, D, 512), F32)], compiler_params=_cparams())(mem, nw, wkv4, dk, dv)


def _softmax_rows(sc):
    e = jnp.exp(sc - jnp.max(sc, axis=-1, keepdims=True))
    return e * (1.0 / jnp.sum(e, axis=-1, keepdims=True))


def _attn_fwd(x, ya, ob, w_out, nxa, wq, k, v, wo):
    L = x.shape[0]
    tl = min(TL, L)
    scale = XA_HD ** -0.5

    def body(x_ref, ya_ref, ob_ref, wout_ref, nxa_ref, wq_ref, k_ref, v_ref, wo_ref,
             x1_ref, x2_ref, hxa_ref, q_ref, ox_ref):
        x1 = x_ref[...] + jnp.dot(ya_ref[...], wout_ref[:D, :], preferred_element_type=F32) \
            + jnp.dot(ob_ref[...], wout_ref[D:, :], preferred_element_type=F32)
        x1_ref[...] = x1
        h, _, _ = _rms(x1, nxa_ref[...])
        hb = h.astype(BF)
        hxa_ref[...] = hb
        qb = jnp.dot(hb, wq_ref[...], preferred_element_type=F32).astype(BF)
        q_ref[...] = qb
        heads = [slice(hd * XA_HD, (hd + 1) * XA_HD) for hd in range(XA_HEADS)]
        ps = [_softmax_rows(_dot_nt(qb[:, sl], k_ref[:, sl]) * scale) for sl in heads]
        oxs = [_dot(p, v_ref[:, sl]) for p, sl in zip(ps, heads)]
        oxb = jnp.concatenate(oxs, axis=1).astype(BF)
        ox_ref[...] = oxb
        x2_ref[...] = x1 + jnp.dot(oxb, wo_ref[...], preferred_element_type=F32)

    outs = [SDS((L, D), F32), SDS((L, D), F32), SDS((L, D), BF), SDS((L, D), BF), SDS((L, D), BF)]
    return pl.pallas_call(
        body, grid=(L // tl,), name="attn_fwd",
        in_specs=[_rows(tl, D), _rows(tl, D), _rows(tl, D), _resident((2 * D, D)), _const((1, D)), _resident((D, D)),
                  _resident((MEM_LEN, D)), _resident((MEM_LEN, D)), _resident((D, D))],
        out_specs=[_rows(tl, D)] * 5, out_shape=outs, compiler_params=_cparams())(x, ya, ob, w_out, nxa, wq, k, v, wo)


def _ffn_loss(x2, tgt, nffn, nfin, wgt, wut, wd):
    L = x2.shape[0]
    tl = min(TL_FFN, L)

    def body(x2_ref, t_ref, nffn_ref, nfin_ref, wg_ref, wu_ref, wd_ref,
             dx2_ref, h_ref, a_ref, dx3_ref, dg_ref, du_ref, acc_ref):
        _zero_first(acc_ref)
        x2v = x2_ref[...]
        h, n2, r2 = _rms(x2v, nffn_ref[...])
        hb = h.astype(BF)
        h_ref[...] = hb
        g = _dot_nt(hb, wg_ref[...])
        u = _dot_nt(hb, wu_ref[...])
        sg = _sigmoid(g)
        ab = (g * sg * u).astype(BF)
        a_ref[...] = ab
        x3 = x2v + jnp.dot(ab, wd_ref[...], preferred_element_type=F32)
        y, n3, r3 = _rms(x3, nfin_ref[...])
        err = y - t_ref[...]
        acc_ref[0:1, :] += _colsum(err * err)
        dx3, dwf = _rms_bwd(err * (1.0 / D), n3, r3, nfin_ref[...])
        acc_ref[1:2, :] += _colsum(dwf)
        dx3b = dx3.astype(BF)
        dx3_ref[...] = dx3b
        da = _dot_nt(dx3b, wd_ref[...])
        dgb = (da * u * sg * (1.0 + g * (1.0 - sg))).astype(BF)
        dub = (da * g * sg).astype(BF)
        dg_ref[...] = dgb
        du_ref[...] = dub
        dh = jnp.dot(dgb, wg_ref[...], preferred_element_type=F32) + jnp.dot(dub, wu_ref[...], preferred_element_type=F32)
        dn, dwn = _rms_bwd(dh, n2, r2, nffn_ref[...])
        acc_ref[2:3, :] += _colsum(dwn)
        dx2_ref[...] = dx3 + dn

    outs = [SDS((L, D), F32), SDS((L, D), BF), SDS((L, FFN), BF), SDS((L, D), BF), SDS((L, FFN), BF),
            SDS((L, FFN), BF), SDS((8, D), F32)]
    wspec = _resident((FFN, D))
    return pl.pallas_call(
        body, grid=(L // tl,), name="ffn_loss",
        in_specs=[_rows(tl, D), _rows(tl, D), _const((1, D)), _const((1, D)), wspec, wspec, wspec],
        out_specs=[_rows(tl, D), _rows(tl, D), _rows(tl, FFN), _rows(tl, D), _rows(tl, FFN), _rows(tl, FFN),
                   _const((8, D))],
        out_shape=outs, compiler_params=_cparams())(x2, tgt, nffn, nfin, wgt, wut, wd)


def _attn_bwd(dx2, x1, q, k, v, nxa, wq, wo, w_out, phases=()):
    L = dx2.shape[0]
    tl = min(TL, L)
    scale = XA_HD ** -0.5

    def body(dx2_ref, x1_ref, q_ref, k_ref, v_ref, nxa_ref, wq_ref, wo_ref, wout_ref,
             dx1_ref, dya_ref, dob_ref, dq_ref, dk_ref, dv_ref, acc_ref):
        _zero_first(dk_ref, dv_ref, acc_ref)
        dx2v = dx2_ref[...]
        dox = _dot_nt(dx2v, wo_ref[...]).astype(BF)
        qb = q_ref[...]
        heads = [slice(hd * XA_HD, (hd + 1) * XA_HD) for hd in range(XA_HEADS)]
        ps = [_softmax_rows(_dot_nt(qb[:, sl], k_ref[:, sl]) * scale) for sl in heads]
        dps = [_dot_nt(dox[:, sl], v_ref[:, sl]) for sl in heads]
        dss = [(p * (dp - jnp.sum(dp * p, axis=-1, keepdims=True)) * scale).astype(BF) for p, dp in zip(ps, dps)]
        for sl, p, ds in zip(heads, ps, dss):
            dv_ref[:, sl] += _dot_tn(p, dox[:, sl])
            dk_ref[:, sl] += _dot_tn(ds, qb[:, sl])
        dqs = [_dot(ds, k_ref[:, sl]) for sl, ds in zip(heads, dss)]
        dqb = jnp.concatenate(dqs, axis=1).astype(BF)
        dq_ref[...] = dqb
        dh = _dot_nt(dqb, wq_ref[...])
        _, n1, r1 = _rms(x1_ref[...], nxa_ref[...])
        dn, dwn = _rms_bwd(dh, n1, r1, nxa_ref[...])
        acc_ref[0:1, :] += _colsum(dwn)
        dx1 = dx2v + dn
        dx1_ref[...] = dx1
        dx1b = dx1.astype(BF)
        dya_ref[...] = _dot_nt(dx1b, wout_ref[:D, :]).astype(BF)
        dob_ref[...] = _dot_nt(dx1b, wout_ref[D:, :]).astype(BF)

    outs = [SDS((L, D), F32), SDS((L, D), BF), SDS((L, D), BF), SDS((L, D), BF), SDS((MEM_LEN, D), F32),
            SDS((MEM_LEN, D), F32), SDS((8, D), F32)]
    return _call(body, (dx2, x1, q, k, v, nxa, wq, wo, w_out), name="attn_bwd", grid=(L // tl,),
                 in_specs=[_rows(tl, D), _rows(tl, D), _rows(tl, D), _resident((MEM_LEN, D)), _resident((MEM_LEN, D)),
                           _const((1, D)), _resident((D, D)), _resident((D, D)), _resident((2 * D, D))],
                 out_specs=[_rows(tl, D)] * 4 + [_const((MEM_LEN, D)), _const((MEM_LEN, D)), _const((8, D))],
                 out_shape=outs, phases=phases)


IN_BWD_PARTS = ((0, 1), (1, 4), (4, 8))


def _in_proj_bwd(x, dx1, dz, dxbc, dhq, dhf, dhi, dhg, ddt, nw, wt, part, done=(), phases=()):
    tl = min(TL, x.shape[0] // 8)
    eighth = x.shape[0] // 8 // tl
    first, steps = IN_BWD_PARTS[part][0] * eighth, (IN_BWD_PARTS[part][1] - IN_BWD_PARTS[part][0]) * eighth
    L = steps * tl
    rows = lambda n: pl.BlockSpec((tl, n), lambda i: (i + first, 0))
    starts = [sum(d.shape[0] for d in done[:k]) // tl for k in range(len(done))]
    assert sum(d.shape[0] for d in done) in (0, L)

    def body(x_ref, dx1_ref, dz_ref, dxbc_ref, dhq_ref, dhf_ref, dhi_ref, dhg_ref, ddt_ref, nw_ref, w_ref, *rest):
        done_refs, (gx_ref, acc_ref) = rest[:-2], rest[-2:]
        step = pl.program_id(0)
        _zero_first(acc_ref)
        dh = _dot(dz_ref[...], w_ref[Z0:XBC0, :]) + _dot(dxbc_ref[...], w_ref[XBC0:DT0, :]) \
            + _dot(ddt_ref[...], w_ref[DT0:DT0 + 128, :]) + _dot(dhq_ref[...], w_ref[HQ0:HF0, :]) \
            + _dot(dhf_ref[...], w_ref[HF0:HI0, :]) + _dot(dhi_ref[...], w_ref[HI0:HG0, :]) \
            + _dot(dhg_ref[...], w_ref[HG0:N_IN, :])
        _, n, r = _rms(x_ref[...], nw_ref[...])
        dn, dwn = _rms_bwd(dh, n, r, nw_ref[...])
        acc_ref[0:1, :] += _colsum(dwn)
        if not done:
            gx_ref[...] = dx1_ref[...] + dn
            return
        gx_ref[1] = dx1_ref[...] + dn
        for ref, start, piece in zip(done_refs, starts, done):
            @pl.when(jnp.logical_and(step >= start, step < start + piece.shape[0] // tl))
            def _(ref=ref):
                gx_ref[0] = ref[...]

    def piece_spec(start, piece):
        return pl.BlockSpec((tl, D), lambda i: (jnp.clip(i - start, 0, piece.shape[0] // tl - 1), 0))

    gx_spec, gx_shape = (pl.BlockSpec((2, tl, D), lambda i: (0, i, 0)), (2, L, D)) if done else (_rows(tl, D), (L, D))
    return _call(
        body, (x, dx1, dz, dxbc, dhq, dhf, dhi, dhg, ddt, nw, wt, *done), grid=(steps,), name="in_proj_bwd_%d" % part,
        in_specs=[rows(D), rows(D), rows(D), rows(1536), rows(D), rows(D), rows(D), rows(D), rows(128),
                  _const((1, D)), _resident((N_IN, D))] + [piece_spec(s, d) for s, d in zip(starts, done)],
        out_specs=[gx_spec, _const((8, D))], out_shape=[SDS(gx_shape, F32), SDS((8, D), F32)], phases=phases)


def _gw_in(h0, dz, dxbc, ddt, dhq, dhf, dhi, dhg, phases=()):
    L = h0.shape[0]
    tl = min(512, L)

    def body(h_ref, dz_ref, dxbc_ref, ddt_ref, dhq_ref, dhf_ref, dhi_ref, dhg_ref, o_ref):
        _zero_first(o_ref)
        hb = h_ref[...]
        o_ref[Z0:XBC0, :] += _dot_tn(dz_ref[...], hb)
        o_ref[XBC0:DT0, :] += _dot_tn(dxbc_ref[...], hb)
        o_ref[DT0:HQ0, :] += _dot_tn(ddt_ref[...], hb)[0:NH_SSD, :]
        o_ref[HQ0:HF0, :] += _dot_tn(dhq_ref[...], hb)
        o_ref[HF0:HI0, :] += _dot_tn(dhf_ref[...], hb)
        o_ref[HI0:HG0, :] += _dot_tn(dhi_ref[...], hb)
        o_ref[HG0:N_IN, :] += _dot_tn(dhg_ref[...], hb)

    return _call(body, (h0, dz, dxbc, ddt, dhq, dhf, dhi, dhg), name="gw_in", grid=(L // tl,),
                 in_specs=[_rows(tl, D), _rows(tl, D), _rows(tl, 1536), _rows(tl, 128), _rows(tl, D), _rows(tl, D),
                           _rows(tl, D), _rows(tl, D)],
                 out_specs=[_const((N_IN, D))], out_shape=[SDS((N_IN, D), F32)], phases=phases)


def _token_tile(L, out_bytes, row_bytes):
    tl = min(2048, L)
    while tl > 256 and out_bytes + 2 * tl * row_bytes > MATMUL_VMEM:
        tl //= 2
    return tl


def _matmul_tn(a, b, name):
    L, M = a.shape
    N = b.shape[1]
    tl = _token_tile(L, 4 * M * N, M * a.dtype.itemsize + N * b.dtype.itemsize)

    def body(a_ref, b_ref, o_ref):
        _zero_first(o_ref)
        o_ref[...] += _dot_tn(a_ref[...], b_ref[...])

    return pl.pallas_call(
        body, grid=(L // tl,), name=name, in_specs=[_rows(tl, M), _rows(tl, N)], out_specs=_const((M, N)),
        out_shape=SDS((M, N), F32), compiler_params=_cparams())(a, b)


def _matmul_tn_pair(a0, a1, b, name):
    L, M = a0.shape
    N = b.shape[1]
    tl = _token_tile(L, 8 * M * N, 2 * M * a0.dtype.itemsize + N * b.dtype.itemsize)

    def body(a0_ref, a1_ref, b_ref, o_ref):
        _zero_first(o_ref)
        bv = b_ref[...].astype(BF)
        o_ref[:M, :] += _dot_tn(a0_ref[...], bv)
        o_ref[M:, :] += _dot_tn(a1_ref[...], bv)

    return pl.pallas_call(
        body, grid=(L // tl,), name=name, in_specs=[_rows(tl, M), _rows(tl, M), _rows(tl, N)],
        out_specs=_const((2 * M, N)), out_shape=SDS((2 * M, N), F32), compiler_params=_cparams())(a0, a1, b)


def _head_expand():
    e = (jnp.right_shift(_iota((128, D), 1), 6) == _iota((128, D), 0)).astype(BF)
    et = (jnp.right_shift(_iota((D, 128), 0), 6) == _iota((D, 128), 1)).astype(BF)
    return e, et


def _conv_shifts(cur, other, up):
    rows = _iota((Q, 1), 0)
    out = []
    for s in (1, 2, 3):
        if up:
            out.append(jnp.where(rows >= Q - s, pltpu.roll(other, Q - s, 0), pltpu.roll(cur, Q - s, 0)))
        else:
            out.append(jnp.where(rows < s, pltpu.roll(other, s, 0), pltpu.roll(cur, s, 0)))
    return out


def _ssd_pre(u, dtr, dtb, alog):
    e, et = _head_expand()
    sgu = _sigmoid(u)
    xc = u * sgu
    lane = _iota((1, 128), 1)
    hmask = (lane < NH_SSD).astype(F32)
    pre = dtr + dtb
    dt = (jnp.maximum(pre, 0.0) + jnp.log(1.0 + jnp.exp(-jnp.abs(pre)))) * hmask
    a_row = -jnp.exp(alog)
    causal = _iota((Q, Q), 1) <= _iota((Q, Q), 0)
    tri = causal.astype(BF)
    acum = _dot_sel(tri, dt * a_row)
    acum_full = _sel_dot(acum, e)
    alast_full = acum_full[Q - 1:Q, :]
    dt_full = _sel_dot(dt, e)
    xs = xc[:, :D]
    return dict(e=e, et=et, sgu=sgu, xs=xs, bm=xc[:, D:D + 256], cm=xc[:, D + 256:], hmask=hmask, pre=pre, dt=dt,
                a_row=a_row, causal=causal, tri=tri, acum=acum, acum_t=acum.T, eA_full=jnp.exp(acum_full),
                dte_full=jnp.exp(alast_full - acum_full), dt_full=dt_full, xdt=xs * dt_full)


def _ssd_decay(pre, hh, cb):
    seg = pre["acum"][:, hh:hh + 1] - pre["acum_t"][hh:hh + 1, :]
    lm = jnp.where(pre["causal"], jnp.exp(jnp.minimum(seg, 0.0)), 0.0)
    return lm, cb * lm


def _ssd_fwd(xbc, dtr, z, conv_w, conv_b, dtb, alog, dskip_full, nw):
    L = xbc.shape[0]
    nc = L // Q

    def chunk(ck, xbc_ref, dtr_ref, z_ref, cw_ref, cb_ref, dtb_ref, alog_ref, dsk_ref, nw_ref,
              ya_ref, y_ref, u_ref, st_ref, prev_ref, s_ref):
        tok = slice(Q * ck, Q * ck + Q)
        xr = xbc_ref[tok, :].astype(F32)
        sh = _conv_shifts(xr, prev_ref[...], up=False)
        u = cb_ref[...] + cw_ref[3:4, :] * xr + cw_ref[2:3, :] * sh[0] + cw_ref[1:2, :] * sh[1] + cw_ref[0:1, :] * sh[2]
        prev_ref[...] = xr
        ub = u.astype(BF)
        u_ref[tok, :] = ub
        pre = _ssd_pre(ub.astype(F32), dtr_ref[tok, :], dtb_ref[...], alog_ref[...])
        lo = _iota((1, 128), 1) < SSD_P
        s_old = s_ref[...]
        st_ref[ck] = s_old
        ys = []
        for g in range(2):
            bg, cg = pre["bm"][:, 128 * g:128 * g + 128], pre["cm"][:, 128 * g:128 * g + 128]
            cb = _dot_nt(cg, bg)
            gs = slice(512 * g, 512 * g + 512)
            yd = []
            for j in range(4 * g, 4 * g + 4):
                xp = pre["xdt"][:, 128 * j:128 * j + 128].astype(BF)
                _, m0 = _ssd_decay(pre, 2 * j, cb)
                _, m1 = _ssd_decay(pre, 2 * j + 1, cb)
                yd.append(jnp.where(lo, _dot(m0, xp), _dot(m1, xp)))
            yoff = _dot_nt(cg, s_old[gs, :]) * pre["eA_full"][:, gs]
            ys.append(jnp.concatenate(yd, axis=1) + yoff)
            st = _dot_tn((pre["xdt"] * pre["dte_full"])[:, gs], bg)
            cdcol = jnp.exp(_dot_sel(pre["et"][gs, :], pre["acum_t"])[:, Q - 1:Q])
            s_ref[gs, :] = s_old[gs, :] * cdcol + st
        y = jnp.concatenate(ys, axis=1) + dsk_ref[...] * pre["xs"]
        yb = y.astype(BF)
        y_ref[tok, :] = yb
        zf = z_ref[tok, :].astype(F32)
        yz = yb.astype(F32) * zf * _sigmoid(zf)
        outs = []
        for g in range(2):
            gs = slice(512 * g, 512 * g + 512)
            o, _, _ = _rms(yz[:, gs], nw_ref[:, gs])
            outs.append(o)
        ya_ref[tok, :] = jnp.concatenate(outs, axis=1).astype(BF)

    def body(*refs):
        _zero_first(*refs[-2:])
        for ck in range(CH):
            chunk(ck, *refs)

    outs = [SDS((L, D), BF), SDS((L, D), BF), SDS((L, 1536), BF), SDS((nc, D, 128), F32)]
    return pl.pallas_call(
        body, grid=(nc // CH,), name="ssd_fwd",
        in_specs=[_rows(CH * Q, 1536), _rows(CH * Q, 128), _rows(CH * Q, D), _const((4, 1536)), _const((1, 1536)), _const((1, 128)),
                  _const((1, 128)), _const((1, D)), _const((1, D))],
        out_specs=[_rows(CH * Q, D), _rows(CH * Q, D), _rows(CH * Q, 1536),
                   pl.BlockSpec((CH, D, 128), lambda i: (i, 0, 0))],
        out_shape=outs, scratch_shapes=[pltpu.VMEM((Q, 1536), F32), pltpu.VMEM((D, 128), F32)],
        compiler_params=_cparams())(xbc, dtr, z, conv_w, conv_b, dtb, alog, dskip_full, nw)


def _ssd_bwd(dya, y, z, u, xbc, dtr, states, conv_w, dtb, alog, dskip_full, nw):
    L = dya.shape[0]
    nc = L // Q

    def chunk(ck, step, dya_ref, y_ref, z_ref, u_ref, xc_ref, dtr_ref, st_ref, cw_ref, dtb_ref, alog_ref, dsk_ref, nw_ref,
              dz_ref, dxbc_ref, ddt_ref, gconv_ref, ghead_ref, glane_ref, gs_ref, ndu_ref):
        tok = slice(Q * ck, Q * ck + Q)
        uf = u_ref[tok, :].astype(F32)
        pre = _ssd_pre(uf, dtr_ref[tok, :], dtb_ref[...], alog_ref[...])
        e, et, xs, xdt = pre["e"], pre["et"], pre["xs"], pre["xdt"]
        lane = _iota((1, 128), 1)
        lo = lane < SSD_P
        sub = _iota((128, 1), 0)
        zf = z_ref[tok, :].astype(F32)
        sgz = _sigmoid(zf)
        sz = zf * sgz
        yv = y_ref[tok, :].astype(F32)
        yz = yv * sz
        dyav = dya_ref[tok, :].astype(F32)
        dyz, dnw = [], []
        for g in range(2):
            gs = slice(512 * g, 512 * g + 512)
            _, n, r = _rms(yz[:, gs], nw_ref[:, gs])
            dv, dw = _rms_bwd(dyav[:, gs], n, r, nw_ref[:, gs])
            dyz.append(dv)
            dnw.append(dw)
        dyz = jnp.concatenate(dyz, axis=1)
        glane_ref[1:2, :] += _colsum(jnp.concatenate(dnw, axis=1))
        dy = dyz * sz
        dz_ref[tok, :] = (dyz * yv * sgz * (1.0 + zf * (1.0 - sgz))).astype(BF)
        glane_ref[0:1, :] += _colsum(dy * xs)
        dxs = dsk_ref[...] * dy

        s_in = st_ref[ck]
        gst = gs_ref[...]
        gy = dy * pre["eA_full"]
        xdte = xdt * pre["dte_full"]
        dacum = jnp.zeros((Q, 128), F32)
        dacum_t = jnp.zeros((128, Q), F32)
        dxdt, dacum_full, ddte_full, dbs, dcs = [], [], [], [], []
        for g in range(2):
            gs = slice(512 * g, 512 * g + 512)
            bg, cg = pre["bm"][:, 128 * g:128 * g + 128], pre["cm"][:, 128 * g:128 * g + 128]
            sg_, dg_ = s_in[gs, :], gst[gs, :]
            yoff = _dot_nt(cg, sg_) * pre["eA_full"][:, gs]
            dc = _dot(gy[:, gs], sg_)
            dsin = _dot_tn(gy[:, gs], cg)
            dacum_full.append(dy[:, gs] * yoff)
            tg = _dot_nt(bg, dg_)
            ddte_full.append(tg * xdt[:, gs])
            db = _dot(xdte[:, gs], dg_)
            cb = _dot_nt(cg, bg)
            dcb = jnp.zeros((Q, Q), F32)
            dxg = []
            for j in range(4 * g, 4 * g + 4):
                xp = xdt[:, 128 * j:128 * j + 128].astype(BF)
                dyp = dy[:, 128 * j:128 * j + 128]
                dxp = jnp.zeros((Q, 128), F32)
                for idx in range(2):
                    hh = 2 * j + idx
                    lm, m = _ssd_decay(pre, hh, cb)
                    dym = jnp.where(lo if idx == 0 else jnp.logical_not(lo), dyp, 0.0).astype(BF)
                    dm = jnp.where(pre["causal"], _dot_nt(dym, xp), 0.0)
                    w = dm * m
                    dacum = dacum + jnp.where(lane == hh, jnp.sum(w, axis=1, keepdims=True), 0.0)
                    dacum_t = dacum_t + jnp.where(sub == hh, jnp.sum(w, axis=0, keepdims=True), 0.0)
                    dcb = dcb + dm * lm
                    dxp = dxp + _dot_tn(m, dym)
                dxg.append(dxp)
            dxdt.append(jnp.concatenate(dxg, axis=1) + tg * pre["dte_full"][:, gs])
            dcs.append(dc + _dot(dcb, bg))
            dbs.append(db + _dot_tn(dcb, cg))
            cdcol = jnp.exp(_dot_sel(et[gs, :], pre["acum_t"])[:, Q - 1:Q])
            gs_ref[gs, :] = dsin + dg_ * cdcol
        dxdt = jnp.concatenate(dxdt, axis=1)
        dacum = dacum + _sel_dot(jnp.concatenate(dacum_full, axis=1), et, 2) - dacum_t.T
        alast = pre["acum"][Q - 1:Q, :]
        dte = jnp.exp(alast - pre["acum"])
        ddte = _sel_dot(jnp.concatenate(ddte_full, axis=1), et, 2) * dte
        dacum = dacum - ddte
        dcd_col = jnp.sum(_dot_sel(e, gst * s_in, 2), axis=1, keepdims=True)
        dcd_row = jnp.broadcast_to(dcd_col, (128, 128)).T[0:1, :]
        dalast = _colsum(ddte) + dcd_row * jnp.exp(alast)
        dacum = dacum + jnp.where(_iota((Q, 1), 0) == Q - 1, dalast, 0.0)
        ddt = _sel_dot(dxdt * xs, et, 2)
        dxs = dxs + dxdt * pre["dt_full"]
        dda = _dot_sel((_iota((Q, Q), 1) >= _iota((Q, Q), 0)).astype(BF), dacum)
        ddt = ddt + dda * pre["a_row"]
        ghead_ref[1:2, :] += _colsum(dda * pre["dt"])
        ddtr = ddt * _sigmoid(pre["pre"]) * pre["hmask"]
        ghead_ref[0:1, :] += _colsum(ddtr)
        ddt_ref[tok, :] = ddtr

        dxc = jnp.concatenate([dxs] + dbs + dcs, axis=1)
        sgu = pre["sgu"]
        du = dxc * sgu * (1.0 + uf * (1.0 - sgu))
        shu = _conv_shifts(du, ndu_ref[...], up=True)
        dxr = cw_ref[3:4, :] * du + cw_ref[2:3, :] * shu[0] + cw_ref[1:2, :] * shu[1] + cw_ref[0:1, :] * shu[2]
        ndu_ref[...] = du
        dxbc_ref[tok, :] = dxr.astype(BF)
        xr = xc_ref[tok, :].astype(F32)
        gconv_ref[3:4, :] += _colsum(du * xr)
        gconv_ref[2:3, :] += _colsum(shu[0] * xr)
        gconv_ref[1:2, :] += _colsum(shu[1] * xr)
        gconv_ref[0:1, :] += _colsum(shu[2] * xr)
        gconv_ref[4:5, :] += _colsum(du)

        @pl.when(jnp.logical_and(step == nc // CH - 1, ck == 0))
        def _():
            ghead_ref[2:3, :] = ghead_ref[1:2, :] * pre["a_row"]
            ghead_ref[3:4, :] = _sel_dot(glane_ref[...], et)[0:1, :]

    def body(*refs):
        _zero_first(*refs[-5:])
        for ck in reversed(range(CH)):
            chunk(ck, pl.program_id(0), *refs)

    rev = lambda i: (nc // CH - 1 - i, 0)
    outs = [SDS((L, D), BF), SDS((L, 1536), BF), SDS((L, 128), F32), SDS((8, 1536), F32), SDS((8, 128), F32),
            SDS((8, D), F32)]
    return pl.pallas_call(
        body, grid=(nc // CH,), name="ssd_bwd",
        in_specs=[pl.BlockSpec((CH * Q, D), rev), pl.BlockSpec((CH * Q, D), rev), pl.BlockSpec((CH * Q, D), rev),
                  pl.BlockSpec((CH * Q, 1536), rev), pl.BlockSpec((CH * Q, 1536), rev),
                  pl.BlockSpec((CH * Q, 128), rev), pl.BlockSpec((CH, D, 128), lambda i: (nc // CH - 1 - i, 0, 0)),
                  _const((4, 1536)), _const((1, 128)), _const((1, 128)), _const((1, D)), _const((1, D))],
        out_specs=[pl.BlockSpec((CH * Q, D), rev), pl.BlockSpec((CH * Q, 1536), rev), pl.BlockSpec((CH * Q, 128), rev),
                   _const((8, 1536)), _const((8, 128)), _const((8, D))],
        out_shape=outs, scratch_shapes=[pltpu.VMEM((D, 128), F32), pltpu.VMEM((Q, 1536), F32)],
        compiler_params=_cparams())(dya, y, z, u, xbc, dtr, states, conv_w, dtb, alog, dskip_full, nw)


def _hg_gates(hq, hf, hlb):
    h0, h1 = hlb[0:1, :], hlb[1:2, :]
    mx = jnp.maximum(h0, h1)
    e0, e1 = jnp.exp(h0 - mx), jnp.exp(h1 - mx)
    lb = e0 / (e0 + e1)
    sg = _sigmoid(hf)
    fg = lb + (1.0 - lb) * sg
    tri = (_iota((Q, Q), 1) <= _iota((Q, Q), 0)).astype(BF)
    return hq * _sigmoid(hq), 1.0 - fg, fg, sg, lb, e1 / (e0 + e1), _dot_sel(tri, jnp.log(fg))


def _hg_intra(b, q, k):
    rowblk = jnp.right_shift(_iota((Q, 1), 0), SUB.bit_length() - 1)
    mids = [b[SUB * i + SUB // 2:SUB * i + SUB // 2 + 1, :] for i in range(NSUB)]
    prevs = [mids[0]] + [b[SUB * i - 1:SUB * i, :] for i in range(1, NSUB)]
    mfull = jnp.concatenate([jnp.broadcast_to(r, (SUB, 128)) for r in mids], axis=0)
    rfull = jnp.concatenate([jnp.broadcast_to(r, (SUB, 128)) for r in prevs], axis=0)
    eqd, ek, eqo = jnp.exp(b - mfull), jnp.exp(mfull - b), jnp.exp(b - rfull)
    qd, qo, khat = q * eqd, q * eqo, k * ek
    rtab = jnp.concatenate(prevs, axis=0)
    djs = [jnp.exp(rtab - mids[j]) for j in range(NSUB)]
    zero = jnp.zeros((SUB, 128), F32)
    cols = []
    for j in range(NSUB):
        pieces = []
        for i in range(NSUB):
            rs = slice(SUB * i, SUB * i + SUB)
            pieces.append(zero if i < j else qd[rs] if i == j else qo[rs] * djs[j][i:i + 1, :])
        cols.append(jnp.concatenate(pieces, axis=0))
    qt = jnp.concatenate(cols, axis=1).astype(BF)
    kt = jnp.concatenate([jnp.where(rowblk == j, khat, 0.0) for j in range(NSUB)], axis=1).astype(BF)
    causal = _iota((Q, Q), 1) <= _iota((Q, Q), 0)
    att = jnp.where(causal, _dot_nt(qt, kt), 0.0)
    return att, qt, kt, (eqd, ek, eqo, djs), causal


def _hg_intra_bwd(dqt, dkt, qt, kt, factors):
    eqd, ek, eqo, djs = factors
    dqd, dqo, dkh, db = [], [], [], []
    for i in range(NSUB):
        rs = slice(SUB * i, SUB * i + SUB)
        diag = slice(128 * i, 128 * i + 128)
        dqd.append(dqt[rs, diag])
        dkh.append(dkt[rs, diag])
        dbi = qt[rs, diag].astype(F32) * dqt[rs, diag] - kt[rs, diag].astype(F32) * dkt[rs, diag]
        acc = jnp.zeros((SUB, 128), F32)
        for j in range(i):
            bl = slice(128 * j, 128 * j + 128)
            acc = acc + dqt[rs, bl] * djs[j][i:i + 1, :]
            dbi = dbi + qt[rs, bl].astype(F32) * dqt[rs, bl]
        dqo.append(acc)
        db.append(dbi)
    cat = lambda t: jnp.concatenate(t, axis=0)
    return cat(dqd) * eqd + cat(dqo) * eqo, cat(dkh) * ek, cat(db)


def _hg_att_exact(b, q, k, b_ref, q_ref, att_t_ref):
    b_ref[...] = b
    q_ref[...] = q
    att_t_ref[...] = jnp.zeros((Q, Q), F32)
    rows, lane = _iota((Q, 1), 0), _iota((1, Q), 1)

    def step(i, carry):
        e = jnp.exp(jnp.minimum(b_ref[pl.ds(i, 1), :] - b, 0.0))
        col = jnp.sum(q_ref[pl.ds(i, 1), :] * k * e, axis=1, keepdims=True)
        att_t_ref[...] = jnp.where(lane == i, jnp.where(rows <= i, col, 0.0), att_t_ref[...])
        return carry

    lax.fori_loop(0, Q, step, 0)
    return att_t_ref[...].T


def _hg_att_exact_bwd(da, b, q, k, b_ref, q_ref, da_t_ref, dq_ref, dk_ref):
    b_ref[...] = b
    q_ref[...] = q
    da_t_ref[...] = da.T
    dk_ref[...] = jnp.zeros((Q, 128), F32)
    lane = _iota((1, Q), 1)

    def step(i, carry):
        e = jnp.exp(jnp.minimum(b_ref[pl.ds(i, 1), :] - b, 0.0))
        g = jnp.sum(jnp.where(lane == i, da_t_ref[...], 0.0), axis=1, keepdims=True) * e
        dq_ref[pl.ds(i, 1), :] = jnp.sum(g * k, axis=0, keepdims=True)
        dk_ref[...] += g * q_ref[pl.ds(i, 1), :]
        return carry

    lax.fori_loop(0, Q, step, 0)
    dq, dk = dq_ref[...], dk_ref[...]
    return dq, dk, q * dq - k * dk


def _hg_fwd(hq, hf, hi, hg, hlb, nw, fast):
    L = hq.shape[0]
    nc = L // Q

    def chunk(exact, ck, hq_ref, hf_ref, hi_ref, hg_ref, hlb_ref, nw_ref, ob_ref, o_ref, st_ref, s_ref, *tmp):
        tok = slice(Q * ck, Q * ck + Q)
        qf, kf, _, _, _, _, bcum = _hg_gates(hq_ref[tok, :].astype(F32), hf_ref[tok, :], hlb_ref[...])
        gate = hg_ref[tok, :].astype(F32)
        heads = [slice(128 * h, 128 * h + 128) for h in range(NH_HG)]
        if exact:
            atts = [_hg_att_exact(bcum[:, sl], qf[:, sl], kf[:, sl], *tmp).astype(BF) for sl in heads]
        else:
            atts = [_hg_intra(bcum[:, sl], qf[:, sl], kf[:, sl])[0].astype(BF) for sl in heads]
        olds = [s_ref[sl, :] for sl in heads]
        outs_ = [_dot(att, hi_ref[tok, sl]) + _dot(qf[:, sl] * jnp.exp(bcum[:, sl]), s)
                 for att, sl, s in zip(atts, heads, olds)]
        for sl, s, o in zip(heads, olds, outs_):
            b, k = bcum[:, sl], kf[:, sl]
            st_ref[ck, sl, :] = s
            blast = b[Q - 1:Q, :]
            s_ref[sl, :] = s * jnp.exp(b.T[:, Q - 1:Q]) + _dot_tn(k * jnp.exp(blast - b), hi_ref[tok, sl])
            ob = o.astype(BF)
            o_ref[tok, sl] = ob
            on, _, _ = _rms(ob.astype(F32), nw_ref[...])
            gt = gate[:, sl]
            ob_ref[tok, sl] = (on * gt * _sigmoid(gt)).astype(BF)

    def run(exact, *refs):
        for ck in range(CH):
            chunk(exact, ck, *refs)

    def body(fast_ref, *refs):
        _zero_first(refs[9])
        pl.when(fast_ref[0] == 1)(lambda: run(False, *refs))
        pl.when(fast_ref[0] != 1)(lambda: run(True, *refs))

    rows = pl.BlockSpec((CH * Q, D), lambda i, f: (i, 0))
    outs = [SDS((L, D), BF), SDS((L, D), BF), SDS((nc, D, 128), F32)]
    grid_spec = pltpu.PrefetchScalarGridSpec(
        num_scalar_prefetch=1, grid=(nc // CH,),
        in_specs=[rows] * 4 + [pl.BlockSpec((2, D), lambda i, f: (0, 0)), pl.BlockSpec((1, 128), lambda i, f: (0, 0))],
        out_specs=[rows, rows, pl.BlockSpec((CH, D, 128), lambda i, f: (i, 0, 0))],
        scratch_shapes=[pltpu.VMEM((D, 128), F32), pltpu.VMEM((Q, 128), F32), pltpu.VMEM((Q, 128), F32),
                        pltpu.VMEM((Q, Q), F32)])
    return pl.pallas_call(body, grid_spec=grid_spec, name="hg_fwd", out_shape=outs,
                          compiler_params=_cparams())(fast, hq, hf, hi, hg, hlb, nw)


def _hg_bwd(dob, o, hq, hf, hi, hg, states, hlb, nw, fast, phases=()):
    L = dob.shape[0]
    nc = L // Q

    def chunk(exact, ck, step, dob_ref, o_ref, hq_ref, hf_ref, hi_ref, hg_ref, st_ref, hlb_ref, nw_ref,
              dhq_ref, dhf_ref, dhi_ref, dhg_ref, acc_ref, gs_ref, *tmp):
        tok = slice(Q * ck, Q * ck + Q)
        hqv = hq_ref[tok, :].astype(F32)
        qf, kf, fg, sg, lb, sm1, bcum = _hg_gates(hqv, hf_ref[tok, :], hlb_ref[...])
        gate = hg_ref[tok, :].astype(F32)
        sgg = _sigmoid(gate)
        nwv = nw_ref[...]
        tri_t = (_iota((Q, Q), 1) >= _iota((Q, Q), 0)).astype(BF)
        ones8 = jnp.ones((8, 128), BF)
        heads = [slice(128 * h, 128 * h + 128) for h in range(NH_HG)]
        row_last = _iota((Q, 1), 0) == Q - 1
        dobs, dnws = [], []
        for sl in heads:
            gt, sgt = gate[:, sl], sgg[:, sl]
            _, n, r = _rms(o_ref[tok, sl].astype(F32), nwv)
            dobv = dob_ref[tok, sl].astype(F32)
            dhg_ref[tok, sl] = (dobv * n * nwv * sgt * (1.0 + gt * (1.0 - sgt))).astype(BF)
            do, dw = _rms_bwd(dobv * gt * sgt, n, r, nwv)
            dnws.append(_colsum(dw))
            dobs.append(do.astype(BF))
        causal = _iota((Q, Q), 1) <= _iota((Q, Q), 0)
        if exact:
            intra = [(_hg_att_exact(bcum[:, sl], qf[:, sl], kf[:, sl], *tmp[:3]),) for sl in heads]
        else:
            intra = [_hg_intra(bcum[:, sl], qf[:, sl], kf[:, sl]) for sl in heads]
        states = [(st_ref[ck, sl, :], gs_ref[sl, :]) for sl in heads]
        das = [jnp.where(causal, _dot_nt(dob_h, hi_ref[tok, sl]), 0.0) for dob_h, sl in zip(dobs, heads)]
        dqhats = [_dot_nt(dob_h, s) for dob_h, (s, _) in zip(dobs, states)]
        dkhats = [_dot_nt(hi_ref[tok, sl], gst) for sl, (_, gst) in zip(heads, states)]
        if not exact:
            dqts = [jnp.dot(da.astype(BF), it[2], preferred_element_type=F32) for da, it in zip(das, intra)]
            dkts = [lax.dot_general(da.astype(BF), it[1], (((0,), (0,)), ((), ())), preferred_element_type=F32)
                    for da, it in zip(das, intra)]
        dqs, dks, dgls = [], [], []
        for h, sl in enumerate(heads):
            b, q, k = bcum[:, sl], qf[:, sl], kf[:, sl]
            att = intra[h][0]
            s, gst = states[h]
            dob_h, dqhat, dkhat = dobs[h], dqhats[h], dkhats[h]
            eb = jnp.exp(b)
            blast = b[Q - 1:Q, :]
            ekl = jnp.exp(blast - b)
            qhat, khat = q * eb, k * ekl
            dhi_ref[tok, sl] = (_dot_tn(att, dob_h) + _dot(khat, gst)).astype(BF)
            if exact:
                dq_i, dk_i, db = _hg_att_exact_bwd(das[h], b, q, k, *tmp)
            else:
                dq_i, dk_i, db = _hg_intra_bwd(dqts[h], dkts[h], *intra[h][1:4])
            dqs.append(dq_i + dqhat * eb)
            dks.append(dk_i + dkhat * ekl)
            qhat_r, khat_r = qhat.astype(BF).astype(F32), khat.astype(BF).astype(F32)
            decay_row = sum(_dot_nt(ones8, part) for part in _split(gst * s, 2))[0:1, :]
            dblast = _colsum(dkhat * khat_r) + decay_row * jnp.exp(blast)
            dgls.append(db + qhat_r * dqhat - khat_r * dkhat + jnp.where(row_last, dblast, 0.0))
            gs_ref[sl, :] = _dot_tn(qhat, dob_h) + gst * jnp.exp(b.T[:, Q - 1:Q])
        dq, dk, db = (jnp.concatenate(t, axis=1) for t in (dqs, dks, dgls))
        dgl = _dot_sel(tri_t, db, 2)
        sgq = _sigmoid(hqv)
        dhq_ref[tok, :] = (dq * sgq * (1.0 + hqv * (1.0 - sgq))).astype(BF)
        dfg = dgl / fg - dk
        dhf_ref[tok, :] = (dfg * (1.0 - lb) * sg * (1.0 - sg)).astype(BF)
        acc_ref[0:1, :] += _colsum(dfg * (1.0 - sg))
        acc_ref[1:2, :] += jnp.concatenate(dnws, axis=1)

        @pl.when(jnp.logical_and(step == nc // CH - 1, ck == 0))
        def _():
            dlb = acc_ref[0:1, :] * lb * sm1
            acc_ref[2:3, :] = dlb
            acc_ref[3:4, :] = -dlb
            tot = acc_ref[1:2, 0:128]
            for h in range(1, NH_HG):
                tot = tot + acc_ref[1:2, 128 * h:128 * h + 128]
            acc_ref[4:5, 0:128] = tot

    def run(exact, step, *refs):
        for ck in reversed(range(CH)):
            chunk(exact, ck, step, *refs)

    def body(fast_ref, *refs):
        step = pl.program_id(0)
        _zero_first(refs[13], refs[14])
        pl.when(fast_ref[0] == 1)(lambda: run(False, step, *refs))
        pl.when(fast_ref[0] != 1)(lambda: run(True, step, *refs))

    rev = pl.BlockSpec((CH * Q, D), lambda i, f: (nc // CH - 1 - i, 0))
    outs = [SDS((L, D), BF)] * 4 + [SDS((8, D), F32)]
    return _call(
        body, (fast, dob, o, hq, hf, hi, hg, states, hlb, nw), name="hg_bwd", grid=(nc // CH,), prefetch=1,
        in_specs=[rev] * 6 + [pl.BlockSpec((CH, D, 128), lambda i, f: (nc // CH - 1 - i, 0, 0)),
                              pl.BlockSpec((2, D), lambda i, f: (0, 0)), pl.BlockSpec((1, 128), lambda i, f: (0, 0))],
        out_specs=[rev] * 4 + [pl.BlockSpec((8, D), lambda i, f: (0, 0))], out_shape=outs,
        scratch_shapes=[pltpu.VMEM((D, 128), F32), pltpu.VMEM((Q, 128), F32), pltpu.VMEM((Q, 128), F32),
                        pltpu.VMEM((Q, Q), F32), pltpu.VMEM((Q, 128), F32), pltpu.VMEM((Q, 128), F32)], phases=phases)


def _place():
    return lax.axis_index("x"), lax.axis_index("y"), lax.axis_index("c")


def _phase_io(phase):
    kind, arrays, halves = phase
    n = len(arrays)
    dma = pltpu.SemaphoreType.DMA
    if kind == "gather":
        outs = [SDS((8,) + a.shape if hc is None else (4,) + a.shape, a.dtype) for a, hc in zip(arrays, halves)]
        return outs, [dma((7 * n,)), dma((7 * n,)), dma((n,))], {}
    if kind == "sibling":
        return [SDS((4, g.shape[1], hc), g.dtype) for g, hc in zip(arrays, halves)], [dma((n,)), dma((n,))], {}
    if kind == "chips":
        return [SDS((3,) + p.shape[1:], p.dtype) for p in arrays], [dma((3 * n,)), dma((3 * n,))], {}
    assert kind == "swap"
    return [SDS(b.shape, b.dtype) for b in arrays], [dma((n,)), dma((n,))], {a: a for a in range(n)}


def _gather_events(ins, outs, sems, halves):
    send_sems, recv_sems, local_sems = sems
    n = len(ins)

    def parts(a):
        x, y, c = _place()
        hc = halves[a]
        me, sibling = (x, y, c), (x, y, 1 - c)
        chips = [(1 - x, y), (x, 1 - y), (1 - x, 1 - y)]

        def slot(p):
            if hc is None:
                return outs[a].at[4 * p[0] + 2 * p[1] + p[2]]
            return outs[a].at[2 * p[0] + p[1], :, pl.ds(p[2] * hc, hc)]

        own = ins[a] if hc is None else ins[a].at[:, pl.ds(c * hc, hc)]

        def copy(k, piece, to, src=None):
            return pltpu.make_async_remote_copy(
                src_ref=slot(piece) if src is None else src, dst_ref=slot(piece),
                send_sem=send_sems.at[7 * a + k], recv_sem=recv_sems.at[7 * a + k], device_id=to, device_id_type=MESH)

        return dict(
            mine=lambda: pltpu.make_async_copy(own, slot(me), local_sems.at[a]),
            starts=lambda: [copy(0, me, sibling, src=own)] + [copy(1 + j, me, (*chip, c), src=own)
                                                               for j, chip in enumerate(chips)],
            arrive=lambda: [copy(1 + j, (*chip, c), me) for j, chip in enumerate(chips)],
            passed=lambda: [copy(4 + j, (*chip, c), sibling) for j, chip in enumerate(chips)],
            from_sibling=lambda: [copy(0, sibling, me)] + [copy(4 + j, (*chip, 1 - c), me)
                                                            for j, chip in enumerate(chips)])

    def first():
        for a in range(n):
            p = parts(a)
            p["mine"]().start()
            for cp in p["starts"]():
                cp.start()

    def mid():
        for a in range(n):
            p = parts(a)
            for cp_in, cp_out in zip(p["arrive"](), p["passed"]()):
                cp_in.wait_recv()
                cp_out.start()

    def last():
        for a in range(n):
            p = parts(a)
            for cp in p["from_sibling"]():
                cp.wait_recv()
            for cp in p["starts"]() + p["passed"]():
                cp.wait_send()
            p["mine"]().wait()

    return dict(first=first, mid=mid, last=last)


def _exchange_events(kind, ins, outs, sems, halves):
    send_sems, recv_sems = sems
    n = len(outs)

    def copies():
        x, y, c = _place()
        if kind == "sibling":
            return [pltpu.make_async_remote_copy(
                src_ref=ins[a].at[:, :, pl.ds((1 - c) * halves[a], halves[a])], dst_ref=outs[a],
                send_sem=send_sems.at[a], recv_sem=recv_sems.at[a], device_id=(x, y, 1 - c), device_id_type=MESH)
                for a in range(n)]
        chips = [(1 - x, y), (x, 1 - y), (1 - x, 1 - y)]
        return [pltpu.make_async_remote_copy(
            src_ref=ins[a].at[2 * px + py], dst_ref=outs[a].at[k], send_sem=send_sems.at[3 * a + k],
            recv_sem=recv_sems.at[3 * a + k], device_id=(px, py, c), device_id_type=MESH)
            for a in range(n) for k, (px, py) in enumerate(chips)]

    def first():
        for cp in copies():
            cp.start()

    def last():
        for cp in copies():
            cp.wait()

    return dict(first=first, last=last)


def _swap_events(outs, sems, halves):
    send_sems, recv_sems = sems
    n = len(outs)

    def copy(a, landing):
        x, y, c = _place()
        cols = lambda which: outs[a].at[:, pl.ds(which * halves[a], halves[a])]
        return pltpu.make_async_remote_copy(
            src_ref=cols(c), dst_ref=cols(1 - c) if landing else cols(c), send_sem=send_sems.at[a],
            recv_sem=recv_sems.at[a], device_id=(x, y, 1 - c), device_id_type=MESH)

    def first():
        for a in range(n):
            copy(a, False).start()

    def last():
        for a in range(n):
            copy(a, True).wait_recv()
        for a in range(n):
            copy(a, False).wait_send()

    return dict(first=first, last=last)


def _phase_events(phase, ins, outs, sems):
    kind, _, halves = phase
    if kind == "gather":
        return _gather_events(ins, outs, sems, halves)
    if kind == "swap":
        return _swap_events(outs, sems, halves)
    return _exchange_events(kind, ins, outs, sems, halves)


def _split_refs(refs, counts):
    out, at = [], 0
    for c in counts:
        out.append(list(refs[at:at + c]))
        at += c
    return out


def _comm_plumbing(phases, first_in, first_out):
    ios = [_phase_io(p) for p in phases]
    arrays = [a for p in phases for a in p[1]]
    out_shape = [o for io in ios for o in io[0]]
    sem_shapes = [s for io in ios for s in io[1]]
    aliases, ai, ao = {}, first_in, first_out
    for p, io in zip(phases, ios):
        aliases.update({ai + k: ao + v for k, v in io[2].items()})
        ai, ao = ai + len(p[1]), ao + len(io[0])

    def events(cins, couts, sems):
        evs = [_phase_events(p, i, o, s) for p, i, o, s in zip(
            phases, _split_refs(cins, [len(p[1]) for p in phases]), _split_refs(couts, [len(io[0]) for io in ios]),
            _split_refs(sems, [len(io[1]) for io in ios]))]

        def run(key):
            for ev in evs:
                if key in ev:
                    ev[key]()

        return {key: (lambda key=key: run(key)) for key in ("first", "mid", "last")}

    def regroup(flat):
        return _split_refs(flat, [len(io[0]) for io in ios])

    return arrays, out_shape, sem_shapes, aliases, events, regroup


def _run_phases(phases, name):
    arrays, out_shape, sem_shapes, aliases, events, regroup = _comm_plumbing(phases, 0, 0)

    def body(*refs):
        cins, couts, sems = _split_refs(refs, [len(arrays), len(out_shape), len(sem_shapes)])
        ev = events(cins, couts, sems)
        for key in ("first", "mid", "last"):
            ev[key]()

    outs = pl.pallas_call(
        body, name=name, in_specs=[ANY] * len(arrays), out_specs=[ANY] * len(out_shape), out_shape=out_shape,
        scratch_shapes=sem_shapes, input_output_aliases=aliases)(*arrays)
    return regroup(outs)


def _call(body, args, *, name, grid, in_specs, out_specs, out_shape, scratch_shapes=(), prefetch=0, phases=(),
          mid_step=None):
    steps = grid[0]
    arrays, c_shape, sem_shapes, aliases, events, regroup = _comm_plumbing(
        phases, prefetch + len(in_specs), len(out_specs))
    counts = [prefetch, len(in_specs), len(arrays), len(out_specs), len(c_shape), len(scratch_shapes), len(sem_shapes)]

    def wrapped(*refs):
        pre, ins, cins, outs, couts, scratch, sems = _split_refs(refs, counts)
        if not phases:
            return body(*pre, *ins, *outs, *scratch)
        step = pl.program_id(0)
        ev = events(cins, couts, sems)
        pl.when(step == 0)(ev["first"])
        body(*pre, *ins, *outs, *scratch)
        pl.when(step == (steps // 2 if mid_step is None else mid_step))(ev["mid"])
        pl.when(step == steps - 1)(ev["last"])

    grid_spec = pltpu.PrefetchScalarGridSpec(
        num_scalar_prefetch=prefetch, grid=grid, in_specs=list(in_specs) + [ANY] * len(arrays),
        out_specs=list(out_specs) + [ANY] * len(c_shape), scratch_shapes=list(scratch_shapes) + sem_shapes)
    outs = pl.pallas_call(
        wrapped, grid_spec=grid_spec, name=name, out_shape=list(out_shape) + c_shape, input_output_aliases=aliases,
        compiler_params=_cparams())(*args, *arrays)
    return list(outs[:len(out_specs)]), regroup(outs[len(out_specs):])


def _tile(rows, cols, nbuf):
    budget = (VMEM_LIMIT // 3) // (2 * nbuf * 4)
    if rows % 8 == 0:
        cands = [t for t in range(8, rows + 1, 8) if rows % t == 0 and t * cols <= budget]
        pref = [t for t in cands if t % 16 == 0]
        return (max(pref) if pref else max(cands) if cands else 8), cols
    cands = [t for t in range(128, cols + 1, 128) if cols % t == 0 and rows * t <= budget]
    return rows, (max(cands) if cands else 128)


def _chip_sum(g, from_sib, place, name):
    _, rows, hc = from_sib.shape
    tr, tc = _tile(rows, hc, 4)
    ni, nj = rows // tr, hc // tc

    def body(p_ref, g_ref, s_ref, hb_ref, own_ref):
        s = g_ref[...] + s_ref[...]
        hb_ref[...] = s.astype(BF)

        @pl.when(pl.program_id(2) == p_ref[1])
        def _():
            own_ref[...] = s

    grid_spec = pltpu.PrefetchScalarGridSpec(
        num_scalar_prefetch=1, grid=(ni, nj, 4),
        in_specs=[pl.BlockSpec((None, tr, tc), lambda i, j, k, p: (k, i, p[0] * nj + j)),
                  pl.BlockSpec((None, tr, tc), lambda i, j, k, p: (k, i, j))],
        out_specs=[pl.BlockSpec((None, tr, tc), lambda i, j, k, p: (k, i, j)),
                   pl.BlockSpec((tr, tc), lambda i, j, k, p: (i, j))])
    return pl.pallas_call(
        body, grid_spec=grid_spec, name=name, out_shape=[SDS((4, rows, hc), BF), SDS((rows, hc), F32)],
        compiler_params=pltpu.CompilerParams(dimension_semantics=("arbitrary",) * 3,
                                             vmem_limit_bytes=VMEM_LIMIT))(place, g, from_sib)


def _total(own, parts, place, name):
    rows, hc = own.shape
    tr, tc = _tile(rows, hc, 5)
    ni, nj = rows // tr, hc // tc

    def body(p_ref, own_ref, parts_ref, o_ref):
        s = own_ref[...]
        for k in range(3):
            s = s + parts_ref[k].astype(F32)
        o_ref[...] = s

    grid_spec = pltpu.PrefetchScalarGridSpec(
        num_scalar_prefetch=1, grid=(ni, nj),
        in_specs=[pl.BlockSpec((tr, tc), lambda i, j, p: (i, j)),
                  pl.BlockSpec((3, tr, tc), lambda i, j, p: (0, i, j))],
        out_specs=pl.BlockSpec((tr, tc), lambda i, j, p: (i, p[0] * nj + j)))
    return pl.pallas_call(
        body, grid_spec=grid_spec, name=name, out_shape=SDS((rows, 2 * hc), F32),
        compiler_params=pltpu.CompilerParams(dimension_semantics=("arbitrary",) * 2,
                                             vmem_limit_bytes=VMEM_LIMIT))(place, own, parts)


def _sum8(parts, name):
    R = parts.shape[1]

    def body(p_ref, o_ref):
        s = p_ref[0]
        for k in range(1, 8):
            s = s + p_ref[k]
        o_ref[...] = s

    return pl.pallas_call(
        body, grid=(1,), name=name, in_specs=[_const((8, R, 128))], out_specs=_const((R, 128)),
        out_shape=SDS((R, 128), F32), compiler_params=_cparams())(parts)


def _adamw(w, g, m, v, name):
    _, R, C = w.shape
    tr, tc = _tile(R, C, 7)
    c1 = 1.0 / (1.0 - ADAM_B1 ** ADAM_STEP)
    c2 = 1.0 / (1.0 - ADAM_B2 ** ADAM_STEP)

    def body(w_ref, g_ref, m_ref, v_ref, d_ref, nm_ref, nv_ref):
        gv = g_ref[...]
        nm = ADAM_B1 * m_ref[...] + (1.0 - ADAM_B1) * gv
        nv = ADAM_B2 * v_ref[...] + (1.0 - ADAM_B2) * gv * gv
        nm_ref[...] = nm
        nv_ref[...] = nv
        d_ref[...] = -ADAM_LR * ((nm * c1) / (jnp.sqrt(nv * c2) + ADAM_EPS) + ADAM_WD * w_ref[...])

    blk3 = pl.BlockSpec((None, tr, tc), lambda i, j: (0, i, j))
    return pl.pallas_call(
        body, grid=(R // tr, C // tc), name=name,
        in_specs=[blk3, pl.BlockSpec((tr, tc), lambda i, j: (i, j)), blk3, blk3], out_specs=[blk3] * 3,
        out_shape=[SDS((1, R, C), F32)] * 3,
        compiler_params=pltpu.CompilerParams(dimension_semantics=("arbitrary",) * 2,
                                             vmem_limit_bytes=VMEM_LIMIT))(w, g, m, v)


def _pack_small(parts):
    rows = []
    for p in parts:
        p = p.reshape(-1)
        rows.append(jnp.pad(p, (0, (-p.shape[0]) % 128)).reshape(-1, 128))
    out = jnp.concatenate(rows, axis=0)
    return jnp.pad(out, ((0, (-out.shape[0]) % 8), (0, 0)))


def _unpack_small(packed, shapes):
    out, row = [], 0
    for shp in shapes:
        n = 1
        for s in shp:
            n *= s
        nr = -(-n // 128)
        out.append(packed[row:row + nr].reshape(-1)[:n].reshape(shp))
        row += nr
    return out


def _pad_lanes(v, n=128):
    return jnp.pad(v, ((0, 0), (0, n - v.shape[1])))


GROUP_FFN = ("ffn_w_gate", "ffn_w_up", "ffn_w_down")
GROUP_ATTN = ("w_out", "xa_wq", "xa_wkv", "xa_wo")


def kernel(x, mem, norm_mix_w, w_in, conv_w, conv_b, dt_bias, a_log, d_skip, ssd_norm_w, hg_lower_bounds, hg_norm_w, w_out, norm_xa_w, norm_mem_w, xa_wq, xa_wkv, xa_wo, norm_ffn_w, ffn_w_gate, ffn_w_up, ffn_w_down, norm_final_w, loss_target, m_norm_mix_w, m_w_in, m_conv_w, m_conv_b, m_dt_bias, m_a_log, m_d_skip, m_ssd_norm_w, m_hg_lower_bounds, m_hg_norm_w, m_w_out, m_norm_xa_w, m_norm_mem_w, m_xa_wq, m_xa_wkv, m_xa_wo, m_norm_ffn_w, m_ffn_w_gate, m_ffn_w_up, m_ffn_w_down, m_norm_final_w, v_norm_mix_w, v_w_in, v_conv_w, v_conv_b, v_dt_bias, v_a_log, v_d_skip, v_ssd_norm_w, v_hg_lower_bounds, v_hg_norm_w, v_w_out, v_norm_xa_w, v_norm_mem_w, v_xa_wq, v_xa_wkv, v_xa_wo, v_norm_ffn_w, v_ffn_w_gate, v_ffn_w_up, v_ffn_w_down, v_norm_final_w):
    w = dict(norm_mix_w=norm_mix_w, w_in=w_in, conv_w=conv_w, conv_b=conv_b, dt_bias=dt_bias, a_log=a_log, d_skip=d_skip,
             ssd_norm_w=ssd_norm_w, hg_lower_bounds=hg_lower_bounds, hg_norm_w=hg_norm_w, w_out=w_out,
             norm_xa_w=norm_xa_w, norm_mem_w=norm_mem_w, xa_wq=xa_wq, xa_wkv=xa_wkv, xa_wo=xa_wo, norm_ffn_w=norm_ffn_w,
             ffn_w_gate=ffn_w_gate, ffn_w_up=ffn_w_up, ffn_w_down=ffn_w_down, norm_final_w=norm_final_w)
    m = dict(norm_mix_w=m_norm_mix_w, w_in=m_w_in, conv_w=m_conv_w, conv_b=m_conv_b, dt_bias=m_dt_bias, a_log=m_a_log,
             d_skip=m_d_skip, ssd_norm_w=m_ssd_norm_w, hg_lower_bounds=m_hg_lower_bounds, hg_norm_w=m_hg_norm_w,
             w_out=m_w_out, norm_xa_w=m_norm_xa_w, norm_mem_w=m_norm_mem_w, xa_wq=m_xa_wq, xa_wkv=m_xa_wkv,
             xa_wo=m_xa_wo, norm_ffn_w=m_norm_ffn_w, ffn_w_gate=m_ffn_w_gate, ffn_w_up=m_ffn_w_up,
             ffn_w_down=m_ffn_w_down, norm_final_w=m_norm_final_w)
    v = dict(norm_mix_w=v_norm_mix_w, w_in=v_w_in, conv_w=v_conv_w, conv_b=v_conv_b, dt_bias=v_dt_bias, a_log=v_a_log,
             d_skip=v_d_skip, ssd_norm_w=v_ssd_norm_w, hg_lower_bounds=v_hg_lower_bounds, hg_norm_w=v_hg_norm_w,
             w_out=v_w_out, norm_xa_w=v_norm_xa_w, norm_mem_w=v_norm_mem_w, xa_wq=v_xa_wq, xa_wkv=v_xa_wkv,
             xa_wo=v_xa_wo, norm_ffn_w=v_norm_ffn_w, ffn_w_gate=v_ffn_w_gate, ffn_w_up=v_ffn_w_up,
             ffn_w_down=v_ffn_w_down, norm_final_w=v_norm_final_w)
    xi, yi, ci = _place()
    chip = 2 * xi + yi
    place = jnp.stack([ci, chip]).astype(jnp.int32)

    def shard(t, name):
        return jnp.swapaxes(t[name], 1, 2) if name in TRANSPOSED else t[name]

    wsh = {name: shard(w, name) for name in BIG}
    half = {name: wsh[name].shape[2] // 2 for name in BIG}
    payload = {name: wsh[name][0].astype(BF) for name in BIG}
    ws = {name: w[name] for name in SMALL}
    xs, mems, tgt = x[0], mem[0], loss_target[0]

    def chip_sums(names, grads, from_sib):
        return [_chip_sum(grads[n], s, place, "grads_chip_sum_" + n) for n, s in zip(names, from_sib)]

    def totals(names, sums, others):
        return [_total(own, o, place, "grads_total_" + n) for n, (_, own), o in zip(names, sums, others)]

    ((w_in4, conv_all),) = _run_phases([("gather", [payload["w_in"], conv_w[0]], [half["w_in"], None])], "gather_w_in")
    w_in_t = w_in4.reshape(N_IN, D)
    ws["conv_w"] = conv_all[0::2].transpose(1, 0, 2).reshape(1, 4, 1536)
    rest = [n for n in BIG if n != "w_in"]
    (h0, z, xbc, hq, hf, hi, hg, dtr), (gathered,) = _in_proj(
        xs, ws["norm_mix_w"], w_in_t, phases=[("gather", [payload[n] for n in rest], [half[n] for n in rest])])
    wg = dict(zip(rest, gathered))
    wg_t, wu_t = wg["ffn_w_gate"].reshape(FFN, D), wg["ffn_w_up"].reshape(FFN, D)
    wd = wg["ffn_w_down"].reshape(FFN, D)
    w_out_f = wg["w_out"].reshape(2 * D, D)
    wq, wo = wg["xa_wq"].reshape(D, D), wg["xa_wo"].reshape(D, D)
    dtb, alog = _pad_lanes(ws["dt_bias"]), _pad_lanes(ws["a_log"])
    dskip_full = jnp.repeat(ws["d_skip"], SSD_P, axis=1)
    cw, conv_bias = ws["conv_w"][0], ws["conv_b"]
    hlb = ws["hg_lower_bounds"]
    hg_fast = (jnp.min(jax.nn.softmax(hlb, axis=0)[0]) >= HG_LB_FLOOR).astype(jnp.int32).reshape(1)

    ya, yssd, u, st_ssd = _ssd_fwd(xbc, dtr, z, cw, conv_bias, dtb, alog, dskip_full, ws["ssd_norm_w"])
    ob, ohg, st_hg = _hg_fwd(hq, hf, hi, hg, hlb, ws["hg_norm_w"], hg_fast)
    kmem, vmem = _mem_kv(mems, ws["norm_mem_w"], wg["xa_wkv"])
    x1, x2, hxa, q, ox = _attn_fwd(xs, ya, ob, w_out_f, ws["norm_xa_w"], wq, kmem, vmem, wo)
    nfin = ws["norm_final_w"].reshape(1, D)
    dx2, hffn, act, dx3, dg, du, acc_f = _ffn_loss(x2, tgt, ws["norm_ffn_w"], nfin, wg_t, wu_t, wd)

    gb = {"ffn_w_gate": _matmul_tn(dg, hffn, "gw_gate").reshape(4, FFN // 4, D),
          "ffn_w_up": _matmul_tn(du, hffn, "gw_up").reshape(4, FFN // 4, D),
          "ffn_w_down": _matmul_tn(act, dx3, "gw_down").reshape(4, FFN // 4, D)}
    (dx1, dya, dob, dq, dk, dv, acc_a), (sib_ffn,) = _attn_bwd(
        dx2, x1, q, kmem, vmem, ws["norm_xa_w"], wq, wo, w_out_f,
        phases=[("sibling", [gb[n] for n in GROUP_FFN], [half[n] for n in GROUP_FFN])])
    sums_ffn = chip_sums(GROUP_FFN, gb, sib_ffn)
    g_nmem, gb["xa_wkv"] = _mem_kv_bwd(mems, ws["norm_mem_w"], wg["xa_wkv"], dk, dv)
    gb["w_out"] = _matmul_tn_pair(ya, ob, dx1, "gw_out").reshape(4, D // 2, D)
    gb["xa_wq"] = _matmul_tn(hxa, dq, "gw_q").reshape(4, D // 4, D)
    gb["xa_wo"] = _matmul_tn(ox, dx2, "gw_o").reshape(4, D // 4, D)
    (dhq, dhf, dhi, dhg, acc_h), (others_ffn, sib_attn) = _hg_bwd(
        dob, ohg, hq, hf, hi, hg, st_hg, hlb, ws["hg_norm_w"], hg_fast,
        phases=[("chips", [hb for hb, _ in sums_ffn], None),
                ("sibling", [gb[n] for n in GROUP_ATTN], [half[n] for n in GROUP_ATTN])])
    red_ffn = totals(GROUP_FFN, sums_ffn, others_ffn)
    sums_attn = chip_sums(GROUP_ATTN, gb, sib_attn)
    dz, dxbc, ddt, gconv, ghead, glane = _ssd_bwd(dya, yssd, z, u, xbc, dtr, st_ssd, cw, dtb, alog, dskip_full,
                                                  ws["ssd_norm_w"])
    (gw_in_t,), (g_ffn, others_attn) = _gw_in(
        h0, dz, dxbc, ddt, dhq, dhf, dhi, dhg,
        phases=[("swap", red_ffn, [half[n] for n in GROUP_FFN]), ("chips", [hb for hb, _ in sums_attn], None)])
    red_attn = totals(GROUP_ATTN, sums_attn, others_attn)
    gb["w_in"] = gw_in_t.reshape(4, N_IN // 4, D)
    dproj = (xs, dx1, dz, dxbc, dhq, dhf, dhi, dhg, ddt, ws["norm_mix_w"], w_in_t)
    (gx_a, acc_ia), (g_attn, (sib_in,)) = _in_proj_bwd(
        *dproj, 0, phases=[("swap", red_attn, [half[n] for n in GROUP_ATTN]), ("sibling", [gb["w_in"]], [half["w_in"]])])
    sums_in = chip_sums(("w_in",), gb, [sib_in])
    (gx_b, acc_ib), ((others_in,),) = _in_proj_bwd(*dproj, 1, phases=[("chips", [sums_in[0][0]], None)])
    (gx, acc_ic), _ = _in_proj_bwd(*dproj, 2, done=(gx_a, gx_b))
    gx, acc_i = gx.reshape(xs.shape), acc_ia + acc_ib + acc_ic
    red_in = totals(("w_in",), sums_in, [others_in])

    gs = {
        "norm_mix_w": acc_i[0:1], "conv_w": gconv[0:4][None], "conv_b": gconv[4:5],
        "dt_bias": ghead[0:1, :NH_SSD], "a_log": ghead[2:3, :NH_SSD], "d_skip": ghead[3:4, :NH_SSD],
        "ssd_norm_w": glane[1:2], "hg_lower_bounds": acc_h[2:4], "hg_norm_w": acc_h[4:5, :128],
        "norm_xa_w": acc_a[0:1], "norm_mem_w": g_nmem, "norm_ffn_w": acc_f[2:3], "norm_final_w": acc_f[1],
    }
    loss = (0.5 / D) * jnp.sum(acc_f[0])
    small_parts = [gs[name] for name in SMALL] + [loss.reshape(1)]
    small_shapes = [gs[name].shape for name in SMALL] + [(1,)]
    (g_in,), (packed,) = _run_phases([("swap", red_in, [half["w_in"]]),
                                      ("gather", [_pack_small(small_parts)], [None])], "grads_finish")
    g_big = dict(zip(GROUP_FFN + GROUP_ATTN + ("w_in",), g_ffn + g_attn + [g_in]))
    small = _unpack_small(_sum8(packed, "small_total"), small_shapes)
    g_small = dict(zip(SMALL, small[:-1]))
    loss_all = small[-1][0]
    g_small["conv_w"] = lax.dynamic_slice_in_dim(g_small["conv_w"], chip * 384, 384, 2)

    grads, delta, new_m, new_v = {}, {}, {}, {}
    for name in BIG:
        outs = (g_big[name][None],) + tuple(_adamw(wsh[name], g_big[name], shard(m, name), shard(v, name),
                                                   "adamw_" + name))
        if name in TRANSPOSED:
            outs = tuple(jnp.swapaxes(o, 1, 2) for o in outs)
        grads[name], delta[name], new_m[name], new_v[name] = outs
    shapes = [w[name].shape for name in SMALL]
    packs = [_pack_small([t[name] for name in SMALL]) for t in (w, g_small, m, v)]
    outs = _adamw(packs[0][None], packs[1], packs[2][None], packs[3][None], "adamw_small")
    for name, g_, d_, nm_, nv_ in zip(SMALL, [g_small[n] for n in SMALL], *[_unpack_small(o[0], shapes) for o in outs]):
        grads[name] = g_.reshape(w[name].shape)
        delta[name], new_m[name], new_v[name] = d_, nm_, nv_

    return (loss_all, gx[None], *[grads[n] for n in WEIGHTS], *[delta[n] for n in WEIGHTS],
            *[new_m[n] for n in WEIGHTS], *[new_v[n] for n in WEIGHTS])
```

```python
import jax
import jax.numpy as jnp
from jax import lax
from jax.experimental import pallas as pl
from jax.experimental.pallas import tpu as pltpu

F32 = jnp.float32
BF = jnp.bfloat16
MESH = pl.DeviceIdType.MESH
SDS = jax.ShapeDtypeStruct
ANY = pl.BlockSpec(memory_space=pl.ANY)

D = 1024
EPS = 1e-6
NH_SSD = 16
SSD_P = 64
NH_HG = 8
Q = 128
CH = 2
SUB = 32
NSUB = Q // SUB
HG_LB_FLOOR = 1e-2
XA_HEADS = 4
XA_HD = 256
MEM_LEN = 256
FFN = 2816
TL = 512
TL_FFN = 256
VMEM_LIMIT = 56 << 20
MATMUL_VMEM = 40 << 20

N_IN = 6672
Z0, XBC0, DT0, HQ0, HF0, HI0, HG0 = 0, 1024, 2560, 2576, 3600, 4624, 5648

ADAM_LR, ADAM_B1, ADAM_B2, ADAM_EPS, ADAM_WD, ADAM_STEP = 0.001, 0.9, 0.999, 1e-08, 0.01, 10

BIG = ("w_in", "w_out", "xa_wq", "xa_wkv", "xa_wo", "ffn_w_gate", "ffn_w_up", "ffn_w_down")
TRANSPOSED = ("w_in", "ffn_w_gate", "ffn_w_up")
SMALL = ("norm_mix_w", "conv_w", "conv_b", "dt_bias", "a_log", "d_skip", "ssd_norm_w", "hg_lower_bounds",
         "hg_norm_w", "norm_xa_w", "norm_mem_w", "norm_ffn_w", "norm_final_w")
WEIGHTS = ("norm_mix_w", "w_in", "conv_w", "conv_b", "dt_bias", "a_log", "d_skip", "ssd_norm_w", "hg_lower_bounds",
           "hg_norm_w", "w_out", "norm_xa_w", "norm_mem_w", "xa_wq", "xa_wkv", "xa_wo", "norm_ffn_w", "ffn_w_gate",
           "ffn_w_up", "ffn_w_down", "norm_final_w")


def _cparams():
    return pltpu.CompilerParams(dimension_semantics=("arbitrary",), vmem_limit_bytes=VMEM_LIMIT)


def _const(shape):
    return pl.BlockSpec(shape, lambda i: (0,) * len(shape))


def _resident(shape):
    return pl.BlockSpec(shape, lambda i: (0,) * len(shape), pipeline_mode=pl.Buffered(1))


def _rows(tl, n):
    return pl.BlockSpec((tl, n), lambda i: (i, 0))


def _dot(a, b):
    return jnp.dot(a.astype(BF), b.astype(BF), preferred_element_type=F32)


def _dot_nt(a, b):
    return lax.dot_general(a.astype(BF), b.astype(BF), (((1,), (1,)), ((), ())), preferred_element_type=F32)


def _dot_tn(a, b):
    return lax.dot_general(a.astype(BF), b.astype(BF), (((0,), (0,)), ((), ())), preferred_element_type=F32)


def _split(v, passes):
    parts, rest = [], v
    for p in range(passes):
        hi = rest.astype(BF)
        parts.append(hi)
        if p + 1 < passes:
            rest = rest - hi.astype(F32)
    return parts


def _sel_dot(a, sel, passes=3):
    sb = sel.astype(BF)
    out = None
    for part in _split(a, passes):
        t = jnp.dot(part, sb, preferred_element_type=F32)
        out = t if out is None else out + t
    return out


def _dot_sel(sel, b, passes=3):
    sb = sel.astype(BF)
    out = None
    for part in _split(b, passes):
        t = jnp.dot(sb, part, preferred_element_type=F32)
        out = t if out is None else out + t
    return out


def _iota(shape, dim):
    return lax.broadcasted_iota(jnp.int32, shape, dim)


def _sigmoid(v):
    return 0.5 * jnp.tanh(0.5 * v) + 0.5


def _rms(v, w):
    r = lax.rsqrt(jnp.mean(v * v, axis=-1, keepdims=True) + EPS)
    n = v * r
    return n * w, n, r


def _rms_bwd(dy, n, r, w):
    dn = dy * w
    return r * (dn - n * jnp.mean(dn * n, axis=-1, keepdims=True)), dy * n


def _colsum(v):
    return jnp.sum(v, axis=0, keepdims=True)


def _zero_first(*refs):
    @pl.when(pl.program_id(0) == 0)
    def _():
        for r in refs:
            r[...] = jnp.zeros_like(r)


def _in_proj(x, nw, wt, phases=()):
    L = x.shape[0]
    tl = min(TL, L)

    def body(x_ref, nw_ref, w_ref, h0_ref, z_ref, xbc_ref, hq_ref, hf_ref, hi_ref, hg_ref, dt_ref):
        h, _, _ = _rms(x_ref[...], nw_ref[...])
        hb = h.astype(BF)
        h0_ref[...] = hb

        def proj(a, b):
            return _dot_nt(hb, w_ref[a:b, :])

        z_ref[...] = proj(Z0, XBC0).astype(BF)
        xbc_ref[...] = proj(XBC0, DT0).astype(BF)
        dt_ref[...] = proj(DT0, DT0 + 128)
        hq_ref[...] = proj(HQ0, HF0).astype(BF)
        hf_ref[...] = proj(HF0, HI0)
        hi_ref[...] = proj(HI0, HG0).astype(BF)
        hg_ref[...] = proj(HG0, N_IN).astype(BF)

    outs = [SDS((L, D), BF), SDS((L, D), BF), SDS((L, 1536), BF), SDS((L, D), BF), SDS((L, D), F32),
            SDS((L, D), BF), SDS((L, D), BF), SDS((L, 128), F32)]
    steps = L // tl
    return _call(body, (x, nw, wt), name="in_proj", grid=(steps,),
                 in_specs=[_rows(tl, D), _const((1, D)), _resident((N_IN, D))],
                 out_specs=[_rows(tl, o.shape[1]) for o in outs], out_shape=outs, phases=phases,
                 mid_step=(3 * steps) // 4)


def _mem_kv(mem, nw, wkv4):
    def body(m_ref, nw_ref, w_ref, k_ref, v_ref):
        m, _, _ = _rms(m_ref[...], nw_ref[...])
        mb = m.astype(BF)
        for i in range(2):
            sl = slice(512 * i, 512 * i + 512)
            k_ref[:, sl] = jnp.dot(mb, w_ref[i], preferred_element_type=F32).astype(BF)
            v_ref[:, sl] = jnp.dot(mb, w_ref[2 + i], preferred_element_type=F32).astype(BF)

    outs = [SDS((MEM_LEN, D), BF)] * 2
    return pl.pallas_call(
        body, grid=(1,), name="mem_kv",
        in_specs=[_const((MEM_LEN, D)), _const((1, D)), _const((4, D, 512))],
        out_specs=[_const((MEM_LEN, D))] * 2, out_shape=outs, compiler_params=_cparams())(mem, nw, wkv4)


def _mem_kv_bwd(mem, nw, wkv4, dk, dv):
    def body(m_ref, nw_ref, w_ref, dk_ref, dv_ref, gnw_ref, gw_ref):
        m, n, _ = _rms(m_ref[...], nw_ref[...])
        mb = m.astype(BF)
        dm = jnp.zeros((MEM_LEN, D), F32)
        for i in range(4):
            src = dk_ref if i < 2 else dv_ref
            d = src[:, 512 * (i % 2):512 * (i % 2) + 512].astype(BF)
            gw_ref[i] = _dot_tn(mb, d)
            dm = dm + _dot_nt(d, w_ref[i])
        gnw_ref[...] = _colsum(dm * n)

    return pl.pallas_call(
        body, grid=(1,), name="mem_kv_bwd",
        in_specs=[_const((MEM_LEN, D)), _const((1, D)), _const((4, D, 512)), _const((MEM_LEN, D)), _const((MEM_LEN, D))],
        out_specs=[_const((1, D)), _const((4, D, 512))],
        out_shape=[SDS((1, D), F32), SDS((4, D, 512), F32)], compiler_params=_cparams())(mem, nw, wkv4, dk, dv)


def _softmax_rows(sc):
    e = jnp.exp(sc - jnp.max(sc, axis=-1, keepdims=True))
    return e * (1.0 / jnp.sum(e, axis=-1, keepdims=True))


def _attn_fwd(x, ya, ob, w_out, nxa, wq, k, v, wo):
    L = x.shape[0]
    tl = min(TL, L)
    scale = XA_HD ** -0.5

    def body(x_ref, ya_ref, ob_ref, wout_ref, nxa_ref, wq_ref, k_ref, v_ref, wo_ref,
             x1_ref, x2_ref, hxa_ref, q_ref, ox_ref):
        x1 = x_ref[...] + jnp.dot(ya_ref[...], wout_ref[:D, :], preferred_element_type=F32) \
            + jnp.dot(ob_ref[...], wout_ref[D:, :], preferred_element_type=F32)
        x1_ref[...] = x1
        h, _, _ = _rms(x1, nxa_ref[...])
        hb = h.astype(BF)
        hxa_ref[...] = hb
        qb = jnp.dot(hb, wq_ref[...], preferred_element_type=F32).astype(BF)
        q_ref[...] = qb
        heads = [slice(hd * XA_HD, (hd + 1) * XA_HD) for hd in range(XA_HEADS)]
        ps = [_softmax_rows(_dot_nt(qb[:, sl], k_ref[:, sl]) * scale) for sl in heads]
        oxs = [_dot(p, v_ref[:, sl]) for p, sl in zip(ps, heads)]
        oxb = jnp.concatenate(oxs, axis=1).astype(BF)
        ox_ref[...] = oxb
        x2_ref[...] = x1 + jnp.dot(oxb, wo_ref[...], preferred_element_type=F32)

    outs = [SDS((L, D), F32), SDS((L, D), F32), SDS((L, D), BF), SDS((L, D), BF), SDS((L, D), BF)]
    return pl.pallas_call(
        body, grid=(L // tl,), name="attn_fwd",
        in_specs=[_rows(tl, D), _rows(tl, D), _rows(tl, D), _resident((2 * D, D)), _const((1, D)), _resident((D, D)),
                  _resident((MEM_LEN, D)), _resident((MEM_LEN, D)), _resident((D, D))],
        out_specs=[_rows(tl, D)] * 5, out_shape=outs, compiler_params=_cparams())(x, ya, ob, w_out, nxa, wq, k, v, wo)


def _ffn_loss(x2, tgt, nffn, nfin, wgt, wut, wd):
    L = x2.shape[0]
    tl = min(TL_FFN, L)

    def body(x2_ref, t_ref, nffn_ref, nfin_ref, wg_ref, wu_ref, wd_ref,
             dx2_ref, h_ref, a_ref, dx3_ref, dg_ref, du_ref, acc_ref):
        _zero_first(acc_ref)
        x2v = x2_ref[...]
        h, n2, r2 = _rms(x2v, nffn_ref[...])
        hb = h.astype(BF)
        h_ref[...] = hb
        g = _dot_nt(hb, wg_ref[...])
        u = _dot_nt(hb, wu_ref[...])
        sg = _sigmoid(g)
        ab = (g * sg * u).astype(BF)
        a_ref[...] = ab
        x3 = x2v + jnp.dot(ab, wd_ref[...], preferred_element_type=F32)
        y, n3, r3 = _rms(x3, nfin_ref[...])
        err = y - t_ref[...]
        acc_ref[0:1, :] += _colsum(err * err)
        dx3, dwf = _rms_bwd(err * (1.0 / D), n3, r3, nfin_ref[...])
        acc_ref[1:2, :] += _colsum(dwf)
        dx3b = dx3.astype(BF)
        dx3_ref[...] = dx3b
        da = _dot_nt(dx3b, wd_ref[...])
        dgb = (da * u * sg * (1.0 + g * (1.0 - sg))).astype(BF)
        dub = (da * g * sg).astype(BF)
        dg_ref[...] = dgb
        du_ref[...] = dub
        dh = jnp.dot(dgb, wg_ref[...], preferred_element_type=F32) + jnp.dot(dub, wu_ref[...], preferred_element_type=F32)
        dn, dwn = _rms_bwd(dh, n2, r2, nffn_ref[...])
        acc_ref[2:3, :] += _colsum(dwn)
        dx2_ref[...] = dx3 + dn

    outs = [SDS((L, D), F32), SDS((L, D), BF), SDS((L, FFN), BF), SDS((L, D), BF), SDS((L, FFN), BF),
            SDS((L, FFN), BF), SDS((8, D), F32)]
    wspec = _resident((FFN, D))
    return pl.pallas_call(
        body, grid=(L // tl,), name="ffn_loss",
        in_specs=[_rows(tl, D), _rows(tl, D), _const((1, D)), _const((1, D)), wspec, wspec, wspec],
        out_specs=[_rows(tl, D), _rows(tl, D), _rows(tl, FFN), _rows(tl, D), _rows(tl, FFN), _rows(tl, FFN),
                   _const((8, D))],
        out_shape=outs, compiler_params=_cparams())(x2, tgt, nffn, nfin, wgt, wut, wd)


def _attn_bwd(dx2, x1, q, k, v, nxa, wq, wo, w_out, phases=()):
    L = dx2.shape[0]
    tl = min(TL, L)
    scale = XA_HD ** -0.5

    def body(dx2_ref, x1_ref, q_ref, k_ref, v_ref, nxa_ref, wq_ref, wo_ref, wout_ref,
             dx1_ref, dya_ref, dob_ref, dq_ref, dk_ref, dv_ref, acc_ref):
        _zero_first(dk_ref, dv_ref, acc_ref)
        dx2v = dx2_ref[...]
        dox = _dot_nt(dx2v, wo_ref[...]).astype(BF)
        qb = q_ref[...]
        heads = [slice(hd * XA_HD, (hd + 1) * XA_HD) for hd in range(XA_HEADS)]
        ps = [_softmax_rows(_dot_nt(qb[:, sl], k_ref[:, sl]) * scale) for sl in heads]
        dps = [_dot_nt(dox[:, sl], v_ref[:, sl]) for sl in heads]
        dss = [(p * (dp - jnp.sum(dp * p, axis=-1, keepdims=True)) * scale).astype(BF) for p, dp in zip(ps, dps)]
        for sl, p, ds in zip(heads, ps, dss):
            dv_ref[:, sl] += _dot_tn(p, dox[:, sl])
            dk_ref[:, sl] += _dot_tn(ds, qb[:, sl])
        dqs = [_dot(ds, k_ref[:, sl]) for sl, ds in zip(heads, dss)]
        dqb = jnp.concatenate(dqs, axis=1).astype(BF)
        dq_ref[...] = dqb
        dh = _dot_nt(dqb, wq_ref[...])
        _, n1, r1 = _rms(x1_ref[...], nxa_ref[...])
        dn, dwn = _rms_bwd(dh, n1, r1, nxa_ref[...])
        acc_ref[0:1, :] += _colsum(dwn)
        dx1 = dx2v + dn
        dx1_ref[...] = dx1
        dx1b = dx1.astype(BF)
        dya_ref[...] = _dot_nt(dx1b, wout_ref[:D, :]).astype(BF)
        dob_ref[...] = _dot_nt(dx1b, wout_ref[D:, :]).astype(BF)

    outs = [SDS((L, D), F32), SDS((L, D), BF), SDS((L, D), BF), SDS((L, D), BF), SDS((MEM_LEN, D), F32),
            SDS((MEM_LEN, D), F32), SDS((8, D), F32)]
    return _call(body, (dx2, x1, q, k, v, nxa, wq, wo, w_out), name="attn_bwd", grid=(L // tl,),
                 in_specs=[_rows(tl, D), _rows(tl, D), _rows(tl, D), _resident((MEM_LEN, D)), _resident((MEM_LEN, D)),
                           _const((1, D)), _resident((D, D)), _resident((D, D)), _resident((2 * D, D))],
                 out_specs=[_rows(tl, D)] * 4 + [_const((MEM_LEN, D)), _const((MEM_LEN, D)), _const((8, D))],
                 out_shape=outs, phases=phases)


IN_BWD_PARTS = ((0, 1), (1, 4), (4, 8))


def _in_proj_bwd(x, dx1, dz, dxbc, dhq, dhf, dhi, dhg, ddt, nw, wt, part, done=(), phases=()):
    tl = min(TL, x.shape[0] // 8)
    eighth = x.shape[0] // 8 // tl
    first, steps = IN_BWD_PARTS[part][0] * eighth, (IN_BWD_PARTS[part][1] - IN_BWD_PARTS[part][0]) * eighth
    L = steps * tl
    rows = lambda n: pl.BlockSpec((tl, n), lambda i: (i + first, 0))
    starts = [sum(d.shape[0] for d in done[:k]) // tl for k in range(len(done))]
    assert sum(d.shape[0] for d in done) in (0, L)

    def body(x_ref, dx1_ref, dz_ref, dxbc_ref, dhq_ref, dhf_ref, dhi_ref, dhg_ref, ddt_ref, nw_ref, w_ref, *rest):
        done_refs, (gx_ref, acc_ref) = rest[:-2], rest[-2:]
        step = pl.program_id(0)
        _zero_first(acc_ref)
        dh = _dot(dz_ref[...], w_ref[Z0:XBC0, :]) + _dot(dxbc_ref[...], w_ref[XBC0:DT0, :]) \
            + _dot(ddt_ref[...], w_ref[DT0:DT0 + 128, :]) + _dot(dhq_ref[...], w_ref[HQ0:HF0, :]) \
            + _dot(dhf_ref[...], w_ref[HF0:HI0, :]) + _dot(dhi_ref[...], w_ref[HI0:HG0, :]) \
            + _dot(dhg_ref[...], w_ref[HG0:N_IN, :])
        _, n, r = _rms(x_ref[...], nw_ref[...])
        dn, dwn = _rms_bwd(dh, n, r, nw_ref[...])
        acc_ref[0:1, :] += _colsum(dwn)
        if not done:
            gx_ref[...] = dx1_ref[...] + dn
            return
        gx_ref[1] = dx1_ref[...] + dn
        for ref, start, piece in zip(done_refs, starts, done):
            @pl.when(jnp.logical_and(step >= start, step < start + piece.shape[0] // tl))
            def _(ref=ref):
                gx_ref[0] = ref[...]

    def piece_spec(start, piece):
        return pl.BlockSpec((tl, D), lambda i: (jnp.clip(i - start, 0, piece.shape[0] // tl - 1), 0))

    gx_spec, gx_shape = (pl.BlockSpec((2, tl, D), lambda i: (0, i, 0)), (2, L, D)) if done else (_rows(tl, D), (L, D))
    return _call(
        body, (x, dx1, dz, dxbc, dhq, dhf, dhi, dhg, ddt, nw, wt, *done), grid=(steps,), name="in_proj_bwd_%d" % part,
        in_specs=[rows(D), rows(D), rows(D), rows(1536), rows(D), rows(D), rows(D), rows(D), rows(128),
                  _const((1, D)), _resident((N_IN, D))] + [piece_spec(s, d) for s, d in zip(starts, done)],
        out_specs=[gx_spec, _const((8, D))], out_shape=[SDS(gx_shape, F32), SDS((8, D), F32)], phases=phases)


def _gw_in(h0, dz, dxbc, ddt, dhq, dhf, dhi, dhg, phases=()):
    L = h0.shape[0]
    tl = min(512, L)

    def body(h_ref, dz_ref, dxbc_ref, ddt_ref, dhq_ref, dhf_ref, dhi_ref, dhg_ref, o_ref):
        _zero_first(o_ref)
        hb = h_ref[...]
        o_ref[Z0:XBC0, :] += _dot_tn(dz_ref[...], hb)
        o_ref[XBC0:DT0, :] += _dot_tn(dxbc_ref[...], hb)
        o_ref[DT0:HQ0, :] += _dot_tn(ddt_ref[...], hb)[0:NH_SSD, :]
        o_ref[HQ0:HF0, :] += _dot_tn(dhq_ref[...], hb)
        o_ref[HF0:HI0, :] += _dot_tn(dhf_ref[...], hb)
        o_ref[HI0:HG0, :] += _dot_tn(dhi_ref[...], hb)
        o_ref[HG0:N_IN, :] += _dot_tn(dhg_ref[...], hb)

    return _call(body, (h0, dz, dxbc, ddt, dhq, dhf, dhi, dhg), name="gw_in", grid=(L // tl,),
                 in_specs=[_rows(tl, D), _rows(tl, D), _rows(tl, 1536), _rows(tl, 128), _rows(tl, D), _rows(tl, D),
                           _rows(tl, D), _rows(tl, D)],
                 out_specs=[_const((N_IN, D))], out_shape=[SDS((N_IN, D), F32)], phases=phases)


def _token_tile(L, out_bytes, row_bytes):
    tl = min(2048, L)
    while tl > 256 and out_bytes + 2 * tl * row_bytes > MATMUL_VMEM:
        tl //= 2
    return tl


def _matmul_tn(a, b, name):
    L, M = a.shape
    N = b.shape[1]
    tl = _token_tile(L, 4 * M * N, M * a.dtype.itemsize + N * b.dtype.itemsize)

    def body(a_ref, b_ref, o_ref):
        _zero_first(o_ref)
        o_ref[...] += _dot_tn(a_ref[...], b_ref[...])

    return pl.pallas_call(
        body, grid=(L // tl,), name=name, in_specs=[_rows(tl, M), _rows(tl, N)], out_specs=_const((M, N)),
        out_shape=SDS((M, N), F32), compiler_params=_cparams())(a, b)


def _matmul_tn_pair(a0, a1, b, name):
    L, M = a0.shape
    N = b.shape[1]
    tl = _token_tile(L, 8 * M * N, 2 * M * a0.dtype.itemsize + N * b.dtype.itemsize)

    def body(a0_ref, a1_ref, b_ref, o_ref):
        _zero_first(o_ref)
        bv = b_ref[...].astype(BF)
        o_ref[:M, :] += _dot_tn(a0_ref[...], bv)
        o_ref[M:, :] += _dot_tn(a1_ref[...], bv)

    return pl.pallas_call(
        body, grid=(L // tl,), name=name, in_specs=[_rows(tl, M), _rows(tl, M), _rows(tl, N)],
        out_specs=_const((2 * M, N)), out_shape=SDS((2 * M, N), F32), compiler_params=_cparams())(a0, a1, b)


def _head_expand():
    e = (jnp.right_shift(_iota((128, D), 1), 6) == _iota((128, D), 0)).astype(BF)
    et = (jnp.right_shift(_iota((D, 128), 0), 6) == _iota((D, 128), 1)).astype(BF)
    return e, et


def _conv_shifts(cur, other, up):
    rows = _iota((Q, 1), 0)
    out = []
    for s in (1, 2, 3):
        if up:
            out.append(jnp.where(rows >= Q - s, pltpu.roll(other, Q - s, 0), pltpu.roll(cur, Q - s, 0)))
        else:
            out.append(jnp.where(rows < s, pltpu.roll(other, s, 0), pltpu.roll(cur, s, 0)))
    return out


def _ssd_pre(u, dtr, dtb, alog):
    e, et = _head_expand()
    sgu = _sigmoid(u)
    xc = u * sgu
    lane = _iota((1, 128), 1)
    hmask = (lane < NH_SSD).astype(F32)
    pre = dtr + dtb
    dt = (jnp.maximum(pre, 0.0) + jnp.log(1.0 + jnp.exp(-jnp.abs(pre)))) * hmask
    a_row = -jnp.exp(alog)
    causal = _iota((Q, Q), 1) <= _iota((Q, Q), 0)
    tri = causal.astype(BF)
    acum = _dot_sel(tri, dt * a_row)
    acum_full = _sel_dot(acum, e)
    alast_full = acum_full[Q - 1:Q, :]
    dt_full = _sel_dot(dt, e)
    xs = xc[:, :D]
    return dict(e=e, et=et, sgu=sgu, xs=xs, bm=xc[:, D:D + 256], cm=xc[:, D + 256:], hmask=hmask, pre=pre, dt=dt,
                a_row=a_row, causal=causal, tri=tri, acum=acum, acum_t=acum.T, eA_full=jnp.exp(acum_full),
                dte_full=jnp.exp(alast_full - acum_full), dt_full=dt_full, xdt=xs * dt_full)


def _ssd_decay(pre, hh, cb):
    seg = pre["acum"][:, hh:hh + 1] - pre["acum_t"][hh:hh + 1, :]
    lm = jnp.where(pre["causal"], jnp.exp(jnp.minimum(seg, 0.0)), 0.0)
    return lm, cb * lm


def _ssd_fwd(xbc, dtr, z, conv_w, conv_b, dtb, alog, dskip_full, nw):
    L = xbc.shape[0]
    nc = L // Q

    def chunk(ck, xbc_ref, dtr_ref, z_ref, cw_ref, cb_ref, dtb_ref, alog_ref, dsk_ref, nw_ref,
              ya_ref, y_ref, u_ref, st_ref, prev_ref, s_ref):
        tok = slice(Q * ck, Q * ck + Q)
        xr = xbc_ref[tok, :].astype(F32)
        sh = _conv_shifts(xr, prev_ref[...], up=False)
        u = cb_ref[...] + cw_ref[3:4, :] * xr + cw_ref[2:3, :] * sh[0] + cw_ref[1:2, :] * sh[1] + cw_ref[0:1, :] * sh[2]
        prev_ref[...] = xr
        ub = u.astype(BF)
        u_ref[tok, :] = ub
        pre = _ssd_pre(ub.astype(F32), dtr_ref[tok, :], dtb_ref[...], alog_ref[...])
        lo = _iota((1, 128), 1) < SSD_P
        s_old = s_ref[...]
        st_ref[ck] = s_old
        ys = []
        for g in range(2):
            bg, cg = pre["bm"][:, 128 * g:128 * g + 128], pre["cm"][:, 128 * g:128 * g + 128]
            cb = _dot_nt(cg, bg)
            gs = slice(512 * g, 512 * g + 512)
            yd = []
            for j in range(4 * g, 4 * g + 4):
                xp = pre["xdt"][:, 128 * j:128 * j + 128].astype(BF)
                _, m0 = _ssd_decay(pre, 2 * j, cb)
                _, m1 = _ssd_decay(pre, 2 * j + 1, cb)
                yd.append(jnp.where(lo, _dot(m0, xp), _dot(m1, xp)))
            yoff = _dot_nt(cg, s_old[gs, :]) * pre["eA_full"][:, gs]
            ys.append(jnp.concatenate(yd, axis=1) + yoff)
            st = _dot_tn((pre["xdt"] * pre["dte_full"])[:, gs], bg)
            cdcol = jnp.exp(_dot_sel(pre["et"][gs, :], pre["acum_t"])[:, Q - 1:Q])
            s_ref[gs, :] = s_old[gs, :] * cdcol + st
        y = jnp.concatenate(ys, axis=1) + dsk_ref[...] * pre["xs"]
        yb = y.astype(BF)
        y_ref[tok, :] = yb
        zf = z_ref[tok, :].astype(F32)
        yz = yb.astype(F32) * zf * _sigmoid(zf)
        outs = []
        for g in range(2):
            gs = slice(512 * g, 512 * g + 512)
            o, _, _ = _rms(yz[:, gs], nw_ref[:, gs])
            outs.append(o)
        ya_ref[tok, :] = jnp.concatenate(outs, axis=1).astype(BF)

    def body(*refs):
        _zero_first(*refs[-2:])
        for ck in range(CH):
            chunk(ck, *refs)

    outs = [SDS((L, D), BF), SDS((L, D), BF), SDS((L, 1536), BF), SDS((nc, D, 128), F32)]
    return pl.pallas_call(
        body, grid=(nc // CH,), name="ssd_fwd",
        in_specs=[_rows(CH * Q, 1536), _rows(CH * Q, 128), _rows(CH * Q, D), _const((4, 1536)), _const((1, 1536)), _const((1, 128)),
                  _const((1, 128)), _const((1, D)), _const((1, D))],
        out_specs=[_rows(CH * Q, D), _rows(CH * Q, D), _rows(CH * Q, 1536),
                   pl.BlockSpec((CH, D, 128), lambda i: (i, 0, 0))],
        out_shape=outs, scratch_shapes=[pltpu.VMEM((Q, 1536), F32), pltpu.VMEM((D, 128), F32)],
        compiler_params=_cparams())(xbc, dtr, z, conv_w, conv_b, dtb, alog, dskip_full, nw)


def _ssd_bwd(dya, y, z, u, xbc, dtr, states, conv_w, dtb, alog, dskip_full, nw):
    L = dya.shape[0]
    nc = L // Q

    def chunk(ck, step, dya_ref, y_ref, z_ref, u_ref, xc_ref, dtr_ref, st_ref, cw_ref, dtb_ref, alog_ref, dsk_ref, nw_ref,
              dz_ref, dxbc_ref, ddt_ref, gconv_ref, ghead_ref, glane_ref, gs_ref, ndu_ref):
        tok = slice(Q * ck, Q * ck + Q)
        uf = u_ref[tok, :].astype(F32)
        pre = _ssd_pre(uf, dtr_ref[tok, :], dtb_ref[...], alog_ref[...])
        e, et, xs, xdt = pre["e"], pre["et"], pre["xs"], pre["xdt"]
        lane = _iota((1, 128), 1)
        lo = lane < SSD_P
        sub = _iota((128, 1), 0)
        zf = z_ref[tok, :].astype(F32)
        sgz = _sigmoid(zf)
        sz = zf * sgz
        yv = y_ref[tok, :].astype(F32)
        yz = yv * sz
        dyav = dya_ref[tok, :].astype(F32)
        dyz, dnw = [], []
        for g in range(2):
            gs = slice(512 * g, 512 * g + 512)
            _, n, r = _rms(yz[:, gs], nw_ref[:, gs])
            dv, dw = _rms_bwd(dyav[:, gs], n, r, nw_ref[:, gs])
            dyz.append(dv)
            dnw.append(dw)
        dyz = jnp.concatenate(dyz, axis=1)
        glane_ref[1:2, :] += _colsum(jnp.concatenate(dnw, axis=1))
        dy = dyz * sz
        dz_ref[tok, :] = (dyz * yv * sgz * (1.0 + zf * (1.0 - sgz))).astype(BF)
        glane_ref[0:1, :] += _colsum(dy * xs)
        dxs = dsk_ref[...] * dy

        s_in = st_ref[ck]
        gst = gs_ref[...]
        gy = dy * pre["eA_full"]
        xdte = xdt * pre["dte_full"]
        dacum = jnp.zeros((Q, 128), F32)
        dacum_t = jnp.zeros((128, Q), F32)
        dxdt, dacum_full, ddte_full, dbs, dcs = [], [], [], [], []
        for g in range(2):
            gs = slice(512 * g, 512 * g + 512)
            bg, cg = pre["bm"][:, 128 * g:128 * g + 128], pre["cm"][:, 128 * g:128 * g + 128]
            sg_, dg_ = s_in[gs, :], gst[gs, :]
            yoff = _dot_nt(cg, sg_) * pre["eA_full"][:, gs]
            dc = _dot(gy[:, gs], sg_)
            dsin = _dot_tn(gy[:, gs], cg)
            dacum_full.append(dy[:, gs] * yoff)
            tg = _dot_nt(bg, dg_)
            ddte_full.append(tg * xdt[:, gs])
            db = _dot(xdte[:, gs], dg_)
            cb = _dot_nt(cg, bg)
            dcb = jnp.zeros((Q, Q), F32)
            dxg = []
            for j in range(4 * g, 4 * g + 4):
                xp = xdt[:, 128 * j:128 * j + 128].astype(BF)
                dyp = dy[:, 128 * j:128 * j + 128]
                dxp = jnp.zeros((Q, 128), F32)
                for idx in range(2):
                    hh = 2 * j + idx
                    lm, m = _ssd_decay(pre, hh, cb)
                    dym = jnp.where(lo if idx == 0 else jnp.logical_not(lo), dyp, 0.0).astype(BF)
                    dm = jnp.where(pre["causal"], _dot_nt(dym, xp), 0.0)
                    w = dm * m
                    dacum = dacum + jnp.where(lane == hh, jnp.sum(w, axis=1, keepdims=True), 0.0)
                    dacum_t = dacum_t + jnp.where(sub == hh, jnp.sum(w, axis=0, keepdims=True), 0.0)
                    dcb = dcb + dm * lm
                    dxp = dxp + _dot_tn(m, dym)
                dxg.append(dxp)
            dxdt.append(jnp.concatenate(dxg, axis=1) + tg * pre["dte_full"][:, gs])
            dcs.append(dc + _dot(dcb, bg))
            dbs.append(db + _dot_tn(dcb, cg))
            cdcol = jnp.exp(_dot_sel(et[gs, :], pre["acum_t"])[:, Q - 1:Q])
            gs_ref[gs, :] = dsin + dg_ * cdcol
        dxdt = jnp.concatenate(dxdt, axis=1)
        dacum = dacum + _sel_dot(jnp.concatenate(dacum_full, axis=1), et, 2) - dacum_t.T
        alast = pre["acum"][Q - 1:Q, :]
        dte = jnp.exp(alast - pre["acum"])
        ddte = _sel_dot(jnp.concatenate(ddte_full, axis=1), et, 2) * dte
        dacum = dacum - ddte
        dcd_col = jnp.sum(_dot_sel(e, gst * s_in, 2), axis=1, keepdims=True)
        dcd_row = jnp.broadcast_to(dcd_col, (128, 128)).T[0:1, :]
        dalast = _colsum(ddte) + dcd_row * jnp.exp(alast)
        dacum = dacum + jnp.where(_iota((Q, 1), 0) == Q - 1, dalast, 0.0)
        ddt = _sel_dot(dxdt * xs, et, 2)
        dxs = dxs + dxdt * pre["dt_full"]
        dda = _dot_sel((_iota((Q, Q), 1) >= _iota((Q, Q), 0)).astype(BF), dacum)
        ddt = ddt + dda * pre["a_row"]
        ghead_ref[1:2, :] += _colsum(dda * pre["dt"])
        ddtr = ddt * _sigmoid(pre["pre"]) * pre["hmask"]
        ghead_ref[0:1, :] += _colsum(ddtr)
        ddt_ref[tok, :] = ddtr

        dxc = jnp.concatenate([dxs] + dbs + dcs, axis=1)
        sgu = pre["sgu"]
        du = dxc * sgu * (1.0 + uf * (1.0 - sgu))
        shu = _conv_shifts(du, ndu_ref[...], up=True)
        dxr = cw_ref[3:4, :] * du + cw_ref[2:3, :] * shu[0] + cw_ref[1:2, :] * shu[1] + cw_ref[0:1, :] * shu[2]
        ndu_ref[...] = du
        dxbc_ref[tok, :] = dxr.astype(BF)
        xr = xc_ref[tok, :].astype(F32)
        gconv_ref[3:4, :] += _colsum(du * xr)
        gconv_ref[2:3, :] += _colsum(shu[0] * xr)
        gconv_ref[1:2, :] += _colsum(shu[1] * xr)
        gconv_ref[0:1, :] += _colsum(shu[2] * xr)
        gconv_ref[4:5, :] += _colsum(du)

        @pl.when(jnp.logical_and(step == nc // CH - 1, ck == 0))
        def _():
            ghead_ref[2:3, :] = ghead_ref[1:2, :] * pre["a_row"]
            ghead_ref[3:4, :] = _sel_dot(glane_ref[...], et)[0:1, :]

    def body(*refs):
        _zero_first(*refs[-5:])
        for ck in reversed(range(CH)):
            chunk(ck, pl.program_id(0), *refs)

    rev = lambda i: (nc // CH - 1 - i, 0)
    outs = [SDS((L, D), BF), SDS((L, 1536), BF), SDS((L, 128), F32), SDS((8, 1536), F32), SDS((8, 128), F32),
            SDS((8, D), F32)]
    return pl.pallas_call(
        body, grid=(nc // CH,), name="ssd_bwd",
        in_specs=[pl.BlockSpec((CH * Q, D), rev), pl.BlockSpec((CH * Q, D), rev), pl.BlockSpec((CH * Q, D), rev),
                  pl.BlockSpec((CH * Q, 1536), rev), pl.BlockSpec((CH * Q, 1536), rev),
                  pl.BlockSpec((CH * Q, 128), rev), pl.BlockSpec((CH, D, 128), lambda i: (nc // CH - 1 - i, 0, 0)),
                  _const((4, 1536)), _const((1, 128)), _const((1, 128)), _const((1, D)), _const((1, D))],
        out_specs=[pl.BlockSpec((CH * Q, D), rev), pl.BlockSpec((CH * Q, 1536), rev), pl.BlockSpec((CH * Q, 128), rev),
                   _const((8, 1536)), _const((8, 128)), _const((8, D))],
        out_shape=outs, scratch_shapes=[pltpu.VMEM((D, 128), F32), pltpu.VMEM((Q, 1536), F32)],
        compiler_params=_cparams())(dya, y, z, u, xbc, dtr, states, conv_w, dtb, alog, dskip_full, nw)


def _hg_gates(hq, hf, hlb):
    h0, h1 = hlb[0:1, :], hlb[1:2, :]
    mx = jnp.maximum(h0, h1)
    e0, e1 = jnp.exp(h0 - mx), jnp.exp(h1 - mx)
    lb = e0 / (e0 + e1)
    sg = _sigmoid(hf)
    fg = lb + (1.0 - lb) * sg
    tri = (_iota((Q, Q), 1) <= _iota((Q, Q), 0)).astype(BF)
    return hq * _sigmoid(hq), 1.0 - fg, fg, sg, lb, e1 / (e0 + e1), _dot_sel(tri, jnp.log(fg))


def _hg_intra(b, q, k):
    rowblk = jnp.right_shift(_iota((Q, 1), 0), SUB.bit_length() - 1)
    mids = [b[SUB * i + SUB // 2:SUB * i + SUB // 2 + 1, :] for i in range(NSUB)]
    prevs = [mids[0]] + [b[SUB * i - 1:SUB * i, :] for i in range(1, NSUB)]
    mfull = jnp.concatenate([jnp.broadcast_to(r, (SUB, 128)) for r in mids], axis=0)
    rfull = jnp.concatenate([jnp.broadcast_to(r, (SUB, 128)) for r in prevs], axis=0)
    eqd, ek, eqo = jnp.exp(b - mfull), jnp.exp(mfull - b), jnp.exp(b - rfull)
    qd, qo, khat = q * eqd, q * eqo, k * ek
    rtab = jnp.concatenate(prevs, axis=0)
    djs = [jnp.exp(rtab - mids[j]) for j in range(NSUB)]
    zero = jnp.zeros((SUB, 128), F32)
    cols = []
    for j in range(NSUB):
        pieces = []
        for i in range(NSUB):
            rs = slice(SUB * i, SUB * i + SUB)
            pieces.append(zero if i < j else qd[rs] if i == j else qo[rs] * djs[j][i:i + 1, :])
        cols.append(jnp.concatenate(pieces, axis=0))
    qt = jnp.concatenate(cols, axis=1).astype(BF)
    kt = jnp.concatenate([jnp.where(rowblk == j, khat, 0.0) for j in range(NSUB)], axis=1).astype(BF)
    causal = _iota((Q, Q), 1) <= _iota((Q, Q), 0)
    att = jnp.where(causal, _dot_nt(qt, kt), 0.0)
    return att, qt, kt, (eqd, ek, eqo, djs), causal


def _hg_intra_bwd(dqt, dkt, qt, kt, factors):
    eqd, ek, eqo, djs = factors
    dqd, dqo, dkh, db = [], [], [], []
    for i in range(NSUB):
        rs = slice(SUB * i, SUB * i + SUB)
        diag = slice(128 * i, 128 * i + 128)
        dqd.append(dqt[rs, diag])
        dkh.append(dkt[rs, diag])
        dbi = qt[rs, diag].astype(F32) * dqt[rs, diag] - kt[rs, diag].astype(F32) * dkt[rs, diag]
        acc = jnp.zeros((SUB, 128), F32)
        for j in range(i):
            bl = slice(128 * j, 128 * j + 128)
            acc = acc + dqt[rs, bl] * djs[j][i:i + 1, :]
            dbi = dbi + qt[rs, bl].astype(F32) * dqt[rs, bl]
        dqo.append(acc)
        db.append(dbi)
    cat = lambda t: jnp.concatenate(t, axis=0)
    return cat(dqd) * eqd + cat(dqo) * eqo, cat(dkh) * ek, cat(db)


def _hg_att_exact(b, q, k, b_ref, q_ref, att_t_ref):
    b_ref[...] = b
    q_ref[...] = q
    att_t_ref[...] = jnp.zeros((Q, Q), F32)
    rows, lane = _iota((Q, 1), 0), _iota((1, Q), 1)

    def step(i, carry):
        e = jnp.exp(jnp.minimum(b_ref[pl.ds(i, 1), :] - b, 0.0))
        col = jnp.sum(q_ref[pl.ds(i, 1), :] * k * e, axis=1, keepdims=True)
        att_t_ref[...] = jnp.where(lane == i, jnp.where(rows <= i, col, 0.0), att_t_ref[...])
        return carry

    lax.fori_loop(0, Q, step, 0)
    return att_t_ref[...].T


def _hg_att_exact_bwd(da, b, q, k, b_ref, q_ref, da_t_ref, dq_ref, dk_ref):
    b_ref[...] = b
    q_ref[...] = q
    da_t_ref[...] = da.T
    dk_ref[...] = jnp.zeros((Q, 128), F32)
    lane = _iota((1, Q), 1)

    def step(i, carry):
        e = jnp.exp(jnp.minimum(b_ref[pl.ds(i, 1), :] - b, 0.0))
        g = jnp.sum(jnp.where(lane == i, da_t_ref[...], 0.0), axis=1, keepdims=True) * e
        dq_ref[pl.ds(i, 1), :] = jnp.sum(g * k, axis=0, keepdims=True)
        dk_ref[...] += g * q_ref[pl.ds(i, 1), :]
        return carry

    lax.fori_loop(0, Q, step, 0)
    dq, dk = dq_ref[...], dk_ref[...]
    return dq, dk, q * dq - k * dk


def _hg_fwd(hq, hf, hi, hg, hlb, nw, fast):
    L = hq.shape[0]
    nc = L // Q

    def chunk(exact, ck, hq_ref, hf_ref, hi_ref, hg_ref, hlb_ref, nw_ref, ob_ref, o_ref, st_ref, s_ref, *tmp):
        tok = slice(Q * ck, Q * ck + Q)
        qf, kf, _, _, _, _, bcum = _hg_gates(hq_ref[tok, :].astype(F32), hf_ref[tok, :], hlb_ref[...])
        gate = hg_ref[tok, :].astype(F32)
        heads = [slice(128 * h, 128 * h + 128) for h in range(NH_HG)]
        if exact:
            atts = [_hg_att_exact(bcum[:, sl], qf[:, sl], kf[:, sl], *tmp).astype(BF) for sl in heads]
        else:
            atts = [_hg_intra(bcum[:, sl], qf[:, sl], kf[:, sl])[0].astype(BF) for sl in heads]
        olds = [s_ref[sl, :] for sl in heads]
        outs_ = [_dot(att, hi_ref[tok, sl]) + _dot(qf[:, sl] * jnp.exp(bcum[:, sl]), s)
                 for att, sl, s in zip(atts, heads, olds)]
        for sl, s, o in zip(heads, olds, outs_):
            b, k = bcum[:, sl], kf[:, sl]
            st_ref[ck, sl, :] = s
            blast = b[Q - 1:Q, :]
            s_ref[sl, :] = s * jnp.exp(b.T[:, Q - 1:Q]) + _dot_tn(k * jnp.exp(blast - b), hi_ref[tok, sl])
            ob = o.astype(BF)
            o_ref[tok, sl] = ob
            on, _, _ = _rms(ob.astype(F32), nw_ref[...])
            gt = gate[:, sl]
            ob_ref[tok, sl] = (on * gt * _sigmoid(gt)).astype(BF)

    def run(exact, *refs):
        for ck in range(CH):
            chunk(exact, ck, *refs)

    def body(fast_ref, *refs):
        _zero_first(refs[9])
        pl.when(fast_ref[0] == 1)(lambda: run(False, *refs))
        pl.when(fast_ref[0] != 1)(lambda: run(True, *refs))

    rows = pl.BlockSpec((CH * Q, D), lambda i, f: (i, 0))
    outs = [SDS((L, D), BF), SDS((L, D), BF), SDS((nc, D, 128), F32)]
    grid_spec = pltpu.PrefetchScalarGridSpec(
        num_scalar_prefetch=1, grid=(nc // CH,),
        in_specs=[rows] * 4 + [pl.BlockSpec((2, D), lambda i, f: (0, 0)), pl.BlockSpec((1, 128), lambda i, f: (0, 0))],
        out_specs=[rows, rows, pl.BlockSpec((CH, D, 128), lambda i, f: (i, 0, 0))],
        scratch_shapes=[pltpu.VMEM((D, 128), F32), pltpu.VMEM((Q, 128), F32), pltpu.VMEM((Q, 128), F32),
                        pltpu.VMEM((Q, Q), F32)])
    return pl.pallas_call(body, grid_spec=grid_spec, name="hg_fwd", out_shape=outs,
                          compiler_params=_cparams())(fast, hq, hf, hi, hg, hlb, nw)


def _hg_bwd(dob, o, hq, hf, hi, hg, states, hlb, nw, fast, phases=()):
    L = dob.shape[0]
    nc = L // Q

    def chunk(exact, ck, step, dob_ref, o_ref, hq_ref, hf_ref, hi_ref, hg_ref, st_ref, hlb_ref, nw_ref,
              dhq_ref, dhf_ref, dhi_ref, dhg_ref, acc_ref, gs_ref, *tmp):
        tok = slice(Q * ck, Q * ck + Q)
        hqv = hq_ref[tok, :].astype(F32)
        qf, kf, fg, sg, lb, sm1, bcum = _hg_gates(hqv, hf_ref[tok, :], hlb_ref[...])
        gate = hg_ref[tok, :].astype(F32)
        sgg = _sigmoid(gate)
        nwv = nw_ref[...]
        tri_t = (_iota((Q, Q), 1) >= _iota((Q, Q), 0)).astype(BF)
        ones8 = jnp.ones((8, 128), BF)
        heads = [slice(128 * h, 128 * h + 128) for h in range(NH_HG)]
        row_last = _iota((Q, 1), 0) == Q - 1
        dobs, dnws = [], []
        for sl in heads:
            gt, sgt = gate[:, sl], sgg[:, sl]
            _, n, r = _rms(o_ref[tok, sl].astype(F32), nwv)
            dobv = dob_ref[tok, sl].astype(F32)
            dhg_ref[tok, sl] = (dobv * n * nwv * sgt * (1.0 + gt * (1.0 - sgt))).astype(BF)
            do, dw = _rms_bwd(dobv * gt * sgt, n, r, nwv)
            dnws.append(_colsum(dw))
            dobs.append(do.astype(BF))
        causal = _iota((Q, Q), 1) <= _iota((Q, Q), 0)
        if exact:
            intra = [(_hg_att_exact(bcum[:, sl], qf[:, sl], kf[:, sl], *tmp[:3]),) for sl in heads]
        else:
            intra = [_hg_intra(bcum[:, sl], qf[:, sl], kf[:, sl]) for sl in heads]
        states = [(st_ref[ck, sl, :], gs_ref[sl, :]) for sl in heads]
        das = [jnp.where(causal, _dot_nt(dob_h, hi_ref[tok, sl]), 0.0) for dob_h, sl in zip(dobs, heads)]
        dqhats = [_dot_nt(dob_h, s) for dob_h, (s, _) in zip(dobs, states)]
        dkhats = [_dot_nt(hi_ref[tok, sl], gst) for sl, (_, gst) in zip(heads, states)]
        if not exact:
            dqts = [jnp.dot(da.astype(BF), it[2], preferred_element_type=F32) for da, it in zip(das, intra)]
            dkts = [lax.dot_general(da.astype(BF), it[1], (((0,), (0,)), ((), ())), preferred_element_type=F32)
                    for da, it in zip(das, intra)]
        dqs, dks, dgls = [], [], []
        for h, sl in enumerate(heads):
            b, q, k = bcum[:, sl], qf[:, sl], kf[:, sl]
            att = intra[h][0]
            s, gst = states[h]
            dob_h, dqhat, dkhat = dobs[h], dqhats[h], dkhats[h]
            eb = jnp.exp(b)
            blast = b[Q - 1:Q, :]
            ekl = jnp.exp(blast - b)
            qhat, khat = q * eb, k * ekl
            dhi_ref[tok, sl] = (_dot_tn(att, dob_h) + _dot(khat, gst)).astype(BF)
            if exact:
                dq_i, dk_i, db = _hg_att_exact_bwd(das[h], b, q, k, *tmp)
            else:
                dq_i, dk_i, db = _hg_intra_bwd(dqts[h], dkts[h], *intra[h][1:4])
            dqs.append(dq_i + dqhat * eb)
            dks.append(dk_i + dkhat * ekl)
            qhat_r, khat_r = qhat.astype(BF).astype(F32), khat.astype(BF).astype(F32)
            decay_row = sum(_dot_nt(ones8, part) for part in _split(gst * s, 2))[0:1, :]
            dblast = _colsum(dkhat * khat_r) + decay_row * jnp.exp(blast)
            dgls.append(db + qhat_r * dqhat - khat_r * dkhat + jnp.where(row_last, dblast, 0.0))
            gs_ref[sl, :] = _dot_tn(qhat, dob_h) + gst * jnp.exp(b.T[:, Q - 1:Q])
        dq, dk, db = (jnp.concatenate(t, axis=1) for t in (dqs, dks, dgls))
        dgl = _dot_sel(tri_t, db, 2)
        sgq = _sigmoid(hqv)
        dhq_ref[tok, :] = (dq * sgq * (1.0 + hqv * (1.0 - sgq))).astype(BF)
        dfg = dgl / fg - dk
        dhf_ref[tok, :] = (dfg * (1.0 - lb) * sg * (1.0 - sg)).astype(BF)
        acc_ref[0:1, :] += _colsum(dfg * (1.0 - sg))
        acc_ref[1:2, :] += jnp.concatenate(dnws, axis=1)

        @pl.when(jnp.logical_and(step == nc // CH - 1, ck == 0))
        def _():
            dlb = acc_ref[0:1, :] * lb * sm1
            acc_ref[2:3, :] = dlb
            acc_ref[3:4, :] = -dlb
            tot = acc_ref[1:2, 0:128]
            for h in range(1, NH_HG):
                tot = tot + acc_ref[1:2, 128 * h:128 * h + 128]
            acc_ref[4:5, 0:128] = tot

    def run(exact, step, *refs):
        for ck in reversed(range(CH)):
            chunk(exact, ck, step, *refs)

    def body(fast_ref, *refs):
        step = pl.program_id(0)
        _zero_first(refs[13], refs[14])
        pl.when(fast_ref[0] == 1)(lambda: run(False, step, *refs))
        pl.when(fast_ref[0] != 1)(lambda: run(True, step, *refs))

    rev = pl.BlockSpec((CH * Q, D), lambda i, f: (nc // CH - 1 - i, 0))
    outs = [SDS((L, D), BF)] * 4 + [SDS((8, D), F32)]
    return _call(
        body, (fast, dob, o, hq, hf, hi, hg, states, hlb, nw), name="hg_bwd", grid=(nc // CH,), prefetch=1,
        in_specs=[rev] * 6 + [pl.BlockSpec((CH, D, 128), lambda i, f: (nc // CH - 1 - i, 0, 0)),
                              pl.BlockSpec((2, D), lambda i, f: (0, 0)), pl.BlockSpec((1, 128), lambda i, f: (0, 0))],
        out_specs=[rev] * 4 + [pl.BlockSpec((8, D), lambda i, f: (0, 0))], out_shape=outs,
        scratch_shapes=[pltpu.VMEM((D, 128), F32), pltpu.VMEM((Q, 128), F32), pltpu.VMEM((Q, 128), F32),
                        pltpu.VMEM((Q, Q), F32), pltpu.VMEM((Q, 128), F32), pltpu.VMEM((Q, 128), F32)], phases=phases)


def _place():
    return lax.axis_index("x"), lax.axis_index("y"), lax.axis_index("c")


def _phase_io(phase):
    kind, arrays, halves = phase
    n = len(arrays)
    dma = pltpu.SemaphoreType.DMA
    if kind == "gather":
        outs = [SDS((8,) + a.shape if hc is None else (4,) + a.shape, a.dtype) for a, hc in zip(arrays, halves)]
        return outs, [dma((7 * n,)), dma((7 * n,)), dma((n,))], {}
    if kind == "sibling":
        return [SDS((4, g.shape[1], hc), g.dtype) for g, hc in zip(arrays, halves)], [dma((n,)), dma((n,))], {}
    if kind == "chips":
        return [SDS((3,) + p.shape[1:], p.dtype) for p in arrays], [dma((3 * n,)), dma((3 * n,))], {}
    assert kind == "swap"
    return [SDS(b.shape, b.dtype) for b in arrays], [dma((n,)), dma((n,))], {a: a for a in range(n)}


def _gather_events(ins, outs, sems, halves):
    send_sems, recv_sems, local_sems = sems
    n = len(ins)

    def parts(a):
        x, y, c = _place()
        hc = halves[a]
        me, sibling = (x, y, c), (x, y, 1 - c)
        chips = [(1 - x, y), (x, 1 - y), (1 - x, 1 - y)]

        def slot(p):
            if hc is None:
                return outs[a].at[4 * p[0] + 2 * p[1] + p[2]]
            return outs[a].at[2 * p[0] + p[1], :, pl.ds(p[2] * hc, hc)]

        own = ins[a] if hc is None else ins[a].at[:, pl.ds(c * hc, hc)]

        def copy(k, piece, to, src=None):
            return pltpu.make_async_remote_copy(
                src_ref=slot(piece) if src is None else src, dst_ref=slot(piece),
                send_sem=send_sems.at[7 * a + k], recv_sem=recv_sems.at[7 * a + k], device_id=to, device_id_type=MESH)

        return dict(
            mine=lambda: pltpu.make_async_copy(own, slot(me), local_sems.at[a]),
            starts=lambda: [copy(0, me, sibling, src=own)] + [copy(1 + j, me, (*chip, c), src=own)
                                                               for j, chip in enumerate(chips)],
            arrive=lambda: [copy(1 + j, (*chip, c), me) for j, chip in enumerate(chips)],
            passed=lambda: [copy(4 + j, (*chip, c), sibling) for j, chip in enumerate(chips)],
            from_sibling=lambda: [copy(0, sibling, me)] + [copy(4 + j, (*chip, 1 - c), me)
                                                            for j, chip in enumerate(chips)])

    def first():
        for a in range(n):
            p = parts(a)
            p["mine"]().start()
            for cp in p["starts"]():
                cp.start()

    def mid():
        for a in range(n):
            p = parts(a)
            for cp_in, cp_out in zip(p["arrive"](), p["passed"]()):
                cp_in.wait_recv()
                cp_out.start()

    def last():
        for a in range(n):
            p = parts(a)
            for cp in p["from_sibling"]():
                cp.wait_recv()
            for cp in p["starts"]() + p["passed"]():
                cp.wait_send()
            p["mine"]().wait()

    return dict(first=first, mid=mid, last=last)


def _exchange_events(kind, ins, outs, sems, halves):
    send_sems, recv_sems = sems
    n = len(outs)

    def copies():
        x, y, c = _place()
        if kind == "sibling":
            return [pltpu.make_async_remote_copy(
                src_ref=ins[a].at[:, :, pl.ds((1 - c) * halves[a], halves[a])], dst_ref=outs[a],
                send_sem=send_sems.at[a], recv_sem=recv_sems.at[a], device_id=(x, y, 1 - c), device_id_type=MESH)
                for a in range(n)]
        chips = [(1 - x, y), (x, 1 - y), (1 - x, 1 - y)]
        return [pltpu.make_async_remote_copy(
            src_ref=ins[a].at[2 * px + py], dst_ref=outs[a].at[k], send_sem=send_sems.at[3 * a + k],
            recv_sem=recv_sems.at[3 * a + k], device_id=(px, py, c), device_id_type=MESH)
            for a in range(n) for k, (px, py) in enumerate(chips)]

    def first():
        for cp in copies():
            cp.start()

    def last():
        for cp in copies():
            cp.wait()

    return dict(first=first, last=last)


def _swap_events(outs, sems, halves):
    send_sems, recv_sems = sems
    n = len(outs)

    def copy(a, landing):
        x, y, c = _place()
        cols = lambda which: outs[a].at[:, pl.ds(which * halves[a], halves[a])]
        return pltpu.make_async_remote_copy(
            src_ref=cols(c), dst_ref=cols(1 - c) if landing else cols(c), send_sem=send_sems.at[a],
            recv_sem=recv_sems.at[a], device_id=(x, y, 1 - c), device_id_type=MESH)

    def first():
        for a in range(n):
            copy(a, False).start()

    def last():
        for a in range(n):
            copy(a, True).wait_recv()
        for a in range(n):
            copy(a, False).wait_send()

    return dict(first=first, last=last)


def _phase_events(phase, ins, outs, sems):
    kind, _, halves = phase
    if kind == "gather":
        return _gather_events(ins, outs, sems, halves)
    if kind == "swap":
        return _swap_events(outs, sems, halves)
    return _exchange_events(kind, ins, outs, sems, halves)


def _split_refs(refs, counts):
    out, at = [], 0
    for c in counts:
        out.append(list(refs[at:at + c]))
        at += c
    return out


def _comm_plumbing(phases, first_in, first_out):
    ios = [_phase_io(p) for p in phases]
    arrays = [a for p in phases for a in p[1]]
    out_shape = [o for io in ios for o in io[0]]
    sem_shapes = [s for io in ios for s in io[1]]
    aliases, ai, ao = {}, first_in, first_out
    for p, io in zip(phases, ios):
        aliases.update({ai + k: ao + v for k, v in io[2].items()})
        ai, ao = ai + len(p[1]), ao + len(io[0])

    def events(cins, couts, sems):
        evs = [_phase_events(p, i, o, s) for p, i, o, s in zip(
            phases, _split_refs(cins, [len(p[1]) for p in phases]), _split_refs(couts, [len(io[0]) for io in ios]),
            _split_refs(sems, [len(io[1]) for io in ios]))]

        def run(key):
            for ev in evs:
                if key in ev:
                    ev[key]()

        return {key: (lambda key=key: run(key)) for key in ("first", "mid", "last")}

    def regroup(flat):
        return _split_refs(flat, [len(io[0]) for io in ios])

    return arrays, out_shape, sem_shapes, aliases, events, regroup


def _run_phases(phases, name):
    arrays, out_shape, sem_shapes, aliases, events, regroup = _comm_plumbing(phases, 0, 0)

    def body(*refs):
        cins, couts, sems = _split_refs(refs, [len(arrays), len(out_shape), len(sem_shapes)])
        ev = events(cins, couts, sems)
        for key in ("first", "mid", "last"):
            ev[key]()

    outs = pl.pallas_call(
        body, name=name, in_specs=[ANY] * len(arrays), out_specs=[ANY] * len(out_shape), out_shape=out_shape,
        scratch_shapes=sem_shapes, input_output_aliases=aliases)(*arrays)
    return regroup(outs)


def _call(body, args, *, name, grid, in_specs, out_specs, out_shape, scratch_shapes=(), prefetch=0, phases=(),
          mid_step=None):
    steps = grid[0]
    arrays, c_shape, sem_shapes, aliases, events, regroup = _comm_plumbing(
        phases, prefetch + len(in_specs), len(out_specs))
    counts = [prefetch, len(in_specs), len(arrays), len(out_specs), len(c_shape), len(scratch_shapes), len(sem_shapes)]

    def wrapped(*refs):
        pre, ins, cins, outs, couts, scratch, sems = _split_refs(refs, counts)
        if not phases:
            return body(*pre, *ins, *outs, *scratch)
        step = pl.program_id(0)
        ev = events(cins, couts, sems)
        pl.when(step == 0)(ev["first"])
        body(*pre, *ins, *outs, *scratch)
        pl.when(step == (steps // 2 if mid_step is None else mid_step))(ev["mid"])
        pl.when(step == steps - 1)(ev["last"])

    grid_spec = pltpu.PrefetchScalarGridSpec(
        num_scalar_prefetch=prefetch, grid=grid, in_specs=list(in_specs) + [ANY] * len(arrays),
        out_specs=list(out_specs) + [ANY] * len(c_shape), scratch_shapes=list(scratch_shapes) + sem_shapes)
    outs = pl.pallas_call(
        wrapped, grid_spec=grid_spec, name=name, out_shape=list(out_shape) + c_shape, input_output_aliases=aliases,
        compiler_params=_cparams())(*args, *arrays)
    return list(outs[:len(out_specs)]), regroup(outs[len(out_specs):])


def _tile(rows, cols, nbuf):
    budget = (VMEM_LIMIT // 3) // (2 * nbuf * 4)
    if rows % 8 == 0:
        cands = [t for t in range(8, rows + 1, 8) if rows % t == 0 and t * cols <= budget]
        pref = [t for t in cands if t % 16 == 0]
        return (max(pref) if pref else max(cands) if cands else 8), cols
    cands = [t for t in range(128, cols + 1, 128) if cols % t == 0 and rows * t <= budget]
    return rows, (max(cands) if cands else 128)


def _chip_sum(g, from_sib, place, name):
    _, rows, hc = from_sib.shape
    tr, tc = _tile(rows, hc, 4)
    ni, nj = rows // tr, hc // tc

    def body(p_ref, g_ref, s_ref, hb_ref, own_ref):
        s = g_ref[...] + s_ref[...]
        hb_ref[...] = s.astype(BF)

        @pl.when(pl.program_id(2) == p_ref[1])
        def _():
            own_ref[...] = s

    grid_spec = pltpu.PrefetchScalarGridSpec(
        num_scalar_prefetch=1, grid=(ni, nj, 4),
        in_specs=[pl.BlockSpec((None, tr, tc), lambda i, j, k, p: (k, i, p[0] * nj + j)),
                  pl.BlockSpec((None, tr, tc), lambda i, j, k, p: (k, i, j))],
        out_specs=[pl.BlockSpec((None, tr, tc), lambda i, j, k, p: (k, i, j)),
                   pl.BlockSpec((tr, tc), lambda i, j, k, p: (i, j))])
    return pl.pallas_call(
        body, grid_spec=grid_spec, name=name, out_shape=[SDS((4, rows, hc), BF), SDS((rows, hc), F32)],
        compiler_params=pltpu.CompilerParams(dimension_semantics=("arbitrary",) * 3,
                                             vmem_limit_bytes=VMEM_LIMIT))(place, g, from_sib)


def _total(own, parts, place, name):
    rows, hc = own.shape
    tr, tc = _tile(rows, hc, 5)
    ni, nj = rows // tr, hc // tc

    def body(p_ref, own_ref, parts_ref, o_ref):
        s = own_ref[...]
        for k in range(3):
            s = s + parts_ref[k].astype(F32)
        o_ref[...] = s

    grid_spec = pltpu.PrefetchScalarGridSpec(
        num_scalar_prefetch=1, grid=(ni, nj),
        in_specs=[pl.BlockSpec((tr, tc), lambda i, j, p: (i, j)),
                  pl.BlockSpec((3, tr, tc), lambda i, j, p: (0, i, j))],
        out_specs=pl.BlockSpec((tr, tc), lambda i, j, p: (i, p[0] * nj + j)))
    return pl.pallas_call(
        body, grid_spec=grid_spec, name=name, out_shape=SDS((rows, 2 * hc), F32),
        compiler_params=pltpu.CompilerParams(dimension_semantics=("arbitrary",) * 2,
                                             vmem_limit_bytes=VMEM_LIMIT))(place, own, parts)


def _sum8(parts, name):
    R = parts.shape[1]

    def body(p_ref, o_ref):
        s = p_ref[0]
        for k in range(1, 8):
            s = s + p_ref[k]
        o_ref[...] = s

    return pl.pallas_call(
        body, grid=(1,), name=name, in_specs=[_const((8, R, 128))], out_specs=_const((R, 128)),
        out_shape=SDS((R, 128), F32), compiler_params=_cparams())(parts)


def _adamw(w, g, m, v, name):
    _, R, C = w.shape
    tr, tc = _tile(R, C, 8)
    c1 = 1.0 / (1.0 - ADAM_B1 ** ADAM_STEP)
    c2 = 1.0 / (1.0 - ADAM_B2 ** ADAM_STEP)

    def body(w_ref, g_ref, m_ref, v_ref, go_ref, d_ref, nm_ref, nv_ref):
        gv = g_ref[...]
        go_ref[...] = gv
        nm = ADAM_B1 * m_ref[...] + (1.0 - ADAM_B1) * gv
        nv = ADAM_B2 * v_ref[...] + (1.0 - ADAM_B2) * gv * gv
        nm_ref[...] = nm
        nv_ref[...] = nv
        d_ref[...] = -ADAM_LR * ((nm * c1) / (jnp.sqrt(nv * c2) + ADAM_EPS) + ADAM_WD * w_ref[...])

    blk3 = pl.BlockSpec((None, tr, tc), lambda i, j: (0, i, j))
    return pl.pallas_call(
        body, grid=(R // tr, C // tc), name=name,
        in_specs=[blk3, pl.BlockSpec((tr, tc), lambda i, j: (i, j)), blk3, blk3], out_specs=[blk3] * 4,
        out_shape=[SDS((1, R, C), F32)] * 4,
        compiler_params=pltpu.CompilerParams(dimension_semantics=("arbitrary",) * 2,
                                             vmem_limit_bytes=VMEM_LIMIT))(w, g, m, v)


def _pack_small(parts):
    rows = []
    for p in parts:
        p = p.reshape(-1)
        rows.append(jnp.pad(p, (0, (-p.shape[0]) % 128)).reshape(-1, 128))
    out = jnp.concatenate(rows, axis=0)
    return jnp.pad(out, ((0, (-out.shape[0]) % 8), (0, 0)))


def _unpack_small(packed, shapes):
    out, row = [], 0
    for shp in shapes:
        n = 1
        for s in shp:
            n *= s
        nr = -(-n // 128)
        out.append(packed[row:row + nr].reshape(-1)[:n].reshape(shp))
        row += nr
    return out


def _pad_lanes(v, n=128):
    return jnp.pad(v, ((0, 0), (0, n - v.shape[1])))


GROUP_FFN = ("ffn_w_gate", "ffn_w_up", "ffn_w_down")
GROUP_ATTN = ("w_out", "xa_wq", "xa_wkv", "xa_wo")


def kernel(x, mem, norm_mix_w, w_in, conv_w, conv_b, dt_bias, a_log, d_skip, ssd_norm_w, hg_lower_bounds, hg_norm_w, w_out, norm_xa_w, norm_mem_w, xa_wq, xa_wkv, xa_wo, norm_ffn_w, ffn_w_gate, ffn_w_up, ffn_w_down, norm_final_w, loss_target, m_norm_mix_w, m_w_in, m_conv_w, m_conv_b, m_dt_bias, m_a_log, m_d_skip, m_ssd_norm_w, m_hg_lower_bounds, m_hg_norm_w, m_w_out, m_norm_xa_w, m_norm_mem_w, m_xa_wq, m_xa_wkv, m_xa_wo, m_norm_ffn_w, m_ffn_w_gate, m_ffn_w_up, m_ffn_w_down, m_norm_final_w, v_norm_mix_w, v_w_in, v_conv_w, v_conv_b, v_dt_bias, v_a_log, v_d_skip, v_ssd_norm_w, v_hg_lower_bounds, v_hg_norm_w, v_w_out, v_norm_xa_w, v_norm_mem_w, v_xa_wq, v_xa_wkv, v_xa_wo, v_norm_ffn_w, v_ffn_w_gate, v_ffn_w_up, v_ffn_w_down, v_norm_final_w):
    w = dict(norm_mix_w=norm_mix_w, w_in=w_in, conv_w=conv_w, conv_b=conv_b, dt_bias=dt_bias, a_log=a_log, d_skip=d_skip,
             ssd_norm_w=ssd_norm_w, hg_lower_bounds=hg_lower_bounds, hg_norm_w=hg_norm_w, w_out=w_out,
             norm_xa_w=norm_xa_w, norm_mem_w=norm_mem_w, xa_wq=xa_wq, xa_wkv=xa_wkv, xa_wo=xa_wo, norm_ffn_w=norm_ffn_w,
             ffn_w_gate=ffn_w_gate, ffn_w_up=ffn_w_up, ffn_w_down=ffn_w_down, norm_final_w=norm_final_w)
    m = dict(norm_mix_w=m_norm_mix_w, w_in=m_w_in, conv_w=m_conv_w, conv_b=m_conv_b, dt_bias=m_dt_bias, a_log=m_a_log,
             d_skip=m_d_skip, ssd_norm_w=m_ssd_norm_w, hg_lower_bounds=m_hg_lower_bounds, hg_norm_w=m_hg_norm_w,
             w_out=m_w_out, norm_xa_w=m_norm_xa_w, norm_mem_w=m_norm_mem_w, xa_wq=m_xa_wq, xa_wkv=m_xa_wkv,
             xa_wo=m_xa_wo, norm_ffn_w=m_norm_ffn_w, ffn_w_gate=m_ffn_w_gate, ffn_w_up=m_ffn_w_up,
             ffn_w_down=m_ffn_w_down, norm_final_w=m_norm_final_w)
    v = dict(norm_mix_w=v_norm_mix_w, w_in=v_w_in, conv_w=v_conv_w, conv_b=v_conv_b, dt_bias=v_dt_bias, a_log=v_a_log,
             d_skip=v_d_skip, ssd_norm_w=v_ssd_norm_w, hg_lower_bounds=v_hg_lower_bounds, hg_norm_w=v_hg_norm_w,
             w_out=v_w_out, norm_xa_w=v_norm_xa_w, norm_mem_w=v_norm_mem_w, xa_wq=v_xa_wq, xa_wkv=v_xa_wkv,
             xa_wo=v_xa_wo, norm_ffn_w=v_norm_ffn_w, ffn_w_gate=v_ffn_w_gate, ffn_w_up=v_ffn_w_up,
             ffn_w_down=v_ffn_w_down, norm_final_w=v_norm_final_w)
    xi, yi, ci = _place()
    chip = 2 * xi + yi
    place = jnp.stack([ci, chip]).astype(jnp.int32)

    def shard(t, name):
        return jnp.swapaxes(t[name], 1, 2) if name in TRANSPOSED else t[name]

    wsh = {name: shard(w, name) for name in BIG}
    half = {name: wsh[name].shape[2] // 2 for name in BIG}
    payload = {name: wsh[name][0].astype(BF) for name in BIG}
    ws = {name: w[name] for name in SMALL}
    xs, mems, tgt = x[0], mem[0], loss_target[0]

    def chip_sums(names, grads, from_sib):
        return [_chip_sum(grads[n], s, place, "grads_chip_sum_" + n) for n, s in zip(names, from_sib)]

    def totals(names, sums, others):
        return [_total(own, o, place, "grads_total_" + n) for n, (_, own), o in zip(names, sums, others)]

    ((w_in4, conv_all),) = _run_phases([("gather", [payload["w_in"], conv_w[0]], [half["w_in"], None])], "gather_w_in")
    w_in_t = w_in4.reshape(N_IN, D)
    ws["conv_w"] = conv_all[0::2].transpose(1, 0, 2).reshape(1, 4, 1536)
    rest = [n for n in BIG if n != "w_in"]
    (h0, z, xbc, hq, hf, hi, hg, dtr), (gathered,) = _in_proj(
        xs, ws["norm_mix_w"], w_in_t, phases=[("gather", [payload[n] for n in rest], [half[n] for n in rest])])
    wg = dict(zip(rest, gathered))
    wg_t, wu_t = wg["ffn_w_gate"].reshape(FFN, D), wg["ffn_w_up"].reshape(FFN, D)
    wd = wg["ffn_w_down"].reshape(FFN, D)
    w_out_f = wg["w_out"].reshape(2 * D, D)
    wq, wo = wg["xa_wq"].reshape(D, D), wg["xa_wo"].reshape(D, D)
    dtb, alog = _pad_lanes(ws["dt_bias"]), _pad_lanes(ws["a_log"])
    dskip_full = jnp.repeat(ws["d_skip"], SSD_P, axis=1)
    cw, conv_bias = ws["conv_w"][0], ws["conv_b"]
    hlb = ws["hg_lower_bounds"]
    hg_fast = (jnp.min(jax.nn.softmax(hlb, axis=0)[0]) >= HG_LB_FLOOR).astype(jnp.int32).reshape(1)

    ya, yssd, u, st_ssd = _ssd_fwd(xbc, dtr, z, cw, conv_bias, dtb, alog, dskip_full, ws["ssd_norm_w"])
    ob, ohg, st_hg = _hg_fwd(hq, hf, hi, hg, hlb, ws["hg_norm_w"], hg_fast)
    kmem, vmem = _mem_kv(mems, ws["norm_mem_w"], wg["xa_wkv"])
    x1, x2, hxa, q, ox = _attn_fwd(xs, ya, ob, w_out_f, ws["norm_xa_w"], wq, kmem, vmem, wo)
    nfin = ws["norm_final_w"].reshape(1, D)
    dx2, hffn, act, dx3, dg, du, acc_f = _ffn_loss(x2, tgt, ws["norm_ffn_w"], nfin, wg_t, wu_t, wd)

    gb = {"ffn_w_gate": _matmul_tn(dg, hffn, "gw_gate").reshape(4, FFN // 4, D),
          "ffn_w_up": _matmul_tn(du, hffn, "gw_up").reshape(4, FFN // 4, D),
          "ffn_w_down": _matmul_tn(act, dx3, "gw_down").reshape(4, FFN // 4, D)}
    (dx1, dya, dob, dq, dk, dv, acc_a), (sib_ffn,) = _attn_bwd(
        dx2, x1, q, kmem, vmem, ws["norm_xa_w"], wq, wo, w_out_f,
        phases=[("sibling", [gb[n] for n in GROUP_FFN], [half[n] for n in GROUP_FFN])])
    sums_ffn = chip_sums(GROUP_FFN, gb, sib_ffn)
    g_nmem, gb["xa_wkv"] = _mem_kv_bwd(mems, ws["norm_mem_w"], wg["xa_wkv"], dk, dv)
    gb["w_out"] = _matmul_tn_pair(ya, ob, dx1, "gw_out").reshape(4, D // 2, D)
    gb["xa_wq"] = _matmul_tn(hxa, dq, "gw_q").reshape(4, D // 4, D)
    gb["xa_wo"] = _matmul_tn(ox, dx2, "gw_o").reshape(4, D // 4, D)
    (dhq, dhf, dhi, dhg, acc_h), (others_ffn, sib_attn) = _hg_bwd(
        dob, ohg, hq, hf, hi, hg, st_hg, hlb, ws["hg_norm_w"], hg_fast,
        phases=[("chips", [hb for hb, _ in sums_ffn], None),
                ("sibling", [gb[n] for n in GROUP_ATTN], [half[n] for n in GROUP_ATTN])])
    red_ffn = totals(GROUP_FFN, sums_ffn, others_ffn)
    sums_attn = chip_sums(GROUP_ATTN, gb, sib_attn)
    dz, dxbc, ddt, gconv, ghead, glane = _ssd_bwd(dya, yssd, z, u, xbc, dtr, st_ssd, cw, dtb, alog, dskip_full,
                                                  ws["ssd_norm_w"])
    (gw_in_t,), (g_ffn, others_attn) = _gw_in(
        h0, dz, dxbc, ddt, dhq, dhf, dhi, dhg,
        phases=[("swap", red_ffn, [half[n] for n in GROUP_FFN]), ("chips", [hb for hb, _ in sums_attn], None)])
    red_attn = totals(GROUP_ATTN, sums_attn, others_attn)
    gb["w_in"] = gw_in_t.reshape(4, N_IN // 4, D)
    dproj = (xs, dx1, dz, dxbc, dhq, dhf, dhi, dhg, ddt, ws["norm_mix_w"], w_in_t)
    (gx_a, acc_ia), (g_attn, (sib_in,)) = _in_proj_bwd(
        *dproj, 0, phases=[("swap", red_attn, [half[n] for n in GROUP_ATTN]), ("sibling", [gb["w_in"]], [half["w_in"]])])
    sums_in = chip_sums(("w_in",), gb, [sib_in])
    (gx_b, acc_ib), ((others_in,),) = _in_proj_bwd(*dproj, 1, phases=[("chips", [sums_in[0][0]], None)])
    (gx, acc_ic), _ = _in_proj_bwd(*dproj, 2, done=(gx_a, gx_b))
    gx, acc_i = gx.reshape(xs.shape), acc_ia + acc_ib + acc_ic
    red_in = totals(("w_in",), sums_in, [others_in])

    gs = {
        "norm_mix_w": acc_i[0:1], "conv_w": gconv[0:4][None], "conv_b": gconv[4:5],
        "dt_bias": ghead[0:1, :NH_SSD], "a_log": ghead[2:3, :NH_SSD], "d_skip": ghead[3:4, :NH_SSD],
        "ssd_norm_w": glane[1:2], "hg_lower_bounds": acc_h[2:4], "hg_norm_w": acc_h[4:5, :128],
        "norm_xa_w": acc_a[0:1], "norm_mem_w": g_nmem, "norm_ffn_w": acc_f[2:3], "norm_final_w": acc_f[1],
    }
    loss = (0.5 / D) * jnp.sum(acc_f[0])
    small_parts = [gs[name] for name in SMALL] + [loss.reshape(1)]
    small_shapes = [gs[name].shape for name in SMALL] + [(1,)]
    (g_in,), (packed,) = _run_phases([("swap", red_in, [half["w_in"]]),
                                      ("gather", [_pack_small(small_parts)], [None])], "grads_finish")
    g_big = dict(zip(GROUP_FFN + GROUP_ATTN + ("w_in",), g_ffn + g_attn + [g_in]))
    small = _unpack_small(_sum8(packed, "small_total"), small_shapes)
    g_small = dict(zip(SMALL, small[:-1]))
    loss_all = small[-1][0]
    g_small["conv_w"] = lax.dynamic_slice_in_dim(g_small["conv_w"], chip * 384, 384, 2)

    grads, delta, new_m, new_v = {}, {}, {}, {}
    for name in BIG:
        outs = tuple(_adamw(wsh[name], g_big[name], shard(m, name), shard(v, name), "adamw_" + name))
        if name in TRANSPOSED:
            outs = tuple(jnp.swapaxes(o, 1, 2) for o in outs)
        grads[name], delta[name], new_m[name], new_v[name] = outs
    shapes = [w[name].shape for name in SMALL]
    packs = [_pack_small([t[name] for name in SMALL]) for t in (w, g_small, m, v)]
    outs = _adamw(packs[0][None], packs[1], packs[2][None], packs[3][None], "adamw_small")[1:]
    for name, g_, d_, nm_, nv_ in zip(SMALL, [g_small[n] for n in SMALL], *[_unpack_small(o[0], shapes) for o in outs]):
        grads[name] = g_.reshape(w[name].shape)
        delta[name], new_m[name], new_v[name] = d_, nm_, nv_

    return (loss_all, gx[None], *[grads[n] for n in WEIGHTS], *[delta[n] for n in WEIGHTS],
            *[new_m[n] for n in WEIGHTS], *[new_v[n] for n in WEIGHTS])
```

```python
import jax
import jax.numpy as jnp
from jax import lax
from jax.experimental import pallas as pl
from jax.experimental.pallas import tpu as pltpu

F32 = jnp.float32
BF = jnp.bfloat16
MESH = pl.DeviceIdType.MESH
SDS = jax.ShapeDtypeStruct
ANY = pl.BlockSpec(memory_space=pl.ANY)

D = 1024
EPS = 1e-6
NH_SSD = 16
SSD_P = 64
NH_HG = 8
Q = 128
CH = 2
SSD_CH = 4
SUB = 32
NSUB = Q // SUB
HG_LB_FLOOR = 1e-2
XA_HEADS = 4
XA_HD = 256
MEM_LEN = 256
FFN = 2816
TL = 512
TL_FFN = 256
VMEM_LIMIT = 56 << 20
MATMUL_VMEM = 40 << 20

N_IN = 6672
Z0, XBC0, DT0, HQ0, HF0, HI0, HG0 = 0, 1024, 2560, 2576, 3600, 4624, 5648

ADAM_LR, ADAM_B1, ADAM_B2, ADAM_EPS, ADAM_WD, ADAM_STEP = 0.001, 0.9, 0.999, 1e-08, 0.01, 10

BIG = ("w_in", "w_out", "xa_wq", "xa_wkv", "xa_wo", "ffn_w_gate", "ffn_w_up", "ffn_w_down")
TRANSPOSED = ("w_in", "ffn_w_gate", "ffn_w_up")
SMALL = ("norm_mix_w", "conv_w", "conv_b", "dt_bias", "a_log", "d_skip", "ssd_norm_w", "hg_lower_bounds",
         "hg_norm_w", "norm_xa_w", "norm_mem_w", "norm_ffn_w", "norm_final_w")
WEIGHTS = ("norm_mix_w", "w_in", "conv_w", "conv_b", "dt_bias", "a_log", "d_skip", "ssd_norm_w", "hg_lower_bounds",
           "hg_norm_w", "w_out", "norm_xa_w", "norm_mem_w", "xa_wq", "xa_wkv", "xa_wo", "norm_ffn_w", "ffn_w_gate",
           "ffn_w_up", "ffn_w_down", "norm_final_w")


def _cparams():
    return pltpu.CompilerParams(dimension_semantics=("arbitrary",), vmem_limit_bytes=VMEM_LIMIT)


def _const(shape):
    return pl.BlockSpec(shape, lambda i: (0,) * len(shape))


def _resident(shape):
    return pl.BlockSpec(shape, lambda i: (0,) * len(shape), pipeline_mode=pl.Buffered(1))


def _rows(tl, n):
    return pl.BlockSpec((tl, n), lambda i: (i, 0))


def _dot(a, b):
    return jnp.dot(a.astype(BF), b.astype(BF), preferred_element_type=F32)


def _dot_nt(a, b):
    return lax.dot_general(a.astype(BF), b.astype(BF), (((1,), (1,)), ((), ())), preferred_element_type=F32)


def _dot_tn(a, b):
    return lax.dot_general(a.astype(BF), b.astype(BF), (((0,), (0,)), ((), ())), preferred_element_type=F32)


def _split(v, passes):
    parts, rest = [], v
    for p in range(passes):
        hi = rest.astype(BF)
        parts.append(hi)
        if p + 1 < passes:
            rest = rest - hi.astype(F32)
    return parts


def _sel_dot(a, sel, passes=3):
    sb = sel.astype(BF)
    out = None
    for part in _split(a, passes):
        t = jnp.dot(part, sb, preferred_element_type=F32)
        out = t if out is None else out + t
    return out


def _dot_sel(sel, b, passes=3):
    sb = sel.astype(BF)
    out = None
    for part in _split(b, passes):
        t = jnp.dot(sb, part, preferred_element_type=F32)
        out = t if out is None else out + t
    return out


def _iota(shape, dim):
    return lax.broadcasted_iota(jnp.int32, shape, dim)


def _sigmoid(v):
    return 0.5 * jnp.tanh(0.5 * v) + 0.5


def _rms(v, w):
    r = lax.rsqrt(jnp.mean(v * v, axis=-1, keepdims=True) + EPS)
    n = v * r
    return n * w, n, r


def _rms_bwd(dy, n, r, w):
    dn = dy * w
    return r * (dn - n * jnp.mean(dn * n, axis=-1, keepdims=True)), dy * n


def _colsum(v):
    return jnp.sum(v, axis=0, keepdims=True)


def _zero_first(*refs):
    @pl.when(pl.program_id(0) == 0)
    def _():
        for r in refs:
            r[...] = jnp.zeros_like(r)


def _in_proj(x, nw, wt, phases=()):
    L = x.shape[0]
    tl = min(TL, L)

    def body(x_ref, nw_ref, w_ref, h0_ref, z_ref, xbc_ref, hq_ref, hf_ref, hi_ref, hg_ref, dt_ref):
        h, _, _ = _rms(x_ref[...], nw_ref[...])
        hb = h.astype(BF)
        h0_ref[...] = hb

        def proj(a, b):
            return _dot_nt(hb, w_ref[a:b, :])

        z_ref[...] = proj(Z0, XBC0).astype(BF)
        xbc_ref[...] = proj(XBC0, DT0).astype(BF)
        dt_ref[...] = proj(DT0, DT0 + 128)
        hq_ref[...] = proj(HQ0, HF0).astype(BF)
        hf_ref[...] = proj(HF0, HI0)
        hi_ref[...] = proj(HI0, HG0).astype(BF)
        hg_ref[...] = proj(HG0, N_IN).astype(BF)

    outs = [SDS((L, D), BF), SDS((L, D), BF), SDS((L, 1536), BF), SDS((L, D), BF), SDS((L, D), F32),
            SDS((L, D), BF), SDS((L, D), BF), SDS((L, 128), F32)]
    steps = L // tl
    return _call(body, (x, nw, wt), name="in_proj", grid=(steps,),
                 in_specs=[_rows(tl, D), _const((1, D)), _resident((N_IN, D))],
                 out_specs=[_rows(tl, o.shape[1]) for o in outs], out_shape=outs, phases=phases,
                 mid_step=(3 * steps) // 4)


def _mem_kv(mem, nw, wkv4):
    def body(m_ref, nw_ref, w_ref, k_ref, v_ref):
        m, _, _ = _rms(m_ref[...], nw_ref[...])
        mb = m.astype(BF)
        for i in range(2):
            sl = slice(512 * i, 512 * i + 512)
            k_ref[:, sl] = jnp.dot(mb, w_ref[i], preferred_element_type=F32).astype(BF)
            v_ref[:, sl] = jnp.dot(mb, w_ref[2 + i], preferred_element_type=F32).astype(BF)

    outs = [SDS((MEM_LEN, D), BF)] * 2
    return pl.pallas_call(
        body, grid=(1,), name="mem_kv",
        in_specs=[_const((MEM_LEN, D)), _const((1, D)), _const((4, D, 512))],
        out_specs=[_const((MEM_LEN, D))] * 2, out_shape=outs, compiler_params=_cparams())(mem, nw, wkv4)


def _mem_kv_bwd(mem, nw, wkv4, dk, dv):
    def body(m_ref, nw_ref, w_ref, dk_ref, dv_ref, gnw_ref, gw_ref):
        m, n, _ = _rms(m_ref[...], nw_ref[...])
        mb = m.astype(BF)
        dm = jnp.zeros((MEM_LEN, D), F32)
        for i in range(4):
            src = dk_ref if i < 2 else dv_ref
            d = src[:, 512 * (i % 2):512 * (i % 2) + 512].astype(BF)
            gw_ref[i] = _dot_tn(mb, d)
            dm = dm + _dot_nt(d, w_ref[i])
        gnw_ref[...] = _colsum(dm * n)

    return pl.pallas_call(
        body, grid=(1,), name="mem_kv_bwd",
        in_specs=[_const((MEM_LEN, D)), _const((1, D)), _const((4, D, 512)), _const((MEM_LEN, D)), _const((MEM_LEN, D))],
        out_specs=[_const((1, D)), _const((4, D, 512))],
        out_shape=[SDS((1, D), F32), SDS((4, D, 512), F32)], compiler_params=_cparams())(mem, nw, wkv4, dk, dv)


def _softmax_rows(sc):
    e = jnp.exp(sc - jnp.max(sc, axis=-1, keepdims=True))
    return e * (1.0 / jnp.sum(e, axis=-1, keepdims=True))


def _attn_fwd(x, ya, ob, w_out, nxa, wq, k, v, wo):
    L = x.shape[0]
    tl = min(TL, L)
    scale = XA_HD ** -0.5

    def body(x_ref, ya_ref, ob_ref, wout_ref, nxa_ref, wq_ref, k_ref, v_ref, wo_ref,
             x1_ref, x2_ref, hxa_ref, q_ref, ox_ref):
        x1 = x_ref[...] + jnp.dot(ya_ref[...], wout_ref[:D, :], preferred_element_type=F32) \
            + jnp.dot(ob_ref[...], wout_ref[D:, :], preferred_element_type=F32)
        x1_ref[...] = x1
        h, _, _ = _rms(x1, nxa_ref[...])
        hb = h.astype(BF)
        hxa_ref[...] = hb
        qb = jnp.dot(hb, wq_ref[...], preferred_element_type=F32).astype(BF)
        q_ref[...] = qb
        heads = [slice(hd * XA_HD, (hd + 1) * XA_HD) for hd in range(XA_HEADS)]
        ps = [_softmax_rows(_dot_nt(qb[:, sl], k_ref[:, sl]) * scale) for sl in heads]
        oxs = [_dot(p, v_ref[:, sl]) for p, sl in zip(ps, heads)]
        oxb = jnp.concatenate(oxs, axis=1).astype(BF)
        ox_ref[...] = oxb
        x2_ref[...] = x1 + jnp.dot(oxb, wo_ref[...], preferred_element_type=F32)

    outs = [SDS((L, D), F32), SDS((L, D), F32), SDS((L, D), BF), SDS((L, D), BF), SDS((L, D), BF)]
    return pl.pallas_call(
        body, grid=(L // tl,), name="attn_fwd",
        in_specs=[_rows(tl, D), _rows(tl, D), _rows(tl, D), _resident((2 * D, D)), _const((1, D)), _resident((D, D)),
                  _resident((MEM_LEN, D)), _resident((MEM_LEN, D)), _resident((D, D))],
        out_specs=[_rows(tl, D)] * 5, out_shape=outs, compiler_params=_cparams())(x, ya, ob, w_out, nxa, wq, k, v, wo)


def _ffn_loss(x2, tgt, nffn, nfin, wgt, wut, wd):
    L = x2.shape[0]
    tl = min(TL_FFN, L)

    def body(x2_ref, t_ref, nffn_ref, nfin_ref, wg_ref, wu_ref, wd_ref,
             dx2_ref, h_ref, a_ref, dx3_ref, dg_ref, du_ref, acc_ref):
        _zero_first(acc_ref)
        x2v = x2_ref[...]
        h, n2, r2 = _rms(x2v, nffn_ref[...])
        hb = h.astype(BF)
        h_ref[...] = hb
        g = _dot_nt(hb, wg_ref[...])
        u = _dot_nt(hb, wu_ref[...])
        sg = _sigmoid(g)
        ab = (g * sg * u).astype(BF)
        a_ref[...] = ab
        x3 = x2v + jnp.dot(ab, wd_ref[...], preferred_element_type=F32)
        y, n3, r3 = _rms(x3, nfin_ref[...])
        err = y - t_ref[...]
        acc_ref[0:1, :] += _colsum(err * err)
        dx3, dwf = _rms_bwd(err * (1.0 / D), n3, r3, nfin_ref[...])
        acc_ref[1:2, :] += _colsum(dwf)
        dx3b = dx3.astype(BF)
        dx3_ref[...] = dx3b
        da = _dot_nt(dx3b, wd_ref[...])
        dgb = (da * u * sg * (1.0 + g * (1.0 - sg))).astype(BF)
        dub = (da * g * sg).astype(BF)
        dg_ref[...] = dgb
        du_ref[...] = dub
        dh = jnp.dot(dgb, wg_ref[...], preferred_element_type=F32) + jnp.dot(dub, wu_ref[...], preferred_element_type=F32)
        dn, dwn = _rms_bwd(dh, n2, r2, nffn_ref[...])
        acc_ref[2:3, :] += _colsum(dwn)
        dx2_ref[...] = dx3 + dn

    outs = [SDS((L, D), F32), SDS((L, D), BF), SDS((L, FFN), BF), SDS((L, D), BF), SDS((L, FFN), BF),
            SDS((L, FFN), BF), SDS((8, D), F32)]
    wspec = _resident((FFN, D))
    return pl.pallas_call(
        body, grid=(L // tl,), name="ffn_loss",
        in_specs=[_rows(tl, D), _rows(tl, D), _const((1, D)), _const((1, D)), wspec, wspec, wspec],
        out_specs=[_rows(tl, D), _rows(tl, D), _rows(tl, FFN), _rows(tl, D), _rows(tl, FFN), _rows(tl, FFN),
                   _const((8, D))],
        out_shape=outs, compiler_params=_cparams())(x2, tgt, nffn, nfin, wgt, wut, wd)


def _attn_bwd(dx2, x1, q, k, v, nxa, wq, wo, w_out, phases=()):
    L = dx2.shape[0]
    tl = min(TL, L)
    scale = XA_HD ** -0.5

    def body(dx2_ref, x1_ref, q_ref, k_ref, v_ref, nxa_ref, wq_ref, wo_ref, wout_ref,
             dx1_ref, dya_ref, dob_ref, dq_ref, dk_ref, dv_ref, acc_ref):
        _zero_first(dk_ref, dv_ref, acc_ref)
        dx2v = dx2_ref[...]
        dox = _dot_nt(dx2v, wo_ref[...]).astype(BF)
        qb = q_ref[...]
        heads = [slice(hd * XA_HD, (hd + 1) * XA_HD) for hd in range(XA_HEADS)]
        ps = [_softmax_rows(_dot_nt(qb[:, sl], k_ref[:, sl]) * scale) for sl in heads]
        dps = [_dot_nt(dox[:, sl], v_ref[:, sl]) for sl in heads]
        dss = [(p * (dp - jnp.sum(dp * p, axis=-1, keepdims=True)) * scale).astype(BF) for p, dp in zip(ps, dps)]
        for sl, p, ds in zip(heads, ps, dss):
            dv_ref[:, sl] += _dot_tn(p, dox[:, sl])
            dk_ref[:, sl] += _dot_tn(ds, qb[:, sl])
        dqs = [_dot(ds, k_ref[:, sl]) for sl, ds in zip(heads, dss)]
        dqb = jnp.concatenate(dqs, axis=1).astype(BF)
        dq_ref[...] = dqb
        dh = _dot_nt(dqb, wq_ref[...])
        _, n1, r1 = _rms(x1_ref[...], nxa_ref[...])
        dn, dwn = _rms_bwd(dh, n1, r1, nxa_ref[...])
        acc_ref[0:1, :] += _colsum(dwn)
        dx1 = dx2v + dn
        dx1_ref[...] = dx1
        dx1b = dx1.astype(BF)
        dya_ref[...] = _dot_nt(dx1b, wout_ref[:D, :]).astype(BF)
        dob_ref[...] = _dot_nt(dx1b, wout_ref[D:, :]).astype(BF)

    outs = [SDS((L, D), F32), SDS((L, D), BF), SDS((L, D), BF), SDS((L, D), BF), SDS((MEM_LEN, D), F32),
            SDS((MEM_LEN, D), F32), SDS((8, D), F32)]
    return _call(body, (dx2, x1, q, k, v, nxa, wq, wo, w_out), name="attn_bwd", grid=(L // tl,),
                 in_specs=[_rows(tl, D), _rows(tl, D), _rows(tl, D), _resident((MEM_LEN, D)), _resident((MEM_LEN, D)),
                           _const((1, D)), _resident((D, D)), _resident((D, D)), _resident((2 * D, D))],
                 out_specs=[_rows(tl, D)] * 4 + [_const((MEM_LEN, D)), _const((MEM_LEN, D)), _const((8, D))],
                 out_shape=outs, phases=phases)


IN_BWD_PARTS = ((0, 1), (1, 4), (4, 8))


def _in_proj_bwd(x, dx1, dz, dxbc, dhq, dhf, dhi, dhg, ddt, nw, wt, part, done=(), phases=()):
    tl = min(TL, x.shape[0] // 8)
    eighth = x.shape[0] // 8 // tl
    first, steps = IN_BWD_PARTS[part][0] * eighth, (IN_BWD_PARTS[part][1] - IN_BWD_PARTS[part][0]) * eighth
    L = steps * tl
    rows = lambda n: pl.BlockSpec((tl, n), lambda i: (i + first, 0))
    starts = [sum(d.shape[0] for d in done[:k]) // tl for k in range(len(done))]
    assert sum(d.shape[0] for d in done) in (0, L)

    def body(x_ref, dx1_ref, dz_ref, dxbc_ref, dhq_ref, dhf_ref, dhi_ref, dhg_ref, ddt_ref, nw_ref, w_ref, *rest):
        done_refs, (gx_ref, acc_ref) = rest[:-2], rest[-2:]
        step = pl.program_id(0)
        _zero_first(acc_ref)
        dh = _dot(dz_ref[...], w_ref[Z0:XBC0, :]) + _dot(dxbc_ref[...], w_ref[XBC0:DT0, :]) \
            + _dot(ddt_ref[...], w_ref[DT0:DT0 + 128, :]) + _dot(dhq_ref[...], w_ref[HQ0:HF0, :]) \
            + _dot(dhf_ref[...], w_ref[HF0:HI0, :]) + _dot(dhi_ref[...], w_ref[HI0:HG0, :]) \
            + _dot(dhg_ref[...], w_ref[HG0:N_IN, :])
        _, n, r = _rms(x_ref[...], nw_ref[...])
        dn, dwn = _rms_bwd(dh, n, r, nw_ref[...])
        acc_ref[0:1, :] += _colsum(dwn)
        if not done:
            gx_ref[...] = dx1_ref[...] + dn
            return
        gx_ref[1] = dx1_ref[...] + dn
        for ref, start, piece in zip(done_refs, starts, done):
            @pl.when(jnp.logical_and(step >= start, step < start + piece.shape[0] // tl))
            def _(ref=ref):
                gx_ref[0] = ref[...]

    def piece_spec(start, piece):
        return pl.BlockSpec((tl, D), lambda i: (jnp.clip(i - start, 0, piece.shape[0] // tl - 1), 0))

    gx_spec, gx_shape = (pl.BlockSpec((2, tl, D), lambda i: (0, i, 0)), (2, L, D)) if done else (_rows(tl, D), (L, D))
    return _call(
        body, (x, dx1, dz, dxbc, dhq, dhf, dhi, dhg, ddt, nw, wt, *done), grid=(steps,), name="in_proj_bwd_%d" % part,
        in_specs=[rows(D), rows(D), rows(D), rows(1536), rows(D), rows(D), rows(D), rows(D), rows(128),
                  _const((1, D)), _resident((N_IN, D))] + [piece_spec(s, d) for s, d in zip(starts, done)],
        out_specs=[gx_spec, _const((8, D))], out_shape=[SDS(gx_shape, F32), SDS((8, D), F32)], phases=phases)


def _gw_in(h0, dz, dxbc, ddt, dhq, dhf, dhi, dhg, phases=()):
    L = h0.shape[0]
    tl = min(512, L)

    def body(h_ref, dz_ref, dxbc_ref, ddt_ref, dhq_ref, dhf_ref, dhi_ref, dhg_ref, o_ref):
        _zero_first(o_ref)
        hb = h_ref[...]
        o_ref[Z0:XBC0, :] += _dot_tn(dz_ref[...], hb)
        o_ref[XBC0:DT0, :] += _dot_tn(dxbc_ref[...], hb)
        o_ref[DT0:HQ0, :] += _dot_tn(ddt_ref[...], hb)[0:NH_SSD, :]
        o_ref[HQ0:HF0, :] += _dot_tn(dhq_ref[...], hb)
        o_ref[HF0:HI0, :] += _dot_tn(dhf_ref[...], hb)
        o_ref[HI0:HG0, :] += _dot_tn(dhi_ref[...], hb)
        o_ref[HG0:N_IN, :] += _dot_tn(dhg_ref[...], hb)

    return _call(body, (h0, dz, dxbc, ddt, dhq, dhf, dhi, dhg), name="gw_in", grid=(L // tl,),
                 in_specs=[_rows(tl, D), _rows(tl, D), _rows(tl, 1536), _rows(tl, 128), _rows(tl, D), _rows(tl, D),
                           _rows(tl, D), _rows(tl, D)],
                 out_specs=[_const((N_IN, D))], out_shape=[SDS((N_IN, D), F32)], phases=phases)


def _token_tile(L, out_bytes, row_bytes):
    tl = min(2048, L)
    while tl > 256 and out_bytes + 2 * tl * row_bytes > MATMUL_VMEM:
        tl //= 2
    return tl


def _matmul_tn(a, b, name):
    L, M = a.shape
    N = b.shape[1]
    tl = _token_tile(L, 4 * M * N, M * a.dtype.itemsize + N * b.dtype.itemsize)

    def body(a_ref, b_ref, o_ref):
        _zero_first(o_ref)
        o_ref[...] += _dot_tn(a_ref[...], b_ref[...])

    return pl.pallas_call(
        body, grid=(L // tl,), name=name, in_specs=[_rows(tl, M), _rows(tl, N)], out_specs=_const((M, N)),
        out_shape=SDS((M, N), F32), compiler_params=_cparams())(a, b)


def _matmul_tn_pair(a0, a1, b, name):
    L, M = a0.shape
    N = b.shape[1]
    tl = _token_tile(L, 8 * M * N, 2 * M * a0.dtype.itemsize + N * b.dtype.itemsize)

    def body(a0_ref, a1_ref, b_ref, o_ref):
        _zero_first(o_ref)
        bv = b_ref[...].astype(BF)
        o_ref[:M, :] += _dot_tn(a0_ref[...], bv)
        o_ref[M:, :] += _dot_tn(a1_ref[...], bv)

    return pl.pallas_call(
        body, grid=(L // tl,), name=name, in_specs=[_rows(tl, M), _rows(tl, M), _rows(tl, N)],
        out_specs=_const((2 * M, N)), out_shape=SDS((2 * M, N), F32), compiler_params=_cparams())(a0, a1, b)


def _head_expand():
    e = (jnp.right_shift(_iota((128, D), 1), 6) == _iota((128, D), 0)).astype(BF)
    et = (jnp.right_shift(_iota((D, 128), 0), 6) == _iota((D, 128), 1)).astype(BF)
    return e, et


def _conv_shifts(cur, other, up):
    rows = _iota((Q, 1), 0)
    out = []
    for s in (1, 2, 3):
        if up:
            out.append(jnp.where(rows >= Q - s, pltpu.roll(other, Q - s, 0), pltpu.roll(cur, Q - s, 0)))
        else:
            out.append(jnp.where(rows < s, pltpu.roll(other, s, 0), pltpu.roll(cur, s, 0)))
    return out


def _ssd_pre(u, dtr, dtb, alog):
    e, et = _head_expand()
    sgu = _sigmoid(u)
    xc = u * sgu
    lane = _iota((1, 128), 1)
    hmask = (lane < NH_SSD).astype(F32)
    pre = dtr + dtb
    dt = (jnp.maximum(pre, 0.0) + jnp.log(1.0 + jnp.exp(-jnp.abs(pre)))) * hmask
    a_row = -jnp.exp(alog)
    causal = _iota((Q, Q), 1) <= _iota((Q, Q), 0)
    tri = causal.astype(BF)
    acum = _dot_sel(tri, dt * a_row)
    acum_full = _sel_dot(acum, e)
    alast_full = acum_full[Q - 1:Q, :]
    dt_full = _sel_dot(dt, e)
    xs = xc[:, :D]
    return dict(e=e, et=et, sgu=sgu, xs=xs, bm=xc[:, D:D + 256], cm=xc[:, D + 256:], hmask=hmask, pre=pre, dt=dt,
                a_row=a_row, causal=causal, tri=tri, acum=acum, acum_t=acum.T, eA_full=jnp.exp(acum_full),
                dte_full=jnp.exp(alast_full - acum_full), dt_full=dt_full, xdt=xs * dt_full)


def _ssd_decay(pre, hh, cb):
    seg = pre["acum"][:, hh:hh + 1] - pre["acum_t"][hh:hh + 1, :]
    lm = jnp.where(pre["causal"], jnp.exp(jnp.minimum(seg, 0.0)), 0.0)
    return lm, cb * lm


def _ssd_fwd(xbc, dtr, z, conv_w, conv_b, dtb, alog, dskip_full, nw):
    L = xbc.shape[0]
    nc = L // Q
    CH = SSD_CH

    def chunk(ck, xbc_ref, dtr_ref, z_ref, cw_ref, cb_ref, dtb_ref, alog_ref, dsk_ref, nw_ref,
              ya_ref, y_ref, u_ref, st_ref, prev_ref, s_ref):
        tok = slice(Q * ck, Q * ck + Q)
        xr = xbc_ref[tok, :].astype(F32)
        sh = _conv_shifts(xr, prev_ref[...], up=False)
        u = cb_ref[...] + cw_ref[3:4, :] * xr + cw_ref[2:3, :] * sh[0] + cw_ref[1:2, :] * sh[1] + cw_ref[0:1, :] * sh[2]
        prev_ref[...] = xr
        ub = u.astype(BF)
        u_ref[tok, :] = ub
        pre = _ssd_pre(ub.astype(F32), dtr_ref[tok, :], dtb_ref[...], alog_ref[...])
        lo = _iota((1, 128), 1) < SSD_P
        s_old = s_ref[...]
        st_ref[ck] = s_old
        ys = []
        for g in range(2):
            bg, cg = pre["bm"][:, 128 * g:128 * g + 128], pre["cm"][:, 128 * g:128 * g + 128]
            cb = _dot_nt(cg, bg)
            gs = slice(512 * g, 512 * g + 512)
            yd = []
            for j in range(4 * g, 4 * g + 4):
                xp = pre["xdt"][:, 128 * j:128 * j + 128].astype(BF)
                _, m0 = _ssd_decay(pre, 2 * j, cb)
                _, m1 = _ssd_decay(pre, 2 * j + 1, cb)
                yd.append(jnp.where(lo, _dot(m0, xp), _dot(m1, xp)))
            yoff = _dot_nt(cg, s_old[gs, :]) * pre["eA_full"][:, gs]
            ys.append(jnp.concatenate(yd, axis=1) + yoff)
            st = _dot_tn((pre["xdt"] * pre["dte_full"])[:, gs], bg)
            cdcol = jnp.exp(_dot_sel(pre["et"][gs, :], pre["acum_t"])[:, Q - 1:Q])
            s_ref[gs, :] = s_old[gs, :] * cdcol + st
        y = jnp.concatenate(ys, axis=1) + dsk_ref[...] * pre["xs"]
        yb = y.astype(BF)
        y_ref[tok, :] = yb
        zf = z_ref[tok, :].astype(F32)
        yz = yb.astype(F32) * zf * _sigmoid(zf)
        outs = []
        for g in range(2):
            gs = slice(512 * g, 512 * g + 512)
            o, _, _ = _rms(yz[:, gs], nw_ref[:, gs])
            outs.append(o)
        ya_ref[tok, :] = jnp.concatenate(outs, axis=1).astype(BF)

    def body(*refs):
        _zero_first(*refs[-2:])
        for ck in range(CH):
            chunk(ck, *refs)

    outs = [SDS((L, D), BF), SDS((L, D), BF), SDS((L, 1536), BF), SDS((nc, D, 128), F32)]
    return pl.pallas_call(
        body, grid=(nc // CH,), name="ssd_fwd",
        in_specs=[_rows(CH * Q, 1536), _rows(CH * Q, 128), _rows(CH * Q, D), _const((4, 1536)), _const((1, 1536)), _const((1, 128)),
                  _const((1, 128)), _const((1, D)), _const((1, D))],
        out_specs=[_rows(CH * Q, D), _rows(CH * Q, D), _rows(CH * Q, 1536),
                   pl.BlockSpec((CH, D, 128), lambda i: (i, 0, 0))],
        out_shape=outs, scratch_shapes=[pltpu.VMEM((Q, 1536), F32), pltpu.VMEM((D, 128), F32)],
        compiler_params=_cparams())(xbc, dtr, z, conv_w, conv_b, dtb, alog, dskip_full, nw)


def _ssd_bwd(dya, y, z, u, xbc, dtr, states, conv_w, dtb, alog, dskip_full, nw):
    L = dya.shape[0]
    nc = L // Q
    CH = SSD_CH

    def chunk(ck, step, dya_ref, y_ref, z_ref, u_ref, xc_ref, dtr_ref, st_ref, cw_ref, dtb_ref, alog_ref, dsk_ref, nw_ref,
              dz_ref, dxbc_ref, ddt_ref, gconv_ref, ghead_ref, glane_ref, gs_ref, ndu_ref):
        tok = slice(Q * ck, Q * ck + Q)
        uf = u_ref[tok, :].astype(F32)
        pre = _ssd_pre(uf, dtr_ref[tok, :], dtb_ref[...], alog_ref[...])
        e, et, xs, xdt = pre["e"], pre["et"], pre["xs"], pre["xdt"]
        lane = _iota((1, 128), 1)
        lo = lane < SSD_P
        sub = _iota((128, 1), 0)
        zf = z_ref[tok, :].astype(F32)
        sgz = _sigmoid(zf)
        sz = zf * sgz
        yv = y_ref[tok, :].astype(F32)
        yz = yv * sz
        dyav = dya_ref[tok, :].astype(F32)
        dyz, dnw = [], []
        for g in range(2):
            gs = slice(512 * g, 512 * g + 512)
            _, n, r = _rms(yz[:, gs], nw_ref[:, gs])
            dv, dw = _rms_bwd(dyav[:, gs], n, r, nw_ref[:, gs])
            dyz.append(dv)
            dnw.append(dw)
        dyz = jnp.concatenate(dyz, axis=1)
        glane_ref[1:2, :] += _colsum(jnp.concatenate(dnw, axis=1))
        dy = dyz * sz
        dz_ref[tok, :] = (dyz * yv * sgz * (1.0 + zf * (1.0 - sgz))).astype(BF)
        glane_ref[0:1, :] += _colsum(dy * xs)
        dxs = dsk_ref[...] * dy

        s_in = st_ref[ck]
        gst = gs_ref[...]
        gy = dy * pre["eA_full"]
        xdte = xdt * pre["dte_full"]
        dacum = jnp.zeros((Q, 128), F32)
        dacum_t = jnp.zeros((128, Q), F32)
        dxdt, dacum_full, ddte_full, dbs, dcs = [], [], [], [], []
        for g in range(2):
            gs = slice(512 * g, 512 * g + 512)
            bg, cg = pre["bm"][:, 128 * g:128 * g + 128], pre["cm"][:, 128 * g:128 * g + 128]
            sg_, dg_ = s_in[gs, :], gst[gs, :]
            yoff = _dot_nt(cg, sg_) * pre["eA_full"][:, gs]
            dc = _dot(gy[:, gs], sg_)
            dsin = _dot_tn(gy[:, gs], cg)
            dacum_full.append(dy[:, gs] * yoff)
            tg = _dot_nt(bg, dg_)
            ddte_full.append(tg * xdt[:, gs])
            db = _dot(xdte[:, gs], dg_)
            cb = _dot_nt(cg, bg)
            dcb = jnp.zeros((Q, Q), F32)
            dxg = []
            for j in range(4 * g, 4 * g + 4):
                xp = xdt[:, 128 * j:128 * j + 128].astype(BF)
                dyp = dy[:, 128 * j:128 * j + 128]
                dxp = jnp.zeros((Q, 128), F32)
                for idx in range(2):
                    hh = 2 * j + idx
                    lm, m = _ssd_decay(pre, hh, cb)
                    dym = jnp.where(lo if idx == 0 else jnp.logical_not(lo), dyp, 0.0).astype(BF)
                    dm = jnp.where(pre["causal"], _dot_nt(dym, xp), 0.0)
                    w = dm * m
                    dacum = dacum + jnp.where(lane == hh, jnp.sum(w, axis=1, keepdims=True), 0.0)
                    dacum_t = dacum_t + jnp.where(sub == hh, jnp.sum(w, axis=0, keepdims=True), 0.0)
                    dcb = dcb + dm * lm
                    dxp = dxp + _dot_tn(m, dym)
                dxg.append(dxp)
            dxdt.append(jnp.concatenate(dxg, axis=1) + tg * pre["dte_full"][:, gs])
            dcs.append(dc + _dot(dcb, bg))
            dbs.append(db + _dot_tn(dcb, cg))
            cdcol = jnp.exp(_dot_sel(et[gs, :], pre["acum_t"])[:, Q - 1:Q])
            gs_ref[gs, :] = dsin + dg_ * cdcol
        dxdt = jnp.concatenate(dxdt, axis=1)
        dacum = dacum + _sel_dot(jnp.concatenate(dacum_full, axis=1), et, 2) - dacum_t.T
        alast = pre["acum"][Q - 1:Q, :]
        dte = jnp.exp(alast - pre["acum"])
        ddte = _sel_dot(jnp.concatenate(ddte_full, axis=1), et, 2) * dte
        dacum = dacum - ddte
        dcd_col = jnp.sum(_dot_sel(e, gst * s_in, 2), axis=1, keepdims=True)
        dcd_row = jnp.broadcast_to(dcd_col, (128, 128)).T[0:1, :]
        dalast = _colsum(ddte) + dcd_row * jnp.exp(alast)
        dacum = dacum + jnp.where(_iota((Q, 1), 0) == Q - 1, dalast, 0.0)
        ddt = _sel_dot(dxdt * xs, et, 2)
        dxs = dxs + dxdt * pre["dt_full"]
        dda = _dot_sel((_iota((Q, Q), 1) >= _iota((Q, Q), 0)).astype(BF), dacum)
        ddt = ddt + dda * pre["a_row"]
        ghead_ref[1:2, :] += _colsum(dda * pre["dt"])
        ddtr = ddt * _sigmoid(pre["pre"]) * pre["hmask"]
        ghead_ref[0:1, :] += _colsum(ddtr)
        ddt_ref[tok, :] = ddtr

        dxc = jnp.concatenate([dxs] + dbs + dcs, axis=1)
        sgu = pre["sgu"]
        du = dxc * sgu * (1.0 + uf * (1.0 - sgu))
        shu = _conv_shifts(du, ndu_ref[...], up=True)
        dxr = cw_ref[3:4, :] * du + cw_ref[2:3, :] * shu[0] + cw_ref[1:2, :] * shu[1] + cw_ref[0:1, :] * shu[2]
        ndu_ref[...] = du
        dxbc_ref[tok, :] = dxr.astype(BF)
        xr = xc_ref[tok, :].astype(F32)
        gconv_ref[3:4, :] += _colsum(du * xr)
        gconv_ref[2:3, :] += _colsum(shu[0] * xr)
        gconv_ref[1:2, :] += _colsum(shu[1] * xr)
        gconv_ref[0:1, :] += _colsum(shu[2] * xr)
        gconv_ref[4:5, :] += _colsum(du)

        @pl.when(jnp.logical_and(step == nc // CH - 1, ck == 0))
        def _():
            ghead_ref[2:3, :] = ghead_ref[1:2, :] * pre["a_row"]
            ghead_ref[3:4, :] = _sel_dot(glane_ref[...], et)[0:1, :]

    def body(*refs):
        _zero_first(*refs[-5:])
        for ck in reversed(range(CH)):
            chunk(ck, pl.program_id(0), *refs)

    rev = lambda i: (nc // CH - 1 - i, 0)
    outs = [SDS((L, D), BF), SDS((L, 1536), BF), SDS((L, 128), F32), SDS((8, 1536), F32), SDS((8, 128), F32),
            SDS((8, D), F32)]
    return pl.pallas_call(
        body, grid=(nc // CH,), name="ssd_bwd",
        in_specs=[pl.BlockSpec((CH * Q, D), rev), pl.BlockSpec((CH * Q, D), rev), pl.BlockSpec((CH * Q, D), rev),
                  pl.BlockSpec((CH * Q, 1536), rev), pl.BlockSpec((CH * Q, 1536), rev),
                  pl.BlockSpec((CH * Q, 128), rev), pl.BlockSpec((CH, D, 128), lambda i: (nc // CH - 1 - i, 0, 0)),
                  _const((4, 1536)), _const((1, 128)), _const((1, 128)), _const((1, D)), _const((1, D))],
        out_specs=[pl.BlockSpec((CH * Q, D), rev), pl.BlockSpec((CH * Q, 1536), rev), pl.BlockSpec((CH * Q, 128), rev),
                   _const((8, 1536)), _const((8, 128)), _const((8, D))],
        out_shape=outs, scratch_shapes=[pltpu.VMEM((D, 128), F32), pltpu.VMEM((Q, 1536), F32)],
        compiler_params=_cparams())(dya, y, z, u, xbc, dtr, states, conv_w, dtb, alog, dskip_full, nw)


def _hg_gates(hq, hf, hlb):
    h0, h1 = hlb[0:1, :], hlb[1:2, :]
    mx = jnp.maximum(h0, h1)
    e0, e1 = jnp.exp(h0 - mx), jnp.exp(h1 - mx)
    lb = e0 / (e0 + e1)
    sg = _sigmoid(hf)
    fg = lb + (1.0 - lb) * sg
    tri = (_iota((Q, Q), 1) <= _iota((Q, Q), 0)).astype(BF)
    return hq * _sigmoid(hq), 1.0 - fg, fg, sg, lb, e1 / (e0 + e1), _dot_sel(tri, jnp.log(fg))


def _hg_intra(b, q, k):
    rowblk = jnp.right_shift(_iota((Q, 1), 0), SUB.bit_length() - 1)
    mids = [b[SUB * i + SUB // 2:SUB * i + SUB // 2 + 1, :] for i in range(NSUB)]
    prevs = [mids[0]] + [b[SUB * i - 1:SUB * i, :] for i in range(1, NSUB)]
    mfull = jnp.concatenate([jnp.broadcast_to(r, (SUB, 128)) for r in mids], axis=0)
    rfull = jnp.concatenate([jnp.broadcast_to(r, (SUB, 128)) for r in prevs], axis=0)
    eqd, ek, eqo = jnp.exp(b - mfull), jnp.exp(mfull - b), jnp.exp(b - rfull)
    qd, qo, khat = q * eqd, q * eqo, k * ek
    rtab = jnp.concatenate(prevs, axis=0)
    djs = [jnp.exp(rtab - mids[j]) for j in range(NSUB)]
    zero = jnp.zeros((SUB, 128), F32)
    cols = []
    for j in range(NSUB):
        pieces = []
        for i in range(NSUB):
            rs = slice(SUB * i, SUB * i + SUB)
            pieces.append(zero if i < j else qd[rs] if i == j else qo[rs] * djs[j][i:i + 1, :])
        cols.append(jnp.concatenate(pieces, axis=0))
    qt = jnp.concatenate(cols, axis=1).astype(BF)
    kt = jnp.concatenate([jnp.where(rowblk == j, khat, 0.0) for j in range(NSUB)], axis=1).astype(BF)
    causal = _iota((Q, Q), 1) <= _iota((Q, Q), 0)
    att = jnp.where(causal, _dot_nt(qt, kt), 0.0)
    return att, qt, kt, (eqd, ek, eqo, djs), causal


def _hg_intra_bwd(dqt, dkt, qt, kt, factors):
    eqd, ek, eqo, djs = factors
    dqd, dqo, dkh, db = [], [], [], []
    for i in range(NSUB):
        rs = slice(SUB * i, SUB * i + SUB)
        diag = slice(128 * i, 128 * i + 128)
        dqd.append(dqt[rs, diag])
        dkh.append(dkt[rs, diag])
        dbi = qt[rs, diag].astype(F32) * dqt[rs, diag] - kt[rs, diag].astype(F32) * dkt[rs, diag]
        acc = jnp.zeros((SUB, 128), F32)
        for j in range(i):
            bl = slice(128 * j, 128 * j + 128)
            acc = acc + dqt[rs, bl] * djs[j][i:i + 1, :]
            dbi = dbi + qt[rs, bl].astype(F32) * dqt[rs, bl]
        dqo.append(acc)
        db.append(dbi)
    cat = lambda t: jnp.concatenate(t, axis=0)
    return cat(dqd) * eqd + cat(dqo) * eqo, cat(dkh) * ek, cat(db)


def _hg_att_exact(b, q, k, b_ref, q_ref, att_t_ref):
    b_ref[...] = b
    q_ref[...] = q
    att_t_ref[...] = jnp.zeros((Q, Q), F32)
    rows, lane = _iota((Q, 1), 0), _iota((1, Q), 1)

    def step(i, carry):
        e = jnp.exp(jnp.minimum(b_ref[pl.ds(i, 1), :] - b, 0.0))
        col = jnp.sum(q_ref[pl.ds(i, 1), :] * k * e, axis=1, keepdims=True)
        att_t_ref[...] = jnp.where(lane == i, jnp.where(rows <= i, col, 0.0), att_t_ref[...])
        return carry

    lax.fori_loop(0, Q, step, 0)
    return att_t_ref[...].T


def _hg_att_exact_bwd(da, b, q, k, b_ref, q_ref, da_t_ref, dq_ref, dk_ref):
    b_ref[...] = b
    q_ref[...] = q
    da_t_ref[...] = da.T
    dk_ref[...] = jnp.zeros((Q, 128), F32)
    lane = _iota((1, Q), 1)

    def step(i, carry):
        e = jnp.exp(jnp.minimum(b_ref[pl.ds(i, 1), :] - b, 0.0))
        g = jnp.sum(jnp.where(lane == i, da_t_ref[...], 0.0), axis=1, keepdims=True) * e
        dq_ref[pl.ds(i, 1), :] = jnp.sum(g * k, axis=0, keepdims=True)
        dk_ref[...] += g * q_ref[pl.ds(i, 1), :]
        return carry

    lax.fori_loop(0, Q, step, 0)
    dq, dk = dq_ref[...], dk_ref[...]
    return dq, dk, q * dq - k * dk


def _hg_fwd(hq, hf, hi, hg, hlb, nw, fast):
    L = hq.shape[0]
    nc = L // Q

    def chunk(exact, ck, hq_ref, hf_ref, hi_ref, hg_ref, hlb_ref, nw_ref, ob_ref, o_ref, st_ref, s_ref, *tmp):
        tok = slice(Q * ck, Q * ck + Q)
        qf, kf, _, _, _, _, bcum = _hg_gates(hq_ref[tok, :].astype(F32), hf_ref[tok, :], hlb_ref[...])
        gate = hg_ref[tok, :].astype(F32)
        heads = [slice(128 * h, 128 * h + 128) for h in range(NH_HG)]
        if exact:
            atts = [_hg_att_exact(bcum[:, sl], qf[:, sl], kf[:, sl], *tmp).astype(BF) for sl in heads]
        else:
            atts = [_hg_intra(bcum[:, sl], qf[:, sl], kf[:, sl])[0].astype(BF) for sl in heads]
        olds = [s_ref[sl, :] for sl in heads]
        outs_ = [_dot(att, hi_ref[tok, sl]) + _dot(qf[:, sl] * jnp.exp(bcum[:, sl]), s)
                 for att, sl, s in zip(atts, heads, olds)]
        for sl, s, o in zip(heads, olds, outs_):
            b, k = bcum[:, sl], kf[:, sl]
            st_ref[ck, sl, :] = s
            blast = b[Q - 1:Q, :]
            s_ref[sl, :] = s * jnp.exp(b.T[:, Q - 1:Q]) + _dot_tn(k * jnp.exp(blast - b), hi_ref[tok, sl])
            ob = o.astype(BF)
            o_ref[tok, sl] = ob
            on, _, _ = _rms(ob.astype(F32), nw_ref[...])
            gt = gate[:, sl]
            ob_ref[tok, sl] = (on * gt * _sigmoid(gt)).astype(BF)

    def run(exact, *refs):
        for ck in range(CH):
            chunk(exact, ck, *refs)

    def body(fast_ref, *refs):
        _zero_first(refs[9])
        pl.when(fast_ref[0] == 1)(lambda: run(False, *refs))
        pl.when(fast_ref[0] != 1)(lambda: run(True, *refs))

    rows = pl.BlockSpec((CH * Q, D), lambda i, f: (i, 0))
    outs = [SDS((L, D), BF), SDS((L, D), BF), SDS((nc, D, 128), F32)]
    grid_spec = pltpu.PrefetchScalarGridSpec(
        num_scalar_prefetch=1, grid=(nc // CH,),
        in_specs=[rows] * 4 + [pl.BlockSpec((2, D), lambda i, f: (0, 0)), pl.BlockSpec((1, 128), lambda i, f: (0, 0))],
        out_specs=[rows, rows, pl.BlockSpec((CH, D, 128), lambda i, f: (i, 0, 0))],
        scratch_shapes=[pltpu.VMEM((D, 128), F32), pltpu.VMEM((Q, 128), F32), pltpu.VMEM((Q, 128), F32),
                        pltpu.VMEM((Q, Q), F32)])
    return pl.pallas_call(body, grid_spec=grid_spec, name="hg_fwd", out_shape=outs,
                          compiler_params=_cparams())(fast, hq, hf, hi, hg, hlb, nw)


def _hg_bwd(dob, o, hq, hf, hi, hg, states, hlb, nw, fast, phases=()):
    L = dob.shape[0]
    nc = L // Q

    def chunk(exact, ck, step, dob_ref, o_ref, hq_ref, hf_ref, hi_ref, hg_ref, st_ref, hlb_ref, nw_ref,
              dhq_ref, dhf_ref, dhi_ref, dhg_ref, acc_ref, gs_ref, *tmp):
        tok = slice(Q * ck, Q * ck + Q)
        hqv = hq_ref[tok, :].astype(F32)
        qf, kf, fg, sg, lb, sm1, bcum = _hg_gates(hqv, hf_ref[tok, :], hlb_ref[...])
        gate = hg_ref[tok, :].astype(F32)
        sgg = _sigmoid(gate)
        nwv = nw_ref[...]
        tri_t = (_iota((Q, Q), 1) >= _iota((Q, Q), 0)).astype(BF)
        ones8 = jnp.ones((8, 128), BF)
        heads = [slice(128 * h, 128 * h + 128) for h in range(NH_HG)]
        row_last = _iota((Q, 1), 0) == Q - 1
        dobs, dnws = [], []
        for sl in heads:
            gt, sgt = gate[:, sl], sgg[:, sl]
            _, n, r = _rms(o_ref[tok, sl].astype(F32), nwv)
            dobv = dob_ref[tok, sl].astype(F32)
            dhg_ref[tok, sl] = (dobv * n * nwv * sgt * (1.0 + gt * (1.0 - sgt))).astype(BF)
            do, dw = _rms_bwd(dobv * gt * sgt, n, r, nwv)
            dnws.append(_colsum(dw))
            dobs.append(do.astype(BF))
        causal = _iota((Q, Q), 1) <= _iota((Q, Q), 0)
        if exact:
            intra = [(_hg_att_exact(bcum[:, sl], qf[:, sl], kf[:, sl], *tmp[:3]),) for sl in heads]
        else:
            intra = [_hg_intra(bcum[:, sl], qf[:, sl], kf[:, sl]) for sl in heads]
        states = [(st_ref[ck, sl, :], gs_ref[sl, :]) for sl in heads]
        das = [jnp.where(causal, _dot_nt(dob_h, hi_ref[tok, sl]), 0.0) for dob_h, sl in zip(dobs, heads)]
        dqhats = [_dot_nt(dob_h, s) for dob_h, (s, _) in zip(dobs, states)]
        dkhats = [_dot_nt(hi_ref[tok, sl], gst) for sl, (_, gst) in zip(heads, states)]
        if not exact:
            dqts = [jnp.dot(da.astype(BF), it[2], preferred_element_type=F32) for da, it in zip(das, intra)]
            dkts = [lax.dot_general(da.astype(BF), it[1], (((0,), (0,)), ((), ())), preferred_element_type=F32)
                    for da, it in zip(das, intra)]
        dqs, dks, dgls = [], [], []
        for h, sl in enumerate(heads):
            b, q, k = bcum[:, sl], qf[:, sl], kf[:, sl]
            att = intra[h][0]
            s, gst = states[h]
            dob_h, dqhat, dkhat = dobs[h], dqhats[h], dkhats[h]
            eb = jnp.exp(b)
            blast = b[Q - 1:Q, :]
            ekl = jnp.exp(blast - b)
            qhat, khat = q * eb, k * ekl
            dhi_ref[tok, sl] = (_dot_tn(att, dob_h) + _dot(khat, gst)).astype(BF)
            if exact:
                dq_i, dk_i, db = _hg_att_exact_bwd(das[h], b, q, k, *tmp)
            else:
                dq_i, dk_i, db = _hg_intra_bwd(dqts[h], dkts[h], *intra[h][1:4])
            dqs.append(dq_i + dqhat * eb)
            dks.append(dk_i + dkhat * ekl)
            qhat_r, khat_r = qhat.astype(BF).astype(F32), khat.astype(BF).astype(F32)
            decay_row = sum(_dot_nt(ones8, part) for part in _split(gst * s, 2))[0:1, :]
            dblast = _colsum(dkhat * khat_r) + decay_row * jnp.exp(blast)
            dgls.append(db + qhat_r * dqhat - khat_r * dkhat + jnp.where(row_last, dblast, 0.0))
            gs_ref[sl, :] = _dot_tn(qhat, dob_h) + gst * jnp.exp(b.T[:, Q - 1:Q])
        dq, dk, db = (jnp.concatenate(t, axis=1) for t in (dqs, dks, dgls))
        dgl = _dot_sel(tri_t, db, 2)
        sgq = _sigmoid(hqv)
        dhq_ref[tok, :] = (dq * sgq * (1.0 + hqv * (1.0 - sgq))).astype(BF)
        dfg = dgl / fg - dk
        dhf_ref[tok, :] = (dfg * (1.0 - lb) * sg * (1.0 - sg)).astype(BF)
        acc_ref[0:1, :] += _colsum(dfg * (1.0 - sg))
        acc_ref[1:2, :] += jnp.concatenate(dnws, axis=1)

        @pl.when(jnp.logical_and(step == nc // CH - 1, ck == 0))
        def _():
            dlb = acc_ref[0:1, :] * lb * sm1
            acc_ref[2:3, :] = dlb
            acc_ref[3:4, :] = -dlb
            tot = acc_ref[1:2, 0:128]
            for h in range(1, NH_HG):
                tot = tot + acc_ref[1:2, 128 * h:128 * h + 128]
            acc_ref[4:5, 0:128] = tot

    def run(exact, step, *refs):
        for ck in reversed(range(CH)):
            chunk(exact, ck, step, *refs)

    def body(fast_ref, *refs):
        step = pl.program_id(0)
        _zero_first(refs[13], refs[14])
        pl.when(fast_ref[0] == 1)(lambda: run(False, step, *refs))
        pl.when(fast_ref[0] != 1)(lambda: run(True, step, *refs))

    rev = pl.BlockSpec((CH * Q, D), lambda i, f: (nc // CH - 1 - i, 0))
    outs = [SDS((L, D), BF)] * 4 + [SDS((8, D), F32)]
    return _call(
        body, (fast, dob, o, hq, hf, hi, hg, states, hlb, nw), name="hg_bwd", grid=(nc // CH,), prefetch=1,
        in_specs=[rev] * 6 + [pl.BlockSpec((CH, D, 128), lambda i, f: (nc // CH - 1 - i, 0, 0)),
                              pl.BlockSpec((2, D), lambda i, f: (0, 0)), pl.BlockSpec((1, 128), lambda i, f: (0, 0))],
        out_specs=[rev] * 4 + [pl.BlockSpec((8, D), lambda i, f: (0, 0))], out_shape=outs,
        scratch_shapes=[pltpu.VMEM((D, 128), F32), pltpu.VMEM((Q, 128), F32), pltpu.VMEM((Q, 128), F32),
                        pltpu.VMEM((Q, Q), F32), pltpu.VMEM((Q, 128), F32), pltpu.VMEM((Q, 128), F32)], phases=phases)


def _place():
    return lax.axis_index("x"), lax.axis_index("y"), lax.axis_index("c")


def _phase_io(phase):
    kind, arrays, halves = phase
    n = len(arrays)
    dma = pltpu.SemaphoreType.DMA
    if kind == "gather":
        outs = [SDS((8,) + a.shape if hc is None else (4,) + a.shape, a.dtype) for a, hc in zip(arrays, halves)]
        return outs, [dma((7 * n,)), dma((7 * n,)), dma((n,))], {}
    if kind == "sibling":
        return [SDS((4, g.shape[1], hc), g.dtype) for g, hc in zip(arrays, halves)], [dma((n,)), dma((n,))], {}
    if kind == "chips":
        return [SDS((3,) + p.shape[1:], p.dtype) for p in arrays], [dma((3 * n,)), dma((3 * n,))], {}
    assert kind == "swap"
    return [SDS(b.shape, b.dtype) for b in arrays], [dma((n,)), dma((n,))], {a: a for a in range(n)}


def _gather_events(ins, outs, sems, halves):
    send_sems, recv_sems, local_sems = sems
    n = len(ins)

    def parts(a):
        x, y, c = _place()
        hc = halves[a]
        me, sibling = (x, y, c), (x, y, 1 - c)
        chips = [(1 - x, y), (x, 1 - y), (1 - x, 1 - y)]

        def slot(p):
            if hc is None:
                return outs[a].at[4 * p[0] + 2 * p[1] + p[2]]
            return outs[a].at[2 * p[0] + p[1], :, pl.ds(p[2] * hc, hc)]

        own = ins[a] if hc is None else ins[a].at[:, pl.ds(c * hc, hc)]

        def copy(k, piece, to, src=None):
            return pltpu.make_async_remote_copy(
                src_ref=slot(piece) if src is None else src, dst_ref=slot(piece),
                send_sem=send_sems.at[7 * a + k], recv_sem=recv_sems.at[7 * a + k], device_id=to, device_id_type=MESH)

        return dict(
            mine=lambda: pltpu.make_async_copy(own, slot(me), local_sems.at[a]),
            starts=lambda: [copy(0, me, sibling, src=own)] + [copy(1 + j, me, (*chip, c), src=own)
                                                               for j, chip in enumerate(chips)],
            arrive=lambda: [copy(1 + j, (*chip, c), me) for j, chip in enumerate(chips)],
            passed=lambda: [copy(4 + j, (*chip, c), sibling) for j, chip in enumerate(chips)],
            from_sibling=lambda: [copy(0, sibling, me)] + [copy(4 + j, (*chip, 1 - c), me)
                                                            for j, chip in enumerate(chips)])

    def first():
        for a in range(n):
            p = parts(a)
            p["mine"]().start()
            for cp in p["starts"]():
                cp.start()

    def mid():
        for a in range(n):
            p = parts(a)
            for cp_in, cp_out in zip(p["arrive"](), p["passed"]()):
                cp_in.wait_recv()
                cp_out.start()

    def last():
        for a in range(n):
            p = parts(a)
            for cp in p["from_sibling"]():
                cp.wait_recv()
            for cp in p["starts"]() + p["passed"]():
                cp.wait_send()
            p["mine"]().wait()

    return dict(first=first, mid=mid, last=last)


def _exchange_events(kind, ins, outs, sems, halves):
    send_sems, recv_sems = sems
    n = len(outs)

    def copies():
        x, y, c = _place()
        if kind == "sibling":
            return [pltpu.make_async_remote_copy(
                src_ref=ins[a].at[:, :, pl.ds((1 - c) * halves[a], halves[a])], dst_ref=outs[a],
                send_sem=send_sems.at[a], recv_sem=recv_sems.at[a], device_id=(x, y, 1 - c), device_id_type=MESH)
                for a in range(n)]
        chips = [(1 - x, y), (x, 1 - y), (1 - x, 1 - y)]
        return [pltpu.make_async_remote_copy(
            src_ref=ins[a].at[2 * px + py], dst_ref=outs[a].at[k], send_sem=send_sems.at[3 * a + k],
            recv_sem=recv_sems.at[3 * a + k], device_id=(px, py, c), device_id_type=MESH)
            for a in range(n) for k, (px, py) in enumerate(chips)]

    def first():
        for cp in copies():
            cp.start()

    def last():
        for cp in copies():
            cp.wait()

    return dict(first=first, last=last)


def _swap_events(outs, sems, halves):
    send_sems, recv_sems = sems
    n = len(outs)

    def copy(a, landing):
        x, y, c = _place()
        cols = lambda which: outs[a].at[:, pl.ds(which * halves[a], halves[a])]
        return pltpu.make_async_remote_copy(
            src_ref=cols(c), dst_ref=cols(1 - c) if landing else cols(c), send_sem=send_sems.at[a],
            recv_sem=recv_sems.at[a], device_id=(x, y, 1 - c), device_id_type=MESH)

    def first():
        for a in range(n):
            copy(a, False).start()

    def last():
        for a in range(n):
            copy(a, True).wait_recv()
        for a in range(n):
            copy(a, False).wait_send()

    return dict(first=first, last=last)


def _phase_events(phase, ins, outs, sems):
    kind, _, halves = phase
    if kind == "gather":
        return _gather_events(ins, outs, sems, halves)
    if kind == "swap":
        return _swap_events(outs, sems, halves)
    return _exchange_events(kind, ins, outs, sems, halves)


def _split_refs(refs, counts):
    out, at = [], 0
    for c in counts:
        out.append(list(refs[at:at + c]))
        at += c
    return out


def _comm_plumbing(phases, first_in, first_out):
    ios = [_phase_io(p) for p in phases]
    arrays = [a for p in phases for a in p[1]]
    out_shape = [o for io in ios for o in io[0]]
    sem_shapes = [s for io in ios for s in io[1]]
    aliases, ai, ao = {}, first_in, first_out
    for p, io in zip(phases, ios):
        aliases.update({ai + k: ao + v for k, v in io[2].items()})
        ai, ao = ai + len(p[1]), ao + len(io[0])

    def events(cins, couts, sems):
        evs = [_phase_events(p, i, o, s) for p, i, o, s in zip(
            phases, _split_refs(cins, [len(p[1]) for p in phases]), _split_refs(couts, [len(io[0]) for io in ios]),
            _split_refs(sems, [len(io[1]) for io in ios]))]

        def run(key):
            for ev in evs:
                if key in ev:
                    ev[key]()

        return {key: (lambda key=key: run(key)) for key in ("first", "mid", "last")}

    def regroup(flat):
        return _split_refs(flat, [len(io[0]) for io in ios])

    return arrays, out_shape, sem_shapes, aliases, events, regroup


def _run_phases(phases, name):
    arrays, out_shape, sem_shapes, aliases, events, regroup = _comm_plumbing(phases, 0, 0)

    def body(*refs):
        cins, couts, sems = _split_refs(refs, [len(arrays), len(out_shape), len(sem_shapes)])
        ev = events(cins, couts, sems)
        for key in ("first", "mid", "last"):
            ev[key]()

    outs = pl.pallas_call(
        body, name=name, in_specs=[ANY] * len(arrays), out_specs=[ANY] * len(out_shape), out_shape=out_shape,
        scratch_shapes=sem_shapes, input_output_aliases=aliases)(*arrays)
    return regroup(outs)


def _call(body, args, *, name, grid, in_specs, out_specs, out_shape, scratch_shapes=(), prefetch=0, phases=(),
          mid_step=None):
    steps = grid[0]
    arrays, c_shape, sem_shapes, aliases, events, regroup = _comm_plumbing(
        phases, prefetch + len(in_specs), len(out_specs))
    counts = [prefetch, len(in_specs), len(arrays), len(out_specs), len(c_shape), len(scratch_shapes), len(sem_shapes)]

    def wrapped(*refs):
        pre, ins, cins, outs, couts, scratch, sems = _split_refs(refs, counts)
        if not phases:
            return body(*pre, *ins, *outs, *scratch)
        step = pl.program_id(0)
        ev = events(cins, couts, sems)
        pl.when(step == 0)(ev["first"])
        body(*pre, *ins, *outs, *scratch)
        pl.when(step == (steps // 2 if mid_step is None else mid_step))(ev["mid"])
        pl.when(step == steps - 1)(ev["last"])

    grid_spec = pltpu.PrefetchScalarGridSpec(
        num_scalar_prefetch=prefetch, grid=grid, in_specs=list(in_specs) + [ANY] * len(arrays),
        out_specs=list(out_specs) + [ANY] * len(c_shape), scratch_shapes=list(scratch_shapes) + sem_shapes)
    outs = pl.pallas_call(
        wrapped, grid_spec=grid_spec, name=name, out_shape=list(out_shape) + c_shape, input_output_aliases=aliases,
        compiler_params=_cparams())(*args, *arrays)
    return list(outs[:len(out_specs)]), regroup(outs[len(out_specs):])


def _tile(rows, cols, nbuf):
    budget = (VMEM_LIMIT // 3) // (2 * nbuf * 4)
    if rows % 8 == 0:
        cands = [t for t in range(8, rows + 1, 8) if rows % t == 0 and t * cols <= budget]
        pref = [t for t in cands if t % 16 == 0]
        return (max(pref) if pref else max(cands) if cands else 8), cols
    cands = [t for t in range(128, cols + 1, 128) if cols % t == 0 and rows * t <= budget]
    return rows, (max(cands) if cands else 128)


def _chip_sum(g, from_sib, place, name):
    _, rows, hc = from_sib.shape
    tr, tc = _tile(rows, hc, 4)
    ni, nj = rows // tr, hc // tc

    def body(p_ref, g_ref, s_ref, hb_ref, own_ref):
        s = g_ref[...] + s_ref[...]
        hb_ref[...] = s.astype(BF)

        @pl.when(pl.program_id(2) == p_ref[1])
        def _():
            own_ref[...] = s

    grid_spec = pltpu.PrefetchScalarGridSpec(
        num_scalar_prefetch=1, grid=(ni, nj, 4),
        in_specs=[pl.BlockSpec((None, tr, tc), lambda i, j, k, p: (k, i, p[0] * nj + j)),
                  pl.BlockSpec((None, tr, tc), lambda i, j, k, p: (k, i, j))],
        out_specs=[pl.BlockSpec((None, tr, tc), lambda i, j, k, p: (k, i, j)),
                   pl.BlockSpec((tr, tc), lambda i, j, k, p: (i, j))])
    return pl.pallas_call(
        body, grid_spec=grid_spec, name=name, out_shape=[SDS((4, rows, hc), BF), SDS((rows, hc), F32)],
        compiler_params=pltpu.CompilerParams(dimension_semantics=("arbitrary",) * 3,
                                             vmem_limit_bytes=VMEM_LIMIT))(place, g, from_sib)


def _total(own, parts, place, name):
    rows, hc = own.shape
    tr, tc = _tile(rows, hc, 5)
    ni, nj = rows // tr, hc // tc

    def body(p_ref, own_ref, parts_ref, o_ref):
        s = own_ref[...]
        for k in range(3):
            s = s + parts_ref[k].astype(F32)
        o_ref[...] = s

    grid_spec = pltpu.PrefetchScalarGridSpec(
        num_scalar_prefetch=1, grid=(ni, nj),
        in_specs=[pl.BlockSpec((tr, tc), lambda i, j, p: (i, j)),
                  pl.BlockSpec((3, tr, tc), lambda i, j, p: (0, i, j))],
        out_specs=pl.BlockSpec((tr, tc), lambda i, j, p: (i, p[0] * nj + j)))
    return pl.pallas_call(
        body, grid_spec=grid_spec, name=name, out_shape=SDS((rows, 2 * hc), F32),
        compiler_params=pltpu.CompilerParams(dimension_semantics=("arbitrary",) * 2,
                                             vmem_limit_bytes=VMEM_LIMIT))(place, own, parts)


def _sum8(parts, name):
    R = parts.shape[1]

    def body(p_ref, o_ref):
        s = p_ref[0]
        for k in range(1, 8):
            s = s + p_ref[k]
        o_ref[...] = s

    return pl.pallas_call(
        body, grid=(1,), name=name, in_specs=[_const((8, R, 128))], out_specs=_const((R, 128)),
        out_shape=SDS((R, 128), F32), compiler_params=_cparams())(parts)


def _adamw(w, g, m, v, name):
    _, R, C = w.shape
    tr, tc = _tile(R, C, 8)
    c1 = 1.0 / (1.0 - ADAM_B1 ** ADAM_STEP)
    c2 = 1.0 / (1.0 - ADAM_B2 ** ADAM_STEP)

    def body(w_ref, g_ref, m_ref, v_ref, go_ref, d_ref, nm_ref, nv_ref):
        gv = g_ref[...]
        go_ref[...] = gv
        nm = ADAM_B1 * m_ref[...] + (1.0 - ADAM_B1) * gv
        nv = ADAM_B2 * v_ref[...] + (1.0 - ADAM_B2) * gv * gv
        nm_ref[...] = nm
        nv_ref[...] = nv
        d_ref[...] = -ADAM_LR * ((nm * c1) / (jnp.sqrt(nv * c2) + ADAM_EPS) + ADAM_WD * w_ref[...])

    blk3 = pl.BlockSpec((None, tr, tc), lambda i, j: (0, i, j))
    return pl.pallas_call(
        body, grid=(R // tr, C // tc), name=name,
        in_specs=[blk3, pl.BlockSpec((tr, tc), lambda i, j: (i, j)), blk3, blk3], out_specs=[blk3] * 4,
        out_shape=[SDS((1, R, C), F32)] * 4,
        compiler_params=pltpu.CompilerParams(dimension_semantics=("arbitrary",) * 2,
                                             vmem_limit_bytes=VMEM_LIMIT))(w, g, m, v)


def _pack_small(parts):
    rows = []
    for p in parts:
        p = p.reshape(-1)
        rows.append(jnp.pad(p, (0, (-p.shape[0]) % 128)).reshape(-1, 128))
    out = jnp.concatenate(rows, axis=0)
    return jnp.pad(out, ((0, (-out.shape[0]) % 8), (0, 0)))


def _unpack_small(packed, shapes):
    out, row = [], 0
    for shp in shapes:
        n = 1
        for s in shp:
            n *= s
        nr = -(-n // 128)
        out.append(packed[row:row + nr].reshape(-1)[:n].reshape(shp))
        row += nr
    return out


def _pad_lanes(v, n=128):
    return jnp.pad(v, ((0, 0), (0, n - v.shape[1])))


GROUP_FFN = ("ffn_w_gate", "ffn_w_up", "ffn_w_down")
GROUP_ATTN = ("w_out", "xa_wq", "xa_wkv", "xa_wo")


def kernel(x, mem, norm_mix_w, w_in, conv_w, conv_b, dt_bias, a_log, d_skip, ssd_norm_w, hg_lower_bounds, hg_norm_w, w_out, norm_xa_w, norm_mem_w, xa_wq, xa_wkv, xa_wo, norm_ffn_w, ffn_w_gate, ffn_w_up, ffn_w_down, norm_final_w, loss_target, m_norm_mix_w, m_w_in, m_conv_w, m_conv_b, m_dt_bias, m_a_log, m_d_skip, m_ssd_norm_w, m_hg_lower_bounds, m_hg_norm_w, m_w_out, m_norm_xa_w, m_norm_mem_w, m_xa_wq, m_xa_wkv, m_xa_wo, m_norm_ffn_w, m_ffn_w_gate, m_ffn_w_up, m_ffn_w_down, m_norm_final_w, v_norm_mix_w, v_w_in, v_conv_w, v_conv_b, v_dt_bias, v_a_log, v_d_skip, v_ssd_norm_w, v_hg_lower_bounds, v_hg_norm_w, v_w_out, v_norm_xa_w, v_norm_mem_w, v_xa_wq, v_xa_wkv, v_xa_wo, v_norm_ffn_w, v_ffn_w_gate, v_ffn_w_up, v_ffn_w_down, v_norm_final_w):
    w = dict(norm_mix_w=norm_mix_w, w_in=w_in, conv_w=conv_w, conv_b=conv_b, dt_bias=dt_bias, a_log=a_log, d_skip=d_skip,
             ssd_norm_w=ssd_norm_w, hg_lower_bounds=hg_lower_bounds, hg_norm_w=hg_norm_w, w_out=w_out,
             norm_xa_w=norm_xa_w, norm_mem_w=norm_mem_w, xa_wq=xa_wq, xa_wkv=xa_wkv, xa_wo=xa_wo, norm_ffn_w=norm_ffn_w,
             ffn_w_gate=ffn_w_gate, ffn_w_up=ffn_w_up, ffn_w_down=ffn_w_down, norm_final_w=norm_final_w)
    m = dict(norm_mix_w=m_norm_mix_w, w_in=m_w_in, conv_w=m_conv_w, conv_b=m_conv_b, dt_bias=m_dt_bias, a_log=m_a_log,
             d_skip=m_d_skip, ssd_norm_w=m_ssd_norm_w, hg_lower_bounds=m_hg_lower_bounds, hg_norm_w=m_hg_norm_w,
             w_out=m_w_out, norm_xa_w=m_norm_xa_w, norm_mem_w=m_norm_mem_w, xa_wq=m_xa_wq, xa_wkv=m_xa_wkv,
             xa_wo=m_xa_wo, norm_ffn_w=m_norm_ffn_w, ffn_w_gate=m_ffn_w_gate, ffn_w_up=m_ffn_w_up,
             ffn_w_down=m_ffn_w_down, norm_final_w=m_norm_final_w)
    v = dict(norm_mix_w=v_norm_mix_w, w_in=v_w_in, conv_w=v_conv_w, conv_b=v_conv_b, dt_bias=v_dt_bias, a_log=v_a_log,
             d_skip=v_d_skip, ssd_norm_w=v_ssd_norm_w, hg_lower_bounds=v_hg_lower_bounds, hg_norm_w=v_hg_norm_w,
             w_out=v_w_out, norm_xa_w=v_norm_xa_w, norm_mem_w=v_norm_mem_w, xa_wq=v_xa_wq, xa_wkv=v_xa_wkv,
             xa_wo=v_xa_wo, norm_ffn_w=v_norm_ffn_w, ffn_w_gate=v_ffn_w_gate, ffn_w_up=v_ffn_w_up,
             ffn_w_down=v_ffn_w_down, norm_final_w=v_norm_final_w)
    xi, yi, ci = _place()
    chip = 2 * xi + yi
    place = jnp.stack([ci, chip]).astype(jnp.int32)

    def shard(t, name):
        return jnp.swapaxes(t[name], 1, 2) if name in TRANSPOSED else t[name]

    wsh = {name: shard(w, name) for name in BIG}
    half = {name: wsh[name].shape[2] // 2 for name in BIG}
    payload = {name: wsh[name][0].astype(BF) for name in BIG}
    ws = {name: w[name] for name in SMALL}
    xs, mems, tgt = x[0], mem[0], loss_target[0]

    def chip_sums(names, grads, from_sib):
        return [_chip_sum(grads[n], s, place, "grads_chip_sum_" + n) for n, s in zip(names, from_sib)]

    def totals(names, sums, others):
        return [_total(own, o, place, "grads_total_" + n) for n, (_, own), o in zip(names, sums, others)]

    ((w_in4, conv_all),) = _run_phases([("gather", [payload["w_in"], conv_w[0]], [half["w_in"], None])], "gather_w_in")
    w_in_t = w_in4.reshape(N_IN, D)
    ws["conv_w"] = conv_all[0::2].transpose(1, 0, 2).reshape(1, 4, 1536)
    rest = [n for n in BIG if n != "w_in"]
    (h0, z, xbc, hq, hf, hi, hg, dtr), (gathered,) = _in_proj(
        xs, ws["norm_mix_w"], w_in_t, phases=[("gather", [payload[n] for n in rest], [half[n] for n in rest])])
    wg = dict(zip(rest, gathered))
    wg_t, wu_t = wg["ffn_w_gate"].reshape(FFN, D), wg["ffn_w_up"].reshape(FFN, D)
    wd = wg["ffn_w_down"].reshape(FFN, D)
    w_out_f = wg["w_out"].reshape(2 * D, D)
    wq, wo = wg["xa_wq"].reshape(D, D), wg["xa_wo"].reshape(D, D)
    dtb, alog = _pad_lanes(ws["dt_bias"]), _pad_lanes(ws["a_log"])
    dskip_full = jnp.repeat(ws["d_skip"], SSD_P, axis=1)
    cw, conv_bias = ws["conv_w"][0], ws["conv_b"]
    hlb = ws["hg_lower_bounds"]
    hg_fast = (jnp.min(jax.nn.softmax(hlb, axis=0)[0]) >= HG_LB_FLOOR).astype(jnp.int32).reshape(1)

    ya, yssd, u, st_ssd = _ssd_fwd(xbc, dtr, z, cw, conv_bias, dtb, alog, dskip_full, ws["ssd_norm_w"])
    ob, ohg, st_hg = _hg_fwd(hq, hf, hi, hg, hlb, ws["hg_norm_w"], hg_fast)
    kmem, vmem = _mem_kv(mems, ws["norm_mem_w"], wg["xa_wkv"])
    x1, x2, hxa, q, ox = _attn_fwd(xs, ya, ob, w_out_f, ws["norm_xa_w"], wq, kmem, vmem, wo)
    nfin = ws["norm_final_w"].reshape(1, D)
    dx2, hffn, act, dx3, dg, du, acc_f = _ffn_loss(x2, tgt, ws["norm_ffn_w"], nfin, wg_t, wu_t, wd)

    gb = {"ffn_w_gate": _matmul_tn(dg, hffn, "gw_gate").reshape(4, FFN // 4, D),
          "ffn_w_up": _matmul_tn(du, hffn, "gw_up").reshape(4, FFN // 4, D),
          "ffn_w_down": _matmul_tn(act, dx3, "gw_down").reshape(4, FFN // 4, D)}
    (dx1, dya, dob, dq, dk, dv, acc_a), (sib_ffn,) = _attn_bwd(
        dx2, x1, q, kmem, vmem, ws["norm_xa_w"], wq, wo, w_out_f,
        phases=[("sibling", [gb[n] for n in GROUP_FFN], [half[n] for n in GROUP_FFN])])
    sums_ffn = chip_sums(GROUP_FFN, gb, sib_ffn)
    g_nmem, gb["xa_wkv"] = _mem_kv_bwd(mems, ws["norm_mem_w"], wg["xa_wkv"], dk, dv)
    gb["w_out"] = _matmul_tn_pair(ya, ob, dx1, "gw_out").reshape(4, D // 2, D)
    gb["xa_wq"] = _matmul_tn(hxa, dq, "gw_q").reshape(4, D // 4, D)
    gb["xa_wo"] = _matmul_tn(ox, dx2, "gw_o").reshape(4, D // 4, D)
    (dhq, dhf, dhi, dhg, acc_h), (others_ffn, sib_attn) = _hg_bwd(
        dob, ohg, hq, hf, hi, hg, st_hg, hlb, ws["hg_norm_w"], hg_fast,
        phases=[("chips", [hb for hb, _ in sums_ffn], None),
                ("sibling", [gb[n] for n in GROUP_ATTN], [half[n] for n in GROUP_ATTN])])
    red_ffn = totals(GROUP_FFN, sums_ffn, others_ffn)
    sums_attn = chip_sums(GROUP_ATTN, gb, sib_attn)
    dz, dxbc, ddt, gconv, ghead, glane = _ssd_bwd(dya, yssd, z, u, xbc, dtr, st_ssd, cw, dtb, alog, dskip_full,
                                                  ws["ssd_norm_w"])
    (gw_in_t,), (g_ffn, others_attn) = _gw_in(
        h0, dz, dxbc, ddt, dhq, dhf, dhi, dhg,
        phases=[("swap", red_ffn, [half[n] for n in GROUP_FFN]), ("chips", [hb for hb, _ in sums_attn], None)])
    red_attn = totals(GROUP_ATTN, sums_attn, others_attn)
    gb["w_in"] = gw_in_t.reshape(4, N_IN // 4, D)
    dproj = (xs, dx1, dz, dxbc, dhq, dhf, dhi, dhg, ddt, ws["norm_mix_w"], w_in_t)
    (gx_a, acc_ia), (g_attn, (sib_in,)) = _in_proj_bwd(
        *dproj, 0, phases=[("swap", red_attn, [half[n] for n in GROUP_ATTN]), ("sibling", [gb["w_in"]], [half["w_in"]])])
    sums_in = chip_sums(("w_in",), gb, [sib_in])
    (gx_b, acc_ib), ((others_in,),) = _in_proj_bwd(*dproj, 1, phases=[("chips", [sums_in[0][0]], None)])
    (gx, acc_ic), _ = _in_proj_bwd(*dproj, 2, done=(gx_a, gx_b))
    gx, acc_i = gx.reshape(xs.shape), acc_ia + acc_ib + acc_ic
    red_in = totals(("w_in",), sums_in, [others_in])

    gs = {
        "norm_mix_w": acc_i[0:1], "conv_w": gconv[0:4][None], "conv_b": gconv[4:5],
        "dt_bias": ghead[0:1, :NH_SSD], "a_log": ghead[2:3, :NH_SSD], "d_skip": ghead[3:4, :NH_SSD],
        "ssd_norm_w": glane[1:2], "hg_lower_bounds": acc_h[2:4], "hg_norm_w": acc_h[4:5, :128],
        "norm_xa_w": acc_a[0:1], "norm_mem_w": g_nmem, "norm_ffn_w": acc_f[2:3], "norm_final_w": acc_f[1],
    }
    loss = (0.5 / D) * jnp.sum(acc_f[0])
    small_parts = [gs[name] for name in SMALL] + [loss.reshape(1)]
    small_shapes = [gs[name].shape for name in SMALL] + [(1,)]
    (g_in,), (packed,) = _run_phases([("swap", red_in, [half["w_in"]]),
                                      ("gather", [_pack_small(small_parts)], [None])], "grads_finish")
    g_big = dict(zip(GROUP_FFN + GROUP_ATTN + ("w_in",), g_ffn + g_attn + [g_in]))
    small = _unpack_small(_sum8(packed, "small_total"), small_shapes)
    g_small = dict(zip(SMALL, small[:-1]))
    loss_all = small[-1][0]
    g_small["conv_w"] = lax.dynamic_slice_in_dim(g_small["conv_w"], chip * 384, 384, 2)

    grads, delta, new_m, new_v = {}, {}, {}, {}
    for name in BIG:
        outs = tuple(_adamw(wsh[name], g_big[name], shard(m, name), shard(v, name), "adamw_" + name))
        if name in TRANSPOSED:
            outs = tuple(jnp.swapaxes(o, 1, 2) for o in outs)
        grads[name], delta[name], new_m[name], new_v[name] = outs
    shapes = [w[name].shape for name in SMALL]
    packs = [_pack_small([t[name] for name in SMALL]) for t in (w, g_small, m, v)]
    outs = _adamw(packs[0][None], packs[1], packs[2][None], packs[3][None], "adamw_small")[1:]
    for name, g_, d_, nm_, nv_ in zip(SMALL, [g_small[n] for n in SMALL], *[_unpack_small(o[0], shapes) for o in outs]):
        grads[name] = g_.reshape(w[name].shape)
        delta[name], new_m[name], new_v[name] = d_, nm_, nv_

    return (loss_all, gx[None], *[grads[n] for n in WEIGHTS], *[delta[n] for n in WEIGHTS],
            *[new_m[n] for n in WEIGHTS], *[new_v[n] for n in WEIGHTS])
```

```python
import jax
import jax.numpy as jnp
from jax import lax
from jax.experimental import pallas as pl
from jax.experimental.pallas import tpu as pltpu

F32 = jnp.float32
BF = jnp.bfloat16
MESH = pl.DeviceIdType.MESH
SDS = jax.ShapeDtypeStruct
ANY = pl.BlockSpec(memory_space=pl.ANY)

D = 1024
EPS = 1e-6
NH_SSD = 16
SSD_P = 64
NH_HG = 8
Q = 128
CH = 2
SUB = 32
NSUB = Q // SUB
HG_LB_FLOOR = 1e-2
XA_HEADS = 4
XA_HD = 256
MEM_LEN = 256
FFN = 2816
TL = 512
TL_FFN = 256
VMEM_LIMIT = 56 << 20
MATMUL_VMEM = 40 << 20

N_IN = 6672
Z0, XBC0, DT0, HQ0, HF0, HI0, HG0 = 0, 1024, 2560, 2576, 3600, 4624, 5648

ADAM_LR, ADAM_B1, ADAM_B2, ADAM_EPS, ADAM_WD, ADAM_STEP = 0.001, 0.9, 0.999, 1e-08, 0.01, 10

BIG = ("w_in", "w_out", "xa_wq", "xa_wkv", "xa_wo", "ffn_w_gate", "ffn_w_up", "ffn_w_down")
TRANSPOSED = ("w_in", "ffn_w_gate", "ffn_w_up")
SMALL = ("norm_mix_w", "conv_w", "conv_b", "dt_bias", "a_log", "d_skip", "ssd_norm_w", "hg_lower_bounds",
         "hg_norm_w", "norm_xa_w", "norm_mem_w", "norm_ffn_w", "norm_final_w")
WEIGHTS = ("norm_mix_w", "w_in", "conv_w", "conv_b", "dt_bias", "a_log", "d_skip", "ssd_norm_w", "hg_lower_bounds",
           "hg_norm_w", "w_out", "norm_xa_w", "norm_mem_w", "xa_wq", "xa_wkv", "xa_wo", "norm_ffn_w", "ffn_w_gate",
           "ffn_w_up", "ffn_w_down", "norm_final_w")


def _cparams():
    return pltpu.CompilerParams(dimension_semantics=("arbitrary",), vmem_limit_bytes=VMEM_LIMIT)


def _const(shape):
    return pl.BlockSpec(shape, lambda i: (0,) * len(shape))


def _resident(shape):
    return pl.BlockSpec(shape, lambda i: (0,) * len(shape), pipeline_mode=pl.Buffered(1))


def _rows(tl, n):
    return pl.BlockSpec((tl, n), lambda i: (i, 0))


def _dot(a, b):
    return jnp.dot(a.astype(BF), b.astype(BF), preferred_element_type=F32)


def _dot_nt(a, b):
    return lax.dot_general(a.astype(BF), b.astype(BF), (((1,), (1,)), ((), ())), preferred_element_type=F32)


def _dot_tn(a, b):
    return lax.dot_general(a.astype(BF), b.astype(BF), (((0,), (0,)), ((), ())), preferred_element_type=F32)


def _split(v, passes):
    parts, rest = [], v
    for p in range(passes):
        hi = rest.astype(BF)
        parts.append(hi)
        if p + 1 < passes:
            rest = rest - hi.astype(F32)
    return parts


def _sel_dot(a, sel, passes=3):
    sb = sel.astype(BF)
    out = None
    for part in _split(a, passes):
        t = jnp.dot(part, sb, preferred_element_type=F32)
        out = t if out is None else out + t
    return out


def _dot_sel(sel, b, passes=3):
    sb = sel.astype(BF)
    out = None
    for part in _split(b, passes):
        t = jnp.dot(sb, part, preferred_element_type=F32)
        out = t if out is None else out + t
    return out


def _iota(shape, dim):
    return lax.broadcasted_iota(jnp.int32, shape, dim)


def _sigmoid(v):
    return 0.5 * jnp.tanh(0.5 * v) + 0.5


def _rms(v, w):
    r = lax.rsqrt(jnp.mean(v * v, axis=-1, keepdims=True) + EPS)
    n = v * r
    return n * w, n, r


def _rms_bwd(dy, n, r, w):
    dn = dy * w
    return r * (dn - n * jnp.mean(dn * n, axis=-1, keepdims=True)), dy * n


def _colsum(v):
    return jnp.sum(v, axis=0, keepdims=True)


def _zero_first(*refs):
    @pl.when(pl.program_id(0) == 0)
    def _():
        for r in refs:
            r[...] = jnp.zeros_like(r)


def _in_proj(x, nw, wt, phases=()):
    L = x.shape[0]
    tl = min(TL, L)

    def body(x_ref, nw_ref, w_ref, h0_ref, z_ref, xbc_ref, hq_ref, hf_ref, hi_ref, hg_ref, dt_ref):
        h, _, _ = _rms(x_ref[...], nw_ref[...])
        hb = h.astype(BF)
        h0_ref[...] = hb

        def proj(a, b):
            return _dot_nt(hb, w_ref[a:b, :])

        z_ref[...] = proj(Z0, XBC0).astype(BF)
        xbc_ref[...] = proj(XBC0, DT0).astype(BF)
        dt_ref[...] = proj(DT0, DT0 + 128)
        hq_ref[...] = proj(HQ0, HF0).astype(BF)
        hf_ref[...] = proj(HF0, HI0)
        hi_ref[...] = proj(HI0, HG0).astype(BF)
        hg_ref[...] = proj(HG0, N_IN).astype(BF)

    outs = [SDS((L, D), BF), SDS((L, D), BF), SDS((L, 1536), BF), SDS((L, D), BF), SDS((L, D), F32),
            SDS((L, D), BF), SDS((L, D), BF), SDS((L, 128), F32)]
    steps = L // tl
    return _call(body, (x, nw, wt), name="in_proj", grid=(steps,),
                 in_specs=[_rows(tl, D), _const((1, D)), _resident((N_IN, D))],
                 out_specs=[_rows(tl, o.shape[1]) for o in outs], out_shape=outs, phases=phases,
                 mid_step=(3 * steps) // 4)


def _mem_kv(mem, nw, wkv4):
    def body(m_ref, nw_ref, w_ref, k_ref, v_ref):
        m, _, _ = _rms(m_ref[...], nw_ref[...])
        mb = m.astype(BF)
        for i in range(2):
            sl = slice(512 * i, 512 * i + 512)
            k_ref[:, sl] = jnp.dot(mb, w_ref[i], preferred_element_type=F32).astype(BF)
            v_ref[:, sl] = jnp.dot(mb, w_ref[2 + i], preferred_element_type=F32).astype(BF)

    outs = [SDS((MEM_LEN, D), BF)] * 2
    return pl.pallas_call(
        body, grid=(1,), name="mem_kv",
        in_specs=[_const((MEM_LEN, D)), _const((1, D)), _const((4, D, 512))],
        out_specs=[_const((MEM_LEN, D))] * 2, out_shape=outs, compiler_params=_cparams())(mem, nw, wkv4)


def _mem_kv_bwd(mem, nw, wkv4, dk, dv):
    def body(m_ref, nw_ref, w_ref, dk_ref, dv_ref, gnw_ref, gw_ref):
        m, n, _ = _rms(m_ref[...], nw_ref[...])
        mb = m.astype(BF)
        dm = jnp.zeros((MEM_LEN, D), F32)
        for i in range(4):
            src = dk_ref if i < 2 else dv_ref
            d = src[:, 512 * (i % 2):512 * (i % 2) + 512].astype(BF)
            gw_ref[i] = _dot_tn(mb, d)
            dm = dm + _dot_nt(d, w_ref[i])
        gnw_ref[...] = _colsum(dm * n)

    return pl.pallas_call(
        body, grid=(1,), name="mem_kv_bwd",
        in_specs=[_const((MEM_LEN, D)), _const((1, D)), _const((4, D, 512)), _const((MEM_LEN, D)), _const((MEM_LEN, D))],
        out_specs=[_const((1, D)), _const((4, D, 512))],
        out_shape=[SDS((1, D), F32), SDS((4, D, 512), F32)], compiler_params=_cparams())(mem, nw, wkv4, dk, dv)


def _softmax_rows(sc):
    e = jnp.exp(sc - jnp.max(sc, axis=-1, keepdims=True))
    return e * (1.0 / jnp.sum(e, axis=-1, keepdims=True))


def _attn_fwd(x, ya, ob, w_out, nxa, wq, k, v, wo):
    L = x.shape[0]
    tl = min(TL, L)
    scale = XA_HD ** -0.5

    def body(x_ref, ya_ref, ob_ref, wout_ref, nxa_ref, wq_ref, k_ref, v_ref, wo_ref,
             x1_ref, x2_ref, hxa_ref, q_ref, ox_ref):
        x1 = x_ref[...] + jnp.dot(ya_ref[...], wout_ref[:D, :], preferred_element_type=F32) \
            + jnp.dot(ob_ref[...], wout_ref[D:, :], preferred_element_type=F32)
        x1_ref[...] = x1
        h, _, _ = _rms(x1, nxa_ref[...])
        hb = h.astype(BF)
        hxa_ref[...] = hb
        qb = jnp.dot(hb, wq_ref[...], preferred_element_type=F32).astype(BF)
        q_ref[...] = qb
        heads = [slice(hd * XA_HD, (hd + 1) * XA_HD) for hd in range(XA_HEADS)]
        ps = [_softmax_rows(_dot_nt(qb[:, sl], k_ref[:, sl]) * scale) for sl in heads]
        oxs = [_dot(p, v_ref[:, sl]) for p, sl in zip(ps, heads)]
        oxb = jnp.concatenate(oxs, axis=1).astype(BF)
        ox_ref[...] = oxb
        x2_ref[...] = x1 + jnp.dot(oxb, wo_ref[...], preferred_element_type=F32)

    outs = [SDS((L, D), F32), SDS((L, D), F32), SDS((L, D), BF), SDS((L, D), BF), SDS((L, D), BF)]
    return pl.pallas_call(
        body, grid=(L // tl,), name="attn_fwd",
        in_specs=[_rows(tl, D), _rows(tl, D), _rows(tl, D), _resident((2 * D, D)), _const((1, D)), _resident((D, D)),
                  _resident((MEM_LEN, D)), _resident((MEM_LEN, D)), _resident((D, D))],
        out_specs=[_rows(tl, D)] * 5, out_shape=outs, compiler_params=_cparams())(x, ya, ob, w_out, nxa, wq, k, v, wo)


def _ffn_loss(x2, tgt, nffn, nfin, wgt, wut, wd):
    L = x2.shape[0]
    tl = min(TL_FFN, L)

    def body(x2_ref, t_ref, nffn_ref, nfin_ref, wg_ref, wu_ref, wd_ref,
             dx2_ref, h_ref, a_ref, dx3_ref, dg_ref, du_ref, acc_ref):
        _zero_first(acc_ref)
        x2v = x2_ref[...]
        h, n2, r2 = _rms(x2v, nffn_ref[...])
        hb = h.astype(BF)
        h_ref[...] = hb
        g = _dot_nt(hb, wg_ref[...])
        u = _dot_nt(hb, wu_ref[...])
        sg = _sigmoid(g)
        ab = (g * sg * u).astype(BF)
        a_ref[...] = ab
        x3 = x2v + jnp.dot(ab, wd_ref[...], preferred_element_type=F32)
        y, n3, r3 = _rms(x3, nfin_ref[...])
        err = y - t_ref[...]
        acc_ref[0:1, :] += _colsum(err * err)
        dx3, dwf = _rms_bwd(err * (1.0 / D), n3, r3, nfin_ref[...])
        acc_ref[1:2, :] += _colsum(dwf)
        dx3b = dx3.astype(BF)
        dx3_ref[...] = dx3b
        da = _dot_nt(dx3b, wd_ref[...])
        dgb = (da * u * sg * (1.0 + g * (1.0 - sg))).astype(BF)
        dub = (da * g * sg).astype(BF)
        dg_ref[...] = dgb
        du_ref[...] = dub
        dh = jnp.dot(dgb, wg_ref[...], preferred_element_type=F32) + jnp.dot(dub, wu_ref[...], preferred_element_type=F32)
        dn, dwn = _rms_bwd(dh, n2, r2, nffn_ref[...])
        acc_ref[2:3, :] += _colsum(dwn)
        dx2_ref[...] = dx3 + dn

    outs = [SDS((L, D), F32), SDS((L, D), BF), SDS((L, FFN), BF), SDS((L, D), BF), SDS((L, FFN), BF),
            SDS((L, FFN), BF), SDS((8, D), F32)]
    wspec = _resident((FFN, D))
    return pl.pallas_call(
        body, grid=(L // tl,), name="ffn_loss",
        in_specs=[_rows(tl, D), _rows(tl, D), _const((1, D)), _const((1, D)), wspec, wspec, wspec],
        out_specs=[_rows(tl, D), _rows(tl, D), _rows(tl, FFN), _rows(tl, D), _rows(tl, FFN), _rows(tl, FFN),
                   _const((8, D))],
        out_shape=outs, compiler_params=_cparams())(x2, tgt, nffn, nfin, wgt, wut, wd)


def _attn_bwd(dx2, x1, q, k, v, nxa, wq, wo, w_out, phases=()):
    L = dx2.shape[0]
    tl = min(TL, L)
    scale = XA_HD ** -0.5

    def body(dx2_ref, x1_ref, q_ref, k_ref, v_ref, nxa_ref, wq_ref, wo_ref, wout_ref,
             dx1_ref, dya_ref, dob_ref, dq_ref, dk_ref, dv_ref, acc_ref):
        _zero_first(dk_ref, dv_ref, acc_ref)
        dx2v = dx2_ref[...]
        dox = _dot_nt(dx2v, wo_ref[...]).astype(BF)
        qb = q_ref[...]
        heads = [slice(hd * XA_HD, (hd + 1) * XA_HD) for hd in range(XA_HEADS)]
        ps = [_softmax_rows(_dot_nt(qb[:, sl], k_ref[:, sl]) * scale) for sl in heads]
        dps = [_dot_nt(dox[:, sl], v_ref[:, sl]) for sl in heads]
        dss = [(p * (dp - jnp.sum(dp * p, axis=-1, keepdims=True)) * scale).astype(BF) for p, dp in zip(ps, dps)]
        for sl, p, ds in zip(heads, ps, dss):
            dv_ref[:, sl] += _dot_tn(p, dox[:, sl])
            dk_ref[:, sl] += _dot_tn(ds, qb[:, sl])
        dqs = [_dot(ds, k_ref[:, sl]) for sl, ds in zip(heads, dss)]
        dqb = jnp.concatenate(dqs, axis=1).astype(BF)
        dq_ref[...] = dqb
        dh = _dot_nt(dqb, wq_ref[...])
        _, n1, r1 = _rms(x1_ref[...], nxa_ref[...])
        dn, dwn = _rms_bwd(dh, n1, r1, nxa_ref[...])
        acc_ref[0:1, :] += _colsum(dwn)
        dx1 = dx2v + dn
        dx1_ref[...] = dx1
        dx1b = dx1.astype(BF)
        dya_ref[...] = _dot_nt(dx1b, wout_ref[:D, :]).astype(BF)
        dob_ref[...] = _dot_nt(dx1b, wout_ref[D:, :]).astype(BF)

    outs = [SDS((L, D), F32), SDS((L, D), BF), SDS((L, D), BF), SDS((L, D), BF), SDS((MEM_LEN, D), F32),
            SDS((MEM_LEN, D), F32), SDS((8, D), F32)]
    return _call(body, (dx2, x1, q, k, v, nxa, wq, wo, w_out), name="attn_bwd", grid=(L // tl,),
                 in_specs=[_rows(tl, D), _rows(tl, D), _rows(tl, D), _resident((MEM_LEN, D)), _resident((MEM_LEN, D)),
                           _const((1, D)), _resident((D, D)), _resident((D, D)), _resident((2 * D, D))],
                 out_specs=[_rows(tl, D)] * 4 + [_const((MEM_LEN, D)), _const((MEM_LEN, D)), _const((8, D))],
                 out_shape=outs, phases=phases)


IN_BWD_PARTS = ((0, 1), (1, 4), (4, 8))


def _in_proj_bwd(x, dx1, dz, dxbc, dhq, dhf, dhi, dhg, ddt, nw, wt, part, done=(), phases=()):
    tl = min(TL, x.shape[0] // 8)
    eighth = x.shape[0] // 8 // tl
    first, steps = IN_BWD_PARTS[part][0] * eighth, (IN_BWD_PARTS[part][1] - IN_BWD_PARTS[part][0]) * eighth
    L = steps * tl
    rows = lambda n: pl.BlockSpec((tl, n), lambda i: (i + first, 0))
    starts = [sum(d.shape[0] for d in done[:k]) // tl for k in range(len(done))]
    assert sum(d.shape[0] for d in done) in (0, L)

    def body(x_ref, dx1_ref, dz_ref, dxbc_ref, dhq_ref, dhf_ref, dhi_ref, dhg_ref, ddt_ref, nw_ref, w_ref, *rest):
        done_refs, (gx_ref, acc_ref) = rest[:-2], rest[-2:]
        step = pl.program_id(0)
        _zero_first(acc_ref)
        dh = _dot(dz_ref[...], w_ref[Z0:XBC0, :]) + _dot(dxbc_ref[...], w_ref[XBC0:DT0, :]) \
            + _dot(ddt_ref[...], w_ref[DT0:DT0 + 128, :]) + _dot(dhq_ref[...], w_ref[HQ0:HF0, :]) \
            + _dot(dhf_ref[...], w_ref[HF0:HI0, :]) + _dot(dhi_ref[...], w_ref[HI0:HG0, :]) \
            + _dot(dhg_ref[...], w_ref[HG0:N_IN, :])
        _, n, r = _rms(x_ref[...], nw_ref[...])
        dn, dwn = _rms_bwd(dh, n, r, nw_ref[...])
        acc_ref[0:1, :] += _colsum(dwn)
        if not done:
            gx_ref[...] = dx1_ref[...] + dn
            return
        gx_ref[1] = dx1_ref[...] + dn
        for ref, start, piece in zip(done_refs, starts, done):
            @pl.when(jnp.logical_and(step >= start, step < start + piece.shape[0] // tl))
            def _(ref=ref):
                gx_ref[0] = ref[...]

    def piece_spec(start, piece):
        return pl.BlockSpec((tl, D), lambda i: (jnp.clip(i - start, 0, piece.shape[0] // tl - 1), 0))

    gx_spec, gx_shape = (pl.BlockSpec((2, tl, D), lambda i: (0, i, 0)), (2, L, D)) if done else (_rows(tl, D), (L, D))
    return _call(
        body, (x, dx1, dz, dxbc, dhq, dhf, dhi, dhg, ddt, nw, wt, *done), grid=(steps,), name="in_proj_bwd_%d" % part,
        in_specs=[rows(D), rows(D), rows(D), rows(1536), rows(D), rows(D), rows(D), rows(D), rows(128),
                  _const((1, D)), _resident((N_IN, D))] + [piece_spec(s, d) for s, d in zip(starts, done)],
        out_specs=[gx_spec, _const((8, D))], out_shape=[SDS(gx_shape, F32), SDS((8, D), F32)], phases=phases)


def _gw_in(h0, dz, dxbc, ddt, dhq, dhf, dhi, dhg, phases=()):
    L = h0.shape[0]
    tl = min(512, L)

    def body(h_ref, dz_ref, dxbc_ref, ddt_ref, dhq_ref, dhf_ref, dhi_ref, dhg_ref, o_ref):
        _zero_first(o_ref)
        hb = h_ref[...]
        o_ref[Z0:XBC0, :] += _dot_tn(dz_ref[...], hb)
        o_ref[XBC0:DT0, :] += _dot_tn(dxbc_ref[...], hb)
        o_ref[DT0:HQ0, :] += _dot_tn(ddt_ref[...], hb)[0:NH_SSD, :]
        o_ref[HQ0:HF0, :] += _dot_tn(dhq_ref[...], hb)
        o_ref[HF0:HI0, :] += _dot_tn(dhf_ref[...], hb)
        o_ref[HI0:HG0, :] += _dot_tn(dhi_ref[...], hb)
        o_ref[HG0:N_IN, :] += _dot_tn(dhg_ref[...], hb)

    return _call(body, (h0, dz, dxbc, ddt, dhq, dhf, dhi, dhg), name="gw_in", grid=(L // tl,),
                 in_specs=[_rows(tl, D), _rows(tl, D), _rows(tl, 1536), _rows(tl, 128), _rows(tl, D), _rows(tl, D),
                           _rows(tl, D), _rows(tl, D)],
                 out_specs=[_const((N_IN, D))], out_shape=[SDS((N_IN, D), F32)], phases=phases)


def _token_tile(L, out_bytes, row_bytes):
    tl = min(2048, L)
    while tl > 256 and out_bytes + 2 * tl * row_bytes > MATMUL_VMEM:
        tl //= 2
    return tl


def _matmul_tn(a, b, name):
    L, M = a.shape
    N = b.shape[1]
    tl = _token_tile(L, 4 * M * N, M * a.dtype.itemsize + N * b.dtype.itemsize)

    def body(a_ref, b_ref, o_ref):
        _zero_first(o_ref)
        o_ref[...] += _dot_tn(a_ref[...], b_ref[...])

    return pl.pallas_call(
        body, grid=(L // tl,), name=name, in_specs=[_rows(tl, M), _rows(tl, N)], out_specs=_const((M, N)),
        out_shape=SDS((M, N), F32), compiler_params=_cparams())(a, b)


def _matmul_tn_pair(a0, a1, b, name):
    L, M = a0.shape
    N = b.shape[1]
    tl = _token_tile(L, 8 * M * N, 2 * M * a0.dtype.itemsize + N * b.dtype.itemsize)

    def body(a0_ref, a1_ref, b_ref, o_ref):
        _zero_first(o_ref)
        bv = b_ref[...].astype(BF)
        o_ref[:M, :] += _dot_tn(a0_ref[...], bv)
        o_ref[M:, :] += _dot_tn(a1_ref[...], bv)

    return pl.pallas_call(
        body, grid=(L // tl,), name=name, in_specs=[_rows(tl, M), _rows(tl, M), _rows(tl, N)],
        out_specs=_const((2 * M, N)), out_shape=SDS((2 * M, N), F32), compiler_params=_cparams())(a0, a1, b)


def _head_expand():
    e = (jnp.right_shift(_iota((128, D), 1), 6) == _iota((128, D), 0)).astype(BF)
    et = (jnp.right_shift(_iota((D, 128), 0), 6) == _iota((D, 128), 1)).astype(BF)
    return e, et


def _conv_shifts(cur, other, up):
    rows = _iota((Q, 1), 0)
    out = []
    for s in (1, 2, 3):
        if up:
            out.append(jnp.where(rows >= Q - s, pltpu.roll(other, Q - s, 0), pltpu.roll(cur, Q - s, 0)))
        else:
            out.append(jnp.where(rows < s, pltpu.roll(other, s, 0), pltpu.roll(cur, s, 0)))
    return out


def _ssd_pre(u, dtr, dtb, alog):
    e, et = _head_expand()
    sgu = _sigmoid(u)
    xc = u * sgu
    lane = _iota((1, 128), 1)
    hmask = (lane < NH_SSD).astype(F32)
    pre = dtr + dtb
    dt = (jnp.maximum(pre, 0.0) + jnp.log(1.0 + jnp.exp(-jnp.abs(pre)))) * hmask
    a_row = -jnp.exp(alog)
    causal = _iota((Q, Q), 1) <= _iota((Q, Q), 0)
    tri = causal.astype(BF)
    acum = _dot_sel(tri, dt * a_row)
    acum_full = _sel_dot(acum, e)
    alast_full = acum_full[Q - 1:Q, :]
    dt_full = _sel_dot(dt, e)
    xs = xc[:, :D]
    return dict(e=e, et=et, sgu=sgu, xs=xs, bm=xc[:, D:D + 256], cm=xc[:, D + 256:], hmask=hmask, pre=pre, dt=dt,
                a_row=a_row, causal=causal, tri=tri, acum=acum, acum_t=acum.T, eA_full=jnp.exp(acum_full),
                dte_full=jnp.exp(alast_full - acum_full), dt_full=dt_full, xdt=xs * dt_full)


def _ssd_decay(pre, hh, cb):
    seg = pre["acum"][:, hh:hh + 1] - pre["acum_t"][hh:hh + 1, :]
    lm = jnp.where(pre["causal"], jnp.exp(jnp.minimum(seg, 0.0)), 0.0)
    return lm, cb * lm


def _ssd_fwd(xbc, dtr, z, conv_w, conv_b, dtb, alog, dskip_full, nw):
    L = xbc.shape[0]
    nc = L // Q

    def chunk(ck, xbc_ref, dtr_ref, z_ref, cw_ref, cb_ref, dtb_ref, alog_ref, dsk_ref, nw_ref,
              ya_ref, y_ref, u_ref, st_ref, prev_ref, s_ref):
        tok = slice(Q * ck, Q * ck + Q)
        xr = xbc_ref[tok, :].astype(F32)
        sh = _conv_shifts(xr, prev_ref[...], up=False)
        u = cb_ref[...] + cw_ref[3:4, :] * xr + cw_ref[2:3, :] * sh[0] + cw_ref[1:2, :] * sh[1] + cw_ref[0:1, :] * sh[2]
        prev_ref[...] = xr
        ub = u.astype(BF)
        u_ref[tok, :] = ub
        pre = _ssd_pre(ub.astype(F32), dtr_ref[tok, :], dtb_ref[...], alog_ref[...])
        lo = _iota((1, 128), 1) < SSD_P
        s_old = s_ref[...]
        st_ref[ck] = s_old
        ys = []
        for g in range(2):
            bg, cg = pre["bm"][:, 128 * g:128 * g + 128], pre["cm"][:, 128 * g:128 * g + 128]
            cb = _dot_nt(cg, bg)
            gs = slice(512 * g, 512 * g + 512)
            yd = []
            for j in range(4 * g, 4 * g + 4):
                xp = pre["xdt"][:, 128 * j:128 * j + 128].astype(BF)
                _, m0 = _ssd_decay(pre, 2 * j, cb)
                _, m1 = _ssd_decay(pre, 2 * j + 1, cb)
                yd.append(jnp.where(lo, _dot(m0, xp), _dot(m1, xp)))
            yoff = _dot_nt(cg, s_old[gs, :]) * pre["eA_full"][:, gs]
            ys.append(jnp.concatenate(yd, axis=1) + yoff)
            st = _dot_tn((pre["xdt"] * pre["dte_full"])[:, gs], bg)
            cdcol = jnp.exp(_dot_sel(pre["et"][gs, :], pre["acum_t"])[:, Q - 1:Q])
            s_ref[gs, :] = s_old[gs, :] * cdcol + st
        y = jnp.concatenate(ys, axis=1) + dsk_ref[...] * pre["xs"]
        yb = y.astype(BF)
        y_ref[tok, :] = yb
        zf = z_ref[tok, :].astype(F32)
        yz = yb.astype(F32) * zf * _sigmoid(zf)
        outs = []
        for g in range(2):
            gs = slice(512 * g, 512 * g + 512)
            o, _, _ = _rms(yz[:, gs], nw_ref[:, gs])
            outs.append(o)
        ya_ref[tok, :] = jnp.concatenate(outs, axis=1).astype(BF)

    def body(*refs):
        _zero_first(*refs[-2:])
        for ck in range(CH):
            chunk(ck, *refs)

    outs = [SDS((L, D), BF), SDS((L, D), BF), SDS((L, 1536), BF), SDS((nc, D, 128), F32)]
    return pl.pallas_call(
        body, grid=(nc // CH,), name="ssd_fwd",
        in_specs=[_rows(CH * Q, 1536), _rows(CH * Q, 128), _rows(CH * Q, D), _const((4, 1536)), _const((1, 1536)), _const((1, 128)),
                  _const((1, 128)), _const((1, D)), _const((1, D))],
        out_specs=[_rows(CH * Q, D), _rows(CH * Q, D), _rows(CH * Q, 1536),
                   pl.BlockSpec((CH, D, 128), lambda i: (i, 0, 0))],
        out_shape=outs, scratch_shapes=[pltpu.VMEM((Q, 1536), F32), pltpu.VMEM((D, 128), F32)],
        compiler_params=_cparams())(xbc, dtr, z, conv_w, conv_b, dtb, alog, dskip_full, nw)


def _ssd_bwd(dya, y, z, u, xbc, dtr, states, conv_w, dtb, alog, dskip_full, nw):
    L = dya.shape[0]
    nc = L // Q

    def chunk(ck, step, dya_ref, y_ref, z_ref, u_ref, xc_ref, dtr_ref, st_ref, cw_ref, dtb_ref, alog_ref, dsk_ref, nw_ref,
              dz_ref, dxbc_ref, ddt_ref, gconv_ref, ghead_ref, glane_ref, gs_ref, ndu_ref):
        tok = slice(Q * ck, Q * ck + Q)
        uf = u_ref[tok, :].astype(F32)
        pre = _ssd_pre(uf, dtr_ref[tok, :], dtb_ref[...], alog_ref[...])
        e, et, xs, xdt = pre["e"], pre["et"], pre["xs"], pre["xdt"]
        lane = _iota((1, 128), 1)
        lo = lane < SSD_P
        sub = _iota((128, 1), 0)
        zf = z_ref[tok, :].astype(F32)
        sgz = _sigmoid(zf)
        sz = zf * sgz
        yv = y_ref[tok, :].astype(F32)
        yz = yv * sz
        dyav = dya_ref[tok, :].astype(F32)
        dyz, dnw = [], []
        for g in range(2):
            gs = slice(512 * g, 512 * g + 512)
            _, n, r = _rms(yz[:, gs], nw_ref[:, gs])
            dv, dw = _rms_bwd(dyav[:, gs], n, r, nw_ref[:, gs])
            dyz.append(dv)
            dnw.append(dw)
        dyz = jnp.concatenate(dyz, axis=1)
        glane_ref[1:2, :] += _colsum(jnp.concatenate(dnw, axis=1))
        dy = dyz * sz
        dz_ref[tok, :] = (dyz * yv * sgz * (1.0 + zf * (1.0 - sgz))).astype(BF)
        glane_ref[0:1, :] += _colsum(dy * xs)
        dxs = dsk_ref[...] * dy

        s_in = st_ref[ck]
        gst = gs_ref[...]
        gy = dy * pre["eA_full"]
        xdte = xdt * pre["dte_full"]
        dacum = jnp.zeros((Q, 128), F32)
        dacum_t = jnp.zeros((128, Q), F32)
        dxdt, dacum_full, ddte_full, dbs, dcs = [], [], [], [], []
        for g in range(2):
            gs = slice(512 * g, 512 * g + 512)
            bg, cg = pre["bm"][:, 128 * g:128 * g + 128], pre["cm"][:, 128 * g:128 * g + 128]
            sg_, dg_ = s_in[gs, :], gst[gs, :]
            yoff = _dot_nt(cg, sg_) * pre["eA_full"][:, gs]
            dc = _dot(gy[:, gs], sg_)
            dsin = _dot_tn(gy[:, gs], cg)
            dacum_full.append(dy[:, gs] * yoff)
            tg = _dot_nt(bg, dg_)
            ddte_full.append(tg * xdt[:, gs])
            db = _dot(xdte[:, gs], dg_)
            cb = _dot_nt(cg, bg)
            dcb = jnp.zeros((Q, Q), F32)
            dxg = []
            for j in range(4 * g, 4 * g + 4):
                xp = xdt[:, 128 * j:128 * j + 128].astype(BF)
                dyp = dy[:, 128 * j:128 * j + 128]
                dxp = jnp.zeros((Q, 128), F32)
                for idx in range(2):
                    hh = 2 * j + idx
                    lm, m = _ssd_decay(pre, hh, cb)
                    dym = jnp.where(lo if idx == 0 else jnp.logical_not(lo), dyp, 0.0).astype(BF)
                    dm = jnp.where(pre["causal"], _dot_nt(dym, xp), 0.0)
                    w = dm * m
                    dacum = dacum + jnp.where(lane == hh, jnp.sum(w, axis=1, keepdims=True), 0.0)
                    dacum_t = dacum_t + jnp.where(sub == hh, jnp.sum(w, axis=0, keepdims=True), 0.0)
                    dcb = dcb + dm * lm
                    dxp = dxp + _dot_tn(m, dym)
                dxg.append(dxp)
            dxdt.append(jnp.concatenate(dxg, axis=1) + tg * pre["dte_full"][:, gs])
            dcs.append(dc + _dot(dcb, bg))
            dbs.append(db + _dot_tn(dcb, cg))
            cdcol = jnp.exp(_dot_sel(et[gs, :], pre["acum_t"])[:, Q - 1:Q])
            gs_ref[gs, :] = dsin + dg_ * cdcol
        dxdt = jnp.concatenate(dxdt, axis=1)
        dacum = dacum + _sel_dot(jnp.concatenate(dacum_full, axis=1), et, 2) - dacum_t.T
        alast = pre["acum"][Q - 1:Q, :]
        dte = jnp.exp(alast - pre["acum"])
        ddte = _sel_dot(jnp.concatenate(ddte_full, axis=1), et, 2) * dte
        dacum = dacum - ddte
        dcd_col = jnp.sum(_dot_sel(e, gst * s_in, 2), axis=1, keepdims=True)
        dcd_row = jnp.broadcast_to(dcd_col, (128, 128)).T[0:1, :]
        dalast = _colsum(ddte) + dcd_row * jnp.exp(alast)
        dacum = dacum + jnp.where(_iota((Q, 1), 0) == Q - 1, dalast, 0.0)
        ddt = _sel_dot(dxdt * xs, et, 2)
        dxs = dxs + dxdt * pre["dt_full"]
        dda = _dot_sel((_iota((Q, Q), 1) >= _iota((Q, Q), 0)).astype(BF), dacum)
        ddt = ddt + dda * pre["a_row"]
        ghead_ref[1:2, :] += _colsum(dda * pre["dt"])
        ddtr = ddt * _sigmoid(pre["pre"]) * pre["hmask"]
        ghead_ref[0:1, :] += _colsum(ddtr)
        ddt_ref[tok, :] = ddtr

        dxc = jnp.concatenate([dxs] + dbs + dcs, axis=1)
        sgu = pre["sgu"]
        du = dxc * sgu * (1.0 + uf * (1.0 - sgu))
        shu = _conv_shifts(du, ndu_ref[...], up=True)
        dxr = cw_ref[3:4, :] * du + cw_ref[2:3, :] * shu[0] + cw_ref[1:2, :] * shu[1] + cw_ref[0:1, :] * shu[2]
        ndu_ref[...] = du
        dxbc_ref[tok, :] = dxr.astype(BF)
        xr = xc_ref[tok, :].astype(F32)
        gconv_ref[3:4, :] += _colsum(du * xr)
        gconv_ref[2:3, :] += _colsum(shu[0] * xr)
        gconv_ref[1:2, :] += _colsum(shu[1] * xr)
        gconv_ref[0:1, :] += _colsum(shu[2] * xr)
        gconv_ref[4:5, :] += _colsum(du)

        @pl.when(jnp.logical_and(step == nc // CH - 1, ck == 0))
        def _():
            ghead_ref[2:3, :] = ghead_ref[1:2, :] * pre["a_row"]
            ghead_ref[3:4, :] = _sel_dot(glane_ref[...], et)[0:1, :]

    def body(*refs):
        _zero_first(*refs[-5:])
        for ck in reversed(range(CH)):
            chunk(ck, pl.program_id(0), *refs)

    rev = lambda i: (nc // CH - 1 - i, 0)
    outs = [SDS((L, D), BF), SDS((L, 1536), BF), SDS((L, 128), F32), SDS((8, 1536), F32), SDS((8, 128), F32),
            SDS((8, D), F32)]
    return pl.pallas_call(
        body, grid=(nc // CH,), name="ssd_bwd",
        in_specs=[pl.BlockSpec((CH * Q, D), rev), pl.BlockSpec((CH * Q, D), rev), pl.BlockSpec((CH * Q, D), rev),
                  pl.BlockSpec((CH * Q, 1536), rev), pl.BlockSpec((CH * Q, 1536), rev),
                  pl.BlockSpec((CH * Q, 128), rev), pl.BlockSpec((CH, D, 128), lambda i: (nc // CH - 1 - i, 0, 0)),
                  _const((4, 1536)), _const((1, 128)), _const((1, 128)), _const((1, D)), _const((1, D))],
        out_specs=[pl.BlockSpec((CH * Q, D), rev), pl.BlockSpec((CH * Q, 1536), rev), pl.BlockSpec((CH * Q, 128), rev),
                   _const((8, 1536)), _const((8, 128)), _const((8, D))],
        out_shape=outs, scratch_shapes=[pltpu.VMEM((D, 128), F32), pltpu.VMEM((Q, 1536), F32)],
        compiler_params=_cparams())(dya, y, z, u, xbc, dtr, states, conv_w, dtb, alog, dskip_full, nw)


def _hg_gates(hq, hf, hlb):
    h0, h1 = hlb[0:1, :], hlb[1:2, :]
    mx = jnp.maximum(h0, h1)
    e0, e1 = jnp.exp(h0 - mx), jnp.exp(h1 - mx)
    lb = e0 / (e0 + e1)
    sg = _sigmoid(hf)
    fg = lb + (1.0 - lb) * sg
    tri = (_iota((Q, Q), 1) <= _iota((Q, Q), 0)).astype(BF)
    return hq * _sigmoid(hq), 1.0 - fg, fg, sg, lb, e1 / (e0 + e1), _dot_sel(tri, jnp.log(fg))


def _hg_intra(b, q, k):
    rowblk = jnp.right_shift(_iota((Q, 1), 0), SUB.bit_length() - 1)
    mids = [b[SUB * i + SUB // 2:SUB * i + SUB // 2 + 1, :] for i in range(NSUB)]
    prevs = [mids[0]] + [b[SUB * i - 1:SUB * i, :] for i in range(1, NSUB)]
    mfull = jnp.concatenate([jnp.broadcast_to(r, (SUB, 128)) for r in mids], axis=0)
    rfull = jnp.concatenate([jnp.broadcast_to(r, (SUB, 128)) for r in prevs], axis=0)
    eqd, ek, eqo = jnp.exp(b - mfull), jnp.exp(mfull - b), jnp.exp(b - rfull)
    qd, qo, khat = q * eqd, q * eqo, k * ek
    rtab = jnp.concatenate(prevs, axis=0)
    djs = [jnp.exp(rtab - mids[j]) for j in range(NSUB)]
    zero = jnp.zeros((SUB, 128), F32)
    cols = []
    for j in range(NSUB):
        pieces = []
        for i in range(NSUB):
            rs = slice(SUB * i, SUB * i + SUB)
            pieces.append(zero if i < j else qd[rs] if i == j else qo[rs] * djs[j][i:i + 1, :])
        cols.append(jnp.concatenate(pieces, axis=0))
    qt = jnp.concatenate(cols, axis=1).astype(BF)
    kt = jnp.concatenate([jnp.where(rowblk == j, khat, 0.0) for j in range(NSUB)], axis=1).astype(BF)
    causal = _iota((Q, Q), 1) <= _iota((Q, Q), 0)
    att = jnp.where(causal, _dot_nt(qt, kt), 0.0)
    return att, qt, kt, (eqd, ek, eqo, djs), causal


def _hg_intra_bwd(dqt, dkt, qt, kt, factors):
    eqd, ek, eqo, djs = factors
    dqd, dqo, dkh, db = [], [], [], []
    for i in range(NSUB):
        rs = slice(SUB * i, SUB * i + SUB)
        diag = slice(128 * i, 128 * i + 128)
        dqd.append(dqt[rs, diag])
        dkh.append(dkt[rs, diag])
        dbi = qt[rs, diag].astype(F32) * dqt[rs, diag] - kt[rs, diag].astype(F32) * dkt[rs, diag]
        acc = jnp.zeros((SUB, 128), F32)
        for j in range(i):
            bl = slice(128 * j, 128 * j + 128)
            acc = acc + dqt[rs, bl] * djs[j][i:i + 1, :]
            dbi = dbi + qt[rs, bl].astype(F32) * dqt[rs, bl]
        dqo.append(acc)
        db.append(dbi)
    cat = lambda t: jnp.concatenate(t, axis=0)
    return cat(dqd) * eqd + cat(dqo) * eqo, cat(dkh) * ek, cat(db)


def _hg_att_exact(b, q, k, b_ref, q_ref, att_t_ref):
    b_ref[...] = b
    q_ref[...] = q
    att_t_ref[...] = jnp.zeros((Q, Q), F32)
    rows, lane = _iota((Q, 1), 0), _iota((1, Q), 1)

    def step(i, carry):
        e = jnp.exp(jnp.minimum(b_ref[pl.ds(i, 1), :] - b, 0.0))
        col = jnp.sum(q_ref[pl.ds(i, 1), :] * k * e, axis=1, keepdims=True)
        att_t_ref[...] = jnp.where(lane == i, jnp.where(rows <= i, col, 0.0), att_t_ref[...])
        return carry

    lax.fori_loop(0, Q, step, 0)
    return att_t_ref[...].T


def _hg_att_exact_bwd(da, b, q, k, b_ref, q_ref, da_t_ref, dq_ref, dk_ref):
    b_ref[...] = b
    q_ref[...] = q
    da_t_ref[...] = da.T
    dk_ref[...] = jnp.zeros((Q, 128), F32)
    lane = _iota((1, Q), 1)

    def step(i, carry):
        e = jnp.exp(jnp.minimum(b_ref[pl.ds(i, 1), :] - b, 0.0))
        g = jnp.sum(jnp.where(lane == i, da_t_ref[...], 0.0), axis=1, keepdims=True) * e
        dq_ref[pl.ds(i, 1), :] = jnp.sum(g * k, axis=0, keepdims=True)
        dk_ref[...] += g * q_ref[pl.ds(i, 1), :]
        return carry

    lax.fori_loop(0, Q, step, 0)
    dq, dk = dq_ref[...], dk_ref[...]
    return dq, dk, q * dq - k * dk


def _hg_fwd(hq, hf, hi, hg, hlb, nw, fast):
    L = hq.shape[0]
    nc = L // Q

    def chunk(exact, ck, hq_ref, hf_ref, hi_ref, hg_ref, hlb_ref, nw_ref, ob_ref, o_ref, st_ref, s_ref, *tmp):
        tok = slice(Q * ck, Q * ck + Q)
        qf, kf, _, _, _, _, bcum = _hg_gates(hq_ref[tok, :].astype(F32), hf_ref[tok, :], hlb_ref[...])
        gate = hg_ref[tok, :].astype(F32)
        heads = [slice(128 * h, 128 * h + 128) for h in range(NH_HG)]
        if exact:
            atts = [_hg_att_exact(bcum[:, sl], qf[:, sl], kf[:, sl], *tmp).astype(BF) for sl in heads]
        else:
            atts = [_hg_intra(bcum[:, sl], qf[:, sl], kf[:, sl])[0].astype(BF) for sl in heads]
        olds = [s_ref[sl, :] for sl in heads]
        outs_ = [_dot(att, hi_ref[tok, sl]) + _dot(qf[:, sl] * jnp.exp(bcum[:, sl]), s)
                 for att, sl, s in zip(atts, heads, olds)]
        for sl, s, o in zip(heads, olds, outs_):
            b, k = bcum[:, sl], kf[:, sl]
            st_ref[ck, sl, :] = s
            blast = b[Q - 1:Q, :]
            s_ref[sl, :] = s * jnp.exp(b.T[:, Q - 1:Q]) + _dot_tn(k * jnp.exp(blast - b), hi_ref[tok, sl])
            ob = o.astype(BF)
            o_ref[tok, sl] = ob
            on, _, _ = _rms(ob.astype(F32), nw_ref[...])
            gt = gate[:, sl]
            ob_ref[tok, sl] = (on * gt * _sigmoid(gt)).astype(BF)

    def run(exact, *refs):
        for ck in range(CH):
            chunk(exact, ck, *refs)

    def body(fast_ref, *refs):
        _zero_first(refs[9])
        pl.when(fast_ref[0] == 1)(lambda: run(False, *refs))
        pl.when(fast_ref[0] != 1)(lambda: run(True, *refs))

    rows = pl.BlockSpec((CH * Q, D), lambda i, f: (i, 0))
    outs = [SDS((L, D), BF), SDS((L, D), BF), SDS((nc, D, 128), F32)]
    grid_spec = pltpu.PrefetchScalarGridSpec(
        num_scalar_prefetch=1, grid=(nc // CH,),
        in_specs=[rows] * 4 + [pl.BlockSpec((2, D), lambda i, f: (0, 0)), pl.BlockSpec((1, 128), lambda i, f: (0, 0))],
        out_specs=[rows, rows, pl.BlockSpec((CH, D, 128), lambda i, f: (i, 0, 0))],
        scratch_shapes=[pltpu.VMEM((D, 128), F32), pltpu.VMEM((Q, 128), F32), pltpu.VMEM((Q, 128), F32),
                        pltpu.VMEM((Q, Q), F32)])
    return pl.pallas_call(body, grid_spec=grid_spec, name="hg_fwd", out_shape=outs,
                          compiler_params=_cparams())(fast, hq, hf, hi, hg, hlb, nw)


def _hg_bwd(dob, o, hq, hf, hi, hg, states, hlb, nw, fast, phases=()):
    L = dob.shape[0]
    nc = L // Q

    def chunk(exact, ck, step, dob_ref, o_ref, hq_ref, hf_ref, hi_ref, hg_ref, st_ref, hlb_ref, nw_ref,
              dhq_ref, dhf_ref, dhi_ref, dhg_ref, acc_ref, gs_ref, *tmp):
        tok = slice(Q * ck, Q * ck + Q)
        hqv = hq_ref[tok, :].astype(F32)
        qf, kf, fg, sg, lb, sm1, bcum = _hg_gates(hqv, hf_ref[tok, :], hlb_ref[...])
        gate = hg_ref[tok, :].astype(F32)
        sgg = _sigmoid(gate)
        nwv = nw_ref[...]
        tri_t = (_iota((Q, Q), 1) >= _iota((Q, Q), 0)).astype(BF)
        ones8 = jnp.ones((8, 128), BF)
        heads = [slice(128 * h, 128 * h + 128) for h in range(NH_HG)]
        row_last = _iota((Q, 1), 0) == Q - 1
        dobs, dnws = [], []
        for sl in heads:
            gt, sgt = gate[:, sl], sgg[:, sl]
            _, n, r = _rms(o_ref[tok, sl].astype(F32), nwv)
            dobv = dob_ref[tok, sl].astype(F32)
            dhg_ref[tok, sl] = (dobv * n * nwv * sgt * (1.0 + gt * (1.0 - sgt))).astype(BF)
            do, dw = _rms_bwd(dobv * gt * sgt, n, r, nwv)
            dnws.append(_colsum(dw))
            dobs.append(do.astype(BF))
        causal = _iota((Q, Q), 1) <= _iota((Q, Q), 0)
        if exact:
            intra = [(_hg_att_exact(bcum[:, sl], qf[:, sl], kf[:, sl], *tmp[:3]),) for sl in heads]
        else:
            intra = [_hg_intra(bcum[:, sl], qf[:, sl], kf[:, sl]) for sl in heads]
        states = [(st_ref[ck, sl, :], gs_ref[sl, :]) for sl in heads]
        das = [jnp.where(causal, _dot_nt(dob_h, hi_ref[tok, sl]), 0.0) for dob_h, sl in zip(dobs, heads)]
        dqhats = [_dot_nt(dob_h, s) for dob_h, (s, _) in zip(dobs, states)]
        dkhats = [_dot_nt(hi_ref[tok, sl], gst) for sl, (_, gst) in zip(heads, states)]
        if not exact:
            dqts = [jnp.dot(da.astype(BF), it[2], preferred_element_type=F32) for da, it in zip(das, intra)]
            dkts = [lax.dot_general(da.astype(BF), it[1], (((0,), (0,)), ((), ())), preferred_element_type=F32)
                    for da, it in zip(das, intra)]
        dqs, dks, dgls = [], [], []
        for h, sl in enumerate(heads):
            b, q, k = bcum[:, sl], qf[:, sl], kf[:, sl]
            att = intra[h][0]
            s, gst = states[h]
            dob_h, dqhat, dkhat = dobs[h], dqhats[h], dkhats[h]
            eb = jnp.exp(b)
            blast = b[Q - 1:Q, :]
            ekl = jnp.exp(blast - b)
            qhat, khat = q * eb, k * ekl
            dhi_ref[tok, sl] = (_dot_tn(att, dob_h) + _dot(khat, gst)).astype(BF)
            if exact:
                dq_i, dk_i, db = _hg_att_exact_bwd(das[h], b, q, k, *tmp)
            else:
                dq_i, dk_i, db = _hg_intra_bwd(dqts[h], dkts[h], *intra[h][1:4])
            dqs.append(dq_i + dqhat * eb)
            dks.append(dk_i + dkhat * ekl)
            qhat_r, khat_r = qhat.astype(BF).astype(F32), khat.astype(BF).astype(F32)
            decay_row = sum(_dot_nt(ones8, part) for part in _split(gst * s, 2))[0:1, :]
            dblast = _colsum(dkhat * khat_r) + decay_row * jnp.exp(blast)
            dgls.append(db + qhat_r * dqhat - khat_r * dkhat + jnp.where(row_last, dblast, 0.0))
            gs_ref[sl, :] = _dot_tn(qhat, dob_h) + gst * jnp.exp(b.T[:, Q - 1:Q])
        dq, dk, db = (jnp.concatenate(t, axis=1) for t in (dqs, dks, dgls))
        dgl = _dot_sel(tri_t, db, 2)
        sgq = _sigmoid(hqv)
        dhq_ref[tok, :] = (dq * sgq * (1.0 + hqv * (1.0 - sgq))).astype(BF)
        dfg = dgl / fg - dk
        dhf_ref[tok, :] = (dfg * (1.0 - lb) * sg * (1.0 - sg)).astype(BF)
        acc_ref[0:1, :] += _colsum(dfg * (1.0 - sg))
        acc_ref[1:2, :] += jnp.concatenate(dnws, axis=1)

        @pl.when(jnp.logical_and(step == nc // CH - 1, ck == 0))
        def _():
            dlb = acc_ref[0:1, :] * lb * sm1
            acc_ref[2:3, :] = dlb
            acc_ref[3:4, :] = -dlb
            tot = acc_ref[1:2, 0:128]
            for h in range(1, NH_HG):
                tot = tot + acc_ref[1:2, 128 * h:128 * h + 128]
            acc_ref[4:5, 0:128] = tot

    def run(exact, step, *refs):
        for ck in reversed(range(CH)):
            chunk(exact, ck, step, *refs)

    def body(fast_ref, *refs):
        step = pl.program_id(0)
        _zero_first(refs[13], refs[14])
        pl.when(fast_ref[0] == 1)(lambda: run(False, step, *refs))
        pl.when(fast_ref[0] != 1)(lambda: run(True, step, *refs))

    rev = pl.BlockSpec((CH * Q, D), lambda i, f: (nc // CH - 1 - i, 0))
    outs = [SDS((L, D), BF)] * 4 + [SDS((8, D), F32)]
    return _call(
        body, (fast, dob, o, hq, hf, hi, hg, states, hlb, nw), name="hg_bwd", grid=(nc // CH,), prefetch=1,
        in_specs=[rev] * 6 + [pl.BlockSpec((CH, D, 128), lambda i, f: (nc // CH - 1 - i, 0, 0)),
                              pl.BlockSpec((2, D), lambda i, f: (0, 0)), pl.BlockSpec((1, 128), lambda i, f: (0, 0))],
        out_specs=[rev] * 4 + [pl.BlockSpec((8, D), lambda i, f: (0, 0))], out_shape=outs,
        scratch_shapes=[pltpu.VMEM((D, 128), F32), pltpu.VMEM((Q, 128), F32), pltpu.VMEM((Q, 128), F32),
                        pltpu.VMEM((Q, Q), F32), pltpu.VMEM((Q, 128), F32), pltpu.VMEM((Q, 128), F32)], phases=phases)


def _place():
    return lax.axis_index("x"), lax.axis_index("y"), lax.axis_index("c")


def _phase_io(phase):
    kind, arrays, halves = phase
    n = len(arrays)
    dma = pltpu.SemaphoreType.DMA
    if kind == "gather":
        outs = [SDS((8,) + a.shape if hc is None else (4,) + a.shape, a.dtype) for a, hc in zip(arrays, halves)]
        return outs, [dma((7 * n,)), dma((7 * n,)), dma((n,))], {}
    if kind == "sibling":
        return [SDS((4, g.shape[1], hc), g.dtype) for g, hc in zip(arrays, halves)], [dma((n,)), dma((n,))], {}
    if kind == "chips":
        return [SDS((3,) + p.shape[1:], p.dtype) for p in arrays], [dma((3 * n,)), dma((3 * n,))], {}
    assert kind == "swap"
    return [SDS(b.shape, b.dtype) for b in arrays], [dma((n,)), dma((n,))], {a: a for a in range(n)}


def _gather_events(ins, outs, sems, halves):
    send_sems, recv_sems, local_sems = sems
    n = len(ins)

    def parts(a):
        x, y, c = _place()
        hc = halves[a]
        me, sibling = (x, y, c), (x, y, 1 - c)
        chips = [(1 - x, y), (x, 1 - y), (1 - x, 1 - y)]

        def slot(p):
            if hc is None:
                return outs[a].at[4 * p[0] + 2 * p[1] + p[2]]
            return outs[a].at[2 * p[0] + p[1], :, pl.ds(p[2] * hc, hc)]

        own = ins[a] if hc is None else ins[a].at[:, pl.ds(c * hc, hc)]

        def copy(k, piece, to, src=None):
            return pltpu.make_async_remote_copy(
                src_ref=slot(piece) if src is None else src, dst_ref=slot(piece),
                send_sem=send_sems.at[7 * a + k], recv_sem=recv_sems.at[7 * a + k], device_id=to, device_id_type=MESH)

        return dict(
            mine=lambda: pltpu.make_async_copy(own, slot(me), local_sems.at[a]),
            starts=lambda: [copy(0, me, sibling, src=own)] + [copy(1 + j, me, (*chip, c), src=own)
                                                               for j, chip in enumerate(chips)],
            arrive=lambda: [copy(1 + j, (*chip, c), me) for j, chip in enumerate(chips)],
            passed=lambda: [copy(4 + j, (*chip, c), sibling) for j, chip in enumerate(chips)],
            from_sibling=lambda: [copy(0, sibling, me)] + [copy(4 + j, (*chip, 1 - c), me)
                                                            for j, chip in enumerate(chips)])

    def first():
        for a in range(n):
            p = parts(a)
            p["mine"]().start()
            for cp in p["starts"]():
                cp.start()

    def mid():
        for a in range(n):
            p = parts(a)
            for cp_in, cp_out in zip(p["arrive"](), p["passed"]()):
                cp_in.wait_recv()
                cp_out.start()

    def last():
        for a in range(n):
            p = parts(a)
            for cp in p["from_sibling"]():
                cp.wait_recv()
            for cp in p["starts"]() + p["passed"]():
                cp.wait_send()
            p["mine"]().wait()

    return dict(first=first, mid=mid, last=last)


def _exchange_events(kind, ins, outs, sems, halves):
    send_sems, recv_sems = sems
    n = len(outs)

    def copies():
        x, y, c = _place()
        if kind == "sibling":
            return [pltpu.make_async_remote_copy(
                src_ref=ins[a].at[:, :, pl.ds((1 - c) * halves[a], halves[a])], dst_ref=outs[a],
                send_sem=send_sems.at[a], recv_sem=recv_sems.at[a], device_id=(x, y, 1 - c), device_id_type=MESH)
                for a in range(n)]
        chips = [(1 - x, y), (x, 1 - y), (1 - x, 1 - y)]
        return [pltpu.make_async_remote_copy(
            src_ref=ins[a].at[2 * px + py], dst_ref=outs[a].at[k], send_sem=send_sems.at[3 * a + k],
            recv_sem=recv_sems.at[3 * a + k], device_id=(px, py, c), device_id_type=MESH)
            for a in range(n) for k, (px, py) in enumerate(chips)]

    def first():
        for cp in copies():
            cp.start()

    def last():
        for cp in copies():
            cp.wait()

    return dict(first=first, last=last)


def _swap_events(outs, sems, halves):
    send_sems, recv_sems = sems
    n = len(outs)

    def copy(a, landing):
        x, y, c = _place()
        cols = lambda which: outs[a].at[:, pl.ds(which * halves[a], halves[a])]
        return pltpu.make_async_remote_copy(
            src_ref=cols(c), dst_ref=cols(1 - c) if landing else cols(c), send_sem=send_sems.at[a],
            recv_sem=recv_sems.at[a], device_id=(x, y, 1 - c), device_id_type=MESH)

    def first():
        for a in range(n):
            copy(a, False).start()

    def last():
        for a in range(n):
            copy(a, True).wait_recv()
        for a in range(n):
            copy(a, False).wait_send()

    return dict(first=first, last=last)


def _phase_events(phase, ins, outs, sems):
    kind, _, halves = phase
    if kind == "gather":
        return _gather_events(ins, outs, sems, halves)
    if kind == "swap":
        return _swap_events(outs, sems, halves)
    return _exchange_events(kind, ins, outs, sems, halves)


def _split_refs(refs, counts):
    out, at = [], 0
    for c in counts:
        out.append(list(refs[at:at + c]))
        at += c
    return out


def _comm_plumbing(phases, first_in, first_out):
    ios = [_phase_io(p) for p in phases]
    arrays = [a for p in phases for a in p[1]]
    out_shape = [o for io in ios for o in io[0]]
    sem_shapes = [s for io in ios for s in io[1]]
    aliases, ai, ao = {}, first_in, first_out
    for p, io in zip(phases, ios):
        aliases.update({ai + k: ao + v for k, v in io[2].items()})
        ai, ao = ai + len(p[1]), ao + len(io[0])

    def events(cins, couts, sems):
        evs = [_phase_events(p, i, o, s) for p, i, o, s in zip(
            phases, _split_refs(cins, [len(p[1]) for p in phases]), _split_refs(couts, [len(io[0]) for io in ios]),
            _split_refs(sems, [len(io[1]) for io in ios]))]

        def run(key):
            for ev in evs:
                if key in ev:
                    ev[key]()

        return {key: (lambda key=key: run(key)) for key in ("first", "mid", "last")}

    def regroup(flat):
        return _split_refs(flat, [len(io[0]) for io in ios])

    return arrays, out_shape, sem_shapes, aliases, events, regroup


def _run_phases(phases, name):
    arrays, out_shape, sem_shapes, aliases, events, regroup = _comm_plumbing(phases, 0, 0)

    def body(*refs):
        cins, couts, sems = _split_refs(refs, [len(arrays), len(out_shape), len(sem_shapes)])
        ev = events(cins, couts, sems)
        for key in ("first", "mid", "last"):
            ev[key]()

    outs = pl.pallas_call(
        body, name=name, in_specs=[ANY] * len(arrays), out_specs=[ANY] * len(out_shape), out_shape=out_shape,
        scratch_shapes=sem_shapes, input_output_aliases=aliases)(*arrays)
    return regroup(outs)


def _call(body, args, *, name, grid, in_specs, out_specs, out_shape, scratch_shapes=(), prefetch=0, phases=(),
          mid_step=None):
    steps = grid[0]
    arrays, c_shape, sem_shapes, aliases, events, regroup = _comm_plumbing(
        phases, prefetch + len(in_specs), len(out_specs))
    counts = [prefetch, len(in_specs), len(arrays), len(out_specs), len(c_shape), len(scratch_shapes), len(sem_shapes)]

    def wrapped(*refs):
        pre, ins, cins, outs, couts, scratch, sems = _split_refs(refs, counts)
        if not phases:
            return body(*pre, *ins, *outs, *scratch)
        step = pl.program_id(0)
        ev = events(cins, couts, sems)
        pl.when(step == 0)(ev["first"])
        body(*pre, *ins, *outs, *scratch)
        pl.when(step == (steps // 2 if mid_step is None else mid_step))(ev["mid"])
        pl.when(step == steps - 1)(ev["last"])

    grid_spec = pltpu.PrefetchScalarGridSpec(
        num_scalar_prefetch=prefetch, grid=grid, in_specs=list(in_specs) + [ANY] * len(arrays),
        out_specs=list(out_specs) + [ANY] * len(c_shape), scratch_shapes=list(scratch_shapes) + sem_shapes)
    outs = pl.pallas_call(
        wrapped, grid_spec=grid_spec, name=name, out_shape=list(out_shape) + c_shape, input_output_aliases=aliases,
        compiler_params=_cparams())(*args, *arrays)
    return list(outs[:len(out_specs)]), regroup(outs[len(out_specs):])


def _tile(rows, cols, nbuf):
    budget = (VMEM_LIMIT // 3) // (2 * nbuf * 4)
    if rows % 8 == 0:
        cands = [t for t in range(8, rows + 1, 8) if rows % t == 0 and t * cols <= budget]
        pref = [t for t in cands if t % 16 == 0]
        return (max(pref) if pref else max(cands) if cands else 8), cols
    cands = [t for t in range(128, cols + 1, 128) if cols % t == 0 and rows * t <= budget]
    return rows, (max(cands) if cands else 128)


def _chip_sum(g, from_sib, place, name):
    _, rows, hc = from_sib.shape
    tr, tc = _tile(rows, hc, 4)
    ni, nj = rows // tr, hc // tc

    def body(p_ref, g_ref, s_ref, hb_ref, own_ref):
        s = g_ref[...] + s_ref[...]
        hb_ref[...] = s.astype(BF)

        @pl.when(pl.program_id(2) == p_ref[1])
        def _():
            own_ref[...] = s

    grid_spec = pltpu.PrefetchScalarGridSpec(
        num_scalar_prefetch=1, grid=(ni, nj, 4),
        in_specs=[pl.BlockSpec((None, tr, tc), lambda i, j, k, p: (k, i, p[0] * nj + j)),
                  pl.BlockSpec((None, tr, tc), lambda i, j, k, p: (k, i, j))],
        out_specs=[pl.BlockSpec((None, tr, tc), lambda i, j, k, p: (k, i, j)),
                   pl.BlockSpec((tr, tc), lambda i, j, k, p: (i, j))])
    return pl.pallas_call(
        body, grid_spec=grid_spec, name=name, out_shape=[SDS((4, rows, hc), BF), SDS((rows, hc), F32)],
        compiler_params=pltpu.CompilerParams(dimension_semantics=("arbitrary",) * 3,
                                             vmem_limit_bytes=VMEM_LIMIT))(place, g, from_sib)


def _total(own, parts, place, name):
    rows, hc = own.shape
    tr, tc = _tile(rows, hc, 5)
    ni, nj = rows // tr, hc // tc

    def body(p_ref, own_ref, parts_ref, o_ref):
        s = own_ref[...]
        for k in range(3):
            s = s + parts_ref[k].astype(F32)
        o_ref[...] = s

    grid_spec = pltpu.PrefetchScalarGridSpec(
        num_scalar_prefetch=1, grid=(ni, nj),
        in_specs=[pl.BlockSpec((tr, tc), lambda i, j, p: (i, j)),
                  pl.BlockSpec((3, tr, tc), lambda i, j, p: (0, i, j))],
        out_specs=pl.BlockSpec((tr, tc), lambda i, j, p: (i, p[0] * nj + j)))
    return pl.pallas_call(
        body, grid_spec=grid_spec, name=name, out_shape=SDS((rows, 2 * hc), F32),
        compiler_params=pltpu.CompilerParams(dimension_semantics=("arbitrary",) * 2,
                                             vmem_limit_bytes=VMEM_LIMIT))(place, own, parts)


def _sum8(parts, name):
    R = parts.shape[1]

    def body(p_ref, o_ref):
        s = p_ref[0]
        for k in range(1, 8):
            s = s + p_ref[k]
        o_ref[...] = s

    return pl.pallas_call(
        body, grid=(1,), name=name, in_specs=[_const((8, R, 128))], out_specs=_const((R, 128)),
        out_shape=SDS((R, 128), F32), compiler_params=_cparams())(parts)


def _adamw(w, g, m, v, name):
    _, R, C = w.shape
    tr, tc = _tile(R, C, 8)
    c1 = 1.0 / (1.0 - ADAM_B1 ** ADAM_STEP)
    c2 = 1.0 / (1.0 - ADAM_B2 ** ADAM_STEP)

    def body(w_ref, g_ref, m_ref, v_ref, go_ref, d_ref, nm_ref, nv_ref):
        gv = g_ref[...]
        go_ref[...] = gv
        nm = ADAM_B1 * m_ref[...] + (1.0 - ADAM_B1) * gv
        nv = ADAM_B2 * v_ref[...] + (1.0 - ADAM_B2) * gv * gv
        nm_ref[...] = nm
        nv_ref[...] = nv
        d_ref[...] = -ADAM_LR * ((nm * c1) / (jnp.sqrt(nv * c2) + ADAM_EPS) + ADAM_WD * w_ref[...])

    blk3 = pl.BlockSpec((None, tr, tc), lambda i, j: (0, i, j))
    return pl.pallas_call(
        body, grid=(R // tr, C // tc), name=name,
        in_specs=[blk3, pl.BlockSpec((tr, tc), lambda i, j: (i, j)), blk3, blk3], out_specs=[blk3] * 4,
        out_shape=[SDS((1, R, C), F32)] * 4,
        compiler_params=pltpu.CompilerParams(dimension_semantics=("arbitrary",) * 2, allow_input_fusion=[True] * 4,
                                             vmem_limit_bytes=VMEM_LIMIT))(w, g, m, v)


def _pack_small(parts):
    rows = []
    for p in parts:
        p = p.reshape(-1)
        rows.append(jnp.pad(p, (0, (-p.shape[0]) % 128)).reshape(-1, 128))
    out = jnp.concatenate(rows, axis=0)
    return jnp.pad(out, ((0, (-out.shape[0]) % 8), (0, 0)))


def _unpack_small(packed, shapes):
    out, row = [], 0
    for shp in shapes:
        n = 1
        for s in shp:
            n *= s
        nr = -(-n // 128)
        out.append(packed[row:row + nr].reshape(-1)[:n].reshape(shp))
        row += nr
    return out


def _pad_lanes(v, n=128):
    return jnp.pad(v, ((0, 0), (0, n - v.shape[1])))


GROUP_FFN = ("ffn_w_gate", "ffn_w_up", "ffn_w_down")
GROUP_ATTN = ("w_out", "xa_wq", "xa_wkv", "xa_wo")


def kernel(x, mem, norm_mix_w, w_in, conv_w, conv_b, dt_bias, a_log, d_skip, ssd_norm_w, hg_lower_bounds, hg_norm_w, w_out, norm_xa_w, norm_mem_w, xa_wq, xa_wkv, xa_wo, norm_ffn_w, ffn_w_gate, ffn_w_up, ffn_w_down, norm_final_w, loss_target, m_norm_mix_w, m_w_in, m_conv_w, m_conv_b, m_dt_bias, m_a_log, m_d_skip, m_ssd_norm_w, m_hg_lower_bounds, m_hg_norm_w, m_w_out, m_norm_xa_w, m_norm_mem_w, m_xa_wq, m_xa_wkv, m_xa_wo, m_norm_ffn_w, m_ffn_w_gate, m_ffn_w_up, m_ffn_w_down, m_norm_final_w, v_norm_mix_w, v_w_in, v_conv_w, v_conv_b, v_dt_bias, v_a_log, v_d_skip, v_ssd_norm_w, v_hg_lower_bounds, v_hg_norm_w, v_w_out, v_norm_xa_w, v_norm_mem_w, v_xa_wq, v_xa_wkv, v_xa_wo, v_norm_ffn_w, v_ffn_w_gate, v_ffn_w_up, v_ffn_w_down, v_norm_final_w):
    w = dict(norm_mix_w=norm_mix_w, w_in=w_in, conv_w=conv_w, conv_b=conv_b, dt_bias=dt_bias, a_log=a_log, d_skip=d_skip,
             ssd_norm_w=ssd_norm_w, hg_lower_bounds=hg_lower_bounds, hg_norm_w=hg_norm_w, w_out=w_out,
             norm_xa_w=norm_xa_w, norm_mem_w=norm_mem_w, xa_wq=xa_wq, xa_wkv=xa_wkv, xa_wo=xa_wo, norm_ffn_w=norm_ffn_w,
             ffn_w_gate=ffn_w_gate, ffn_w_up=ffn_w_up, ffn_w_down=ffn_w_down, norm_final_w=norm_final_w)
    m = dict(norm_mix_w=m_norm_mix_w, w_in=m_w_in, conv_w=m_conv_w, conv_b=m_conv_b, dt_bias=m_dt_bias, a_log=m_a_log,
             d_skip=m_d_skip, ssd_norm_w=m_ssd_norm_w, hg_lower_bounds=m_hg_lower_bounds, hg_norm_w=m_hg_norm_w,
             w_out=m_w_out, norm_xa_w=m_norm_xa_w, norm_mem_w=m_norm_mem_w, xa_wq=m_xa_wq, xa_wkv=m_xa_wkv,
             xa_wo=m_xa_wo, norm_ffn_w=m_norm_ffn_w, ffn_w_gate=m_ffn_w_gate, ffn_w_up=m_ffn_w_up,
             ffn_w_down=m_ffn_w_down, norm_final_w=m_norm_final_w)
    v = dict(norm_mix_w=v_norm_mix_w, w_in=v_w_in, conv_w=v_conv_w, conv_b=v_conv_b, dt_bias=v_dt_bias, a_log=v_a_log,
             d_skip=v_d_skip, ssd_norm_w=v_ssd_norm_w, hg_lower_bounds=v_hg_lower_bounds, hg_norm_w=v_hg_norm_w,
             w_out=v_w_out, norm_xa_w=v_norm_xa_w, norm_mem_w=v_norm_mem_w, xa_wq=v_xa_wq, xa_wkv=v_xa_wkv,
             xa_wo=v_xa_wo, norm_ffn_w=v_norm_ffn_w, ffn_w_gate=v_ffn_w_gate, ffn_w_up=v_ffn_w_up,
             ffn_w_down=v_ffn_w_down, norm_final_w=v_norm_final_w)
    xi, yi, ci = _place()
    chip = 2 * xi + yi
    place = jnp.stack([ci, chip]).astype(jnp.int32)

    def shard(t, name):
        return jnp.swapaxes(t[name], 1, 2) if name in TRANSPOSED else t[name]

    wsh = {name: shard(w, name) for name in BIG}
    half = {name: wsh[name].shape[2] // 2 for name in BIG}
    payload = {name: wsh[name][0].astype(BF) for name in BIG}
    ws = {name: w[name] for name in SMALL}
    xs, mems, tgt = x[0], mem[0], loss_target[0]

    def chip_sums(names, grads, from_sib):
        return [_chip_sum(grads[n], s, place, "grads_chip_sum_" + n) for n, s in zip(names, from_sib)]

    def totals(names, sums, others):
        return [_total(own, o, place, "grads_total_" + n) for n, (_, own), o in zip(names, sums, others)]

    ((w_in4, conv_all),) = _run_phases([("gather", [payload["w_in"], conv_w[0]], [half["w_in"], None])], "gather_w_in")
    w_in_t = w_in4.reshape(N_IN, D)
    ws["conv_w"] = conv_all[0::2].transpose(1, 0, 2).reshape(1, 4, 1536)
    rest = [n for n in BIG if n != "w_in"]
    (h0, z, xbc, hq, hf, hi, hg, dtr), (gathered,) = _in_proj(
        xs, ws["norm_mix_w"], w_in_t, phases=[("gather", [payload[n] for n in rest], [half[n] for n in rest])])
    wg = dict(zip(rest, gathered))
    wg_t, wu_t = wg["ffn_w_gate"].reshape(FFN, D), wg["ffn_w_up"].reshape(FFN, D)
    wd = wg["ffn_w_down"].reshape(FFN, D)
    w_out_f = wg["w_out"].reshape(2 * D, D)
    wq, wo = wg["xa_wq"].reshape(D, D), wg["xa_wo"].reshape(D, D)
    dtb, alog = _pad_lanes(ws["dt_bias"]), _pad_lanes(ws["a_log"])
    dskip_full = jnp.repeat(ws["d_skip"], SSD_P, axis=1)
    cw, conv_bias = ws["conv_w"][0], ws["conv_b"]
    hlb = ws["hg_lower_bounds"]
    hg_fast = (jnp.min(jax.nn.softmax(hlb, axis=0)[0]) >= HG_LB_FLOOR).astype(jnp.int32).reshape(1)

    ya, yssd, u, st_ssd = _ssd_fwd(xbc, dtr, z, cw, conv_bias, dtb, alog, dskip_full, ws["ssd_norm_w"])
    ob, ohg, st_hg = _hg_fwd(hq, hf, hi, hg, hlb, ws["hg_norm_w"], hg_fast)
    kmem, vmem = _mem_kv(mems, ws["norm_mem_w"], wg["xa_wkv"])
    x1, x2, hxa, q, ox = _attn_fwd(xs, ya, ob, w_out_f, ws["norm_xa_w"], wq, kmem, vmem, wo)
    nfin = ws["norm_final_w"].reshape(1, D)
    dx2, hffn, act, dx3, dg, du, acc_f = _ffn_loss(x2, tgt, ws["norm_ffn_w"], nfin, wg_t, wu_t, wd)

    gb = {"ffn_w_gate": _matmul_tn(dg, hffn, "gw_gate").reshape(4, FFN // 4, D),
          "ffn_w_up": _matmul_tn(du, hffn, "gw_up").reshape(4, FFN // 4, D),
          "ffn_w_down": _matmul_tn(act, dx3, "gw_down").reshape(4, FFN // 4, D)}
    (dx1, dya, dob, dq, dk, dv, acc_a), (sib_ffn,) = _attn_bwd(
        dx2, x1, q, kmem, vmem, ws["norm_xa_w"], wq, wo, w_out_f,
        phases=[("sibling", [gb[n] for n in GROUP_FFN], [half[n] for n in GROUP_FFN])])
    sums_ffn = chip_sums(GROUP_FFN, gb, sib_ffn)
    g_nmem, gb["xa_wkv"] = _mem_kv_bwd(mems, ws["norm_mem_w"], wg["xa_wkv"], dk, dv)
    gb["w_out"] = _matmul_tn_pair(ya, ob, dx1, "gw_out").reshape(4, D // 2, D)
    gb["xa_wq"] = _matmul_tn(hxa, dq, "gw_q").reshape(4, D // 4, D)
    gb["xa_wo"] = _matmul_tn(ox, dx2, "gw_o").reshape(4, D // 4, D)
    (dhq, dhf, dhi, dhg, acc_h), (others_ffn, sib_attn) = _hg_bwd(
        dob, ohg, hq, hf, hi, hg, st_hg, hlb, ws["hg_norm_w"], hg_fast,
        phases=[("chips", [hb for hb, _ in sums_ffn], None),
                ("sibling", [gb[n] for n in GROUP_ATTN], [half[n] for n in GROUP_ATTN])])
    red_ffn = totals(GROUP_FFN, sums_ffn, others_ffn)
    sums_attn = chip_sums(GROUP_ATTN, gb, sib_attn)
    dz, dxbc, ddt, gconv, ghead, glane = _ssd_bwd(dya, yssd, z, u, xbc, dtr, st_ssd, cw, dtb, alog, dskip_full,
                                                  ws["ssd_norm_w"])
    (gw_in_t,), (g_ffn, others_attn) = _gw_in(
        h0, dz, dxbc, ddt, dhq, dhf, dhi, dhg,
        phases=[("swap", red_ffn, [half[n] for n in GROUP_FFN]), ("chips", [hb for hb, _ in sums_attn], None)])
    red_attn = totals(GROUP_ATTN, sums_attn, others_attn)
    gb["w_in"] = gw_in_t.reshape(4, N_IN // 4, D)
    dproj = (xs, dx1, dz, dxbc, dhq, dhf, dhi, dhg, ddt, ws["norm_mix_w"], w_in_t)
    (gx_a, acc_ia), (g_attn, (sib_in,)) = _in_proj_bwd(
        *dproj, 0, phases=[("swap", red_attn, [half[n] for n in GROUP_ATTN]), ("sibling", [gb["w_in"]], [half["w_in"]])])
    sums_in = chip_sums(("w_in",), gb, [sib_in])
    (gx_b, acc_ib), ((others_in,),) = _in_proj_bwd(*dproj, 1, phases=[("chips", [sums_in[0][0]], None)])
    (gx, acc_ic), _ = _in_proj_bwd(*dproj, 2, done=(gx_a, gx_b))
    gx, acc_i = gx.reshape(xs.shape), acc_ia + acc_ib + acc_ic
    red_in = totals(("w_in",), sums_in, [others_in])

    gs = {
        "norm_mix_w": acc_i[0:1], "conv_w": gconv[0:4][None], "conv_b": gconv[4:5],
        "dt_bias": ghead[0:1, :NH_SSD], "a_log": ghead[2:3, :NH_SSD], "d_skip": ghead[3:4, :NH_SSD],
        "ssd_norm_w": glane[1:2], "hg_lower_bounds": acc_h[2:4], "hg_norm_w": acc_h[4:5, :128],
        "norm_xa_w": acc_a[0:1], "norm_mem_w": g_nmem, "norm_ffn_w": acc_f[2:3], "norm_final_w": acc_f[1],
    }
    loss = (0.5 / D) * jnp.sum(acc_f[0])
    small_parts = [gs[name] for name in SMALL] + [loss.reshape(1)]
    small_shapes = [gs[name].shape for name in SMALL] + [(1,)]
    (g_in,), (packed,) = _run_phases([("swap", red_in, [half["w_in"]]),
                                      ("gather", [_pack_small(small_parts)], [None])], "grads_finish")
    g_big = dict(zip(GROUP_FFN + GROUP_ATTN + ("w_in",), g_ffn + g_attn + [g_in]))
    small = _unpack_small(_sum8(packed, "small_total"), small_shapes)
    g_small = dict(zip(SMALL, small[:-1]))
    loss_all = small[-1][0]
    g_small["conv_w"] = lax.dynamic_slice_in_dim(g_small["conv_w"], chip * 384, 384, 2)

    grads, delta, new_m, new_v = {}, {}, {}, {}
    for name in BIG:
        outs = tuple(_adamw(wsh[name], g_big[name], shard(m, name), shard(v, name), "adamw_" + name))
        if name in TRANSPOSED:
            outs = tuple(jnp.swapaxes(o, 1, 2) for o in outs)
        grads[name], delta[name], new_m[name], new_v[name] = outs
    shapes = [w[name].shape for name in SMALL]
    packs = [_pack_small([t[name] for name in SMALL]) for t in (w, g_small, m, v)]
    outs = _adamw(packs[0][None], packs[1], packs[2][None], packs[3][None], "adamw_small")[1:]
    for name, g_, d_, nm_, nv_ in zip(SMALL, [g_small[n] for n in SMALL], *[_unpack_small(o[0], shapes) for o in outs]):
        grads[name] = g_.reshape(w[name].shape)
        delta[name], new_m[name], new_v[name] = d_, nm_, nv_

    return (loss_all, gx[None], *[grads[n] for n in WEIGHTS], *[delta[n] for n in WEIGHTS],
            *[new_m[n] for n in WEIGHTS], *[new_v[n] for n in WEIGHTS])
```
